```python
import jax, jax.numpy as jnp
from jax import lax
import numpy as np

D_MODEL = 1024
BATCH = 16
SEQ = 2048
DEPTH = 2

CHUNK = 64
Q_BLOCK = 128
N_MIXERS = 2
N_RET_LAYERS = (DEPTH + 1) // 2
N_MLA_LAYERS = DEPTH // 2
RMS_EPS = 1e-6
ROPE_THETA = 10000.0

RET_HEADS = D_MODEL // 256
RET_QK_DIM = 256
RET_V_DIM = 2 * D_MODEL // RET_HEADS
RET_GAMMA_BASE = -5.0

MLA_HEADS = D_MODEL // 128
MLA_Q_RANK = 384
MLA_KV_RANK = D_MODEL // 4
MLA_NOPE_DIM = 128
MLA_ROPE_DIM = 64
MLA_V_DIM = 128
MLA_QK_DIM = MLA_NOPE_DIM + MLA_ROPE_DIM
MASK_VALUE = -1e30

FFN_DIM = 2816
CONV_WIDTH = 3

kernel_name = "hybrid_retention_mla_convffn_trunk"


def rms_norm(x, gain):
    xf = x.astype(jnp.float32)
    y = xf * lax.rsqrt(jnp.mean(xf * xf, axis=-1, keepdims=True) + RMS_EPS)
    return (y * gain.astype(jnp.float32)).astype(x.dtype)


def rope(x, pos):
    half = x.shape[-1] // 2
    inv_freq = ROPE_THETA ** (-jnp.arange(half, dtype=jnp.float32) / half)
    ang = pos.astype(jnp.float32)[:, None] * inv_freq[None, :]
    cos = jnp.cos(ang)[None, :, None, :]
    sin = jnp.sin(ang)[None, :, None, :]
    xf = x.astype(jnp.float32)
    x1, x2 = xf[..., :half], xf[..., half:]
    return jnp.concatenate([x1 * cos - x2 * sin, x2 * cos + x1 * sin], axis=-1).astype(x.dtype)


def retention_mixer(h, w_in, gn_gain, w_out):
    B, S, _ = h.shape
    H, dk, dv = RET_HEADS, RET_QK_DIM, RET_V_DIM
    proj = h @ w_in
    q, k, v, g = jnp.split(proj, [H * dk, 2 * H * dk, 2 * H * dk + H * dv], axis=-1)
    pos = jnp.arange(S)
    q = rope(q.reshape(B, S, H, dk), pos)
    k = rope(k.reshape(B, S, H, dk), pos) * (dk ** -0.5)
    v = v.reshape(B, S, H, dv)
    n_chunks = S // CHUNK

    def to_chunks(t):
        return t.reshape(B, n_chunks, CHUNK, H, t.shape[-1]).transpose(1, 0, 3, 2, 4)

    log_gamma = jnp.log1p(-jnp.exp2(RET_GAMMA_BASE - jnp.arange(H, dtype=jnp.float32)))
    idx = jnp.arange(CHUNK, dtype=jnp.float32)
    intra_decay = jnp.exp(log_gamma[:, None, None] * jnp.abs(idx[:, None] - idx[None, :]))
    q_decay = jnp.exp(log_gamma[:, None] * (idx + 1.0))[:, :, None]
    k_decay = jnp.exp(log_gamma[:, None] * (CHUNK - 1.0 - idx))[:, :, None]
    chunk_decay = jnp.exp(log_gamma * CHUNK)[:, None, None]

    def step(state, qkv):
        qc, kc, vc = qkv
        scores = jnp.einsum('bhid,bhjd->bhij', qc, kc) * intra_decay
        inner = jnp.einsum('bhij,bhjv->bhiv', scores, vc)
        cross = jnp.einsum('bhid,bhdv->bhiv', qc * q_decay, state)
        state = state * chunk_decay + jnp.einsum('bhjd,bhjv->bhdv', kc * k_decay, vc)
        return state, inner + cross

    state0 = jnp.zeros((B, H, dk, dv), jnp.float32)
    _, out = lax.scan(step, state0, (to_chunks(q), to_chunks(k), to_chunks(v)))
    out = out.transpose(1, 0, 3, 2, 4).reshape(B, S, H, dv)
    out = rms_norm(out, gn_gain).astype(h.dtype)
    out = out.reshape(B, S, H * dv) * jax.nn.silu(g)
    return out @ w_out


def mla_mixer(h, w_in, q_norm_g, w_qb, kv_norm_g, w_kvb, q_head_g, k_head_g, w_out):
    B, S, _ = h.shape
    H = MLA_HEADS
    proj = h @ w_in
    c_q, c_kv, k_rope = jnp.split(proj, [MLA_Q_RANK, MLA_Q_RANK + MLA_KV_RANK], axis=-1)
    q = (rms_norm(c_q, q_norm_g) @ w_qb).reshape(B, S, H, MLA_QK_DIM)
    kv = (rms_norm(c_kv, kv_norm_g) @ w_kvb).reshape(B, S, H, MLA_NOPE_DIM + MLA_V_DIM)
    k_nope, v = kv[..., :MLA_NOPE_DIM], kv[..., MLA_NOPE_DIM:]
    k_rope = jnp.broadcast_to(k_rope[:, :, None, :], (B, S, H, MLA_ROPE_DIM))
    k = jnp.concatenate([k_nope, k_rope], axis=-1)
    q = rms_norm(q, q_head_g)
    k = rms_norm(k, k_head_g)
    pos = jnp.arange(S)
    q = jnp.concatenate([q[..., :MLA_NOPE_DIM], rope(q[..., MLA_NOPE_DIM:], pos)], axis=-1)
    k = jnp.concatenate([k[..., :MLA_NOPE_DIM], rope(k[..., MLA_NOPE_DIM:], pos)], axis=-1)
    q = q * (MLA_QK_DIM ** -0.5)
    chunk_id = jnp.arange(S) // CHUNK
    outs = []
    for blk in range(S // Q_BLOCK):
        start, stop = blk * Q_BLOCK, (blk + 1) * Q_BLOCK
        logits = jnp.einsum('bqhd,bkhd->bhqk', q[:, start:stop], k[:, :stop]).astype(jnp.float32)
        mask = chunk_id[None, :stop] <= chunk_id[start:stop, None]
        logits = jnp.where(mask, logits, MASK_VALUE)
        p = jax.nn.softmax(logits, axis=-1).astype(v.dtype)
        outs.append(jnp.einsum('bhqk,bkhd->bqhd', p, v[:, :stop]))
    o = jnp.concatenate(outs, axis=1).reshape(B, S, H * MLA_V_DIM)
    return o @ w_out


def conv_ffn(h, w_in, conv_w, conv_b, w_out):
    a, g = jnp.split(h @ w_in, 2, axis=-1)
    g = lax.conv_general_dilated(
        g, conv_w[:, None, :], window_strides=(1,), padding=[(CONV_WIDTH - 1, 0)],
        dimension_numbers=('NWC', 'WIO', 'NWC'), feature_group_count=FFN_DIM) + conv_b
    return (jax.nn.silu(g) * a) @ w_out


def _fwd_setup_inputs(seed: int = 0) -> dict:
    key = jax.random.key(seed)
    ks = jax.random.split(key, 20)

    def dense(k, lead, fan_in, fan_out):
        return jax.random.normal(k, (lead, fan_in, fan_out), jnp.float32) * (fan_in ** -0.5)

    def gain(k, shape):
        return 1.0 + 0.01 * jax.random.normal(k, shape, jnp.float32)

    R, M, L = N_RET_LAYERS, N_MLA_LAYERS, DEPTH
    ret_in_width = 2 * RET_HEADS * RET_QK_DIM + 2 * RET_HEADS * RET_V_DIM
    mla_in_width = MLA_Q_RANK + MLA_KV_RANK + MLA_ROPE_DIM
    return {
        "x": jax.random.normal(ks[0], (BATCH, SEQ, D_MODEL), jnp.float32),
        "ret_norm": gain(ks[1], (R, D_MODEL)),
        "ret_w_in": dense(ks[2], R, D_MODEL, ret_in_width),
        "ret_gn": gain(ks[3], (R, RET_HEADS, RET_V_DIM)),
        "ret_w_out": dense(ks[4], R, RET_HEADS * RET_V_DIM, D_MODEL),
        "mla_norm": gain(ks[5], (M, D_MODEL)),
        "mla_w_in": dense(ks[6], M, D_MODEL, mla_in_width),
        "mla_q_norm": gain(ks[7], (M, MLA_Q_RANK)),
        "mla_w_qb": dense(ks[8], M, MLA_Q_RANK, MLA_HEADS * MLA_QK_DIM),
        "mla_kv_norm": gain(ks[9], (M, MLA_KV_RANK)),
        "mla_w_kvb": dense(ks[10], M, MLA_KV_RANK, MLA_HEADS * (MLA_NOPE_DIM + MLA_V_DIM)),
        "mla_q_head_norm": gain(ks[11], (M, MLA_QK_DIM)),
        "mla_k_head_norm": gain(ks[12], (M, MLA_QK_DIM)),
        "mla_w_out": dense(ks[13], M, MLA_HEADS * MLA_V_DIM, D_MODEL),
        "ffn_norm": gain(ks[14], (L, D_MODEL)),
        "ffn_w_in": dense(ks[15], L, D_MODEL, 2 * FFN_DIM),
        "ffn_conv_w": jax.random.normal(ks[16], (L, CONV_WIDTH, FFN_DIM), jnp.float32) * (CONV_WIDTH ** -0.5),
        "ffn_conv_b": 0.01 * jax.random.normal(ks[17], (L, FFN_DIM), jnp.float32),
        "ffn_w_out": dense(ks[18], L, FFN_DIM, D_MODEL),
    }


def _fwd_reference(x, ret_norm, ret_w_in, ret_gn, ret_w_out, mla_norm, mla_w_in, mla_q_norm, mla_w_qb,
              mla_kv_norm, mla_w_kvb, mla_q_head_norm, mla_k_head_norm, mla_w_out,
              ffn_norm, ffn_w_in, ffn_conv_w, ffn_conv_b, ffn_w_out):
    for i in range(DEPTH):
        j = i // N_MIXERS
        if i % N_MIXERS == 0:
            x = x + retention_mixer(rms_norm(x, ret_norm[j]), ret_w_in[j], ret_gn[j], ret_w_out[j])
        else:
            x = x + mla_mixer(rms_norm(x, mla_norm[j]), mla_w_in[j], mla_q_norm[j], mla_w_qb[j],
                              mla_kv_norm[j], mla_w_kvb[j], mla_q_head_norm[j], mla_k_head_norm[j],
                              mla_w_out[j])
        x = x + conv_ffn(rms_norm(x, ffn_norm[i]), ffn_w_in[i], ffn_conv_w[i], ffn_conv_b[i], ffn_w_out[i])
    return x


import jax as _jax
import jax.numpy as _jnp

TWIN_FORMAT = 'train_step'
FWD_PARAMS = ['x', 'ret_norm', 'ret_w_in', 'ret_gn', 'ret_w_out', 'mla_norm', 'mla_w_in', 'mla_q_norm', 'mla_w_qb', 'mla_kv_norm', 'mla_w_kvb', 'mla_q_head_norm', 'mla_k_head_norm', 'mla_w_out', 'ffn_norm', 'ffn_w_in', 'ffn_conv_w', 'ffn_conv_b', 'ffn_w_out']
TWIN_WEIGHTS = ['ret_norm', 'ret_w_in', 'ret_gn', 'ret_w_out', 'mla_norm', 'mla_w_in', 'mla_q_norm', 'mla_w_qb', 'mla_kv_norm', 'mla_w_kvb', 'mla_q_head_norm', 'mla_k_head_norm', 'mla_w_out', 'ffn_norm', 'ffn_w_in', 'ffn_conv_w', 'ffn_conv_b', 'ffn_w_out']
TWIN_DIFF_INPUT = 'x'
TWIN_INPUTS = ['x', 'ret_norm', 'ret_w_in', 'ret_gn', 'ret_w_out', 'mla_norm', 'mla_w_in', 'mla_q_norm', 'mla_w_qb', 'mla_kv_norm', 'mla_w_kvb', 'mla_q_head_norm', 'mla_k_head_norm', 'mla_w_out', 'ffn_norm', 'ffn_w_in', 'ffn_conv_w', 'ffn_conv_b', 'ffn_w_out', 'loss_target', 'm_ret_norm', 'm_ret_w_in', 'm_ret_gn', 'm_ret_w_out', 'm_mla_norm', 'm_mla_w_in', 'm_mla_q_norm', 'm_mla_w_qb', 'm_mla_kv_norm', 'm_mla_w_kvb', 'm_mla_q_head_norm', 'm_mla_k_head_norm', 'm_mla_w_out', 'm_ffn_norm', 'm_ffn_w_in', 'm_ffn_conv_w', 'm_ffn_conv_b', 'm_ffn_w_out', 'v_ret_norm', 'v_ret_w_in', 'v_ret_gn', 'v_ret_w_out', 'v_mla_norm', 'v_mla_w_in', 'v_mla_q_norm', 'v_mla_w_qb', 'v_mla_kv_norm', 'v_mla_w_kvb', 'v_mla_q_head_norm', 'v_mla_k_head_norm', 'v_mla_w_out', 'v_ffn_norm', 'v_ffn_w_in', 'v_ffn_conv_w', 'v_ffn_conv_b', 'v_ffn_w_out']
TWIN_OUTPUTS = ['loss', 'grad_x', 'grad_ret_norm', 'grad_ret_w_in', 'grad_ret_gn', 'grad_ret_w_out', 'grad_mla_norm', 'grad_mla_w_in', 'grad_mla_q_norm', 'grad_mla_w_qb', 'grad_mla_kv_norm', 'grad_mla_w_kvb', 'grad_mla_q_head_norm', 'grad_mla_k_head_norm', 'grad_mla_w_out', 'grad_ffn_norm', 'grad_ffn_w_in', 'grad_ffn_conv_w', 'grad_ffn_conv_b', 'grad_ffn_w_out', 'delta_ret_norm', 'delta_ret_w_in', 'delta_ret_gn', 'delta_ret_w_out', 'delta_mla_norm', 'delta_mla_w_in', 'delta_mla_q_norm', 'delta_mla_w_qb', 'delta_mla_kv_norm', 'delta_mla_w_kvb', 'delta_mla_q_head_norm', 'delta_mla_k_head_norm', 'delta_mla_w_out', 'delta_ffn_norm', 'delta_ffn_w_in', 'delta_ffn_conv_w', 'delta_ffn_conv_b', 'delta_ffn_w_out', 'new_m_ret_norm', 'new_m_ret_w_in', 'new_m_ret_gn', 'new_m_ret_w_out', 'new_m_mla_norm', 'new_m_mla_w_in', 'new_m_mla_q_norm', 'new_m_mla_w_qb', 'new_m_mla_kv_norm', 'new_m_mla_w_kvb', 'new_m_mla_q_head_norm', 'new_m_mla_k_head_norm', 'new_m_mla_w_out', 'new_m_ffn_norm', 'new_m_ffn_w_in', 'new_m_ffn_conv_w', 'new_m_ffn_conv_b', 'new_m_ffn_w_out', 'new_v_ret_norm', 'new_v_ret_w_in', 'new_v_ret_gn', 'new_v_ret_w_out', 'new_v_mla_norm', 'new_v_mla_w_in', 'new_v_mla_q_norm', 'new_v_mla_w_qb', 'new_v_mla_kv_norm', 'new_v_mla_w_kvb', 'new_v_mla_q_head_norm', 'new_v_mla_k_head_norm', 'new_v_mla_w_out', 'new_v_ffn_norm', 'new_v_ffn_w_in', 'new_v_ffn_conv_w', 'new_v_ffn_conv_b', 'new_v_ffn_w_out']
TWIN_LEAF_KINDS = {'loss': 'loss', 'grad_x': 'grad_x', 'grad_ret_norm': 'grad_w', 'grad_ret_w_in': 'grad_w', 'grad_ret_gn': 'grad_w', 'grad_ret_w_out': 'grad_w', 'grad_mla_norm': 'grad_w', 'grad_mla_w_in': 'grad_w', 'grad_mla_q_norm': 'grad_w', 'grad_mla_w_qb': 'grad_w', 'grad_mla_kv_norm': 'grad_w', 'grad_mla_w_kvb': 'grad_w', 'grad_mla_q_head_norm': 'grad_w', 'grad_mla_k_head_norm': 'grad_w', 'grad_mla_w_out': 'grad_w', 'grad_ffn_norm': 'grad_w', 'grad_ffn_w_in': 'grad_w', 'grad_ffn_conv_w': 'grad_w', 'grad_ffn_conv_b': 'grad_w', 'grad_ffn_w_out': 'grad_w', 'delta_ret_norm': 'delta_w', 'delta_ret_w_in': 'delta_w', 'delta_ret_gn': 'delta_w', 'delta_ret_w_out': 'delta_w', 'delta_mla_norm': 'delta_w', 'delta_mla_w_in': 'delta_w', 'delta_mla_q_norm': 'delta_w', 'delta_mla_w_qb': 'delta_w', 'delta_mla_kv_norm': 'delta_w', 'delta_mla_w_kvb': 'delta_w', 'delta_mla_q_head_norm': 'delta_w', 'delta_mla_k_head_norm': 'delta_w', 'delta_mla_w_out': 'delta_w', 'delta_ffn_norm': 'delta_w', 'delta_ffn_w_in': 'delta_w', 'delta_ffn_conv_w': 'delta_w', 'delta_ffn_conv_b': 'delta_w', 'delta_ffn_w_out': 'delta_w', 'new_m_ret_norm': 'new_m', 'new_m_ret_w_in': 'new_m', 'new_m_ret_gn': 'new_m', 'new_m_ret_w_out': 'new_m', 'new_m_mla_norm': 'new_m', 'new_m_mla_w_in': 'new_m', 'new_m_mla_q_norm': 'new_m', 'new_m_mla_w_qb': 'new_m', 'new_m_mla_kv_norm': 'new_m', 'new_m_mla_w_kvb': 'new_m', 'new_m_mla_q_head_norm': 'new_m', 'new_m_mla_k_head_norm': 'new_m', 'new_m_mla_w_out': 'new_m', 'new_m_ffn_norm': 'new_m', 'new_m_ffn_w_in': 'new_m', 'new_m_ffn_conv_w': 'new_m', 'new_m_ffn_conv_b': 'new_m', 'new_m_ffn_w_out': 'new_m', 'new_v_ret_norm': 'new_v', 'new_v_ret_w_in': 'new_v', 'new_v_ret_gn': 'new_v', 'new_v_ret_w_out': 'new_v', 'new_v_mla_norm': 'new_v', 'new_v_mla_w_in': 'new_v', 'new_v_mla_q_norm': 'new_v', 'new_v_mla_w_qb': 'new_v', 'new_v_mla_kv_norm': 'new_v', 'new_v_mla_w_kvb': 'new_v', 'new_v_mla_q_head_norm': 'new_v', 'new_v_mla_k_head_norm': 'new_v', 'new_v_mla_w_out': 'new_v', 'new_v_ffn_norm': 'new_v', 'new_v_ffn_w_in': 'new_v', 'new_v_ffn_conv_w': 'new_v', 'new_v_ffn_conv_b': 'new_v', 'new_v_ffn_w_out': 'new_v'}


def _forward(args):
    return _fwd_reference(*[args[k] for k in FWD_PARAMS])


def _output_shape():
    out = _jax.eval_shape(lambda: _forward(_fwd_setup_inputs(0)))
    return out.shape, out.dtype

N_MICROBATCH = 1
ADAM_LR = 0.001
ADAM_B1 = 0.9
ADAM_B2 = 0.999
ADAM_EPS = 1e-08
ADAM_WD = 0.01
ADAM_STEP = 10
PER_EXAMPLE_BATCH_AXIS = {'x': 0, 'loss_target': 0}
SHARED_INPUTS = []
_WEIGHT_DTYPES = {'ret_norm': _jnp.float32, 'ret_w_in': _jnp.float32, 'ret_gn': _jnp.float32, 'ret_w_out': _jnp.float32, 'mla_norm': _jnp.float32, 'mla_w_in': _jnp.float32, 'mla_q_norm': _jnp.float32, 'mla_w_qb': _jnp.float32, 'mla_kv_norm': _jnp.float32, 'mla_w_kvb': _jnp.float32, 'mla_q_head_norm': _jnp.float32, 'mla_k_head_norm': _jnp.float32, 'mla_w_out': _jnp.float32, 'ffn_norm': _jnp.float32, 'ffn_w_in': _jnp.float32, 'ffn_conv_w': _jnp.float32, 'ffn_conv_b': _jnp.float32, 'ffn_w_out': _jnp.float32}
MOMENT_SCALE = {'ret_norm': 1.284926e+01, 'ret_w_in': 2.938788e-01, 'ret_gn': 5.541192e+00, 'ret_w_out': 3.641178e-01, 'mla_norm': 1.262099e-01, 'mla_w_in': 1.482710e-01, 'mla_q_norm': 9.936192e-02, 'mla_w_qb': 5.322180e-02, 'mla_kv_norm': 6.471180e-01, 'mla_w_kvb': 6.837166e-02, 'mla_q_head_norm': 7.245298e-01, 'mla_k_head_norm': 7.241481e-01, 'mla_w_out': 7.781665e-02, 'ffn_norm': 2.571697e+01, 'ffn_w_in': 2.161220e-01, 'ffn_conv_w': 2.802393e+00, 'ffn_conv_b': 3.417571e+00, 'ffn_w_out': 3.415071e-01}


def _to_microbatches(a, axis):
    t = _jnp.moveaxis(a, axis, 0)
    t = t.reshape((N_MICROBATCH, t.shape[0] // N_MICROBATCH) + t.shape[1:])
    return _jnp.moveaxis(t, 1, axis + 1)


def setup_inputs(seed: int = 0) -> dict:
    inp = _fwd_setup_inputs(seed)
    key = _jax.random.fold_in(_jax.random.key(seed), 7919)
    shape, _ = _output_shape()
    out = dict(inp)
    out["loss_target"] = _jax.random.normal(_jax.random.fold_in(key, 0), shape, _jnp.float32)
    for i, name in enumerate(TWIN_WEIGHTS):
        w = inp[name].astype(_jnp.float32)
        if MOMENT_SCALE is None:
            s = _jnp.sqrt(_jnp.mean(_jnp.square(w)) + 1e-30)
        else:
            s = MOMENT_SCALE[name]
        km, kv = _jax.random.split(_jax.random.fold_in(key, i + 1))
        out[name] = w
        out["m_" + name] = s * _jax.random.normal(km, w.shape, _jnp.float32)
        out["v_" + name] = (s * s) * _jax.random.uniform(kv, w.shape, _jnp.float32, 0.5, 1.5)
    if N_MICROBATCH > 1:
        for name, axis in PER_EXAMPLE_BATCH_AXIS.items():
            out[name] = _to_microbatches(out[name], axis)
    return {'x': out['x'], 'ret_norm': out['ret_norm'], 'ret_w_in': out['ret_w_in'], 'ret_gn': out['ret_gn'], 'ret_w_out': out['ret_w_out'], 'mla_norm': out['mla_norm'], 'mla_w_in': out['mla_w_in'], 'mla_q_norm': out['mla_q_norm'], 'mla_w_qb': out['mla_w_qb'], 'mla_kv_norm': out['mla_kv_norm'], 'mla_w_kvb': out['mla_w_kvb'], 'mla_q_head_norm': out['mla_q_head_norm'], 'mla_k_head_norm': out['mla_k_head_norm'], 'mla_w_out': out['mla_w_out'], 'ffn_norm': out['ffn_norm'], 'ffn_w_in': out['ffn_w_in'], 'ffn_conv_w': out['ffn_conv_w'], 'ffn_conv_b': out['ffn_conv_b'], 'ffn_w_out': out['ffn_w_out'], 'loss_target': out['loss_target'], 'm_ret_norm': out['m_ret_norm'], 'm_ret_w_in': out['m_ret_w_in'], 'm_ret_gn': out['m_ret_gn'], 'm_ret_w_out': out['m_ret_w_out'], 'm_mla_norm': out['m_mla_norm'], 'm_mla_w_in': out['m_mla_w_in'], 'm_mla_q_norm': out['m_mla_q_norm'], 'm_mla_w_qb': out['m_mla_w_qb'], 'm_mla_kv_norm': out['m_mla_kv_norm'], 'm_mla_w_kvb': out['m_mla_w_kvb'], 'm_mla_q_head_norm': out['m_mla_q_head_norm'], 'm_mla_k_head_norm': out['m_mla_k_head_norm'], 'm_mla_w_out': out['m_mla_w_out'], 'm_ffn_norm': out['m_ffn_norm'], 'm_ffn_w_in': out['m_ffn_w_in'], 'm_ffn_conv_w': out['m_ffn_conv_w'], 'm_ffn_conv_b': out['m_ffn_conv_b'], 'm_ffn_w_out': out['m_ffn_w_out'], 'v_ret_norm': out['v_ret_norm'], 'v_ret_w_in': out['v_ret_w_in'], 'v_ret_gn': out['v_ret_gn'], 'v_ret_w_out': out['v_ret_w_out'], 'v_mla_norm': out['v_mla_norm'], 'v_mla_w_in': out['v_mla_w_in'], 'v_mla_q_norm': out['v_mla_q_norm'], 'v_mla_w_qb': out['v_mla_w_qb'], 'v_mla_kv_norm': out['v_mla_kv_norm'], 'v_mla_w_kvb': out['v_mla_w_kvb'], 'v_mla_q_head_norm': out['v_mla_q_head_norm'], 'v_mla_k_head_norm': out['v_mla_k_head_norm'], 'v_mla_w_out': out['v_mla_w_out'], 'v_ffn_norm': out['v_ffn_norm'], 'v_ffn_w_in': out['v_ffn_w_in'], 'v_ffn_conv_w': out['v_ffn_conv_w'], 'v_ffn_conv_b': out['v_ffn_conv_b'], 'v_ffn_w_out': out['v_ffn_w_out']}


def _loss(weights, diff, rest, loss_target):
    with _jax.named_scope("forward"):
        args = {**rest, TWIN_DIFF_INPUT: diff, **{k: w.astype(_WEIGHT_DTYPES[k]) for k, w in weights.items()}}
        y = _forward(args)
    with _jax.named_scope("loss_head"):
        err = _jnp.square(y.astype(_jnp.float32) - loss_target)
        return 0.5 * _jnp.sum(_jnp.mean(err, axis=-1)) if err.ndim else 0.5 * err


def _adamw(w, g, m, v):
    m = ADAM_B1 * m + (1.0 - ADAM_B1) * g
    v = ADAM_B2 * v + (1.0 - ADAM_B2) * _jnp.square(g)
    m_hat = m / (1.0 - ADAM_B1 ** ADAM_STEP)
    v_hat = v / (1.0 - ADAM_B2 ** ADAM_STEP)
    delta = -ADAM_LR * (m_hat / (_jnp.sqrt(v_hat) + ADAM_EPS) + ADAM_WD * w)
    return delta, m, v


def reference(x, ret_norm, ret_w_in, ret_gn, ret_w_out, mla_norm, mla_w_in, mla_q_norm, mla_w_qb, mla_kv_norm, mla_w_kvb, mla_q_head_norm, mla_k_head_norm, mla_w_out, ffn_norm, ffn_w_in, ffn_conv_w, ffn_conv_b, ffn_w_out, loss_target, m_ret_norm, m_ret_w_in, m_ret_gn, m_ret_w_out, m_mla_norm, m_mla_w_in, m_mla_q_norm, m_mla_w_qb, m_mla_kv_norm, m_mla_w_kvb, m_mla_q_head_norm, m_mla_k_head_norm, m_mla_w_out, m_ffn_norm, m_ffn_w_in, m_ffn_conv_w, m_ffn_conv_b, m_ffn_w_out, v_ret_norm, v_ret_w_in, v_ret_gn, v_ret_w_out, v_mla_norm, v_mla_w_in, v_mla_q_norm, v_mla_w_qb, v_mla_kv_norm, v_mla_w_kvb, v_mla_q_head_norm, v_mla_k_head_norm, v_mla_w_out, v_ffn_norm, v_ffn_w_in, v_ffn_conv_w, v_ffn_conv_b, v_ffn_w_out):
    given = dict(x=x, ret_norm=ret_norm, ret_w_in=ret_w_in, ret_gn=ret_gn, ret_w_out=ret_w_out, mla_norm=mla_norm, mla_w_in=mla_w_in, mla_q_norm=mla_q_norm, mla_w_qb=mla_w_qb, mla_kv_norm=mla_kv_norm, mla_w_kvb=mla_w_kvb, mla_q_head_norm=mla_q_head_norm, mla_k_head_norm=mla_k_head_norm, mla_w_out=mla_w_out, ffn_norm=ffn_norm, ffn_w_in=ffn_w_in, ffn_conv_w=ffn_conv_w, ffn_conv_b=ffn_conv_b, ffn_w_out=ffn_w_out, loss_target=loss_target, m_ret_norm=m_ret_norm, m_ret_w_in=m_ret_w_in, m_ret_gn=m_ret_gn, m_ret_w_out=m_ret_w_out, m_mla_norm=m_mla_norm, m_mla_w_in=m_mla_w_in, m_mla_q_norm=m_mla_q_norm, m_mla_w_qb=m_mla_w_qb, m_mla_kv_norm=m_mla_kv_norm, m_mla_w_kvb=m_mla_w_kvb, m_mla_q_head_norm=m_mla_q_head_norm, m_mla_k_head_norm=m_mla_k_head_norm, m_mla_w_out=m_mla_w_out, m_ffn_norm=m_ffn_norm, m_ffn_w_in=m_ffn_w_in, m_ffn_conv_w=m_ffn_conv_w, m_ffn_conv_b=m_ffn_conv_b, m_ffn_w_out=m_ffn_w_out, v_ret_norm=v_ret_norm, v_ret_w_in=v_ret_w_in, v_ret_gn=v_ret_gn, v_ret_w_out=v_ret_w_out, v_mla_norm=v_mla_norm, v_mla_w_in=v_mla_w_in, v_mla_q_norm=v_mla_q_norm, v_mla_w_qb=v_mla_w_qb, v_mla_kv_norm=v_mla_kv_norm, v_mla_w_kvb=v_mla_w_kvb, v_mla_q_head_norm=v_mla_q_head_norm, v_mla_k_head_norm=v_mla_k_head_norm, v_mla_w_out=v_mla_w_out, v_ffn_norm=v_ffn_norm, v_ffn_w_in=v_ffn_w_in, v_ffn_conv_w=v_ffn_conv_w, v_ffn_conv_b=v_ffn_conv_b, v_ffn_w_out=v_ffn_w_out)
    weights = {n: given[n] for n in TWIN_WEIGHTS}
    shared = {n: given[n] for n in SHARED_INPUTS}
    per_example = {n: given[n] for n in ['x']}
    grad_fn = _jax.value_and_grad(_loss, argnums=(0, 1))

    def one_microbatch(ex, loss_target):
        ex = dict(ex)
        diff = ex.pop(TWIN_DIFF_INPUT)
        return grad_fn(weights, diff, {**shared, **ex}, loss_target)

    if N_MICROBATCH == 1:
        loss, (grad_w, grad_x) = one_microbatch(per_example, given["loss_target"])
    else:
        def body(carry, xs):
            loss_sum, grad_sum = carry
            l_k, (gw_k, gx_k) = one_microbatch(xs[0], xs[1])
            with _jax.named_scope("update"):
                return (loss_sum + l_k, _jax.tree.map(_jnp.add, grad_sum, gw_k)), gx_k

        init = (_jnp.zeros((), _jnp.float32), _jax.tree.map(_jnp.zeros_like, weights))
        (loss, grad_w), grad_x = _jax.lax.scan(body, init, (per_example, given["loss_target"]))
    with _jax.named_scope("update"):
        delta_w, new_m, new_v = {}, {}, {}
        for n in TWIN_WEIGHTS:
            delta_w[n], new_m[n], new_v[n] = _adamw(weights[n], grad_w[n], given["m_" + n], given["v_" + n])
    return (loss, grad_x, *[grad_w[n] for n in TWIN_WEIGHTS], *[delta_w[n] for n in TWIN_WEIGHTS],
            *[new_m[n] for n in TWIN_WEIGHTS], *[new_v[n] for n in TWIN_WEIGHTS])
```

```python
import functools

import jax
import jax.numpy as jnp
from jax import lax
from jax.experimental import pallas as pl
from jax.experimental.pallas import tpu as pltpu

F32, BF16 = jnp.float32, jnp.bfloat16

NDEV = 8
D_MODEL = 1024
CHUNK = 64
RMS_EPS = 1e-6
ROPE_THETA = 10000.0
RET_H, RET_DK, RET_DV = 4, 256, 512
RET_SC = 256
MLA_H, MLA_QR, MLA_KVR = 8, 384, 256
MLA_NOPE, MLA_ROPE, MLA_V = 128, 64, 128
MLA_QK = MLA_NOPE + MLA_ROPE
MASK_VALUE = -1e30
FFN = 2816
FSH = FFN * 2 // NDEV
ATT_TQ = 256
ADAM_LR, ADAM_B1, ADAM_B2, ADAM_EPS, ADAM_WD, ADAM_STEP = 0.001, 0.9, 0.999, 1e-08, 0.01, 10
MESH = pl.DeviceIdType.MESH
VMEM_LIMIT = 56 * 2 ** 20


def _cp(sem):
    return pltpu.CompilerParams(dimension_semantics=sem, vmem_limit_bytes=VMEM_LIMIT)


def _dot(a, b, dims):
    return lax.dot_general(a, b, (dims, ((), ())), preferred_element_type=F32)


NN = ((1,), (0,))
NT = ((1,), (1,))
TN = ((0,), (0,))


def _mm(name, a, b, *, grid, a_spec, b_spec, o_spec, out_shape, dims, kax=None, res=None, res_spec=None,
        jb=0, acc_shape=None):
    nk = grid[kax] if kax is not None else 1

    def body(*refs):
        if res is not None:
            a_ref, b_ref, r_ref, o_ref = refs[:4]
        else:
            a_ref, b_ref, o_ref = refs[:3]
        if jb:
            part = _dot(a_ref[0], b_ref[0], dims)
            for j in range(1, jb):
                part = part + _dot(a_ref[j], b_ref[j], dims)
        else:
            part = _dot(a_ref[...], b_ref[...], dims)

        def fin(acc):
            if res is not None:
                acc = acc + r_ref[...]
            o_ref[...] = acc.astype(o_ref.dtype)

        if nk == 1:
            fin(part)
        else:
            acc_ref = refs[-1]
            k = pl.program_id(kax)

            @pl.when(k == 0)
            def _():
                acc_ref[...] = part

            @pl.when(k > 0)
            def _():
                acc_ref[...] += part

            @pl.when(k == nk - 1)
            def _():
                fin(acc_ref[...])

    sem = tuple("arbitrary" if i == kax else "parallel" for i in range(len(grid)))
    in_specs = [a_spec, b_spec] + ([res_spec] if res is not None else [])
    args = (a, b) + ((res,) if res is not None else ())
    scratch = [pltpu.VMEM(acc_shape, F32)] if nk > 1 else []
    return pl.pallas_call(body, name=name, grid=grid, in_specs=in_specs, out_specs=o_spec, out_shape=out_shape,
                          scratch_shapes=scratch, compiler_params=_cp(sem))(*args)


def _bs(shape, fn):
    return pl.BlockSpec(shape, fn)


def _rms_fwd(name, x, g, tm=512):
    T, D = x.shape

    def body(x_ref, g_ref, o_ref):
        xf = x_ref[...]
        r = lax.rsqrt(jnp.mean(xf * xf, axis=-1, keepdims=True) + RMS_EPS)
        o_ref[...] = (xf * r * g_ref[...]).astype(o_ref.dtype)

    return pl.pallas_call(
        body, name=name, grid=(T // tm,),
        in_specs=[_bs((tm, D), lambda i: (i, 0)), _bs((1, D), lambda i: (0, 0))],
        out_specs=_bs((tm, D), lambda i: (i, 0)), out_shape=jax.ShapeDtypeStruct((T, D), BF16),
        compiler_params=_cp(("parallel",)))(x, g)


def _rms_bwd(name, x, g, dh, dres=None, tm=512):
    T, D = x.shape

    def body(*refs):
        if dres is not None:
            x_ref, g_ref, dh_ref, dres_ref, dx_ref, dg_ref = refs
        else:
            x_ref, g_ref, dh_ref, dx_ref, dg_ref = refs
        i = pl.program_id(0)
        xf = x_ref[...]
        r = lax.rsqrt(jnp.mean(xf * xf, axis=-1, keepdims=True) + RMS_EPS)
        xh = xf * r
        d = dh_ref[...].astype(F32)
        dxh = d * g_ref[...]
        dx = r * (dxh - xh * jnp.mean(dxh * xh, axis=-1, keepdims=True))
        if dres is not None:
            dx = dx + dres_ref[...]
        dx_ref[...] = dx
        part = jnp.sum(d * xh, axis=0, keepdims=True)

        @pl.when(i == 0)
        def _():
            dg_ref[...] = part

        @pl.when(i > 0)
        def _():
            dg_ref[...] += part

    row = _bs((tm, D), lambda i: (i, 0))
    vec = _bs((1, D), lambda i: (0, 0))
    in_specs = [row, vec, row] + ([row] if dres is not None else [])
    args = (x, g, dh) + ((dres,) if dres is not None else ())
    return pl.pallas_call(
        body, name=name, grid=(T // tm,), in_specs=in_specs, out_specs=[row, vec],
        out_shape=[jax.ShapeDtypeStruct((T, D), F32), jax.ShapeDtypeStruct((1, D), F32)],
        compiler_params=_cp(("arbitrary",)))(*args)


def _loss(y, tgt, tm=512):
    T, D = y.shape

    def body(y_ref, t_ref, dy_ref, s_ref):
        i = pl.program_id(0)
        e = y_ref[...] - t_ref[...]
        dy_ref[...] = e * (1.0 / D)
        part = jnp.sum(e * e, axis=0, keepdims=True)

        @pl.when(i == 0)
        def _():
            s_ref[...] = part

        @pl.when(i > 0)
        def _():
            s_ref[...] += part

    row = _bs((tm, D), lambda i: (i, 0))
    return pl.pallas_call(
        body, name="loss_head", grid=(T // tm,), in_specs=[row, row], out_specs=[row, _bs((1, D), lambda i: (0, 0))],
        out_shape=[jax.ShapeDtypeStruct((T, D), F32), jax.ShapeDtypeStruct((1, D), F32)],
        compiler_params=_cp(("arbitrary",)))(y, tgt)


def _shift_rows(t, k, row):
    return jnp.where(row >= k, pltpu.roll(t, k, 0), 0.0)


def _shift_rows_up(t, k, row, n):
    return jnp.where(row < n - k, pltpu.roll(t, n - k, 0), 0.0)


def _convffn_fwd(name, u, cw, cb, B, S):
    _, J, T, F = u.shape

    def body(u_ref, cw_ref, cb_ref, o_ref):
        a = u_ref[0].astype(F32)
        g = u_ref[1].astype(F32)
        row = lax.broadcasted_iota(jnp.int32, (S, F), 0)
        w0, w1, w2 = cw_ref[0:1, :], cw_ref[1:2, :], cw_ref[2:3, :]
        gc = _shift_rows(g, 2, row) * w0 + _shift_rows(g, 1, row) * w1 + g * w2 + cb_ref[...]
        o_ref[...] = (gc * jax.nn.sigmoid(gc) * a).astype(o_ref.dtype)

    return pl.pallas_call(
        body, name=name, grid=(J, B),
        in_specs=[_bs((2, None, S, F), lambda j, b: (0, j, b, 0)), _bs((None, 3, F), lambda j, b: (j, 0, 0)),
                  _bs((None, 1, F), lambda j, b: (j, 0, 0))],
        out_specs=_bs((None, S, F), lambda j, b: (j, b, 0)), out_shape=jax.ShapeDtypeStruct((J, T, F), BF16),
        compiler_params=_cp(("parallel", "parallel")))(u, cw, cb)


def _convffn_bwd(name, u, cw, cb, dgt, B, S):
    _, J, T, F = u.shape

    def body(u_ref, cw_ref, cb_ref, d_ref, du_ref, dcw_ref, dcb_ref):
        b = pl.program_id(1)
        a = u_ref[0].astype(F32)
        g = u_ref[1].astype(F32)
        d = d_ref[...].astype(F32)
        row = lax.broadcasted_iota(jnp.int32, (S, F), 0)
        w0, w1, w2 = cw_ref[0:1, :], cw_ref[1:2, :], cw_ref[2:3, :]
        g1, g2 = _shift_rows(g, 1, row), _shift_rows(g, 2, row)
        gc = g2 * w0 + g1 * w1 + g * w2 + cb_ref[...]
        sg = jax.nn.sigmoid(gc)
        du_ref[0] = (d * gc * sg).astype(du_ref.dtype)
        dgc = d * a * (sg * (1.0 + gc * (1.0 - sg)))
        dg = dgc * w2 + _shift_rows_up(dgc, 1, row, S) * w1 + _shift_rows_up(dgc, 2, row, S) * w0
        du_ref[1] = dg.astype(du_ref.dtype)
        parts = [jnp.sum(dgc * g2, axis=0, keepdims=True), jnp.sum(dgc * g1, axis=0, keepdims=True),
                 jnp.sum(dgc * g, axis=0, keepdims=True)]
        pb = jnp.sum(dgc, axis=0, keepdims=True)

        @pl.when(b == 0)
        def _():
            for k in range(3):
                dcw_ref[k:k + 1, :] = parts[k]
            dcb_ref[...] = pb

        @pl.when(b > 0)
        def _():
            for k in range(3):
                dcw_ref[k:k + 1, :] += parts[k]
            dcb_ref[...] += pb

    uspec = _bs((2, None, S, F), lambda j, b: (0, j, b, 0))
    return pl.pallas_call(
        body, name=name, grid=(J, B),
        in_specs=[uspec, _bs((None, 3, F), lambda j, b: (j, 0, 0)), _bs((None, 1, F), lambda j, b: (j, 0, 0)),
                  _bs((None, S, F), lambda j, b: (j, b, 0))],
        out_specs=[uspec, _bs((None, 3, F), lambda j, b: (j, 0, 0)), _bs((None, 1, F), lambda j, b: (j, 0, 0))],
        out_shape=[jax.ShapeDtypeStruct(u.shape, BF16), jax.ShapeDtypeStruct((J, 3, F), F32),
                   jax.ShapeDtypeStruct((J, 1, F), F32)],
        compiler_params=_cp(("parallel", "arbitrary")))(u, cw, cb, dgt)


def _ret_tables(S):
    half = RET_DK // 2
    inv = ROPE_THETA ** (-jnp.arange(half, dtype=F32) / half)
    ang = jnp.arange(S).astype(F32)[:, None] * inv[None, :]
    lg = jnp.log1p(-jnp.exp2(-5.0 - jnp.arange(RET_H, dtype=F32)))
    i = jnp.arange(RET_SC, dtype=F32)
    same_or_earlier = (jnp.floor(i[None, :] / CHUNK) <= jnp.floor(i[:, None] / CHUNK)).astype(F32)
    dm = jnp.exp(lg[:, None, None] * jnp.abs(i[:, None] - i[None, :])) * same_or_earlier[None]
    qd = jnp.exp(lg[:, None] * (i + 1.0))[:, :, None]
    kd = jnp.exp(lg[:, None] * (RET_SC - 1.0 - i))[:, :, None]
    cd = jnp.exp(lg * RET_SC)[:, None, None]
    return jnp.cos(ang), jnp.sin(ang), dm, qd, kd, cd


def _rope_halves(t, cs, sn):
    h = t.shape[-1] // 2
    t1, t2 = t[:, :h], t[:, h:]
    return jnp.concatenate([t1 * cs - t2 * sn, t2 * cs + t1 * sn], axis=-1)


def _unrope_halves(d, cs, sn):
    h = d.shape[-1] // 2
    d1, d2 = d[:, :h], d[:, h:]
    return jnp.concatenate([d1 * cs + d2 * sn, d2 * cs - d1 * sn], axis=-1)


def _ret_specs(nC, order):
    SC = RET_SC

    def sp(shape, fn):
        return _bs(shape, lambda *g: fn(*order(*g)))

    q = sp((SC, RET_DK), lambda b, h, c: (b * nC + c, h))
    k = sp((SC, RET_DK), lambda b, h, c: (b * nC + c, RET_H + h))
    v = sp((SC, RET_DV), lambda b, h, c: (b * nC + c, RET_H + h))
    g = sp((SC, RET_DV), lambda b, h, c: (b * nC + c, 2 * RET_H + h))
    cs = sp((SC, RET_DK // 2), lambda b, h, c: (c, 0))
    dm = sp((None, SC, SC), lambda b, h, c: (h, 0, 0))
    dv = sp((None, SC, 1), lambda b, h, c: (h, 0, 0))
    cd = sp((None, 1, 1), lambda b, h, c: (h, 0, 0))
    gn = sp((None, 1, RET_DV), lambda b, h, c: (h, 0, 0))
    wide = sp((SC, RET_DV), lambda b, h, c: (b * nC + c, h))
    narrow = sp((SC, RET_DK), lambda b, h, c: (b * nC + c, h))
    st = sp((None, None, None, RET_DK, RET_DV), lambda b, h, c: (b, h, c, 0, 0))
    return dict(q=q, k=k, v=v, g=g, cs=cs, dm=dm, dv=dv, cd=cd, gn=gn, wide=wide, narrow=narrow, st=st)


def _ret_fwd(proj, tabs, gn, B, S):
    T = B * S
    nC = S // RET_SC
    cos, sin, dm, qd, kd, cd = tabs
    s = _ret_specs(nC, lambda b, h, c: (b, h, c))

    def body(q_ref, k_ref, v_ref, g_ref, cos_ref, sin_ref, dm_ref, qd_ref, kd_ref, cd_ref, gn_ref,
             o_ref, gt_ref, st_ref, state):
        c = pl.program_id(2)

        @pl.when(c == 0)
        def _():
            state[...] = jnp.zeros_like(state)

        cs, sn = cos_ref[...], sin_ref[...]
        qf = _rope_halves(q_ref[...].astype(F32), cs, sn)
        kf = _rope_halves(k_ref[...].astype(F32), cs, sn) * (RET_DK ** -0.5)
        v = v_ref[...]
        p = _dot(qf.astype(BF16), kf.astype(BF16), NT) * dm_ref[...]
        st = state[...]
        stb = st.astype(BF16)
        st_ref[...] = stb
        o = _dot(p.astype(BF16), v, NN) + _dot((qf * qd_ref[...]).astype(BF16), stb, NN)
        state[...] = st * cd_ref[...] + _dot((kf * kd_ref[...]).astype(BF16), v, TN)
        o_ref[...] = o
        r = lax.rsqrt(jnp.mean(o * o, axis=-1, keepdims=True) + RMS_EPS)
        gf = g_ref[...].astype(F32)
        gt_ref[...] = ((o * r * gn_ref[...]) * (gf * jax.nn.sigmoid(gf))).astype(BF16)

    return pl.pallas_call(
        body, name="ret_fwd", grid=(B, RET_H, nC),
        in_specs=[s["q"], s["k"], s["v"], s["g"], s["cs"], s["cs"], s["dm"], s["dv"], s["dv"], s["cd"], s["gn"]],
        out_specs=[s["wide"], s["wide"], s["st"]],
        out_shape=[jax.ShapeDtypeStruct((T, RET_H * RET_DV), F32), jax.ShapeDtypeStruct((T, RET_H * RET_DV), BF16),
                   jax.ShapeDtypeStruct((B, RET_H, nC, RET_DK, RET_DV), BF16)],
        scratch_shapes=[pltpu.VMEM((RET_DK, RET_DV), F32)],
        compiler_params=_cp(("parallel", "parallel", "arbitrary")))(proj, proj, proj, proj, cos, sin, dm, qd, kd, cd, gn)


def _ret_bwd(proj, o_raw, states, dgt, tabs, gn, B, S):
    T = B * S
    nC = S // RET_SC
    cos, sin, dm, qd, kd, cd = tabs
    s = _ret_specs(nC, lambda h, b, c: (b, h, nC - 1 - c))

    def body(q_ref, k_ref, v_ref, g_ref, o_ref, st_ref, d_ref, cos_ref, sin_ref, dm_ref, qd_ref, kd_ref, cd_ref,
             gn_ref, dq_ref, dk_ref, dv_ref, dg_ref, dgn_ref, dstate):
        b, c = pl.program_id(1), pl.program_id(2)

        @pl.when(c == 0)
        def _():
            dstate[...] = jnp.zeros_like(dstate)

        @pl.when((b == 0) & (c == 0))
        def _():
            dgn_ref[...] = jnp.zeros_like(dgn_ref)

        cs, sn = cos_ref[...], sin_ref[...]
        qf = _rope_halves(q_ref[...].astype(F32), cs, sn)
        kf = _rope_halves(k_ref[...].astype(F32), cs, sn) * (RET_DK ** -0.5)
        v = v_ref[...]
        gnv = gn_ref[...]
        o = o_ref[...]
        r = lax.rsqrt(jnp.mean(o * o, axis=-1, keepdims=True) + RMS_EPS)
        oh = o * r
        gf = g_ref[...].astype(F32)
        sg = jax.nn.sigmoid(gf)
        d = d_ref[...].astype(F32)
        dg_ref[...] = (d * (oh * gnv) * (sg * (1.0 + gf * (1.0 - sg)))).astype(BF16)
        don = d * (gf * sg)
        dgn_ref[...] += jnp.sum(don * oh, axis=0, keepdims=True)
        doh = don * gnv
        dO = (r * (doh - oh * jnp.mean(doh * oh, axis=-1, keepdims=True))).astype(BF16)
        dmv = dm_ref[...]
        qb, kb = qf.astype(BF16), kf.astype(BF16)
        p = (_dot(qb, kb, NT) * dmv).astype(BF16)
        dp = (_dot(dO, v, NT) * dmv).astype(BF16)
        st = st_ref[...]
        dsn = dstate[...]
        dsb = dsn.astype(BF16)
        qdv, kdv = qd_ref[...], kd_ref[...]
        dq = _dot(dp, kb, NN) + _dot(dO, st, NT) * qdv
        dk = _dot(dp, qb, TN) + _dot(v, dsb, NT) * kdv
        dv = _dot(p, dO, TN) + _dot((kf * kdv).astype(BF16), dsb, NN)
        dstate[...] = dsn * cd_ref[...] + _dot((qf * qdv).astype(BF16), dO, TN)
        dq_ref[...] = _unrope_halves(dq, cs, sn).astype(BF16)
        dk_ref[...] = (_unrope_halves(dk, cs, sn) * (RET_DK ** -0.5)).astype(BF16)
        dv_ref[...] = dv.astype(BF16)

    return pl.pallas_call(
        body, name="ret_bwd", grid=(RET_H, B, nC),
        in_specs=[s["q"], s["k"], s["v"], s["g"], s["wide"], s["st"], s["wide"], s["cs"], s["cs"], s["dm"], s["dv"],
                  s["dv"], s["cd"], s["gn"]],
        out_specs=[s["narrow"], s["narrow"], s["wide"], s["wide"], s["gn"]],
        out_shape=[jax.ShapeDtypeStruct((T, RET_H * RET_DK), BF16), jax.ShapeDtypeStruct((T, RET_H * RET_DK), BF16),
                   jax.ShapeDtypeStruct((T, RET_H * RET_DV), BF16), jax.ShapeDtypeStruct((T, RET_H * RET_DV), BF16),
                   jax.ShapeDtypeStruct((RET_H, 1, RET_DV), F32)],
        scratch_shapes=[pltpu.VMEM((RET_DK, RET_DV), F32)],
        compiler_params=_cp(("arbitrary", "arbitrary", "arbitrary")))(
            proj, proj, proj, proj, o_raw, states, dgt, cos, sin, dm, qd, kd, cd, gn)


def _mla_tables(S):
    half = MLA_ROPE // 2
    inv = ROPE_THETA ** (-jnp.arange(half, dtype=F32) / half)
    ang = jnp.arange(S).astype(F32)[:, None] * inv[None, :]
    cos, sin = jnp.cos(ang), jnp.sin(ang)
    i = jnp.arange(MLA_ROPE)
    swap = (i[:, None] == (i[None, :] + half) % MLA_ROPE).astype(F32)
    return jnp.concatenate([cos, cos], axis=-1), jnp.concatenate([-sin, sin], axis=-1), swap


def _swap_halves(t, swap):
    return jnp.dot(t, swap, precision=lax.Precision.HIGHEST, preferred_element_type=F32)


def _head_norm_rope(n, r_, gn, gr, cos, sin, swap, scale):
    ssq = jnp.sum(n * n, axis=-1, keepdims=True) + jnp.sum(r_ * r_, axis=-1, keepdims=True)
    rstd = lax.rsqrt(ssq * (1.0 / MLA_QK) + RMS_EPS)
    yn = n * rstd * gn
    yr = r_ * rstd * gr
    yr = yr * cos + _swap_halves(yr, swap) * sin
    if scale != 1.0:
        yn, yr = yn * scale, yr * scale
    return yn, yr, rstd


def _head_norm_rope_bwd(dn, dr, n, r_, gn, gr, cos, sin, swap, scale):
    ssq = jnp.sum(n * n, axis=-1, keepdims=True) + jnp.sum(r_ * r_, axis=-1, keepdims=True)
    rstd = lax.rsqrt(ssq * (1.0 / MLA_QK) + RMS_EPS)
    hn, hr = n * rstd, r_ * rstd
    dyn = dn * scale if scale != 1.0 else dn
    dr = dr * scale if scale != 1.0 else dr
    dyr = dr * cos + _swap_halves(dr * sin, swap)
    dgn = jnp.sum(dyn * hn, axis=0, keepdims=True)
    dgr = jnp.sum(dyr * hr, axis=0, keepdims=True)
    dhn, dhr = dyn * gn, dyr * gr
    mt = (jnp.sum(dhn * hn, axis=-1, keepdims=True) + jnp.sum(dhr * hr, axis=-1, keepdims=True)) * (1.0 / MLA_QK)
    return rstd * (dhn - hn * mt), rstd * (dhr - hr * mt), dgn, dgr


def _chunk_mask(i, j, tq):
    qc = (i * tq + lax.broadcasted_iota(jnp.int32, (tq, tq), 0)) // CHUNK
    kc = (j * tq + lax.broadcasted_iota(jnp.int32, (tq, tq), 1)) // CHUNK
    return kc <= qc


def _mla_fwd(q_raw, kv, kr, gains, tabs, B, S):
    T = B * S
    TQ = ATT_TQ
    nQ = S // TQ
    qgn, qgr, kgn, kgr = gains
    cos, sin, swap = tabs
    scale = MLA_QK ** -0.5

    def body(q_ref, kv_ref, kr_ref, qgn_ref, qgr_ref, kgn_ref, kgr_ref, cq_ref, sq_ref, ck_ref, sk_ref, sw_ref,
             o_ref, lse_ref, kn_s, kr_s, v_s):
        i = pl.program_id(2)
        sw = sw_ref[...]

        @pl.when(i == 0)
        def _():
            def prep(t, _):
                rows = pl.ds(pl.multiple_of(t * TQ, TQ), TQ)
                kn, krr, _ = _head_norm_rope(kv_ref[rows, :MLA_NOPE], kr_ref[rows, :], kgn_ref[...], kgr_ref[...],
                                             ck_ref[rows, :], sk_ref[rows, :], sw, 1.0)
                kn_s[rows, :] = kn.astype(BF16)
                kr_s[rows, :] = krr.astype(BF16)
                v_s[rows, :] = kv_ref[rows, MLA_NOPE:].astype(BF16)
                return 0

            lax.fori_loop(0, nQ, prep, 0)

        qn, qr, _ = _head_norm_rope(q_ref[:, :MLA_NOPE], q_ref[:, MLA_NOPE:], qgn_ref[...], qgr_ref[...],
                                    cq_ref[...], sq_ref[...], sw, scale)
        qn, qr = qn.astype(BF16), qr.astype(BF16)

        def step(j, carry):
            m, l, acc = carry
            rows = pl.ds(pl.multiple_of(j * TQ, TQ), TQ)
            s = _dot(qn, kn_s[rows, :], NT) + _dot(qr, kr_s[rows, :], NT)
            s = jnp.where(_chunk_mask(i, j, TQ), s, MASK_VALUE)
            m_new = jnp.maximum(m, jnp.max(s, axis=-1, keepdims=True))
            alpha = jnp.exp(m - m_new)
            p = jnp.exp(s - m_new)
            l = alpha * l + jnp.sum(p, axis=-1, keepdims=True)
            acc = alpha * acc + _dot(p.astype(BF16), v_s[rows, :], NN)
            return m_new, l, acc

        m, l, acc = lax.fori_loop(0, i + 1, step, (jnp.full((TQ, 1), MASK_VALUE, F32), jnp.zeros((TQ, 1), F32),
                                                   jnp.zeros((TQ, MLA_V), F32)))
        o_ref[...] = (acc / l).astype(BF16)
        lse_ref[...] = m + jnp.log(l)

    def vec(n):
        return _bs((1, n), lambda b, h, i: (0, 0))

    return pl.pallas_call(
        body, name="mla_fwd", grid=(B, MLA_H, nQ),
        in_specs=[_bs((None, TQ, MLA_QK), lambda b, h, i: (h, b * nQ + i, 0)),
                  _bs((None, S, MLA_NOPE + MLA_V), lambda b, h, i: (h, b, 0)),
                  _bs((S, MLA_ROPE), lambda b, h, i: (b, 0)),
                  vec(MLA_NOPE), vec(MLA_ROPE), vec(MLA_NOPE), vec(MLA_ROPE),
                  _bs((TQ, MLA_ROPE), lambda b, h, i: (i, 0)), _bs((TQ, MLA_ROPE), lambda b, h, i: (i, 0)),
                  _bs((S, MLA_ROPE), lambda b, h, i: (0, 0)), _bs((S, MLA_ROPE), lambda b, h, i: (0, 0)),
                  _bs((MLA_ROPE, MLA_ROPE), lambda b, h, i: (0, 0))],
        out_specs=[_bs((None, TQ, MLA_V), lambda b, h, i: (h, b * nQ + i, 0)),
                   _bs((None, TQ, 1), lambda b, h, i: (h, b * nQ + i, 0))],
        out_shape=[jax.ShapeDtypeStruct((MLA_H, T, MLA_V), BF16), jax.ShapeDtypeStruct((MLA_H, T, 1), F32)],
        scratch_shapes=[pltpu.VMEM((S, MLA_NOPE), BF16), pltpu.VMEM((S, MLA_ROPE), BF16), pltpu.VMEM((S, MLA_V), BF16)],
        compiler_params=_cp(("parallel", "parallel", "arbitrary")))(
            q_raw, kv, kr, qgn, qgr, kgn, kgr, cos, sin, cos, sin, swap)


def _mla_bwd(q_raw, kv, kr, o, lse, do, gains, tabs, B, S):
    T = B * S
    TQ = ATT_TQ
    nQ = S // TQ
    qgn, qgr, kgn, kgr = gains
    cos, sin, swap = tabs
    scale = MLA_QK ** -0.5

    def body(q_ref, kv_ref, kr_ref, o_ref, lse_ref, do_ref, qgn_ref, qgr_ref, kgn_ref, kgr_ref, c_ref, s_ref, sw_ref,
             dq_ref, dkv_ref, dkr_ref, dqgn_ref, dqgr_ref, dkgn_ref, dkgr_ref,
             qn_s, qr_s, kn_s, kr_s, v_s, dl_s, dqn_s, dqr_s, dkn_s, dkr_s):
        b, h = pl.program_id(0), pl.program_id(1)
        sw = sw_ref[...]

        def blk(t):
            return pl.ds(pl.multiple_of(t * TQ, TQ), TQ)

        def prep(t, _):
            rows = blk(t)
            cs, sn = c_ref[rows, :], s_ref[rows, :]
            qn, qr, _ = _head_norm_rope(q_ref[rows, :MLA_NOPE], q_ref[rows, MLA_NOPE:], qgn_ref[...], qgr_ref[...],
                                        cs, sn, sw, scale)
            qn_s[rows, :] = qn.astype(BF16)
            qr_s[rows, :] = qr.astype(BF16)
            kn, krr, _ = _head_norm_rope(kv_ref[rows, :MLA_NOPE], kr_ref[rows, :], kgn_ref[...], kgr_ref[...],
                                         cs, sn, sw, 1.0)
            kn_s[rows, :] = kn.astype(BF16)
            kr_s[rows, :] = krr.astype(BF16)
            v_s[rows, :] = kv_ref[rows, MLA_NOPE:].astype(BF16)
            dl_s[rows, :] = jnp.sum(do_ref[rows, :].astype(F32) * o_ref[rows, :].astype(F32), axis=-1, keepdims=True)
            dqn_s[rows, :] = jnp.zeros((TQ, MLA_NOPE), F32)
            dqr_s[rows, :] = jnp.zeros((TQ, MLA_ROPE), F32)
            return 0

        lax.fori_loop(0, nQ, prep, 0)

        def kv_block(j, _):
            krows = blk(j)
            knj, krj, vj = kn_s[krows, :], kr_s[krows, :], v_s[krows, :]

            def q_block(i, carry):
                dkn, dkr, dv = carry
                rows = blk(i)
                qni, qri, doi = qn_s[rows, :], qr_s[rows, :], do_ref[rows, :]
                s = _dot(qni, knj, NT) + _dot(qri, krj, NT)
                s = jnp.where(_chunk_mask(i, j, TQ), s, MASK_VALUE)
                p = jnp.exp(s - lse_ref[rows, :])
                dp = _dot(doi, vj, NT)
                ds = (p * (dp - dl_s[rows, :])).astype(BF16)
                dv = dv + _dot(p.astype(BF16), doi, TN)
                dqn_s[rows, :] += _dot(ds, knj, NN)
                dqr_s[rows, :] += _dot(ds, krj, NN)
                dkn = dkn + _dot(ds, qni, TN)
                dkr = dkr + _dot(ds, qri, TN)
                return dkn, dkr, dv

            dkn, dkr, dv = lax.fori_loop(j, nQ, q_block, (jnp.zeros((TQ, MLA_NOPE), F32), jnp.zeros((TQ, MLA_ROPE), F32),
                                                          jnp.zeros((TQ, MLA_V), F32)))
            dkn_s[krows, :] = dkn
            dkr_s[krows, :] = dkr
            dkv_ref[krows, MLA_NOPE:] = dv.astype(BF16)
            return 0

        lax.fori_loop(0, nQ, kv_block, 0)

        def post(t, carry):
            rows = blk(t)
            cs, sn = c_ref[rows, :], s_ref[rows, :]
            dqn, dqr, a0, a1 = _head_norm_rope_bwd(dqn_s[rows, :], dqr_s[rows, :], q_ref[rows, :MLA_NOPE],
                                                   q_ref[rows, MLA_NOPE:], qgn_ref[...], qgr_ref[...], cs, sn, sw, scale)
            dq_ref[rows, :MLA_NOPE] = dqn.astype(BF16)
            dq_ref[rows, MLA_NOPE:] = dqr.astype(BF16)
            dkn, dkr, a2, a3 = _head_norm_rope_bwd(dkn_s[rows, :], dkr_s[rows, :], kv_ref[rows, :MLA_NOPE],
                                                   kr_ref[rows, :], kgn_ref[...], kgr_ref[...], cs, sn, sw, 1.0)
            dkv_ref[rows, :MLA_NOPE] = dkn.astype(BF16)

            @pl.when(h == 0)
            def _():
                dkr_ref[rows, :] = dkr

            @pl.when(h > 0)
            def _():
                dkr_ref[rows, :] += dkr

            return carry[0] + a0, carry[1] + a1, carry[2] + a2, carry[3] + a3

        gqn, gqr, gkn, gkr = lax.fori_loop(
            0, nQ, post, (jnp.zeros((1, MLA_NOPE), F32), jnp.zeros((1, MLA_ROPE), F32), jnp.zeros((1, MLA_NOPE), F32),
                          jnp.zeros((1, MLA_ROPE), F32)))
        first = (b == 0) & (h == 0)

        @pl.when(first)
        def _():
            dqgn_ref[...] = gqn
            dqgr_ref[...] = gqr
            dkgn_ref[...] = gkn
            dkgr_ref[...] = gkr

        @pl.when(jnp.logical_not(first))
        def _():
            dqgn_ref[...] += gqn
            dqgr_ref[...] += gqr
            dkgn_ref[...] += gkn
            dkgr_ref[...] += gkr

    def vec(n):
        return _bs((1, n), lambda b, h: (0, 0))

    def head(n):
        return _bs((None, S, n), lambda b, h: (h, b, 0))

    tab = _bs((S, MLA_ROPE), lambda b, h: (0, 0))
    return pl.pallas_call(
        body, name="mla_bwd", grid=(B, MLA_H),
        in_specs=[head(MLA_QK), head(MLA_NOPE + MLA_V), _bs((S, MLA_ROPE), lambda b, h: (b, 0)), head(MLA_V), head(1),
                  head(MLA_V), vec(MLA_NOPE), vec(MLA_ROPE), vec(MLA_NOPE), vec(MLA_ROPE), tab, tab,
                  _bs((MLA_ROPE, MLA_ROPE), lambda b, h: (0, 0))],
        out_specs=[head(MLA_QK), head(MLA_NOPE + MLA_V), _bs((S, MLA_ROPE), lambda b, h: (b, 0)),
                   vec(MLA_NOPE), vec(MLA_ROPE), vec(MLA_NOPE), vec(MLA_ROPE)],
        out_shape=[jax.ShapeDtypeStruct((MLA_H, T, MLA_QK), BF16), jax.ShapeDtypeStruct((MLA_H, T, MLA_NOPE + MLA_V), BF16),
                   jax.ShapeDtypeStruct((T, MLA_ROPE), F32), jax.ShapeDtypeStruct((1, MLA_NOPE), F32),
                   jax.ShapeDtypeStruct((1, MLA_ROPE), F32), jax.ShapeDtypeStruct((1, MLA_NOPE), F32),
                   jax.ShapeDtypeStruct((1, MLA_ROPE), F32)],
        scratch_shapes=[pltpu.VMEM((S, MLA_NOPE), BF16), pltpu.VMEM((S, MLA_ROPE), BF16), pltpu.VMEM((S, MLA_NOPE), BF16),
                        pltpu.VMEM((S, MLA_ROPE), BF16), pltpu.VMEM((S, MLA_V), BF16), pltpu.VMEM((S, 1), F32),
                        pltpu.VMEM((S, MLA_NOPE), F32), pltpu.VMEM((S, MLA_ROPE), F32), pltpu.VMEM((S, MLA_NOPE), F32),
                        pltpu.VMEM((S, MLA_ROPE), F32)],
        compiler_params=_cp(("arbitrary", "arbitrary")))(q_raw, kv, kr, o, lse, do, qgn, qgr, kgn, kgr, cos, sin, swap)


def _adamw(name, recv, w, m, v, tr=None):
    n, R, C = recv.shape
    tr = R if tr is None else tr
    c1 = 1.0 - ADAM_B1 ** ADAM_STEP
    c2 = 1.0 - ADAM_B2 ** ADAM_STEP

    def body(r_ref, w_ref, m_ref, v_ref, g_ref, d_ref, nm_ref, nv_ref):
        g = r_ref[0].astype(F32)
        for k in range(1, n):
            g = g + r_ref[k].astype(F32)
        mm = ADAM_B1 * m_ref[...] + (1.0 - ADAM_B1) * g
        vv = ADAM_B2 * v_ref[...] + (1.0 - ADAM_B2) * (g * g)
        g_ref[...] = g
        nm_ref[...] = mm
        nv_ref[...] = vv
        d_ref[...] = -ADAM_LR * ((mm / c1) / (jnp.sqrt(vv / c2) + ADAM_EPS) + ADAM_WD * w_ref[...])

    blk = _bs((tr, C), lambda i: (i, 0))
    return pl.pallas_call(
        body, name=name, grid=(R // tr,), in_specs=[_bs((n, tr, C), lambda i: (0, i, 0)), blk, blk, blk],
        out_specs=[blk] * 4, out_shape=[jax.ShapeDtypeStruct((R, C), F32)] * 4,
        compiler_params=_cp(("parallel",)))(recv, w, m, v)


def _sum8(name, a):
    n, R, C = a.shape

    def body(a_ref, o_ref):
        s = a_ref[0]
        for k in range(1, n):
            s = s + a_ref[k]
        o_ref[...] = s

    return pl.pallas_call(body, name=name, out_shape=jax.ShapeDtypeStruct((R, C), a.dtype))(a)


def _place():
    return lax.axis_index("x"), lax.axis_index("y"), lax.axis_index("c")


def _idx(d):
    return 4 * d[0] + 2 * d[1] + d[2]


ANY = pl.BlockSpec(memory_space=pl.ANY)


def _all_gather(name, arrs):
    n = len(arrs)

    def body(*refs):
        ins, outs = refs[:n], refs[n:2 * n]
        send, recv, loc = refs[2 * n:]
        x, y, c = _place()
        me, sib = (x, y, c), (x, y, 1 - c)
        chips = [(1 - x, y), (x, 1 - y), (1 - x, 1 - y)]

        def cp(a, k, block, to, src=None):
            dst = outs[a].at[_idx(block)]
            return pltpu.make_async_remote_copy(src_ref=dst if src is None else src, dst_ref=dst, send_sem=send.at[a, k],
                                                recv_sem=recv.at[a, k], device_id=to, device_id_type=MESH)

        mine = [pltpu.make_async_copy(ins[a], outs[a].at[_idx(me)], loc.at[a]) for a in range(n)]
        for m_ in mine:
            m_.start()
        first = []
        for a in range(n):
            first.append(cp(a, 0, me, sib, src=ins[a]))
            first += [cp(a, 1 + j, me, (*chip, c), src=ins[a]) for j, chip in enumerate(chips)]
        for f in first:
            f.start()
        passed = []
        for j, chip in enumerate(chips):
            for a in range(n):
                cp(a, 1 + j, (*chip, c), me).wait_recv()
                p = cp(a, 4 + j, (*chip, c), sib)
                p.start()
                passed.append(p)
        for a in range(n):
            cp(a, 0, sib, me).wait_recv()
        for j, chip in enumerate(chips):
            for a in range(n):
                cp(a, 4 + j, (*chip, 1 - c), me).wait_recv()
        for f in first + passed:
            f.wait_send()
        for m_ in mine:
            m_.wait()

    return pl.pallas_call(
        body, name=name, in_specs=[ANY] * n, out_specs=[ANY] * n,
        out_shape=[jax.ShapeDtypeStruct((NDEV,) + a.shape, a.dtype) for a in arrs],
        scratch_shapes=[pltpu.SemaphoreType.DMA((n, 7)), pltpu.SemaphoreType.DMA((n, 7)), pltpu.SemaphoreType.DMA((n,))],
    )(*arrs)


def _all_to_all(name, arrs):
    n = len(arrs)

    def body(*refs):
        ins, outs = refs[:n], refs[n:2 * n]
        send, recv, loc = refs[2 * n:]
        x, y, c = _place()
        me = _idx((x, y, c))
        mine = [pltpu.make_async_copy(ins[a].at[me], outs[a].at[me], loc.at[a]) for a in range(n)]
        for m_ in mine:
            m_.start()
        copies = []
        for k in range(1, NDEV):
            peer = (x ^ (k >> 2), y ^ ((k >> 1) & 1), c ^ (k & 1))
            for a in range(n):
                copies.append(pltpu.make_async_remote_copy(
                    src_ref=ins[a].at[_idx(peer)], dst_ref=outs[a].at[me], send_sem=send.at[a, k - 1],
                    recv_sem=recv.at[a, k - 1], device_id=peer, device_id_type=MESH))
        for cp in copies:
            cp.start()
        for cp in copies:
            cp.wait()
        for m_ in mine:
            m_.wait()

    return pl.pallas_call(
        body, name=name, in_specs=[ANY] * n, out_specs=[ANY] * n,
        out_shape=[jax.ShapeDtypeStruct(a.shape, a.dtype) for a in arrs],
        scratch_shapes=[pltpu.SemaphoreType.DMA((n, 7)), pltpu.SemaphoreType.DMA((n, 7)), pltpu.SemaphoreType.DMA((n,))],
    )(*arrs)


def _sds(shape, dt):
    return jax.ShapeDtypeStruct(shape, dt)


def _proj_shared(name, h, w, out_dtype, tm=1024):
    T, K = h.shape
    J, _, n = w.shape
    return _mm(name, h, w, grid=(T // tm, J), a_spec=_bs((tm, K), lambda m, j: (m, 0)),
               b_spec=_bs((None, K, n), lambda m, j: (j, 0, 0)), o_spec=_bs((None, tm, n), lambda m, j: (j, m, 0)),
               out_shape=_sds((J, T, n), out_dtype), dims=NN)


def _proj_shared_dx(name, d, w, tm=1024):
    J, T, n = d.shape
    K = w.shape[1]
    return _mm(name, d, w, grid=(T // tm, J), a_spec=_bs((None, tm, n), lambda m, k: (k, m, 0)),
               b_spec=_bs((None, K, n), lambda m, k: (k, 0, 0)), o_spec=_bs((tm, K), lambda m, k: (m, 0)),
               out_shape=_sds((T, K), F32), dims=NT, kax=1, acc_shape=(tm, K))


def _proj_shared_dw(name, h, d, tt=512):
    T, K = h.shape
    J, _, n = d.shape
    return _mm(name, h, d, grid=(J, T // tt), a_spec=_bs((tt, K), lambda j, t: (t, 0)),
               b_spec=_bs((None, tt, n), lambda j, t: (j, t, 0)), o_spec=_bs((None, K, n), lambda j, t: (j, 0, 0)),
               out_shape=_sds((J, K, n), BF16), dims=TN, kax=1, acc_shape=(K, n))


def _out_proj(name, a, w, res, tm=512):
    J, T, k = a.shape
    N = w.shape[2]
    return _mm(name, a, w, grid=(T // tm,), a_spec=_bs((J, tm, k), lambda m: (0, m, 0)),
               b_spec=_bs((J, k, N), lambda m: (0, 0, 0)), o_spec=_bs((tm, N), lambda m: (m, 0)),
               out_shape=_sds((T, N), F32), dims=NN, res=res, res_spec=_bs((tm, N), lambda m: (m, 0)), jb=J)


def _out_proj_dx(name, dx, w, tm=512):
    T, N = dx.shape
    J, k, _ = w.shape
    return _mm(name, dx, w, grid=(T // tm, J), a_spec=_bs((tm, N), lambda m, j: (m, 0)),
               b_spec=_bs((None, k, N), lambda m, j: (j, 0, 0)), o_spec=_bs((None, tm, k), lambda m, j: (j, m, 0)),
               out_shape=_sds((J, T, k), BF16), dims=NT)


def _out_proj_dw(name, a, dx, tt=512):
    J, T, k = a.shape
    N = dx.shape[1]
    return _mm(name, a, dx, grid=(J, T // tt), a_spec=_bs((None, tt, k), lambda j, t: (j, t, 0)),
               b_spec=_bs((tt, N), lambda j, t: (t, 0)), o_spec=_bs((None, k, N), lambda j, t: (j, 0, 0)),
               out_shape=_sds((J, k, N), BF16), dims=TN, kax=1, acc_shape=(k, N))


def _bf16(x):
    return x.astype(BF16)


def _ffn_fwd(i, x, norm_g, w_in, cw, cb, w_out, B, S):
    h = _rms_fwd(f"ffn{i}_norm", x, norm_g)
    u = _proj_shared(f"ffn{i}_in", h, w_in, BF16)
    u4 = u.reshape(2, 4, u.shape[1], FSH)
    gt = _convffn_fwd(f"ffn{i}_gate", u4, cw, cb, B, S)
    y = _out_proj(f"ffn{i}_out", gt, w_out, x)
    return y, (x, h, u4, gt)


def _ffn_bwd(i, dy, saved, norm_g, w_in, cw, cb, w_out, B, S):
    x, h, u4, gt = saved
    dgt = _out_proj_dx(f"ffn{i}_out_dx", _bf16(dy), w_out)
    dw_out = _out_proj_dw(f"ffn{i}_out_dw", gt, _bf16(dy))
    du4, dcw, dcb = _convffn_bwd(f"ffn{i}_gate_bwd", u4, cw, cb, dgt, B, S)
    du = du4.reshape(NDEV, du4.shape[2], FSH)
    dh = _proj_shared_dx(f"ffn{i}_in_dx", du, w_in)
    dw_in = _proj_shared_dw(f"ffn{i}_in_dw", h, du)
    dx, dgn = _rms_bwd(f"ffn{i}_norm_bwd", x, norm_g, dh, dres=dy)
    return dx, dict(w_in=dw_in, w_out=dw_out, norm=dgn, cw=dcw, cb=dcb)


def kernel(x, ret_norm, ret_w_in, ret_gn, ret_w_out, mla_norm, mla_w_in, mla_q_norm, mla_w_qb, mla_kv_norm, mla_w_kvb, mla_q_head_norm, mla_k_head_norm, mla_w_out, ffn_norm, ffn_w_in, ffn_conv_w, ffn_conv_b, ffn_w_out, loss_target, m_ret_norm, m_ret_w_in, m_ret_gn, m_ret_w_out, m_mla_norm, m_mla_w_in, m_mla_q_norm, m_mla_w_qb, m_mla_kv_norm, m_mla_w_kvb, m_mla_q_head_norm, m_mla_k_head_norm, m_mla_w_out, m_ffn_norm, m_ffn_w_in, m_ffn_conv_w, m_ffn_conv_b, m_ffn_w_out, v_ret_norm, v_ret_w_in, v_ret_gn, v_ret_w_out, v_mla_norm, v_mla_w_in, v_mla_q_norm, v_mla_w_qb, v_mla_kv_norm, v_mla_w_kvb, v_mla_q_head_norm, v_mla_k_head_norm, v_mla_w_out, v_ffn_norm, v_ffn_w_in, v_ffn_conv_w, v_ffn_conv_b, v_ffn_w_out):
    B, S, D = x.shape
    T = B * S
    w = dict(ret_norm=ret_norm, ret_w_in=ret_w_in, ret_gn=ret_gn, ret_w_out=ret_w_out, mla_norm=mla_norm,
             mla_w_in=mla_w_in, mla_q_norm=mla_q_norm, mla_w_qb=mla_w_qb, mla_kv_norm=mla_kv_norm, mla_w_kvb=mla_w_kvb,
             mla_q_head_norm=mla_q_head_norm, mla_k_head_norm=mla_k_head_norm, mla_w_out=mla_w_out, ffn_norm=ffn_norm,
             ffn_w_in=ffn_w_in, ffn_conv_w=ffn_conv_w, ffn_conv_b=ffn_conv_b, ffn_w_out=ffn_w_out)
    mom = dict(ret_norm=m_ret_norm, ret_w_in=m_ret_w_in, ret_gn=m_ret_gn, ret_w_out=m_ret_w_out, mla_norm=m_mla_norm,
               mla_w_in=m_mla_w_in, mla_q_norm=m_mla_q_norm, mla_w_qb=m_mla_w_qb, mla_kv_norm=m_mla_kv_norm,
               mla_w_kvb=m_mla_w_kvb, mla_q_head_norm=m_mla_q_head_norm, mla_k_head_norm=m_mla_k_head_norm,
               mla_w_out=m_mla_w_out, ffn_norm=m_ffn_norm, ffn_w_in=m_ffn_w_in, ffn_conv_w=m_ffn_conv_w,
               ffn_conv_b=m_ffn_conv_b, ffn_w_out=m_ffn_w_out)
    var = dict(ret_norm=v_ret_norm, ret_w_in=v_ret_w_in, ret_gn=v_ret_gn, ret_w_out=v_ret_w_out, mla_norm=v_mla_norm,
               mla_w_in=v_mla_w_in, mla_q_norm=v_mla_q_norm, mla_w_qb=v_mla_w_qb, mla_kv_norm=v_mla_kv_norm,
               mla_w_kvb=v_mla_w_kvb, mla_q_head_norm=v_mla_q_head_norm, mla_k_head_norm=v_mla_k_head_norm,
               mla_w_out=v_mla_w_out, ffn_norm=v_ffn_norm, ffn_w_in=v_ffn_w_in, ffn_conv_w=v_ffn_conv_w,
               ffn_conv_b=v_ffn_conv_b, ffn_w_out=v_ffn_w_out)
    BIG = ["ret_w_in", "ret_w_out", "mla_w_in", "mla_w_qb", "mla_w_kvb", "mla_w_out", "ffn_w_in", "ffn_w_out"]
    REPL = ["ret_norm", "ffn_norm", "mla_q_head_norm", "mla_k_head_norm", "ffn_conv_b"]
    SHARDED_SMALL = ["ffn_conv_w", "ret_gn", "mla_norm", "mla_q_norm", "mla_kv_norm"]
    dev = _idx(_place())

    big2d = {k: w[k].reshape((-1,) + w[k].shape[-1:]) if k not in ("ffn_w_in", "ffn_w_out") else w[k] for k in BIG}
    small_vec = jnp.concatenate([w[k].reshape(-1) for k in SHARDED_SMALL])
    n_small = small_vec.shape[0]
    small_vec = jnp.pad(small_vec, (0, 3072 - n_small)).reshape(24, 128)
    gathered = _all_gather("gather_weights", [_bf16(big2d[k]) for k in BIG] + [small_vec])
    G = dict(zip(BIG, gathered[:-1]))
    sg = gathered[-1].reshape(NDEV, 3072)
    o0 = 0
    conv_w_full = sg[:, o0:o0 + 2112].reshape(NDEV, 2, 3, 352).transpose(1, 2, 0, 3).reshape(2, 3, FFN)
    o0 += 2112
    ret_gn_full = sg[:, o0:o0 + 256].reshape(NDEV, RET_H, 64).transpose(1, 0, 2).reshape(RET_H, 1, RET_DV)
    o0 += 256
    mla_norm_full = sg[:, o0:o0 + 128].reshape(1, D)
    o0 += 128
    q_norm_full = sg[:, o0:o0 + 48].reshape(1, MLA_QR)
    o0 += 48
    kv_norm_full = sg[:, o0:o0 + 32].reshape(1, MLA_KVR)

    Wret_in = G["ret_w_in"]
    Wret_out = G["ret_w_out"].reshape(RET_H, RET_DV, D)
    Wmla_in = G["mla_w_in"].reshape(D, MLA_QR + MLA_KVR + MLA_ROPE)
    Wq, Wkv, Wkr = Wmla_in[:, :MLA_QR], Wmla_in[:, MLA_QR:MLA_QR + MLA_KVR], Wmla_in[:, MLA_QR + MLA_KVR:]
    Wqb, Wkvb, Wmla_out = G["mla_w_qb"], G["mla_w_kvb"], G["mla_w_out"]
    Wffn_in = [G["ffn_w_in"][:, i] for i in range(2)]
    Wffn_out = [G["ffn_w_out"][:, i].reshape(4, FSH, D) for i in range(2)]
    cw = [conv_w_full[i].reshape(3, 4, FSH).transpose(1, 0, 2) for i in range(2)]
    cb = [ffn_conv_b[i].reshape(4, 1, FSH) for i in range(2)]
    fnorm = [ffn_norm[i].reshape(1, D) for i in range(2)]
    rtabs = _ret_tables(S)
    mtabs = _mla_tables(S)
    qh, kh = mla_q_head_norm.reshape(1, MLA_QK), mla_k_head_norm.reshape(1, MLA_QK)
    gains = (qh[:, :MLA_NOPE], qh[:, MLA_NOPE:], kh[:, :MLA_NOPE], kh[:, MLA_NOPE:])

    x0 = x.reshape(T, D)
    tgt = loss_target.reshape(T, D)
    h0 = _rms_fwd("ret_norm", x0, ret_norm.reshape(1, D))
    proj = _mm("ret_in", h0, Wret_in, grid=(T // 1024, NDEV), a_spec=_bs((1024, D), lambda m, j: (m, 0)),
               b_spec=_bs((None, D, 768), lambda m, j: (j, 0, 0)), o_spec=_bs((1024, 768), lambda m, j: (m, j)),
               out_shape=_sds((T, 6144), BF16), dims=NN)
    o_raw, rgt, states = _ret_fwd(proj, rtabs, ret_gn_full, B, S)
    x1 = _mm("ret_out", rgt, Wret_out.reshape(RET_H * RET_DV, D), grid=(T // 512,),
             a_spec=_bs((512, RET_H * RET_DV), lambda m: (m, 0)), b_spec=_bs((RET_H * RET_DV, D), lambda m: (0, 0)),
             o_spec=_bs((512, D), lambda m: (m, 0)), out_shape=_sds((T, D), F32), dims=NN, res=x0,
             res_spec=_bs((512, D), lambda m: (m, 0)))
    x2, ffn0_saved = _ffn_fwd(0, x1, fnorm[0], Wffn_in[0], cw[0], cb[0], Wffn_out[0], B, S)

    h2 = _rms_fwd("mla_norm", x2, mla_norm_full)

    def small_proj(name, wmat):
        n = wmat.shape[1]
        return _mm(name, h2, wmat, grid=(T // 512,), a_spec=_bs((512, D), lambda m: (m, 0)),
                   b_spec=_bs((D, n), lambda m: (0, 0)), o_spec=_bs((512, n), lambda m: (m, 0)),
                   out_shape=_sds((T, n), F32), dims=NN)

    c_q, c_kv, k_rope = small_proj("mla_in_q", Wq), small_proj("mla_in_kv", Wkv), small_proj("mla_in_kr", Wkr)
    cqn = _rms_fwd("mla_q_norm", c_q, q_norm_full)
    ckvn = _rms_fwd("mla_kv_norm", c_kv, kv_norm_full)
    q_raw = _proj_shared("mla_qb", cqn, Wqb, F32)
    kvh = _proj_shared("mla_kvb", ckvn, Wkvb, F32)
    att, lse = _mla_fwd(q_raw, kvh, k_rope, gains, mtabs, B, S)
    x3 = _out_proj("mla_out", att, Wmla_out, x2)
    y, ffn1_saved = _ffn_fwd(1, x3, fnorm[1], Wffn_in[1], cw[1], cb[1], Wffn_out[1], B, S)

    dy, colsq = _loss(y, tgt)
    loss = lax.psum(0.5 * jnp.sum(colsq) / D, ("x", "y", "c"))

    dx3, gf1 = _ffn_bwd(1, dy, ffn1_saved, fnorm[1], Wffn_in[1], cw[1], cb[1], Wffn_out[1], B, S)
    dx3b = _bf16(dx3)
    datt = _out_proj_dx("mla_out_dx", dx3b, Wmla_out)
    dWmla_out = _out_proj_dw("mla_out_dw", att, dx3b)
    dq_raw, dkvh, dkr, dqgn, dqgr, dkgn, dkgr = _mla_bwd(q_raw, kvh, k_rope, att, lse, datt, gains, mtabs, B, S)
    dcqn = _proj_shared_dx("mla_qb_dx", dq_raw, Wqb)
    dWqb = _proj_shared_dw("mla_qb_dw", cqn, dq_raw)
    dckvn = _proj_shared_dx("mla_kvb_dx", dkvh, Wkvb)
    dWkvb = _proj_shared_dw("mla_kvb_dw", ckvn, dkvh)
    dcq, dg_qn = _rms_bwd("mla_q_norm_bwd", c_q, q_norm_full, dcqn)
    dckv, dg_kvn = _rms_bwd("mla_kv_norm_bwd", c_kv, kv_norm_full, dckvn)
    dproj2 = _bf16(jnp.concatenate([dcq, dckv, dkr], axis=-1))
    dh2 = _mm("mla_in_dx", dproj2, Wmla_in, grid=(T // 512,), a_spec=_bs((512, 704), lambda m: (m, 0)),
              b_spec=_bs((D, 704), lambda m: (0, 0)), o_spec=_bs((512, D), lambda m: (m, 0)),
              out_shape=_sds((T, D), F32), dims=NT)
    dWmla_in = _mm("mla_in_dw", h2, dproj2, grid=(T // 512,), a_spec=_bs((512, D), lambda t: (t, 0)),
                   b_spec=_bs((512, 704), lambda t: (t, 0)), o_spec=_bs((D, 704), lambda t: (0, 0)),
                   out_shape=_sds((D, 704), BF16), dims=TN, kax=0, acc_shape=(D, 704))
    dx2, dg_mla_norm = _rms_bwd("mla_norm_bwd", x2, mla_norm_full, dh2, dres=dx3)

    dx1, gf0 = _ffn_bwd(0, dx2, ffn0_saved, fnorm[0], Wffn_in[0], cw[0], cb[0], Wffn_out[0], B, S)
    dx1b = _bf16(dx1)
    drgt = _mm("ret_out_dx", dx1b, Wret_out.reshape(RET_H * RET_DV, D), grid=(T // 512, RET_H),
               a_spec=_bs((512, D), lambda m, j: (m, 0)), b_spec=_bs((RET_DV, D), lambda m, j: (j, 0)),
               o_spec=_bs((512, RET_DV), lambda m, j: (m, j)), out_shape=_sds((T, RET_H * RET_DV), BF16), dims=NT)
    dWret_out = _mm("ret_out_dw", rgt, dx1b, grid=(RET_H, T // 512), a_spec=_bs((512, RET_DV), lambda j, t: (t, j)),
                    b_spec=_bs((512, D), lambda j, t: (t, 0)), o_spec=_bs((RET_DV, D), lambda j, t: (j, 0)),
                    out_shape=_sds((RET_H * RET_DV, D), BF16), dims=TN, kax=1, acc_shape=(RET_DV, D))
    dq, dk, dv, dg, dgn_ret = _ret_bwd(proj, o_raw, states, drgt, rtabs, ret_gn_full, B, S)
    dproj = jnp.concatenate([dq, dk, dv, dg], axis=-1)
    dh0 = _mm("ret_in_dx", dproj, Wret_in, grid=(T // 1024, NDEV), a_spec=_bs((1024, 768), lambda m, k: (m, k)),
              b_spec=_bs((None, D, 768), lambda m, k: (k, 0, 0)), o_spec=_bs((1024, D), lambda m, k: (m, 0)),
              out_shape=_sds((T, D), F32), dims=NT, kax=1, acc_shape=(1024, D))
    dWret_in = _mm("ret_in_dw", h0, dproj, grid=(NDEV, T // 512), a_spec=_bs((512, D), lambda j, t: (t, 0)),
                   b_spec=_bs((512, 768), lambda j, t: (t, j)), o_spec=_bs((None, D, 768), lambda j, t: (j, 0, 0)),
                   out_shape=_sds((NDEV, D, 768), BF16), dims=TN, kax=1, acc_shape=(D, 768))
    dx0, dg_ret_norm = _rms_bwd("ret_norm_bwd", x0, ret_norm.reshape(1, D), dh0, dres=dx1)
    grad_x = dx0.reshape(B, S, D)

    dW = dict(
        ret_w_in=dWret_in,
        ret_w_out=dWret_out.reshape(NDEV, 256, D),
        mla_w_in=dWmla_in.reshape(NDEV, 128, 704),
        mla_w_qb=dWqb, mla_w_kvb=dWkvb, mla_w_out=dWmla_out,
        ffn_w_in=jnp.stack([gf0["w_in"], gf1["w_in"]], axis=1).reshape(NDEV, 2 * D, FSH),
        ffn_w_out=jnp.stack([gf0["w_out"].reshape(NDEV, 352, D), gf1["w_out"].reshape(NDEV, 352, D)],
                            axis=1).reshape(NDEV, 2 * 352, D),
    )
    dconv_w = jnp.stack([g_["cw"].transpose(1, 0, 2).reshape(3, FFN) for g_ in (gf0, gf1)])
    dconv_b = jnp.stack([g_["cb"].reshape(FFN) for g_ in (gf0, gf1)])
    small_parts = [dg_ret_norm, gf0["norm"], gf1["norm"], dg_mla_norm, dg_qn, dg_kvn, dqgn, dqgr, dkgn, dkgr, dgn_ret,
                   dconv_w, dconv_b]
    small_g = jnp.concatenate([p.reshape(-1) for p in small_parts]).reshape(232, 128)
    received = _all_to_all("exchange_grads", [dW[k] for k in BIG])
    small_all = _all_gather("gather_small_grads", [small_g])[0]
    sred = _sum8("sum_small_grads", small_all).reshape(-1)

    def take(n):
        nonlocal off
        out = sred[off:off + n]
        off += n
        return out

    off = 0
    g_small = dict(ret_norm=take(D).reshape(1, D), ffn_norm=take(2 * D).reshape(2, D), mla_norm=take(D),
                   mla_q_norm=take(MLA_QR), mla_kv_norm=take(MLA_KVR))
    g_small["mla_q_head_norm"] = take(MLA_QK).reshape(1, MLA_QK)
    g_small["mla_k_head_norm"] = take(MLA_QK).reshape(1, MLA_QK)
    g_small["ret_gn"] = take(RET_H * RET_DV).reshape(1, RET_H, RET_DV)
    g_small["ffn_conv_w"] = take(2 * 3 * FFN).reshape(2, 3, FFN)
    g_small["ffn_conv_b"] = take(2 * FFN).reshape(2, FFN)
    g_small["mla_norm"] = lax.dynamic_slice(g_small["mla_norm"], (dev * 128,), (128,)).reshape(1, 128)
    g_small["mla_q_norm"] = lax.dynamic_slice(g_small["mla_q_norm"], (dev * 48,), (48,)).reshape(1, 48)
    g_small["mla_kv_norm"] = lax.dynamic_slice(g_small["mla_kv_norm"], (dev * 32,), (32,)).reshape(1, 32)
    g_small["ret_gn"] = lax.dynamic_slice(g_small["ret_gn"], (0, 0, dev * 64), (1, RET_H, 64))
    g_small["ffn_conv_w"] = lax.dynamic_slice(g_small["ffn_conv_w"], (0, 0, dev * 352), (2, 3, 352))

    grads, delta, new_m, new_v = {}, {}, {}, {}
    for k, rc in zip(BIG, received):
        shp = w[k].shape
        R, C = rc.shape[1], rc.shape[2]
        tr = next(t for t in (256, 128, 64) if R % t == 0)
        g_, d_, m_, v_ = _adamw(f"adamw_{k}", rc, w[k].reshape(R, C), mom[k].reshape(R, C), var[k].reshape(R, C), tr=tr)
        grads[k], delta[k], new_m[k], new_v[k] = (t.reshape(shp) for t in (g_, d_, m_, v_))
    SMALL = REPL + SHARDED_SMALL

    def pack(d):
        vflat = jnp.concatenate([d[k].reshape(-1) for k in SMALL])
        return jnp.pad(vflat, (0, 96 * 128 - vflat.shape[0])).reshape(96, 128)

    ps = _adamw("adamw_small", pack(g_small)[None], pack(w), pack(mom), pack(var))
    off = 0
    for k in SMALL:
        n = w[k].size
        grads[k], delta[k], new_m[k], new_v[k] = (t.reshape(-1)[off:off + n].reshape(w[k].shape) for t in ps)
        off += n
    names = list(w)
    return (loss, grad_x, *[grads[k] for k in names], *[delta[k] for k in names], *[new_m[k] for k in names],
            *[new_v[k] for k in names])
```

```python
import functools

import jax
import jax.numpy as jnp
from jax import lax
from jax.experimental import pallas as pl
from jax.experimental.pallas import tpu as pltpu

F32, BF16 = jnp.float32, jnp.bfloat16

NDEV = 8
D_MODEL = 1024
CHUNK = 64
RMS_EPS = 1e-6
ROPE_THETA = 10000.0
RET_H, RET_DK, RET_DV = 4, 256, 512
RET_SC = 256
MLA_H, MLA_QR, MLA_KVR = 8, 384, 256
MLA_NOPE, MLA_ROPE, MLA_V = 128, 64, 128
MLA_QK = MLA_NOPE + MLA_ROPE
MASK_VALUE = -1e30
FFN = 2816
FSH = FFN * 2 // NDEV
ATT_TQ = 256
ADAM_LR, ADAM_B1, ADAM_B2, ADAM_EPS, ADAM_WD, ADAM_STEP = 0.001, 0.9, 0.999, 1e-08, 0.01, 10
MESH = pl.DeviceIdType.MESH
VMEM_LIMIT = 56 * 2 ** 20


def _cp(sem):
    return pltpu.CompilerParams(dimension_semantics=sem, vmem_limit_bytes=VMEM_LIMIT)


def _dot(a, b, dims):
    return lax.dot_general(a, b, (dims, ((), ())), preferred_element_type=F32)


NN = ((1,), (0,))
NT = ((1,), (1,))
TN = ((0,), (0,))


def _mm(name, a, b, *, grid, a_spec, b_spec, o_spec, out_shape, dims, kax=None, res=None, res_spec=None,
        jb=0, acc_shape=None):
    nk = grid[kax] if kax is not None else 1

    def body(*refs):
        if res is not None:
            a_ref, b_ref, r_ref, o_ref = refs[:4]
        else:
            a_ref, b_ref, o_ref = refs[:3]
        if jb:
            part = _dot(a_ref[0], b_ref[0], dims)
            for j in range(1, jb):
                part = part + _dot(a_ref[j], b_ref[j], dims)
        else:
            part = _dot(a_ref[...], b_ref[...], dims)

        def fin(acc):
            if res is not None:
                acc = acc + r_ref[...]
            o_ref[...] = acc.astype(o_ref.dtype)

        if nk == 1:
            fin(part)
        else:
            acc_ref = refs[-1]
            k = pl.program_id(kax)

            @pl.when(k == 0)
            def _():
                acc_ref[...] = part

            @pl.when(k > 0)
            def _():
                acc_ref[...] += part

            @pl.when(k == nk - 1)
            def _():
                fin(acc_ref[...])

    sem = tuple("arbitrary" if i == kax else "parallel" for i in range(len(grid)))
    in_specs = [a_spec, b_spec] + ([res_spec] if res is not None else [])
    args = (a, b) + ((res,) if res is not None else ())
    scratch = [pltpu.VMEM(acc_shape, F32)] if nk > 1 else []
    return pl.pallas_call(body, name=name, grid=grid, in_specs=in_specs, out_specs=o_spec, out_shape=out_shape,
                          scratch_shapes=scratch, compiler_params=_cp(sem))(*args)


def _bs(shape, fn):
    return pl.BlockSpec(shape, fn)


def _rms_fwd(name, x, g, tm=512):
    T, D = x.shape

    def body(x_ref, g_ref, o_ref):
        xf = x_ref[...]
        r = lax.rsqrt(jnp.mean(xf * xf, axis=-1, keepdims=True) + RMS_EPS)
        o_ref[...] = (xf * r * g_ref[...]).astype(o_ref.dtype)

    return pl.pallas_call(
        body, name=name, grid=(T // tm,),
        in_specs=[_bs((tm, D), lambda i: (i, 0)), _bs((1, D), lambda i: (0, 0))],
        out_specs=_bs((tm, D), lambda i: (i, 0)), out_shape=jax.ShapeDtypeStruct((T, D), BF16),
        compiler_params=_cp(("parallel",)))(x, g)


def _rms_bwd(name, x, g, dh, dres=None, tm=512):
    T, D = x.shape

    def body(*refs):
        if dres is not None:
            x_ref, g_ref, dh_ref, dres_ref, dx_ref, dg_ref = refs
        else:
            x_ref, g_ref, dh_ref, dx_ref, dg_ref = refs
        i = pl.program_id(0)
        xf = x_ref[...]
        r = lax.rsqrt(jnp.mean(xf * xf, axis=-1, keepdims=True) + RMS_EPS)
        xh = xf * r
        d = dh_ref[...].astype(F32)
        dxh = d * g_ref[...]
        dx = r * (dxh - xh * jnp.mean(dxh * xh, axis=-1, keepdims=True))
        if dres is not None:
            dx = dx + dres_ref[...]
        dx_ref[...] = dx
        part = jnp.sum(d * xh, axis=0, keepdims=True)

        @pl.when(i == 0)
        def _():
            dg_ref[...] = part

        @pl.when(i > 0)
        def _():
            dg_ref[...] += part

    row = _bs((tm, D), lambda i: (i, 0))
    vec = _bs((1, D), lambda i: (0, 0))
    in_specs = [row, vec, row] + ([row] if dres is not None else [])
    args = (x, g, dh) + ((dres,) if dres is not None else ())
    return pl.pallas_call(
        body, name=name, grid=(T // tm,), in_specs=in_specs, out_specs=[row, vec],
        out_shape=[jax.ShapeDtypeStruct((T, D), F32), jax.ShapeDtypeStruct((1, D), F32)],
        compiler_params=_cp(("arbitrary",)))(*args)


def _loss(y, tgt, tm=512):
    T, D = y.shape

    def body(y_ref, t_ref, dy_ref, s_ref):
        i = pl.program_id(0)
        e = y_ref[...] - t_ref[...]
        dy_ref[...] = e * (1.0 / D)
        part = jnp.sum(e * e, axis=0, keepdims=True)

        @pl.when(i == 0)
        def _():
            s_ref[...] = part

        @pl.when(i > 0)
        def _():
            s_ref[...] += part

    row = _bs((tm, D), lambda i: (i, 0))
    return pl.pallas_call(
        body, name="loss_head", grid=(T // tm,), in_specs=[row, row], out_specs=[row, _bs((1, D), lambda i: (0, 0))],
        out_shape=[jax.ShapeDtypeStruct((T, D), F32), jax.ShapeDtypeStruct((1, D), F32)],
        compiler_params=_cp(("arbitrary",)))(y, tgt)


def _shift_rows(t, k, row):
    return jnp.where(row >= k, pltpu.roll(t, k, 0), 0.0)


def _shift_rows_up(t, k, row, n):
    return jnp.where(row < n - k, pltpu.roll(t, n - k, 0), 0.0)


def _convffn_fwd(name, u, cw, cb, B, S):
    _, J, T, F = u.shape

    def body(u_ref, cw_ref, cb_ref, o_ref):
        a = u_ref[0].astype(F32)
        g = u_ref[1].astype(F32)
        row = lax.broadcasted_iota(jnp.int32, (S, F), 0)
        w0, w1, w2 = cw_ref[0:1, :], cw_ref[1:2, :], cw_ref[2:3, :]
        gc = _shift_rows(g, 2, row) * w0 + _shift_rows(g, 1, row) * w1 + g * w2 + cb_ref[...]
        o_ref[...] = (gc * jax.nn.sigmoid(gc) * a).astype(o_ref.dtype)

    return pl.pallas_call(
        body, name=name, grid=(J, B),
        in_specs=[_bs((2, None, S, F), lambda j, b: (0, j, b, 0)), _bs((None, 3, F), lambda j, b: (j, 0, 0)),
                  _bs((None, 1, F), lambda j, b: (j, 0, 0))],
        out_specs=_bs((None, S, F), lambda j, b: (j, b, 0)), out_shape=jax.ShapeDtypeStruct((J, T, F), BF16),
        compiler_params=_cp(("parallel", "parallel")))(u, cw, cb)


def _convffn_bwd(name, u, cw, cb, dgt, B, S):
    _, J, T, F = u.shape

    def body(u_ref, cw_ref, cb_ref, d_ref, du_ref, dcw_ref, dcb_ref):
        b = pl.program_id(1)
        a = u_ref[0].astype(F32)
        g = u_ref[1].astype(F32)
        d = d_ref[...].astype(F32)
        row = lax.broadcasted_iota(jnp.int32, (S, F), 0)
        w0, w1, w2 = cw_ref[0:1, :], cw_ref[1:2, :], cw_ref[2:3, :]
        g1, g2 = _shift_rows(g, 1, row), _shift_rows(g, 2, row)
        gc = g2 * w0 + g1 * w1 + g * w2 + cb_ref[...]
        sg = jax.nn.sigmoid(gc)
        du_ref[0] = (d * gc * sg).astype(du_ref.dtype)
        dgc = d * a * (sg * (1.0 + gc * (1.0 - sg)))
        dg = dgc * w2 + _shift_rows_up(dgc, 1, row, S) * w1 + _shift_rows_up(dgc, 2, row, S) * w0
        du_ref[1] = dg.astype(du_ref.dtype)
        parts = [jnp.sum(dgc * g2, axis=0, keepdims=True), jnp.sum(dgc * g1, axis=0, keepdims=True),
                 jnp.sum(dgc * g, axis=0, keepdims=True)]
        pb = jnp.sum(dgc, axis=0, keepdims=True)

        @pl.when(b == 0)
        def _():
            for k in range(3):
                dcw_ref[k:k + 1, :] = parts[k]
            dcb_ref[...] = pb

        @pl.when(b > 0)
        def _():
            for k in range(3):
                dcw_ref[k:k + 1, :] += parts[k]
            dcb_ref[...] += pb

    uspec = _bs((2, None, S, F), lambda j, b: (0, j, b, 0))
    return pl.pallas_call(
        body, name=name, grid=(J, B),
        in_specs=[uspec, _bs((None, 3, F), lambda j, b: (j, 0, 0)), _bs((None, 1, F), lambda j, b: (j, 0, 0)),
                  _bs((None, S, F), lambda j, b: (j, b, 0))],
        out_specs=[uspec, _bs((None, 3, F), lambda j, b: (j, 0, 0)), _bs((None, 1, F), lambda j, b: (j, 0, 0))],
        out_shape=[jax.ShapeDtypeStruct(u.shape, BF16), jax.ShapeDtypeStruct((J, 3, F), F32),
                   jax.ShapeDtypeStruct((J, 1, F), F32)],
        compiler_params=_cp(("parallel", "arbitrary")))(u, cw, cb, dgt)


def _ret_tables(S):
    half = RET_DK // 2
    inv = ROPE_THETA ** (-jnp.arange(half, dtype=F32) / half)
    ang = jnp.arange(S).astype(F32)[:, None] * inv[None, :]
    lg = jnp.log1p(-jnp.exp2(-5.0 - jnp.arange(RET_H, dtype=F32)))
    i = jnp.arange(RET_SC, dtype=F32)
    same_or_earlier = (jnp.floor(i[None, :] / CHUNK) <= jnp.floor(i[:, None] / CHUNK)).astype(F32)
    dm = jnp.exp(lg[:, None, None] * jnp.abs(i[:, None] - i[None, :])) * same_or_earlier[None]
    qd = jnp.exp(lg[:, None] * (i + 1.0))[:, :, None]
    kd = jnp.exp(lg[:, None] * (RET_SC - 1.0 - i))[:, :, None]
    cd = jnp.exp(lg * RET_SC)[:, None, None]
    return jnp.cos(ang), jnp.sin(ang), dm, qd, kd, cd


def _rope_halves(t, cs, sn):
    h = t.shape[-1] // 2
    t1, t2 = t[:, :h], t[:, h:]
    return jnp.concatenate([t1 * cs - t2 * sn, t2 * cs + t1 * sn], axis=-1)


def _unrope_halves(d, cs, sn):
    h = d.shape[-1] // 2
    d1, d2 = d[:, :h], d[:, h:]
    return jnp.concatenate([d1 * cs + d2 * sn, d2 * cs - d1 * sn], axis=-1)


def _ret_specs(nC, order):
    SC = RET_SC

    def sp(shape, fn):
        return _bs(shape, lambda *g: fn(*order(*g)))

    q = sp((SC, RET_DK), lambda b, h, c: (b * nC + c, h))
    k = sp((SC, RET_DK), lambda b, h, c: (b * nC + c, RET_H + h))
    v = sp((SC, RET_DV), lambda b, h, c: (b * nC + c, RET_H + h))
    g = sp((SC, RET_DV), lambda b, h, c: (b * nC + c, 2 * RET_H + h))
    cs = sp((SC, RET_DK // 2), lambda b, h, c: (c, 0))
    dm = sp((None, SC, SC), lambda b, h, c: (h, 0, 0))
    dv = sp((None, SC, 1), lambda b, h, c: (h, 0, 0))
    cd = sp((None, 1, 1), lambda b, h, c: (h, 0, 0))
    gn = sp((None, 1, RET_DV), lambda b, h, c: (h, 0, 0))
    wide = sp((SC, RET_DV), lambda b, h, c: (b * nC + c, h))
    narrow = sp((SC, RET_DK), lambda b, h, c: (b * nC + c, h))
    st = sp((None, None, None, RET_DK, RET_DV), lambda b, h, c: (b, h, c, 0, 0))
    return dict(q=q, k=k, v=v, g=g, cs=cs, dm=dm, dv=dv, cd=cd, gn=gn, wide=wide, narrow=narrow, st=st)


def _ret_fwd(proj, tabs, gn, B, S):
    T = B * S
    nC = S // RET_SC
    cos, sin, dm, qd, kd, cd = tabs
    s = _ret_specs(nC, lambda b, h, c: (b, h, c))

    def body(q_ref, k_ref, v_ref, g_ref, cos_ref, sin_ref, dm_ref, qd_ref, kd_ref, cd_ref, gn_ref,
             o_ref, gt_ref, st_ref, state):
        c = pl.program_id(2)

        @pl.when(c == 0)
        def _():
            state[...] = jnp.zeros_like(state)

        cs, sn = cos_ref[...], sin_ref[...]
        qf = _rope_halves(q_ref[...].astype(F32), cs, sn)
        kf = _rope_halves(k_ref[...].astype(F32), cs, sn) * (RET_DK ** -0.5)
        v = v_ref[...]
        p = _dot(qf.astype(BF16), kf.astype(BF16), NT) * dm_ref[...]
        st = state[...]
        stb = st.astype(BF16)
        st_ref[...] = stb
        o = _dot(p.astype(BF16), v, NN) + _dot((qf * qd_ref[...]).astype(BF16), stb, NN)
        state[...] = st * cd_ref[...] + _dot((kf * kd_ref[...]).astype(BF16), v, TN)
        o_ref[...] = o
        r = lax.rsqrt(jnp.mean(o * o, axis=-1, keepdims=True) + RMS_EPS)
        gf = g_ref[...].astype(F32)
        gt_ref[...] = ((o * r * gn_ref[...]) * (gf * jax.nn.sigmoid(gf))).astype(BF16)

    return pl.pallas_call(
        body, name="ret_fwd", grid=(B, RET_H, nC),
        in_specs=[s["q"], s["k"], s["v"], s["g"], s["cs"], s["cs"], s["dm"], s["dv"], s["dv"], s["cd"], s["gn"]],
        out_specs=[s["wide"], s["wide"], s["st"]],
        out_shape=[jax.ShapeDtypeStruct((T, RET_H * RET_DV), F32), jax.ShapeDtypeStruct((T, RET_H * RET_DV), BF16),
                   jax.ShapeDtypeStruct((B, RET_H, nC, RET_DK, RET_DV), BF16)],
        scratch_shapes=[pltpu.VMEM((RET_DK, RET_DV), F32)],
        compiler_params=_cp(("parallel", "parallel", "arbitrary")))(proj, proj, proj, proj, cos, sin, dm, qd, kd, cd, gn)


def _ret_bwd(proj, o_raw, states, dgt, tabs, gn, B, S):
    T = B * S
    nC = S // RET_SC
    cos, sin, dm, qd, kd, cd = tabs
    s = _ret_specs(nC, lambda h, b, c: (b, h, nC - 1 - c))

    def body(q_ref, k_ref, v_ref, g_ref, o_ref, st_ref, d_ref, cos_ref, sin_ref, dm_ref, qd_ref, kd_ref, cd_ref,
             gn_ref, dq_ref, dk_ref, dv_ref, dg_ref, dgn_ref, dstate):
        b, c = pl.program_id(1), pl.program_id(2)

        @pl.when(c == 0)
        def _():
            dstate[...] = jnp.zeros_like(dstate)

        @pl.when((b == 0) & (c == 0))
        def _():
            dgn_ref[...] = jnp.zeros_like(dgn_ref)

        cs, sn = cos_ref[...], sin_ref[...]
        qf = _rope_halves(q_ref[...].astype(F32), cs, sn)
        kf = _rope_halves(k_ref[...].astype(F32), cs, sn) * (RET_DK ** -0.5)
        v = v_ref[...]
        gnv = gn_ref[...]
        o = o_ref[...]
        r = lax.rsqrt(jnp.mean(o * o, axis=-1, keepdims=True) + RMS_EPS)
        oh = o * r
        gf = g_ref[...].astype(F32)
        sg = jax.nn.sigmoid(gf)
        d = d_ref[...].astype(F32)
        dg_ref[...] = (d * (oh * gnv) * (sg * (1.0 + gf * (1.0 - sg)))).astype(BF16)
        don = d * (gf * sg)
        dgn_ref[...] += jnp.sum(don * oh, axis=0, keepdims=True)
        doh = don * gnv
        dO = (r * (doh - oh * jnp.mean(doh * oh, axis=-1, keepdims=True))).astype(BF16)
        dmv = dm_ref[...]
        qb, kb = qf.astype(BF16), kf.astype(BF16)
        p = (_dot(qb, kb, NT) * dmv).astype(BF16)
        dp = (_dot(dO, v, NT) * dmv).astype(BF16)
        st = st_ref[...]
        dsn = dstate[...]
        dsb = dsn.astype(BF16)
        qdv, kdv = qd_ref[...], kd_ref[...]
        dq = _dot(dp, kb, NN) + _dot(dO, st, NT) * qdv
        dk = _dot(dp, qb, TN) + _dot(v, dsb, NT) * kdv
        dv = _dot(p, dO, TN) + _dot((kf * kdv).astype(BF16), dsb, NN)
        dstate[...] = dsn * cd_ref[...] + _dot((qf * qdv).astype(BF16), dO, TN)
        dq_ref[...] = _unrope_halves(dq, cs, sn).astype(BF16)
        dk_ref[...] = (_unrope_halves(dk, cs, sn) * (RET_DK ** -0.5)).astype(BF16)
        dv_ref[...] = dv.astype(BF16)

    return pl.pallas_call(
        body, name="ret_bwd", grid=(RET_H, B, nC),
        in_specs=[s["q"], s["k"], s["v"], s["g"], s["wide"], s["st"], s["wide"], s["cs"], s["cs"], s["dm"], s["dv"],
                  s["dv"], s["cd"], s["gn"]],
        out_specs=[s["narrow"], s["narrow"], s["wide"], s["wide"], s["gn"]],
        out_shape=[jax.ShapeDtypeStruct((T, RET_H * RET_DK), BF16), jax.ShapeDtypeStruct((T, RET_H * RET_DK), BF16),
                   jax.ShapeDtypeStruct((T, RET_H * RET_DV), BF16), jax.ShapeDtypeStruct((T, RET_H * RET_DV), BF16),
                   jax.ShapeDtypeStruct((RET_H, 1, RET_DV), F32)],
        scratch_shapes=[pltpu.VMEM((RET_DK, RET_DV), F32)],
        compiler_params=_cp(("arbitrary", "arbitrary", "arbitrary")))(
            proj, proj, proj, proj, o_raw, states, dgt, cos, sin, dm, qd, kd, cd, gn)


def _mla_tables(S):
    half = MLA_ROPE // 2
    inv = ROPE_THETA ** (-jnp.arange(half, dtype=F32) / half)
    ang = jnp.arange(S).astype(F32)[:, None] * inv[None, :]
    cos, sin = jnp.cos(ang), jnp.sin(ang)
    i = jnp.arange(MLA_ROPE)
    swap = (i[:, None] == (i[None, :] + half) % MLA_ROPE).astype(F32)
    return jnp.concatenate([cos, cos], axis=-1), jnp.concatenate([-sin, sin], axis=-1), swap


def _swap_halves(t, swap):
    return jnp.dot(t, swap, precision=lax.Precision.HIGHEST, preferred_element_type=F32)


def _head_norm_rope(n, r_, gn, gr, cos, sin, swap, scale):
    ssq = jnp.sum(n * n, axis=-1, keepdims=True) + jnp.sum(r_ * r_, axis=-1, keepdims=True)
    rstd = lax.rsqrt(ssq * (1.0 / MLA_QK) + RMS_EPS)
    yn = n * rstd * gn
    yr = r_ * rstd * gr
    yr = yr * cos + _swap_halves(yr, swap) * sin
    if scale != 1.0:
        yn, yr = yn * scale, yr * scale
    return yn, yr, rstd


def _head_norm_rope_bwd(dn, dr, n, r_, gn, gr, cos, sin, swap, scale):
    ssq = jnp.sum(n * n, axis=-1, keepdims=True) + jnp.sum(r_ * r_, axis=-1, keepdims=True)
    rstd = lax.rsqrt(ssq * (1.0 / MLA_QK) + RMS_EPS)
    hn, hr = n * rstd, r_ * rstd
    dyn = dn * scale if scale != 1.0 else dn
    dr = dr * scale if scale != 1.0 else dr
    dyr = dr * cos + _swap_halves(dr * sin, swap)
    dgn = jnp.sum(dyn * hn, axis=0, keepdims=True)
    dgr = jnp.sum(dyr * hr, axis=0, keepdims=True)
    dhn, dhr = dyn * gn, dyr * gr
    mt = (jnp.sum(dhn * hn, axis=-1, keepdims=True) + jnp.sum(dhr * hr, axis=-1, keepdims=True)) * (1.0 / MLA_QK)
    return rstd * (dhn - hn * mt), rstd * (dhr - hr * mt), dgn, dgr


MLA_PAD = 256


def _diag_bias():
    i = jnp.arange(ATT_TQ)
    return jnp.where((i[None, :] // CHUNK) <= (i[:, None] // CHUNK), 0.0, MASK_VALUE).astype(F32)


def _store_padded(dst, rows, n, r_):
    dst[rows, :MLA_NOPE] = n.astype(BF16)
    dst[rows, MLA_NOPE:MLA_QK] = r_.astype(BF16)
    dst[rows, MLA_QK:] = jnp.zeros((n.shape[0], MLA_PAD - MLA_QK), BF16)


def _mla_fwd(q_raw, kv, kr, gains, tabs, B, S):
    T = B * S
    TQ = ATT_TQ
    nQ = S // TQ
    qgn, qgr, kgn, kgr = gains
    cos, sin, swap = tabs
    scale = MLA_QK ** -0.5

    def body(q_ref, kv_ref, kr_ref, qgn_ref, qgr_ref, kgn_ref, kgr_ref, c_ref, s_ref, sw_ref, bias_ref,
             o_ref, lse_ref, qf_s, kf_s, v_s):
        sw = sw_ref[...]

        def prep(t, _):
            rows = pl.ds(pl.multiple_of(t * TQ, TQ), TQ)
            cs, sn = c_ref[rows, :], s_ref[rows, :]
            qn, qr, _ = _head_norm_rope(q_ref[rows, :MLA_NOPE], q_ref[rows, MLA_NOPE:], qgn_ref[...], qgr_ref[...],
                                        cs, sn, sw, scale)
            _store_padded(qf_s, rows, qn, qr)
            kn, krr, _ = _head_norm_rope(kv_ref[rows, :MLA_NOPE], kr_ref[rows, :], kgn_ref[...], kgr_ref[...],
                                         cs, sn, sw, 1.0)
            _store_padded(kf_s, rows, kn, krr)
            v_s[rows, :] = kv_ref[rows, MLA_NOPE:].astype(BF16)
            return 0

        lax.fori_loop(0, nQ, prep, 0)
        for i in range(nQ):
            rows = slice(i * TQ, (i + 1) * TQ)
            q = qf_s[rows, :]
            sd = _dot(q, kf_s[rows, :], NT) + bias_ref[...]
            m = jnp.max(sd, axis=-1, keepdims=True)
            if i:
                sl = _dot(q, kf_s[:i * TQ, :], NT)
                m = jnp.maximum(m, jnp.max(sl, axis=-1, keepdims=True))
            pd = jnp.exp(sd - m)
            l = jnp.sum(pd, axis=-1, keepdims=True)
            acc = _dot(pd.astype(BF16), v_s[rows, :], NN)
            if i:
                pl_ = jnp.exp(sl - m)
                l = l + jnp.sum(pl_, axis=-1, keepdims=True)
                acc = acc + _dot(pl_.astype(BF16), v_s[:i * TQ, :], NN)
            o_ref[rows, :] = (acc / l).astype(BF16)
            lse_ref[rows, :] = m + jnp.log(l)

    def vec(n):
        return _bs((1, n), lambda b, h: (0, 0))

    def head(n):
        return _bs((None, S, n), lambda b, h: (h, b, 0))

    tab = _bs((S, MLA_ROPE), lambda b, h: (0, 0))
    return pl.pallas_call(
        body, name="mla_fwd", grid=(B, MLA_H),
        in_specs=[head(MLA_QK), head(MLA_NOPE + MLA_V), _bs((S, MLA_ROPE), lambda b, h: (b, 0)),
                  vec(MLA_NOPE), vec(MLA_ROPE), vec(MLA_NOPE), vec(MLA_ROPE), tab, tab,
                  _bs((MLA_ROPE, MLA_ROPE), lambda b, h: (0, 0)), _bs((TQ, TQ), lambda b, h: (0, 0))],
        out_specs=[head(MLA_V), head(1)],
        out_shape=[jax.ShapeDtypeStruct((MLA_H, T, MLA_V), BF16), jax.ShapeDtypeStruct((MLA_H, T, 1), F32)],
        scratch_shapes=[pltpu.VMEM((S, MLA_PAD), BF16), pltpu.VMEM((S, MLA_PAD), BF16), pltpu.VMEM((S, MLA_V), BF16)],
        compiler_params=_cp(("parallel", "parallel")))(
            q_raw, kv, kr, qgn, qgr, kgn, kgr, cos, sin, swap, _diag_bias())


def _mla_bwd(q_raw, kv, kr, o, lse, do, gains, tabs, B, S):
    T = B * S
    TQ = ATT_TQ
    nQ = S // TQ
    qgn, qgr, kgn, kgr = gains
    cos, sin, swap = tabs
    scale = MLA_QK ** -0.5

    def body(q_ref, kv_ref, kr_ref, o_ref, lse_ref, do_ref, qgn_ref, qgr_ref, kgn_ref, kgr_ref, c_ref, s_ref, sw_ref,
             bias_ref, dq_ref, dkv_ref, dkr_ref, dqgn_ref, dqgr_ref, dkgn_ref, dkgr_ref,
             qf_s, kf_s, v_s, dl_s, dq_s, dk_s, dv_s):
        b, h = pl.program_id(0), pl.program_id(1)
        sw = sw_ref[...]

        def blk(t):
            return pl.ds(pl.multiple_of(t * TQ, TQ), TQ)

        def prep(t, _):
            rows = blk(t)
            cs, sn = c_ref[rows, :], s_ref[rows, :]
            qn, qr, _ = _head_norm_rope(q_ref[rows, :MLA_NOPE], q_ref[rows, MLA_NOPE:], qgn_ref[...], qgr_ref[...],
                                        cs, sn, sw, scale)
            _store_padded(qf_s, rows, qn, qr)
            kn, krr, _ = _head_norm_rope(kv_ref[rows, :MLA_NOPE], kr_ref[rows, :], kgn_ref[...], kgr_ref[...],
                                         cs, sn, sw, 1.0)
            _store_padded(kf_s, rows, kn, krr)
            v_s[rows, :] = kv_ref[rows, MLA_NOPE:].astype(BF16)
            dl_s[rows, :] = jnp.sum(do_ref[rows, :].astype(F32) * o_ref[rows, :].astype(F32), axis=-1, keepdims=True)
            dk_s[rows, :] = jnp.zeros((TQ, MLA_PAD), F32)
            dv_s[rows, :] = jnp.zeros((TQ, MLA_V), F32)
            return 0

        lax.fori_loop(0, nQ, prep, 0)

        gqn, gqr = jnp.zeros((1, MLA_NOPE), F32), jnp.zeros((1, MLA_ROPE), F32)
        for i in range(nQ):
            rows = slice(i * TQ, (i + 1) * TQ)
            q, doi, lse_i, dl_i = qf_s[rows, :], do_ref[rows, :], lse_ref[rows, :], dl_s[rows, :]

            def part(cols, bias):
                k, v = kf_s[cols, :], v_s[cols, :]
                s = _dot(q, k, NT)
                if bias is not None:
                    s = s + bias
                p = jnp.exp(s - lse_i)
                ds = (p * (_dot(doi, v, NT) - dl_i)).astype(BF16)
                dk_s[cols, :] += _dot(ds, q, TN)
                dv_s[cols, :] += _dot(p.astype(BF16), doi, TN)
                return _dot(ds, k, NN)

            dq = part(rows, bias_ref[...])
            if i:
                dq = dq + part(slice(0, i * TQ), None)
            dq_s[...] = dq
            dqn, dqr, a0, a1 = _head_norm_rope_bwd(dq_s[:, :MLA_NOPE], dq_s[:, MLA_NOPE:MLA_QK], q_ref[rows, :MLA_NOPE],
                                                   q_ref[rows, MLA_NOPE:], qgn_ref[...], qgr_ref[...], c_ref[rows, :],
                                                   s_ref[rows, :], sw, scale)
            dq_ref[rows, :MLA_NOPE] = dqn.astype(BF16)
            dq_ref[rows, MLA_NOPE:] = dqr.astype(BF16)
            gqn, gqr = gqn + a0, gqr + a1

        def post(t, carry):
            rows = blk(t)
            dkn, dkr, a2, a3 = _head_norm_rope_bwd(dk_s[rows, :MLA_NOPE], dk_s[rows, MLA_NOPE:MLA_QK],
                                                   kv_ref[rows, :MLA_NOPE], kr_ref[rows, :], kgn_ref[...], kgr_ref[...],
                                                   c_ref[rows, :], s_ref[rows, :], sw, 1.0)
            dkv_ref[rows, :MLA_NOPE] = dkn.astype(BF16)
            dkv_ref[rows, MLA_NOPE:] = dv_s[rows, :].astype(BF16)

            @pl.when(h == 0)
            def _():
                dkr_ref[rows, :] = dkr

            @pl.when(h > 0)
            def _():
                dkr_ref[rows, :] += dkr

            return carry[0] + a2, carry[1] + a3

        gkn, gkr = lax.fori_loop(0, nQ, post, (jnp.zeros((1, MLA_NOPE), F32), jnp.zeros((1, MLA_ROPE), F32)))
        first = (b == 0) & (h == 0)

        @pl.when(first)
        def _():
            dqgn_ref[...] = gqn
            dqgr_ref[...] = gqr
            dkgn_ref[...] = gkn
            dkgr_ref[...] = gkr

        @pl.when(jnp.logical_not(first))
        def _():
            dqgn_ref[...] += gqn
            dqgr_ref[...] += gqr
            dkgn_ref[...] += gkn
            dkgr_ref[...] += gkr

    def vec(n):
        return _bs((1, n), lambda b, h: (0, 0))

    def head(n):
        return _bs((None, S, n), lambda b, h: (h, b, 0))

    tab = _bs((S, MLA_ROPE), lambda b, h: (0, 0))
    return pl.pallas_call(
        body, name="mla_bwd", grid=(B, MLA_H),
        in_specs=[head(MLA_QK), head(MLA_NOPE + MLA_V), _bs((S, MLA_ROPE), lambda b, h: (b, 0)), head(MLA_V), head(1),
                  head(MLA_V), vec(MLA_NOPE), vec(MLA_ROPE), vec(MLA_NOPE), vec(MLA_ROPE), tab, tab,
                  _bs((MLA_ROPE, MLA_ROPE), lambda b, h: (0, 0)), _bs((TQ, TQ), lambda b, h: (0, 0))],
        out_specs=[head(MLA_QK), head(MLA_NOPE + MLA_V), _bs((S, MLA_ROPE), lambda b, h: (b, 0)),
                   vec(MLA_NOPE), vec(MLA_ROPE), vec(MLA_NOPE), vec(MLA_ROPE)],
        out_shape=[jax.ShapeDtypeStruct((MLA_H, T, MLA_QK), BF16), jax.ShapeDtypeStruct((MLA_H, T, MLA_NOPE + MLA_V), BF16),
                   jax.ShapeDtypeStruct((T, MLA_ROPE), F32), jax.ShapeDtypeStruct((1, MLA_NOPE), F32),
                   jax.ShapeDtypeStruct((1, MLA_ROPE), F32), jax.ShapeDtypeStruct((1, MLA_NOPE), F32),
                   jax.ShapeDtypeStruct((1, MLA_ROPE), F32)],
        scratch_shapes=[pltpu.VMEM((S, MLA_PAD), BF16), pltpu.VMEM((S, MLA_PAD), BF16), pltpu.VMEM((S, MLA_V), BF16),
                        pltpu.VMEM((S, 1), F32), pltpu.VMEM((TQ, MLA_PAD), F32), pltpu.VMEM((S, MLA_PAD), F32),
                        pltpu.VMEM((S, MLA_V), F32)],
        compiler_params=_cp(("arbitrary", "arbitrary")))(
            q_raw, kv, kr, o, lse, do, qgn, qgr, kgn, kgr, cos, sin, swap, _diag_bias())


def _adamw(name, recv, w, m, v, tr=None):
    n, R, C = recv.shape
    tr = R if tr is None else tr
    c1 = 1.0 - ADAM_B1 ** ADAM_STEP
    c2 = 1.0 - ADAM_B2 ** ADAM_STEP

    def body(r_ref, w_ref, m_ref, v_ref, g_ref, d_ref, nm_ref, nv_ref):
        g = r_ref[0].astype(F32)
        for k in range(1, n):
            g = g + r_ref[k].astype(F32)
        mm = ADAM_B1 * m_ref[...] + (1.0 - ADAM_B1) * g
        vv = ADAM_B2 * v_ref[...] + (1.0 - ADAM_B2) * (g * g)
        g_ref[...] = g
        nm_ref[...] = mm
        nv_ref[...] = vv
        d_ref[...] = -ADAM_LR * ((mm / c1) / (jnp.sqrt(vv / c2) + ADAM_EPS) + ADAM_WD * w_ref[...])

    blk = _bs((tr, C), lambda i: (i, 0))
    return pl.pallas_call(
        body, name=name, grid=(R // tr,), in_specs=[_bs((n, tr, C), lambda i: (0, i, 0)), blk, blk, blk],
        out_specs=[blk] * 4, out_shape=[jax.ShapeDtypeStruct((R, C), F32)] * 4,
        compiler_params=_cp(("parallel",)))(recv, w, m, v)


def _sum8(name, a):
    n, R, C = a.shape

    def body(a_ref, o_ref):
        s = a_ref[0]
        for k in range(1, n):
            s = s + a_ref[k]
        o_ref[...] = s

    return pl.pallas_call(body, name=name, out_shape=jax.ShapeDtypeStruct((R, C), a.dtype))(a)


def _place():
    return lax.axis_index("x"), lax.axis_index("y"), lax.axis_index("c")


def _idx(d):
    return 4 * d[0] + 2 * d[1] + d[2]


ANY = pl.BlockSpec(memory_space=pl.ANY)


def _all_gather(name, arrs):
    n = len(arrs)

    def body(*refs):
        ins, outs = refs[:n], refs[n:2 * n]
        send, recv, loc = refs[2 * n:]
        x, y, c = _place()
        me, sib = (x, y, c), (x, y, 1 - c)
        chips = [(1 - x, y), (x, 1 - y), (1 - x, 1 - y)]

        def cp(a, k, block, to, src=None):
            dst = outs[a].at[_idx(block)]
            return pltpu.make_async_remote_copy(src_ref=dst if src is None else src, dst_ref=dst, send_sem=send.at[a, k],
                                                recv_sem=recv.at[a, k], device_id=to, device_id_type=MESH)

        mine = [pltpu.make_async_copy(ins[a], outs[a].at[_idx(me)], loc.at[a]) for a in range(n)]
        for m_ in mine:
            m_.start()
        first = []
        for a in range(n):
            first.append(cp(a, 0, me, sib, src=ins[a]))
            first += [cp(a, 1 + j, me, (*chip, c), src=ins[a]) for j, chip in enumerate(chips)]
        for f in first:
            f.start()
        passed = []
        for j, chip in enumerate(chips):
            for a in range(n):
                cp(a, 1 + j, (*chip, c), me).wait_recv()
                p = cp(a, 4 + j, (*chip, c), sib)
                p.start()
                passed.append(p)
        for a in range(n):
            cp(a, 0, sib, me).wait_recv()
        for j, chip in enumerate(chips):
            for a in range(n):
                cp(a, 4 + j, (*chip, 1 - c), me).wait_recv()
        for f in first + passed:
            f.wait_send()
        for m_ in mine:
            m_.wait()

    return pl.pallas_call(
        body, name=name, in_specs=[ANY] * n, out_specs=[ANY] * n,
        out_shape=[jax.ShapeDtypeStruct((NDEV,) + a.shape, a.dtype) for a in arrs],
        scratch_shapes=[pltpu.SemaphoreType.DMA((n, 7)), pltpu.SemaphoreType.DMA((n, 7)), pltpu.SemaphoreType.DMA((n,))],
    )(*arrs)


def _all_to_all(name, arrs):
    n = len(arrs)

    def body(*refs):
        ins, outs = refs[:n], refs[n:2 * n]
        send, recv, loc = refs[2 * n:]
        x, y, c = _place()
        me = _idx((x, y, c))
        mine = [pltpu.make_async_copy(ins[a].at[me], outs[a].at[me], loc.at[a]) for a in range(n)]
        for m_ in mine:
            m_.start()
        copies = []
        for k in range(1, NDEV):
            peer = (x ^ (k >> 2), y ^ ((k >> 1) & 1), c ^ (k & 1))
            for a in range(n):
                copies.append(pltpu.make_async_remote_copy(
                    src_ref=ins[a].at[_idx(peer)], dst_ref=outs[a].at[me], send_sem=send.at[a, k - 1],
                    recv_sem=recv.at[a, k - 1], device_id=peer, device_id_type=MESH))
        for cp in copies:
            cp.start()
        for cp in copies:
            cp.wait()
        for m_ in mine:
            m_.wait()

    return pl.pallas_call(
        body, name=name, in_specs=[ANY] * n, out_specs=[ANY] * n,
        out_shape=[jax.ShapeDtypeStruct(a.shape, a.dtype) for a in arrs],
        scratch_shapes=[pltpu.SemaphoreType.DMA((n, 7)), pltpu.SemaphoreType.DMA((n, 7)), pltpu.SemaphoreType.DMA((n,))],
    )(*arrs)


def _sds(shape, dt):
    return jax.ShapeDtypeStruct(shape, dt)


def _proj_shared(name, h, w, out_dtype, tm=1024):
    T, K = h.shape
    J, _, n = w.shape
    return _mm(name, h, w, grid=(T // tm, J), a_spec=_bs((tm, K), lambda m, j: (m, 0)),
               b_spec=_bs((None, K, n), lambda m, j: (j, 0, 0)), o_spec=_bs((None, tm, n), lambda m, j: (j, m, 0)),
               out_shape=_sds((J, T, n), out_dtype), dims=NN)


def _proj_shared_dx(name, d, w, tm=1024):
    J, T, n = d.shape
    K = w.shape[1]
    return _mm(name, d, w, grid=(T // tm, J), a_spec=_bs((None, tm, n), lambda m, k: (k, m, 0)),
               b_spec=_bs((None, K, n), lambda m, k: (k, 0, 0)), o_spec=_bs((tm, K), lambda m, k: (m, 0)),
               out_shape=_sds((T, K), F32), dims=NT, kax=1, acc_shape=(tm, K))


def _proj_shared_dw(name, h, d, tt=512):
    T, K = h.shape
    J, _, n = d.shape
    return _mm(name, h, d, grid=(J, T // tt), a_spec=_bs((tt, K), lambda j, t: (t, 0)),
               b_spec=_bs((None, tt, n), lambda j, t: (j, t, 0)), o_spec=_bs((None, K, n), lambda j, t: (j, 0, 0)),
               out_shape=_sds((J, K, n), BF16), dims=TN, kax=1, acc_shape=(K, n))


def _out_proj(name, a, w, res, tm=512):
    J, T, k = a.shape
    N = w.shape[2]
    return _mm(name, a, w, grid=(T // tm,), a_spec=_bs((J, tm, k), lambda m: (0, m, 0)),
               b_spec=_bs((J, k, N), lambda m: (0, 0, 0)), o_spec=_bs((tm, N), lambda m: (m, 0)),
               out_shape=_sds((T, N), F32), dims=NN, res=res, res_spec=_bs((tm, N), lambda m: (m, 0)), jb=J)


def _out_proj_dx(name, dx, w, tm=512):
    T, N = dx.shape
    J, k, _ = w.shape
    return _mm(name, dx, w, grid=(T // tm, J), a_spec=_bs((tm, N), lambda m, j: (m, 0)),
               b_spec=_bs((None, k, N), lambda m, j: (j, 0, 0)), o_spec=_bs((None, tm, k), lambda m, j: (j, m, 0)),
               out_shape=_sds((J, T, k), BF16), dims=NT)


def _out_proj_dw(name, a, dx, tt=512):
    J, T, k = a.shape
    N = dx.shape[1]
    return _mm(name, a, dx, grid=(J, T // tt), a_spec=_bs((None, tt, k), lambda j, t: (j, t, 0)),
               b_spec=_bs((tt, N), lambda j, t: (t, 0)), o_spec=_bs((None, k, N), lambda j, t: (j, 0, 0)),
               out_shape=_sds((J, k, N), BF16), dims=TN, kax=1, acc_shape=(k, N))


def _bf16(x):
    return x.astype(BF16)


def _ffn_fwd(i, x, norm_g, w_in, cw, cb, w_out, B, S):
    h = _rms_fwd(f"ffn{i}_norm", x, norm_g)
    u = _proj_shared(f"ffn{i}_in", h, w_in, BF16)
    u4 = u.reshape(2, 4, u.shape[1], FSH)
    gt = _convffn_fwd(f"ffn{i}_gate", u4, cw, cb, B, S)
    y = _out_proj(f"ffn{i}_out", gt, w_out, x)
    return y, (x, h, u4, gt)


def _ffn_bwd(i, dy, saved, norm_g, w_in, cw, cb, w_out, B, S):
    x, h, u4, gt = saved
    dgt = _out_proj_dx(f"ffn{i}_out_dx", _bf16(dy), w_out)
    dw_out = _out_proj_dw(f"ffn{i}_out_dw", gt, _bf16(dy))
    du4, dcw, dcb = _convffn_bwd(f"ffn{i}_gate_bwd", u4, cw, cb, dgt, B, S)
    du = du4.reshape(NDEV, du4.shape[2], FSH)
    dh = _proj_shared_dx(f"ffn{i}_in_dx", du, w_in)
    dw_in = _proj_shared_dw(f"ffn{i}_in_dw", h, du)
    dx, dgn = _rms_bwd(f"ffn{i}_norm_bwd", x, norm_g, dh, dres=dy)
    return dx, dict(w_in=dw_in, w_out=dw_out, norm=dgn, cw=dcw, cb=dcb)


def kernel(x, ret_norm, ret_w_in, ret_gn, ret_w_out, mla_norm, mla_w_in, mla_q_norm, mla_w_qb, mla_kv_norm, mla_w_kvb, mla_q_head_norm, mla_k_head_norm, mla_w_out, ffn_norm, ffn_w_in, ffn_conv_w, ffn_conv_b, ffn_w_out, loss_target, m_ret_norm, m_ret_w_in, m_ret_gn, m_ret_w_out, m_mla_norm, m_mla_w_in, m_mla_q_norm, m_mla_w_qb, m_mla_kv_norm, m_mla_w_kvb, m_mla_q_head_norm, m_mla_k_head_norm, m_mla_w_out, m_ffn_norm, m_ffn_w_in, m_ffn_conv_w, m_ffn_conv_b, m_ffn_w_out, v_ret_norm, v_ret_w_in, v_ret_gn, v_ret_w_out, v_mla_norm, v_mla_w_in, v_mla_q_norm, v_mla_w_qb, v_mla_kv_norm, v_mla_w_kvb, v_mla_q_head_norm, v_mla_k_head_norm, v_mla_w_out, v_ffn_norm, v_ffn_w_in, v_ffn_conv_w, v_ffn_conv_b, v_ffn_w_out):
    B, S, D = x.shape
    T = B * S
    w = dict(ret_norm=ret_norm, ret_w_in=ret_w_in, ret_gn=ret_gn, ret_w_out=ret_w_out, mla_norm=mla_norm,
             mla_w_in=mla_w_in, mla_q_norm=mla_q_norm, mla_w_qb=mla_w_qb, mla_kv_norm=mla_kv_norm, mla_w_kvb=mla_w_kvb,
             mla_q_head_norm=mla_q_head_norm, mla_k_head_norm=mla_k_head_norm, mla_w_out=mla_w_out, ffn_norm=ffn_norm,
             ffn_w_in=ffn_w_in, ffn_conv_w=ffn_conv_w, ffn_conv_b=ffn_conv_b, ffn_w_out=ffn_w_out)
    mom = dict(ret_norm=m_ret_norm, ret_w_in=m_ret_w_in, ret_gn=m_ret_gn, ret_w_out=m_ret_w_out, mla_norm=m_mla_norm,
               mla_w_in=m_mla_w_in, mla_q_norm=m_mla_q_norm, mla_w_qb=m_mla_w_qb, mla_kv_norm=m_mla_kv_norm,
               mla_w_kvb=m_mla_w_kvb, mla_q_head_norm=m_mla_q_head_norm, mla_k_head_norm=m_mla_k_head_norm,
               mla_w_out=m_mla_w_out, ffn_norm=m_ffn_norm, ffn_w_in=m_ffn_w_in, ffn_conv_w=m_ffn_conv_w,
               ffn_conv_b=m_ffn_conv_b, ffn_w_out=m_ffn_w_out)
    var = dict(ret_norm=v_ret_norm, ret_w_in=v_ret_w_in, ret_gn=v_ret_gn, ret_w_out=v_ret_w_out, mla_norm=v_mla_norm,
               mla_w_in=v_mla_w_in, mla_q_norm=v_mla_q_norm, mla_w_qb=v_mla_w_qb, mla_kv_norm=v_mla_kv_norm,
               mla_w_kvb=v_mla_w_kvb, mla_q_head_norm=v_mla_q_head_norm, mla_k_head_norm=v_mla_k_head_norm,
               mla_w_out=v_mla_w_out, ffn_norm=v_ffn_norm, ffn_w_in=v_ffn_w_in, ffn_conv_w=v_ffn_conv_w,
               ffn_conv_b=v_ffn_conv_b, ffn_w_out=v_ffn_w_out)
    BIG = ["ret_w_in", "ret_w_out", "mla_w_in", "mla_w_qb", "mla_w_kvb", "mla_w_out", "ffn_w_in", "ffn_w_out"]
    REPL = ["ret_norm", "ffn_norm", "mla_q_head_norm", "mla_k_head_norm", "ffn_conv_b"]
    SHARDED_SMALL = ["ffn_conv_w", "ret_gn", "mla_norm", "mla_q_norm", "mla_kv_norm"]
    dev = _idx(_place())

    big2d = {k: w[k].reshape((-1,) + w[k].shape[-1:]) if k not in ("ffn_w_in", "ffn_w_out") else w[k] for k in BIG}
    small_vec = jnp.concatenate([w[k].reshape(-1) for k in SHARDED_SMALL])
    n_small = small_vec.shape[0]
    small_vec = jnp.pad(small_vec, (0, 3072 - n_small)).reshape(24, 128)
    gathered = _all_gather("gather_weights", [_bf16(big2d[k]) for k in BIG] + [small_vec])
    G = dict(zip(BIG, gathered[:-1]))
    sg = gathered[-1].reshape(NDEV, 3072)
    o0 = 0
    conv_w_full = sg[:, o0:o0 + 2112].reshape(NDEV, 2, 3, 352).transpose(1, 2, 0, 3).reshape(2, 3, FFN)
    o0 += 2112
    ret_gn_full = sg[:, o0:o0 + 256].reshape(NDEV, RET_H, 64).transpose(1, 0, 2).reshape(RET_H, 1, RET_DV)
    o0 += 256
    mla_norm_full = sg[:, o0:o0 + 128].reshape(1, D)
    o0 += 128
    q_norm_full = sg[:, o0:o0 + 48].reshape(1, MLA_QR)
    o0 += 48
    kv_norm_full = sg[:, o0:o0 + 32].reshape(1, MLA_KVR)

    Wret_in = G["ret_w_in"]
    Wret_out = G["ret_w_out"].reshape(RET_H, RET_DV, D)
    Wmla_in = G["mla_w_in"].reshape(D, MLA_QR + MLA_KVR + MLA_ROPE)
    Wq, Wkv, Wkr = Wmla_in[:, :MLA_QR], Wmla_in[:, MLA_QR:MLA_QR + MLA_KVR], Wmla_in[:, MLA_QR + MLA_KVR:]
    Wqb, Wkvb, Wmla_out = G["mla_w_qb"], G["mla_w_kvb"], G["mla_w_out"]
    Wffn_in = [G["ffn_w_in"][:, i] for i in range(2)]
    Wffn_out = [G["ffn_w_out"][:, i].reshape(4, FSH, D) for i in range(2)]
    cw = [conv_w_full[i].reshape(3, 4, FSH).transpose(1, 0, 2) for i in range(2)]
    cb = [ffn_conv_b[i].reshape(4, 1, FSH) for i in range(2)]
    fnorm = [ffn_norm[i].reshape(1, D) for i in range(2)]
    rtabs = _ret_tables(S)
    mtabs = _mla_tables(S)
    qh, kh = mla_q_head_norm.reshape(1, MLA_QK), mla_k_head_norm.reshape(1, MLA_QK)
    gains = (qh[:, :MLA_NOPE], qh[:, MLA_NOPE:], kh[:, :MLA_NOPE], kh[:, MLA_NOPE:])

    x0 = x.reshape(T, D)
    tgt = loss_target.reshape(T, D)
    h0 = _rms_fwd("ret_norm", x0, ret_norm.reshape(1, D))
    proj = _mm("ret_in", h0, Wret_in, grid=(T // 1024, NDEV), a_spec=_bs((1024, D), lambda m, j: (m, 0)),
               b_spec=_bs((None, D, 768), lambda m, j: (j, 0, 0)), o_spec=_bs((1024, 768), lambda m, j: (m, j)),
               out_shape=_sds((T, 6144), BF16), dims=NN)
    o_raw, rgt, states = _ret_fwd(proj, rtabs, ret_gn_full, B, S)
    x1 = _mm("ret_out", rgt, Wret_out.reshape(RET_H * RET_DV, D), grid=(T // 512,),
             a_spec=_bs((512, RET_H * RET_DV), lambda m: (m, 0)), b_spec=_bs((RET_H * RET_DV, D), lambda m: (0, 0)),
             o_spec=_bs((512, D), lambda m: (m, 0)), out_shape=_sds((T, D), F32), dims=NN, res=x0,
             res_spec=_bs((512, D), lambda m: (m, 0)))
    x2, ffn0_saved = _ffn_fwd(0, x1, fnorm[0], Wffn_in[0], cw[0], cb[0], Wffn_out[0], B, S)

    h2 = _rms_fwd("mla_norm", x2, mla_norm_full)

    def small_proj(name, wmat):
        n = wmat.shape[1]
        return _mm(name, h2, wmat, grid=(T // 512,), a_spec=_bs((512, D), lambda m: (m, 0)),
                   b_spec=_bs((D, n), lambda m: (0, 0)), o_spec=_bs((512, n), lambda m: (m, 0)),
                   out_shape=_sds((T, n), F32), dims=NN)

    c_q, c_kv, k_rope = small_proj("mla_in_q", Wq), small_proj("mla_in_kv", Wkv), small_proj("mla_in_kr", Wkr)
    cqn = _rms_fwd("mla_q_norm", c_q, q_norm_full)
    ckvn = _rms_fwd("mla_kv_norm", c_kv, kv_norm_full)
    q_raw = _proj_shared("mla_qb", cqn, Wqb, F32)
    kvh = _proj_shared("mla_kvb", ckvn, Wkvb, F32)
    att, lse = _mla_fwd(q_raw, kvh, k_rope, gains, mtabs, B, S)
    x3 = _out_proj("mla_out", att, Wmla_out, x2)
    y, ffn1_saved = _ffn_fwd(1, x3, fnorm[1], Wffn_in[1], cw[1], cb[1], Wffn_out[1], B, S)

    dy, colsq = _loss(y, tgt)
    loss = lax.psum(0.5 * jnp.sum(colsq) / D, ("x", "y", "c"))

    dx3, gf1 = _ffn_bwd(1, dy, ffn1_saved, fnorm[1], Wffn_in[1], cw[1], cb[1], Wffn_out[1], B, S)
    dx3b = _bf16(dx3)
    datt = _out_proj_dx("mla_out_dx", dx3b, Wmla_out)
    dWmla_out = _out_proj_dw("mla_out_dw", att, dx3b)
    dq_raw, dkvh, dkr, dqgn, dqgr, dkgn, dkgr = _mla_bwd(q_raw, kvh, k_rope, att, lse, datt, gains, mtabs, B, S)
    dcqn = _proj_shared_dx("mla_qb_dx", dq_raw, Wqb)
    dWqb = _proj_shared_dw("mla_qb_dw", cqn, dq_raw)
    dckvn = _proj_shared_dx("mla_kvb_dx", dkvh, Wkvb)
    dWkvb = _proj_shared_dw("mla_kvb_dw", ckvn, dkvh)
    dcq, dg_qn = _rms_bwd("mla_q_norm_bwd", c_q, q_norm_full, dcqn)
    dckv, dg_kvn = _rms_bwd("mla_kv_norm_bwd", c_kv, kv_norm_full, dckvn)
    dproj2 = _bf16(jnp.concatenate([dcq, dckv, dkr], axis=-1))
    dh2 = _mm("mla_in_dx", dproj2, Wmla_in, grid=(T // 512,), a_spec=_bs((512, 704), lambda m: (m, 0)),
              b_spec=_bs((D, 704), lambda m: (0, 0)), o_spec=_bs((512, D), lambda m: (m, 0)),
              out_shape=_sds((T, D), F32), dims=NT)
    dWmla_in = _mm("mla_in_dw", h2, dproj2, grid=(T // 512,), a_spec=_bs((512, D), lambda t: (t, 0)),
                   b_spec=_bs((512, 704), lambda t: (t, 0)), o_spec=_bs((D, 704), lambda t: (0, 0)),
                   out_shape=_sds((D, 704), BF16), dims=TN, kax=0, acc_shape=(D, 704))
    dx2, dg_mla_norm = _rms_bwd("mla_norm_bwd", x2, mla_norm_full, dh2, dres=dx3)

    dx1, gf0 = _ffn_bwd(0, dx2, ffn0_saved, fnorm[0], Wffn_in[0], cw[0], cb[0], Wffn_out[0], B, S)
    dx1b = _bf16(dx1)
    drgt = _mm("ret_out_dx", dx1b, Wret_out.reshape(RET_H * RET_DV, D), grid=(T // 512, RET_H),
               a_spec=_bs((512, D), lambda m, j: (m, 0)), b_spec=_bs((RET_DV, D), lambda m, j: (j, 0)),
               o_spec=_bs((512, RET_DV), lambda m, j: (m, j)), out_shape=_sds((T, RET_H * RET_DV), BF16), dims=NT)
    dWret_out = _mm("ret_out_dw", rgt, dx1b, grid=(RET_H, T // 512), a_spec=_bs((512, RET_DV), lambda j, t: (t, j)),
                    b_spec=_bs((512, D), lambda j, t: (t, 0)), o_spec=_bs((RET_DV, D), lambda j, t: (j, 0)),
                    out_shape=_sds((RET_H * RET_DV, D), BF16), dims=TN, kax=1, acc_shape=(RET_DV, D))
    dq, dk, dv, dg, dgn_ret = _ret_bwd(proj, o_raw, states, drgt, rtabs, ret_gn_full, B, S)
    dproj = jnp.concatenate([dq, dk, dv, dg], axis=-1)
    dh0 = _mm("ret_in_dx", dproj, Wret_in, grid=(T // 1024, NDEV), a_spec=_bs((1024, 768), lambda m, k: (m, k)),
              b_spec=_bs((None, D, 768), lambda m, k: (k, 0, 0)), o_spec=_bs((1024, D), lambda m, k: (m, 0)),
              out_shape=_sds((T, D), F32), dims=NT, kax=1, acc_shape=(1024, D))
    dWret_in = _mm("ret_in_dw", h0, dproj, grid=(NDEV, T // 512), a_spec=_bs((512, D), lambda j, t: (t, 0)),
                   b_spec=_bs((512, 768), lambda j, t: (t, j)), o_spec=_bs((None, D, 768), lambda j, t: (j, 0, 0)),
                   out_shape=_sds((NDEV, D, 768), BF16), dims=TN, kax=1, acc_shape=(D, 768))
    dx0, dg_ret_norm = _rms_bwd("ret_norm_bwd", x0, ret_norm.reshape(1, D), dh0, dres=dx1)
    grad_x = dx0.reshape(B, S, D)

    dW = dict(
        ret_w_in=dWret_in,
        ret_w_out=dWret_out.reshape(NDEV, 256, D),
        mla_w_in=dWmla_in.reshape(NDEV, 128, 704),
        mla_w_qb=dWqb, mla_w_kvb=dWkvb, mla_w_out=dWmla_out,
        ffn_w_in=jnp.stack([gf0["w_in"], gf1["w_in"]], axis=1).reshape(NDEV, 2 * D, FSH),
        ffn_w_out=jnp.stack([gf0["w_out"].reshape(NDEV, 352, D), gf1["w_out"].reshape(NDEV, 352, D)],
                            axis=1).reshape(NDEV, 2 * 352, D),
    )
    dconv_w = jnp.stack([g_["cw"].transpose(1, 0, 2).reshape(3, FFN) for g_ in (gf0, gf1)])
    dconv_b = jnp.stack([g_["cb"].reshape(FFN) for g_ in (gf0, gf1)])
    small_parts = [dg_ret_norm, gf0["norm"], gf1["norm"], dg_mla_norm, dg_qn, dg_kvn, dqgn, dqgr, dkgn, dkgr, dgn_ret,
                   dconv_w, dconv_b]
    small_g = jnp.concatenate([p.reshape(-1) for p in small_parts]).reshape(232, 128)
    received = _all_to_all("exchange_grads", [dW[k] for k in BIG])
    small_all = _all_gather("gather_small_grads", [small_g])[0]
    sred = _sum8("sum_small_grads", small_all).reshape(-1)

    def take(n):
        nonlocal off
        out = sred[off:off + n]
        off += n
        return out

    off = 0
    g_small = dict(ret_norm=take(D).reshape(1, D), ffn_norm=take(2 * D).reshape(2, D), mla_norm=take(D),
                   mla_q_norm=take(MLA_QR), mla_kv_norm=take(MLA_KVR))
    g_small["mla_q_head_norm"] = take(MLA_QK).reshape(1, MLA_QK)
    g_small["mla_k_head_norm"] = take(MLA_QK).reshape(1, MLA_QK)
    g_small["ret_gn"] = take(RET_H * RET_DV).reshape(1, RET_H, RET_DV)
    g_small["ffn_conv_w"] = take(2 * 3 * FFN).reshape(2, 3, FFN)
    g_small["ffn_conv_b"] = take(2 * FFN).reshape(2, FFN)
    g_small["mla_norm"] = lax.dynamic_slice(g_small["mla_norm"], (dev * 128,), (128,)).reshape(1, 128)
    g_small["mla_q_norm"] = lax.dynamic_slice(g_small["mla_q_norm"], (dev * 48,), (48,)).reshape(1, 48)
    g_small["mla_kv_norm"] = lax.dynamic_slice(g_small["mla_kv_norm"], (dev * 32,), (32,)).reshape(1, 32)
    g_small["ret_gn"] = lax.dynamic_slice(g_small["ret_gn"], (0, 0, dev * 64), (1, RET_H, 64))
    g_small["ffn_conv_w"] = lax.dynamic_slice(g_small["ffn_conv_w"], (0, 0, dev * 352), (2, 3, 352))

    grads, delta, new_m, new_v = {}, {}, {}, {}
    for k, rc in zip(BIG, received):
        shp = w[k].shape
        R, C = rc.shape[1], rc.shape[2]
        tr = next(t for t in (256, 128, 64) if R % t == 0)
        g_, d_, m_, v_ = _adamw(f"adamw_{k}", rc, w[k].reshape(R, C), mom[k].reshape(R, C), var[k].reshape(R, C), tr=tr)
        grads[k], delta[k], new_m[k], new_v[k] = (t.reshape(shp) for t in (g_, d_, m_, v_))
    SMALL = REPL + SHARDED_SMALL

    def pack(d):
        vflat = jnp.concatenate([d[k].reshape(-1) for k in SMALL])
        return jnp.pad(vflat, (0, 96 * 128 - vflat.shape[0])).reshape(96, 128)

    ps = _adamw("adamw_small", pack(g_small)[None], pack(w), pack(mom), pack(var))
    off = 0
    for k in SMALL:
        n = w[k].size
        grads[k], delta[k], new_m[k], new_v[k] = (t.reshape(-1)[off:off + n].reshape(w[k].shape) for t in ps)
        off += n
    names = list(w)
    return (loss, grad_x, *[grads[k] for k in names], *[delta[k] for k in names], *[new_m[k] for k in names],
            *[new_v[k] for k in names])
```

```python
import functools

import jax
import jax.numpy as jnp
from jax import lax
from jax.experimental import pallas as pl
from jax.experimental.pallas import tpu as pltpu

F32, BF16 = jnp.float32, jnp.bfloat16

NDEV = 8
D_MODEL = 1024
CHUNK = 64
RMS_EPS = 1e-6
ROPE_THETA = 10000.0
RET_H, RET_DK, RET_DV = 4, 256, 512
RET_SC = 256
MLA_H, MLA_QR, MLA_KVR = 8, 384, 256
MLA_NOPE, MLA_ROPE, MLA_V = 128, 64, 128
MLA_QK = MLA_NOPE + MLA_ROPE
MASK_VALUE = -1e30
FFN = 2816
FSH = FFN * 2 // NDEV
ATT_TQ = 256
ADAM_LR, ADAM_B1, ADAM_B2, ADAM_EPS, ADAM_WD, ADAM_STEP = 0.001, 0.9, 0.999, 1e-08, 0.01, 10
MESH = pl.DeviceIdType.MESH
VMEM_LIMIT = 56 * 2 ** 20


def _cp(sem):
    return pltpu.CompilerParams(dimension_semantics=sem, vmem_limit_bytes=VMEM_LIMIT)


def _dot(a, b, dims):
    return lax.dot_general(a, b, (dims, ((), ())), preferred_element_type=F32)


NN = ((1,), (0,))
NT = ((1,), (1,))
TN = ((0,), (0,))


def _place():
    return lax.axis_index("x"), lax.axis_index("y"), lax.axis_index("c")


def _idx(d):
    return 4 * d[0] + 2 * d[1] + d[2]


ANY = pl.BlockSpec(memory_space=pl.ANY)


class _Gather:
    def __init__(self, arrs):
        self.srcs = list(arrs)
        self.out_shape = [jax.ShapeDtypeStruct((NDEV,) + a.shape, a.dtype) for a in arrs]

    def _copies(self, ins, outs, send, recv, loc):
        n = len(self.srcs)
        x, y, c = _place()
        me, sib = (x, y, c), (x, y, 1 - c)
        chips = [(1 - x, y), (x, 1 - y), (1 - x, 1 - y)]

        def cp(a, k, block, to, src=None):
            dst = outs[a].at[_idx(block)]
            return pltpu.make_async_remote_copy(src_ref=dst if src is None else src, dst_ref=dst, send_sem=send.at[a, k],
                                                recv_sem=recv.at[a, k], device_id=to, device_id_type=MESH)

        mine = [pltpu.make_async_copy(ins[a], outs[a].at[_idx(me)], loc.at[a]) for a in range(n)]
        first = [cp(a, 0, me, sib, src=ins[a]) for a in range(n)]
        first += [cp(a, 1 + j, me, (*chip, c), src=ins[a]) for a in range(n) for j, chip in enumerate(chips)]
        landed = [cp(a, 1 + j, (*chip, c), me) for j, chip in enumerate(chips) for a in range(n)]
        passed = [cp(a, 4 + j, (*chip, c), sib) for j, chip in enumerate(chips) for a in range(n)]
        from_sib = [cp(a, 0, sib, me) for a in range(n)]
        from_sib += [cp(a, 4 + j, (*chip, 1 - c), me) for j, chip in enumerate(chips) for a in range(n)]
        return mine, first, landed, passed, from_sib

    def start(self, *refs):
        mine, first, _, _, _ = self._copies(*refs)
        for cp in mine + first:
            cp.start()

    def mid(self, *refs):
        _, _, landed, passed, _ = self._copies(*refs)
        for got, on in zip(landed, passed):
            got.wait_recv()
            on.start()

    def finish(self, *refs):
        mine, first, _, passed, from_sib = self._copies(*refs)
        for cp in from_sib:
            cp.wait_recv()
        for cp in first + passed:
            cp.wait_send()
        for cp in mine:
            cp.wait()


class _Exchange:
    def __init__(self, arrs):
        self.srcs = list(arrs)
        self.out_shape = [jax.ShapeDtypeStruct(a.shape, a.dtype) for a in arrs]

    def _copies(self, ins, outs, send, recv, loc):
        n = len(self.srcs)
        x, y, c = _place()
        me = _idx((x, y, c))
        mine = [pltpu.make_async_copy(ins[a].at[me], outs[a].at[me], loc.at[a]) for a in range(n)]
        remote = []
        for k in range(1, NDEV):
            peer = (x ^ (k >> 2), y ^ ((k >> 1) & 1), c ^ (k & 1))
            remote += [pltpu.make_async_remote_copy(
                src_ref=ins[a].at[_idx(peer)], dst_ref=outs[a].at[me], send_sem=send.at[a, k - 1],
                recv_sem=recv.at[a, k - 1], device_id=peer, device_id_type=MESH) for a in range(n)]
        return mine, remote

    def start(self, *refs):
        mine, remote = self._copies(*refs)
        for cp in mine + remote:
            cp.start()

    def mid(self, *refs):
        pass

    def finish(self, *refs):
        mine, remote = self._copies(*refs)
        for cp in remote + mine:
            cp.wait()


def _comm_scratch(n):
    return [pltpu.SemaphoreType.DMA((n, 7)), pltpu.SemaphoreType.DMA((n, 7)), pltpu.SemaphoreType.DMA((n,))]


def _comm_call(name, comm):
    n = len(comm.srcs)

    def body(*refs):
        parts = (refs[:n], refs[n:2 * n]) + tuple(refs[2 * n:])
        comm.start(*parts)
        comm.mid(*parts)
        comm.finish(*parts)

    return pl.pallas_call(body, name=name, in_specs=[ANY] * n, out_specs=[ANY] * n, out_shape=comm.out_shape,
                          scratch_shapes=_comm_scratch(n))(*comm.srcs)


def _pcall(body, *, name, grid, in_specs, out_specs, out_shape, scratch_shapes, sem, args, comm=None):
    if comm is None:
        return pl.pallas_call(body, name=name, grid=grid, in_specs=in_specs, out_specs=out_specs, out_shape=out_shape,
                              scratch_shapes=scratch_shapes, compiler_params=_cp(sem))(*args), None
    ni, no, ns, nc = len(in_specs), len(out_shape), len(scratch_shapes), len(comm.srcs)
    total = 1
    for g in grid:
        total *= g
    middle = total // 2

    def wrapped(*refs):
        ins, csrc = refs[:ni], refs[ni:ni + nc]
        outs, cdst = refs[ni + nc:ni + nc + no], refs[ni + nc + no:ni + 2 * nc + no]
        scr, sems = refs[ni + 2 * nc + no:ni + 2 * nc + no + ns], refs[ni + 2 * nc + no + ns:]
        step = pl.program_id(0)
        for k in range(1, len(grid)):
            step = step * grid[k] + pl.program_id(k)
        parts = (csrc, cdst) + tuple(sems)

        @pl.when(step == 0)
        def _():
            comm.start(*parts)

        body(*ins, *outs, *scr)

        @pl.when(step == middle)
        def _():
            comm.mid(*parts)

        @pl.when(step == total - 1)
        def _():
            comm.finish(*parts)

    res = pl.pallas_call(
        wrapped, name=name, grid=grid, in_specs=list(in_specs) + [ANY] * nc, out_specs=list(out_specs) + [ANY] * nc,
        out_shape=list(out_shape) + comm.out_shape, scratch_shapes=list(scratch_shapes) + _comm_scratch(nc),
        compiler_params=_cp(("arbitrary",) * len(grid)))(*args, *comm.srcs)
    return res[:no], res[no:]


def _mm(name, a, b, *, grid, a_spec, b_spec, o_spec, out_shape, dims, kax=None, res=None, res_spec=None,
        jb=0, acc_shape=None, comm=None):
    nk = grid[kax] if kax is not None else 1

    def body(*refs):
        if res is not None:
            a_ref, b_ref, r_ref, o_ref = refs[:4]
        else:
            a_ref, b_ref, o_ref = refs[:3]

        def product():
            if not jb:
                return _dot(a_ref[...], b_ref[...], dims)
            part = _dot(a_ref[0], b_ref[0], dims)
            for j in range(1, jb):
                part = part + _dot(a_ref[j], b_ref[j], dims)
            return part

        def fin(acc):
            if res is not None:
                acc = acc + r_ref[...]
            o_ref[...] = acc.astype(o_ref.dtype)

        if nk == 1:
            fin(product())
        else:
            acc_ref = refs[-1]
            k = pl.program_id(kax)

            @pl.when(k == 0)
            def _():
                acc_ref[...] = jnp.zeros_like(acc_ref)

            acc_ref[...] += product()

            @pl.when(k == nk - 1)
            def _():
                fin(acc_ref[...])

    sem = tuple("arbitrary" if i == kax else "parallel" for i in range(len(grid)))
    in_specs = [a_spec, b_spec] + ([res_spec] if res is not None else [])
    args = (a, b) + ((res,) if res is not None else ())
    scratch = [pltpu.VMEM(acc_shape, F32)] if nk > 1 else []
    (out,), got = _pcall(body, name=name, grid=grid, in_specs=in_specs, out_specs=[o_spec], out_shape=[out_shape],
                         scratch_shapes=scratch, sem=sem, args=args, comm=comm)
    return out if comm is None else (out, got)


def _bs(shape, fn):
    return pl.BlockSpec(shape, fn)


def _rms_fwd(name, x, g, tm=512):
    T, D = x.shape

    def body(x_ref, g_ref, o_ref):
        xf = x_ref[...]
        r = lax.rsqrt(jnp.mean(xf * xf, axis=-1, keepdims=True) + RMS_EPS)
        o_ref[...] = (xf * r * g_ref[...]).astype(o_ref.dtype)

    return pl.pallas_call(
        body, name=name, grid=(T // tm,),
        in_specs=[_bs((tm, D), lambda i: (i, 0)), _bs((1, D), lambda i: (0, 0))],
        out_specs=_bs((tm, D), lambda i: (i, 0)), out_shape=jax.ShapeDtypeStruct((T, D), BF16),
        compiler_params=_cp(("parallel",)))(x, g)


def _rms_bwd(name, x, g, dh, dres=None, tm=512):
    T, D = x.shape

    def body(*refs):
        if dres is not None:
            x_ref, g_ref, dh_ref, dres_ref, dx_ref, dg_ref = refs
        else:
            x_ref, g_ref, dh_ref, dx_ref, dg_ref = refs
        i = pl.program_id(0)
        xf = x_ref[...]
        r = lax.rsqrt(jnp.mean(xf * xf, axis=-1, keepdims=True) + RMS_EPS)
        xh = xf * r
        d = dh_ref[...].astype(F32)
        dxh = d * g_ref[...]
        dx = r * (dxh - xh * jnp.mean(dxh * xh, axis=-1, keepdims=True))
        if dres is not None:
            dx = dx + dres_ref[...]
        dx_ref[...] = dx
        part = jnp.sum(d * xh, axis=0, keepdims=True)

        @pl.when(i == 0)
        def _():
            dg_ref[...] = part

        @pl.when(i > 0)
        def _():
            dg_ref[...] += part

    row = _bs((tm, D), lambda i: (i, 0))
    vec = _bs((1, D), lambda i: (0, 0))
    in_specs = [row, vec, row] + ([row] if dres is not None else [])
    args = (x, g, dh) + ((dres,) if dres is not None else ())
    return pl.pallas_call(
        body, name=name, grid=(T // tm,), in_specs=in_specs, out_specs=[row, vec],
        out_shape=[jax.ShapeDtypeStruct((T, D), F32), jax.ShapeDtypeStruct((1, D), F32)],
        compiler_params=_cp(("arbitrary",)))(*args)


def _loss(y, tgt, tm=512):
    T, D = y.shape

    def body(y_ref, t_ref, dy_ref, s_ref):
        i = pl.program_id(0)
        e = y_ref[...] - t_ref[...]
        dy_ref[...] = e * (1.0 / D)
        part = jnp.sum(e * e, axis=0, keepdims=True)

        @pl.when(i == 0)
        def _():
            s_ref[...] = part

        @pl.when(i > 0)
        def _():
            s_ref[...] += part

    row = _bs((tm, D), lambda i: (i, 0))
    return pl.pallas_call(
        body, name="loss_head", grid=(T // tm,), in_specs=[row, row], out_specs=[row, _bs((1, D), lambda i: (0, 0))],
        out_shape=[jax.ShapeDtypeStruct((T, D), F32), jax.ShapeDtypeStruct((1, D), F32)],
        compiler_params=_cp(("arbitrary",)))(y, tgt)


def _shift_rows(t, k, row):
    return jnp.where(row >= k, pltpu.roll(t, k, 0), 0.0)


def _shift_rows_up(t, k, row, n):
    return jnp.where(row < n - k, pltpu.roll(t, n - k, 0), 0.0)


def _convffn_fwd(name, u, cw, cb, B, S):
    _, J, T, F = u.shape

    def body(u_ref, cw_ref, cb_ref, o_ref):
        a = u_ref[0].astype(F32)
        g = u_ref[1].astype(F32)
        row = lax.broadcasted_iota(jnp.int32, (S, F), 0)
        w0, w1, w2 = cw_ref[0:1, :], cw_ref[1:2, :], cw_ref[2:3, :]
        gc = _shift_rows(g, 2, row) * w0 + _shift_rows(g, 1, row) * w1 + g * w2 + cb_ref[...]
        o_ref[...] = (gc * jax.nn.sigmoid(gc) * a).astype(o_ref.dtype)

    return pl.pallas_call(
        body, name=name, grid=(J, B),
        in_specs=[_bs((2, None, S, F), lambda j, b: (0, j, b, 0)), _bs((None, 3, F), lambda j, b: (j, 0, 0)),
                  _bs((None, 1, F), lambda j, b: (j, 0, 0))],
        out_specs=_bs((None, S, F), lambda j, b: (j, b, 0)), out_shape=jax.ShapeDtypeStruct((J, T, F), BF16),
        compiler_params=_cp(("parallel", "parallel")))(u, cw, cb)


def _convffn_bwd(name, u, cw, cb, dgt, B, S):
    _, J, T, F = u.shape

    def body(u_ref, cw_ref, cb_ref, d_ref, du_ref, dcw_ref, dcb_ref):
        b = pl.program_id(1)
        a = u_ref[0].astype(F32)
        g = u_ref[1].astype(F32)
        d = d_ref[...].astype(F32)
        row = lax.broadcasted_iota(jnp.int32, (S, F), 0)
        w0, w1, w2 = cw_ref[0:1, :], cw_ref[1:2, :], cw_ref[2:3, :]
        g1, g2 = _shift_rows(g, 1, row), _shift_rows(g, 2, row)
        gc = g2 * w0 + g1 * w1 + g * w2 + cb_ref[...]
        sg = jax.nn.sigmoid(gc)
        du_ref[0] = (d * gc * sg).astype(du_ref.dtype)
        dgc = d * a * (sg * (1.0 + gc * (1.0 - sg)))
        dg = dgc * w2 + _shift_rows_up(dgc, 1, row, S) * w1 + _shift_rows_up(dgc, 2, row, S) * w0
        du_ref[1] = dg.astype(du_ref.dtype)
        parts = [jnp.sum(dgc * g2, axis=0, keepdims=True), jnp.sum(dgc * g1, axis=0, keepdims=True),
                 jnp.sum(dgc * g, axis=0, keepdims=True)]
        pb = jnp.sum(dgc, axis=0, keepdims=True)

        @pl.when(b == 0)
        def _():
            for k in range(3):
                dcw_ref[k:k + 1, :] = parts[k]
            dcb_ref[...] = pb

        @pl.when(b > 0)
        def _():
            for k in range(3):
                dcw_ref[k:k + 1, :] += parts[k]
            dcb_ref[...] += pb

    uspec = _bs((2, None, S, F), lambda j, b: (0, j, b, 0))
    return pl.pallas_call(
        body, name=name, grid=(J, B),
        in_specs=[uspec, _bs((None, 3, F), lambda j, b: (j, 0, 0)), _bs((None, 1, F), lambda j, b: (j, 0, 0)),
                  _bs((None, S, F), lambda j, b: (j, b, 0))],
        out_specs=[uspec, _bs((None, 3, F), lambda j, b: (j, 0, 0)), _bs((None, 1, F), lambda j, b: (j, 0, 0))],
        out_shape=[jax.ShapeDtypeStruct(u.shape, BF16), jax.ShapeDtypeStruct((J, 3, F), F32),
                   jax.ShapeDtypeStruct((J, 1, F), F32)],
        compiler_params=_cp(("parallel", "arbitrary")))(u, cw, cb, dgt)


def _ret_tables(S):
    half = RET_DK // 2
    inv = ROPE_THETA ** (-jnp.arange(half, dtype=F32) / half)
    ang = jnp.arange(S).astype(F32)[:, None] * inv[None, :]
    lg = jnp.log1p(-jnp.exp2(-5.0 - jnp.arange(RET_H, dtype=F32)))
    i = jnp.arange(RET_SC, dtype=F32)
    same_or_earlier = (jnp.floor(i[None, :] / CHUNK) <= jnp.floor(i[:, None] / CHUNK)).astype(F32)
    dm = jnp.exp(lg[:, None, None] * jnp.abs(i[:, None] - i[None, :])) * same_or_earlier[None]
    qd = jnp.exp(lg[:, None] * (i + 1.0))[:, :, None]
    kd = jnp.exp(lg[:, None] * (RET_SC - 1.0 - i))[:, :, None]
    cd = jnp.exp(lg * RET_SC)[:, None, None]
    return jnp.cos(ang), jnp.sin(ang), dm, qd, kd, cd


def _rope_halves(t, cs, sn):
    h = t.shape[-1] // 2
    t1, t2 = t[:, :h], t[:, h:]
    return jnp.concatenate([t1 * cs - t2 * sn, t2 * cs + t1 * sn], axis=-1)


def _unrope_halves(d, cs, sn):
    h = d.shape[-1] // 2
    d1, d2 = d[:, :h], d[:, h:]
    return jnp.concatenate([d1 * cs + d2 * sn, d2 * cs - d1 * sn], axis=-1)


def _ret_specs(nC, order):
    SC = RET_SC

    def sp(shape, fn):
        return _bs(shape, lambda *g: fn(*order(*g)))

    q = sp((SC, RET_DK), lambda b, h, c: (b * nC + c, h))
    k = sp((SC, RET_DK), lambda b, h, c: (b * nC + c, RET_H + h))
    v = sp((SC, RET_DV), lambda b, h, c: (b * nC + c, RET_H + h))
    g = sp((SC, RET_DV), lambda b, h, c: (b * nC + c, 2 * RET_H + h))
    cs = sp((SC, RET_DK // 2), lambda b, h, c: (c, 0))
    dm = sp((None, SC, SC), lambda b, h, c: (h, 0, 0))
    dv = sp((None, SC, 1), lambda b, h, c: (h, 0, 0))
    cd = sp((None, 1, 1), lambda b, h, c: (h, 0, 0))
    gn = sp((None, 1, RET_DV), lambda b, h, c: (h, 0, 0))
    wide = sp((SC, RET_DV), lambda b, h, c: (b * nC + c, h))
    narrow = sp((SC, RET_DK), lambda b, h, c: (b * nC + c, h))
    st = sp((None, None, None, RET_DK, RET_DV), lambda b, h, c: (b, h, c, 0, 0))
    return dict(q=q, k=k, v=v, g=g, cs=cs, dm=dm, dv=dv, cd=cd, gn=gn, wide=wide, narrow=narrow, st=st)


def _ret_fwd(proj, tabs, gn, B, S, comm=None):
    T = B * S
    nC = S // RET_SC
    cos, sin, dm, qd, kd, cd = tabs
    s = _ret_specs(nC, lambda b, h, c: (b, h, c))

    def body(q_ref, k_ref, v_ref, g_ref, cos_ref, sin_ref, dm_ref, qd_ref, kd_ref, cd_ref, gn_ref,
             o_ref, gt_ref, st_ref, state):
        c = pl.program_id(2)

        @pl.when(c == 0)
        def _():
            state[...] = jnp.zeros_like(state)

        cs, sn = cos_ref[...], sin_ref[...]
        qf = _rope_halves(q_ref[...].astype(F32), cs, sn)
        kf = _rope_halves(k_ref[...].astype(F32), cs, sn) * (RET_DK ** -0.5)
        v = v_ref[...]
        p = _dot(qf.astype(BF16), kf.astype(BF16), NT) * dm_ref[...]
        st = state[...]
        stb = st.astype(BF16)
        st_ref[...] = stb
        o = _dot(p.astype(BF16), v, NN) + _dot((qf * qd_ref[...]).astype(BF16), stb, NN)
        state[...] = st * cd_ref[...] + _dot((kf * kd_ref[...]).astype(BF16), v, TN)
        o_ref[...] = o
        r = lax.rsqrt(jnp.mean(o * o, axis=-1, keepdims=True) + RMS_EPS)
        gf = g_ref[...].astype(F32)
        gt_ref[...] = ((o * r * gn_ref[...]) * (gf * jax.nn.sigmoid(gf))).astype(BF16)

    return _pcall(
        body, name="ret_fwd", grid=(B, RET_H, nC),
        in_specs=[s["q"], s["k"], s["v"], s["g"], s["cs"], s["cs"], s["dm"], s["dv"], s["dv"], s["cd"], s["gn"]],
        out_specs=[s["wide"], s["wide"], s["st"]],
        out_shape=[jax.ShapeDtypeStruct((T, RET_H * RET_DV), F32), jax.ShapeDtypeStruct((T, RET_H * RET_DV), BF16),
                   jax.ShapeDtypeStruct((B, RET_H, nC, RET_DK, RET_DV), BF16)],
        scratch_shapes=[pltpu.VMEM((RET_DK, RET_DV), F32)], sem=("parallel", "parallel", "arbitrary"),
        args=(proj, proj, proj, proj, cos, sin, dm, qd, kd, cd, gn), comm=comm)


def _ret_bwd(proj, o_raw, states, dgt, tabs, gn, B, S, comm=None):
    T = B * S
    nC = S // RET_SC
    cos, sin, dm, qd, kd, cd = tabs
    s = _ret_specs(nC, lambda h, b, c: (b, h, nC - 1 - c))

    def body(q_ref, k_ref, v_ref, g_ref, o_ref, st_ref, d_ref, cos_ref, sin_ref, dm_ref, qd_ref, kd_ref, cd_ref,
             gn_ref, dq_ref, dk_ref, dv_ref, dg_ref, dgn_ref, dstate):
        b, c = pl.program_id(1), pl.program_id(2)

        @pl.when(c == 0)
        def _():
            dstate[...] = jnp.zeros_like(dstate)

        @pl.when((b == 0) & (c == 0))
        def _():
            dgn_ref[...] = jnp.zeros_like(dgn_ref)

        cs, sn = cos_ref[...], sin_ref[...]
        qf = _rope_halves(q_ref[...].astype(F32), cs, sn)
        kf = _rope_halves(k_ref[...].astype(F32), cs, sn) * (RET_DK ** -0.5)
        v = v_ref[...]
        gnv = gn_ref[...]
        o = o_ref[...]
        r = lax.rsqrt(jnp.mean(o * o, axis=-1, keepdims=True) + RMS_EPS)
        oh = o * r
        gf = g_ref[...].astype(F32)
        sg = jax.nn.sigmoid(gf)
        d = d_ref[...].astype(F32)
        dg_ref[...] = (d * (oh * gnv) * (sg * (1.0 + gf * (1.0 - sg)))).astype(BF16)
        don = d * (gf * sg)
        dgn_ref[...] += jnp.sum(don * oh, axis=0, keepdims=True)
        doh = don * gnv
        dO = (r * (doh - oh * jnp.mean(doh * oh, axis=-1, keepdims=True))).astype(BF16)
        dmv = dm_ref[...]
        qb, kb = qf.astype(BF16), kf.astype(BF16)
        p = (_dot(qb, kb, NT) * dmv).astype(BF16)
        dp = (_dot(dO, v, NT) * dmv).astype(BF16)
        st = st_ref[...]
        dsn = dstate[...]
        dsb = dsn.astype(BF16)
        qdv, kdv = qd_ref[...], kd_ref[...]
        dq = _dot(dp, kb, NN) + _dot(dO, st, NT) * qdv
        dk = _dot(dp, qb, TN) + _dot(v, dsb, NT) * kdv
        dv = _dot(p, dO, TN) + _dot((kf * kdv).astype(BF16), dsb, NN)
        dstate[...] = dsn * cd_ref[...] + _dot((qf * qdv).astype(BF16), dO, TN)
        dq_ref[...] = _unrope_halves(dq, cs, sn).astype(BF16)
        dk_ref[...] = (_unrope_halves(dk, cs, sn) * (RET_DK ** -0.5)).astype(BF16)
        dv_ref[...] = dv.astype(BF16)

    return _pcall(
        body, name="ret_bwd", grid=(RET_H, B, nC),
        in_specs=[s["q"], s["k"], s["v"], s["g"], s["wide"], s["st"], s["wide"], s["cs"], s["cs"], s["dm"], s["dv"],
                  s["dv"], s["cd"], s["gn"]],
        out_specs=[s["narrow"], s["narrow"], s["wide"], s["wide"], s["gn"]],
        out_shape=[jax.ShapeDtypeStruct((T, RET_H * RET_DK), BF16), jax.ShapeDtypeStruct((T, RET_H * RET_DK), BF16),
                   jax.ShapeDtypeStruct((T, RET_H * RET_DV), BF16), jax.ShapeDtypeStruct((T, RET_H * RET_DV), BF16),
                   jax.ShapeDtypeStruct((RET_H, 1, RET_DV), F32)],
        scratch_shapes=[pltpu.VMEM((RET_DK, RET_DV), F32)], sem=("arbitrary", "arbitrary", "arbitrary"),
        args=(proj, proj, proj, proj, o_raw, states, dgt, cos, sin, dm, qd, kd, cd, gn), comm=comm)


def _mla_tables(S):
    half = MLA_ROPE // 2
    inv = ROPE_THETA ** (-jnp.arange(half, dtype=F32) / half)
    ang = jnp.arange(S).astype(F32)[:, None] * inv[None, :]
    cos, sin = jnp.cos(ang), jnp.sin(ang)
    i = jnp.arange(MLA_ROPE)
    swap = (i[:, None] == (i[None, :] + half) % MLA_ROPE).astype(F32)
    return jnp.concatenate([cos, cos], axis=-1), jnp.concatenate([-sin, sin], axis=-1), swap


def _swap_halves(t, swap):
    return jnp.dot(t, swap, precision=lax.Precision.HIGHEST, preferred_element_type=F32)


def _head_norm_rope(n, r_, gn, gr, cos, sin, swap, scale):
    ssq = jnp.sum(n * n, axis=-1, keepdims=True) + jnp.sum(r_ * r_, axis=-1, keepdims=True)
    rstd = lax.rsqrt(ssq * (1.0 / MLA_QK) + RMS_EPS)
    yn = n * rstd * gn
    yr = r_ * rstd * gr
    yr = yr * cos + _swap_halves(yr, swap) * sin
    if scale != 1.0:
        yn, yr = yn * scale, yr * scale
    return yn, yr, rstd


def _head_norm_rope_bwd(dn, dr, n, r_, gn, gr, cos, sin, swap, scale):
    ssq = jnp.sum(n * n, axis=-1, keepdims=True) + jnp.sum(r_ * r_, axis=-1, keepdims=True)
    rstd = lax.rsqrt(ssq * (1.0 / MLA_QK) + RMS_EPS)
    hn, hr = n * rstd, r_ * rstd
    dyn = dn * scale if scale != 1.0 else dn
    dr = dr * scale if scale != 1.0 else dr
    dyr = dr * cos + _swap_halves(dr * sin, swap)
    dgn = jnp.sum(dyn * hn, axis=0, keepdims=True)
    dgr = jnp.sum(dyr * hr, axis=0, keepdims=True)
    dhn, dhr = dyn * gn, dyr * gr
    mt = (jnp.sum(dhn * hn, axis=-1, keepdims=True) + jnp.sum(dhr * hr, axis=-1, keepdims=True)) * (1.0 / MLA_QK)
    return rstd * (dhn - hn * mt), rstd * (dhr - hr * mt), dgn, dgr


MLA_PAD = 256


def _diag_bias():
    i = jnp.arange(ATT_TQ)
    return jnp.where((i[None, :] // CHUNK) <= (i[:, None] // CHUNK), 0.0, MASK_VALUE).astype(F32)


def _store_padded(dst, rows, n, r_):
    dst[rows, :MLA_NOPE] = n.astype(BF16)
    dst[rows, MLA_NOPE:MLA_QK] = r_.astype(BF16)
    dst[rows, MLA_QK:] = jnp.zeros((n.shape[0], MLA_PAD - MLA_QK), BF16)


def _mla_fwd(q_raw, kv, kr, gains, tabs, B, S, comm=None):
    T = B * S
    TQ = ATT_TQ
    nQ = S // TQ
    qgn, qgr, kgn, kgr = gains
    cos, sin, swap = tabs
    scale = MLA_QK ** -0.5

    def body(q_ref, kv_ref, kr_ref, qgn_ref, qgr_ref, kgn_ref, kgr_ref, c_ref, s_ref, sw_ref, bias_ref,
             o_ref, lse_ref, qf_s, kf_s, v_s):
        sw = sw_ref[...]

        def prep(t, _):
            rows = pl.ds(pl.multiple_of(t * TQ, TQ), TQ)
            cs, sn = c_ref[rows, :], s_ref[rows, :]
            qn, qr, _ = _head_norm_rope(q_ref[rows, :MLA_NOPE], q_ref[rows, MLA_NOPE:], qgn_ref[...], qgr_ref[...],
                                        cs, sn, sw, scale)
            _store_padded(qf_s, rows, qn, qr)
            kn, krr, _ = _head_norm_rope(kv_ref[rows, :MLA_NOPE], kr_ref[rows, :], kgn_ref[...], kgr_ref[...],
                                         cs, sn, sw, 1.0)
            _store_padded(kf_s, rows, kn, krr)
            v_s[rows, :] = kv_ref[rows, MLA_NOPE:].astype(BF16)
            return 0

        lax.fori_loop(0, nQ, prep, 0)
        for i in range(nQ):
            rows = slice(i * TQ, (i + 1) * TQ)
            q = qf_s[rows, :]
            sd = _dot(q, kf_s[rows, :], NT) + bias_ref[...]
            m = jnp.max(sd, axis=-1, keepdims=True)
            if i:
                sl = _dot(q, kf_s[:i * TQ, :], NT)
                m = jnp.maximum(m, jnp.max(sl, axis=-1, keepdims=True))
            pd = jnp.exp(sd - m)
            l = jnp.sum(pd, axis=-1, keepdims=True)
            acc = _dot(pd.astype(BF16), v_s[rows, :], NN)
            if i:
                pl_ = jnp.exp(sl - m)
                l = l + jnp.sum(pl_, axis=-1, keepdims=True)
                acc = acc + _dot(pl_.astype(BF16), v_s[:i * TQ, :], NN)
            o_ref[rows, :] = (acc / l).astype(BF16)
            lse_ref[rows, :] = m + jnp.log(l)

    def vec(n):
        return _bs((1, n), lambda b, h: (0, 0))

    def head(n):
        return _bs((None, S, n), lambda b, h: (h, b, 0))

    tab = _bs((S, MLA_ROPE), lambda b, h: (0, 0))
    return _pcall(
        body, name="mla_fwd", grid=(B, MLA_H),
        in_specs=[head(MLA_QK), head(MLA_NOPE + MLA_V), _bs((S, MLA_ROPE), lambda b, h: (b, 0)),
                  vec(MLA_NOPE), vec(MLA_ROPE), vec(MLA_NOPE), vec(MLA_ROPE), tab, tab,
                  _bs((MLA_ROPE, MLA_ROPE), lambda b, h: (0, 0)), _bs((TQ, TQ), lambda b, h: (0, 0))],
        out_specs=[head(MLA_V), head(1)],
        out_shape=[jax.ShapeDtypeStruct((MLA_H, T, MLA_V), BF16), jax.ShapeDtypeStruct((MLA_H, T, 1), F32)],
        scratch_shapes=[pltpu.VMEM((S, MLA_PAD), BF16), pltpu.VMEM((S, MLA_PAD), BF16), pltpu.VMEM((S, MLA_V), BF16)],
        sem=("parallel", "parallel"), args=(q_raw, kv, kr, qgn, qgr, kgn, kgr, cos, sin, swap, _diag_bias()), comm=comm)


def _mla_bwd(q_raw, kv, kr, o, lse, do, gains, tabs, B, S, comm=None):
    T = B * S
    TQ = ATT_TQ
    nQ = S // TQ
    qgn, qgr, kgn, kgr = gains
    cos, sin, swap = tabs
    scale = MLA_QK ** -0.5

    def body(q_ref, kv_ref, kr_ref, o_ref, lse_ref, do_ref, qgn_ref, qgr_ref, kgn_ref, kgr_ref, c_ref, s_ref, sw_ref,
             bias_ref, dq_ref, dkv_ref, dkr_ref, dqgn_ref, dqgr_ref, dkgn_ref, dkgr_ref,
             qf_s, kf_s, v_s, dl_s, dq_s, dk_s, dv_s):
        b, h = pl.program_id(0), pl.program_id(1)
        sw = sw_ref[...]

        def blk(t):
            return pl.ds(pl.multiple_of(t * TQ, TQ), TQ)

        def prep(t, _):
            rows = blk(t)
            cs, sn = c_ref[rows, :], s_ref[rows, :]
            qn, qr, _ = _head_norm_rope(q_ref[rows, :MLA_NOPE], q_ref[rows, MLA_NOPE:], qgn_ref[...], qgr_ref[...],
                                        cs, sn, sw, scale)
            _store_padded(qf_s, rows, qn, qr)
            kn, krr, _ = _head_norm_rope(kv_ref[rows, :MLA_NOPE], kr_ref[rows, :], kgn_ref[...], kgr_ref[...],
                                         cs, sn, sw, 1.0)
            _store_padded(kf_s, rows, kn, krr)
            v_s[rows, :] = kv_ref[rows, MLA_NOPE:].astype(BF16)
            dl_s[rows, :] = jnp.sum(do_ref[rows, :].astype(F32) * o_ref[rows, :].astype(F32), axis=-1, keepdims=True)
            dk_s[rows, :] = jnp.zeros((TQ, MLA_PAD), F32)
            dv_s[rows, :] = jnp.zeros((TQ, MLA_V), F32)
            return 0

        lax.fori_loop(0, nQ, prep, 0)

        gqn, gqr = jnp.zeros((1, MLA_NOPE), F32), jnp.zeros((1, MLA_ROPE), F32)
        for i in range(nQ):
            rows = slice(i * TQ, (i + 1) * TQ)
            q, doi, lse_i, dl_i = qf_s[rows, :], do_ref[rows, :], lse_ref[rows, :], dl_s[rows, :]

            def part(cols, bias):
                k, v = kf_s[cols, :], v_s[cols, :]
                s = _dot(q, k, NT)
                if bias is not None:
                    s = s + bias
                p = jnp.exp(s - lse_i)
                ds = (p * (_dot(doi, v, NT) - dl_i)).astype(BF16)
                dk_s[cols, :] += _dot(ds, q, TN)
                dv_s[cols, :] += _dot(p.astype(BF16), doi, TN)
                return _dot(ds, k, NN)

            dq = part(rows, bias_ref[...])
            if i:
                dq = dq + part(slice(0, i * TQ), None)
            dq_s[...] = dq
            dqn, dqr, a0, a1 = _head_norm_rope_bwd(dq_s[:, :MLA_NOPE], dq_s[:, MLA_NOPE:MLA_QK], q_ref[rows, :MLA_NOPE],
                                                   q_ref[rows, MLA_NOPE:], qgn_ref[...], qgr_ref[...], c_ref[rows, :],
                                                   s_ref[rows, :], sw, scale)
            dq_ref[rows, :MLA_NOPE] = dqn.astype(BF16)
            dq_ref[rows, MLA_NOPE:] = dqr.astype(BF16)
            gqn, gqr = gqn + a0, gqr + a1

        def post(t, carry):
            rows = blk(t)
            dkn, dkr, a2, a3 = _head_norm_rope_bwd(dk_s[rows, :MLA_NOPE], dk_s[rows, MLA_NOPE:MLA_QK],
                                                   kv_ref[rows, :MLA_NOPE], kr_ref[rows, :], kgn_ref[...], kgr_ref[...],
                                                   c_ref[rows, :], s_ref[rows, :], sw, 1.0)
            dkv_ref[rows, :MLA_NOPE] = dkn.astype(BF16)
            dkv_ref[rows, MLA_NOPE:] = dv_s[rows, :].astype(BF16)

            @pl.when(h == 0)
            def _():
                dkr_ref[rows, :] = dkr

            @pl.when(h > 0)
            def _():
                dkr_ref[rows, :] += dkr

            return carry[0] + a2, carry[1] + a3

        gkn, gkr = lax.fori_loop(0, nQ, post, (jnp.zeros((1, MLA_NOPE), F32), jnp.zeros((1, MLA_ROPE), F32)))
        first = (b == 0) & (h == 0)

        @pl.when(first)
        def _():
            dqgn_ref[...] = gqn
            dqgr_ref[...] = gqr
            dkgn_ref[...] = gkn
            dkgr_ref[...] = gkr

        @pl.when(jnp.logical_not(first))
        def _():
            dqgn_ref[...] += gqn
            dqgr_ref[...] += gqr
            dkgn_ref[...] += gkn
            dkgr_ref[...] += gkr

    def vec(n):
        return _bs((1, n), lambda b, h: (0, 0))

    def head(n):
        return _bs((None, S, n), lambda b, h: (h, b, 0))

    tab = _bs((S, MLA_ROPE), lambda b, h: (0, 0))
    return _pcall(
        body, name="mla_bwd", grid=(B, MLA_H),
        in_specs=[head(MLA_QK), head(MLA_NOPE + MLA_V), _bs((S, MLA_ROPE), lambda b, h: (b, 0)), head(MLA_V), head(1),
                  head(MLA_V), vec(MLA_NOPE), vec(MLA_ROPE), vec(MLA_NOPE), vec(MLA_ROPE), tab, tab,
                  _bs((MLA_ROPE, MLA_ROPE), lambda b, h: (0, 0)), _bs((TQ, TQ), lambda b, h: (0, 0))],
        out_specs=[head(MLA_QK), head(MLA_NOPE + MLA_V), _bs((S, MLA_ROPE), lambda b, h: (b, 0)),
                   vec(MLA_NOPE), vec(MLA_ROPE), vec(MLA_NOPE), vec(MLA_ROPE)],
        out_shape=[jax.ShapeDtypeStruct((MLA_H, T, MLA_QK), BF16), jax.ShapeDtypeStruct((MLA_H, T, MLA_NOPE + MLA_V), BF16),
                   jax.ShapeDtypeStruct((T, MLA_ROPE), F32), jax.ShapeDtypeStruct((1, MLA_NOPE), F32),
                   jax.ShapeDtypeStruct((1, MLA_ROPE), F32), jax.ShapeDtypeStruct((1, MLA_NOPE), F32),
                   jax.ShapeDtypeStruct((1, MLA_ROPE), F32)],
        scratch_shapes=[pltpu.VMEM((S, MLA_PAD), BF16), pltpu.VMEM((S, MLA_PAD), BF16), pltpu.VMEM((S, MLA_V), BF16),
                        pltpu.VMEM((S, 1), F32), pltpu.VMEM((TQ, MLA_PAD), F32), pltpu.VMEM((S, MLA_PAD), F32),
                        pltpu.VMEM((S, MLA_V), F32)],
        sem=("arbitrary", "arbitrary"),
        args=(q_raw, kv, kr, o, lse, do, qgn, qgr, kgn, kgr, cos, sin, swap, _diag_bias()), comm=comm)


def _adamw(name, recvs, w, m, v, tr=None):
    n, R, C = recvs[0].shape
    L = len(recvs)
    tr = R if tr is None else tr
    per = R // tr
    c1 = 1.0 - ADAM_B1 ** ADAM_STEP
    c2 = 1.0 - ADAM_B2 ** ADAM_STEP

    def body(*refs):
        r_refs = refs[:L]
        w_ref, m_ref, v_ref, g_ref, d_ref, nm_ref, nv_ref = refs[L:]
        layer = pl.program_id(0) // per

        def total(r_ref):
            t = r_ref[0].astype(F32)
            for k in range(1, n):
                t = t + r_ref[k].astype(F32)
            return t

        g = total(r_refs[0])
        for l in range(1, L):
            g = jnp.where(layer == l, total(r_refs[l]), g)
        mm = ADAM_B1 * m_ref[...] + (1.0 - ADAM_B1) * g
        vv = ADAM_B2 * v_ref[...] + (1.0 - ADAM_B2) * (g * g)
        g_ref[...] = g
        nm_ref[...] = mm
        nv_ref[...] = vv
        d_ref[...] = -ADAM_LR * ((mm / c1) / (jnp.sqrt(vv / c2) + ADAM_EPS) + ADAM_WD * w_ref[...])

    blk = _bs((tr, C), lambda i: (i, 0))
    r_specs = [_bs((n, tr, C), functools.partial(lambda l, i: (0, jnp.clip(i - l * per, 0, per - 1), 0), l))
               for l in range(L)]
    return pl.pallas_call(
        body, name=name, grid=(L * per,), in_specs=r_specs + [blk, blk, blk],
        out_specs=[blk] * 4, out_shape=[jax.ShapeDtypeStruct((L * R, C), F32)] * 4,
        compiler_params=_cp(("arbitrary",)))(*recvs, w, m, v)


def _sum8(name, a):
    n, R, C = a.shape

    def body(a_ref, o_ref):
        s = a_ref[0]
        for k in range(1, n):
            s = s + a_ref[k]
        o_ref[...] = s

    return pl.pallas_call(body, name=name, out_shape=jax.ShapeDtypeStruct((R, C), a.dtype))(a)


def _sds(shape, dt):
    return jax.ShapeDtypeStruct(shape, dt)


def _proj_shared(name, h, w, out_dtype, tm=1024, comm=None):
    T, K = h.shape
    J, _, n = w.shape
    return _mm(name, h, w, grid=(T // tm, J), a_spec=_bs((tm, K), lambda m, j: (m, 0)),
               b_spec=_bs((None, K, n), lambda m, j: (j, 0, 0)), o_spec=_bs((None, tm, n), lambda m, j: (j, m, 0)),
               out_shape=_sds((J, T, n), out_dtype), dims=NN, comm=comm)


def _proj_shared_dx(name, d, w, tm=1024, comm=None):
    J, T, n = d.shape
    K = w.shape[1]
    return _mm(name, d, w, grid=(T // tm, J), a_spec=_bs((None, tm, n), lambda m, k: (k, m, 0)),
               b_spec=_bs((None, K, n), lambda m, k: (k, 0, 0)), o_spec=_bs((tm, K), lambda m, k: (m, 0)),
               out_shape=_sds((T, K), F32), dims=NT, kax=1, acc_shape=(tm, K), comm=comm)


def _proj_shared_dw(name, h, d, tt=1024):
    T, K = h.shape
    J, _, n = d.shape
    return _mm(name, h, d, grid=(J, T // tt), a_spec=_bs((tt, K), lambda j, t: (t, 0)),
               b_spec=_bs((None, tt, n), lambda j, t: (j, t, 0)), o_spec=_bs((None, K, n), lambda j, t: (j, 0, 0)),
               out_shape=_sds((J, K, n), BF16), dims=TN, kax=1, acc_shape=(K, n))


def _out_proj(name, a, w, res, tm=512):
    J, T, k = a.shape
    N = w.shape[2]
    return _mm(name, a, w, grid=(T // tm,), a_spec=_bs((J, tm, k), lambda m: (0, m, 0)),
               b_spec=_bs((J, k, N), lambda m: (0, 0, 0)), o_spec=_bs((tm, N), lambda m: (m, 0)),
               out_shape=_sds((T, N), F32), dims=NN, res=res, res_spec=_bs((tm, N), lambda m: (m, 0)), jb=J)


def _out_proj_dx(name, dx, w, tm=512, comm=None):
    T, N = dx.shape
    J, k, _ = w.shape
    return _mm(name, dx, w, grid=(T // tm, J), a_spec=_bs((tm, N), lambda m, j: (m, 0)),
               b_spec=_bs((None, k, N), lambda m, j: (j, 0, 0)), o_spec=_bs((None, tm, k), lambda m, j: (j, m, 0)),
               out_shape=_sds((J, T, k), BF16), dims=NT, comm=comm)


def _out_proj_dw(name, a, dx, tt=512):
    J, T, k = a.shape
    N = dx.shape[1]
    return _mm(name, a, dx, grid=(J, T // tt), a_spec=_bs((None, tt, k), lambda j, t: (j, t, 0)),
               b_spec=_bs((tt, N), lambda j, t: (t, 0)), o_spec=_bs((None, k, N), lambda j, t: (j, 0, 0)),
               out_shape=_sds((J, k, N), BF16), dims=TN, kax=1, acc_shape=(k, N))


def _bf16(x):
    return x.astype(BF16)


def _ffn_fwd(i, x, norm_g, w_in, cw, cb, w_out, B, S, comm_in=None):
    h = _rms_fwd(f"ffn{i}_norm", x, norm_g)
    u = _proj_shared(f"ffn{i}_in", h, w_in, BF16, comm=comm_in)
    u, got = u if comm_in is not None else (u, None)
    u4 = u.reshape(2, 4, u.shape[1], FSH)
    gt = _convffn_fwd(f"ffn{i}_gate", u4, cw, cb, B, S)
    y = _out_proj(f"ffn{i}_out", gt, w_out, x)
    return y, (x, h, u4, gt), got


def _ffn_bwd(i, dy, saved, norm_g, w_in, cw, cb, w_out, B, S, comm_out_dx=None):
    x, h, u4, gt = saved
    dyb = _bf16(dy)
    dgt = _out_proj_dx(f"ffn{i}_out_dx", dyb, w_out, comm=comm_out_dx)
    dgt, got0 = dgt if comm_out_dx is not None else (dgt, None)
    dw_out = _out_proj_dw(f"ffn{i}_out_dw", gt, dyb).reshape(NDEV, FSH // 2, D_MODEL)
    du4, dcw, dcb = _convffn_bwd(f"ffn{i}_gate_bwd", u4, cw, cb, dgt, B, S)
    du = du4.reshape(NDEV, du4.shape[2], FSH)
    dh, (r_out,) = _proj_shared_dx(f"ffn{i}_in_dx", du, w_in, comm=_Exchange([dw_out]))
    dw_in = _proj_shared_dw(f"ffn{i}_in_dw", h, du)
    dx, dgn = _rms_bwd(f"ffn{i}_norm_bwd", x, norm_g, dh, dres=dy)
    return dx, dict(w_in=dw_in, norm=dgn, cw=dcw, cb=dcb), got0, r_out


def kernel(x, ret_norm, ret_w_in, ret_gn, ret_w_out, mla_norm, mla_w_in, mla_q_norm, mla_w_qb, mla_kv_norm, mla_w_kvb, mla_q_head_norm, mla_k_head_norm, mla_w_out, ffn_norm, ffn_w_in, ffn_conv_w, ffn_conv_b, ffn_w_out, loss_target, m_ret_norm, m_ret_w_in, m_ret_gn, m_ret_w_out, m_mla_norm, m_mla_w_in, m_mla_q_norm, m_mla_w_qb, m_mla_kv_norm, m_mla_w_kvb, m_mla_q_head_norm, m_mla_k_head_norm, m_mla_w_out, m_ffn_norm, m_ffn_w_in, m_ffn_conv_w, m_ffn_conv_b, m_ffn_w_out, v_ret_norm, v_ret_w_in, v_ret_gn, v_ret_w_out, v_mla_norm, v_mla_w_in, v_mla_q_norm, v_mla_w_qb, v_mla_kv_norm, v_mla_w_kvb, v_mla_q_head_norm, v_mla_k_head_norm, v_mla_w_out, v_ffn_norm, v_ffn_w_in, v_ffn_conv_w, v_ffn_conv_b, v_ffn_w_out):
    B, S, D = x.shape
    T = B * S
    w = dict(ret_norm=ret_norm, ret_w_in=ret_w_in, ret_gn=ret_gn, ret_w_out=ret_w_out, mla_norm=mla_norm,
             mla_w_in=mla_w_in, mla_q_norm=mla_q_norm, mla_w_qb=mla_w_qb, mla_kv_norm=mla_kv_norm, mla_w_kvb=mla_w_kvb,
             mla_q_head_norm=mla_q_head_norm, mla_k_head_norm=mla_k_head_norm, mla_w_out=mla_w_out, ffn_norm=ffn_norm,
             ffn_w_in=ffn_w_in, ffn_conv_w=ffn_conv_w, ffn_conv_b=ffn_conv_b, ffn_w_out=ffn_w_out)
    mom = dict(ret_norm=m_ret_norm, ret_w_in=m_ret_w_in, ret_gn=m_ret_gn, ret_w_out=m_ret_w_out, mla_norm=m_mla_norm,
               mla_w_in=m_mla_w_in, mla_q_norm=m_mla_q_norm, mla_w_qb=m_mla_w_qb, mla_kv_norm=m_mla_kv_norm,
               mla_w_kvb=m_mla_w_kvb, mla_q_head_norm=m_mla_q_head_norm, mla_k_head_norm=m_mla_k_head_norm,
               mla_w_out=m_mla_w_out, ffn_norm=m_ffn_norm, ffn_w_in=m_ffn_w_in, ffn_conv_w=m_ffn_conv_w,
               ffn_conv_b=m_ffn_conv_b, ffn_w_out=m_ffn_w_out)
    var = dict(ret_norm=v_ret_norm, ret_w_in=v_ret_w_in, ret_gn=v_ret_gn, ret_w_out=v_ret_w_out, mla_norm=v_mla_norm,
               mla_w_in=v_mla_w_in, mla_q_norm=v_mla_q_norm, mla_w_qb=v_mla_w_qb, mla_kv_norm=v_mla_kv_norm,
               mla_w_kvb=v_mla_w_kvb, mla_q_head_norm=v_mla_q_head_norm, mla_k_head_norm=v_mla_k_head_norm,
               mla_w_out=v_mla_w_out, ffn_norm=v_ffn_norm, ffn_w_in=v_ffn_w_in, ffn_conv_w=v_ffn_conv_w,
               ffn_conv_b=v_ffn_conv_b, ffn_w_out=v_ffn_w_out)
    BIG = ["ret_w_in", "ret_w_out", "mla_w_in", "mla_w_qb", "mla_w_kvb", "mla_w_out", "ffn_w_in", "ffn_w_out"]
    REPL = ["ret_norm", "ffn_norm", "mla_q_head_norm", "mla_k_head_norm", "ffn_conv_b"]
    SHARDED_SMALL = ["ffn_conv_w", "ret_gn", "mla_norm", "mla_q_norm", "mla_kv_norm"]
    dev = _idx(_place())

    def blk16(k, i=0):
        return _bf16(w[k][i])

    small_vec = jnp.concatenate([w[k].reshape(-1) for k in SHARDED_SMALL])
    n_small = small_vec.shape[0]
    small_vec = jnp.pad(small_vec, (0, 3072 - n_small)).reshape(24, 128)
    G = {}
    G["ret_w_in"], G["ret_w_out"], sg = _comm_call(
        "gather_ret_weights", _Gather([blk16("ret_w_in"), blk16("ret_w_out"), small_vec]))
    sg = sg.reshape(NDEV, 3072)
    o0 = 0
    conv_w_full = sg[:, o0:o0 + 2112].reshape(NDEV, 2, 3, 352).transpose(1, 2, 0, 3).reshape(2, 3, FFN)
    o0 += 2112
    ret_gn_full = sg[:, o0:o0 + 256].reshape(NDEV, RET_H, 64).transpose(1, 0, 2).reshape(RET_H, 1, RET_DV)
    o0 += 256
    mla_norm_full = sg[:, o0:o0 + 128].reshape(1, D)
    o0 += 128
    q_norm_full = sg[:, o0:o0 + 48].reshape(1, MLA_QR)
    o0 += 48
    kv_norm_full = sg[:, o0:o0 + 32].reshape(1, MLA_KVR)

    Wret_in = G["ret_w_in"]
    Wret_out = G["ret_w_out"].reshape(RET_H * RET_DV, D)
    cw = [conv_w_full[i].reshape(3, 4, FSH).transpose(1, 0, 2) for i in range(2)]
    cb = [ffn_conv_b[i].reshape(4, 1, FSH) for i in range(2)]
    fnorm = [ffn_norm[i].reshape(1, D) for i in range(2)]
    rtabs = _ret_tables(S)
    mtabs = _mla_tables(S)
    qh, kh = mla_q_head_norm.reshape(1, MLA_QK), mla_k_head_norm.reshape(1, MLA_QK)
    gains = (qh[:, :MLA_NOPE], qh[:, MLA_NOPE:], kh[:, :MLA_NOPE], kh[:, MLA_NOPE:])

    x0 = x.reshape(T, D)
    tgt = loss_target.reshape(T, D)
    h0 = _rms_fwd("ret_norm", x0, ret_norm.reshape(1, D))
    proj, (Wffn_in0,) = _mm(
        "ret_in", h0, Wret_in, grid=(T // 1024, NDEV), a_spec=_bs((1024, D), lambda m, j: (m, 0)),
        b_spec=_bs((None, D, 768), lambda m, j: (j, 0, 0)), o_spec=_bs((1024, 768), lambda m, j: (m, j)),
        out_shape=_sds((T, 6144), BF16), dims=NN, comm=_Gather([blk16("ffn_w_in", 0)]))
    MLA_W = ["mla_w_in", "mla_w_qb", "mla_w_kvb", "mla_w_out"]
    (o_raw, rgt, states), got = _ret_fwd(proj, rtabs, ret_gn_full, B, S,
                                         comm=_Gather([blk16("ffn_w_out", 0)] + [blk16(k) for k in MLA_W]))
    Wffn_out0 = got[0].reshape(4, FSH, D)
    Wmla_in = got[1].reshape(D, MLA_QR + MLA_KVR + MLA_ROPE)
    Wq, Wkv, Wkr = Wmla_in[:, :MLA_QR], Wmla_in[:, MLA_QR:MLA_QR + MLA_KVR], Wmla_in[:, MLA_QR + MLA_KVR:]
    Wqb, Wkvb, Wmla_out = got[2:]
    x1 = _mm("ret_out", rgt, Wret_out, grid=(T // 512,),
             a_spec=_bs((512, RET_H * RET_DV), lambda m: (m, 0)), b_spec=_bs((RET_H * RET_DV, D), lambda m: (0, 0)),
             o_spec=_bs((512, D), lambda m: (m, 0)), out_shape=_sds((T, D), F32), dims=NN, res=x0,
             res_spec=_bs((512, D), lambda m: (m, 0)))
    x2, ffn0_saved, (Wffn_in1,) = _ffn_fwd(0, x1, fnorm[0], Wffn_in0, cw[0], cb[0], Wffn_out0, B, S,
                                           comm_in=_Gather([blk16("ffn_w_in", 1)]))

    h2 = _rms_fwd("mla_norm", x2, mla_norm_full)

    def small_proj(name, wmat):
        n = wmat.shape[1]
        return _mm(name, h2, wmat, grid=(T // 512,), a_spec=_bs((512, D), lambda m: (m, 0)),
                   b_spec=_bs((D, n), lambda m: (0, 0)), o_spec=_bs((512, n), lambda m: (m, 0)),
                   out_shape=_sds((T, n), F32), dims=NN)

    c_q, c_kv, k_rope = small_proj("mla_in_q", Wq), small_proj("mla_in_kv", Wkv), small_proj("mla_in_kr", Wkr)
    cqn = _rms_fwd("mla_q_norm", c_q, q_norm_full)
    ckvn = _rms_fwd("mla_kv_norm", c_kv, kv_norm_full)
    q_raw = _proj_shared("mla_qb", cqn, Wqb, F32)
    kvh = _proj_shared("mla_kvb", ckvn, Wkvb, F32)
    (att, lse), (Wffn_out1,) = _mla_fwd(q_raw, kvh, k_rope, gains, mtabs, B, S, comm=_Gather([blk16("ffn_w_out", 1)]))
    Wffn_out1 = Wffn_out1.reshape(4, FSH, D)
    x3 = _out_proj("mla_out", att, Wmla_out, x2)
    y, ffn1_saved, _ = _ffn_fwd(1, x3, fnorm[1], Wffn_in1, cw[1], cb[1], Wffn_out1, B, S)

    dy, colsq = _loss(y, tgt)
    loss = lax.psum(0.5 * jnp.sum(colsq) / D, ("x", "y", "c"))

    dx3, gf1, _, r_ffn1_out = _ffn_bwd(1, dy, ffn1_saved, fnorm[1], Wffn_in1, cw[1], cb[1], Wffn_out1, B, S)
    dx3b = _bf16(dx3)
    datt = _out_proj_dx("mla_out_dx", dx3b, Wmla_out)
    dWmla_out = _out_proj_dw("mla_out_dw", att, dx3b)
    (dq_raw, dkvh, dkr, dqgn, dqgr, dkgn, dkgr), (r_ffn1_in,) = _mla_bwd(
        q_raw, kvh, k_rope, att, lse, datt, gains, mtabs, B, S, comm=_Exchange([gf1["w_in"]]))
    dcqn = _proj_shared_dx("mla_qb_dx", dq_raw, Wqb)
    dWqb = _proj_shared_dw("mla_qb_dw", cqn, dq_raw)
    dckvn = _proj_shared_dx("mla_kvb_dx", dkvh, Wkvb)
    dWkvb = _proj_shared_dw("mla_kvb_dw", ckvn, dkvh)
    dcq, dg_qn = _rms_bwd("mla_q_norm_bwd", c_q, q_norm_full, dcqn)
    dckv, dg_kvn = _rms_bwd("mla_kv_norm_bwd", c_kv, kv_norm_full, dckvn)
    dproj2 = _bf16(jnp.concatenate([dcq, dckv, dkr], axis=-1))
    dh2 = _mm("mla_in_dx", dproj2, Wmla_in, grid=(T // 512,), a_spec=_bs((512, 704), lambda m: (m, 0)),
              b_spec=_bs((D, 704), lambda m: (0, 0)), o_spec=_bs((512, D), lambda m: (m, 0)),
              out_shape=_sds((T, D), F32), dims=NT)
    dWmla_in = _mm("mla_in_dw", h2, dproj2, grid=(T // 512,), a_spec=_bs((512, D), lambda t: (t, 0)),
                   b_spec=_bs((512, 704), lambda t: (t, 0)), o_spec=_bs((D, 704), lambda t: (0, 0)),
                   out_shape=_sds((D, 704), BF16), dims=TN, kax=0, acc_shape=(D, 704)).reshape(NDEV, 128, 704)
    dx2, dg_mla_norm = _rms_bwd("mla_norm_bwd", x2, mla_norm_full, dh2, dres=dx3)

    dx1, gf0, r_mla, r_ffn0_out = _ffn_bwd(0, dx2, ffn0_saved, fnorm[0], Wffn_in0, cw[0], cb[0], Wffn_out0, B, S,
                                           comm_out_dx=_Exchange([dWmla_in, dWqb, dWkvb, dWmla_out]))
    dx1b = _bf16(dx1)
    drgt = _mm("ret_out_dx", dx1b, Wret_out, grid=(T // 512, RET_H),
               a_spec=_bs((512, D), lambda m, j: (m, 0)), b_spec=_bs((RET_DV, D), lambda m, j: (j, 0)),
               o_spec=_bs((512, RET_DV), lambda m, j: (m, j)), out_shape=_sds((T, RET_H * RET_DV), BF16), dims=NT)
    dWret_out = _mm("ret_out_dw", rgt, dx1b, grid=(RET_H, T // 512), a_spec=_bs((512, RET_DV), lambda j, t: (t, j)),
                    b_spec=_bs((512, D), lambda j, t: (t, 0)), o_spec=_bs((RET_DV, D), lambda j, t: (j, 0)),
                    out_shape=_sds((RET_H * RET_DV, D), BF16), dims=TN, kax=1,
                    acc_shape=(RET_DV, D)).reshape(NDEV, 256, D)
    (dq, dk, dv, dg, dgn_ret), (r_ffn0_in,) = _ret_bwd(proj, o_raw, states, drgt, rtabs, ret_gn_full, B, S,
                                                       comm=_Exchange([gf0["w_in"]]))
    dproj = jnp.concatenate([dq, dk, dv, dg], axis=-1)
    dWret_in, (r_ret_out,) = _mm(
        "ret_in_dw", h0, dproj, grid=(NDEV, T // 1024), a_spec=_bs((1024, D), lambda j, t: (t, 0)),
        b_spec=_bs((1024, 768), lambda j, t: (t, j)), o_spec=_bs((None, D, 768), lambda j, t: (j, 0, 0)),
        out_shape=_sds((NDEV, D, 768), BF16), dims=TN, kax=1, acc_shape=(D, 768), comm=_Exchange([dWret_out]))
    dh0, (r_ret_in,) = _mm(
        "ret_in_dx", dproj, Wret_in, grid=(T // 1024, NDEV), a_spec=_bs((1024, 768), lambda m, k: (m, k)),
        b_spec=_bs((None, D, 768), lambda m, k: (k, 0, 0)), o_spec=_bs((1024, D), lambda m, k: (m, 0)),
        out_shape=_sds((T, D), F32), dims=NT, kax=1, acc_shape=(1024, D), comm=_Exchange([dWret_in]))
    dx0, dg_ret_norm = _rms_bwd("ret_norm_bwd", x0, ret_norm.reshape(1, D), dh0, dres=dx1)
    grad_x = dx0.reshape(B, S, D)
    received = dict(ret_w_in=[r_ret_in], ret_w_out=[r_ret_out], mla_w_in=[r_mla[0]], mla_w_qb=[r_mla[1]],
                    mla_w_kvb=[r_mla[2]], mla_w_out=[r_mla[3]], ffn_w_in=[r_ffn0_in, r_ffn1_in],
                    ffn_w_out=[r_ffn0_out, r_ffn1_out])

    dconv_w = jnp.stack([g_["cw"].transpose(1, 0, 2).reshape(3, FFN) for g_ in (gf0, gf1)])
    dconv_b = jnp.stack([g_["cb"].reshape(FFN) for g_ in (gf0, gf1)])
    small_parts = [dg_ret_norm, gf0["norm"], gf1["norm"], dg_mla_norm, dg_qn, dg_kvn, dqgn, dqgr, dkgn, dkgr, dgn_ret,
                   dconv_w, dconv_b]
    small_g = jnp.concatenate([p.reshape(-1) for p in small_parts]).reshape(232, 128)
    small_all = _comm_call("gather_small_grads", _Gather([small_g]))[0]
    sred = _sum8("sum_small_grads", small_all).reshape(-1)

    def take(n):
        nonlocal off
        out = sred[off:off + n]
        off += n
        return out

    off = 0
    g_small = dict(ret_norm=take(D).reshape(1, D), ffn_norm=take(2 * D).reshape(2, D), mla_norm=take(D),
                   mla_q_norm=take(MLA_QR), mla_kv_norm=take(MLA_KVR))
    g_small["mla_q_head_norm"] = take(MLA_QK).reshape(1, MLA_QK)
    g_small["mla_k_head_norm"] = take(MLA_QK).reshape(1, MLA_QK)
    g_small["ret_gn"] = take(RET_H * RET_DV).reshape(1, RET_H, RET_DV)
    g_small["ffn_conv_w"] = take(2 * 3 * FFN).reshape(2, 3, FFN)
    g_small["ffn_conv_b"] = take(2 * FFN).reshape(2, FFN)
    g_small["mla_norm"] = lax.dynamic_slice(g_small["mla_norm"], (dev * 128,), (128,)).reshape(1, 128)
    g_small["mla_q_norm"] = lax.dynamic_slice(g_small["mla_q_norm"], (dev * 48,), (48,)).reshape(1, 48)
    g_small["mla_kv_norm"] = lax.dynamic_slice(g_small["mla_kv_norm"], (dev * 32,), (32,)).reshape(1, 32)
    g_small["ret_gn"] = lax.dynamic_slice(g_small["ret_gn"], (0, 0, dev * 64), (1, RET_H, 64))
    g_small["ffn_conv_w"] = lax.dynamic_slice(g_small["ffn_conv_w"], (0, 0, dev * 352), (2, 3, 352))

    grads, delta, new_m, new_v = {}, {}, {}, {}
    for k in BIG:
        rcs = received[k]
        shp = w[k].shape
        R, C = rcs[0].shape[1], rcs[0].shape[2]
        rows = len(rcs) * R
        tr = max(t for t in range(16, 257, 16) if R % t == 0)
        g_, d_, m_, v_ = _adamw(f"adamw_{k}", rcs, w[k].reshape(rows, C), mom[k].reshape(rows, C),
                                var[k].reshape(rows, C), tr=tr)
        grads[k], delta[k], new_m[k], new_v[k] = (t.reshape(shp) for t in (g_, d_, m_, v_))
    SMALL = REPL + SHARDED_SMALL

    def pack(d):
        vflat = jnp.concatenate([d[k].reshape(-1) for k in SMALL])
        return jnp.pad(vflat, (0, 96 * 128 - vflat.shape[0])).reshape(96, 128)

    ps = _adamw("adamw_small", [pack(g_small)[None]], pack(w), pack(mom), pack(var))
    off = 0
    for k in SMALL:
        n = w[k].size
        grads[k], delta[k], new_m[k], new_v[k] = (t.reshape(-1)[off:off + n].reshape(w[k].shape) for t in ps)
        off += n
    names = list(w)
    return (loss, grad_x, *[grads[k] for k in names], *[delta[k] for k in names], *[new_m[k] for k in names],
            *[new_v[k] for k in names])
```

```python
import functools

import jax
import jax.numpy as jnp
from jax import lax
from jax.experimental import pallas as pl
from jax.experimental.pallas import tpu as pltpu

F32, BF16 = jnp.float32, jnp.bfloat16

NDEV = 8
D_MODEL = 1024
CHUNK = 64
RMS_EPS = 1e-6
ROPE_THETA = 10000.0
RET_H, RET_DK, RET_DV = 4, 256, 512
RET_SC = 256
MLA_H, MLA_QR, MLA_KVR = 8, 384, 256
MLA_NOPE, MLA_ROPE, MLA_V = 128, 64, 128
MLA_QK = MLA_NOPE + MLA_ROPE
MASK_VALUE = -1e30
FFN = 2816
FSH = FFN * 2 // NDEV
ATT_TQ = 256
ADAM_LR, ADAM_B1, ADAM_B2, ADAM_EPS, ADAM_WD, ADAM_STEP = 0.001, 0.9, 0.999, 1e-08, 0.01, 10
MESH = pl.DeviceIdType.MESH
VMEM_LIMIT = 56 * 2 ** 20


def _cp(sem):
    return pltpu.CompilerParams(dimension_semantics=sem, vmem_limit_bytes=VMEM_LIMIT)


def _dot(a, b, dims):
    return lax.dot_general(a, b, (dims, ((), ())), preferred_element_type=F32)


NN = ((1,), (0,))
NT = ((1,), (1,))
TN = ((0,), (0,))


def _place():
    return lax.axis_index("x"), lax.axis_index("y"), lax.axis_index("c")


def _idx(d):
    return 4 * d[0] + 2 * d[1] + d[2]


ANY = pl.BlockSpec(memory_space=pl.ANY)


class _Gather:
    def __init__(self, arrs):
        self.srcs = list(arrs)
        self.out_shape = [jax.ShapeDtypeStruct((NDEV,) + a.shape, a.dtype) for a in arrs]

    def _copies(self, ins, outs, send, recv, loc):
        n = len(self.srcs)
        x, y, c = _place()
        me, sib = (x, y, c), (x, y, 1 - c)
        chips = [(1 - x, y), (x, 1 - y), (1 - x, 1 - y)]

        def cp(a, k, block, to, src=None):
            dst = outs[a].at[_idx(block)]
            return pltpu.make_async_remote_copy(src_ref=dst if src is None else src, dst_ref=dst, send_sem=send.at[a, k],
                                                recv_sem=recv.at[a, k], device_id=to, device_id_type=MESH)

        mine = [pltpu.make_async_copy(ins[a], outs[a].at[_idx(me)], loc.at[a]) for a in range(n)]
        first = [cp(a, 0, me, sib, src=ins[a]) for a in range(n)]
        first += [cp(a, 1 + j, me, (*chip, c), src=ins[a]) for a in range(n) for j, chip in enumerate(chips)]
        landed = [cp(a, 1 + j, (*chip, c), me) for j, chip in enumerate(chips) for a in range(n)]
        passed = [cp(a, 4 + j, (*chip, c), sib) for j, chip in enumerate(chips) for a in range(n)]
        from_sib = [cp(a, 0, sib, me) for a in range(n)]
        from_sib += [cp(a, 4 + j, (*chip, 1 - c), me) for j, chip in enumerate(chips) for a in range(n)]
        return mine, first, landed, passed, from_sib

    def start(self, *refs):
        mine, first, _, _, _ = self._copies(*refs)
        for cp in mine + first:
            cp.start()

    def mid(self, *refs):
        _, _, landed, passed, _ = self._copies(*refs)
        for got, on in zip(landed, passed):
            got.wait_recv()
            on.start()

    def finish(self, *refs):
        mine, first, _, passed, from_sib = self._copies(*refs)
        for cp in from_sib:
            cp.wait_recv()
        for cp in first + passed:
            cp.wait_send()
        for cp in mine:
            cp.wait()


class _Exchange:
    def __init__(self, arrs, rows=None):
        self.srcs = list(arrs)
        self.rows = rows if rows is not None else [None] * len(arrs)
        self.out_shape = [jax.ShapeDtypeStruct(a.shape if r is None else (a.shape[0], r[1]) + a.shape[2:], a.dtype)
                          for a, r in zip(arrs, self.rows)]

    def _copies(self, ins, outs, send, recv, loc):
        n = len(self.srcs)
        x, y, c = _place()
        me = _idx((x, y, c))

        def src(a, q):
            r = self.rows[a]
            return ins[a].at[q] if r is None else ins[a].at[q, pl.ds(r[0], r[1])]

        mine = [pltpu.make_async_copy(src(a, me), outs[a].at[me], loc.at[a]) for a in range(n)]
        remote = []
        for k in range(1, NDEV):
            peer = (x ^ (k >> 2), y ^ ((k >> 1) & 1), c ^ (k & 1))
            remote += [pltpu.make_async_remote_copy(
                src_ref=src(a, _idx(peer)), dst_ref=outs[a].at[me], send_sem=send.at[a, k - 1],
                recv_sem=recv.at[a, k - 1], device_id=peer, device_id_type=MESH) for a in range(n)]
        return mine, remote

    def start(self, *refs):
        mine, remote = self._copies(*refs)
        for cp in mine + remote:
            cp.start()

    def mid(self, *refs):
        pass

    def finish(self, *refs):
        mine, remote = self._copies(*refs)
        for cp in remote + mine:
            cp.wait()


def _comm_scratch(n):
    return [pltpu.SemaphoreType.DMA((n, 7)), pltpu.SemaphoreType.DMA((n, 7)), pltpu.SemaphoreType.DMA((n,))]


def _comm_call(name, comm):
    n = len(comm.srcs)

    def body(*refs):
        parts = (refs[:n], refs[n:2 * n]) + tuple(refs[2 * n:])
        comm.start(*parts)
        comm.mid(*parts)
        comm.finish(*parts)

    return pl.pallas_call(body, name=name, in_specs=[ANY] * n, out_specs=[ANY] * n, out_shape=comm.out_shape,
                          scratch_shapes=_comm_scratch(n))(*comm.srcs)


def _pcall(body, *, name, grid, in_specs, out_specs, out_shape, scratch_shapes, sem, args, comm=None):
    if comm is None:
        return pl.pallas_call(body, name=name, grid=grid, in_specs=in_specs, out_specs=out_specs, out_shape=out_shape,
                              scratch_shapes=scratch_shapes, compiler_params=_cp(sem))(*args), None
    ni, no, ns, nc = len(in_specs), len(out_shape), len(scratch_shapes), len(comm.srcs)
    total = 1
    for g in grid:
        total *= g
    middle = (3 * total) // 5

    def wrapped(*refs):
        ins, csrc = refs[:ni], refs[ni:ni + nc]
        outs, cdst = refs[ni + nc:ni + nc + no], refs[ni + nc + no:ni + 2 * nc + no]
        scr, sems = refs[ni + 2 * nc + no:ni + 2 * nc + no + ns], refs[ni + 2 * nc + no + ns:]
        step = pl.program_id(0)
        for k in range(1, len(grid)):
            step = step * grid[k] + pl.program_id(k)
        parts = (csrc, cdst) + tuple(sems)

        @pl.when(step == 0)
        def _():
            comm.start(*parts)

        body(*ins, *outs, *scr)

        @pl.when(step == middle)
        def _():
            comm.mid(*parts)

        @pl.when(step == total - 1)
        def _():
            comm.finish(*parts)

    res = pl.pallas_call(
        wrapped, name=name, grid=grid, in_specs=list(in_specs) + [ANY] * nc, out_specs=list(out_specs) + [ANY] * nc,
        out_shape=list(out_shape) + comm.out_shape, scratch_shapes=list(scratch_shapes) + _comm_scratch(nc),
        compiler_params=_cp(("arbitrary",) * len(grid)))(*args, *comm.srcs)
    return res[:no], res[no:]


def _mm(name, a, b, *, grid, a_spec, b_spec, o_spec, out_shape, dims, kax=None, res=None, res_spec=None,
        jb=0, acc_shape=None, comm=None):
    nk = grid[kax] if kax is not None else 1

    def body(*refs):
        if res is not None:
            a_ref, b_ref, r_ref, o_ref = refs[:4]
        else:
            a_ref, b_ref, o_ref = refs[:3]

        def product():
            if not jb:
                return _dot(a_ref[...], b_ref[...], dims)
            part = _dot(a_ref[0], b_ref[0], dims)
            for j in range(1, jb):
                part = part + _dot(a_ref[j], b_ref[j], dims)
            return part

        def fin(acc):
            if res is not None:
                acc = acc + r_ref[...]
            o_ref[...] = acc.astype(o_ref.dtype)

        if nk == 1:
            fin(product())
        else:
            acc_ref = refs[-1]
            k = pl.program_id(kax)

            @pl.when(k == 0)
            def _():
                acc_ref[...] = jnp.zeros_like(acc_ref)

            acc_ref[...] += product()

            @pl.when(k == nk - 1)
            def _():
                fin(acc_ref[...])

    sem = tuple("arbitrary" if i == kax else "parallel" for i in range(len(grid)))
    in_specs = [a_spec, b_spec] + ([res_spec] if res is not None else [])
    args = (a, b) + ((res,) if res is not None else ())
    scratch = [pltpu.VMEM(acc_shape, F32)] if nk > 1 else []
    (out,), got = _pcall(body, name=name, grid=grid, in_specs=in_specs, out_specs=[o_spec], out_shape=[out_shape],
                         scratch_shapes=scratch, sem=sem, args=args, comm=comm)
    return out if comm is None else (out, got)


def _bs(shape, fn):
    return pl.BlockSpec(shape, fn)


def _rms_fwd(name, x, g, tm=512):
    T, D = x.shape

    def body(x_ref, g_ref, o_ref):
        xf = x_ref[...]
        r = lax.rsqrt(jnp.mean(xf * xf, axis=-1, keepdims=True) + RMS_EPS)
        o_ref[...] = (xf * r * g_ref[...]).astype(o_ref.dtype)

    return pl.pallas_call(
        body, name=name, grid=(T // tm,),
        in_specs=[_bs((tm, D), lambda i: (i, 0)), _bs((1, D), lambda i: (0, 0))],
        out_specs=_bs((tm, D), lambda i: (i, 0)), out_shape=jax.ShapeDtypeStruct((T, D), BF16),
        compiler_params=_cp(("parallel",)))(x, g)


def _rms_bwd(name, x, g, dh, dres=None, tm=512):
    T, D = x.shape

    def body(*refs):
        if dres is not None:
            x_ref, g_ref, dh_ref, dres_ref, dx_ref, dg_ref = refs
        else:
            x_ref, g_ref, dh_ref, dx_ref, dg_ref = refs
        i = pl.program_id(0)
        xf = x_ref[...]
        r = lax.rsqrt(jnp.mean(xf * xf, axis=-1, keepdims=True) + RMS_EPS)
        xh = xf * r
        d = dh_ref[...].astype(F32)
        dxh = d * g_ref[...]
        dx = r * (dxh - xh * jnp.mean(dxh * xh, axis=-1, keepdims=True))
        if dres is not None:
            dx = dx + dres_ref[...]
        dx_ref[...] = dx
        part = jnp.sum(d * xh, axis=0, keepdims=True)

        @pl.when(i == 0)
        def _():
            dg_ref[...] = part

        @pl.when(i > 0)
        def _():
            dg_ref[...] += part

    row = _bs((tm, D), lambda i: (i, 0))
    vec = _bs((1, D), lambda i: (0, 0))
    in_specs = [row, vec, row] + ([row] if dres is not None else [])
    args = (x, g, dh) + ((dres,) if dres is not None else ())
    return pl.pallas_call(
        body, name=name, grid=(T // tm,), in_specs=in_specs, out_specs=[row, vec],
        out_shape=[jax.ShapeDtypeStruct((T, D), F32), jax.ShapeDtypeStruct((1, D), F32)],
        compiler_params=_cp(("arbitrary",)))(*args)


def _loss(y, tgt, tm=512):
    T, D = y.shape

    def body(y_ref, t_ref, dy_ref, s_ref):
        i = pl.program_id(0)
        e = y_ref[...] - t_ref[...]
        dy_ref[...] = e * (1.0 / D)
        part = jnp.sum(e * e, axis=0, keepdims=True)

        @pl.when(i == 0)
        def _():
            s_ref[...] = part

        @pl.when(i > 0)
        def _():
            s_ref[...] += part

    row = _bs((tm, D), lambda i: (i, 0))
    return pl.pallas_call(
        body, name="loss_head", grid=(T // tm,), in_specs=[row, row], out_specs=[row, _bs((1, D), lambda i: (0, 0))],
        out_shape=[jax.ShapeDtypeStruct((T, D), F32), jax.ShapeDtypeStruct((1, D), F32)],
        compiler_params=_cp(("arbitrary",)))(y, tgt)


def _shift_rows(t, k, row):
    return jnp.where(row >= k, pltpu.roll(t, k, 0), 0.0)


def _shift_rows_up(t, k, row, n):
    return jnp.where(row < n - k, pltpu.roll(t, n - k, 0), 0.0)


def _convffn_fwd(name, u, cw, cb, B, S):
    _, J, T, F = u.shape

    def body(u_ref, cw_ref, cb_ref, o_ref):
        a = u_ref[0].astype(F32)
        g = u_ref[1].astype(F32)
        row = lax.broadcasted_iota(jnp.int32, (S, F), 0)
        w0, w1, w2 = cw_ref[0:1, :], cw_ref[1:2, :], cw_ref[2:3, :]
        gc = _shift_rows(g, 2, row) * w0 + _shift_rows(g, 1, row) * w1 + g * w2 + cb_ref[...]
        o_ref[...] = (gc * jax.nn.sigmoid(gc) * a).astype(o_ref.dtype)

    return pl.pallas_call(
        body, name=name, grid=(J, B),
        in_specs=[_bs((2, None, S, F), lambda j, b: (0, j, b, 0)), _bs((None, 3, F), lambda j, b: (j, 0, 0)),
                  _bs((None, 1, F), lambda j, b: (j, 0, 0))],
        out_specs=_bs((None, S, F), lambda j, b: (j, b, 0)), out_shape=jax.ShapeDtypeStruct((J, T, F), BF16),
        compiler_params=_cp(("parallel", "parallel")))(u, cw, cb)


def _convffn_bwd(name, u, cw, cb, dgt, B, S):
    _, J, T, F = u.shape

    def body(u_ref, cw_ref, cb_ref, d_ref, du_ref, dcw_ref, dcb_ref):
        b = pl.program_id(1)
        a = u_ref[0].astype(F32)
        g = u_ref[1].astype(F32)
        d = d_ref[...].astype(F32)
        row = lax.broadcasted_iota(jnp.int32, (S, F), 0)
        w0, w1, w2 = cw_ref[0:1, :], cw_ref[1:2, :], cw_ref[2:3, :]
        g1, g2 = _shift_rows(g, 1, row), _shift_rows(g, 2, row)
        gc = g2 * w0 + g1 * w1 + g * w2 + cb_ref[...]
        sg = jax.nn.sigmoid(gc)
        du_ref[0] = (d * gc * sg).astype(du_ref.dtype)
        dgc = d * a * (sg * (1.0 + gc * (1.0 - sg)))
        dg = dgc * w2 + _shift_rows_up(dgc, 1, row, S) * w1 + _shift_rows_up(dgc, 2, row, S) * w0
        du_ref[1] = dg.astype(du_ref.dtype)
        parts = [jnp.sum(dgc * g2, axis=0, keepdims=True), jnp.sum(dgc * g1, axis=0, keepdims=True),
                 jnp.sum(dgc * g, axis=0, keepdims=True)]
        pb = jnp.sum(dgc, axis=0, keepdims=True)

        @pl.when(b == 0)
        def _():
            for k in range(3):
                dcw_ref[k:k + 1, :] = parts[k]
            dcb_ref[...] = pb

        @pl.when(b > 0)
        def _():
            for k in range(3):
                dcw_ref[k:k + 1, :] += parts[k]
            dcb_ref[...] += pb

    uspec = _bs((2, None, S, F), lambda j, b: (0, j, b, 0))
    return pl.pallas_call(
        body, name=name, grid=(J, B),
        in_specs=[uspec, _bs((None, 3, F), lambda j, b: (j, 0, 0)), _bs((None, 1, F), lambda j, b: (j, 0, 0)),
                  _bs((None, S, F), lambda j, b: (j, b, 0))],
        out_specs=[uspec, _bs((None, 3, F), lambda j, b: (j, 0, 0)), _bs((None, 1, F), lambda j, b: (j, 0, 0))],
        out_shape=[jax.ShapeDtypeStruct(u.shape, BF16), jax.ShapeDtypeStruct((J, 3, F), F32),
                   jax.ShapeDtypeStruct((J, 1, F), F32)],
        compiler_params=_cp(("parallel", "arbitrary")))(u, cw, cb, dgt)


def _ret_tables(S):
    half = RET_DK // 2
    inv = ROPE_THETA ** (-jnp.arange(half, dtype=F32) / half)
    ang = jnp.arange(S).astype(F32)[:, None] * inv[None, :]
    lg = jnp.log1p(-jnp.exp2(-5.0 - jnp.arange(RET_H, dtype=F32)))
    i = jnp.arange(RET_SC, dtype=F32)
    same_or_earlier = (jnp.floor(i[None, :] / CHUNK) <= jnp.floor(i[:, None] / CHUNK)).astype(F32)
    dm = jnp.exp(lg[:, None, None] * jnp.abs(i[:, None] - i[None, :])) * same_or_earlier[None]
    qd = jnp.exp(lg[:, None] * (i + 1.0))[:, :, None]
    kd = jnp.exp(lg[:, None] * (RET_SC - 1.0 - i))[:, :, None]
    cd = jnp.exp(lg * RET_SC)[:, None, None]
    return jnp.cos(ang), jnp.sin(ang), dm, qd, kd, cd


def _rope_halves(t, cs, sn):
    h = t.shape[-1] // 2
    t1, t2 = t[:, :h], t[:, h:]
    return jnp.concatenate([t1 * cs - t2 * sn, t2 * cs + t1 * sn], axis=-1)


def _unrope_halves(d, cs, sn):
    h = d.shape[-1] // 2
    d1, d2 = d[:, :h], d[:, h:]
    return jnp.concatenate([d1 * cs + d2 * sn, d2 * cs - d1 * sn], axis=-1)


def _ret_specs(nC, order):
    SC = RET_SC

    def sp(shape, fn):
        return _bs(shape, lambda *g: fn(*order(*g)))

    q = sp((SC, RET_DK), lambda b, h, c: (b * nC + c, h))
    k = sp((SC, RET_DK), lambda b, h, c: (b * nC + c, RET_H + h))
    v = sp((SC, RET_DV), lambda b, h, c: (b * nC + c, RET_H + h))
    g = sp((SC, RET_DV), lambda b, h, c: (b * nC + c, 2 * RET_H + h))
    cs = sp((SC, RET_DK // 2), lambda b, h, c: (c, 0))
    dm = sp((None, SC, SC), lambda b, h, c: (h, 0, 0))
    dv = sp((None, SC, 1), lambda b, h, c: (h, 0, 0))
    cd = sp((None, 1, 1), lambda b, h, c: (h, 0, 0))
    gn = sp((None, 1, RET_DV), lambda b, h, c: (h, 0, 0))
    wide = sp((SC, RET_DV), lambda b, h, c: (b * nC + c, h))
    narrow = sp((SC, RET_DK), lambda b, h, c: (b * nC + c, h))
    st = sp((None, None, None, RET_DK, RET_DV), lambda b, h, c: (b, h, c, 0, 0))
    return dict(q=q, k=k, v=v, g=g, cs=cs, dm=dm, dv=dv, cd=cd, gn=gn, wide=wide, narrow=narrow, st=st)


def _ret_fwd(proj, tabs, gn, B, S, comm=None):
    T = B * S
    nC = S // RET_SC
    cos, sin, dm, qd, kd, cd = tabs
    s = _ret_specs(nC, lambda b, h, c: (b, h, c))

    def body(q_ref, k_ref, v_ref, g_ref, cos_ref, sin_ref, dm_ref, qd_ref, kd_ref, cd_ref, gn_ref,
             o_ref, gt_ref, st_ref, state):
        c = pl.program_id(2)

        @pl.when(c == 0)
        def _():
            state[...] = jnp.zeros_like(state)

        cs, sn = cos_ref[...], sin_ref[...]
        qf = _rope_halves(q_ref[...].astype(F32), cs, sn)
        kf = _rope_halves(k_ref[...].astype(F32), cs, sn) * (RET_DK ** -0.5)
        v = v_ref[...]
        p = _dot(qf.astype(BF16), kf.astype(BF16), NT) * dm_ref[...]
        st = state[...]
        stb = st.astype(BF16)
        st_ref[...] = stb
        o = _dot(p.astype(BF16), v, NN) + _dot((qf * qd_ref[...]).astype(BF16), stb, NN)
        state[...] = st * cd_ref[...] + _dot((kf * kd_ref[...]).astype(BF16), v, TN)
        o_ref[...] = o
        r = lax.rsqrt(jnp.mean(o * o, axis=-1, keepdims=True) + RMS_EPS)
        gf = g_ref[...].astype(F32)
        gt_ref[...] = ((o * r * gn_ref[...]) * (gf * jax.nn.sigmoid(gf))).astype(BF16)

    return _pcall(
        body, name="ret_fwd", grid=(B, RET_H, nC),
        in_specs=[s["q"], s["k"], s["v"], s["g"], s["cs"], s["cs"], s["dm"], s["dv"], s["dv"], s["cd"], s["gn"]],
        out_specs=[s["wide"], s["wide"], s["st"]],
        out_shape=[jax.ShapeDtypeStruct((T, RET_H * RET_DV), F32), jax.ShapeDtypeStruct((T, RET_H * RET_DV), BF16),
                   jax.ShapeDtypeStruct((B, RET_H, nC, RET_DK, RET_DV), BF16)],
        scratch_shapes=[pltpu.VMEM((RET_DK, RET_DV), F32)], sem=("parallel", "parallel", "arbitrary"),
        args=(proj, proj, proj, proj, cos, sin, dm, qd, kd, cd, gn), comm=comm)


def _ret_bwd(proj, o_raw, states, dgt, tabs, gn, B, S, comm=None):
    T = B * S
    nC = S // RET_SC
    cos, sin, dm, qd, kd, cd = tabs
    s = _ret_specs(nC, lambda h, b, c: (b, h, nC - 1 - c))

    def body(q_ref, k_ref, v_ref, g_ref, o_ref, st_ref, d_ref, cos_ref, sin_ref, dm_ref, qd_ref, kd_ref, cd_ref,
             gn_ref, dq_ref, dk_ref, dv_ref, dg_ref, dgn_ref, dstate):
        b, c = pl.program_id(1), pl.program_id(2)

        @pl.when(c == 0)
        def _():
            dstate[...] = jnp.zeros_like(dstate)

        @pl.when((b == 0) & (c == 0))
        def _():
            dgn_ref[...] = jnp.zeros_like(dgn_ref)

        cs, sn = cos_ref[...], sin_ref[...]
        qf = _rope_halves(q_ref[...].astype(F32), cs, sn)
        kf = _rope_halves(k_ref[...].astype(F32), cs, sn) * (RET_DK ** -0.5)
        v = v_ref[...]
        gnv = gn_ref[...]
        o = o_ref[...]
        r = lax.rsqrt(jnp.mean(o * o, axis=-1, keepdims=True) + RMS_EPS)
        oh = o * r
        gf = g_ref[...].astype(F32)
        sg = jax.nn.sigmoid(gf)
        d = d_ref[...].astype(F32)
        dg_ref[...] = (d * (oh * gnv) * (sg * (1.0 + gf * (1.0 - sg)))).astype(BF16)
        don = d * (gf * sg)
        dgn_ref[...] += jnp.sum(don * oh, axis=0, keepdims=True)
        doh = don * gnv
        dO = (r * (doh - oh * jnp.mean(doh * oh, axis=-1, keepdims=True))).astype(BF16)
        dmv = dm_ref[...]
        qb, kb = qf.astype(BF16), kf.astype(BF16)
        p = (_dot(qb, kb, NT) * dmv).astype(BF16)
        dp = (_dot(dO, v, NT) * dmv).astype(BF16)
        st = st_ref[...]
        dsn = dstate[...]
        dsb = dsn.astype(BF16)
        qdv, kdv = qd_ref[...], kd_ref[...]
        dq = _dot(dp, kb, NN) + _dot(dO, st, NT) * qdv
        dk = _dot(dp, qb, TN) + _dot(v, dsb, NT) * kdv
        dv = _dot(p, dO, TN) + _dot((kf * kdv).astype(BF16), dsb, NN)
        dstate[...] = dsn * cd_ref[...] + _dot((qf * qdv).astype(BF16), dO, TN)
        dq_ref[...] = _unrope_halves(dq, cs, sn).astype(BF16)
        dk_ref[...] = (_unrope_halves(dk, cs, sn) * (RET_DK ** -0.5)).astype(BF16)
        dv_ref[...] = dv.astype(BF16)

    return _pcall(
        body, name="ret_bwd", grid=(RET_H, B, nC),
        in_specs=[s["q"], s["k"], s["v"], s["g"], s["wide"], s["st"], s["wide"], s["cs"], s["cs"], s["dm"], s["dv"],
                  s["dv"], s["cd"], s["gn"]],
        out_specs=[s["narrow"], s["narrow"], s["wide"], s["wide"], s["gn"]],
        out_shape=[jax.ShapeDtypeStruct((T, RET_H * RET_DK), BF16), jax.ShapeDtypeStruct((T, RET_H * RET_DK), BF16),
                   jax.ShapeDtypeStruct((T, RET_H * RET_DV), BF16), jax.ShapeDtypeStruct((T, RET_H * RET_DV), BF16),
                   jax.ShapeDtypeStruct((RET_H, 1, RET_DV), F32)],
        scratch_shapes=[pltpu.VMEM((RET_DK, RET_DV), F32)], sem=("arbitrary", "arbitrary", "arbitrary"),
        args=(proj, proj, proj, proj, o_raw, states, dgt, cos, sin, dm, qd, kd, cd, gn), comm=comm)


def _mla_tables(S):
    half = MLA_ROPE // 2
    inv = ROPE_THETA ** (-jnp.arange(half, dtype=F32) / half)
    ang = jnp.arange(S).astype(F32)[:, None] * inv[None, :]
    cos, sin = jnp.cos(ang), jnp.sin(ang)
    i = jnp.arange(MLA_ROPE)
    swap = (i[:, None] == (i[None, :] + half) % MLA_ROPE).astype(F32)
    return jnp.concatenate([cos, cos], axis=-1), jnp.concatenate([-sin, sin], axis=-1), swap


def _swap_halves(t, swap):
    return jnp.dot(t, swap, precision=lax.Precision.HIGHEST, preferred_element_type=F32)


def _head_norm_rope(n, r_, gn, gr, cos, sin, swap, scale):
    ssq = jnp.sum(n * n, axis=-1, keepdims=True) + jnp.sum(r_ * r_, axis=-1, keepdims=True)
    rstd = lax.rsqrt(ssq * (1.0 / MLA_QK) + RMS_EPS)
    yn = n * rstd * gn
    yr = r_ * rstd * gr
    yr = yr * cos + _swap_halves(yr, swap) * sin
    if scale != 1.0:
        yn, yr = yn * scale, yr * scale
    return yn, yr, rstd


def _head_norm_rope_bwd(dn, dr, n, r_, gn, gr, cos, sin, swap, scale):
    ssq = jnp.sum(n * n, axis=-1, keepdims=True) + jnp.sum(r_ * r_, axis=-1, keepdims=True)
    rstd = lax.rsqrt(ssq * (1.0 / MLA_QK) + RMS_EPS)
    hn, hr = n * rstd, r_ * rstd
    dyn = dn * scale if scale != 1.0 else dn
    dr = dr * scale if scale != 1.0 else dr
    dyr = dr * cos + _swap_halves(dr * sin, swap)
    dgn = jnp.sum(dyn * hn, axis=0, keepdims=True)
    dgr = jnp.sum(dyr * hr, axis=0, keepdims=True)
    dhn, dhr = dyn * gn, dyr * gr
    mt = (jnp.sum(dhn * hn, axis=-1, keepdims=True) + jnp.sum(dhr * hr, axis=-1, keepdims=True)) * (1.0 / MLA_QK)
    return rstd * (dhn - hn * mt), rstd * (dhr - hr * mt), dgn, dgr


MLA_PAD = 256


def _diag_bias():
    i = jnp.arange(ATT_TQ)
    return jnp.where((i[None, :] // CHUNK) <= (i[:, None] // CHUNK), 0.0, MASK_VALUE).astype(F32)


def _store_padded(dst, rows, n, r_):
    dst[rows, :MLA_NOPE] = n.astype(BF16)
    dst[rows, MLA_NOPE:MLA_QK] = r_.astype(BF16)
    dst[rows, MLA_QK:] = jnp.zeros((n.shape[0], MLA_PAD - MLA_QK), BF16)


def _mla_fwd(q_raw, kv, kr, gains, tabs, B, S, comm=None):
    T = B * S
    TQ = ATT_TQ
    nQ = S // TQ
    qgn, qgr, kgn, kgr = gains
    cos, sin, swap = tabs
    scale = MLA_QK ** -0.5

    def body(q_ref, kv_ref, kr_ref, qgn_ref, qgr_ref, kgn_ref, kgr_ref, c_ref, s_ref, sw_ref, bias_ref,
             o_ref, lse_ref, qf_s, kf_s, v_s):
        sw = sw_ref[...]

        def prep(t, _):
            rows = pl.ds(pl.multiple_of(t * TQ, TQ), TQ)
            cs, sn = c_ref[rows, :], s_ref[rows, :]
            qn, qr, _ = _head_norm_rope(q_ref[rows, :MLA_NOPE], q_ref[rows, MLA_NOPE:], qgn_ref[...], qgr_ref[...],
                                        cs, sn, sw, scale)
            _store_padded(qf_s, rows, qn, qr)
            kn, krr, _ = _head_norm_rope(kv_ref[rows, :MLA_NOPE], kr_ref[rows, :], kgn_ref[...], kgr_ref[...],
                                         cs, sn, sw, 1.0)
            _store_padded(kf_s, rows, kn, krr)
            v_s[rows, :] = kv_ref[rows, MLA_NOPE:].astype(BF16)
            return 0

        lax.fori_loop(0, nQ, prep, 0)
        for i in range(nQ):
            rows = slice(i * TQ, (i + 1) * TQ)
            q = qf_s[rows, :]
            sd = _dot(q, kf_s[rows, :], NT) + bias_ref[...]
            m = jnp.max(sd, axis=-1, keepdims=True)
            if i:
                sl = _dot(q, kf_s[:i * TQ, :], NT)
                m = jnp.maximum(m, jnp.max(sl, axis=-1, keepdims=True))
            pd = jnp.exp(sd - m)
            l = jnp.sum(pd, axis=-1, keepdims=True)
            acc = _dot(pd.astype(BF16), v_s[rows, :], NN)
            if i:
                pl_ = jnp.exp(sl - m)
                l = l + jnp.sum(pl_, axis=-1, keepdims=True)
                acc = acc + _dot(pl_.astype(BF16), v_s[:i * TQ, :], NN)
            o_ref[rows, :] = (acc / l).astype(BF16)
            lse_ref[rows, :] = m + jnp.log(l)

    def vec(n):
        return _bs((1, n), lambda b, h: (0, 0))

    def head(n):
        return _bs((None, S, n), lambda b, h: (h, b, 0))

    tab = _bs((S, MLA_ROPE), lambda b, h: (0, 0))
    return _pcall(
        body, name="mla_fwd", grid=(B, MLA_H),
        in_specs=[head(MLA_QK), head(MLA_NOPE + MLA_V), _bs((S, MLA_ROPE), lambda b, h: (b, 0)),
                  vec(MLA_NOPE), vec(MLA_ROPE), vec(MLA_NOPE), vec(MLA_ROPE), tab, tab,
                  _bs((MLA_ROPE, MLA_ROPE), lambda b, h: (0, 0)), _bs((TQ, TQ), lambda b, h: (0, 0))],
        out_specs=[head(MLA_V), head(1)],
        out_shape=[jax.ShapeDtypeStruct((MLA_H, T, MLA_V), BF16), jax.ShapeDtypeStruct((MLA_H, T, 1), F32)],
        scratch_shapes=[pltpu.VMEM((S, MLA_PAD), BF16), pltpu.VMEM((S, MLA_PAD), BF16), pltpu.VMEM((S, MLA_V), BF16)],
        sem=("parallel", "parallel"), args=(q_raw, kv, kr, qgn, qgr, kgn, kgr, cos, sin, swap, _diag_bias()), comm=comm)


def _mla_bwd(q_raw, kv, kr, o, lse, do, gains, tabs, B, S, comm=None):
    T = B * S
    TQ = ATT_TQ
    nQ = S // TQ
    qgn, qgr, kgn, kgr = gains
    cos, sin, swap = tabs
    scale = MLA_QK ** -0.5

    def body(q_ref, kv_ref, kr_ref, o_ref, lse_ref, do_ref, qgn_ref, qgr_ref, kgn_ref, kgr_ref, c_ref, s_ref, sw_ref,
             bias_ref, dq_ref, dkv_ref, dkr_ref, dqgn_ref, dqgr_ref, dkgn_ref, dkgr_ref,
             qf_s, kf_s, v_s, dl_s, dq_s, dk_s, dv_s):
        b, h = pl.program_id(0), pl.program_id(1)
        sw = sw_ref[...]

        def blk(t):
            return pl.ds(pl.multiple_of(t * TQ, TQ), TQ)

        def prep(t, _):
            rows = blk(t)
            cs, sn = c_ref[rows, :], s_ref[rows, :]
            qn, qr, _ = _head_norm_rope(q_ref[rows, :MLA_NOPE], q_ref[rows, MLA_NOPE:], qgn_ref[...], qgr_ref[...],
                                        cs, sn, sw, scale)
            _store_padded(qf_s, rows, qn, qr)
            kn, krr, _ = _head_norm_rope(kv_ref[rows, :MLA_NOPE], kr_ref[rows, :], kgn_ref[...], kgr_ref[...],
                                         cs, sn, sw, 1.0)
            _store_padded(kf_s, rows, kn, krr)
            v_s[rows, :] = kv_ref[rows, MLA_NOPE:].astype(BF16)
            dl_s[rows, :] = jnp.sum(do_ref[rows, :].astype(F32) * o_ref[rows, :].astype(F32), axis=-1, keepdims=True)
            dk_s[rows, :] = jnp.zeros((TQ, MLA_PAD), F32)
            dv_s[rows, :] = jnp.zeros((TQ, MLA_V), F32)
            return 0

        lax.fori_loop(0, nQ, prep, 0)

        gqn, gqr = jnp.zeros((1, MLA_NOPE), F32), jnp.zeros((1, MLA_ROPE), F32)
        for i in range(nQ):
            rows = slice(i * TQ, (i + 1) * TQ)
            q, doi, lse_i, dl_i = qf_s[rows, :], do_ref[rows, :], lse_ref[rows, :], dl_s[rows, :]

            def part(cols, bias):
                k, v = kf_s[cols, :], v_s[cols, :]
                s = _dot(q, k, NT)
                if bias is not None:
                    s = s + bias
                p = jnp.exp(s - lse_i)
                ds = (p * (_dot(doi, v, NT) - dl_i)).astype(BF16)
                dk_s[cols, :] += _dot(ds, q, TN)
                dv_s[cols, :] += _dot(p.astype(BF16), doi, TN)
                return _dot(ds, k, NN)

            dq = part(rows, bias_ref[...])
            if i:
                dq = dq + part(slice(0, i * TQ), None)
            dq_s[...] = dq
            dqn, dqr, a0, a1 = _head_norm_rope_bwd(dq_s[:, :MLA_NOPE], dq_s[:, MLA_NOPE:MLA_QK], q_ref[rows, :MLA_NOPE],
                                                   q_ref[rows, MLA_NOPE:], qgn_ref[...], qgr_ref[...], c_ref[rows, :],
                                                   s_ref[rows, :], sw, scale)
            dq_ref[rows, :MLA_NOPE] = dqn.astype(BF16)
            dq_ref[rows, MLA_NOPE:] = dqr.astype(BF16)
            gqn, gqr = gqn + a0, gqr + a1

        def post(t, carry):
            rows = blk(t)
            dkn, dkr, a2, a3 = _head_norm_rope_bwd(dk_s[rows, :MLA_NOPE], dk_s[rows, MLA_NOPE:MLA_QK],
                                                   kv_ref[rows, :MLA_NOPE], kr_ref[rows, :], kgn_ref[...], kgr_ref[...],
                                                   c_ref[rows, :], s_ref[rows, :], sw, 1.0)
            dkv_ref[rows, :MLA_NOPE] = dkn.astype(BF16)
            dkv_ref[rows, MLA_NOPE:] = dv_s[rows, :].astype(BF16)

            @pl.when(h == 0)
            def _():
                dkr_ref[rows, :] = dkr

            @pl.when(h > 0)
            def _():
                dkr_ref[rows, :] += dkr

            return carry[0] + a2, carry[1] + a3

        gkn, gkr = lax.fori_loop(0, nQ, post, (jnp.zeros((1, MLA_NOPE), F32), jnp.zeros((1, MLA_ROPE), F32)))
        first = (b == 0) & (h == 0)

        @pl.when(first)
        def _():
            dqgn_ref[...] = gqn
            dqgr_ref[...] = gqr
            dkgn_ref[...] = gkn
            dkgr_ref[...] = gkr

        @pl.when(jnp.logical_not(first))
        def _():
            dqgn_ref[...] += gqn
            dqgr_ref[...] += gqr
            dkgn_ref[...] += gkn
            dkgr_ref[...] += gkr

    def vec(n):
        return _bs((1, n), lambda b, h: (0, 0))

    def head(n):
        return _bs((None, S, n), lambda b, h: (h, b, 0))

    tab = _bs((S, MLA_ROPE), lambda b, h: (0, 0))
    return _pcall(
        body, name="mla_bwd", grid=(B, MLA_H),
        in_specs=[head(MLA_QK), head(MLA_NOPE + MLA_V), _bs((S, MLA_ROPE), lambda b, h: (b, 0)), head(MLA_V), head(1),
                  head(MLA_V), vec(MLA_NOPE), vec(MLA_ROPE), vec(MLA_NOPE), vec(MLA_ROPE), tab, tab,
                  _bs((MLA_ROPE, MLA_ROPE), lambda b, h: (0, 0)), _bs((TQ, TQ), lambda b, h: (0, 0))],
        out_specs=[head(MLA_QK), head(MLA_NOPE + MLA_V), _bs((S, MLA_ROPE), lambda b, h: (b, 0)),
                   vec(MLA_NOPE), vec(MLA_ROPE), vec(MLA_NOPE), vec(MLA_ROPE)],
        out_shape=[jax.ShapeDtypeStruct((MLA_H, T, MLA_QK), BF16), jax.ShapeDtypeStruct((MLA_H, T, MLA_NOPE + MLA_V), BF16),
                   jax.ShapeDtypeStruct((T, MLA_ROPE), F32), jax.ShapeDtypeStruct((1, MLA_NOPE), F32),
                   jax.ShapeDtypeStruct((1, MLA_ROPE), F32), jax.ShapeDtypeStruct((1, MLA_NOPE), F32),
                   jax.ShapeDtypeStruct((1, MLA_ROPE), F32)],
        scratch_shapes=[pltpu.VMEM((S, MLA_PAD), BF16), pltpu.VMEM((S, MLA_PAD), BF16), pltpu.VMEM((S, MLA_V), BF16),
                        pltpu.VMEM((S, 1), F32), pltpu.VMEM((TQ, MLA_PAD), F32), pltpu.VMEM((S, MLA_PAD), F32),
                        pltpu.VMEM((S, MLA_V), F32)],
        sem=("arbitrary", "arbitrary"),
        args=(q_raw, kv, kr, o, lse, do, qgn, qgr, kgn, kgr, cos, sin, swap, _diag_bias()), comm=comm)


def _adamw(name, recvs, w, m, v, tr=None, comm=None):
    n, R, C = recvs[0].shape
    L = len(recvs)
    tr = R if tr is None else tr
    per = R // tr
    c1 = 1.0 - ADAM_B1 ** ADAM_STEP
    c2 = 1.0 - ADAM_B2 ** ADAM_STEP

    def body(*refs):
        r_refs = refs[:L]
        w_ref, m_ref, v_ref, g_ref, d_ref, nm_ref, nv_ref = refs[L:]
        layer = pl.program_id(0) // per

        def total(r_ref):
            t = r_ref[0].astype(F32)
            for k in range(1, n):
                t = t + r_ref[k].astype(F32)
            return t

        g = total(r_refs[0])
        for l in range(1, L):
            g = jnp.where(layer == l, total(r_refs[l]), g)
        mm = ADAM_B1 * m_ref[...] + (1.0 - ADAM_B1) * g
        vv = ADAM_B2 * v_ref[...] + (1.0 - ADAM_B2) * (g * g)
        g_ref[...] = g
        nm_ref[...] = mm
        nv_ref[...] = vv
        d_ref[...] = -ADAM_LR * ((mm / c1) / (jnp.sqrt(vv / c2) + ADAM_EPS) + ADAM_WD * w_ref[...])

    blk = _bs((tr, C), lambda i: (i, 0))
    r_specs = [_bs((n, tr, C), functools.partial(lambda l, i: (0, jnp.clip(i - l * per, 0, per - 1), 0), l))
               for l in range(L)]
    outs, got = _pcall(body, name=name, grid=(L * per,), in_specs=r_specs + [blk, blk, blk], out_specs=[blk] * 4,
                       out_shape=[jax.ShapeDtypeStruct((L * R, C), F32)] * 4, scratch_shapes=[], sem=("arbitrary",),
                       args=(*recvs, w, m, v), comm=comm)
    return outs if comm is None else (outs, got)


def _sum8(name, a):
    n, R, C = a.shape

    def body(a_ref, o_ref):
        s = a_ref[0]
        for k in range(1, n):
            s = s + a_ref[k]
        o_ref[...] = s

    return pl.pallas_call(body, name=name, out_shape=jax.ShapeDtypeStruct((R, C), a.dtype))(a)


def _sds(shape, dt):
    return jax.ShapeDtypeStruct(shape, dt)


def _proj_shared(name, h, w, out_dtype, tm=1024, comm=None):
    T, K = h.shape
    J, _, n = w.shape
    return _mm(name, h, w, grid=(T // tm, J), a_spec=_bs((tm, K), lambda m, j: (m, 0)),
               b_spec=_bs((None, K, n), lambda m, j: (j, 0, 0)), o_spec=_bs((None, tm, n), lambda m, j: (j, m, 0)),
               out_shape=_sds((J, T, n), out_dtype), dims=NN, comm=comm)


def _proj_shared_dx(name, d, w, tm=1024, comm=None):
    J, T, n = d.shape
    K = w.shape[1]
    return _mm(name, d, w, grid=(T // tm, J), a_spec=_bs((None, tm, n), lambda m, k: (k, m, 0)),
               b_spec=_bs((None, K, n), lambda m, k: (k, 0, 0)), o_spec=_bs((tm, K), lambda m, k: (m, 0)),
               out_shape=_sds((T, K), F32), dims=NT, kax=1, acc_shape=(tm, K), comm=comm)


def _proj_shared_dw(name, h, d, tt=1024, comm=None):
    T, K = h.shape
    J, _, n = d.shape
    return _mm(name, h, d, grid=(J, T // tt), a_spec=_bs((tt, K), lambda j, t: (t, 0)),
               b_spec=_bs((None, tt, n), lambda j, t: (j, t, 0)), o_spec=_bs((None, K, n), lambda j, t: (j, 0, 0)),
               out_shape=_sds((J, K, n), BF16), dims=TN, kax=1, acc_shape=(K, n), comm=comm)


def _out_proj(name, a, w, res, tm=512):
    J, T, k = a.shape
    N = w.shape[2]
    return _mm(name, a, w, grid=(T // tm,), a_spec=_bs((J, tm, k), lambda m: (0, m, 0)),
               b_spec=_bs((J, k, N), lambda m: (0, 0, 0)), o_spec=_bs((tm, N), lambda m: (m, 0)),
               out_shape=_sds((T, N), F32), dims=NN, res=res, res_spec=_bs((tm, N), lambda m: (m, 0)), jb=J)


def _out_proj_dx(name, dx, w, tm=512, comm=None):
    T, N = dx.shape
    J, k, _ = w.shape
    return _mm(name, dx, w, grid=(T // tm, J), a_spec=_bs((tm, N), lambda m, j: (m, 0)),
               b_spec=_bs((None, k, N), lambda m, j: (j, 0, 0)), o_spec=_bs((None, tm, k), lambda m, j: (j, m, 0)),
               out_shape=_sds((J, T, k), BF16), dims=NT, comm=comm)


def _out_proj_dw(name, a, dx, tt=512, comm=None):
    J, T, k = a.shape
    N = dx.shape[1]
    return _mm(name, a, dx, grid=(J, T // tt), a_spec=_bs((None, tt, k), lambda j, t: (j, t, 0)),
               b_spec=_bs((tt, N), lambda j, t: (t, 0)), o_spec=_bs((None, k, N), lambda j, t: (j, 0, 0)),
               out_shape=_sds((J, k, N), BF16), dims=TN, kax=1, acc_shape=(k, N), comm=comm)


def _bf16(x):
    return x.astype(BF16)


def _ffn_fwd(i, x, norm_g, w_in, cw, cb, w_out, B, S, comm_in=None):
    h = _rms_fwd(f"ffn{i}_norm", x, norm_g)
    u = _proj_shared(f"ffn{i}_in", h, w_in, BF16, comm=comm_in)
    u, got = u if comm_in is not None else (u, None)
    u4 = u.reshape(2, 4, u.shape[1], FSH)
    gt = _convffn_fwd(f"ffn{i}_gate", u4, cw, cb, B, S)
    y = _out_proj(f"ffn{i}_out", gt, w_out, x)
    return y, (x, h, u4, gt), got


def _ffn_bwd(i, dy, saved, norm_g, w_in, cw, cb, w_out, B, S, comm_out_dx=None):
    x, h, u4, gt = saved
    dyb = _bf16(dy)
    dgt = _out_proj_dx(f"ffn{i}_out_dx", dyb, w_out, comm=comm_out_dx)
    dgt, got0 = dgt if comm_out_dx is not None else (dgt, None)
    dw_out = _out_proj_dw(f"ffn{i}_out_dw", gt, dyb).reshape(NDEV, FSH // 2, D_MODEL)
    du4, dcw, dcb = _convffn_bwd(f"ffn{i}_gate_bwd", u4, cw, cb, dgt, B, S)
    du = du4.reshape(NDEV, du4.shape[2], FSH)
    dh, (r_out,) = _proj_shared_dx(f"ffn{i}_in_dx", du, w_in, comm=_Exchange([dw_out]))
    dw_in = _proj_shared_dw(f"ffn{i}_in_dw", h, du)
    dx, dgn = _rms_bwd(f"ffn{i}_norm_bwd", x, norm_g, dh, dres=dy)
    return dx, dict(w_in=dw_in, norm=dgn, cw=dcw, cb=dcb), got0, r_out


def kernel(x, ret_norm, ret_w_in, ret_gn, ret_w_out, mla_norm, mla_w_in, mla_q_norm, mla_w_qb, mla_kv_norm, mla_w_kvb, mla_q_head_norm, mla_k_head_norm, mla_w_out, ffn_norm, ffn_w_in, ffn_conv_w, ffn_conv_b, ffn_w_out, loss_target, m_ret_norm, m_ret_w_in, m_ret_gn, m_ret_w_out, m_mla_norm, m_mla_w_in, m_mla_q_norm, m_mla_w_qb, m_mla_kv_norm, m_mla_w_kvb, m_mla_q_head_norm, m_mla_k_head_norm, m_mla_w_out, m_ffn_norm, m_ffn_w_in, m_ffn_conv_w, m_ffn_conv_b, m_ffn_w_out, v_ret_norm, v_ret_w_in, v_ret_gn, v_ret_w_out, v_mla_norm, v_mla_w_in, v_mla_q_norm, v_mla_w_qb, v_mla_kv_norm, v_mla_w_kvb, v_mla_q_head_norm, v_mla_k_head_norm, v_mla_w_out, v_ffn_norm, v_ffn_w_in, v_ffn_conv_w, v_ffn_conv_b, v_ffn_w_out):
    B, S, D = x.shape
    T = B * S
    w = dict(ret_norm=ret_norm, ret_w_in=ret_w_in, ret_gn=ret_gn, ret_w_out=ret_w_out, mla_norm=mla_norm,
             mla_w_in=mla_w_in, mla_q_norm=mla_q_norm, mla_w_qb=mla_w_qb, mla_kv_norm=mla_kv_norm, mla_w_kvb=mla_w_kvb,
             mla_q_head_norm=mla_q_head_norm, mla_k_head_norm=mla_k_head_norm, mla_w_out=mla_w_out, ffn_norm=ffn_norm,
             ffn_w_in=ffn_w_in, ffn_conv_w=ffn_conv_w, ffn_conv_b=ffn_conv_b, ffn_w_out=ffn_w_out)
    mom = dict(ret_norm=m_ret_norm, ret_w_in=m_ret_w_in, ret_gn=m_ret_gn, ret_w_out=m_ret_w_out, mla_norm=m_mla_norm,
               mla_w_in=m_mla_w_in, mla_q_norm=m_mla_q_norm, mla_w_qb=m_mla_w_qb, mla_kv_norm=m_mla_kv_norm,
               mla_w_kvb=m_mla_w_kvb, mla_q_head_norm=m_mla_q_head_norm, mla_k_head_norm=m_mla_k_head_norm,
               mla_w_out=m_mla_w_out, ffn_norm=m_ffn_norm, ffn_w_in=m_ffn_w_in, ffn_conv_w=m_ffn_conv_w,
               ffn_conv_b=m_ffn_conv_b, ffn_w_out=m_ffn_w_out)
    var = dict(ret_norm=v_ret_norm, ret_w_in=v_ret_w_in, ret_gn=v_ret_gn, ret_w_out=v_ret_w_out, mla_norm=v_mla_norm,
               mla_w_in=v_mla_w_in, mla_q_norm=v_mla_q_norm, mla_w_qb=v_mla_w_qb, mla_kv_norm=v_mla_kv_norm,
               mla_w_kvb=v_mla_w_kvb, mla_q_head_norm=v_mla_q_head_norm, mla_k_head_norm=v_mla_k_head_norm,
               mla_w_out=v_mla_w_out, ffn_norm=v_ffn_norm, ffn_w_in=v_ffn_w_in, ffn_conv_w=v_ffn_conv_w,
               ffn_conv_b=v_ffn_conv_b, ffn_w_out=v_ffn_w_out)
    BIG = ["ret_w_in", "ret_w_out", "mla_w_in", "mla_w_qb", "mla_w_kvb", "mla_w_out", "ffn_w_in", "ffn_w_out"]
    REPL = ["ret_norm", "ffn_norm", "mla_q_head_norm", "mla_k_head_norm", "ffn_conv_b"]
    SHARDED_SMALL = ["ffn_conv_w", "ret_gn", "mla_norm", "mla_q_norm", "mla_kv_norm"]
    dev = _idx(_place())

    def blk16(k, i=0):
        return _bf16(w[k][i])

    small_vec = jnp.concatenate([w[k].reshape(-1) for k in SHARDED_SMALL])
    n_small = small_vec.shape[0]
    small_vec = jnp.pad(small_vec, (0, 3072 - n_small)).reshape(24, 128)
    Wret_in, sg = _comm_call("gather_ret_w_in", _Gather([blk16("ret_w_in"), small_vec]))
    sg = sg.reshape(NDEV, 3072)
    o0 = 0
    conv_w_full = sg[:, o0:o0 + 2112].reshape(NDEV, 2, 3, 352).transpose(1, 2, 0, 3).reshape(2, 3, FFN)
    o0 += 2112
    ret_gn_full = sg[:, o0:o0 + 256].reshape(NDEV, RET_H, 64).transpose(1, 0, 2).reshape(RET_H, 1, RET_DV)
    o0 += 256
    mla_norm_full = sg[:, o0:o0 + 128].reshape(1, D)
    o0 += 128
    q_norm_full = sg[:, o0:o0 + 48].reshape(1, MLA_QR)
    o0 += 48
    kv_norm_full = sg[:, o0:o0 + 32].reshape(1, MLA_KVR)

    cw = [conv_w_full[i].reshape(3, 4, FSH).transpose(1, 0, 2) for i in range(2)]
    cb = [ffn_conv_b[i].reshape(4, 1, FSH) for i in range(2)]
    fnorm = [ffn_norm[i].reshape(1, D) for i in range(2)]
    rtabs = _ret_tables(S)
    mtabs = _mla_tables(S)
    qh, kh = mla_q_head_norm.reshape(1, MLA_QK), mla_k_head_norm.reshape(1, MLA_QK)
    gains = (qh[:, :MLA_NOPE], qh[:, MLA_NOPE:], kh[:, :MLA_NOPE], kh[:, MLA_NOPE:])

    x0 = x.reshape(T, D)
    tgt = loss_target.reshape(T, D)
    h0 = _rms_fwd("ret_norm", x0, ret_norm.reshape(1, D))
    proj, (Wret_out, Wffn_out0) = _mm(
        "ret_in", h0, Wret_in, grid=(T // 1024, NDEV), a_spec=_bs((1024, D), lambda m, j: (m, 0)),
        b_spec=_bs((None, D, 768), lambda m, j: (j, 0, 0)), o_spec=_bs((1024, 768), lambda m, j: (m, j)),
        out_shape=_sds((T, 6144), BF16), dims=NN, comm=_Gather([blk16("ret_w_out"), blk16("ffn_w_out", 0)]))
    Wret_out = Wret_out.reshape(RET_H * RET_DV, D)
    Wffn_out0 = Wffn_out0.reshape(4, FSH, D)
    (o_raw, rgt, states), (Wffn_in0,) = _ret_fwd(proj, rtabs, ret_gn_full, B, S, comm=_Gather([blk16("ffn_w_in", 0)]))
    x1 = _mm("ret_out", rgt, Wret_out, grid=(T // 512,),
             a_spec=_bs((512, RET_H * RET_DV), lambda m: (m, 0)), b_spec=_bs((RET_H * RET_DV, D), lambda m: (0, 0)),
             o_spec=_bs((512, D), lambda m: (m, 0)), out_shape=_sds((T, D), F32), dims=NN, res=x0,
             res_spec=_bs((512, D), lambda m: (m, 0)))
    MLA_W = ["mla_w_in", "mla_w_qb", "mla_w_kvb", "mla_w_out"]
    x2, ffn0_saved, got = _ffn_fwd(0, x1, fnorm[0], Wffn_in0, cw[0], cb[0], Wffn_out0, B, S,
                                   comm_in=_Gather([blk16(k) for k in MLA_W]))
    Wmla_in = got[0].reshape(D, MLA_QR + MLA_KVR + MLA_ROPE)
    Wq, Wkv, Wkr = Wmla_in[:, :MLA_QR], Wmla_in[:, MLA_QR:MLA_QR + MLA_KVR], Wmla_in[:, MLA_QR + MLA_KVR:]
    Wqb, Wkvb, Wmla_out = got[1:]

    h2 = _rms_fwd("mla_norm", x2, mla_norm_full)

    def small_proj(name, wmat):
        n = wmat.shape[1]
        return _mm(name, h2, wmat, grid=(T // 512,), a_spec=_bs((512, D), lambda m: (m, 0)),
                   b_spec=_bs((D, n), lambda m: (0, 0)), o_spec=_bs((512, n), lambda m: (m, 0)),
                   out_shape=_sds((T, n), F32), dims=NN)

    c_q, c_kv, k_rope = small_proj("mla_in_q", Wq), small_proj("mla_in_kv", Wkv), small_proj("mla_in_kr", Wkr)
    cqn = _rms_fwd("mla_q_norm", c_q, q_norm_full)
    ckvn = _rms_fwd("mla_kv_norm", c_kv, kv_norm_full)
    q_raw = _proj_shared("mla_qb", cqn, Wqb, F32)
    kvh = _proj_shared("mla_kvb", ckvn, Wkvb, F32)
    (att, lse), (Wffn_in1, Wffn_out1) = _mla_fwd(q_raw, kvh, k_rope, gains, mtabs, B, S,
                                                 comm=_Gather([blk16("ffn_w_in", 1), blk16("ffn_w_out", 1)]))
    Wffn_out1 = Wffn_out1.reshape(4, FSH, D)
    x3 = _out_proj("mla_out", att, Wmla_out, x2)
    y, ffn1_saved, _ = _ffn_fwd(1, x3, fnorm[1], Wffn_in1, cw[1], cb[1], Wffn_out1, B, S)

    dy, colsq = _loss(y, tgt)
    loss = lax.psum(0.5 * jnp.sum(colsq) / D, ("x", "y", "c"))

    dx3, gf1, _, r_ffn1_out = _ffn_bwd(1, dy, ffn1_saved, fnorm[1], Wffn_in1, cw[1], cb[1], Wffn_out1, B, S)
    dx3b = _bf16(dx3)
    datt = _out_proj_dx("mla_out_dx", dx3b, Wmla_out)
    (dq_raw, dkvh, dkr, dqgn, dqgr, dkgn, dkgr), (r_ffn1_in,) = _mla_bwd(
        q_raw, kvh, k_rope, att, lse, datt, gains, mtabs, B, S, comm=_Exchange([gf1["w_in"]]))
    dcqn = _proj_shared_dx("mla_qb_dx", dq_raw, Wqb)
    dckvn = _proj_shared_dx("mla_kvb_dx", dkvh, Wkvb)
    dcq, dg_qn = _rms_bwd("mla_q_norm_bwd", c_q, q_norm_full, dcqn)
    dckv, dg_kvn = _rms_bwd("mla_kv_norm_bwd", c_kv, kv_norm_full, dckvn)
    dproj2 = _bf16(jnp.concatenate([dcq, dckv, dkr], axis=-1))
    dh2 = _mm("mla_in_dx", dproj2, Wmla_in, grid=(T // 512,), a_spec=_bs((512, 704), lambda m: (m, 0)),
              b_spec=_bs((D, 704), lambda m: (0, 0)), o_spec=_bs((512, D), lambda m: (m, 0)),
              out_shape=_sds((T, D), F32), dims=NT)
    dx2, dg_mla_norm = _rms_bwd("mla_norm_bwd", x2, mla_norm_full, dh2, dres=dx3)

    dx1, gf0, _, r_ffn0_out = _ffn_bwd(0, dx2, ffn0_saved, fnorm[0], Wffn_in0, cw[0], cb[0], Wffn_out0, B, S)
    dx1b = _bf16(dx1)
    drgt = _mm("ret_out_dx", dx1b, Wret_out, grid=(T // 512, RET_H),
               a_spec=_bs((512, D), lambda m, j: (m, 0)), b_spec=_bs((RET_DV, D), lambda m, j: (j, 0)),
               o_spec=_bs((512, RET_DV), lambda m, j: (m, j)), out_shape=_sds((T, RET_H * RET_DV), BF16), dims=NT)
    dWret_out = _mm("ret_out_dw", rgt, dx1b, grid=(RET_H, T // 512), a_spec=_bs((512, RET_DV), lambda j, t: (t, j)),
                    b_spec=_bs((512, D), lambda j, t: (t, 0)), o_spec=_bs((RET_DV, D), lambda j, t: (j, 0)),
                    out_shape=_sds((RET_H * RET_DV, D), BF16), dims=TN, kax=1,
                    acc_shape=(RET_DV, D)).reshape(NDEV, 256, D)
    (dq, dk, dv, dg, dgn_ret), (r_ffn0_in,) = _ret_bwd(proj, o_raw, states, drgt, rtabs, ret_gn_full, B, S,
                                                       comm=_Exchange([gf0["w_in"]]))
    dproj = jnp.concatenate([dq, dk, dv, dg], axis=-1)
    dWret_in, (r_ret_out,) = _mm(
        "ret_in_dw", h0, dproj, grid=(NDEV, T // 1024), a_spec=_bs((1024, D), lambda j, t: (t, 0)),
        b_spec=_bs((1024, 768), lambda j, t: (t, j)), o_spec=_bs((None, D, 768), lambda j, t: (j, 0, 0)),
        out_shape=_sds((NDEV, D, 768), BF16), dims=TN, kax=1, acc_shape=(D, 768), comm=_Exchange([dWret_out]))
    half = D // 2
    dh0, (r_ret_in_a,) = _mm(
        "ret_in_dx", dproj, Wret_in, grid=(T // 1024, NDEV), a_spec=_bs((1024, 768), lambda m, k: (m, k)),
        b_spec=_bs((None, D, 768), lambda m, k: (k, 0, 0)), o_spec=_bs((1024, D), lambda m, k: (m, 0)),
        out_shape=_sds((T, D), F32), dims=NT, kax=1, acc_shape=(1024, D),
        comm=_Exchange([dWret_in], rows=[(0, half)]))
    dx0, dg_ret_norm = _rms_bwd("ret_norm_bwd", x0, ret_norm.reshape(1, D), dh0, dres=dx1)
    grad_x = dx0.reshape(B, S, D)
    dWmla_out, (r_ret_in_b,) = _out_proj_dw("mla_out_dw", att, dx3b,
                                            comm=_Exchange([dWret_in], rows=[(half, half)]))
    dWqb, (r_mla_out,) = _proj_shared_dw("mla_qb_dw", cqn, dq_raw, comm=_Exchange([dWmla_out]))
    dWkvb, (r_mla_qb,) = _proj_shared_dw("mla_kvb_dw", ckvn, dkvh, comm=_Exchange([dWqb]))
    dWmla_in, (r_mla_kvb,) = _mm(
        "mla_in_dw", h2, dproj2, grid=(T // 512,), a_spec=_bs((512, D), lambda t: (t, 0)),
        b_spec=_bs((512, 704), lambda t: (t, 0)), o_spec=_bs((D, 704), lambda t: (0, 0)),
        out_shape=_sds((D, 704), BF16), dims=TN, kax=0, acc_shape=(D, 704), comm=_Exchange([dWkvb]))
    dWmla_in = dWmla_in.reshape(NDEV, 128, 704)
    received = dict(ret_w_in=[r_ret_in_a, r_ret_in_b], ret_w_out=[r_ret_out], mla_w_qb=[r_mla_qb],
                    mla_w_kvb=[r_mla_kvb], mla_w_out=[r_mla_out], ffn_w_in=[r_ffn0_in, r_ffn1_in],
                    ffn_w_out=[r_ffn0_out, r_ffn1_out])

    dconv_w = jnp.stack([g_["cw"].transpose(1, 0, 2).reshape(3, FFN) for g_ in (gf0, gf1)])
    dconv_b = jnp.stack([g_["cb"].reshape(FFN) for g_ in (gf0, gf1)])
    small_parts = [dg_ret_norm, gf0["norm"], gf1["norm"], dg_mla_norm, dg_qn, dg_kvn, dqgn, dqgr, dkgn, dkgr, dgn_ret,
                   dconv_w, dconv_b]
    small_g = jnp.concatenate([p.reshape(-1) for p in small_parts]).reshape(232, 128)
    small_all = _comm_call("gather_small_grads", _Gather([small_g]))[0]
    sred = _sum8("sum_small_grads", small_all).reshape(-1)

    def take(n):
        nonlocal off
        out = sred[off:off + n]
        off += n
        return out

    off = 0
    g_small = dict(ret_norm=take(D).reshape(1, D), ffn_norm=take(2 * D).reshape(2, D), mla_norm=take(D),
                   mla_q_norm=take(MLA_QR), mla_kv_norm=take(MLA_KVR))
    g_small["mla_q_head_norm"] = take(MLA_QK).reshape(1, MLA_QK)
    g_small["mla_k_head_norm"] = take(MLA_QK).reshape(1, MLA_QK)
    g_small["ret_gn"] = take(RET_H * RET_DV).reshape(1, RET_H, RET_DV)
    g_small["ffn_conv_w"] = take(2 * 3 * FFN).reshape(2, 3, FFN)
    g_small["ffn_conv_b"] = take(2 * FFN).reshape(2, FFN)
    g_small["mla_norm"] = lax.dynamic_slice(g_small["mla_norm"], (dev * 128,), (128,)).reshape(1, 128)
    g_small["mla_q_norm"] = lax.dynamic_slice(g_small["mla_q_norm"], (dev * 48,), (48,)).reshape(1, 48)
    g_small["mla_kv_norm"] = lax.dynamic_slice(g_small["mla_kv_norm"], (dev * 32,), (32,)).reshape(1, 32)
    g_small["ret_gn"] = lax.dynamic_slice(g_small["ret_gn"], (0, 0, dev * 64), (1, RET_H, 64))
    g_small["ffn_conv_w"] = lax.dynamic_slice(g_small["ffn_conv_w"], (0, 0, dev * 352), (2, 3, 352))

    grads, delta, new_m, new_v = {}, {}, {}, {}
    for k in ["ffn_w_in"] + [k for k in BIG if k != "ffn_w_in"]:
        rcs = received[k]
        shp = w[k].shape
        R, C = rcs[0].shape[1], rcs[0].shape[2]
        rows = len(rcs) * R
        tr = max(t for t in range(16, 257, 16) if R % t == 0)
        comm = _Exchange([dWmla_in]) if k == "ffn_w_in" else None
        res = _adamw(f"adamw_{k}", rcs, w[k].reshape(rows, C), mom[k].reshape(rows, C), var[k].reshape(rows, C),
                     tr=tr, comm=comm)
        if comm is not None:
            res, received["mla_w_in"] = res
        grads[k], delta[k], new_m[k], new_v[k] = (t.reshape(shp) for t in res)
    SMALL = REPL + SHARDED_SMALL

    def pack(d):
        vflat = jnp.concatenate([d[k].reshape(-1) for k in SMALL])
        return jnp.pad(vflat, (0, 96 * 128 - vflat.shape[0])).reshape(96, 128)

    ps = _adamw("adamw_small", [pack(g_small)[None]], pack(w), pack(mom), pack(var))
    off = 0
    for k in SMALL:
        n = w[k].size
        grads[k], delta[k], new_m[k], new_v[k] = (t.reshape(-1)[off:off + n].reshape(w[k].shape) for t in ps)
        off += n
    names = list(w)
    return (loss, grad_x, *[grads[k] for k in names], *[delta[k] for k in names], *[new_m[k] for k in names],
            *[new_v[k] for k in names])
```

```python
import functools

import jax
import jax.numpy as jnp
from jax import lax
from jax.experimental import pallas as pl
from jax.experimental.pallas import tpu as pltpu

F32, BF16 = jnp.float32, jnp.bfloat16

NDEV = 8
D_MODEL = 1024
CHUNK = 64
RMS_EPS = 1e-6
ROPE_THETA = 10000.0
RET_H, RET_DK, RET_DV = 4, 256, 512
RET_SC = 256
MLA_H, MLA_QR, MLA_KVR = 8, 384, 256
MLA_NOPE, MLA_ROPE, MLA_V = 128, 64, 128
MLA_QK = MLA_NOPE + MLA_ROPE
MASK_VALUE = -1e30
FFN = 2816
FSH = FFN * 2 // NDEV
ATT_TQ = 256
ADAM_LR, ADAM_B1, ADAM_B2, ADAM_EPS, ADAM_WD, ADAM_STEP = 0.001, 0.9, 0.999, 1e-08, 0.01, 10
MESH = pl.DeviceIdType.MESH
VMEM_LIMIT = 56 * 2 ** 20


def _cp(sem):
    return pltpu.CompilerParams(dimension_semantics=sem, vmem_limit_bytes=VMEM_LIMIT)


def _dot(a, b, dims):
    return lax.dot_general(a, b, (dims, ((), ())), preferred_element_type=F32)


NN = ((1,), (0,))
NT = ((1,), (1,))
TN = ((0,), (0,))


def _place():
    return lax.axis_index("x"), lax.axis_index("y"), lax.axis_index("c")


def _idx(d):
    return 4 * d[0] + 2 * d[1] + d[2]


ANY = pl.BlockSpec(memory_space=pl.ANY)


class _Gather:
    def __init__(self, arrs):
        self.srcs = list(arrs)
        self.out_shape = [jax.ShapeDtypeStruct((NDEV,) + a.shape, a.dtype) for a in arrs]

    def _copies(self, ins, outs, send, recv, loc):
        n = len(self.srcs)
        x, y, c = _place()
        me, sib = (x, y, c), (x, y, 1 - c)
        chips = [(1 - x, y), (x, 1 - y), (1 - x, 1 - y)]

        def cp(a, k, block, to, src=None):
            dst = outs[a].at[_idx(block)]
            return pltpu.make_async_remote_copy(src_ref=dst if src is None else src, dst_ref=dst, send_sem=send.at[a, k],
                                                recv_sem=recv.at[a, k], device_id=to, device_id_type=MESH)

        mine = [pltpu.make_async_copy(ins[a], outs[a].at[_idx(me)], loc.at[a]) for a in range(n)]
        first = [cp(a, 0, me, sib, src=ins[a]) for a in range(n)]
        first += [cp(a, 1 + j, me, (*chip, c), src=ins[a]) for a in range(n) for j, chip in enumerate(chips)]
        landed = [cp(a, 1 + j, (*chip, c), me) for j, chip in enumerate(chips) for a in range(n)]
        passed = [cp(a, 4 + j, (*chip, c), sib) for j, chip in enumerate(chips) for a in range(n)]
        from_sib = [cp(a, 0, sib, me) for a in range(n)]
        from_sib += [cp(a, 4 + j, (*chip, 1 - c), me) for j, chip in enumerate(chips) for a in range(n)]
        return mine, first, landed, passed, from_sib

    def start(self, *refs):
        mine, first, _, _, _ = self._copies(*refs)
        for cp in mine + first:
            cp.start()

    def mid(self, *refs):
        _, _, landed, passed, _ = self._copies(*refs)
        for got, on in zip(landed, passed):
            got.wait_recv()
            on.start()

    def finish(self, *refs):
        mine, first, _, passed, from_sib = self._copies(*refs)
        for cp in from_sib:
            cp.wait_recv()
        for cp in first + passed:
            cp.wait_send()
        for cp in mine:
            cp.wait()


class _Exchange:
    def __init__(self, arrs, rows=None):
        self.srcs = list(arrs)
        self.rows = rows if rows is not None else [None] * len(arrs)
        self.out_shape = [jax.ShapeDtypeStruct(a.shape if r is None else (a.shape[0], r[1]) + a.shape[2:], a.dtype)
                          for a, r in zip(arrs, self.rows)]

    def _copies(self, ins, outs, send, recv, loc):
        n = len(self.srcs)
        x, y, c = _place()
        me = _idx((x, y, c))

        def src(a, q):
            r = self.rows[a]
            return ins[a].at[q] if r is None else ins[a].at[q, pl.ds(r[0], r[1])]

        mine = [pltpu.make_async_copy(src(a, me), outs[a].at[me], loc.at[a]) for a in range(n)]
        remote = []
        for k in range(1, NDEV):
            peer = (x ^ (k >> 2), y ^ ((k >> 1) & 1), c ^ (k & 1))
            remote += [pltpu.make_async_remote_copy(
                src_ref=src(a, _idx(peer)), dst_ref=outs[a].at[me], send_sem=send.at[a, k - 1],
                recv_sem=recv.at[a, k - 1], device_id=peer, device_id_type=MESH) for a in range(n)]
        return mine, remote

    def start(self, *refs):
        mine, remote = self._copies(*refs)
        for cp in mine + remote:
            cp.start()

    def mid(self, *refs):
        pass

    def finish(self, *refs):
        mine, remote = self._copies(*refs)
        for cp in remote + mine:
            cp.wait()


def _comm_scratch(n):
    return [pltpu.SemaphoreType.DMA((n, 7)), pltpu.SemaphoreType.DMA((n, 7)), pltpu.SemaphoreType.DMA((n,))]


def _comm_call(name, comm):
    n = len(comm.srcs)

    def body(*refs):
        parts = (refs[:n], refs[n:2 * n]) + tuple(refs[2 * n:])
        comm.start(*parts)
        comm.mid(*parts)
        comm.finish(*parts)

    return pl.pallas_call(body, name=name, in_specs=[ANY] * n, out_specs=[ANY] * n, out_shape=comm.out_shape,
                          scratch_shapes=_comm_scratch(n))(*comm.srcs)


def _pcall(body, *, name, grid, in_specs, out_specs, out_shape, scratch_shapes, sem, args, comm=None):
    if comm is None:
        return pl.pallas_call(body, name=name, grid=grid, in_specs=in_specs, out_specs=out_specs, out_shape=out_shape,
                              scratch_shapes=scratch_shapes, compiler_params=_cp(sem))(*args), None
    ni, no, ns, nc = len(in_specs), len(out_shape), len(scratch_shapes), len(comm.srcs)
    total = 1
    for g in grid:
        total *= g
    middle = (3 * total) // 5

    def wrapped(*refs):
        ins, csrc = refs[:ni], refs[ni:ni + nc]
        outs, cdst = refs[ni + nc:ni + nc + no], refs[ni + nc + no:ni + 2 * nc + no]
        scr, sems = refs[ni + 2 * nc + no:ni + 2 * nc + no + ns], refs[ni + 2 * nc + no + ns:]
        step = pl.program_id(0)
        for k in range(1, len(grid)):
            step = step * grid[k] + pl.program_id(k)
        parts = (csrc, cdst) + tuple(sems)

        @pl.when(step == 0)
        def _():
            comm.start(*parts)

        body(*ins, *outs, *scr)

        @pl.when(step == middle)
        def _():
            comm.mid(*parts)

        @pl.when(step == total - 1)
        def _():
            comm.finish(*parts)

    res = pl.pallas_call(
        wrapped, name=name, grid=grid, in_specs=list(in_specs) + [ANY] * nc, out_specs=list(out_specs) + [ANY] * nc,
        out_shape=list(out_shape) + comm.out_shape, scratch_shapes=list(scratch_shapes) + _comm_scratch(nc),
        compiler_params=_cp(("arbitrary",) * len(grid)))(*args, *comm.srcs)
    return res[:no], res[no:]


def _mm(name, a, b, *, grid, a_spec, b_spec, o_spec, out_shape, dims, kax=None, res=None, res_spec=None,
        jb=0, acc_shape=None, comm=None):
    nk = grid[kax] if kax is not None else 1

    def body(*refs):
        if res is not None:
            a_ref, b_ref, r_ref, o_ref = refs[:4]
        else:
            a_ref, b_ref, o_ref = refs[:3]

        def product():
            if not jb:
                return _dot(a_ref[...], b_ref[...], dims)
            part = _dot(a_ref[0], b_ref[0], dims)
            for j in range(1, jb):
                part = part + _dot(a_ref[j], b_ref[j], dims)
            return part

        def fin(acc):
            if res is not None:
                acc = acc + r_ref[...]
            o_ref[...] = acc.astype(o_ref.dtype)

        if nk == 1:
            fin(product())
        else:
            acc_ref = refs[-1]
            k = pl.program_id(kax)

            @pl.when(k == 0)
            def _():
                acc_ref[...] = jnp.zeros_like(acc_ref)

            acc_ref[...] += product()

            @pl.when(k == nk - 1)
            def _():
                fin(acc_ref[...])

    sem = tuple("arbitrary" if i == kax else "parallel" for i in range(len(grid)))
    in_specs = [a_spec, b_spec] + ([res_spec] if res is not None else [])
    args = (a, b) + ((res,) if res is not None else ())
    scratch = [pltpu.VMEM(acc_shape, F32)] if nk > 1 else []
    (out,), got = _pcall(body, name=name, grid=grid, in_specs=in_specs, out_specs=[o_spec], out_shape=[out_shape],
                         scratch_shapes=scratch, sem=sem, args=args, comm=comm)
    return out if comm is None else (out, got)


def _bs(shape, fn):
    return pl.BlockSpec(shape, fn)


def _rms_fwd(name, x, g, tm=512):
    T, D = x.shape

    def body(x_ref, g_ref, o_ref):
        xf = x_ref[...]
        r = lax.rsqrt(jnp.mean(xf * xf, axis=-1, keepdims=True) + RMS_EPS)
        o_ref[...] = (xf * r * g_ref[...]).astype(o_ref.dtype)

    return pl.pallas_call(
        body, name=name, grid=(T // tm,),
        in_specs=[_bs((tm, D), lambda i: (i, 0)), _bs((1, D), lambda i: (0, 0))],
        out_specs=_bs((tm, D), lambda i: (i, 0)), out_shape=jax.ShapeDtypeStruct((T, D), BF16),
        compiler_params=_cp(("parallel",)))(x, g)


def _rms_bwd(name, x, g, dh, dres=None, tm=512, also_bf16=False):
    T, D = x.shape

    def body(*refs):
        if also_bf16:
            refs, dxb_ref = refs[:-1], refs[-1]
        if dres is not None:
            x_ref, g_ref, dh_ref, dres_ref, dx_ref, dg_ref = refs
        else:
            x_ref, g_ref, dh_ref, dx_ref, dg_ref = refs
        i = pl.program_id(0)
        xf = x_ref[...]
        r = lax.rsqrt(jnp.mean(xf * xf, axis=-1, keepdims=True) + RMS_EPS)
        xh = xf * r
        d = dh_ref[...].astype(F32)
        dxh = d * g_ref[...]
        dx = r * (dxh - xh * jnp.mean(dxh * xh, axis=-1, keepdims=True))
        if dres is not None:
            dx = dx + dres_ref[...]
        dx_ref[...] = dx
        if also_bf16:
            dxb_ref[...] = dx.astype(BF16)
        part = jnp.sum(d * xh, axis=0, keepdims=True)

        @pl.when(i == 0)
        def _():
            dg_ref[...] = part

        @pl.when(i > 0)
        def _():
            dg_ref[...] += part

    row = _bs((tm, D), lambda i: (i, 0))
    vec = _bs((1, D), lambda i: (0, 0))
    in_specs = [row, vec, row] + ([row] if dres is not None else [])
    args = (x, g, dh) + ((dres,) if dres is not None else ())
    extra = [jax.ShapeDtypeStruct((T, D), BF16)] if also_bf16 else []
    return pl.pallas_call(
        body, name=name, grid=(T // tm,), in_specs=in_specs, out_specs=[row, vec] + [row] * len(extra),
        out_shape=[jax.ShapeDtypeStruct((T, D), F32), jax.ShapeDtypeStruct((1, D), F32)] + extra,
        compiler_params=_cp(("arbitrary",)))(*args)


def _loss(y, tgt, tm=512):
    T, D = y.shape

    def body(y_ref, t_ref, dy_ref, s_ref, dyb_ref):
        i = pl.program_id(0)
        e = y_ref[...] - t_ref[...]
        dy = e * (1.0 / D)
        dy_ref[...] = dy
        dyb_ref[...] = dy.astype(BF16)
        part = jnp.sum(e * e, axis=0, keepdims=True)

        @pl.when(i == 0)
        def _():
            s_ref[...] = part

        @pl.when(i > 0)
        def _():
            s_ref[...] += part

    row = _bs((tm, D), lambda i: (i, 0))
    return pl.pallas_call(
        body, name="loss_head", grid=(T // tm,), in_specs=[row, row],
        out_specs=[row, _bs((1, D), lambda i: (0, 0)), row],
        out_shape=[jax.ShapeDtypeStruct((T, D), F32), jax.ShapeDtypeStruct((1, D), F32),
                   jax.ShapeDtypeStruct((T, D), BF16)],
        compiler_params=_cp(("arbitrary",)))(y, tgt)


def _shift_rows(t, k, row):
    return jnp.where(row >= k, pltpu.roll(t, k, 0), 0.0)


def _shift_rows_up(t, k, row, n):
    return jnp.where(row < n - k, pltpu.roll(t, n - k, 0), 0.0)


def _convffn_fwd(name, u, cw, cb, B, S):
    _, J, T, F = u.shape

    def body(u_ref, cw_ref, cb_ref, o_ref):
        a = u_ref[0].astype(F32)
        g = u_ref[1].astype(F32)
        row = lax.broadcasted_iota(jnp.int32, (S, F), 0)
        w0, w1, w2 = cw_ref[0:1, :], cw_ref[1:2, :], cw_ref[2:3, :]
        gc = _shift_rows(g, 2, row) * w0 + _shift_rows(g, 1, row) * w1 + g * w2 + cb_ref[...]
        o_ref[...] = (gc * jax.nn.sigmoid(gc) * a).astype(o_ref.dtype)

    return pl.pallas_call(
        body, name=name, grid=(J, B),
        in_specs=[_bs((2, None, S, F), lambda j, b: (0, j, b, 0)), _bs((None, 3, F), lambda j, b: (j, 0, 0)),
                  _bs((None, 1, F), lambda j, b: (j, 0, 0))],
        out_specs=_bs((None, S, F), lambda j, b: (j, b, 0)), out_shape=jax.ShapeDtypeStruct((J, T, F), BF16),
        compiler_params=_cp(("parallel", "parallel")))(u, cw, cb)


def _convffn_bwd(name, u, cw, cb, dgt, B, S):
    _, J, T, F = u.shape

    def body(u_ref, cw_ref, cb_ref, d_ref, du_ref, dcw_ref, dcb_ref):
        b = pl.program_id(1)
        a = u_ref[0].astype(F32)
        g = u_ref[1].astype(F32)
        d = d_ref[...].astype(F32)
        row = lax.broadcasted_iota(jnp.int32, (S, F), 0)
        w0, w1, w2 = cw_ref[0:1, :], cw_ref[1:2, :], cw_ref[2:3, :]
        g1, g2 = _shift_rows(g, 1, row), _shift_rows(g, 2, row)
        gc = g2 * w0 + g1 * w1 + g * w2 + cb_ref[...]
        sg = jax.nn.sigmoid(gc)
        du_ref[0] = (d * gc * sg).astype(du_ref.dtype)
        dgc = d * a * (sg * (1.0 + gc * (1.0 - sg)))
        dg = dgc * w2 + _shift_rows_up(dgc, 1, row, S) * w1 + _shift_rows_up(dgc, 2, row, S) * w0
        du_ref[1] = dg.astype(du_ref.dtype)
        parts = [jnp.sum(dgc * g2, axis=0, keepdims=True), jnp.sum(dgc * g1, axis=0, keepdims=True),
                 jnp.sum(dgc * g, axis=0, keepdims=True)]
        pb = jnp.sum(dgc, axis=0, keepdims=True)

        @pl.when(b == 0)
        def _():
            for k in range(3):
                dcw_ref[k:k + 1, :] = parts[k]
            dcb_ref[...] = pb

        @pl.when(b > 0)
        def _():
            for k in range(3):
                dcw_ref[k:k + 1, :] += parts[k]
            dcb_ref[...] += pb

    uspec = _bs((2, None, S, F), lambda j, b: (0, j, b, 0))
    return pl.pallas_call(
        body, name=name, grid=(J, B),
        in_specs=[uspec, _bs((None, 3, F), lambda j, b: (j, 0, 0)), _bs((None, 1, F), lambda j, b: (j, 0, 0)),
                  _bs((None, S, F), lambda j, b: (j, b, 0))],
        out_specs=[uspec, _bs((None, 3, F), lambda j, b: (j, 0, 0)), _bs((None, 1, F), lambda j, b: (j, 0, 0))],
        out_shape=[jax.ShapeDtypeStruct(u.shape, BF16), jax.ShapeDtypeStruct((J, 3, F), F32),
                   jax.ShapeDtypeStruct((J, 1, F), F32)],
        compiler_params=_cp(("parallel", "arbitrary")))(u, cw, cb, dgt)


def _ret_tables(S):
    half = RET_DK // 2
    inv = ROPE_THETA ** (-jnp.arange(half, dtype=F32) / half)
    ang = jnp.arange(S).astype(F32)[:, None] * inv[None, :]
    lg = jnp.log1p(-jnp.exp2(-5.0 - jnp.arange(RET_H, dtype=F32)))
    i = jnp.arange(RET_SC, dtype=F32)
    same_or_earlier = (jnp.floor(i[None, :] / CHUNK) <= jnp.floor(i[:, None] / CHUNK)).astype(F32)
    dm = jnp.exp(lg[:, None, None] * jnp.abs(i[:, None] - i[None, :])) * same_or_earlier[None]
    qd = jnp.exp(lg[:, None] * (i + 1.0))[:, :, None]
    kd = jnp.exp(lg[:, None] * (RET_SC - 1.0 - i))[:, :, None]
    cd = jnp.exp(lg * RET_SC)[:, None, None]
    return jnp.cos(ang), jnp.sin(ang), dm, qd, kd, cd


def _rope_halves(t, cs, sn):
    h = t.shape[-1] // 2
    t1, t2 = t[:, :h], t[:, h:]
    return jnp.concatenate([t1 * cs - t2 * sn, t2 * cs + t1 * sn], axis=-1)


def _unrope_halves(d, cs, sn):
    h = d.shape[-1] // 2
    d1, d2 = d[:, :h], d[:, h:]
    return jnp.concatenate([d1 * cs + d2 * sn, d2 * cs - d1 * sn], axis=-1)


def _ret_specs(nC, order):
    SC = RET_SC

    def sp(shape, fn):
        return _bs(shape, lambda *g: fn(*order(*g)))

    q = sp((SC, RET_DK), lambda b, h, c: (b * nC + c, h))
    k = sp((SC, RET_DK), lambda b, h, c: (b * nC + c, RET_H + h))
    v = sp((SC, RET_DV), lambda b, h, c: (b * nC + c, RET_H + h))
    g = sp((SC, RET_DV), lambda b, h, c: (b * nC + c, 2 * RET_H + h))
    cs = sp((SC, RET_DK // 2), lambda b, h, c: (c, 0))
    dm = sp((None, SC, SC), lambda b, h, c: (h, 0, 0))
    dv = sp((None, SC, 1), lambda b, h, c: (h, 0, 0))
    cd = sp((None, 1, 1), lambda b, h, c: (h, 0, 0))
    gn = sp((None, 1, RET_DV), lambda b, h, c: (h, 0, 0))
    wide = sp((SC, RET_DV), lambda b, h, c: (b * nC + c, h))
    narrow = sp((SC, RET_DK), lambda b, h, c: (b * nC + c, h))
    st = sp((None, None, None, RET_DK, RET_DV), lambda b, h, c: (b, h, c, 0, 0))
    return dict(q=q, k=k, v=v, g=g, cs=cs, dm=dm, dv=dv, cd=cd, gn=gn, wide=wide, narrow=narrow, st=st)


def _ret_fwd(proj, tabs, gn, B, S, comm=None):
    T = B * S
    nC = S // RET_SC
    cos, sin, dm, qd, kd, cd = tabs
    s = _ret_specs(nC, lambda b, h, c: (b, h, c))

    def body(q_ref, k_ref, v_ref, g_ref, cos_ref, sin_ref, dm_ref, qd_ref, kd_ref, cd_ref, gn_ref,
             o_ref, gt_ref, st_ref, state):
        c = pl.program_id(2)

        @pl.when(c == 0)
        def _():
            state[...] = jnp.zeros_like(state)

        cs, sn = cos_ref[...], sin_ref[...]
        qf = _rope_halves(q_ref[...].astype(F32), cs, sn)
        kf = _rope_halves(k_ref[...].astype(F32), cs, sn) * (RET_DK ** -0.5)
        v = v_ref[...]
        p = _dot(qf.astype(BF16), kf.astype(BF16), NT) * dm_ref[...]
        st = state[...]
        stb = st.astype(BF16)
        st_ref[...] = stb
        o = _dot(p.astype(BF16), v, NN) + _dot((qf * qd_ref[...]).astype(BF16), stb, NN)
        state[...] = st * cd_ref[...] + _dot((kf * kd_ref[...]).astype(BF16), v, TN)
        o_ref[...] = o
        r = lax.rsqrt(jnp.mean(o * o, axis=-1, keepdims=True) + RMS_EPS)
        gf = g_ref[...].astype(F32)
        gt_ref[...] = ((o * r * gn_ref[...]) * (gf * jax.nn.sigmoid(gf))).astype(BF16)

    return _pcall(
        body, name="ret_fwd", grid=(B, RET_H, nC),
        in_specs=[s["q"], s["k"], s["v"], s["g"], s["cs"], s["cs"], s["dm"], s["dv"], s["dv"], s["cd"], s["gn"]],
        out_specs=[s["wide"], s["wide"], s["st"]],
        out_shape=[jax.ShapeDtypeStruct((T, RET_H * RET_DV), F32), jax.ShapeDtypeStruct((T, RET_H * RET_DV), BF16),
                   jax.ShapeDtypeStruct((B, RET_H, nC, RET_DK, RET_DV), BF16)],
        scratch_shapes=[pltpu.VMEM((RET_DK, RET_DV), F32)], sem=("parallel", "parallel", "arbitrary"),
        args=(proj, proj, proj, proj, cos, sin, dm, qd, kd, cd, gn), comm=comm)


def _ret_bwd(proj, o_raw, states, dgt, tabs, gn, B, S, comm=None):
    T = B * S
    nC = S // RET_SC
    cos, sin, dm, qd, kd, cd = tabs
    s = _ret_specs(nC, lambda h, b, c: (b, h, nC - 1 - c))

    def body(q_ref, k_ref, v_ref, g_ref, o_ref, st_ref, d_ref, cos_ref, sin_ref, dm_ref, qd_ref, kd_ref, cd_ref,
             gn_ref, dq_ref, dk_ref, dv_ref, dg_ref, dgn_ref, dstate):
        b, c = pl.program_id(1), pl.program_id(2)

        @pl.when(c == 0)
        def _():
            dstate[...] = jnp.zeros_like(dstate)

        @pl.when((b == 0) & (c == 0))
        def _():
            dgn_ref[...] = jnp.zeros_like(dgn_ref)

        cs, sn = cos_ref[...], sin_ref[...]
        qf = _rope_halves(q_ref[...].astype(F32), cs, sn)
        kf = _rope_halves(k_ref[...].astype(F32), cs, sn) * (RET_DK ** -0.5)
        v = v_ref[...]
        gnv = gn_ref[...]
        o = o_ref[...]
        r = lax.rsqrt(jnp.mean(o * o, axis=-1, keepdims=True) + RMS_EPS)
        oh = o * r
        gf = g_ref[...].astype(F32)
        sg = jax.nn.sigmoid(gf)
        d = d_ref[...].astype(F32)
        dg_ref[...] = (d * (oh * gnv) * (sg * (1.0 + gf * (1.0 - sg)))).astype(BF16)
        don = d * (gf * sg)
        dgn_ref[...] += jnp.sum(don * oh, axis=0, keepdims=True)
        doh = don * gnv
        dO = (r * (doh - oh * jnp.mean(doh * oh, axis=-1, keepdims=True))).astype(BF16)
        dmv = dm_ref[...]
        qb, kb = qf.astype(BF16), kf.astype(BF16)
        p = (_dot(qb, kb, NT) * dmv).astype(BF16)
        dp = (_dot(dO, v, NT) * dmv).astype(BF16)
        st = st_ref[...]
        dsn = dstate[...]
        dsb = dsn.astype(BF16)
        qdv, kdv = qd_ref[...], kd_ref[...]
        dq = _dot(dp, kb, NN) + _dot(dO, st, NT) * qdv
        dk = _dot(dp, qb, TN) + _dot(v, dsb, NT) * kdv
        dv = _dot(p, dO, TN) + _dot((kf * kdv).astype(BF16), dsb, NN)
        dstate[...] = dsn * cd_ref[...] + _dot((qf * qdv).astype(BF16), dO, TN)
        dq_ref[...] = _unrope_halves(dq, cs, sn).astype(BF16)
        dk_ref[...] = (_unrope_halves(dk, cs, sn) * (RET_DK ** -0.5)).astype(BF16)
        dv_ref[...] = dv.astype(BF16)

    return _pcall(
        body, name="ret_bwd", grid=(RET_H, B, nC),
        in_specs=[s["q"], s["k"], s["v"], s["g"], s["wide"], s["st"], s["wide"], s["cs"], s["cs"], s["dm"], s["dv"],
                  s["dv"], s["cd"], s["gn"]],
        out_specs=[s["narrow"], s["narrow"], s["wide"], s["wide"], s["gn"]],
        out_shape=[jax.ShapeDtypeStruct((T, RET_H * RET_DK), BF16), jax.ShapeDtypeStruct((T, RET_H * RET_DK), BF16),
                   jax.ShapeDtypeStruct((T, RET_H * RET_DV), BF16), jax.ShapeDtypeStruct((T, RET_H * RET_DV), BF16),
                   jax.ShapeDtypeStruct((RET_H, 1, RET_DV), F32)],
        scratch_shapes=[pltpu.VMEM((RET_DK, RET_DV), F32)], sem=("arbitrary", "arbitrary", "arbitrary"),
        args=(proj, proj, proj, proj, o_raw, states, dgt, cos, sin, dm, qd, kd, cd, gn), comm=comm)


def _mla_tables(S):
    half = MLA_ROPE // 2
    inv = ROPE_THETA ** (-jnp.arange(half, dtype=F32) / half)
    ang = jnp.arange(S).astype(F32)[:, None] * inv[None, :]
    cos, sin = jnp.cos(ang), jnp.sin(ang)
    i = jnp.arange(MLA_ROPE)
    swap = (i[:, None] == (i[None, :] + half) % MLA_ROPE).astype(F32)
    return jnp.concatenate([cos, cos], axis=-1), jnp.concatenate([-sin, sin], axis=-1), swap


def _swap_halves(t, swap):
    return jnp.dot(t, swap, precision=lax.Precision.HIGHEST, preferred_element_type=F32)


def _head_norm_rope(n, r_, gn, gr, cos, sin, swap, scale):
    ssq = jnp.sum(n * n, axis=-1, keepdims=True) + jnp.sum(r_ * r_, axis=-1, keepdims=True)
    rstd = lax.rsqrt(ssq * (1.0 / MLA_QK) + RMS_EPS)
    yn = n * rstd * gn
    yr = r_ * rstd * gr
    yr = yr * cos + _swap_halves(yr, swap) * sin
    if scale != 1.0:
        yn, yr = yn * scale, yr * scale
    return yn, yr, rstd


def _head_norm_rope_bwd(dn, dr, n, r_, gn, gr, cos, sin, swap, scale):
    ssq = jnp.sum(n * n, axis=-1, keepdims=True) + jnp.sum(r_ * r_, axis=-1, keepdims=True)
    rstd = lax.rsqrt(ssq * (1.0 / MLA_QK) + RMS_EPS)
    hn, hr = n * rstd, r_ * rstd
    dyn = dn * scale if scale != 1.0 else dn
    dr = dr * scale if scale != 1.0 else dr
    dyr = dr * cos + _swap_halves(dr * sin, swap)
    dgn = jnp.sum(dyn * hn, axis=0, keepdims=True)
    dgr = jnp.sum(dyr * hr, axis=0, keepdims=True)
    dhn, dhr = dyn * gn, dyr * gr
    mt = (jnp.sum(dhn * hn, axis=-1, keepdims=True) + jnp.sum(dhr * hr, axis=-1, keepdims=True)) * (1.0 / MLA_QK)
    return rstd * (dhn - hn * mt), rstd * (dhr - hr * mt), dgn, dgr


MLA_PAD = 256


def _diag_bias():
    i = jnp.arange(ATT_TQ)
    return jnp.where((i[None, :] // CHUNK) <= (i[:, None] // CHUNK), 0.0, MASK_VALUE).astype(F32)


def _store_padded(dst, rows, n, r_):
    dst[rows, :MLA_NOPE] = n.astype(BF16)
    dst[rows, MLA_NOPE:MLA_QK] = r_.astype(BF16)
    dst[rows, MLA_QK:] = jnp.zeros((n.shape[0], MLA_PAD - MLA_QK), BF16)


def _mla_fwd(q_raw, kv, kr, gains, tabs, B, S, comm=None):
    T = B * S
    TQ = ATT_TQ
    nQ = S // TQ
    qgn, qgr, kgn, kgr = gains
    cos, sin, swap = tabs
    scale = MLA_QK ** -0.5

    def body(q_ref, kv_ref, kr_ref, qgn_ref, qgr_ref, kgn_ref, kgr_ref, c_ref, s_ref, sw_ref, bias_ref,
             o_ref, lse_ref, qf_s, kf_s, v_s):
        sw = sw_ref[...]

        def prep(t, _):
            rows = pl.ds(pl.multiple_of(t * TQ, TQ), TQ)
            cs, sn = c_ref[rows, :], s_ref[rows, :]
            qn, qr, _ = _head_norm_rope(q_ref[rows, :MLA_NOPE], q_ref[rows, MLA_NOPE:MLA_QK], qgn_ref[...], qgr_ref[...],
                                        cs, sn, sw, scale)
            _store_padded(qf_s, rows, qn, qr)
            kn, krr, _ = _head_norm_rope(kv_ref[rows, :MLA_NOPE], kr_ref[rows, :], kgn_ref[...], kgr_ref[...],
                                         cs, sn, sw, 1.0)
            _store_padded(kf_s, rows, kn, krr)
            v_s[rows, :] = kv_ref[rows, MLA_NOPE:].astype(BF16)
            return 0

        lax.fori_loop(0, nQ, prep, 0)
        for i in range(nQ):
            rows = slice(i * TQ, (i + 1) * TQ)
            q = qf_s[rows, :]
            sd = _dot(q, kf_s[rows, :], NT) + bias_ref[...]
            m = jnp.max(sd, axis=-1, keepdims=True)
            if i:
                sl = _dot(q, kf_s[:i * TQ, :], NT)
                m = jnp.maximum(m, jnp.max(sl, axis=-1, keepdims=True))
            pd = jnp.exp(sd - m)
            l = jnp.sum(pd, axis=-1, keepdims=True)
            acc = _dot(pd.astype(BF16), v_s[rows, :], NN)
            if i:
                pl_ = jnp.exp(sl - m)
                l = l + jnp.sum(pl_, axis=-1, keepdims=True)
                acc = acc + _dot(pl_.astype(BF16), v_s[:i * TQ, :], NN)
            o_ref[rows, :] = (acc / l).astype(BF16)
            lse_ref[rows, :] = m + jnp.log(l)

    def vec(n):
        return _bs((1, n), lambda b, h: (0, 0))

    def cols(n):
        return _bs((S, n), lambda b, h: (b, h))

    tab = _bs((S, MLA_ROPE), lambda b, h: (0, 0))
    return _pcall(
        body, name="mla_fwd", grid=(B, MLA_H),
        in_specs=[cols(MLA_PAD), cols(MLA_NOPE + MLA_V), _bs((S, MLA_ROPE), lambda b, h: (b, 0)),
                  vec(MLA_NOPE), vec(MLA_ROPE), vec(MLA_NOPE), vec(MLA_ROPE), tab, tab,
                  _bs((MLA_ROPE, MLA_ROPE), lambda b, h: (0, 0)), _bs((TQ, TQ), lambda b, h: (0, 0))],
        out_specs=[cols(MLA_V), _bs((None, S, 1), lambda b, h: (h, b, 0)), cols(MLA_PAD), cols(MLA_PAD)],
        out_shape=[jax.ShapeDtypeStruct((T, MLA_H * MLA_V), BF16), jax.ShapeDtypeStruct((MLA_H, T, 1), F32),
                   jax.ShapeDtypeStruct((T, MLA_H * MLA_PAD), BF16), jax.ShapeDtypeStruct((T, MLA_H * MLA_PAD), BF16)],
        scratch_shapes=[pltpu.VMEM((S, MLA_V), BF16)],
        sem=("parallel", "parallel"), args=(q_raw, kv, kr, qgn, qgr, kgn, kgr, cos, sin, swap, _diag_bias()), comm=comm)


def _mla_bwd(q_raw, kv, kr, o, lse, do, qf, kf, gains, tabs, B, S, comm=None):
    T = B * S
    TQ = ATT_TQ
    nQ = S // TQ
    qgn, qgr, kgn, kgr = gains
    cos, sin, swap = tabs
    scale = MLA_QK ** -0.5

    def body(q_ref, kv_ref, kr_ref, o_ref, lse_ref, do_ref, qf_s, kf_s, qgn_ref, qgr_ref, kgn_ref, kgr_ref, c_ref, s_ref,
             sw_ref, bias_ref, dq_ref, dkv_ref, dkr_ref, dqgn_ref, dqgr_ref, dkgn_ref, dkgr_ref,
             v_s, dl_s, dq_s, dk_s, dv_s):
        b, h = pl.program_id(0), pl.program_id(1)
        sw = sw_ref[...]

        def blk(t):
            return pl.ds(pl.multiple_of(t * TQ, TQ), TQ)

        def prep(t, _):
            rows = blk(t)
            v_s[rows, :] = kv_ref[rows, MLA_NOPE:].astype(BF16)
            dl_s[rows, :] = jnp.sum(do_ref[rows, :].astype(F32) * o_ref[rows, :].astype(F32), axis=-1, keepdims=True)
            dk_s[rows, :] = jnp.zeros((TQ, MLA_PAD), F32)
            dv_s[rows, :] = jnp.zeros((TQ, MLA_V), F32)
            return 0

        lax.fori_loop(0, nQ, prep, 0)

        gqn, gqr = jnp.zeros((1, MLA_NOPE), F32), jnp.zeros((1, MLA_ROPE), F32)
        for i in range(nQ):
            rows = slice(i * TQ, (i + 1) * TQ)
            q, doi, lse_i, dl_i = qf_s[rows, :], do_ref[rows, :], lse_ref[rows, :], dl_s[rows, :]

            def part(cols, bias):
                k, v = kf_s[cols, :], v_s[cols, :]
                s = _dot(q, k, NT)
                if bias is not None:
                    s = s + bias
                p = jnp.exp(s - lse_i)
                ds = (p * (_dot(doi, v, NT) - dl_i)).astype(BF16)
                dk_s[cols, :] += _dot(ds, q, TN)
                dv_s[cols, :] += _dot(p.astype(BF16), doi, TN)
                return _dot(ds, k, NN)

            dq = part(rows, bias_ref[...])
            if i:
                dq = dq + part(slice(0, i * TQ), None)
            dq_s[...] = dq
            dqn, dqr, a0, a1 = _head_norm_rope_bwd(dq_s[:, :MLA_NOPE], dq_s[:, MLA_NOPE:MLA_QK], q_ref[rows, :MLA_NOPE],
                                                   q_ref[rows, MLA_NOPE:MLA_QK], qgn_ref[...], qgr_ref[...],
                                                   c_ref[rows, :], s_ref[rows, :], sw, scale)
            _store_padded(dq_ref, rows, dqn, dqr)
            gqn, gqr = gqn + a0, gqr + a1

        def post(t, carry):
            rows = blk(t)
            dkn, dkr, a2, a3 = _head_norm_rope_bwd(dk_s[rows, :MLA_NOPE], dk_s[rows, MLA_NOPE:MLA_QK],
                                                   kv_ref[rows, :MLA_NOPE], kr_ref[rows, :], kgn_ref[...], kgr_ref[...],
                                                   c_ref[rows, :], s_ref[rows, :], sw, 1.0)
            dkv_ref[rows, :MLA_NOPE] = dkn.astype(BF16)
            dkv_ref[rows, MLA_NOPE:] = dv_s[rows, :].astype(BF16)

            @pl.when(h == 0)
            def _():
                dkr_ref[rows, :] = dkr

            @pl.when(h > 0)
            def _():
                dkr_ref[rows, :] += dkr

            return carry[0] + a2, carry[1] + a3

        gkn, gkr = lax.fori_loop(0, nQ, post, (jnp.zeros((1, MLA_NOPE), F32), jnp.zeros((1, MLA_ROPE), F32)))
        first = (b == 0) & (h == 0)

        @pl.when(first)
        def _():
            dqgn_ref[...] = gqn
            dqgr_ref[...] = gqr
            dkgn_ref[...] = gkn
            dkgr_ref[...] = gkr

        @pl.when(jnp.logical_not(first))
        def _():
            dqgn_ref[...] += gqn
            dqgr_ref[...] += gqr
            dkgn_ref[...] += gkn
            dkgr_ref[...] += gkr

    def vec(n):
        return _bs((1, n), lambda b, h: (0, 0))

    def cols(n):
        return _bs((S, n), lambda b, h: (b, h))

    tab = _bs((S, MLA_ROPE), lambda b, h: (0, 0))
    return _pcall(
        body, name="mla_bwd", grid=(B, MLA_H),
        in_specs=[cols(MLA_PAD), cols(MLA_NOPE + MLA_V), _bs((S, MLA_ROPE), lambda b, h: (b, 0)), cols(MLA_V),
                  _bs((None, S, 1), lambda b, h: (h, b, 0)), cols(MLA_V), cols(MLA_PAD), cols(MLA_PAD),
                  vec(MLA_NOPE), vec(MLA_ROPE), vec(MLA_NOPE), vec(MLA_ROPE), tab, tab,
                  _bs((MLA_ROPE, MLA_ROPE), lambda b, h: (0, 0)), _bs((TQ, TQ), lambda b, h: (0, 0))],
        out_specs=[cols(MLA_PAD), cols(MLA_NOPE + MLA_V), _bs((S, MLA_ROPE), lambda b, h: (b, 0)),
                   vec(MLA_NOPE), vec(MLA_ROPE), vec(MLA_NOPE), vec(MLA_ROPE)],
        out_shape=[jax.ShapeDtypeStruct((T, MLA_H * MLA_PAD), BF16),
                   jax.ShapeDtypeStruct((T, MLA_H * (MLA_NOPE + MLA_V)), BF16),
                   jax.ShapeDtypeStruct((T, MLA_ROPE), F32), jax.ShapeDtypeStruct((1, MLA_NOPE), F32),
                   jax.ShapeDtypeStruct((1, MLA_ROPE), F32), jax.ShapeDtypeStruct((1, MLA_NOPE), F32),
                   jax.ShapeDtypeStruct((1, MLA_ROPE), F32)],
        scratch_shapes=[pltpu.VMEM((S, MLA_V), BF16), pltpu.VMEM((S, 1), F32), pltpu.VMEM((TQ, MLA_PAD), F32),
                        pltpu.VMEM((S, MLA_PAD), F32), pltpu.VMEM((S, MLA_V), F32)],
        sem=("arbitrary", "arbitrary"),
        args=(q_raw, kv, kr, o, lse, do, qf, kf, qgn, qgr, kgn, kgr, cos, sin, swap, _diag_bias()), comm=comm)


def _adamw(name, recvs, w, m, v, tr=None, comm=None):
    n, R, C = recvs[0].shape
    L = len(recvs)
    tr = R if tr is None else tr
    per = R // tr
    c1 = 1.0 - ADAM_B1 ** ADAM_STEP
    c2 = 1.0 - ADAM_B2 ** ADAM_STEP

    def body(*refs):
        r_refs = refs[:L]
        w_ref, m_ref, v_ref, g_ref, d_ref, nm_ref, nv_ref = refs[L:]
        layer = pl.program_id(0) // per

        def total(r_ref):
            t = r_ref[0].astype(F32)
            for k in range(1, n):
                t = t + r_ref[k].astype(F32)
            return t

        g = total(r_refs[0])
        for l in range(1, L):
            g = jnp.where(layer == l, total(r_refs[l]), g)
        mm = ADAM_B1 * m_ref[...] + (1.0 - ADAM_B1) * g
        vv = ADAM_B2 * v_ref[...] + (1.0 - ADAM_B2) * (g * g)
        g_ref[...] = g
        nm_ref[...] = mm
        nv_ref[...] = vv
        d_ref[...] = -ADAM_LR * ((mm / c1) / (jnp.sqrt(vv / c2) + ADAM_EPS) + ADAM_WD * w_ref[...])

    blk = _bs((tr, C), lambda i: (i, 0))
    r_specs = [_bs((n, tr, C), functools.partial(lambda l, i: (0, jnp.clip(i - l * per, 0, per - 1), 0), l))
               for l in range(L)]
    outs, got = _pcall(body, name=name, grid=(L * per,), in_specs=r_specs + [blk, blk, blk], out_specs=[blk] * 4,
                       out_shape=[jax.ShapeDtypeStruct((L * R, C), F32)] * 4, scratch_shapes=[], sem=("arbitrary",),
                       args=(*recvs, w, m, v), comm=comm)
    return outs if comm is None else (outs, got)


def _sum8(name, a):
    n, R, C = a.shape

    def body(a_ref, o_ref):
        s = a_ref[0]
        for k in range(1, n):
            s = s + a_ref[k]
        o_ref[...] = s

    return pl.pallas_call(body, name=name, out_shape=jax.ShapeDtypeStruct((R, C), a.dtype))(a)


def _sds(shape, dt):
    return jax.ShapeDtypeStruct(shape, dt)


def _proj_shared(name, h, w, out_dtype, tm=1024, comm=None):
    T, K = h.shape
    J, _, n = w.shape
    return _mm(name, h, w, grid=(T // tm, J), a_spec=_bs((tm, K), lambda m, j: (m, 0)),
               b_spec=_bs((None, K, n), lambda m, j: (j, 0, 0)), o_spec=_bs((None, tm, n), lambda m, j: (j, m, 0)),
               out_shape=_sds((J, T, n), out_dtype), dims=NN, comm=comm)


def _proj_shared_dx(name, d, w, tm=1024, comm=None):
    J, T, n = d.shape
    K = w.shape[1]
    return _mm(name, d, w, grid=(T // tm, J), a_spec=_bs((None, tm, n), lambda m, k: (k, m, 0)),
               b_spec=_bs((None, K, n), lambda m, k: (k, 0, 0)), o_spec=_bs((tm, K), lambda m, k: (m, 0)),
               out_shape=_sds((T, K), F32), dims=NT, kax=1, acc_shape=(tm, K), comm=comm)


def _proj_shared_dw(name, h, d, tt=1024, comm=None):
    T, K = h.shape
    J, _, n = d.shape
    return _mm(name, h, d, grid=(J, T // tt), a_spec=_bs((tt, K), lambda j, t: (t, 0)),
               b_spec=_bs((None, tt, n), lambda j, t: (j, t, 0)), o_spec=_bs((None, K, n), lambda j, t: (j, 0, 0)),
               out_shape=_sds((J, K, n), BF16), dims=TN, kax=1, acc_shape=(K, n), comm=comm)


def _out_proj(name, a, w, res, tm=512):
    J, T, k = a.shape
    N = w.shape[2]
    return _mm(name, a, w, grid=(T // tm,), a_spec=_bs((J, tm, k), lambda m: (0, m, 0)),
               b_spec=_bs((J, k, N), lambda m: (0, 0, 0)), o_spec=_bs((tm, N), lambda m: (m, 0)),
               out_shape=_sds((T, N), F32), dims=NN, res=res, res_spec=_bs((tm, N), lambda m: (m, 0)), jb=J)


def _out_proj_dx(name, dx, w, tm=512, comm=None):
    T, N = dx.shape
    J, k, _ = w.shape
    return _mm(name, dx, w, grid=(T // tm, J), a_spec=_bs((tm, N), lambda m, j: (m, 0)),
               b_spec=_bs((None, k, N), lambda m, j: (j, 0, 0)), o_spec=_bs((None, tm, k), lambda m, j: (j, m, 0)),
               out_shape=_sds((J, T, k), BF16), dims=NT, comm=comm)


def _out_proj_dw(name, a, dx, tt=512, comm=None):
    J, T, k = a.shape
    N = dx.shape[1]
    return _mm(name, a, dx, grid=(J, T // tt), a_spec=_bs((None, tt, k), lambda j, t: (j, t, 0)),
               b_spec=_bs((tt, N), lambda j, t: (t, 0)), o_spec=_bs((None, k, N), lambda j, t: (j, 0, 0)),
               out_shape=_sds((J, k, N), BF16), dims=TN, kax=1, acc_shape=(k, N), comm=comm)


def _dense(name, a, b, dims, out_dtype, tm=512, res=None, comm=None):
    if dims == TN:
        T, K = a.shape
        N = b.shape[1]
        return _mm(name, a, b, grid=(T // tm,), a_spec=_bs((tm, K), lambda t: (t, 0)),
                   b_spec=_bs((tm, N), lambda t: (t, 0)), o_spec=_bs((K, N), lambda t: (0, 0)),
                   out_shape=_sds((K, N), out_dtype), dims=TN, kax=0, acc_shape=(K, N), comm=comm)
    M, K = a.shape
    N = b.shape[1] if dims == NN else b.shape[0]
    row = _bs((tm, N), lambda m: (m, 0))
    return _mm(name, a, b, grid=(M // tm,), a_spec=_bs((tm, K), lambda m: (m, 0)), b_spec=_bs(b.shape, lambda m: (0, 0)),
               o_spec=row, out_shape=_sds((M, N), out_dtype), dims=dims, res=res,
               res_spec=row if res is not None else None, comm=comm)


def _bf16(x):
    return x.astype(BF16)


def _ffn_fwd(i, x, norm_g, w_in, cw, cb, w_out, B, S, comm_in=None):
    h = _rms_fwd(f"ffn{i}_norm", x, norm_g)
    u = _proj_shared(f"ffn{i}_in", h, w_in, BF16, comm=comm_in)
    u, got = u if comm_in is not None else (u, None)
    u4 = u.reshape(2, 4, u.shape[1], FSH)
    gt = _convffn_fwd(f"ffn{i}_gate", u4, cw, cb, B, S)
    y = _out_proj(f"ffn{i}_out", gt, w_out, x)
    return y, (x, h, u4, gt), got


def _ffn_bwd(i, dy, dyb, saved, norm_g, w_in, cw, cb, w_out, B, S):
    x, h, u4, gt = saved
    dgt = _out_proj_dx(f"ffn{i}_out_dx", dyb, w_out)
    dw_out = _out_proj_dw(f"ffn{i}_out_dw", gt, dyb).reshape(NDEV, FSH // 2, D_MODEL)
    du4, dcw, dcb = _convffn_bwd(f"ffn{i}_gate_bwd", u4, cw, cb, dgt, B, S)
    du = du4.reshape(NDEV, du4.shape[2], FSH)
    dh, (r_out,) = _proj_shared_dx(f"ffn{i}_in_dx", du, w_in, comm=_Exchange([dw_out]))
    dw_in = _proj_shared_dw(f"ffn{i}_in_dw", h, du)
    dx, dgn, dxb = _rms_bwd(f"ffn{i}_norm_bwd", x, norm_g, dh, dres=dy, also_bf16=True)
    return dx, dxb, dict(w_in=dw_in, norm=dgn, cw=dcw, cb=dcb), r_out


def kernel(x, ret_norm, ret_w_in, ret_gn, ret_w_out, mla_norm, mla_w_in, mla_q_norm, mla_w_qb, mla_kv_norm, mla_w_kvb, mla_q_head_norm, mla_k_head_norm, mla_w_out, ffn_norm, ffn_w_in, ffn_conv_w, ffn_conv_b, ffn_w_out, loss_target, m_ret_norm, m_ret_w_in, m_ret_gn, m_ret_w_out, m_mla_norm, m_mla_w_in, m_mla_q_norm, m_mla_w_qb, m_mla_kv_norm, m_mla_w_kvb, m_mla_q_head_norm, m_mla_k_head_norm, m_mla_w_out, m_ffn_norm, m_ffn_w_in, m_ffn_conv_w, m_ffn_conv_b, m_ffn_w_out, v_ret_norm, v_ret_w_in, v_ret_gn, v_ret_w_out, v_mla_norm, v_mla_w_in, v_mla_q_norm, v_mla_w_qb, v_mla_kv_norm, v_mla_w_kvb, v_mla_q_head_norm, v_mla_k_head_norm, v_mla_w_out, v_ffn_norm, v_ffn_w_in, v_ffn_conv_w, v_ffn_conv_b, v_ffn_w_out):
    B, S, D = x.shape
    T = B * S
    w = dict(ret_norm=ret_norm, ret_w_in=ret_w_in, ret_gn=ret_gn, ret_w_out=ret_w_out, mla_norm=mla_norm,
             mla_w_in=mla_w_in, mla_q_norm=mla_q_norm, mla_w_qb=mla_w_qb, mla_kv_norm=mla_kv_norm, mla_w_kvb=mla_w_kvb,
             mla_q_head_norm=mla_q_head_norm, mla_k_head_norm=mla_k_head_norm, mla_w_out=mla_w_out, ffn_norm=ffn_norm,
             ffn_w_in=ffn_w_in, ffn_conv_w=ffn_conv_w, ffn_conv_b=ffn_conv_b, ffn_w_out=ffn_w_out)
    mom = dict(ret_norm=m_ret_norm, ret_w_in=m_ret_w_in, ret_gn=m_ret_gn, ret_w_out=m_ret_w_out, mla_norm=m_mla_norm,
               mla_w_in=m_mla_w_in, mla_q_norm=m_mla_q_norm, mla_w_qb=m_mla_w_qb, mla_kv_norm=m_mla_kv_norm,
               mla_w_kvb=m_mla_w_kvb, mla_q_head_norm=m_mla_q_head_norm, mla_k_head_norm=m_mla_k_head_norm,
               mla_w_out=m_mla_w_out, ffn_norm=m_ffn_norm, ffn_w_in=m_ffn_w_in, ffn_conv_w=m_ffn_conv_w,
               ffn_conv_b=m_ffn_conv_b, ffn_w_out=m_ffn_w_out)
    var = dict(ret_norm=v_ret_norm, ret_w_in=v_ret_w_in, ret_gn=v_ret_gn, ret_w_out=v_ret_w_out, mla_norm=v_mla_norm,
               mla_w_in=v_mla_w_in, mla_q_norm=v_mla_q_norm, mla_w_qb=v_mla_w_qb, mla_kv_norm=v_mla_kv_norm,
               mla_w_kvb=v_mla_w_kvb, mla_q_head_norm=v_mla_q_head_norm, mla_k_head_norm=v_mla_k_head_norm,
               mla_w_out=v_mla_w_out, ffn_norm=v_ffn_norm, ffn_w_in=v_ffn_w_in, ffn_conv_w=v_ffn_conv_w,
               ffn_conv_b=v_ffn_conv_b, ffn_w_out=v_ffn_w_out)
    BIG = ["ret_w_in", "ret_w_out", "mla_w_in", "mla_w_qb", "mla_w_kvb", "mla_w_out", "ffn_w_in", "ffn_w_out"]
    REPL = ["ret_norm", "ffn_norm", "mla_q_head_norm", "mla_k_head_norm", "ffn_conv_b"]
    SHARDED_SMALL = ["ffn_conv_w", "ret_gn", "mla_norm", "mla_q_norm", "mla_kv_norm"]
    dev = _idx(_place())

    def blk16(k, i=0):
        return _bf16(w[k][i])

    small_vec = jnp.concatenate([w[k].reshape(-1) for k in SHARDED_SMALL])
    n_small = small_vec.shape[0]
    small_vec = jnp.pad(small_vec, (0, 3072 - n_small)).reshape(24, 128)
    Wret_in, sg = _comm_call("gather_ret_w_in", _Gather([blk16("ret_w_in"), small_vec]))
    sg = sg.reshape(NDEV, 3072)
    o0 = 0
    conv_w_full = sg[:, o0:o0 + 2112].reshape(NDEV, 2, 3, 352).transpose(1, 2, 0, 3).reshape(2, 3, FFN)
    o0 += 2112
    ret_gn_full = sg[:, o0:o0 + 256].reshape(NDEV, RET_H, 64).transpose(1, 0, 2).reshape(RET_H, 1, RET_DV)
    o0 += 256
    mla_norm_full = sg[:, o0:o0 + 128].reshape(1, D)
    o0 += 128
    q_norm_full = sg[:, o0:o0 + 48].reshape(1, MLA_QR)
    o0 += 48
    kv_norm_full = sg[:, o0:o0 + 32].reshape(1, MLA_KVR)

    cw = [conv_w_full[i].reshape(3, 4, FSH).transpose(1, 0, 2) for i in range(2)]
    cb = [ffn_conv_b[i].reshape(4, 1, FSH) for i in range(2)]
    fnorm = [ffn_norm[i].reshape(1, D) for i in range(2)]
    rtabs = _ret_tables(S)
    mtabs = _mla_tables(S)
    qh, kh = mla_q_head_norm.reshape(1, MLA_QK), mla_k_head_norm.reshape(1, MLA_QK)
    gains = (qh[:, :MLA_NOPE], qh[:, MLA_NOPE:], kh[:, :MLA_NOPE], kh[:, MLA_NOPE:])

    x0 = x.reshape(T, D)
    tgt = loss_target.reshape(T, D)
    h0 = _rms_fwd("ret_norm", x0, ret_norm.reshape(1, D))
    proj, (Wret_out, Wffn_out0) = _mm(
        "ret_in", h0, Wret_in, grid=(T // 1024, NDEV), a_spec=_bs((1024, D), lambda m, j: (m, 0)),
        b_spec=_bs((None, D, 768), lambda m, j: (j, 0, 0)), o_spec=_bs((1024, 768), lambda m, j: (m, j)),
        out_shape=_sds((T, 6144), BF16), dims=NN, comm=_Gather([blk16("ret_w_out"), blk16("ffn_w_out", 0)]))
    Wret_out = Wret_out.reshape(RET_H * RET_DV, D)
    Wffn_out0 = Wffn_out0.reshape(4, FSH, D)
    (o_raw, rgt, states), (Wffn_in0,) = _ret_fwd(proj, rtabs, ret_gn_full, B, S, comm=_Gather([blk16("ffn_w_in", 0)]))
    x1 = _mm("ret_out", rgt, Wret_out, grid=(T // 512,),
             a_spec=_bs((512, RET_H * RET_DV), lambda m: (m, 0)), b_spec=_bs((RET_H * RET_DV, D), lambda m: (0, 0)),
             o_spec=_bs((512, D), lambda m: (m, 0)), out_shape=_sds((T, D), F32), dims=NN, res=x0,
             res_spec=_bs((512, D), lambda m: (m, 0)))
    MLA_W = ["mla_w_in", "mla_w_qb", "mla_w_kvb", "mla_w_out"]
    x2, ffn0_saved, got = _ffn_fwd(0, x1, fnorm[0], Wffn_in0, cw[0], cb[0], Wffn_out0, B, S,
                                   comm_in=_Gather([blk16(k) for k in MLA_W]))
    Wmla_in = got[0].reshape(D, MLA_QR + MLA_KVR + MLA_ROPE)
    Wq, Wkv, Wkr = Wmla_in[:, :MLA_QR], Wmla_in[:, MLA_QR:MLA_QR + MLA_KVR], Wmla_in[:, MLA_QR + MLA_KVR:]
    Wqb, Wkvb, Wmla_out = got[1:]

    h2 = _rms_fwd("mla_norm", x2, mla_norm_full)

    c_q, c_kv, k_rope = (_dense(n, h2, wm, NN, F32) for n, wm in
                         (("mla_in_q", Wq), ("mla_in_kv", Wkv), ("mla_in_kr", Wkr)))
    cqn = _rms_fwd("mla_q_norm", c_q, q_norm_full)
    ckvn = _rms_fwd("mla_kv_norm", c_kv, kv_norm_full)
    Wqb2 = jnp.pad(Wqb, ((0, 0), (0, 0), (0, MLA_PAD - MLA_QK))).transpose(1, 0, 2).reshape(MLA_QR, MLA_H * MLA_PAD)
    Wkvb2 = Wkvb.transpose(1, 0, 2).reshape(MLA_KVR, MLA_H * (MLA_NOPE + MLA_V))
    Wmla_out2 = Wmla_out.reshape(D, D)
    q_raw = _dense("mla_qb", cqn, Wqb2, NN, F32, tm=1024)
    kvh = _dense("mla_kvb", ckvn, Wkvb2, NN, F32, tm=1024)
    (att, lse, qf, kf), (Wffn_in1, Wffn_out1) = _mla_fwd(
        q_raw, kvh, k_rope, gains, mtabs, B, S, comm=_Gather([blk16("ffn_w_in", 1), blk16("ffn_w_out", 1)]))
    Wffn_out1 = Wffn_out1.reshape(4, FSH, D)
    x3 = _dense("mla_out", att, Wmla_out2, NN, F32, res=x2)
    y, ffn1_saved, _ = _ffn_fwd(1, x3, fnorm[1], Wffn_in1, cw[1], cb[1], Wffn_out1, B, S)

    dy, colsq, dyb = _loss(y, tgt)
    loss = lax.psum(0.5 * jnp.sum(colsq) / D, ("x", "y", "c"))

    dx3, dx3b, gf1, r_ffn1_out = _ffn_bwd(1, dy, dyb, ffn1_saved, fnorm[1], Wffn_in1, cw[1], cb[1], Wffn_out1, B, S)
    datt = _dense("mla_out_dx", dx3b, Wmla_out2, NT, BF16)
    (dq_raw, dkvh, dkr, dqgn, dqgr, dkgn, dkgr), (r_ffn1_in,) = _mla_bwd(
        q_raw, kvh, k_rope, att, lse, datt, qf, kf, gains, mtabs, B, S, comm=_Exchange([gf1["w_in"]]))
    dcqn = _dense("mla_qb_dx", dq_raw, Wqb2, NT, F32, tm=1024)
    dckvn = _dense("mla_kvb_dx", dkvh, Wkvb2, NT, F32, tm=1024)
    dcq, dg_qn = _rms_bwd("mla_q_norm_bwd", c_q, q_norm_full, dcqn)
    dckv, dg_kvn = _rms_bwd("mla_kv_norm_bwd", c_kv, kv_norm_full, dckvn)
    dproj2 = _bf16(jnp.concatenate([dcq, dckv, dkr], axis=-1))
    dh2 = _dense("mla_in_dx", dproj2, Wmla_in, NT, F32)
    dx2, dg_mla_norm, dx2b = _rms_bwd("mla_norm_bwd", x2, mla_norm_full, dh2, dres=dx3, also_bf16=True)

    dx1, dx1b, gf0, r_ffn0_out = _ffn_bwd(0, dx2, dx2b, ffn0_saved, fnorm[0], Wffn_in0, cw[0], cb[0], Wffn_out0, B, S)
    drgt = _mm("ret_out_dx", dx1b, Wret_out, grid=(T // 512, RET_H),
               a_spec=_bs((512, D), lambda m, j: (m, 0)), b_spec=_bs((RET_DV, D), lambda m, j: (j, 0)),
               o_spec=_bs((512, RET_DV), lambda m, j: (m, j)), out_shape=_sds((T, RET_H * RET_DV), BF16), dims=NT)
    dWret_out = _mm("ret_out_dw", rgt, dx1b, grid=(RET_H, T // 512), a_spec=_bs((512, RET_DV), lambda j, t: (t, j)),
                    b_spec=_bs((512, D), lambda j, t: (t, 0)), o_spec=_bs((RET_DV, D), lambda j, t: (j, 0)),
                    out_shape=_sds((RET_H * RET_DV, D), BF16), dims=TN, kax=1,
                    acc_shape=(RET_DV, D)).reshape(NDEV, 256, D)
    (dq, dk, dv, dg, dgn_ret), (r_ffn0_in,) = _ret_bwd(proj, o_raw, states, drgt, rtabs, ret_gn_full, B, S,
                                                       comm=_Exchange([gf0["w_in"]]))
    dproj = jnp.concatenate([dq, dk, dv, dg], axis=-1)
    dWret_in, (r_ret_out,) = _mm(
        "ret_in_dw", h0, dproj, grid=(NDEV, T // 1024), a_spec=_bs((1024, D), lambda j, t: (t, 0)),
        b_spec=_bs((1024, 768), lambda j, t: (t, j)), o_spec=_bs((None, D, 768), lambda j, t: (j, 0, 0)),
        out_shape=_sds((NDEV, D, 768), BF16), dims=TN, kax=1, acc_shape=(D, 768), comm=_Exchange([dWret_out]))
    half = D // 2
    dh0, (r_ret_in_a,) = _mm(
        "ret_in_dx", dproj, Wret_in, grid=(T // 1024, NDEV), a_spec=_bs((1024, 768), lambda m, k: (m, k)),
        b_spec=_bs((None, D, 768), lambda m, k: (k, 0, 0)), o_spec=_bs((1024, D), lambda m, k: (m, 0)),
        out_shape=_sds((T, D), F32), dims=NT, kax=1, acc_shape=(1024, D),
        comm=_Exchange([dWret_in], rows=[(0, half)]))
    dx0, dg_ret_norm = _rms_bwd("ret_norm_bwd", x0, ret_norm.reshape(1, D), dh0, dres=dx1)
    grad_x = dx0.reshape(B, S, D)
    dWmla_out, (r_ret_in_b,) = _dense("mla_out_dw", att, dx3b, TN, BF16, tm=1024,
                                      comm=_Exchange([dWret_in], rows=[(half, half)]))
    dWmla_out = dWmla_out.reshape(NDEV, MLA_V, D)
    dWqb, (r_mla_out,) = _dense("mla_qb_dw", cqn, dq_raw, TN, BF16, tm=1024, comm=_Exchange([dWmla_out]))
    dWqb = dWqb.reshape(MLA_QR, MLA_H, MLA_PAD)[:, :, :MLA_QK].transpose(1, 0, 2)
    dWkvb, (r_mla_qb,) = _dense("mla_kvb_dw", ckvn, dkvh, TN, BF16, tm=1024, comm=_Exchange([dWqb]))
    dWkvb = dWkvb.reshape(MLA_KVR, MLA_H, MLA_NOPE + MLA_V).transpose(1, 0, 2)
    dWmla_in, (r_mla_kvb,) = _dense("mla_in_dw", h2, dproj2, TN, BF16, comm=_Exchange([dWkvb]))
    dWmla_in = dWmla_in.reshape(NDEV, 128, 704)
    received = dict(ret_w_in=[r_ret_in_a, r_ret_in_b], ret_w_out=[r_ret_out], mla_w_qb=[r_mla_qb],
                    mla_w_kvb=[r_mla_kvb], mla_w_out=[r_mla_out], ffn_w_in=[r_ffn0_in, r_ffn1_in],
                    ffn_w_out=[r_ffn0_out, r_ffn1_out])

    dconv_w = jnp.stack([g_["cw"].transpose(1, 0, 2).reshape(3, FFN) for g_ in (gf0, gf1)])
    dconv_b = jnp.stack([g_["cb"].reshape(FFN) for g_ in (gf0, gf1)])
    small_parts = [dg_ret_norm, gf0["norm"], gf1["norm"], dg_mla_norm, dg_qn, dg_kvn, dqgn, dqgr, dkgn, dkgr, dgn_ret,
                   dconv_w, dconv_b]
    small_g = jnp.concatenate([p.reshape(-1) for p in small_parts]).reshape(232, 128)
    small_all = _comm_call("gather_small_grads", _Gather([small_g]))[0]
    sred = _sum8("sum_small_grads", small_all).reshape(-1)

    def take(n):
        nonlocal off
        out = sred[off:off + n]
        off += n
        return out

    off = 0
    g_small = dict(ret_norm=take(D).reshape(1, D), ffn_norm=take(2 * D).reshape(2, D), mla_norm=take(D),
                   mla_q_norm=take(MLA_QR), mla_kv_norm=take(MLA_KVR))
    g_small["mla_q_head_norm"] = take(MLA_QK).reshape(1, MLA_QK)
    g_small["mla_k_head_norm"] = take(MLA_QK).reshape(1, MLA_QK)
    g_small["ret_gn"] = take(RET_H * RET_DV).reshape(1, RET_H, RET_DV)
    g_small["ffn_conv_w"] = take(2 * 3 * FFN).reshape(2, 3, FFN)
    g_small["ffn_conv_b"] = take(2 * FFN).reshape(2, FFN)
    g_small["mla_norm"] = lax.dynamic_slice(g_small["mla_norm"], (dev * 128,), (128,)).reshape(1, 128)
    g_small["mla_q_norm"] = lax.dynamic_slice(g_small["mla_q_norm"], (dev * 48,), (48,)).reshape(1, 48)
    g_small["mla_kv_norm"] = lax.dynamic_slice(g_small["mla_kv_norm"], (dev * 32,), (32,)).reshape(1, 32)
    g_small["ret_gn"] = lax.dynamic_slice(g_small["ret_gn"], (0, 0, dev * 64), (1, RET_H, 64))
    g_small["ffn_conv_w"] = lax.dynamic_slice(g_small["ffn_conv_w"], (0, 0, dev * 352), (2, 3, 352))

    grads, delta, new_m, new_v = {}, {}, {}, {}
    for k in ["ffn_w_in"] + [k for k in BIG if k != "ffn_w_in"]:
        rcs = received[k]
        shp = w[k].shape
        R, C = rcs[0].shape[1], rcs[0].shape[2]
        rows = len(rcs) * R
        tr = max(t for t in range(16, 257, 16) if R % t == 0)
        comm = _Exchange([dWmla_in]) if k == "ffn_w_in" else None
        res = _adamw(f"adamw_{k}", rcs, w[k].reshape(rows, C), mom[k].reshape(rows, C), var[k].reshape(rows, C),
                     tr=tr, comm=comm)
        if comm is not None:
            res, received["mla_w_in"] = res
        grads[k], delta[k], new_m[k], new_v[k] = (t.reshape(shp) for t in res)
    SMALL = REPL + SHARDED_SMALL

    def pack(d):
        vflat = jnp.concatenate([d[k].reshape(-1) for k in SMALL])
        return jnp.pad(vflat, (0, 96 * 128 - vflat.shape[0])).reshape(96, 128)

    ps = _adamw("adamw_small", [pack(g_small)[None]], pack(w), pack(mom), pack(var))
    off = 0
    for k in SMALL:
        n = w[k].size
        grads[k], delta[k], new_m[k], new_v[k] = (t.reshape(-1)[off:off + n].reshape(w[k].shape) for t in ps)
        off += n
    names = list(w)
    return (loss, grad_x, *[grads[k] for k in names], *[delta[k] for k in names], *[new_m[k] for k in names],
            *[new_v[k] for k in names])
```

```python
import functools

import jax
import jax.numpy as jnp
from jax import lax
from jax.experimental import pallas as pl
from jax.experimental.pallas import tpu as pltpu

F32, BF16 = jnp.float32, jnp.bfloat16

NDEV = 8
D_MODEL = 1024
CHUNK = 64
RMS_EPS = 1e-6
ROPE_THETA = 10000.0
RET_H, RET_DK, RET_DV = 4, 256, 512
RET_SC = 256
MLA_H, MLA_QR, MLA_KVR = 8, 384, 256
MLA_NOPE, MLA_ROPE, MLA_V = 128, 64, 128
MLA_QK = MLA_NOPE + MLA_ROPE
MASK_VALUE = -1e30
FFN = 2816
FSH = FFN * 2 // NDEV
ATT_TQ = 256
ADAM_LR, ADAM_B1, ADAM_B2, ADAM_EPS, ADAM_WD, ADAM_STEP = 0.001, 0.9, 0.999, 1e-08, 0.01, 10
MESH = pl.DeviceIdType.MESH
VMEM_LIMIT = 56 * 2 ** 20


def _cp(sem):
    return pltpu.CompilerParams(dimension_semantics=sem, vmem_limit_bytes=VMEM_LIMIT)


def _dot(a, b, dims):
    return lax.dot_general(a, b, (dims, ((), ())), preferred_element_type=F32)


NN = ((1,), (0,))
NT = ((1,), (1,))
TN = ((0,), (0,))


def _place():
    return lax.axis_index("x"), lax.axis_index("y"), lax.axis_index("c")


def _idx(d):
    return 4 * d[0] + 2 * d[1] + d[2]


ANY = pl.BlockSpec(memory_space=pl.ANY)


class _Gather:
    def __init__(self, arrs):
        self.srcs = list(arrs)
        self.out_shape = [jax.ShapeDtypeStruct((NDEV,) + a.shape, a.dtype) for a in arrs]

    def _copies(self, ins, outs, send, recv, loc):
        n = len(self.srcs)
        x, y, c = _place()
        me, sib = (x, y, c), (x, y, 1 - c)
        chips = [(1 - x, y), (x, 1 - y), (1 - x, 1 - y)]

        def cp(a, k, block, to, src=None):
            dst = outs[a].at[_idx(block)]
            return pltpu.make_async_remote_copy(src_ref=dst if src is None else src, dst_ref=dst, send_sem=send.at[a, k],
                                                recv_sem=recv.at[a, k], device_id=to, device_id_type=MESH)

        mine = [pltpu.make_async_copy(ins[a], outs[a].at[_idx(me)], loc.at[a]) for a in range(n)]
        first = [cp(a, 0, me, sib, src=ins[a]) for a in range(n)]
        first += [cp(a, 1 + j, me, (*chip, c), src=ins[a]) for a in range(n) for j, chip in enumerate(chips)]
        landed = [cp(a, 1 + j, (*chip, c), me) for j, chip in enumerate(chips) for a in range(n)]
        passed = [cp(a, 4 + j, (*chip, c), sib) for j, chip in enumerate(chips) for a in range(n)]
        from_sib = [cp(a, 0, sib, me) for a in range(n)]
        from_sib += [cp(a, 4 + j, (*chip, 1 - c), me) for j, chip in enumerate(chips) for a in range(n)]
        return mine, first, landed, passed, from_sib

    def start(self, *refs):
        mine, first, _, _, _ = self._copies(*refs)
        for cp in mine + first:
            cp.start()

    def mid(self, *refs):
        _, _, landed, passed, _ = self._copies(*refs)
        for got, on in zip(landed, passed):
            got.wait_recv()
            on.start()

    def finish(self, *refs):
        mine, first, _, passed, from_sib = self._copies(*refs)
        for cp in from_sib:
            cp.wait_recv()
        for cp in first + passed:
            cp.wait_send()
        for cp in mine:
            cp.wait()


class _Exchange:
    def __init__(self, arrs, rows=None):
        self.srcs = list(arrs)
        self.rows = rows if rows is not None else [None] * len(arrs)
        self.out_shape = [jax.ShapeDtypeStruct(a.shape if r is None else (a.shape[0], r[1]) + a.shape[2:], a.dtype)
                          for a, r in zip(arrs, self.rows)]

    def _copies(self, ins, outs, send, recv, loc):
        n = len(self.srcs)
        x, y, c = _place()
        me = _idx((x, y, c))

        def src(a, q):
            r = self.rows[a]
            return ins[a].at[q] if r is None else ins[a].at[q, pl.ds(r[0], r[1])]

        mine = [pltpu.make_async_copy(src(a, me), outs[a].at[me], loc.at[a]) for a in range(n)]
        remote = []
        for k in range(1, NDEV):
            peer = (x ^ (k >> 2), y ^ ((k >> 1) & 1), c ^ (k & 1))
            remote += [pltpu.make_async_remote_copy(
                src_ref=src(a, _idx(peer)), dst_ref=outs[a].at[me], send_sem=send.at[a, k - 1],
                recv_sem=recv.at[a, k - 1], device_id=peer, device_id_type=MESH) for a in range(n)]
        return mine, remote

    def start(self, *refs):
        mine, remote = self._copies(*refs)
        for cp in mine + remote:
            cp.start()

    def mid(self, *refs):
        pass

    def finish(self, *refs):
        mine, remote = self._copies(*refs)
        for cp in remote + mine:
            cp.wait()


def _comm_scratch(n):
    return [pltpu.SemaphoreType.DMA((n, 7)), pltpu.SemaphoreType.DMA((n, 7)), pltpu.SemaphoreType.DMA((n,))]


def _comm_call(name, comm):
    n = len(comm.srcs)

    def body(*refs):
        parts = (refs[:n], refs[n:2 * n]) + tuple(refs[2 * n:])
        comm.start(*parts)
        comm.mid(*parts)
        comm.finish(*parts)

    return pl.pallas_call(body, name=name, in_specs=[ANY] * n, out_specs=[ANY] * n, out_shape=comm.out_shape,
                          scratch_shapes=_comm_scratch(n))(*comm.srcs)


def _pcall(body, *, name, grid, in_specs, out_specs, out_shape, scratch_shapes, sem, args, comm=None):
    if comm is None:
        return pl.pallas_call(body, name=name, grid=grid, in_specs=in_specs, out_specs=out_specs, out_shape=out_shape,
                              scratch_shapes=scratch_shapes, compiler_params=_cp(sem))(*args), None
    ni, no, ns, nc = len(in_specs), len(out_shape), len(scratch_shapes), len(comm.srcs)
    total = 1
    for g in grid:
        total *= g
    middle = (3 * total) // 5

    def wrapped(*refs):
        ins, csrc = refs[:ni], refs[ni:ni + nc]
        outs, cdst = refs[ni + nc:ni + nc + no], refs[ni + nc + no:ni + 2 * nc + no]
        scr, sems = refs[ni + 2 * nc + no:ni + 2 * nc + no + ns], refs[ni + 2 * nc + no + ns:]
        step = pl.program_id(0)
        for k in range(1, len(grid)):
            step = step * grid[k] + pl.program_id(k)
        parts = (csrc, cdst) + tuple(sems)

        @pl.when(step == 0)
        def _():
            comm.start(*parts)

        body(*ins, *outs, *scr)

        @pl.when(step == middle)
        def _():
            comm.mid(*parts)

        @pl.when(step == total - 1)
        def _():
            comm.finish(*parts)

    res = pl.pallas_call(
        wrapped, name=name, grid=grid, in_specs=list(in_specs) + [ANY] * nc, out_specs=list(out_specs) + [ANY] * nc,
        out_shape=list(out_shape) + comm.out_shape, scratch_shapes=list(scratch_shapes) + _comm_scratch(nc),
        compiler_params=_cp(("arbitrary",) * len(grid)))(*args, *comm.srcs)
    return res[:no], res[no:]


def _mm(name, a, b, *, grid, a_spec, b_spec, o_spec, out_shape, dims, kax=None, res=None, res_spec=None,
        jb=0, acc_shape=None, comm=None):
    nk = grid[kax] if kax is not None else 1

    def body(*refs):
        if res is not None:
            a_ref, b_ref, r_ref, o_ref = refs[:4]
        else:
            a_ref, b_ref, o_ref = refs[:3]

        def product():
            if not jb:
                return _dot(a_ref[...], b_ref[...], dims)
            part = _dot(a_ref[0], b_ref[0], dims)
            for j in range(1, jb):
                part = part + _dot(a_ref[j], b_ref[j], dims)
            return part

        def fin(acc):
            if res is not None:
                acc = acc + r_ref[...]
            o_ref[...] = acc.astype(o_ref.dtype)

        if nk == 1:
            fin(product())
        else:
            acc_ref = refs[-1]
            k = pl.program_id(kax)

            @pl.when(k == 0)
            def _():
                acc_ref[...] = jnp.zeros_like(acc_ref)

            acc_ref[...] += product()

            @pl.when(k == nk - 1)
            def _():
                fin(acc_ref[...])

    sem = tuple("arbitrary" if i == kax else "parallel" for i in range(len(grid)))
    in_specs = [a_spec, b_spec] + ([res_spec] if res is not None else [])
    args = (a, b) + ((res,) if res is not None else ())
    scratch = [pltpu.VMEM(acc_shape, F32)] if nk > 1 else []
    (out,), got = _pcall(body, name=name, grid=grid, in_specs=in_specs, out_specs=[o_spec], out_shape=[out_shape],
                         scratch_shapes=scratch, sem=sem, args=args, comm=comm)
    return out if comm is None else (out, got)


def _bs(shape, fn):
    return pl.BlockSpec(shape, fn)


def _rms_fwd(name, x, g, tm=512):
    T, D = x.shape

    def body(x_ref, g_ref, o_ref):
        xf = x_ref[...]
        r = lax.rsqrt(jnp.mean(xf * xf, axis=-1, keepdims=True) + RMS_EPS)
        o_ref[...] = (xf * r * g_ref[...]).astype(o_ref.dtype)

    return pl.pallas_call(
        body, name=name, grid=(T // tm,),
        in_specs=[_bs((tm, D), lambda i: (i, 0)), _bs((1, D), lambda i: (0, 0))],
        out_specs=_bs((tm, D), lambda i: (i, 0)), out_shape=jax.ShapeDtypeStruct((T, D), BF16),
        compiler_params=_cp(("parallel",)))(x, g)


def _rms_bwd(name, x, g, dh, dres=None, tm=512, also_bf16=False):
    T, D = x.shape

    def body(*refs):
        if also_bf16:
            refs, dxb_ref = refs[:-1], refs[-1]
        if dres is not None:
            x_ref, g_ref, dh_ref, dres_ref, dx_ref, dg_ref = refs
        else:
            x_ref, g_ref, dh_ref, dx_ref, dg_ref = refs
        i = pl.program_id(0)
        xf = x_ref[...]
        r = lax.rsqrt(jnp.mean(xf * xf, axis=-1, keepdims=True) + RMS_EPS)
        xh = xf * r
        d = dh_ref[...].astype(F32)
        dxh = d * g_ref[...]
        dx = r * (dxh - xh * jnp.mean(dxh * xh, axis=-1, keepdims=True))
        if dres is not None:
            dx = dx + dres_ref[...]
        dx_ref[...] = dx
        if also_bf16:
            dxb_ref[...] = dx.astype(BF16)
        part = jnp.sum(d * xh, axis=0, keepdims=True)

        @pl.when(i == 0)
        def _():
            dg_ref[...] = part

        @pl.when(i > 0)
        def _():
            dg_ref[...] += part

    row = _bs((tm, D), lambda i: (i, 0))
    vec = _bs((1, D), lambda i: (0, 0))
    in_specs = [row, vec, row] + ([row] if dres is not None else [])
    args = (x, g, dh) + ((dres,) if dres is not None else ())
    extra = [jax.ShapeDtypeStruct((T, D), BF16)] if also_bf16 else []
    return pl.pallas_call(
        body, name=name, grid=(T // tm,), in_specs=in_specs, out_specs=[row, vec] + [row] * len(extra),
        out_shape=[jax.ShapeDtypeStruct((T, D), F32), jax.ShapeDtypeStruct((1, D), F32)] + extra,
        compiler_params=_cp(("arbitrary",)))(*args)


def _loss(y, tgt, tm=512):
    T, D = y.shape

    def body(y_ref, t_ref, dy_ref, s_ref, dyb_ref):
        i = pl.program_id(0)
        e = y_ref[...] - t_ref[...]
        dy = e * (1.0 / D)
        dy_ref[...] = dy
        dyb_ref[...] = dy.astype(BF16)
        part = jnp.sum(e * e, axis=0, keepdims=True)

        @pl.when(i == 0)
        def _():
            s_ref[...] = part

        @pl.when(i > 0)
        def _():
            s_ref[...] += part

    row = _bs((tm, D), lambda i: (i, 0))
    return pl.pallas_call(
        body, name="loss_head", grid=(T // tm,), in_specs=[row, row],
        out_specs=[row, _bs((1, D), lambda i: (0, 0)), row],
        out_shape=[jax.ShapeDtypeStruct((T, D), F32), jax.ShapeDtypeStruct((1, D), F32),
                   jax.ShapeDtypeStruct((T, D), BF16)],
        compiler_params=_cp(("arbitrary",)))(y, tgt)


def _shift_rows(t, k, row):
    return jnp.where(row >= k, pltpu.roll(t, k, 0), 0.0)


def _shift_rows_up(t, k, row, n):
    return jnp.where(row < n - k, pltpu.roll(t, n - k, 0), 0.0)


def _convffn_fwd(name, u, cw, cb, B, S):
    _, J, T, F = u.shape

    def body(u_ref, cw_ref, cb_ref, o_ref):
        a = u_ref[0].astype(F32)
        g = u_ref[1].astype(F32)
        row = lax.broadcasted_iota(jnp.int32, (S, F), 0)
        w0, w1, w2 = cw_ref[0:1, :], cw_ref[1:2, :], cw_ref[2:3, :]
        gc = _shift_rows(g, 2, row) * w0 + _shift_rows(g, 1, row) * w1 + g * w2 + cb_ref[...]
        o_ref[...] = (gc * jax.nn.sigmoid(gc) * a).astype(o_ref.dtype)

    return pl.pallas_call(
        body, name=name, grid=(J, B),
        in_specs=[_bs((2, None, S, F), lambda j, b: (0, j, b, 0)), _bs((None, 3, F), lambda j, b: (j, 0, 0)),
                  _bs((None, 1, F), lambda j, b: (j, 0, 0))],
        out_specs=_bs((None, S, F), lambda j, b: (j, b, 0)), out_shape=jax.ShapeDtypeStruct((J, T, F), BF16),
        compiler_params=_cp(("parallel", "parallel")))(u, cw, cb)


def _convffn_bwd(name, u, cw, cb, dgt, B, S):
    _, J, T, F = u.shape

    def body(u_ref, cw_ref, cb_ref, d_ref, du_ref, dcw_ref, dcb_ref):
        b = pl.program_id(1)
        a = u_ref[0].astype(F32)
        g = u_ref[1].astype(F32)
        d = d_ref[...].astype(F32)
        row = lax.broadcasted_iota(jnp.int32, (S, F), 0)
        w0, w1, w2 = cw_ref[0:1, :], cw_ref[1:2, :], cw_ref[2:3, :]
        g1, g2 = _shift_rows(g, 1, row), _shift_rows(g, 2, row)
        gc = g2 * w0 + g1 * w1 + g * w2 + cb_ref[...]
        sg = jax.nn.sigmoid(gc)
        du_ref[0] = (d * gc * sg).astype(du_ref.dtype)
        dgc = d * a * (sg * (1.0 + gc * (1.0 - sg)))
        dg = dgc * w2 + _shift_rows_up(dgc, 1, row, S) * w1 + _shift_rows_up(dgc, 2, row, S) * w0
        du_ref[1] = dg.astype(du_ref.dtype)
        parts = [jnp.sum(dgc * g2, axis=0, keepdims=True), jnp.sum(dgc * g1, axis=0, keepdims=True),
                 jnp.sum(dgc * g, axis=0, keepdims=True)]
        pb = jnp.sum(dgc, axis=0, keepdims=True)

        @pl.when(b == 0)
        def _():
            for k in range(3):
                dcw_ref[k:k + 1, :] = parts[k]
            dcb_ref[...] = pb

        @pl.when(b > 0)
        def _():
            for k in range(3):
                dcw_ref[k:k + 1, :] += parts[k]
            dcb_ref[...] += pb

    uspec = _bs((2, None, S, F), lambda j, b: (0, j, b, 0))
    return pl.pallas_call(
        body, name=name, grid=(J, B),
        in_specs=[uspec, _bs((None, 3, F), lambda j, b: (j, 0, 0)), _bs((None, 1, F), lambda j, b: (j, 0, 0)),
                  _bs((None, S, F), lambda j, b: (j, b, 0))],
        out_specs=[uspec, _bs((None, 3, F), lambda j, b: (j, 0, 0)), _bs((None, 1, F), lambda j, b: (j, 0, 0))],
        out_shape=[jax.ShapeDtypeStruct(u.shape, BF16), jax.ShapeDtypeStruct((J, 3, F), F32),
                   jax.ShapeDtypeStruct((J, 1, F), F32)],
        compiler_params=_cp(("parallel", "arbitrary")))(u, cw, cb, dgt)


def _ret_tables(S):
    half = RET_DK // 2
    inv = ROPE_THETA ** (-jnp.arange(half, dtype=F32) / half)
    ang = jnp.arange(S).astype(F32)[:, None] * inv[None, :]
    lg = jnp.log1p(-jnp.exp2(-5.0 - jnp.arange(RET_H, dtype=F32)))
    i = jnp.arange(RET_SC, dtype=F32)
    same_or_earlier = (jnp.floor(i[None, :] / CHUNK) <= jnp.floor(i[:, None] / CHUNK)).astype(F32)
    dm = jnp.exp(lg[:, None, None] * jnp.abs(i[:, None] - i[None, :])) * same_or_earlier[None]
    qd = jnp.exp(lg[:, None] * (i + 1.0))[:, :, None]
    kd = jnp.exp(lg[:, None] * (RET_SC - 1.0 - i))[:, :, None]
    cd = jnp.exp(lg * RET_SC)[:, None, None]
    return jnp.cos(ang), jnp.sin(ang), dm, qd, kd, cd


def _rope_halves(t, cs, sn):
    h = t.shape[-1] // 2
    t1, t2 = t[:, :h], t[:, h:]
    return jnp.concatenate([t1 * cs - t2 * sn, t2 * cs + t1 * sn], axis=-1)


def _unrope_halves(d, cs, sn):
    h = d.shape[-1] // 2
    d1, d2 = d[:, :h], d[:, h:]
    return jnp.concatenate([d1 * cs + d2 * sn, d2 * cs - d1 * sn], axis=-1)


def _ret_specs(nC, order):
    SC = RET_SC

    def sp(shape, fn):
        return _bs(shape, lambda *g: fn(*order(*g)))

    q = sp((SC, RET_DK), lambda b, h, c: (b * nC + c, h))
    k = sp((SC, RET_DK), lambda b, h, c: (b * nC + c, RET_H + h))
    v = sp((SC, RET_DV), lambda b, h, c: (b * nC + c, RET_H + h))
    g = sp((SC, RET_DV), lambda b, h, c: (b * nC + c, 2 * RET_H + h))
    cs = sp((SC, RET_DK // 2), lambda b, h, c: (c, 0))
    dm = sp((None, SC, SC), lambda b, h, c: (h, 0, 0))
    dv = sp((None, SC, 1), lambda b, h, c: (h, 0, 0))
    cd = sp((None, 1, 1), lambda b, h, c: (h, 0, 0))
    gn = sp((None, 1, RET_DV), lambda b, h, c: (h, 0, 0))
    wide = sp((SC, RET_DV), lambda b, h, c: (b * nC + c, h))
    narrow = sp((SC, RET_DK), lambda b, h, c: (b * nC + c, h))
    st = sp((None, None, None, RET_DK, RET_DV), lambda b, h, c: (b, h, c, 0, 0))
    return dict(q=q, k=k, v=v, g=g, cs=cs, dm=dm, dv=dv, cd=cd, gn=gn, wide=wide, narrow=narrow, st=st)


def _ret_fwd(proj, tabs, gn, B, S, comm=None):
    T = B * S
    nC = S // RET_SC
    cos, sin, dm, qd, kd, cd = tabs
    s = _ret_specs(nC, lambda b, h, c: (b, h, c))

    def body(q_ref, k_ref, v_ref, g_ref, cos_ref, sin_ref, dm_ref, qd_ref, kd_ref, cd_ref, gn_ref,
             o_ref, gt_ref, st_ref, state):
        c = pl.program_id(2)

        @pl.when(c == 0)
        def _():
            state[...] = jnp.zeros_like(state)

        cs, sn = cos_ref[...], sin_ref[...]
        qf = _rope_halves(q_ref[...].astype(F32), cs, sn)
        kf = _rope_halves(k_ref[...].astype(F32), cs, sn) * (RET_DK ** -0.5)
        v = v_ref[...]
        p = _dot(qf.astype(BF16), kf.astype(BF16), NT) * dm_ref[...]
        st = state[...]
        stb = st.astype(BF16)
        st_ref[...] = stb
        o = _dot(p.astype(BF16), v, NN) + _dot((qf * qd_ref[...]).astype(BF16), stb, NN)
        state[...] = st * cd_ref[...] + _dot((kf * kd_ref[...]).astype(BF16), v, TN)
        o_ref[...] = o
        r = lax.rsqrt(jnp.mean(o * o, axis=-1, keepdims=True) + RMS_EPS)
        gf = g_ref[...].astype(F32)
        gt_ref[...] = ((o * r * gn_ref[...]) * (gf * jax.nn.sigmoid(gf))).astype(BF16)

    return _pcall(
        body, name="ret_fwd", grid=(B, RET_H, nC),
        in_specs=[s["q"], s["k"], s["v"], s["g"], s["cs"], s["cs"], s["dm"], s["dv"], s["dv"], s["cd"], s["gn"]],
        out_specs=[s["wide"], s["wide"], s["st"]],
        out_shape=[jax.ShapeDtypeStruct((T, RET_H * RET_DV), F32), jax.ShapeDtypeStruct((T, RET_H * RET_DV), BF16),
                   jax.ShapeDtypeStruct((B, RET_H, nC, RET_DK, RET_DV), BF16)],
        scratch_shapes=[pltpu.VMEM((RET_DK, RET_DV), F32)], sem=("parallel", "parallel", "arbitrary"),
        args=(proj, proj, proj, proj, cos, sin, dm, qd, kd, cd, gn), comm=comm)


def _ret_bwd(proj, o_raw, states, dgt, tabs, gn, B, S, comm=None):
    T = B * S
    nC = S // RET_SC
    cos, sin, dm, qd, kd, cd = tabs
    s = _ret_specs(nC, lambda h, b, c: (b, h, nC - 1 - c))

    def body(q_ref, k_ref, v_ref, g_ref, o_ref, st_ref, d_ref, cos_ref, sin_ref, dm_ref, qd_ref, kd_ref, cd_ref,
             gn_ref, dq_ref, dk_ref, dv_ref, dg_ref, dgn_ref, dstate):
        b, c = pl.program_id(1), pl.program_id(2)

        @pl.when(c == 0)
        def _():
            dstate[...] = jnp.zeros_like(dstate)

        @pl.when((b == 0) & (c == 0))
        def _():
            dgn_ref[...] = jnp.zeros_like(dgn_ref)

        cs, sn = cos_ref[...], sin_ref[...]
        qf = _rope_halves(q_ref[...].astype(F32), cs, sn)
        kf = _rope_halves(k_ref[...].astype(F32), cs, sn) * (RET_DK ** -0.5)
        v = v_ref[...]
        gnv = gn_ref[...]
        o = o_ref[...]
        r = lax.rsqrt(jnp.mean(o * o, axis=-1, keepdims=True) + RMS_EPS)
        oh = o * r
        gf = g_ref[...].astype(F32)
        sg = jax.nn.sigmoid(gf)
        d = d_ref[...].astype(F32)
        dg_ref[...] = (d * (oh * gnv) * (sg * (1.0 + gf * (1.0 - sg)))).astype(BF16)
        don = d * (gf * sg)
        dgn_ref[...] += jnp.sum(don * oh, axis=0, keepdims=True)
        doh = don * gnv
        dO = (r * (doh - oh * jnp.mean(doh * oh, axis=-1, keepdims=True))).astype(BF16)
        dmv = dm_ref[...]
        qb, kb = qf.astype(BF16), kf.astype(BF16)
        p = (_dot(qb, kb, NT) * dmv).astype(BF16)
        dp = (_dot(dO, v, NT) * dmv).astype(BF16)
        st = st_ref[...]
        dsn = dstate[...]
        dsb = dsn.astype(BF16)
        qdv, kdv = qd_ref[...], kd_ref[...]
        dq = _dot(dp, kb, NN) + _dot(dO, st, NT) * qdv
        dk = _dot(dp, qb, TN) + _dot(v, dsb, NT) * kdv
        dv = _dot(p, dO, TN) + _dot((kf * kdv).astype(BF16), dsb, NN)
        dstate[...] = dsn * cd_ref[...] + _dot((qf * qdv).astype(BF16), dO, TN)
        dq_ref[...] = _unrope_halves(dq, cs, sn).astype(BF16)
        dk_ref[...] = (_unrope_halves(dk, cs, sn) * (RET_DK ** -0.5)).astype(BF16)
        dv_ref[...] = dv.astype(BF16)

    return _pcall(
        body, name="ret_bwd", grid=(RET_H, B, nC),
        in_specs=[s["q"], s["k"], s["v"], s["g"], s["wide"], s["st"], s["wide"], s["cs"], s["cs"], s["dm"], s["dv"],
                  s["dv"], s["cd"], s["gn"]],
        out_specs=[s["narrow"], s["narrow"], s["wide"], s["wide"], s["gn"]],
        out_shape=[jax.ShapeDtypeStruct((T, RET_H * RET_DK), BF16), jax.ShapeDtypeStruct((T, RET_H * RET_DK), BF16),
                   jax.ShapeDtypeStruct((T, RET_H * RET_DV), BF16), jax.ShapeDtypeStruct((T, RET_H * RET_DV), BF16),
                   jax.ShapeDtypeStruct((RET_H, 1, RET_DV), F32)],
        scratch_shapes=[pltpu.VMEM((RET_DK, RET_DV), F32)], sem=("arbitrary", "arbitrary", "arbitrary"),
        args=(proj, proj, proj, proj, o_raw, states, dgt, cos, sin, dm, qd, kd, cd, gn), comm=comm)


MLA_PAD = 256
MLA_R2 = 2 * MLA_ROPE


def _dup(t):
    return jnp.concatenate([t, t], axis=-1)


def _fold(t):
    return t[..., :MLA_ROPE] + t[..., MLA_ROPE:]


def _mla_tables(S):
    half = MLA_ROPE // 2
    inv = ROPE_THETA ** (-jnp.arange(half, dtype=F32) / half)
    ang = jnp.arange(S).astype(F32)[:, None] * inv[None, :]
    cos, sin, zero = jnp.cos(ang), jnp.sin(ang), jnp.zeros((S, MLA_ROPE), F32)
    return jnp.concatenate([cos, cos, zero], axis=-1), jnp.concatenate([-sin, sin, zero], axis=-1)


def _head_norm_rope(n, r2, gn, gr2, cos, sin, scale):
    ssq = jnp.sum(n * n, axis=-1, keepdims=True) + 0.5 * jnp.sum(r2 * r2, axis=-1, keepdims=True)
    rstd = lax.rsqrt(ssq * (1.0 / MLA_QK) + RMS_EPS)
    yn = n * rstd * gn
    yr = r2 * rstd * gr2
    z = yr * cos + pltpu.roll(yr, MLA_ROPE // 2, 1) * sin
    if scale != 1.0:
        yn, z = yn * scale, z * scale
    return yn, z


def _head_norm_rope_bwd(dn, dz, n, r2, gn, gr2, cos, sin, scale):
    ssq = jnp.sum(n * n, axis=-1, keepdims=True) + 0.5 * jnp.sum(r2 * r2, axis=-1, keepdims=True)
    rstd = lax.rsqrt(ssq * (1.0 / MLA_QK) + RMS_EPS)
    hn, hr = n * rstd, r2 * rstd
    if scale != 1.0:
        dn, dz = dn * scale, dz * scale
    dyr = dz * cos + pltpu.roll(dz * sin, MLA_R2 - MLA_ROPE // 2, 1)
    dgn = jnp.sum(dn * hn, axis=0, keepdims=True)
    dgr = jnp.sum(dyr * hr, axis=0, keepdims=True)
    dhn, dhr = dn * gn, dyr * gr2
    mt = (jnp.sum(dhn * hn, axis=-1, keepdims=True) + jnp.sum(dhr * hr, axis=-1, keepdims=True)) * (1.0 / MLA_QK)
    return rstd * (dhn - hn * mt), rstd * (dhr - 0.5 * hr * mt), dgn, dgr


def _diag_bias():
    i = jnp.arange(ATT_TQ)
    return jnp.where((i[None, :] // CHUNK) <= (i[:, None] // CHUNK), 0.0, MASK_VALUE).astype(F32)


def _store_pair(dst, rows, n, r2):
    dst[rows, :MLA_NOPE] = n.astype(BF16)
    dst[rows, MLA_NOPE:] = r2.astype(BF16)


def _mla_fwd(q_raw, kv, kr, gains, tabs, B, S, comm=None):
    T = B * S
    TQ = ATT_TQ
    nQ = S // TQ
    qgn, qgr, kgn, kgr = gains
    cos, sin = tabs
    scale = MLA_QK ** -0.5

    def body(q_ref, kv_ref, kr_ref, qgn_ref, qgr_ref, kgn_ref, kgr_ref, c_ref, s_ref, bias_ref,
             o_ref, lse_ref, qf_s, kf_s, v_s):
        def prep(t, _):
            rows = pl.ds(pl.multiple_of(t * TQ, TQ), TQ)
            cs, sn = c_ref[rows, :], s_ref[rows, :]
            qn, qr = _head_norm_rope(q_ref[rows, :MLA_NOPE], q_ref[rows, MLA_NOPE:], qgn_ref[...], qgr_ref[...],
                                     cs, sn, scale)
            _store_pair(qf_s, rows, qn, qr)
            kn, krr = _head_norm_rope(kv_ref[rows, :MLA_NOPE], kr_ref[rows, :], kgn_ref[...], kgr_ref[...], cs, sn, 1.0)
            _store_pair(kf_s, rows, kn, krr)
            v_s[rows, :] = kv_ref[rows, MLA_NOPE:].astype(BF16)
            return 0

        lax.fori_loop(0, nQ, prep, 0, unroll=2)
        for i in range(nQ):
            rows = slice(i * TQ, (i + 1) * TQ)
            q = qf_s[rows, :]
            sd = _dot(q, kf_s[rows, :], NT) + bias_ref[...]
            m = jnp.max(sd, axis=-1, keepdims=True)
            if i:
                sl = _dot(q, kf_s[:i * TQ, :], NT)
                m = jnp.maximum(m, jnp.max(sl, axis=-1, keepdims=True))
            pd = jnp.exp(sd - m)
            l = jnp.sum(pd, axis=-1, keepdims=True)
            acc = _dot(pd.astype(BF16), v_s[rows, :], NN)
            if i:
                pl_ = jnp.exp(sl - m)
                l = l + jnp.sum(pl_, axis=-1, keepdims=True)
                acc = acc + _dot(pl_.astype(BF16), v_s[:i * TQ, :], NN)
            o_ref[rows, :] = (acc / l).astype(BF16)
            lse_ref[rows, :] = m + jnp.log(l)

    def vec(n):
        return _bs((1, n), lambda b, h: (0, 0))

    def cols(n):
        return _bs((S, n), lambda b, h: (b, h))

    tab = _bs((S, MLA_R2), lambda b, h: (0, 0))
    return _pcall(
        body, name="mla_fwd", grid=(B, MLA_H),
        in_specs=[cols(MLA_PAD), cols(MLA_NOPE + MLA_V), _bs((S, MLA_R2), lambda b, h: (b, 0)),
                  vec(MLA_NOPE), vec(MLA_R2), vec(MLA_NOPE), vec(MLA_R2), tab, tab,
                  _bs((TQ, TQ), lambda b, h: (0, 0))],
        out_specs=[cols(MLA_V), _bs((None, S, 1), lambda b, h: (h, b, 0)), cols(MLA_PAD), cols(MLA_PAD)],
        out_shape=[jax.ShapeDtypeStruct((T, MLA_H * MLA_V), BF16), jax.ShapeDtypeStruct((MLA_H, T, 1), F32),
                   jax.ShapeDtypeStruct((T, MLA_H * MLA_PAD), BF16), jax.ShapeDtypeStruct((T, MLA_H * MLA_PAD), BF16)],
        scratch_shapes=[pltpu.VMEM((S, MLA_V), BF16)],
        sem=("parallel", "parallel"), args=(q_raw, kv, kr, qgn, qgr, kgn, kgr, cos, sin, _diag_bias()), comm=comm)


def _mla_bwd(q_raw, kv, kr, o, lse, do, qf, kf, gains, tabs, B, S, comm=None):
    T = B * S
    TQ = ATT_TQ
    nQ = S // TQ
    qgn, qgr, kgn, kgr = gains
    cos, sin = tabs
    scale = MLA_QK ** -0.5

    def body(q_ref, kv_ref, kr_ref, o_ref, lse_ref, do_ref, qf_s, kf_s, qgn_ref, qgr_ref, kgn_ref, kgr_ref, c_ref, s_ref,
             bias_ref, dq_ref, dkv_ref, dkr_ref, dqgn_ref, dqgr_ref, dkgn_ref, dkgr_ref,
             v_s, dl_s, dq_s, dk_s, dv_s):
        b, h = pl.program_id(0), pl.program_id(1)

        def blk(t):
            return pl.ds(pl.multiple_of(t * TQ, TQ), TQ)

        def prep(t, _):
            rows = blk(t)
            v_s[rows, :] = kv_ref[rows, MLA_NOPE:].astype(BF16)
            dl_s[rows, :] = jnp.sum(do_ref[rows, :].astype(F32) * o_ref[rows, :].astype(F32), axis=-1, keepdims=True)
            dk_s[rows, :] = jnp.zeros((TQ, MLA_PAD), F32)
            dv_s[rows, :] = jnp.zeros((TQ, MLA_V), F32)
            return 0

        lax.fori_loop(0, nQ, prep, 0, unroll=2)

        gqn, gqr = jnp.zeros((1, MLA_NOPE), F32), jnp.zeros((1, MLA_R2), F32)
        for i in range(nQ):
            rows = slice(i * TQ, (i + 1) * TQ)
            q, doi, lse_i, dl_i = qf_s[rows, :], do_ref[rows, :], lse_ref[rows, :], dl_s[rows, :]

            def part(cols, bias):
                k, v = kf_s[cols, :], v_s[cols, :]
                s = _dot(q, k, NT)
                if bias is not None:
                    s = s + bias
                p = jnp.exp(s - lse_i)
                ds = (p * (_dot(doi, v, NT) - dl_i)).astype(BF16)
                dk_s[cols, :] += _dot(ds, q, TN)
                dv_s[cols, :] += _dot(p.astype(BF16), doi, TN)
                return _dot(ds, k, NN)

            dq = part(rows, bias_ref[...])
            if i:
                dq = dq + part(slice(0, i * TQ), None)
            dq_s[...] = dq
            dqn, dqr, a0, a1 = _head_norm_rope_bwd(dq_s[:, :MLA_NOPE], dq_s[:, MLA_NOPE:], q_ref[rows, :MLA_NOPE],
                                                   q_ref[rows, MLA_NOPE:], qgn_ref[...], qgr_ref[...],
                                                   c_ref[rows, :], s_ref[rows, :], scale)
            _store_pair(dq_ref, rows, dqn, dqr)
            gqn, gqr = gqn + a0, gqr + a1

        def post(t, carry):
            rows = blk(t)
            dkn, dkr, a2, a3 = _head_norm_rope_bwd(dk_s[rows, :MLA_NOPE], dk_s[rows, MLA_NOPE:],
                                                   kv_ref[rows, :MLA_NOPE], kr_ref[rows, :], kgn_ref[...], kgr_ref[...],
                                                   c_ref[rows, :], s_ref[rows, :], 1.0)
            dkv_ref[rows, :MLA_NOPE] = dkn.astype(BF16)
            dkv_ref[rows, MLA_NOPE:] = dv_s[rows, :].astype(BF16)

            @pl.when(h == 0)
            def _():
                dkr_ref[rows, :] = dkr

            @pl.when(h > 0)
            def _():
                dkr_ref[rows, :] += dkr

            return carry[0] + a2, carry[1] + a3

        gkn, gkr = lax.fori_loop(0, nQ, post, (jnp.zeros((1, MLA_NOPE), F32), jnp.zeros((1, MLA_R2), F32)), unroll=2)
        first = (b == 0) & (h == 0)

        @pl.when(first)
        def _():
            dqgn_ref[...] = gqn
            dqgr_ref[...] = gqr
            dkgn_ref[...] = gkn
            dkgr_ref[...] = gkr

        @pl.when(jnp.logical_not(first))
        def _():
            dqgn_ref[...] += gqn
            dqgr_ref[...] += gqr
            dkgn_ref[...] += gkn
            dkgr_ref[...] += gkr

    def vec(n):
        return _bs((1, n), lambda b, h: (0, 0))

    def cols(n):
        return _bs((S, n), lambda b, h: (b, h))

    tab = _bs((S, MLA_R2), lambda b, h: (0, 0))
    return _pcall(
        body, name="mla_bwd", grid=(B, MLA_H),
        in_specs=[cols(MLA_PAD), cols(MLA_NOPE + MLA_V), _bs((S, MLA_R2), lambda b, h: (b, 0)), cols(MLA_V),
                  _bs((None, S, 1), lambda b, h: (h, b, 0)), cols(MLA_V), cols(MLA_PAD), cols(MLA_PAD),
                  vec(MLA_NOPE), vec(MLA_R2), vec(MLA_NOPE), vec(MLA_R2), tab, tab,
                  _bs((TQ, TQ), lambda b, h: (0, 0))],
        out_specs=[cols(MLA_PAD), cols(MLA_NOPE + MLA_V), _bs((S, MLA_R2), lambda b, h: (b, 0)),
                   vec(MLA_NOPE), vec(MLA_R2), vec(MLA_NOPE), vec(MLA_R2)],
        out_shape=[jax.ShapeDtypeStruct((T, MLA_H * MLA_PAD), BF16),
                   jax.ShapeDtypeStruct((T, MLA_H * (MLA_NOPE + MLA_V)), BF16),
                   jax.ShapeDtypeStruct((T, MLA_R2), F32), jax.ShapeDtypeStruct((1, MLA_NOPE), F32),
                   jax.ShapeDtypeStruct((1, MLA_R2), F32), jax.ShapeDtypeStruct((1, MLA_NOPE), F32),
                   jax.ShapeDtypeStruct((1, MLA_R2), F32)],
        scratch_shapes=[pltpu.VMEM((S, MLA_V), BF16), pltpu.VMEM((S, 1), F32), pltpu.VMEM((TQ, MLA_PAD), F32),
                        pltpu.VMEM((S, MLA_PAD), F32), pltpu.VMEM((S, MLA_V), F32)],
        sem=("arbitrary", "arbitrary"),
        args=(q_raw, kv, kr, o, lse, do, qf, kf, qgn, qgr, kgn, kgr, cos, sin, _diag_bias()), comm=comm)


def _adamw(name, recvs, w, m, v, tr=None, comm=None):
    n, R, C = recvs[0].shape
    L = len(recvs)
    tr = R if tr is None else tr
    per = R // tr
    c1 = 1.0 - ADAM_B1 ** ADAM_STEP
    c2 = 1.0 - ADAM_B2 ** ADAM_STEP

    def body(*refs):
        r_refs = refs[:L]
        w_ref, m_ref, v_ref, g_ref, d_ref, nm_ref, nv_ref = refs[L:]
        layer = pl.program_id(0) // per

        def total(r_ref):
            t = r_ref[0].astype(F32)
            for k in range(1, n):
                t = t + r_ref[k].astype(F32)
            return t

        g = total(r_refs[0])
        for l in range(1, L):
            g = jnp.where(layer == l, total(r_refs[l]), g)
        mm = ADAM_B1 * m_ref[...] + (1.0 - ADAM_B1) * g
        vv = ADAM_B2 * v_ref[...] + (1.0 - ADAM_B2) * (g * g)
        g_ref[...] = g
        nm_ref[...] = mm
        nv_ref[...] = vv
        d_ref[...] = -ADAM_LR * ((mm / c1) / (jnp.sqrt(vv / c2) + ADAM_EPS) + ADAM_WD * w_ref[...])

    blk = _bs((tr, C), lambda i: (i, 0))
    r_specs = [_bs((n, tr, C), functools.partial(lambda l, i: (0, jnp.clip(i - l * per, 0, per - 1), 0), l))
               for l in range(L)]
    outs, got = _pcall(body, name=name, grid=(L * per,), in_specs=r_specs + [blk, blk, blk], out_specs=[blk] * 4,
                       out_shape=[jax.ShapeDtypeStruct((L * R, C), F32)] * 4, scratch_shapes=[], sem=("arbitrary",),
                       args=(*recvs, w, m, v), comm=comm)
    return outs if comm is None else (outs, got)


def _sum8(name, a):
    n, R, C = a.shape

    def body(a_ref, o_ref):
        s = a_ref[0]
        for k in range(1, n):
            s = s + a_ref[k]
        o_ref[...] = s

    return pl.pallas_call(body, name=name, out_shape=jax.ShapeDtypeStruct((R, C), a.dtype))(a)


def _sds(shape, dt):
    return jax.ShapeDtypeStruct(shape, dt)


def _proj_shared(name, h, w, out_dtype, tm=1024, comm=None):
    T, K = h.shape
    J, _, n = w.shape
    return _mm(name, h, w, grid=(T // tm, J), a_spec=_bs((tm, K), lambda m, j: (m, 0)),
               b_spec=_bs((None, K, n), lambda m, j: (j, 0, 0)), o_spec=_bs((None, tm, n), lambda m, j: (j, m, 0)),
               out_shape=_sds((J, T, n), out_dtype), dims=NN, comm=comm)


def _proj_shared_dx(name, d, w, tm=1024, comm=None):
    J, T, n = d.shape
    K = w.shape[1]
    return _mm(name, d, w, grid=(T // tm, J), a_spec=_bs((None, tm, n), lambda m, k: (k, m, 0)),
               b_spec=_bs((None, K, n), lambda m, k: (k, 0, 0)), o_spec=_bs((tm, K), lambda m, k: (m, 0)),
               out_shape=_sds((T, K), F32), dims=NT, kax=1, acc_shape=(tm, K), comm=comm)


def _proj_shared_dw(name, h, d, tt=1024, comm=None):
    T, K = h.shape
    J, _, n = d.shape
    return _mm(name, h, d, grid=(J, T // tt), a_spec=_bs((tt, K), lambda j, t: (t, 0)),
               b_spec=_bs((None, tt, n), lambda j, t: (j, t, 0)), o_spec=_bs((None, K, n), lambda j, t: (j, 0, 0)),
               out_shape=_sds((J, K, n), BF16), dims=TN, kax=1, acc_shape=(K, n), comm=comm)


def _out_proj(name, a, w, res, tm=512):
    J, T, k = a.shape
    N = w.shape[2]
    return _mm(name, a, w, grid=(T // tm,), a_spec=_bs((J, tm, k), lambda m: (0, m, 0)),
               b_spec=_bs((J, k, N), lambda m: (0, 0, 0)), o_spec=_bs((tm, N), lambda m: (m, 0)),
               out_shape=_sds((T, N), F32), dims=NN, res=res, res_spec=_bs((tm, N), lambda m: (m, 0)), jb=J)


def _out_proj_dx(name, dx, w, tm=512, comm=None):
    T, N = dx.shape
    J, k, _ = w.shape
    return _mm(name, dx, w, grid=(T // tm, J), a_spec=_bs((tm, N), lambda m, j: (m, 0)),
               b_spec=_bs((None, k, N), lambda m, j: (j, 0, 0)), o_spec=_bs((None, tm, k), lambda m, j: (j, m, 0)),
               out_shape=_sds((J, T, k), BF16), dims=NT, comm=comm)


def _out_proj_dw(name, a, dx, tt=512, comm=None):
    J, T, k = a.shape
    N = dx.shape[1]
    return _mm(name, a, dx, grid=(J, T // tt), a_spec=_bs((None, tt, k), lambda j, t: (j, t, 0)),
               b_spec=_bs((tt, N), lambda j, t: (t, 0)), o_spec=_bs((None, k, N), lambda j, t: (j, 0, 0)),
               out_shape=_sds((J, k, N), BF16), dims=TN, kax=1, acc_shape=(k, N), comm=comm)


def _dense(name, a, b, dims, out_dtype, tm=512, res=None, comm=None):
    if dims == TN:
        T, K = a.shape
        N = b.shape[1]
        return _mm(name, a, b, grid=(T // tm,), a_spec=_bs((tm, K), lambda t: (t, 0)),
                   b_spec=_bs((tm, N), lambda t: (t, 0)), o_spec=_bs((K, N), lambda t: (0, 0)),
                   out_shape=_sds((K, N), out_dtype), dims=TN, kax=0, acc_shape=(K, N), comm=comm)
    M, K = a.shape
    N = b.shape[1] if dims == NN else b.shape[0]
    row = _bs((tm, N), lambda m: (m, 0))
    return _mm(name, a, b, grid=(M // tm,), a_spec=_bs((tm, K), lambda m: (m, 0)), b_spec=_bs(b.shape, lambda m: (0, 0)),
               o_spec=row, out_shape=_sds((M, N), out_dtype), dims=dims, res=res,
               res_spec=row if res is not None else None, comm=comm)


def _bf16(x):
    return x.astype(BF16)


def _ffn_fwd(i, x, norm_g, w_in, cw, cb, w_out, B, S, comm_in=None):
    h = _rms_fwd(f"ffn{i}_norm", x, norm_g)
    u = _proj_shared(f"ffn{i}_in", h, w_in, BF16, comm=comm_in)
    u, got = u if comm_in is not None else (u, None)
    u4 = u.reshape(2, 4, u.shape[1], FSH)
    gt = _convffn_fwd(f"ffn{i}_gate", u4, cw, cb, B, S)
    y = _out_proj(f"ffn{i}_out", gt, w_out, x)
    return y, (x, h, u4, gt), got


def _ffn_bwd(i, dy, dyb, saved, norm_g, w_in, cw, cb, w_out, B, S):
    x, h, u4, gt = saved
    dgt = _out_proj_dx(f"ffn{i}_out_dx", dyb, w_out)
    dw_out = _out_proj_dw(f"ffn{i}_out_dw", gt, dyb).reshape(NDEV, FSH // 2, D_MODEL)
    du4, dcw, dcb = _convffn_bwd(f"ffn{i}_gate_bwd", u4, cw, cb, dgt, B, S)
    du = du4.reshape(NDEV, du4.shape[2], FSH)
    dh, (r_out,) = _proj_shared_dx(f"ffn{i}_in_dx", du, w_in, comm=_Exchange([dw_out]))
    dw_in = _proj_shared_dw(f"ffn{i}_in_dw", h, du)
    dx, dgn, dxb = _rms_bwd(f"ffn{i}_norm_bwd", x, norm_g, dh, dres=dy, also_bf16=True)
    return dx, dxb, dict(w_in=dw_in, norm=dgn, cw=dcw, cb=dcb), r_out


def kernel(x, ret_norm, ret_w_in, ret_gn, ret_w_out, mla_norm, mla_w_in, mla_q_norm, mla_w_qb, mla_kv_norm, mla_w_kvb, mla_q_head_norm, mla_k_head_norm, mla_w_out, ffn_norm, ffn_w_in, ffn_conv_w, ffn_conv_b, ffn_w_out, loss_target, m_ret_norm, m_ret_w_in, m_ret_gn, m_ret_w_out, m_mla_norm, m_mla_w_in, m_mla_q_norm, m_mla_w_qb, m_mla_kv_norm, m_mla_w_kvb, m_mla_q_head_norm, m_mla_k_head_norm, m_mla_w_out, m_ffn_norm, m_ffn_w_in, m_ffn_conv_w, m_ffn_conv_b, m_ffn_w_out, v_ret_norm, v_ret_w_in, v_ret_gn, v_ret_w_out, v_mla_norm, v_mla_w_in, v_mla_q_norm, v_mla_w_qb, v_mla_kv_norm, v_mla_w_kvb, v_mla_q_head_norm, v_mla_k_head_norm, v_mla_w_out, v_ffn_norm, v_ffn_w_in, v_ffn_conv_w, v_ffn_conv_b, v_ffn_w_out):
    B, S, D = x.shape
    T = B * S
    w = dict(ret_norm=ret_norm, ret_w_in=ret_w_in, ret_gn=ret_gn, ret_w_out=ret_w_out, mla_norm=mla_norm,
             mla_w_in=mla_w_in, mla_q_norm=mla_q_norm, mla_w_qb=mla_w_qb, mla_kv_norm=mla_kv_norm, mla_w_kvb=mla_w_kvb,
             mla_q_head_norm=mla_q_head_norm, mla_k_head_norm=mla_k_head_norm, mla_w_out=mla_w_out, ffn_norm=ffn_norm,
             ffn_w_in=ffn_w_in, ffn_conv_w=ffn_conv_w, ffn_conv_b=ffn_conv_b, ffn_w_out=ffn_w_out)
    mom = dict(ret_norm=m_ret_norm, ret_w_in=m_ret_w_in, ret_gn=m_ret_gn, ret_w_out=m_ret_w_out, mla_norm=m_mla_norm,
               mla_w_in=m_mla_w_in, mla_q_norm=m_mla_q_norm, mla_w_qb=m_mla_w_qb, mla_kv_norm=m_mla_kv_norm,
               mla_w_kvb=m_mla_w_kvb, mla_q_head_norm=m_mla_q_head_norm, mla_k_head_norm=m_mla_k_head_norm,
               mla_w_out=m_mla_w_out, ffn_norm=m_ffn_norm, ffn_w_in=m_ffn_w_in, ffn_conv_w=m_ffn_conv_w,
               ffn_conv_b=m_ffn_conv_b, ffn_w_out=m_ffn_w_out)
    var = dict(ret_norm=v_ret_norm, ret_w_in=v_ret_w_in, ret_gn=v_ret_gn, ret_w_out=v_ret_w_out, mla_norm=v_mla_norm,
               mla_w_in=v_mla_w_in, mla_q_norm=v_mla_q_norm, mla_w_qb=v_mla_w_qb, mla_kv_norm=v_mla_kv_norm,
               mla_w_kvb=v_mla_w_kvb, mla_q_head_norm=v_mla_q_head_norm, mla_k_head_norm=v_mla_k_head_norm,
               mla_w_out=v_mla_w_out, ffn_norm=v_ffn_norm, ffn_w_in=v_ffn_w_in, ffn_conv_w=v_ffn_conv_w,
               ffn_conv_b=v_ffn_conv_b, ffn_w_out=v_ffn_w_out)
    BIG = ["ret_w_in", "ret_w_out", "mla_w_in", "mla_w_qb", "mla_w_kvb", "mla_w_out", "ffn_w_in", "ffn_w_out"]
    REPL = ["ret_norm", "ffn_norm", "mla_q_head_norm", "mla_k_head_norm", "ffn_conv_b"]
    SHARDED_SMALL = ["ffn_conv_w", "ret_gn", "mla_norm", "mla_q_norm", "mla_kv_norm"]
    dev = _idx(_place())

    def blk16(k, i=0):
        return _bf16(w[k][i])

    small_vec = jnp.concatenate([w[k].reshape(-1) for k in SHARDED_SMALL])
    n_small = small_vec.shape[0]
    small_vec = jnp.pad(small_vec, (0, 3072 - n_small)).reshape(24, 128)
    Wret_in, sg = _comm_call("gather_ret_w_in", _Gather([blk16("ret_w_in"), small_vec]))
    sg = sg.reshape(NDEV, 3072)
    o0 = 0
    conv_w_full = sg[:, o0:o0 + 2112].reshape(NDEV, 2, 3, 352).transpose(1, 2, 0, 3).reshape(2, 3, FFN)
    o0 += 2112
    ret_gn_full = sg[:, o0:o0 + 256].reshape(NDEV, RET_H, 64).transpose(1, 0, 2).reshape(RET_H, 1, RET_DV)
    o0 += 256
    mla_norm_full = sg[:, o0:o0 + 128].reshape(1, D)
    o0 += 128
    q_norm_full = sg[:, o0:o0 + 48].reshape(1, MLA_QR)
    o0 += 48
    kv_norm_full = sg[:, o0:o0 + 32].reshape(1, MLA_KVR)

    cw = [conv_w_full[i].reshape(3, 4, FSH).transpose(1, 0, 2) for i in range(2)]
    cb = [ffn_conv_b[i].reshape(4, 1, FSH) for i in range(2)]
    fnorm = [ffn_norm[i].reshape(1, D) for i in range(2)]
    rtabs = _ret_tables(S)
    mtabs = _mla_tables(S)
    qh, kh = mla_q_head_norm.reshape(1, MLA_QK), mla_k_head_norm.reshape(1, MLA_QK)
    gains = (qh[:, :MLA_NOPE], _dup(qh[:, MLA_NOPE:]), kh[:, :MLA_NOPE], _dup(kh[:, MLA_NOPE:]))

    x0 = x.reshape(T, D)
    tgt = loss_target.reshape(T, D)
    h0 = _rms_fwd("ret_norm", x0, ret_norm.reshape(1, D))
    proj, (Wret_out, Wffn_out0) = _mm(
        "ret_in", h0, Wret_in, grid=(T // 1024, NDEV), a_spec=_bs((1024, D), lambda m, j: (m, 0)),
        b_spec=_bs((None, D, 768), lambda m, j: (j, 0, 0)), o_spec=_bs((1024, 768), lambda m, j: (m, j)),
        out_shape=_sds((T, 6144), BF16), dims=NN, comm=_Gather([blk16("ret_w_out"), blk16("ffn_w_out", 0)]))
    Wret_out = Wret_out.reshape(RET_H * RET_DV, D)
    Wffn_out0 = Wffn_out0.reshape(4, FSH, D)
    (o_raw, rgt, states), (Wffn_in0,) = _ret_fwd(proj, rtabs, ret_gn_full, B, S, comm=_Gather([blk16("ffn_w_in", 0)]))
    x1 = _mm("ret_out", rgt, Wret_out, grid=(T // 512,),
             a_spec=_bs((512, RET_H * RET_DV), lambda m: (m, 0)), b_spec=_bs((RET_H * RET_DV, D), lambda m: (0, 0)),
             o_spec=_bs((512, D), lambda m: (m, 0)), out_shape=_sds((T, D), F32), dims=NN, res=x0,
             res_spec=_bs((512, D), lambda m: (m, 0)))
    MLA_W = ["mla_w_in", "mla_w_qb", "mla_w_kvb", "mla_w_out"]
    x2, ffn0_saved, got = _ffn_fwd(0, x1, fnorm[0], Wffn_in0, cw[0], cb[0], Wffn_out0, B, S,
                                   comm_in=_Gather([blk16(k) for k in MLA_W]))
    Wmla_in = got[0].reshape(D, MLA_QR + MLA_KVR + MLA_ROPE)
    Wq, Wkv, Wkr = Wmla_in[:, :MLA_QR], Wmla_in[:, MLA_QR:MLA_QR + MLA_KVR], Wmla_in[:, MLA_QR + MLA_KVR:]
    Wqb, Wkvb, Wmla_out = got[1:]

    h2 = _rms_fwd("mla_norm", x2, mla_norm_full)

    c_q, c_kv, k_rope = (_dense(n, h2, wm, NN, F32) for n, wm in
                         (("mla_in_q", Wq), ("mla_in_kv", Wkv), ("mla_in_kr", _dup(Wkr))))
    cqn = _rms_fwd("mla_q_norm", c_q, q_norm_full)
    ckvn = _rms_fwd("mla_kv_norm", c_kv, kv_norm_full)
    Wqb2 = jnp.concatenate([Wqb, Wqb[:, :, MLA_NOPE:]], axis=2).transpose(1, 0, 2).reshape(MLA_QR, MLA_H * MLA_PAD)
    Wkvb2 = Wkvb.transpose(1, 0, 2).reshape(MLA_KVR, MLA_H * (MLA_NOPE + MLA_V))
    Wmla_out2 = Wmla_out.reshape(D, D)
    q_raw = _dense("mla_qb", cqn, Wqb2, NN, F32, tm=1024)
    kvh = _dense("mla_kvb", ckvn, Wkvb2, NN, F32, tm=1024)
    (att, lse, qf, kf), (Wffn_in1, Wffn_out1) = _mla_fwd(
        q_raw, kvh, k_rope, gains, mtabs, B, S, comm=_Gather([blk16("ffn_w_in", 1), blk16("ffn_w_out", 1)]))
    Wffn_out1 = Wffn_out1.reshape(4, FSH, D)
    x3 = _dense("mla_out", att, Wmla_out2, NN, F32, res=x2)
    y, ffn1_saved, _ = _ffn_fwd(1, x3, fnorm[1], Wffn_in1, cw[1], cb[1], Wffn_out1, B, S)

    dy, colsq, dyb = _loss(y, tgt)
    loss = lax.psum(0.5 * jnp.sum(colsq) / D, ("x", "y", "c"))

    dx3, dx3b, gf1, r_ffn1_out = _ffn_bwd(1, dy, dyb, ffn1_saved, fnorm[1], Wffn_in1, cw[1], cb[1], Wffn_out1, B, S)
    datt = _dense("mla_out_dx", dx3b, Wmla_out2, NT, BF16)
    (dq_raw, dkvh, dkr, dqgn, dqgr, dkgn, dkgr), (r_ffn1_in,) = _mla_bwd(
        q_raw, kvh, k_rope, att, lse, datt, qf, kf, gains, mtabs, B, S, comm=_Exchange([gf1["w_in"]]))
    dcqn = _dense("mla_qb_dx", dq_raw, Wqb2, NT, F32, tm=1024)
    dckvn = _dense("mla_kvb_dx", dkvh, Wkvb2, NT, F32, tm=1024)
    dcq, dg_qn = _rms_bwd("mla_q_norm_bwd", c_q, q_norm_full, dcqn)
    dckv, dg_kvn = _rms_bwd("mla_kv_norm_bwd", c_kv, kv_norm_full, dckvn)
    dqgr, dkgr = _fold(dqgr), _fold(dkgr)
    dproj2 = _bf16(jnp.concatenate([dcq, dckv, _fold(dkr)], axis=-1))
    dh2 = _dense("mla_in_dx", dproj2, Wmla_in, NT, F32)
    dx2, dg_mla_norm, dx2b = _rms_bwd("mla_norm_bwd", x2, mla_norm_full, dh2, dres=dx3, also_bf16=True)

    dx1, dx1b, gf0, r_ffn0_out = _ffn_bwd(0, dx2, dx2b, ffn0_saved, fnorm[0], Wffn_in0, cw[0], cb[0], Wffn_out0, B, S)
    drgt = _mm("ret_out_dx", dx1b, Wret_out, grid=(T // 512, RET_H),
               a_spec=_bs((512, D), lambda m, j: (m, 0)), b_spec=_bs((RET_DV, D), lambda m, j: (j, 0)),
               o_spec=_bs((512, RET_DV), lambda m, j: (m, j)), out_shape=_sds((T, RET_H * RET_DV), BF16), dims=NT)
    dWret_out = _mm("ret_out_dw", rgt, dx1b, grid=(RET_H, T // 512), a_spec=_bs((512, RET_DV), lambda j, t: (t, j)),
                    b_spec=_bs((512, D), lambda j, t: (t, 0)), o_spec=_bs((RET_DV, D), lambda j, t: (j, 0)),
                    out_shape=_sds((RET_H * RET_DV, D), BF16), dims=TN, kax=1,
                    acc_shape=(RET_DV, D)).reshape(NDEV, 256, D)
    (dq, dk, dv, dg, dgn_ret), (r_ffn0_in,) = _ret_bwd(proj, o_raw, states, drgt, rtabs, ret_gn_full, B, S,
                                                       comm=_Exchange([gf0["w_in"]]))
    dproj = jnp.concatenate([dq, dk, dv, dg], axis=-1)
    dWret_in, (r_ret_out,) = _mm(
        "ret_in_dw", h0, dproj, grid=(NDEV, T // 1024), a_spec=_bs((1024, D), lambda j, t: (t, 0)),
        b_spec=_bs((1024, 768), lambda j, t: (t, j)), o_spec=_bs((None, D, 768), lambda j, t: (j, 0, 0)),
        out_shape=_sds((NDEV, D, 768), BF16), dims=TN, kax=1, acc_shape=(D, 768), comm=_Exchange([dWret_out]))
    half = D // 2
    dh0, (r_ret_in_a,) = _mm(
        "ret_in_dx", dproj, Wret_in, grid=(T // 1024, NDEV), a_spec=_bs((1024, 768), lambda m, k: (m, k)),
        b_spec=_bs((None, D, 768), lambda m, k: (k, 0, 0)), o_spec=_bs((1024, D), lambda m, k: (m, 0)),
        out_shape=_sds((T, D), F32), dims=NT, kax=1, acc_shape=(1024, D),
        comm=_Exchange([dWret_in], rows=[(0, half)]))
    dx0, dg_ret_norm = _rms_bwd("ret_norm_bwd", x0, ret_norm.reshape(1, D), dh0, dres=dx1)
    grad_x = dx0.reshape(B, S, D)
    dWmla_out, (r_ret_in_b,) = _dense("mla_out_dw", att, dx3b, TN, BF16, tm=1024,
                                      comm=_Exchange([dWret_in], rows=[(half, half)]))
    dWmla_out = dWmla_out.reshape(NDEV, MLA_V, D)
    dWqb, (r_mla_out,) = _dense("mla_qb_dw", cqn, dq_raw, TN, BF16, tm=1024, comm=_Exchange([dWmla_out]))
    dWqb = dWqb.reshape(MLA_QR, MLA_H, MLA_PAD)
    dWqb = jnp.concatenate([dWqb[:, :, :MLA_NOPE], _fold(dWqb[:, :, MLA_NOPE:])], axis=2).transpose(1, 0, 2)
    dWkvb, (r_mla_qb,) = _dense("mla_kvb_dw", ckvn, dkvh, TN, BF16, tm=1024, comm=_Exchange([dWqb]))
    dWkvb = dWkvb.reshape(MLA_KVR, MLA_H, MLA_NOPE + MLA_V).transpose(1, 0, 2)
    dWmla_in, (r_mla_kvb,) = _dense("mla_in_dw", h2, dproj2, TN, BF16, comm=_Exchange([dWkvb]))
    dWmla_in = dWmla_in.reshape(NDEV, 128, 704)
    received = dict(ret_w_in=[r_ret_in_a, r_ret_in_b], ret_w_out=[r_ret_out], mla_w_qb=[r_mla_qb],
                    mla_w_kvb=[r_mla_kvb], mla_w_out=[r_mla_out], ffn_w_in=[r_ffn0_in, r_ffn1_in],
                    ffn_w_out=[r_ffn0_out, r_ffn1_out])

    dconv_w = jnp.stack([g_["cw"].transpose(1, 0, 2).reshape(3, FFN) for g_ in (gf0, gf1)])
    dconv_b = jnp.stack([g_["cb"].reshape(FFN) for g_ in (gf0, gf1)])
    small_parts = [dg_ret_norm, gf0["norm"], gf1["norm"], dg_mla_norm, dg_qn, dg_kvn, dqgn, dqgr, dkgn, dkgr, dgn_ret,
                   dconv_w, dconv_b]
    small_g = jnp.concatenate([p.reshape(-1) for p in small_parts]).reshape(232, 128)
    small_all = _comm_call("gather_small_grads", _Gather([small_g]))[0]
    sred = _sum8("sum_small_grads", small_all).reshape(-1)

    def take(n):
        nonlocal off
        out = sred[off:off + n]
        off += n
        return out

    off = 0
    g_small = dict(ret_norm=take(D).reshape(1, D), ffn_norm=take(2 * D).reshape(2, D), mla_norm=take(D),
                   mla_q_norm=take(MLA_QR), mla_kv_norm=take(MLA_KVR))
    g_small["mla_q_head_norm"] = take(MLA_QK).reshape(1, MLA_QK)
    g_small["mla_k_head_norm"] = take(MLA_QK).reshape(1, MLA_QK)
    g_small["ret_gn"] = take(RET_H * RET_DV).reshape(1, RET_H, RET_DV)
    g_small["ffn_conv_w"] = take(2 * 3 * FFN).reshape(2, 3, FFN)
    g_small["ffn_conv_b"] = take(2 * FFN).reshape(2, FFN)
    g_small["mla_norm"] = lax.dynamic_slice(g_small["mla_norm"], (dev * 128,), (128,)).reshape(1, 128)
    g_small["mla_q_norm"] = lax.dynamic_slice(g_small["mla_q_norm"], (dev * 48,), (48,)).reshape(1, 48)
    g_small["mla_kv_norm"] = lax.dynamic_slice(g_small["mla_kv_norm"], (dev * 32,), (32,)).reshape(1, 32)
    g_small["ret_gn"] = lax.dynamic_slice(g_small["ret_gn"], (0, 0, dev * 64), (1, RET_H, 64))
    g_small["ffn_conv_w"] = lax.dynamic_slice(g_small["ffn_conv_w"], (0, 0, dev * 352), (2, 3, 352))

    grads, delta, new_m, new_v = {}, {}, {}, {}
    for k in ["ffn_w_in"] + [k for k in BIG if k != "ffn_w_in"]:
        rcs = received[k]
        shp = w[k].shape
        R, C = rcs[0].shape[1], rcs[0].shape[2]
        rows = len(rcs) * R
        tr = max(t for t in range(16, 257, 16) if R % t == 0)
        comm = _Exchange([dWmla_in]) if k == "ffn_w_in" else None
        res = _adamw(f"adamw_{k}", rcs, w[k].reshape(rows, C), mom[k].reshape(rows, C), var[k].reshape(rows, C),
                     tr=tr, comm=comm)
        if comm is not None:
            res, received["mla_w_in"] = res
        grads[k], delta[k], new_m[k], new_v[k] = (t.reshape(shp) for t in res)
    SMALL = REPL + SHARDED_SMALL

    def pack(d):
        vflat = jnp.concatenate([d[k].reshape(-1) for k in SMALL])
        return jnp.pad(vflat, (0, 96 * 128 - vflat.shape[0])).reshape(96, 128)

    ps = _adamw("adamw_small", [pack(g_small)[None]], pack(w), pack(mom), pack(var))
    off = 0
    for k in SMALL:
        n = w[k].size
        grads[k], delta[k], new_m[k], new_v[k] = (t.reshape(-1)[off:off + n].reshape(w[k].shape) for t in ps)
        off += n
    names = list(w)
    return (loss, grad_x, *[grads[k] for k in names], *[delta[k] for k in names], *[new_m[k] for k in names],
            *[new_v[k] for k in names])
```

```python
import functools

import jax
import jax.numpy as jnp
from jax import lax
from jax.experimental import pallas as pl
from jax.experimental.pallas import tpu as pltpu

F32, BF16 = jnp.float32, jnp.bfloat16

NDEV = 8
D_MODEL = 1024
CHUNK = 64
RMS_EPS = 1e-6
ROPE_THETA = 10000.0
RET_H, RET_DK, RET_DV = 4, 256, 512
RET_SC = 256
MLA_H, MLA_QR, MLA_KVR = 8, 384, 256
MLA_NOPE, MLA_ROPE, MLA_V = 128, 64, 128
MLA_QK = MLA_NOPE + MLA_ROPE
MASK_VALUE = -1e30
FFN = 2816
FSH = FFN * 2 // NDEV
ATT_TQ = 256
ADAM_LR, ADAM_B1, ADAM_B2, ADAM_EPS, ADAM_WD, ADAM_STEP = 0.001, 0.9, 0.999, 1e-08, 0.01, 10
MESH = pl.DeviceIdType.MESH
VMEM_LIMIT = 56 * 2 ** 20


def _cp(sem):
    return pltpu.CompilerParams(dimension_semantics=sem, vmem_limit_bytes=VMEM_LIMIT)


def _dot(a, b, dims):
    return lax.dot_general(a, b, (dims, ((), ())), preferred_element_type=F32)


NN = ((1,), (0,))
NT = ((1,), (1,))
TN = ((0,), (0,))


def _place():
    return lax.axis_index("x"), lax.axis_index("y"), lax.axis_index("c")


def _idx(d):
    return 4 * d[0] + 2 * d[1] + d[2]


ANY = pl.BlockSpec(memory_space=pl.ANY)


class _Gather:
    def __init__(self, arrs):
        self.srcs = list(arrs)
        self.out_shape = [jax.ShapeDtypeStruct((NDEV,) + a.shape, a.dtype) for a in arrs]

    def _copies(self, ins, outs, send, recv, loc):
        n = len(self.srcs)
        x, y, c = _place()
        me, sib = (x, y, c), (x, y, 1 - c)
        chips = [(1 - x, y), (x, 1 - y), (1 - x, 1 - y)]

        def cp(a, k, block, to, src=None):
            dst = outs[a].at[_idx(block)]
            return pltpu.make_async_remote_copy(src_ref=dst if src is None else src, dst_ref=dst, send_sem=send.at[a, k],
                                                recv_sem=recv.at[a, k], device_id=to, device_id_type=MESH)

        mine = [pltpu.make_async_copy(ins[a], outs[a].at[_idx(me)], loc.at[a]) for a in range(n)]
        first = [cp(a, 0, me, sib, src=ins[a]) for a in range(n)]
        first += [cp(a, 1 + j, me, (*chip, c), src=ins[a]) for a in range(n) for j, chip in enumerate(chips)]
        landed = [cp(a, 1 + j, (*chip, c), me) for j, chip in enumerate(chips) for a in range(n)]
        passed = [cp(a, 4 + j, (*chip, c), sib) for j, chip in enumerate(chips) for a in range(n)]
        from_sib = [cp(a, 0, sib, me) for a in range(n)]
        from_sib += [cp(a, 4 + j, (*chip, 1 - c), me) for j, chip in enumerate(chips) for a in range(n)]
        return mine, first, landed, passed, from_sib

    def start(self, *refs):
        mine, first, _, _, _ = self._copies(*refs)
        for cp in mine + first:
            cp.start()

    def mid(self, *refs):
        _, _, landed, passed, _ = self._copies(*refs)
        for got, on in zip(landed, passed):
            got.wait_recv()
            on.start()

    def finish(self, *refs):
        mine, first, _, passed, from_sib = self._copies(*refs)
        for cp in from_sib:
            cp.wait_recv()
        for cp in first + passed:
            cp.wait_send()
        for cp in mine:
            cp.wait()


class _Exchange:
    def __init__(self, arrs, rows=None):
        self.srcs = list(arrs)
        self.rows = rows if rows is not None else [None] * len(arrs)
        self.out_shape = [jax.ShapeDtypeStruct(a.shape if r is None else (a.shape[0], r[1]) + a.shape[2:], a.dtype)
                          for a, r in zip(arrs, self.rows)]

    def _copies(self, ins, outs, send, recv, loc):
        n = len(self.srcs)
        x, y, c = _place()
        me = _idx((x, y, c))

        def src(a, q):
            r = self.rows[a]
            return ins[a].at[q] if r is None else ins[a].at[q, pl.ds(r[0], r[1])]

        mine = [pltpu.make_async_copy(src(a, me), outs[a].at[me], loc.at[a]) for a in range(n)]
        remote = []
        for k in range(1, NDEV):
            peer = (x ^ (k >> 2), y ^ ((k >> 1) & 1), c ^ (k & 1))
            remote += [pltpu.make_async_remote_copy(
                src_ref=src(a, _idx(peer)), dst_ref=outs[a].at[me], send_sem=send.at[a, k - 1],
                recv_sem=recv.at[a, k - 1], device_id=peer, device_id_type=MESH) for a in range(n)]
        return mine, remote

    def start(self, *refs):
        mine, remote = self._copies(*refs)
        for cp in mine + remote:
            cp.start()

    def mid(self, *refs):
        pass

    def finish(self, *refs):
        mine, remote = self._copies(*refs)
        for cp in remote + mine:
            cp.wait()


def _comm_scratch(n):
    return [pltpu.SemaphoreType.DMA((n, 7)), pltpu.SemaphoreType.DMA((n, 7)), pltpu.SemaphoreType.DMA((n,))]


def _comm_call(name, comm):
    n = len(comm.srcs)

    def body(*refs):
        parts = (refs[:n], refs[n:2 * n]) + tuple(refs[2 * n:])
        comm.start(*parts)
        comm.mid(*parts)
        comm.finish(*parts)

    return pl.pallas_call(body, name=name, in_specs=[ANY] * n, out_specs=[ANY] * n, out_shape=comm.out_shape,
                          scratch_shapes=_comm_scratch(n))(*comm.srcs)


def _pcall(body, *, name, grid, in_specs, out_specs, out_shape, scratch_shapes, sem, args, comm=None):
    if comm is None:
        return pl.pallas_call(body, name=name, grid=grid, in_specs=in_specs, out_specs=out_specs, out_shape=out_shape,
                              scratch_shapes=scratch_shapes, compiler_params=_cp(sem))(*args), None
    ni, no, ns, nc = len(in_specs), len(out_shape), len(scratch_shapes), len(comm.srcs)
    total = 1
    for g in grid:
        total *= g
    middle = (3 * total) // 5

    def wrapped(*refs):
        ins, csrc = refs[:ni], refs[ni:ni + nc]
        outs, cdst = refs[ni + nc:ni + nc + no], refs[ni + nc + no:ni + 2 * nc + no]
        scr, sems = refs[ni + 2 * nc + no:ni + 2 * nc + no + ns], refs[ni + 2 * nc + no + ns:]
        step = pl.program_id(0)
        for k in range(1, len(grid)):
            step = step * grid[k] + pl.program_id(k)
        parts = (csrc, cdst) + tuple(sems)

        @pl.when(step == 0)
        def _():
            comm.start(*parts)

        body(*ins, *outs, *scr)

        @pl.when(step == middle)
        def _():
            comm.mid(*parts)

        @pl.when(step == total - 1)
        def _():
            comm.finish(*parts)

    res = pl.pallas_call(
        wrapped, name=name, grid=grid, in_specs=list(in_specs) + [ANY] * nc, out_specs=list(out_specs) + [ANY] * nc,
        out_shape=list(out_shape) + comm.out_shape, scratch_shapes=list(scratch_shapes) + _comm_scratch(nc),
        compiler_params=_cp(("arbitrary",) * len(grid)))(*args, *comm.srcs)
    return res[:no], res[no:]


def _mm(name, a, b, *, grid, a_spec, b_spec, o_spec, out_shape, dims, kax=None, res=None, res_spec=None,
        jb=0, acc_shape=None, comm=None):
    nk = grid[kax] if kax is not None else 1

    def body(*refs):
        if res is not None:
            a_ref, b_ref, r_ref, o_ref = refs[:4]
        else:
            a_ref, b_ref, o_ref = refs[:3]

        def product():
            if not jb:
                return _dot(a_ref[...], b_ref[...], dims)
            part = _dot(a_ref[0], b_ref[0], dims)
            for j in range(1, jb):
                part = part + _dot(a_ref[j], b_ref[j], dims)
            return part

        def fin(acc):
            if res is not None:
                acc = acc + r_ref[...]
            o_ref[...] = acc.astype(o_ref.dtype)

        if nk == 1:
            fin(product())
        else:
            acc_ref = refs[-1]
            k = pl.program_id(kax)

            @pl.when(k == 0)
            def _():
                acc_ref[...] = jnp.zeros_like(acc_ref)

            acc_ref[...] += product()

            @pl.when(k == nk - 1)
            def _():
                fin(acc_ref[...])

    sem = tuple("arbitrary" if i == kax else "parallel" for i in range(len(grid)))
    in_specs = [a_spec, b_spec] + ([res_spec] if res is not None else [])
    args = (a, b) + ((res,) if res is not None else ())
    scratch = [pltpu.VMEM(acc_shape, F32)] if nk > 1 else []
    (out,), got = _pcall(body, name=name, grid=grid, in_specs=in_specs, out_specs=[o_spec], out_shape=[out_shape],
                         scratch_shapes=scratch, sem=sem, args=args, comm=comm)
    return out if comm is None else (out, got)


def _bs(shape, fn):
    return pl.BlockSpec(shape, fn)


def _rms_fwd(name, x, g, tm=512):
    T, D = x.shape

    def body(x_ref, g_ref, o_ref):
        xf = x_ref[...]
        r = lax.rsqrt(jnp.mean(xf * xf, axis=-1, keepdims=True) + RMS_EPS)
        o_ref[...] = (xf * r * g_ref[...]).astype(o_ref.dtype)

    return pl.pallas_call(
        body, name=name, grid=(T // tm,),
        in_specs=[_bs((tm, D), lambda i: (i, 0)), _bs((1, D), lambda i: (0, 0))],
        out_specs=_bs((tm, D), lambda i: (i, 0)), out_shape=jax.ShapeDtypeStruct((T, D), BF16),
        compiler_params=_cp(("parallel",)))(x, g)


def _rms_bwd(name, x, g, dh, dres=None, tm=512, also_bf16=False):
    T, D = x.shape

    def body(*refs):
        if also_bf16:
            refs, dxb_ref = refs[:-1], refs[-1]
        if dres is not None:
            x_ref, g_ref, dh_ref, dres_ref, dx_ref, dg_ref = refs
        else:
            x_ref, g_ref, dh_ref, dx_ref, dg_ref = refs
        i = pl.program_id(0)
        xf = x_ref[...]
        r = lax.rsqrt(jnp.mean(xf * xf, axis=-1, keepdims=True) + RMS_EPS)
        xh = xf * r
        d = dh_ref[...].astype(F32)
        dxh = d * g_ref[...]
        dx = r * (dxh - xh * jnp.mean(dxh * xh, axis=-1, keepdims=True))
        if dres is not None:
            dx = dx + dres_ref[...]
        dx_ref[...] = dx
        if also_bf16:
            dxb_ref[...] = dx.astype(BF16)
        part = jnp.sum(d * xh, axis=0, keepdims=True)

        @pl.when(i == 0)
        def _():
            dg_ref[...] = part

        @pl.when(i > 0)
        def _():
            dg_ref[...] += part

    row = _bs((tm, D), lambda i: (i, 0))
    vec = _bs((1, D), lambda i: (0, 0))
    in_specs = [row, vec, row] + ([row] if dres is not None else [])
    args = (x, g, dh) + ((dres,) if dres is not None else ())
    extra = [jax.ShapeDtypeStruct((T, D), BF16)] if also_bf16 else []
    return pl.pallas_call(
        body, name=name, grid=(T // tm,), in_specs=in_specs, out_specs=[row, vec] + [row] * len(extra),
        out_shape=[jax.ShapeDtypeStruct((T, D), F32), jax.ShapeDtypeStruct((1, D), F32)] + extra,
        compiler_params=_cp(("arbitrary",)))(*args)


def _loss(y, tgt, tm=512):
    T, D = y.shape

    def body(y_ref, t_ref, dy_ref, s_ref, dyb_ref):
        i = pl.program_id(0)
        e = y_ref[...] - t_ref[...]
        dy = e * (1.0 / D)
        dy_ref[...] = dy
        dyb_ref[...] = dy.astype(BF16)
        part = jnp.sum(e * e, axis=0, keepdims=True)

        @pl.when(i == 0)
        def _():
            s_ref[...] = part

        @pl.when(i > 0)
        def _():
            s_ref[...] += part

    row = _bs((tm, D), lambda i: (i, 0))
    return pl.pallas_call(
        body, name="loss_head", grid=(T // tm,), in_specs=[row, row],
        out_specs=[row, _bs((1, D), lambda i: (0, 0)), row],
        out_shape=[jax.ShapeDtypeStruct((T, D), F32), jax.ShapeDtypeStruct((1, D), F32),
                   jax.ShapeDtypeStruct((T, D), BF16)],
        compiler_params=_cp(("arbitrary",)))(y, tgt)


def _shift_rows(t, k, row):
    return jnp.where(row >= k, pltpu.roll(t, k, 0), 0.0)


def _shift_rows_up(t, k, row, n):
    return jnp.where(row < n - k, pltpu.roll(t, n - k, 0), 0.0)


def _convffn_fwd(name, u, cw, cb, B, S):
    _, J, T, F = u.shape

    def body(u_ref, cw_ref, cb_ref, o_ref):
        a = u_ref[0].astype(F32)
        g = u_ref[1].astype(F32)
        row = lax.broadcasted_iota(jnp.int32, (S, F), 0)
        w0, w1, w2 = cw_ref[0:1, :], cw_ref[1:2, :], cw_ref[2:3, :]
        gc = _shift_rows(g, 2, row) * w0 + _shift_rows(g, 1, row) * w1 + g * w2 + cb_ref[...]
        o_ref[...] = (gc * jax.nn.sigmoid(gc) * a).astype(o_ref.dtype)

    return pl.pallas_call(
        body, name=name, grid=(J, B),
        in_specs=[_bs((2, None, S, F), lambda j, b: (0, j, b, 0)), _bs((None, 3, F), lambda j, b: (j, 0, 0)),
                  _bs((None, 1, F), lambda j, b: (j, 0, 0))],
        out_specs=_bs((None, S, F), lambda j, b: (j, b, 0)), out_shape=jax.ShapeDtypeStruct((J, T, F), BF16),
        compiler_params=_cp(("parallel", "parallel")))(u, cw, cb)


def _convffn_bwd(name, u, cw, cb, dgt, B, S):
    _, J, T, F = u.shape

    def body(u_ref, cw_ref, cb_ref, d_ref, du_ref, dcw_ref, dcb_ref):
        b = pl.program_id(1)
        a = u_ref[0].astype(F32)
        g = u_ref[1].astype(F32)
        d = d_ref[...].astype(F32)
        row = lax.broadcasted_iota(jnp.int32, (S, F), 0)
        w0, w1, w2 = cw_ref[0:1, :], cw_ref[1:2, :], cw_ref[2:3, :]
        g1, g2 = _shift_rows(g, 1, row), _shift_rows(g, 2, row)
        gc = g2 * w0 + g1 * w1 + g * w2 + cb_ref[...]
        sg = jax.nn.sigmoid(gc)
        du_ref[0] = (d * gc * sg).astype(du_ref.dtype)
        dgc = d * a * (sg * (1.0 + gc * (1.0 - sg)))
        dg = dgc * w2 + _shift_rows_up(dgc, 1, row, S) * w1 + _shift_rows_up(dgc, 2, row, S) * w0
        du_ref[1] = dg.astype(du_ref.dtype)
        parts = [jnp.sum(dgc * g2, axis=0, keepdims=True), jnp.sum(dgc * g1, axis=0, keepdims=True),
                 jnp.sum(dgc * g, axis=0, keepdims=True)]
        pb = jnp.sum(dgc, axis=0, keepdims=True)

        @pl.when(b == 0)
        def _():
            for k in range(3):
                dcw_ref[k:k + 1, :] = parts[k]
            dcb_ref[...] = pb

        @pl.when(b > 0)
        def _():
            for k in range(3):
                dcw_ref[k:k + 1, :] += parts[k]
            dcb_ref[...] += pb

    uspec = _bs((2, None, S, F), lambda j, b: (0, j, b, 0))
    return pl.pallas_call(
        body, name=name, grid=(J, B),
        in_specs=[uspec, _bs((None, 3, F), lambda j, b: (j, 0, 0)), _bs((None, 1, F), lambda j, b: (j, 0, 0)),
                  _bs((None, S, F), lambda j, b: (j, b, 0))],
        out_specs=[uspec, _bs((None, 3, F), lambda j, b: (j, 0, 0)), _bs((None, 1, F), lambda j, b: (j, 0, 0))],
        out_shape=[jax.ShapeDtypeStruct(u.shape, BF16), jax.ShapeDtypeStruct((J, 3, F), F32),
                   jax.ShapeDtypeStruct((J, 1, F), F32)],
        compiler_params=_cp(("parallel", "arbitrary")))(u, cw, cb, dgt)


def _ret_tables(S):
    half = RET_DK // 2
    inv = ROPE_THETA ** (-jnp.arange(half, dtype=F32) / half)
    ang = jnp.arange(S).astype(F32)[:, None] * inv[None, :]
    lg = jnp.log1p(-jnp.exp2(-5.0 - jnp.arange(RET_H, dtype=F32)))
    i = jnp.arange(RET_SC, dtype=F32)
    same_or_earlier = (jnp.floor(i[None, :] / CHUNK) <= jnp.floor(i[:, None] / CHUNK)).astype(F32)
    dm = jnp.exp(lg[:, None, None] * jnp.abs(i[:, None] - i[None, :])) * same_or_earlier[None]
    qd = jnp.exp(lg[:, None] * (i + 1.0))[:, :, None]
    kd = jnp.exp(lg[:, None] * (RET_SC - 1.0 - i))[:, :, None]
    cd = jnp.exp(lg * RET_SC)[:, None, None]
    return jnp.cos(ang), jnp.sin(ang), dm, qd, kd, cd


def _rope_halves(t, cs, sn):
    h = t.shape[-1] // 2
    t1, t2 = t[:, :h], t[:, h:]
    return jnp.concatenate([t1 * cs - t2 * sn, t2 * cs + t1 * sn], axis=-1)


def _unrope_halves(d, cs, sn):
    h = d.shape[-1] // 2
    d1, d2 = d[:, :h], d[:, h:]
    return jnp.concatenate([d1 * cs + d2 * sn, d2 * cs - d1 * sn], axis=-1)


def _ret_specs(nC, order):
    SC = RET_SC

    def sp(shape, fn):
        return _bs(shape, lambda *g: fn(*order(*g)))

    q = sp((SC, RET_DK), lambda b, h, c: (b * nC + c, h))
    k = sp((SC, RET_DK), lambda b, h, c: (b * nC + c, RET_H + h))
    v = sp((SC, RET_DV), lambda b, h, c: (b * nC + c, RET_H + h))
    g = sp((SC, RET_DV), lambda b, h, c: (b * nC + c, 2 * RET_H + h))
    cs = sp((SC, RET_DK // 2), lambda b, h, c: (c, 0))
    dm = sp((None, SC, SC), lambda b, h, c: (h, 0, 0))
    dv = sp((None, SC, 1), lambda b, h, c: (h, 0, 0))
    cd = sp((None, 1, 1), lambda b, h, c: (h, 0, 0))
    gn = sp((None, 1, RET_DV), lambda b, h, c: (h, 0, 0))
    wide = sp((SC, RET_DV), lambda b, h, c: (b * nC + c, h))
    narrow = sp((SC, RET_DK), lambda b, h, c: (b * nC + c, h))
    st = sp((None, None, None, RET_DK, RET_DV), lambda b, h, c: (b, h, c, 0, 0))
    return dict(q=q, k=k, v=v, g=g, cs=cs, dm=dm, dv=dv, cd=cd, gn=gn, wide=wide, narrow=narrow, st=st)


def _ret_fwd(proj, tabs, gn, B, S, comm=None):
    T = B * S
    nC = S // RET_SC
    cos, sin, dm, qd, kd, cd = tabs
    s = _ret_specs(nC, lambda b, h, c: (b, h, c))

    def body(q_ref, k_ref, v_ref, g_ref, cos_ref, sin_ref, dm_ref, qd_ref, kd_ref, cd_ref, gn_ref,
             o_ref, gt_ref, st_ref, state):
        c = pl.program_id(2)

        @pl.when(c == 0)
        def _():
            state[...] = jnp.zeros_like(state)

        cs, sn = cos_ref[...], sin_ref[...]
        qf = _rope_halves(q_ref[...].astype(F32), cs, sn)
        kf = _rope_halves(k_ref[...].astype(F32), cs, sn) * (RET_DK ** -0.5)
        v = v_ref[...]
        p = _dot(qf.astype(BF16), kf.astype(BF16), NT) * dm_ref[...]
        st = state[...]
        stb = st.astype(BF16)
        st_ref[...] = stb
        o = _dot(p.astype(BF16), v, NN) + _dot((qf * qd_ref[...]).astype(BF16), stb, NN)
        state[...] = st * cd_ref[...] + _dot((kf * kd_ref[...]).astype(BF16), v, TN)
        o_ref[...] = o
        r = lax.rsqrt(jnp.mean(o * o, axis=-1, keepdims=True) + RMS_EPS)
        gf = g_ref[...].astype(F32)
        gt_ref[...] = ((o * r * gn_ref[...]) * (gf * jax.nn.sigmoid(gf))).astype(BF16)

    return _pcall(
        body, name="ret_fwd", grid=(B, RET_H, nC),
        in_specs=[s["q"], s["k"], s["v"], s["g"], s["cs"], s["cs"], s["dm"], s["dv"], s["dv"], s["cd"], s["gn"]],
        out_specs=[s["wide"], s["wide"], s["st"]],
        out_shape=[jax.ShapeDtypeStruct((T, RET_H * RET_DV), F32), jax.ShapeDtypeStruct((T, RET_H * RET_DV), BF16),
                   jax.ShapeDtypeStruct((B, RET_H, nC, RET_DK, RET_DV), BF16)],
        scratch_shapes=[pltpu.VMEM((RET_DK, RET_DV), F32)], sem=("parallel", "parallel", "arbitrary"),
        args=(proj, proj, proj, proj, cos, sin, dm, qd, kd, cd, gn), comm=comm)


def _ret_bwd(proj, o_raw, states, dgt, tabs, gn, B, S, comm=None):
    T = B * S
    nC = S // RET_SC
    cos, sin, dm, qd, kd, cd = tabs
    s = _ret_specs(nC, lambda h, b, c: (b, h, nC - 1 - c))

    def body(q_ref, k_ref, v_ref, g_ref, o_ref, st_ref, d_ref, cos_ref, sin_ref, dm_ref, qd_ref, kd_ref, cd_ref,
             gn_ref, dq_ref, dk_ref, dv_ref, dg_ref, dgn_ref, dstate):
        b, c = pl.program_id(1), pl.program_id(2)

        @pl.when(c == 0)
        def _():
            dstate[...] = jnp.zeros_like(dstate)

        @pl.when((b == 0) & (c == 0))
        def _():
            dgn_ref[...] = jnp.zeros_like(dgn_ref)

        cs, sn = cos_ref[...], sin_ref[...]
        qf = _rope_halves(q_ref[...].astype(F32), cs, sn)
        kf = _rope_halves(k_ref[...].astype(F32), cs, sn) * (RET_DK ** -0.5)
        v = v_ref[...]
        gnv = gn_ref[...]
        o = o_ref[...]
        r = lax.rsqrt(jnp.mean(o * o, axis=-1, keepdims=True) + RMS_EPS)
        oh = o * r
        gf = g_ref[...].astype(F32)
        sg = jax.nn.sigmoid(gf)
        d = d_ref[...].astype(F32)
        dg_ref[...] = (d * (oh * gnv) * (sg * (1.0 + gf * (1.0 - sg)))).astype(BF16)
        don = d * (gf * sg)
        dgn_ref[...] += jnp.sum(don * oh, axis=0, keepdims=True)
        doh = don * gnv
        dO = (r * (doh - oh * jnp.mean(doh * oh, axis=-1, keepdims=True))).astype(BF16)
        dmv = dm_ref[...]
        qb, kb = qf.astype(BF16), kf.astype(BF16)
        p = (_dot(qb, kb, NT) * dmv).astype(BF16)
        dp = (_dot(dO, v, NT) * dmv).astype(BF16)
        st = st_ref[...]
        dsn = dstate[...]
        dsb = dsn.astype(BF16)
        qdv, kdv = qd_ref[...], kd_ref[...]
        dq = _dot(dp, kb, NN) + _dot(dO, st, NT) * qdv
        dk = _dot(dp, qb, TN) + _dot(v, dsb, NT) * kdv
        dv = _dot(p, dO, TN) + _dot((kf * kdv).astype(BF16), dsb, NN)
        dstate[...] = dsn * cd_ref[...] + _dot((qf * qdv).astype(BF16), dO, TN)
        dq_ref[...] = _unrope_halves(dq, cs, sn).astype(BF16)
        dk_ref[...] = (_unrope_halves(dk, cs, sn) * (RET_DK ** -0.5)).astype(BF16)
        dv_ref[...] = dv.astype(BF16)

    return _pcall(
        body, name="ret_bwd", grid=(RET_H, B, nC),
        in_specs=[s["q"], s["k"], s["v"], s["g"], s["wide"], s["st"], s["wide"], s["cs"], s["cs"], s["dm"], s["dv"],
                  s["dv"], s["cd"], s["gn"]],
        out_specs=[s["narrow"], s["narrow"], s["wide"], s["wide"], s["gn"]],
        out_shape=[jax.ShapeDtypeStruct((T, RET_H * RET_DK), BF16), jax.ShapeDtypeStruct((T, RET_H * RET_DK), BF16),
                   jax.ShapeDtypeStruct((T, RET_H * RET_DV), BF16), jax.ShapeDtypeStruct((T, RET_H * RET_DV), BF16),
                   jax.ShapeDtypeStruct((RET_H, 1, RET_DV), F32)],
        scratch_shapes=[pltpu.VMEM((RET_DK, RET_DV), F32)], sem=("arbitrary", "arbitrary", "arbitrary"),
        args=(proj, proj, proj, proj, o_raw, states, dgt, cos, sin, dm, qd, kd, cd, gn), comm=comm)


MLA_PAD = 256
MLA_R2 = 2 * MLA_ROPE


def _dup(t):
    return jnp.concatenate([t, t], axis=-1)


def _fold(t):
    return t[..., :MLA_ROPE] + t[..., MLA_ROPE:]


def _mla_tables(S):
    half = MLA_ROPE // 2
    inv = ROPE_THETA ** (-jnp.arange(half, dtype=F32) / half)
    ang = jnp.arange(S).astype(F32)[:, None] * inv[None, :]
    cos, sin, zero = jnp.cos(ang), jnp.sin(ang), jnp.zeros((S, MLA_ROPE), F32)
    return jnp.concatenate([cos, cos, zero], axis=-1), jnp.concatenate([-sin, sin, zero], axis=-1)


def _head_norm_rope(n, r2, gn, gr2, cos, sin, scale):
    ssq = jnp.sum(n * n, axis=-1, keepdims=True) + 0.5 * jnp.sum(r2 * r2, axis=-1, keepdims=True)
    rstd = lax.rsqrt(ssq * (1.0 / MLA_QK) + RMS_EPS)
    yn = n * rstd * gn
    yr = r2 * rstd * gr2
    z = yr * cos + pltpu.roll(yr, MLA_ROPE // 2, 1) * sin
    if scale != 1.0:
        yn, z = yn * scale, z * scale
    return yn, z


def _head_norm_rope_bwd(dn, dz, n, r2, gn, gr2, cos, sin, scale):
    ssq = jnp.sum(n * n, axis=-1, keepdims=True) + 0.5 * jnp.sum(r2 * r2, axis=-1, keepdims=True)
    rstd = lax.rsqrt(ssq * (1.0 / MLA_QK) + RMS_EPS)
    hn, hr = n * rstd, r2 * rstd
    if scale != 1.0:
        dn, dz = dn * scale, dz * scale
    dyr = dz * cos + pltpu.roll(dz * sin, MLA_R2 - MLA_ROPE // 2, 1)
    dgn = jnp.sum(dn * hn, axis=0, keepdims=True)
    dgr = jnp.sum(dyr * hr, axis=0, keepdims=True)
    dhn, dhr = dn * gn, dyr * gr2
    mt = (jnp.sum(dhn * hn, axis=-1, keepdims=True) + jnp.sum(dhr * hr, axis=-1, keepdims=True)) * (1.0 / MLA_QK)
    return rstd * (dhn - hn * mt), rstd * (dhr - 0.5 * hr * mt), dgn, dgr


def _diag_bias():
    i = jnp.arange(ATT_TQ)
    return jnp.where((i[None, :] // CHUNK) <= (i[:, None] // CHUNK), 0.0, MASK_VALUE).astype(F32)


def _store_pair(dst, rows, n, r2):
    dst[rows, :MLA_NOPE] = n.astype(BF16)
    dst[rows, MLA_NOPE:] = r2.astype(BF16)


def _mla_fwd(q_raw, kv, kr, gains, tabs, B, S, comm=None):
    T = B * S
    TQ = ATT_TQ
    nQ = S // TQ
    qgn, qgr, kgn, kgr = gains
    cos, sin = tabs
    scale = MLA_QK ** -0.5

    def body(q_ref, kv_ref, kr_ref, qgn_ref, qgr_ref, kgn_ref, kgr_ref, c_ref, s_ref, bias_ref,
             o_ref, lse_ref, qf_s, kf_s, v_s):
        def prep(t, _):
            rows = pl.ds(pl.multiple_of(t * TQ, TQ), TQ)
            cs, sn = c_ref[rows, :], s_ref[rows, :]
            qn, qr = _head_norm_rope(q_ref[rows, :MLA_NOPE], q_ref[rows, MLA_NOPE:], qgn_ref[...], qgr_ref[...],
                                     cs, sn, scale)
            _store_pair(qf_s, rows, qn, qr)
            kn, krr = _head_norm_rope(kv_ref[rows, :MLA_NOPE], kr_ref[rows, :], kgn_ref[...], kgr_ref[...], cs, sn, 1.0)
            _store_pair(kf_s, rows, kn, krr)
            v_s[rows, :] = kv_ref[rows, MLA_NOPE:].astype(BF16)
            return 0

        lax.fori_loop(0, nQ, prep, 0, unroll=2)
        for i in range(nQ):
            rows = slice(i * TQ, (i + 1) * TQ)
            q = qf_s[rows, :]
            sd = _dot(q, kf_s[rows, :], NT) + bias_ref[...]
            m = jnp.max(sd, axis=-1, keepdims=True)
            if i:
                sl = _dot(q, kf_s[:i * TQ, :], NT)
                m = jnp.maximum(m, jnp.max(sl, axis=-1, keepdims=True))
            pd = jnp.exp(sd - m)
            l = jnp.sum(pd, axis=-1, keepdims=True)
            acc = _dot(pd.astype(BF16), v_s[rows, :], NN)
            if i:
                pl_ = jnp.exp(sl - m)
                l = l + jnp.sum(pl_, axis=-1, keepdims=True)
                acc = acc + _dot(pl_.astype(BF16), v_s[:i * TQ, :], NN)
            o_ref[rows, :] = (acc / l).astype(BF16)
            lse_ref[rows, :] = m + jnp.log(l)

    def vec(n):
        return _bs((1, n), lambda b, h: (0, 0))

    def cols(n):
        return _bs((S, n), lambda b, h: (b, h))

    tab = _bs((S, MLA_R2), lambda b, h: (0, 0))
    return _pcall(
        body, name="mla_fwd", grid=(B, MLA_H),
        in_specs=[cols(MLA_PAD), cols(MLA_NOPE + MLA_V), _bs((S, MLA_R2), lambda b, h: (b, 0)),
                  vec(MLA_NOPE), vec(MLA_R2), vec(MLA_NOPE), vec(MLA_R2), tab, tab,
                  _bs((TQ, TQ), lambda b, h: (0, 0))],
        out_specs=[cols(MLA_V), _bs((None, S, 1), lambda b, h: (h, b, 0)), cols(MLA_PAD), cols(MLA_PAD)],
        out_shape=[jax.ShapeDtypeStruct((T, MLA_H * MLA_V), BF16), jax.ShapeDtypeStruct((MLA_H, T, 1), F32),
                   jax.ShapeDtypeStruct((T, MLA_H * MLA_PAD), BF16), jax.ShapeDtypeStruct((T, MLA_H * MLA_PAD), BF16)],
        scratch_shapes=[pltpu.VMEM((S, MLA_V), BF16)],
        sem=("parallel", "parallel"), args=(q_raw, kv, kr, qgn, qgr, kgn, kgr, cos, sin, _diag_bias()), comm=comm)


def _mla_bwd(q_raw, kv, kr, o, lse, do, qf, kf, gains, tabs, B, S, comm=None):
    T = B * S
    TQ = ATT_TQ
    nQ = S // TQ
    qgn, qgr, kgn, kgr = gains
    cos, sin = tabs
    scale = MLA_QK ** -0.5

    def body(q_ref, kv_ref, kr_ref, o_ref, lse_ref, do_ref, qf_s, kf_s, qgn_ref, qgr_ref, kgn_ref, kgr_ref, c_ref, s_ref,
             bias_ref, dq_ref, dkv_ref, dkr_ref, dqgn_ref, dqgr_ref, dkgn_ref, dkgr_ref,
             v_s, dl_s, dq_s, dk_s, dv_s):
        b, h = pl.program_id(0), pl.program_id(1)

        def blk(t):
            return pl.ds(pl.multiple_of(t * TQ, TQ), TQ)

        def prep(t, _):
            rows = blk(t)
            v_s[rows, :] = kv_ref[rows, MLA_NOPE:].astype(BF16)
            dl_s[rows, :] = jnp.sum(do_ref[rows, :].astype(F32) * o_ref[rows, :].astype(F32), axis=-1, keepdims=True)
            dk_s[rows, :] = jnp.zeros((TQ, MLA_PAD), F32)
            dv_s[rows, :] = jnp.zeros((TQ, MLA_V), F32)
            return 0

        lax.fori_loop(0, nQ, prep, 0, unroll=2)

        gqn, gqr = jnp.zeros((1, MLA_NOPE), F32), jnp.zeros((1, MLA_R2), F32)
        for i in range(nQ):
            rows = slice(i * TQ, (i + 1) * TQ)
            q, doi, lse_i, dl_i = qf_s[rows, :], do_ref[rows, :], lse_ref[rows, :], dl_s[rows, :]

            def part(cols, bias):
                k, v = kf_s[cols, :], v_s[cols, :]
                s = _dot(q, k, NT)
                if bias is not None:
                    s = s + bias
                p = jnp.exp(s - lse_i)
                ds = (p * (_dot(doi, v, NT) - dl_i)).astype(BF16)
                dk_s[cols, :] += _dot(ds, q, TN)
                dv_s[cols, :] += _dot(p.astype(BF16), doi, TN)
                return _dot(ds, k, NN)

            dq = part(rows, bias_ref[...])
            if i:
                dq = dq + part(slice(0, i * TQ), None)
            dq_s[...] = dq
            dqn, dqr, a0, a1 = _head_norm_rope_bwd(dq_s[:, :MLA_NOPE], dq_s[:, MLA_NOPE:], q_ref[rows, :MLA_NOPE],
                                                   q_ref[rows, MLA_NOPE:], qgn_ref[...], qgr_ref[...],
                                                   c_ref[rows, :], s_ref[rows, :], scale)
            _store_pair(dq_ref, rows, dqn, dqr)
            gqn, gqr = gqn + a0, gqr + a1

        def post(t, carry):
            rows = blk(t)
            dkn, dkr, a2, a3 = _head_norm_rope_bwd(dk_s[rows, :MLA_NOPE], dk_s[rows, MLA_NOPE:],
                                                   kv_ref[rows, :MLA_NOPE], kr_ref[rows, :], kgn_ref[...], kgr_ref[...],
                                                   c_ref[rows, :], s_ref[rows, :], 1.0)
            dkv_ref[rows, :MLA_NOPE] = dkn.astype(BF16)
            dkv_ref[rows, MLA_NOPE:] = dv_s[rows, :].astype(BF16)

            @pl.when(h == 0)
            def _():
                dkr_ref[rows, :] = dkr

            @pl.when(h > 0)
            def _():
                dkr_ref[rows, :] += dkr

            return carry[0] + a2, carry[1] + a3

        gkn, gkr = lax.fori_loop(0, nQ, post, (jnp.zeros((1, MLA_NOPE), F32), jnp.zeros((1, MLA_R2), F32)), unroll=2)
        first = (b == 0) & (h == 0)

        @pl.when(first)
        def _():
            dqgn_ref[...] = gqn
            dqgr_ref[...] = gqr
            dkgn_ref[...] = gkn
            dkgr_ref[...] = gkr

        @pl.when(jnp.logical_not(first))
        def _():
            dqgn_ref[...] += gqn
            dqgr_ref[...] += gqr
            dkgn_ref[...] += gkn
            dkgr_ref[...] += gkr

    def vec(n):
        return _bs((1, n), lambda b, h: (0, 0))

    def cols(n):
        return _bs((S, n), lambda b, h: (b, h))

    tab = _bs((S, MLA_R2), lambda b, h: (0, 0))
    return _pcall(
        body, name="mla_bwd", grid=(B, MLA_H),
        in_specs=[cols(MLA_PAD), cols(MLA_NOPE + MLA_V), _bs((S, MLA_R2), lambda b, h: (b, 0)), cols(MLA_V),
                  _bs((None, S, 1), lambda b, h: (h, b, 0)), cols(MLA_V), cols(MLA_PAD), cols(MLA_PAD),
                  vec(MLA_NOPE), vec(MLA_R2), vec(MLA_NOPE), vec(MLA_R2), tab, tab,
                  _bs((TQ, TQ), lambda b, h: (0, 0))],
        out_specs=[cols(MLA_PAD), cols(MLA_NOPE + MLA_V), _bs((S, MLA_R2), lambda b, h: (b, 0)),
                   vec(MLA_NOPE), vec(MLA_R2), vec(MLA_NOPE), vec(MLA_R2)],
        out_shape=[jax.ShapeDtypeStruct((T, MLA_H * MLA_PAD), BF16),
                   jax.ShapeDtypeStruct((T, MLA_H * (MLA_NOPE + MLA_V)), BF16),
                   jax.ShapeDtypeStruct((T, MLA_R2), F32), jax.ShapeDtypeStruct((1, MLA_NOPE), F32),
                   jax.ShapeDtypeStruct((1, MLA_R2), F32), jax.ShapeDtypeStruct((1, MLA_NOPE), F32),
                   jax.ShapeDtypeStruct((1, MLA_R2), F32)],
        scratch_shapes=[pltpu.VMEM((S, MLA_V), BF16), pltpu.VMEM((S, 1), F32), pltpu.VMEM((TQ, MLA_PAD), F32),
                        pltpu.VMEM((S, MLA_PAD), F32), pltpu.VMEM((S, MLA_V), F32)],
        sem=("arbitrary", "arbitrary"),
        args=(q_raw, kv, kr, o, lse, do, qf, kf, qgn, qgr, kgn, kgr, cos, sin, _diag_bias()), comm=comm)


def _adamw(name, recvs, w, m, v, tr=None, comm=None):
    n, R, C = recvs[0].shape
    L = len(recvs)
    Lw, Rw, _ = w.shape
    assert Lw * Rw == L * R and w.shape[2] == C
    tr = R if tr is None else tr
    assert R % tr == 0 and Rw % tr == 0
    per = R // tr
    per_w = Rw // tr
    c1 = 1.0 - ADAM_B1 ** ADAM_STEP
    c2 = 1.0 - ADAM_B2 ** ADAM_STEP

    def body(*refs):
        r_refs = refs[:L]
        w_ref, m_ref, v_ref, g_ref, d_ref, nm_ref, nv_ref = refs[L:]
        layer = pl.program_id(0) // per

        def total(r_ref):
            t = r_ref[0].astype(F32)
            for k in range(1, n):
                t = t + r_ref[k].astype(F32)
            return t

        g = total(r_refs[0])
        for l in range(1, L):
            g = jnp.where(layer == l, total(r_refs[l]), g)
        mm = ADAM_B1 * m_ref[...] + (1.0 - ADAM_B1) * g
        vv = ADAM_B2 * v_ref[...] + (1.0 - ADAM_B2) * (g * g)
        g_ref[...] = g
        nm_ref[...] = mm
        nv_ref[...] = vv
        d_ref[...] = -ADAM_LR * ((mm / c1) / (jnp.sqrt(vv / c2) + ADAM_EPS) + ADAM_WD * w_ref[...])

    blk = _bs((None, tr, C), lambda i: (i // per_w, i % per_w, 0))
    r_specs = [_bs((n, tr, C), functools.partial(lambda l, i: (0, jnp.clip(i - l * per, 0, per - 1), 0), l))
               for l in range(L)]
    outs, got = _pcall(body, name=name, grid=(L * per,), in_specs=r_specs + [blk, blk, blk], out_specs=[blk] * 4,
                       out_shape=[jax.ShapeDtypeStruct(w.shape, F32)] * 4, scratch_shapes=[], sem=("arbitrary",),
                       args=(*recvs, w, m, v), comm=comm)
    return outs if comm is None else (outs, got)


def _sum8(name, a):
    n, R, C = a.shape

    def body(a_ref, o_ref):
        s = a_ref[0]
        for k in range(1, n):
            s = s + a_ref[k]
        o_ref[...] = s

    return pl.pallas_call(body, name=name, out_shape=jax.ShapeDtypeStruct((R, C), a.dtype))(a)


def _sds(shape, dt):
    return jax.ShapeDtypeStruct(shape, dt)


def _norm_proj(name, x, g, w, o_spec, out_shape, tm=1024, comm=None):
    T, K = x.shape
    J, _, n = w.shape

    def body(x_ref, g_ref, w_ref, o_ref, h_ref, hs):
        @pl.when(pl.program_id(1) == 0)
        def _():
            xf = x_ref[...]
            r = lax.rsqrt(jnp.mean(xf * xf, axis=-1, keepdims=True) + RMS_EPS)
            h = (xf * r * g_ref[...]).astype(BF16)
            hs[...] = h
            h_ref[...] = h

        o_ref[...] = _dot(hs[...], w_ref[...], NN).astype(o_ref.dtype)

    row = _bs((tm, K), lambda m, j: (m, 0))
    (out, h), got = _pcall(
        body, name=name, grid=(T // tm, J),
        in_specs=[row, _bs((1, K), lambda m, j: (0, 0)), _bs((None, K, n), lambda m, j: (j, 0, 0))],
        out_specs=[o_spec, row], out_shape=[out_shape, _sds((T, K), BF16)], scratch_shapes=[pltpu.VMEM((tm, K), BF16)],
        sem=("parallel", "arbitrary"), args=(x, g, w), comm=comm)
    return out, h, got


def _proj_shared_dx(name, d, w, tm=1024, comm=None):
    J, T, n = d.shape
    K = w.shape[1]
    return _mm(name, d, w, grid=(T // tm, J), a_spec=_bs((None, tm, n), lambda m, k: (k, m, 0)),
               b_spec=_bs((None, K, n), lambda m, k: (k, 0, 0)), o_spec=_bs((tm, K), lambda m, k: (m, 0)),
               out_shape=_sds((T, K), F32), dims=NT, kax=1, acc_shape=(tm, K), comm=comm)


def _proj_shared_dw(name, h, d, tt=1024, comm=None):
    T, K = h.shape
    J, _, n = d.shape
    return _mm(name, h, d, grid=(J, T // tt), a_spec=_bs((tt, K), lambda j, t: (t, 0)),
               b_spec=_bs((None, tt, n), lambda j, t: (j, t, 0)), o_spec=_bs((None, K, n), lambda j, t: (j, 0, 0)),
               out_shape=_sds((J, K, n), BF16), dims=TN, kax=1, acc_shape=(K, n), comm=comm)


def _out_proj(name, a, w, res, tm=512):
    J, T, k = a.shape
    N = w.shape[2]
    return _mm(name, a, w, grid=(T // tm,), a_spec=_bs((J, tm, k), lambda m: (0, m, 0)),
               b_spec=_bs((J, k, N), lambda m: (0, 0, 0)), o_spec=_bs((tm, N), lambda m: (m, 0)),
               out_shape=_sds((T, N), F32), dims=NN, res=res, res_spec=_bs((tm, N), lambda m: (m, 0)), jb=J)


def _out_proj_dx(name, dx, w, tm=1024, comm=None):
    T, N = dx.shape
    J, k, _ = w.shape
    return _mm(name, dx, w, grid=(T // tm, J), a_spec=_bs((tm, N), lambda m, j: (m, 0)),
               b_spec=_bs((None, k, N), lambda m, j: (j, 0, 0)), o_spec=_bs((None, tm, k), lambda m, j: (j, m, 0)),
               out_shape=_sds((J, T, k), BF16), dims=NT, comm=comm)


def _out_proj_dw(name, a, dx, tt=1024, comm=None):
    J, T, k = a.shape
    N = dx.shape[1]
    return _mm(name, a, dx, grid=(J, T // tt), a_spec=_bs((None, tt, k), lambda j, t: (j, t, 0)),
               b_spec=_bs((tt, N), lambda j, t: (t, 0)), o_spec=_bs((None, k, N), lambda j, t: (j, 0, 0)),
               out_shape=_sds((J, k, N), BF16), dims=TN, kax=1, acc_shape=(k, N), comm=comm)


def _dense(name, a, b, dims, out_dtype, tm=512, res=None, comm=None):
    if dims == TN:
        T, K = a.shape
        N = b.shape[1]
        return _mm(name, a, b, grid=(T // tm,), a_spec=_bs((tm, K), lambda t: (t, 0)),
                   b_spec=_bs((tm, N), lambda t: (t, 0)), o_spec=_bs((K, N), lambda t: (0, 0)),
                   out_shape=_sds((K, N), out_dtype), dims=TN, kax=0, acc_shape=(K, N), comm=comm)
    M, K = a.shape
    N = b.shape[1] if dims == NN else b.shape[0]
    row = _bs((tm, N), lambda m: (m, 0))
    return _mm(name, a, b, grid=(M // tm,), a_spec=_bs((tm, K), lambda m: (m, 0)), b_spec=_bs(b.shape, lambda m: (0, 0)),
               o_spec=row, out_shape=_sds((M, N), out_dtype), dims=dims, res=res,
               res_spec=row if res is not None else None, comm=comm)


def _bf16(x):
    return x.astype(BF16)


def _ffn_fwd(i, x, norm_g, w_in, cw, cb, w_out, B, S, comm_in=None):
    T = x.shape[0]
    u, h, got = _norm_proj(f"ffn{i}_in", x, norm_g, w_in, _bs((None, 1024, FSH), lambda m, j: (j, m, 0)),
                           _sds((NDEV, T, FSH), BF16), comm=comm_in)
    u4 = u.reshape(2, 4, T, FSH)
    gt = _convffn_fwd(f"ffn{i}_gate", u4, cw, cb, B, S)
    y = _out_proj(f"ffn{i}_out", gt, w_out, x)
    return y, (x, h, u4, gt), got


def _ffn_bwd(i, dy, dyb, saved, norm_g, w_in, cw, cb, w_out, B, S):
    x, h, u4, gt = saved
    dgt = _out_proj_dx(f"ffn{i}_out_dx", dyb, w_out)
    dw_out = _out_proj_dw(f"ffn{i}_out_dw", gt, dyb).reshape(NDEV, FSH // 2, D_MODEL)
    du4, dcw, dcb = _convffn_bwd(f"ffn{i}_gate_bwd", u4, cw, cb, dgt, B, S)
    du = du4.reshape(NDEV, du4.shape[2], FSH)
    dh, (r_out,) = _proj_shared_dx(f"ffn{i}_in_dx", du, w_in, comm=_Exchange([dw_out]))
    dw_in = _proj_shared_dw(f"ffn{i}_in_dw", h, du)
    dx, dgn, dxb = _rms_bwd(f"ffn{i}_norm_bwd", x, norm_g, dh, dres=dy, also_bf16=True)
    return dx, dxb, dict(w_in=dw_in, norm=dgn, cw=dcw, cb=dcb), r_out


def kernel(x, ret_norm, ret_w_in, ret_gn, ret_w_out, mla_norm, mla_w_in, mla_q_norm, mla_w_qb, mla_kv_norm, mla_w_kvb, mla_q_head_norm, mla_k_head_norm, mla_w_out, ffn_norm, ffn_w_in, ffn_conv_w, ffn_conv_b, ffn_w_out, loss_target, m_ret_norm, m_ret_w_in, m_ret_gn, m_ret_w_out, m_mla_norm, m_mla_w_in, m_mla_q_norm, m_mla_w_qb, m_mla_kv_norm, m_mla_w_kvb, m_mla_q_head_norm, m_mla_k_head_norm, m_mla_w_out, m_ffn_norm, m_ffn_w_in, m_ffn_conv_w, m_ffn_conv_b, m_ffn_w_out, v_ret_norm, v_ret_w_in, v_ret_gn, v_ret_w_out, v_mla_norm, v_mla_w_in, v_mla_q_norm, v_mla_w_qb, v_mla_kv_norm, v_mla_w_kvb, v_mla_q_head_norm, v_mla_k_head_norm, v_mla_w_out, v_ffn_norm, v_ffn_w_in, v_ffn_conv_w, v_ffn_conv_b, v_ffn_w_out):
    B, S, D = x.shape
    T = B * S
    w = dict(ret_norm=ret_norm, ret_w_in=ret_w_in, ret_gn=ret_gn, ret_w_out=ret_w_out, mla_norm=mla_norm,
             mla_w_in=mla_w_in, mla_q_norm=mla_q_norm, mla_w_qb=mla_w_qb, mla_kv_norm=mla_kv_norm, mla_w_kvb=mla_w_kvb,
             mla_q_head_norm=mla_q_head_norm, mla_k_head_norm=mla_k_head_norm, mla_w_out=mla_w_out, ffn_norm=ffn_norm,
             ffn_w_in=ffn_w_in, ffn_conv_w=ffn_conv_w, ffn_conv_b=ffn_conv_b, ffn_w_out=ffn_w_out)
    mom = dict(ret_norm=m_ret_norm, ret_w_in=m_ret_w_in, ret_gn=m_ret_gn, ret_w_out=m_ret_w_out, mla_norm=m_mla_norm,
               mla_w_in=m_mla_w_in, mla_q_norm=m_mla_q_norm, mla_w_qb=m_mla_w_qb, mla_kv_norm=m_mla_kv_norm,
               mla_w_kvb=m_mla_w_kvb, mla_q_head_norm=m_mla_q_head_norm, mla_k_head_norm=m_mla_k_head_norm,
               mla_w_out=m_mla_w_out, ffn_norm=m_ffn_norm, ffn_w_in=m_ffn_w_in, ffn_conv_w=m_ffn_conv_w,
               ffn_conv_b=m_ffn_conv_b, ffn_w_out=m_ffn_w_out)
    var = dict(ret_norm=v_ret_norm, ret_w_in=v_ret_w_in, ret_gn=v_ret_gn, ret_w_out=v_ret_w_out, mla_norm=v_mla_norm,
               mla_w_in=v_mla_w_in, mla_q_norm=v_mla_q_norm, mla_w_qb=v_mla_w_qb, mla_kv_norm=v_mla_kv_norm,
               mla_w_kvb=v_mla_w_kvb, mla_q_head_norm=v_mla_q_head_norm, mla_k_head_norm=v_mla_k_head_norm,
               mla_w_out=v_mla_w_out, ffn_norm=v_ffn_norm, ffn_w_in=v_ffn_w_in, ffn_conv_w=v_ffn_conv_w,
               ffn_conv_b=v_ffn_conv_b, ffn_w_out=v_ffn_w_out)
    BIG = ["ret_w_in", "ret_w_out", "mla_w_in", "mla_w_qb", "mla_w_kvb", "mla_w_out", "ffn_w_in", "ffn_w_out"]
    REPL = ["ret_norm", "ffn_norm", "mla_q_head_norm", "mla_k_head_norm", "ffn_conv_b"]
    SHARDED_SMALL = ["ffn_conv_w", "ret_gn", "mla_norm", "mla_q_norm", "mla_kv_norm"]
    dev = _idx(_place())

    def blk16(k, i=0):
        return _bf16(w[k][i])

    small_vec = jnp.concatenate([w[k].reshape(-1) for k in SHARDED_SMALL])
    n_small = small_vec.shape[0]
    small_vec = jnp.pad(small_vec, (0, 3072 - n_small)).reshape(24, 128)
    Wret_in, sg = _comm_call("gather_ret_w_in", _Gather([blk16("ret_w_in"), small_vec]))
    sg = sg.reshape(NDEV, 3072)
    o0 = 0
    conv_w_full = sg[:, o0:o0 + 2112].reshape(NDEV, 2, 3, 352).transpose(1, 2, 0, 3).reshape(2, 3, FFN)
    o0 += 2112
    ret_gn_full = sg[:, o0:o0 + 256].reshape(NDEV, RET_H, 64).transpose(1, 0, 2).reshape(RET_H, 1, RET_DV)
    o0 += 256
    mla_norm_full = sg[:, o0:o0 + 128].reshape(1, D)
    o0 += 128
    q_norm_full = sg[:, o0:o0 + 48].reshape(1, MLA_QR)
    o0 += 48
    kv_norm_full = sg[:, o0:o0 + 32].reshape(1, MLA_KVR)

    cw = [conv_w_full[i].reshape(3, 4, FSH).transpose(1, 0, 2) for i in range(2)]
    cb = [ffn_conv_b[i].reshape(4, 1, FSH) for i in range(2)]
    fnorm = [ffn_norm[i].reshape(1, D) for i in range(2)]
    rtabs = _ret_tables(S)
    mtabs = _mla_tables(S)
    qh, kh = mla_q_head_norm.reshape(1, MLA_QK), mla_k_head_norm.reshape(1, MLA_QK)
    gains = (qh[:, :MLA_NOPE], _dup(qh[:, MLA_NOPE:]), kh[:, :MLA_NOPE], _dup(kh[:, MLA_NOPE:]))

    x0 = x.reshape(T, D)
    tgt = loss_target.reshape(T, D)
    proj, h0, (Wret_out, Wffn_out0) = _norm_proj(
        "ret_in", x0, ret_norm.reshape(1, D), Wret_in, _bs((1024, 768), lambda m, j: (m, j)), _sds((T, 6144), BF16),
        comm=_Gather([blk16("ret_w_out"), blk16("ffn_w_out", 0)]))
    Wret_out = Wret_out.reshape(RET_H * RET_DV, D)
    Wffn_out0 = Wffn_out0.reshape(4, FSH, D)
    (o_raw, rgt, states), (Wffn_in0,) = _ret_fwd(proj, rtabs, ret_gn_full, B, S, comm=_Gather([blk16("ffn_w_in", 0)]))
    x1 = _dense("ret_out", rgt, Wret_out, NN, F32, res=x0)
    MLA_W = ["mla_w_in", "mla_w_qb", "mla_w_kvb", "mla_w_out"]
    x2, ffn0_saved, got = _ffn_fwd(0, x1, fnorm[0], Wffn_in0, cw[0], cb[0], Wffn_out0, B, S,
                                   comm_in=_Gather([blk16(k) for k in MLA_W]))
    Wmla_in = got[0].reshape(D, MLA_QR + MLA_KVR + MLA_ROPE)
    Wq, Wkv, Wkr = Wmla_in[:, :MLA_QR], Wmla_in[:, MLA_QR:MLA_QR + MLA_KVR], Wmla_in[:, MLA_QR + MLA_KVR:]
    Wqb, Wkvb, Wmla_out = got[1:]

    h2 = _rms_fwd("mla_norm", x2, mla_norm_full)

    c_q, c_kv, k_rope = (_dense(n, h2, wm, NN, F32) for n, wm in
                         (("mla_in_q", Wq), ("mla_in_kv", Wkv), ("mla_in_kr", _dup(Wkr))))
    cqn = _rms_fwd("mla_q_norm", c_q, q_norm_full)
    ckvn = _rms_fwd("mla_kv_norm", c_kv, kv_norm_full)
    Wqb2 = jnp.concatenate([Wqb, Wqb[:, :, MLA_NOPE:]], axis=2).transpose(1, 0, 2).reshape(MLA_QR, MLA_H * MLA_PAD)
    Wkvb2 = Wkvb.transpose(1, 0, 2).reshape(MLA_KVR, MLA_H * (MLA_NOPE + MLA_V))
    Wmla_out2 = Wmla_out.reshape(D, D)
    q_raw = _dense("mla_qb", cqn, Wqb2, NN, F32, tm=1024)
    kvh = _dense("mla_kvb", ckvn, Wkvb2, NN, F32, tm=1024)
    (att, lse, qf, kf), (Wffn_in1, Wffn_out1) = _mla_fwd(
        q_raw, kvh, k_rope, gains, mtabs, B, S, comm=_Gather([blk16("ffn_w_in", 1), blk16("ffn_w_out", 1)]))
    Wffn_out1 = Wffn_out1.reshape(4, FSH, D)
    x3 = _dense("mla_out", att, Wmla_out2, NN, F32, res=x2)
    y, ffn1_saved, _ = _ffn_fwd(1, x3, fnorm[1], Wffn_in1, cw[1], cb[1], Wffn_out1, B, S)

    dy, colsq, dyb = _loss(y, tgt)
    loss = lax.psum(0.5 * jnp.sum(colsq) / D, ("x", "y", "c"))

    dx3, dx3b, gf1, r_ffn1_out = _ffn_bwd(1, dy, dyb, ffn1_saved, fnorm[1], Wffn_in1, cw[1], cb[1], Wffn_out1, B, S)
    datt = _dense("mla_out_dx", dx3b, Wmla_out2, NT, BF16)
    (dq_raw, dkvh, dkr, dqgn, dqgr, dkgn, dkgr), (r_ffn1_in,) = _mla_bwd(
        q_raw, kvh, k_rope, att, lse, datt, qf, kf, gains, mtabs, B, S, comm=_Exchange([gf1["w_in"]]))
    dcqn = _dense("mla_qb_dx", dq_raw, Wqb2, NT, F32, tm=1024)
    dckvn = _dense("mla_kvb_dx", dkvh, Wkvb2, NT, F32, tm=1024)
    dcq, dg_qn = _rms_bwd("mla_q_norm_bwd", c_q, q_norm_full, dcqn)
    dckv, dg_kvn = _rms_bwd("mla_kv_norm_bwd", c_kv, kv_norm_full, dckvn)
    dqgr, dkgr = _fold(dqgr), _fold(dkgr)
    dproj2 = _bf16(jnp.concatenate([dcq, dckv, _fold(dkr)], axis=-1))
    dh2 = _dense("mla_in_dx", dproj2, Wmla_in, NT, F32)
    dx2, dg_mla_norm, dx2b = _rms_bwd("mla_norm_bwd", x2, mla_norm_full, dh2, dres=dx3, also_bf16=True)

    dx1, dx1b, gf0, r_ffn0_out = _ffn_bwd(0, dx2, dx2b, ffn0_saved, fnorm[0], Wffn_in0, cw[0], cb[0], Wffn_out0, B, S)
    drgt = _dense("ret_out_dx", dx1b, Wret_out, NT, BF16)
    dWret_out = _dense("ret_out_dw", rgt, dx1b, TN, BF16, tm=1024).reshape(NDEV, 256, D)
    (dq, dk, dv, dg, dgn_ret), (r_ffn0_in,) = _ret_bwd(proj, o_raw, states, drgt, rtabs, ret_gn_full, B, S,
                                                       comm=_Exchange([gf0["w_in"]]))
    dproj = jnp.concatenate([dq, dk, dv, dg], axis=-1)
    dWret_in, (r_ret_out,) = _mm(
        "ret_in_dw", h0, dproj, grid=(NDEV, T // 1024), a_spec=_bs((1024, D), lambda j, t: (t, 0)),
        b_spec=_bs((1024, 768), lambda j, t: (t, j)), o_spec=_bs((None, D, 768), lambda j, t: (j, 0, 0)),
        out_shape=_sds((NDEV, D, 768), BF16), dims=TN, kax=1, acc_shape=(D, 768), comm=_Exchange([dWret_out]))
    half = D // 2
    dh0, (r_ret_in_a,) = _mm(
        "ret_in_dx", dproj, Wret_in, grid=(T // 1024, NDEV), a_spec=_bs((1024, 768), lambda m, k: (m, k)),
        b_spec=_bs((None, D, 768), lambda m, k: (k, 0, 0)), o_spec=_bs((1024, D), lambda m, k: (m, 0)),
        out_shape=_sds((T, D), F32), dims=NT, kax=1, acc_shape=(1024, D),
        comm=_Exchange([dWret_in], rows=[(0, half)]))
    dx0, dg_ret_norm = _rms_bwd("ret_norm_bwd", x0, ret_norm.reshape(1, D), dh0, dres=dx1)
    grad_x = dx0.reshape(B, S, D)
    dWmla_out, (r_ret_in_b,) = _dense("mla_out_dw", att, dx3b, TN, BF16, tm=1024,
                                      comm=_Exchange([dWret_in], rows=[(half, half)]))
    dWmla_out = dWmla_out.reshape(NDEV, MLA_V, D)
    dWqb, (r_mla_out,) = _dense("mla_qb_dw", cqn, dq_raw, TN, BF16, tm=1024, comm=_Exchange([dWmla_out]))
    dWqb = dWqb.reshape(MLA_QR, MLA_H, MLA_PAD)
    dWqb = jnp.concatenate([dWqb[:, :, :MLA_NOPE], _fold(dWqb[:, :, MLA_NOPE:])], axis=2).transpose(1, 0, 2)
    dWkvb, (r_mla_qb,) = _dense("mla_kvb_dw", ckvn, dkvh, TN, BF16, tm=1024, comm=_Exchange([dWqb]))
    dWkvb = dWkvb.reshape(MLA_KVR, MLA_H, MLA_NOPE + MLA_V).transpose(1, 0, 2)
    dWmla_in, (r_mla_kvb,) = _dense("mla_in_dw", h2, dproj2, TN, BF16, comm=_Exchange([dWkvb]))
    dWmla_in = dWmla_in.reshape(NDEV, 128, 704)
    received = dict(ret_w_in=[r_ret_in_a, r_ret_in_b], ret_w_out=[r_ret_out], mla_w_qb=[r_mla_qb],
                    mla_w_kvb=[r_mla_kvb], mla_w_out=[r_mla_out], ffn_w_in=[r_ffn0_in, r_ffn1_in],
                    ffn_w_out=[r_ffn0_out, r_ffn1_out])

    dconv_w = jnp.stack([g_["cw"].transpose(1, 0, 2).reshape(3, FFN) for g_ in (gf0, gf1)])
    dconv_b = jnp.stack([g_["cb"].reshape(FFN) for g_ in (gf0, gf1)])
    small_parts = [dg_ret_norm, gf0["norm"], gf1["norm"], dg_mla_norm, dg_qn, dg_kvn, dqgn, dqgr, dkgn, dkgr, dgn_ret,
                   dconv_w, dconv_b]
    small_g = jnp.concatenate([p.reshape(-1) for p in small_parts]).reshape(232, 128)
    small_all = _comm_call("gather_small_grads", _Gather([small_g]))[0]
    sred = _sum8("sum_small_grads", small_all).reshape(-1)

    def take(n):
        nonlocal off
        out = sred[off:off + n]
        off += n
        return out

    off = 0
    g_small = dict(ret_norm=take(D).reshape(1, D), ffn_norm=take(2 * D).reshape(2, D), mla_norm=take(D),
                   mla_q_norm=take(MLA_QR), mla_kv_norm=take(MLA_KVR))
    g_small["mla_q_head_norm"] = take(MLA_QK).reshape(1, MLA_QK)
    g_small["mla_k_head_norm"] = take(MLA_QK).reshape(1, MLA_QK)
    g_small["ret_gn"] = take(RET_H * RET_DV).reshape(1, RET_H, RET_DV)
    g_small["ffn_conv_w"] = take(2 * 3 * FFN).reshape(2, 3, FFN)
    g_small["ffn_conv_b"] = take(2 * FFN).reshape(2, FFN)
    g_small["mla_norm"] = lax.dynamic_slice(g_small["mla_norm"], (dev * 128,), (128,)).reshape(1, 128)
    g_small["mla_q_norm"] = lax.dynamic_slice(g_small["mla_q_norm"], (dev * 48,), (48,)).reshape(1, 48)
    g_small["mla_kv_norm"] = lax.dynamic_slice(g_small["mla_kv_norm"], (dev * 32,), (32,)).reshape(1, 32)
    g_small["ret_gn"] = lax.dynamic_slice(g_small["ret_gn"], (0, 0, dev * 64), (1, RET_H, 64))
    g_small["ffn_conv_w"] = lax.dynamic_slice(g_small["ffn_conv_w"], (0, 0, dev * 352), (2, 3, 352))

    grads, delta, new_m, new_v = {}, {}, {}, {}
    for k in ["ffn_w_in"] + [k for k in BIG if k != "ffn_w_in"]:
        rcs = received[k]
        tr = max(t for t in range(16, 257, 16) if rcs[0].shape[1] % t == 0)
        comm = _Exchange([dWmla_in]) if k == "ffn_w_in" else None
        res = _adamw(f"adamw_{k}", rcs, w[k], mom[k], var[k], tr=tr, comm=comm)
        if comm is not None:
            res, received["mla_w_in"] = res
        grads[k], delta[k], new_m[k], new_v[k] = res
    SMALL = REPL + SHARDED_SMALL

    def pack(d):
        vflat = jnp.concatenate([d[k].reshape(-1) for k in SMALL])
        return jnp.pad(vflat, (0, 96 * 128 - vflat.shape[0])).reshape(1, 96, 128)

    ps = _adamw("adamw_small", [pack(g_small)], pack(w), pack(mom), pack(var))
    off = 0
    for k in SMALL:
        n = w[k].size
        grads[k], delta[k], new_m[k], new_v[k] = (t.reshape(-1)[off:off + n].reshape(w[k].shape) for t in ps)
        off += n
    names = list(w)
    return (loss, grad_x, *[grads[k] for k in names], *[delta[k] for k in names], *[new_m[k] for k in names],
            *[new_v[k] for k in names])
```

```python
import functools

import jax
import jax.numpy as jnp
from jax import lax
from jax.experimental import pallas as pl
from jax.experimental.pallas import tpu as pltpu

F32, BF16 = jnp.float32, jnp.bfloat16

NDEV = 8
D_MODEL = 1024
CHUNK = 64
RMS_EPS = 1e-6
ROPE_THETA = 10000.0
RET_H, RET_DK, RET_DV = 4, 256, 512
RET_SC = 256
MLA_H, MLA_QR, MLA_KVR = 8, 384, 256
MLA_NOPE, MLA_ROPE, MLA_V = 128, 64, 128
MLA_QK = MLA_NOPE + MLA_ROPE
MASK_VALUE = -1e30
FFN = 2816
FSH = FFN * 2 // NDEV
ATT_TQ = 256
ADAM_LR, ADAM_B1, ADAM_B2, ADAM_EPS, ADAM_WD, ADAM_STEP = 0.001, 0.9, 0.999, 1e-08, 0.01, 10
MESH = pl.DeviceIdType.MESH
VMEM_LIMIT = 56 * 2 ** 20


def _cp(sem):
    return pltpu.CompilerParams(dimension_semantics=sem, vmem_limit_bytes=VMEM_LIMIT)


def _dot(a, b, dims):
    return lax.dot_general(a, b, (dims, ((), ())), preferred_element_type=F32)


NN = ((1,), (0,))
NT = ((1,), (1,))
TN = ((0,), (0,))


def _place():
    return lax.axis_index("x"), lax.axis_index("y"), lax.axis_index("c")


def _idx(d):
    return 4 * d[0] + 2 * d[1] + d[2]


ANY = pl.BlockSpec(memory_space=pl.ANY)


class _Gather:
    sem0 = 0

    def __init__(self, arrs):
        self.srcs = list(arrs)
        self.out_shape = [jax.ShapeDtypeStruct((NDEV,) + a.shape, a.dtype) for a in arrs]

    def _copies(self, ins, outs, send, recv, loc):
        n = len(self.srcs)
        s0 = self.sem0
        x, y, c = _place()
        me, sib = (x, y, c), (x, y, 1 - c)
        chips = [(1 - x, y), (x, 1 - y), (1 - x, 1 - y)]

        def cp(a, k, block, to, src=None):
            dst = outs[a].at[_idx(block)]
            return pltpu.make_async_remote_copy(src_ref=dst if src is None else src, dst_ref=dst,
                                                send_sem=send.at[s0 + a, k], recv_sem=recv.at[s0 + a, k], device_id=to,
                                                device_id_type=MESH)

        mine = [pltpu.make_async_copy(ins[a], outs[a].at[_idx(me)], loc.at[s0 + a]) for a in range(n)]
        first = [cp(a, 0, me, sib, src=ins[a]) for a in range(n)]
        first += [cp(a, 1 + j, me, (*chip, c), src=ins[a]) for a in range(n) for j, chip in enumerate(chips)]
        landed = [cp(a, 1 + j, (*chip, c), me) for j, chip in enumerate(chips) for a in range(n)]
        passed = [cp(a, 4 + j, (*chip, c), sib) for j, chip in enumerate(chips) for a in range(n)]
        from_sib = [cp(a, 0, sib, me) for a in range(n)]
        from_sib += [cp(a, 4 + j, (*chip, 1 - c), me) for j, chip in enumerate(chips) for a in range(n)]
        return mine, first, landed, passed, from_sib

    def start(self, *refs):
        mine, first, _, _, _ = self._copies(*refs)
        for cp in mine + first:
            cp.start()

    def mid(self, *refs):
        _, _, landed, passed, _ = self._copies(*refs)
        for got, on in zip(landed, passed):
            got.wait_recv()
            on.start()

    def finish(self, *refs):
        mine, first, _, passed, from_sib = self._copies(*refs)
        for cp in from_sib:
            cp.wait_recv()
        for cp in first + passed:
            cp.wait_send()
        for cp in mine:
            cp.wait()


class _Exchange:
    sem0 = 0

    def __init__(self, arrs, rows=None):
        self.srcs = list(arrs)
        self.rows = rows if rows is not None else [None] * len(arrs)
        self.out_shape = [jax.ShapeDtypeStruct(a.shape if r is None else (a.shape[0], r[1]) + a.shape[2:], a.dtype)
                          for a, r in zip(arrs, self.rows)]

    def _copies(self, ins, outs, send, recv, loc):
        n = len(self.srcs)
        s0 = self.sem0
        x, y, c = _place()
        me = _idx((x, y, c))

        def src(a, q):
            r = self.rows[a]
            return ins[a].at[q] if r is None else ins[a].at[q, pl.ds(r[0], r[1])]

        mine = [pltpu.make_async_copy(src(a, me), outs[a].at[me], loc.at[s0 + a]) for a in range(n)]
        remote = []
        for k in range(1, NDEV):
            peer = (x ^ (k >> 2), y ^ ((k >> 1) & 1), c ^ (k & 1))
            remote += [pltpu.make_async_remote_copy(
                src_ref=src(a, _idx(peer)), dst_ref=outs[a].at[me], send_sem=send.at[s0 + a, k - 1],
                recv_sem=recv.at[s0 + a, k - 1], device_id=peer, device_id_type=MESH) for a in range(n)]
        return mine, remote

    def start(self, *refs):
        mine, remote = self._copies(*refs)
        for cp in mine + remote:
            cp.start()

    def mid(self, *refs):
        pass

    def finish(self, *refs):
        mine, remote = self._copies(*refs)
        for cp in remote + mine:
            cp.wait()


class _Both:
    def __init__(self, one, two):
        self.parts = (one, two)
        two.sem0 = len(one.srcs)
        self.srcs = one.srcs + two.srcs
        self.out_shape = one.out_shape + two.out_shape

    def _each(self, phase, ins, outs, send, recv, loc):
        n = len(self.parts[0].srcs)
        getattr(self.parts[0], phase)(ins[:n], outs[:n], send, recv, loc)
        getattr(self.parts[1], phase)(ins[n:], outs[n:], send, recv, loc)

    def start(self, *refs):
        self._each("start", *refs)

    def mid(self, *refs):
        self._each("mid", *refs)

    def finish(self, *refs):
        self._each("finish", *refs)


def _comm_scratch(n):
    return [pltpu.SemaphoreType.DMA((n, 7)), pltpu.SemaphoreType.DMA((n, 7)), pltpu.SemaphoreType.DMA((n,))]


def _comm_call(name, comm):
    n = len(comm.srcs)

    def body(*refs):
        parts = (refs[:n], refs[n:2 * n]) + tuple(refs[2 * n:])
        comm.start(*parts)
        comm.mid(*parts)
        comm.finish(*parts)

    return pl.pallas_call(body, name=name, in_specs=[ANY] * n, out_specs=[ANY] * n, out_shape=comm.out_shape,
                          scratch_shapes=_comm_scratch(n))(*comm.srcs)


def _pcall(body, *, name, grid, in_specs, out_specs, out_shape, scratch_shapes, sem, args, comm=None):
    if comm is None:
        return pl.pallas_call(body, name=name, grid=grid, in_specs=in_specs, out_specs=out_specs, out_shape=out_shape,
                              scratch_shapes=scratch_shapes, compiler_params=_cp(sem))(*args), None
    ni, no, ns, nc = len(in_specs), len(out_shape), len(scratch_shapes), len(comm.srcs)
    total = 1
    for g in grid:
        total *= g
    middle = (3 * total) // 5

    def wrapped(*refs):
        ins, csrc = refs[:ni], refs[ni:ni + nc]
        outs, cdst = refs[ni + nc:ni + nc + no], refs[ni + nc + no:ni + 2 * nc + no]
        scr, sems = refs[ni + 2 * nc + no:ni + 2 * nc + no + ns], refs[ni + 2 * nc + no + ns:]
        step = pl.program_id(0)
        for k in range(1, len(grid)):
            step = step * grid[k] + pl.program_id(k)
        parts = (csrc, cdst) + tuple(sems)

        @pl.when(step == 0)
        def _():
            comm.start(*parts)

        body(*ins, *outs, *scr)

        @pl.when(step == middle)
        def _():
            comm.mid(*parts)

        @pl.when(step == total - 1)
        def _():
            comm.finish(*parts)

    res = pl.pallas_call(
        wrapped, name=name, grid=grid, in_specs=list(in_specs) + [ANY] * nc, out_specs=list(out_specs) + [ANY] * nc,
        out_shape=list(out_shape) + comm.out_shape, scratch_shapes=list(scratch_shapes) + _comm_scratch(nc),
        compiler_params=_cp(("arbitrary",) * len(grid)))(*args, *comm.srcs)
    return res[:no], res[no:]


def _mm(name, a, b, *, grid, a_spec, b_spec, o_spec, out_shape, dims, kax=None, res=None, res_spec=None,
        jb=0, acc_shape=None, comm=None):
    nk = grid[kax] if kax is not None else 1

    def body(*refs):
        if res is not None:
            a_ref, b_ref, r_ref, o_ref = refs[:4]
        else:
            a_ref, b_ref, o_ref = refs[:3]

        def product():
            if not jb:
                return _dot(a_ref[...], b_ref[...], dims)
            part = _dot(a_ref[0], b_ref[0], dims)
            for j in range(1, jb):
                part = part + _dot(a_ref[j], b_ref[j], dims)
            return part

        def fin(acc):
            if res is not None:
                acc = acc + r_ref[...]
            o_ref[...] = acc.astype(o_ref.dtype)

        if nk == 1:
            fin(product())
        else:
            acc_ref = refs[-1]
            k = pl.program_id(kax)

            @pl.when(k == 0)
            def _():
                acc_ref[...] = jnp.zeros_like(acc_ref)

            acc_ref[...] += product()

            @pl.when(k == nk - 1)
            def _():
                fin(acc_ref[...])

    sem = tuple("arbitrary" if i == kax else "parallel" for i in range(len(grid)))
    in_specs = [a_spec, b_spec] + ([res_spec] if res is not None else [])
    args = (a, b) + ((res,) if res is not None else ())
    scratch = [pltpu.VMEM(acc_shape, F32)] if nk > 1 else []
    (out,), got = _pcall(body, name=name, grid=grid, in_specs=in_specs, out_specs=[o_spec], out_shape=[out_shape],
                         scratch_shapes=scratch, sem=sem, args=args, comm=comm)
    return out if comm is None else (out, got)


def _bs(shape, fn):
    return pl.BlockSpec(shape, fn)


def _rms_fwd(name, x, g, tm=512):
    T, D = x.shape

    def body(x_ref, g_ref, o_ref):
        xf = x_ref[...]
        r = lax.rsqrt(jnp.mean(xf * xf, axis=-1, keepdims=True) + RMS_EPS)
        o_ref[...] = (xf * r * g_ref[...]).astype(o_ref.dtype)

    return pl.pallas_call(
        body, name=name, grid=(T // tm,),
        in_specs=[_bs((tm, D), lambda i: (i, 0)), _bs((1, D), lambda i: (0, 0))],
        out_specs=_bs((tm, D), lambda i: (i, 0)), out_shape=jax.ShapeDtypeStruct((T, D), BF16),
        compiler_params=_cp(("parallel",)))(x, g)


def _rms_bwd(name, x, g, dh, dres=None, tm=512, also_bf16=False, comm=None):
    T, D = x.shape

    def body(*refs):
        if also_bf16:
            refs, dxb_ref = refs[:-1], refs[-1]
        if dres is not None:
            x_ref, g_ref, dh_ref, dres_ref, dx_ref, dg_ref = refs
        else:
            x_ref, g_ref, dh_ref, dx_ref, dg_ref = refs
        i = pl.program_id(0)
        xf = x_ref[...]
        r = lax.rsqrt(jnp.mean(xf * xf, axis=-1, keepdims=True) + RMS_EPS)
        xh = xf * r
        d = dh_ref[...].astype(F32)
        dxh = d * g_ref[...]
        dx = r * (dxh - xh * jnp.mean(dxh * xh, axis=-1, keepdims=True))
        if dres is not None:
            dx = dx + dres_ref[...]
        dx_ref[...] = dx
        if also_bf16:
            dxb_ref[...] = dx.astype(BF16)
        part = jnp.sum(d * xh, axis=0, keepdims=True)

        @pl.when(i == 0)
        def _():
            dg_ref[...] = part

        @pl.when(i > 0)
        def _():
            dg_ref[...] += part

    row = _bs((tm, D), lambda i: (i, 0))
    vec = _bs((1, D), lambda i: (0, 0))
    in_specs = [row, vec, row] + ([row] if dres is not None else [])
    args = (x, g, dh) + ((dres,) if dres is not None else ())
    extra = [jax.ShapeDtypeStruct((T, D), BF16)] if also_bf16 else []
    outs, got = _pcall(
        body, name=name, grid=(T // tm,), in_specs=in_specs, out_specs=[row, vec] + [row] * len(extra),
        out_shape=[jax.ShapeDtypeStruct((T, D), F32), jax.ShapeDtypeStruct((1, D), F32)] + extra, scratch_shapes=[],
        sem=("arbitrary",), args=args, comm=comm)
    return outs if comm is None else (outs, got)


def _loss(y, tgt, tm=512):
    T, D = y.shape

    def body(y_ref, t_ref, dy_ref, s_ref, dyb_ref):
        i = pl.program_id(0)
        e = y_ref[...] - t_ref[...]
        dy = e * (1.0 / D)
        dy_ref[...] = dy
        dyb_ref[...] = dy.astype(BF16)
        part = jnp.sum(e * e, axis=0, keepdims=True)

        @pl.when(i == 0)
        def _():
            s_ref[...] = part

        @pl.when(i > 0)
        def _():
            s_ref[...] += part

    row = _bs((tm, D), lambda i: (i, 0))
    return pl.pallas_call(
        body, name="loss_head", grid=(T // tm,), in_specs=[row, row],
        out_specs=[row, _bs((1, D), lambda i: (0, 0)), row],
        out_shape=[jax.ShapeDtypeStruct((T, D), F32), jax.ShapeDtypeStruct((1, D), F32),
                   jax.ShapeDtypeStruct((T, D), BF16)],
        compiler_params=_cp(("arbitrary",)))(y, tgt)


def _shift_rows(t, k, row):
    return jnp.where(row >= k, pltpu.roll(t, k, 0), 0.0)


def _shift_rows_up(t, k, row, n):
    return jnp.where(row < n - k, pltpu.roll(t, n - k, 0), 0.0)


def _convffn_fwd(name, u, cw, cb, B, S):
    _, J, T, F = u.shape

    def body(u_ref, cw_ref, cb_ref, o_ref):
        a = u_ref[0].astype(F32)
        g = u_ref[1].astype(F32)
        row = lax.broadcasted_iota(jnp.int32, (S, F), 0)
        w0, w1, w2 = cw_ref[0:1, :], cw_ref[1:2, :], cw_ref[2:3, :]
        gc = _shift_rows(g, 2, row) * w0 + _shift_rows(g, 1, row) * w1 + g * w2 + cb_ref[...]
        o_ref[...] = (gc * jax.nn.sigmoid(gc) * a).astype(o_ref.dtype)

    return pl.pallas_call(
        body, name=name, grid=(J, B),
        in_specs=[_bs((2, None, S, F), lambda j, b: (0, j, b, 0)), _bs((None, 3, F), lambda j, b: (j, 0, 0)),
                  _bs((None, 1, F), lambda j, b: (j, 0, 0))],
        out_specs=_bs((None, S, F), lambda j, b: (j, b, 0)), out_shape=jax.ShapeDtypeStruct((J, T, F), BF16),
        compiler_params=_cp(("parallel", "parallel")))(u, cw, cb)


def _convffn_bwd(name, u, cw, cb, dgt, B, S, comm=None):
    _, J, T, F = u.shape

    def body(u_ref, cw_ref, cb_ref, d_ref, du_ref, dcw_ref, dcb_ref):
        b = pl.program_id(1)
        a = u_ref[0].astype(F32)
        g = u_ref[1].astype(F32)
        d = d_ref[...].astype(F32)
        row = lax.broadcasted_iota(jnp.int32, (S, F), 0)
        w0, w1, w2 = cw_ref[0:1, :], cw_ref[1:2, :], cw_ref[2:3, :]
        g1, g2 = _shift_rows(g, 1, row), _shift_rows(g, 2, row)
        gc = g2 * w0 + g1 * w1 + g * w2 + cb_ref[...]
        sg = jax.nn.sigmoid(gc)
        du_ref[0] = (d * gc * sg).astype(du_ref.dtype)
        dgc = d * a * (sg * (1.0 + gc * (1.0 - sg)))
        dg = dgc * w2 + _shift_rows_up(dgc, 1, row, S) * w1 + _shift_rows_up(dgc, 2, row, S) * w0
        du_ref[1] = dg.astype(du_ref.dtype)
        parts = [jnp.sum(dgc * g2, axis=0, keepdims=True), jnp.sum(dgc * g1, axis=0, keepdims=True),
                 jnp.sum(dgc * g, axis=0, keepdims=True)]
        pb = jnp.sum(dgc, axis=0, keepdims=True)

        @pl.when(b == 0)
        def _():
            for k in range(3):
                dcw_ref[k:k + 1, :] = parts[k]
            dcb_ref[...] = pb

        @pl.when(b > 0)
        def _():
            for k in range(3):
                dcw_ref[k:k + 1, :] += parts[k]
            dcb_ref[...] += pb

    uspec = _bs((2, None, S, F), lambda j, b: (0, j, b, 0))
    return _pcall(
        body, name=name, grid=(J, B),
        in_specs=[uspec, _bs((None, 3, F), lambda j, b: (j, 0, 0)), _bs((None, 1, F), lambda j, b: (j, 0, 0)),
                  _bs((None, S, F), lambda j, b: (j, b, 0))],
        out_specs=[uspec, _bs((None, 3, F), lambda j, b: (j, 0, 0)), _bs((None, 1, F), lambda j, b: (j, 0, 0))],
        out_shape=[jax.ShapeDtypeStruct(u.shape, BF16), jax.ShapeDtypeStruct((J, 3, F), F32),
                   jax.ShapeDtypeStruct((J, 1, F), F32)],
        scratch_shapes=[], sem=("parallel", "arbitrary"), args=(u, cw, cb, dgt), comm=comm)


def _ret_tables(S):
    half = RET_DK // 2
    inv = ROPE_THETA ** (-jnp.arange(half, dtype=F32) / half)
    ang = jnp.arange(S).astype(F32)[:, None] * inv[None, :]
    lg = jnp.log1p(-jnp.exp2(-5.0 - jnp.arange(RET_H, dtype=F32)))
    i = jnp.arange(RET_SC, dtype=F32)
    same_or_earlier = (jnp.floor(i[None, :] / CHUNK) <= jnp.floor(i[:, None] / CHUNK)).astype(F32)
    dm = jnp.exp(lg[:, None, None] * jnp.abs(i[:, None] - i[None, :])) * same_or_earlier[None]
    qd = jnp.exp(lg[:, None] * (i + 1.0))[:, :, None]
    kd = jnp.exp(lg[:, None] * (RET_SC - 1.0 - i))[:, :, None]
    cd = jnp.exp(lg * RET_SC)[:, None, None]
    return jnp.cos(ang), jnp.sin(ang), dm, qd, kd, cd


def _rope_halves(t, cs, sn):
    h = t.shape[-1] // 2
    t1, t2 = t[:, :h], t[:, h:]
    return jnp.concatenate([t1 * cs - t2 * sn, t2 * cs + t1 * sn], axis=-1)


def _unrope_halves(d, cs, sn):
    h = d.shape[-1] // 2
    d1, d2 = d[:, :h], d[:, h:]
    return jnp.concatenate([d1 * cs + d2 * sn, d2 * cs - d1 * sn], axis=-1)


def _ret_specs(nC, order):
    SC = RET_SC

    def sp(shape, fn):
        return _bs(shape, lambda *g: fn(*order(*g)))

    q = sp((SC, RET_DK), lambda b, h, c: (b * nC + c, h))
    k = sp((SC, RET_DK), lambda b, h, c: (b * nC + c, RET_H + h))
    v = sp((SC, RET_DV), lambda b, h, c: (b * nC + c, RET_H + h))
    g = sp((SC, RET_DV), lambda b, h, c: (b * nC + c, 2 * RET_H + h))
    cs = sp((SC, RET_DK // 2), lambda b, h, c: (c, 0))
    dm = sp((None, SC, SC), lambda b, h, c: (h, 0, 0))
    dv = sp((None, SC, 1), lambda b, h, c: (h, 0, 0))
    cd = sp((None, 1, 1), lambda b, h, c: (h, 0, 0))
    gn = sp((None, 1, RET_DV), lambda b, h, c: (h, 0, 0))
    wide = sp((SC, RET_DV), lambda b, h, c: (b * nC + c, h))
    narrow = sp((SC, RET_DK), lambda b, h, c: (b * nC + c, h))
    st = sp((None, None, None, RET_DK, RET_DV), lambda b, h, c: (b, h, c, 0, 0))
    return dict(q=q, k=k, v=v, g=g, cs=cs, dm=dm, dv=dv, cd=cd, gn=gn, wide=wide, narrow=narrow, st=st)


def _ret_fwd(proj, tabs, gn, B, S, comm=None):
    T = B * S
    nC = S // RET_SC
    cos, sin, dm, qd, kd, cd = tabs
    s = _ret_specs(nC, lambda b, h, c: (b, h, c))

    def body(q_ref, k_ref, v_ref, g_ref, cos_ref, sin_ref, dm_ref, qd_ref, kd_ref, cd_ref, gn_ref,
             o_ref, gt_ref, st_ref, state):
        c = pl.program_id(2)

        @pl.when(c == 0)
        def _():
            state[...] = jnp.zeros_like(state)

        cs, sn = cos_ref[...], sin_ref[...]
        qf = _rope_halves(q_ref[...].astype(F32), cs, sn)
        kf = _rope_halves(k_ref[...].astype(F32), cs, sn) * (RET_DK ** -0.5)
        v = v_ref[...]
        p = _dot(qf.astype(BF16), kf.astype(BF16), NT) * dm_ref[...]
        st = state[...]
        stb = st.astype(BF16)
        st_ref[...] = stb
        o = _dot(p.astype(BF16), v, NN) + _dot((qf * qd_ref[...]).astype(BF16), stb, NN)
        state[...] = st * cd_ref[...] + _dot((kf * kd_ref[...]).astype(BF16), v, TN)
        o_ref[...] = o
        r = lax.rsqrt(jnp.mean(o * o, axis=-1, keepdims=True) + RMS_EPS)
        gf = g_ref[...].astype(F32)
        gt_ref[...] = ((o * r * gn_ref[...]) * (gf * jax.nn.sigmoid(gf))).astype(BF16)

    return _pcall(
        body, name="ret_fwd", grid=(B, RET_H, nC),
        in_specs=[s["q"], s["k"], s["v"], s["g"], s["cs"], s["cs"], s["dm"], s["dv"], s["dv"], s["cd"], s["gn"]],
        out_specs=[s["wide"], s["wide"], s["st"]],
        out_shape=[jax.ShapeDtypeStruct((T, RET_H * RET_DV), F32), jax.ShapeDtypeStruct((T, RET_H * RET_DV), BF16),
                   jax.ShapeDtypeStruct((B, RET_H, nC, RET_DK, RET_DV), BF16)],
        scratch_shapes=[pltpu.VMEM((RET_DK, RET_DV), F32)], sem=("parallel", "parallel", "arbitrary"),
        args=(proj, proj, proj, proj, cos, sin, dm, qd, kd, cd, gn), comm=comm)


def _ret_bwd(proj, o_raw, states, dgt, tabs, gn, B, S, comm=None):
    T = B * S
    nC = S // RET_SC
    cos, sin, dm, qd, kd, cd = tabs
    s = _ret_specs(nC, lambda h, b, c: (b, h, nC - 1 - c))

    def body(q_ref, k_ref, v_ref, g_ref, o_ref, st_ref, d_ref, cos_ref, sin_ref, dm_ref, qd_ref, kd_ref, cd_ref,
             gn_ref, dq_ref, dk_ref, dv_ref, dg_ref, dgn_ref, dstate):
        b, c = pl.program_id(1), pl.program_id(2)

        @pl.when(c == 0)
        def _():
            dstate[...] = jnp.zeros_like(dstate)

        @pl.when((b == 0) & (c == 0))
        def _():
            dgn_ref[...] = jnp.zeros_like(dgn_ref)

        cs, sn = cos_ref[...], sin_ref[...]
        qf = _rope_halves(q_ref[...].astype(F32), cs, sn)
        kf = _rope_halves(k_ref[...].astype(F32), cs, sn) * (RET_DK ** -0.5)
        v = v_ref[...]
        gnv = gn_ref[...]
        o = o_ref[...]
        r = lax.rsqrt(jnp.mean(o * o, axis=-1, keepdims=True) + RMS_EPS)
        oh = o * r
        gf = g_ref[...].astype(F32)
        sg = jax.nn.sigmoid(gf)
        d = d_ref[...].astype(F32)
        dg_ref[...] = (d * (oh * gnv) * (sg * (1.0 + gf * (1.0 - sg)))).astype(BF16)
        don = d * (gf * sg)
        dgn_ref[...] += jnp.sum(don * oh, axis=0, keepdims=True)
        doh = don * gnv
        dO = (r * (doh - oh * jnp.mean(doh * oh, axis=-1, keepdims=True))).astype(BF16)
        dmv = dm_ref[...]
        qb, kb = qf.astype(BF16), kf.astype(BF16)
        p = (_dot(qb, kb, NT) * dmv).astype(BF16)
        dp = (_dot(dO, v, NT) * dmv).astype(BF16)
        st = st_ref[...]
        dsn = dstate[...]
        dsb = dsn.astype(BF16)
        qdv, kdv = qd_ref[...], kd_ref[...]
        dq = _dot(dp, kb, NN) + _dot(dO, st, NT) * qdv
        dk = _dot(dp, qb, TN) + _dot(v, dsb, NT) * kdv
        dv = _dot(p, dO, TN) + _dot((kf * kdv).astype(BF16), dsb, NN)
        dstate[...] = dsn * cd_ref[...] + _dot((qf * qdv).astype(BF16), dO, TN)
        dq_ref[...] = _unrope_halves(dq, cs, sn).astype(BF16)
        dk_ref[...] = (_unrope_halves(dk, cs, sn) * (RET_DK ** -0.5)).astype(BF16)
        dv_ref[...] = dv.astype(BF16)

    return _pcall(
        body, name="ret_bwd", grid=(RET_H, B, nC),
        in_specs=[s["q"], s["k"], s["v"], s["g"], s["wide"], s["st"], s["wide"], s["cs"], s["cs"], s["dm"], s["dv"],
                  s["dv"], s["cd"], s["gn"]],
        out_specs=[s["narrow"], s["narrow"], s["wide"], s["wide"], s["gn"]],
        out_shape=[jax.ShapeDtypeStruct((T, RET_H * RET_DK), BF16), jax.ShapeDtypeStruct((T, RET_H * RET_DK), BF16),
                   jax.ShapeDtypeStruct((T, RET_H * RET_DV), BF16), jax.ShapeDtypeStruct((T, RET_H * RET_DV), BF16),
                   jax.ShapeDtypeStruct((RET_H, 1, RET_DV), F32)],
        scratch_shapes=[pltpu.VMEM((RET_DK, RET_DV), F32)], sem=("arbitrary", "arbitrary", "arbitrary"),
        args=(proj, proj, proj, proj, o_raw, states, dgt, cos, sin, dm, qd, kd, cd, gn), comm=comm)


MLA_PAD = 256
MLA_R2 = 2 * MLA_ROPE


def _dup(t):
    return jnp.concatenate([t, t], axis=-1)


def _fold(t):
    return t[..., :MLA_ROPE] + t[..., MLA_ROPE:]


def _mla_tables(S):
    half = MLA_ROPE // 2
    inv = ROPE_THETA ** (-jnp.arange(half, dtype=F32) / half)
    ang = jnp.arange(S).astype(F32)[:, None] * inv[None, :]
    cos, sin, zero = jnp.cos(ang), jnp.sin(ang), jnp.zeros((S, MLA_ROPE), F32)
    return jnp.concatenate([cos, cos, zero], axis=-1), jnp.concatenate([-sin, sin, zero], axis=-1)


def _head_norm_rope(n, r2, gn, gr2, cos, sin, scale):
    ssq = jnp.sum(n * n, axis=-1, keepdims=True) + 0.5 * jnp.sum(r2 * r2, axis=-1, keepdims=True)
    rstd = lax.rsqrt(ssq * (1.0 / MLA_QK) + RMS_EPS)
    yn = n * rstd * gn
    yr = r2 * rstd * gr2
    z = yr * cos + pltpu.roll(yr, MLA_ROPE // 2, 1) * sin
    if scale != 1.0:
        yn, z = yn * scale, z * scale
    return yn, z


def _head_norm_rope_bwd(dn, dz, n, r2, gn, gr2, cos, sin, scale):
    ssq = jnp.sum(n * n, axis=-1, keepdims=True) + 0.5 * jnp.sum(r2 * r2, axis=-1, keepdims=True)
    rstd = lax.rsqrt(ssq * (1.0 / MLA_QK) + RMS_EPS)
    hn, hr = n * rstd, r2 * rstd
    if scale != 1.0:
        dn, dz = dn * scale, dz * scale
    dyr = dz * cos + pltpu.roll(dz * sin, MLA_R2 - MLA_ROPE // 2, 1)
    dgn = jnp.sum(dn * hn, axis=0, keepdims=True)
    dgr = jnp.sum(dyr * hr, axis=0, keepdims=True)
    dhn, dhr = dn * gn, dyr * gr2
    mt = (jnp.sum(dhn * hn, axis=-1, keepdims=True) + jnp.sum(dhr * hr, axis=-1, keepdims=True)) * (1.0 / MLA_QK)
    return rstd * (dhn - hn * mt), rstd * (dhr - 0.5 * hr * mt), dgn, dgr


def _diag_bias():
    i = jnp.arange(ATT_TQ)
    return jnp.where((i[None, :] // CHUNK) <= (i[:, None] // CHUNK), 0.0, MASK_VALUE).astype(F32)


def _store_pair(dst, rows, n, r2):
    dst[rows, :MLA_NOPE] = n.astype(BF16)
    dst[rows, MLA_NOPE:] = r2.astype(BF16)


def _mla_fwd(q_raw, kv, kr, gains, tabs, B, S, comm=None):
    T = B * S
    TQ = ATT_TQ
    nQ = S // TQ
    qgn, qgr, kgn, kgr = gains
    cos, sin = tabs
    scale = MLA_QK ** -0.5

    def body(q_ref, kv_ref, kr_ref, qgn_ref, qgr_ref, kgn_ref, kgr_ref, c_ref, s_ref, bias_ref,
             o_ref, lse_ref, qf_s, kf_s, v_s):
        def prep(t, _):
            rows = pl.ds(pl.multiple_of(t * TQ, TQ), TQ)
            cs, sn = c_ref[rows, :], s_ref[rows, :]
            qn, qr = _head_norm_rope(q_ref[rows, :MLA_NOPE], q_ref[rows, MLA_NOPE:], qgn_ref[...], qgr_ref[...],
                                     cs, sn, scale)
            _store_pair(qf_s, rows, qn, qr)
            kn, krr = _head_norm_rope(kv_ref[rows, :MLA_NOPE], kr_ref[rows, :], kgn_ref[...], kgr_ref[...], cs, sn, 1.0)
            _store_pair(kf_s, rows, kn, krr)
            v_s[rows, :] = kv_ref[rows, MLA_NOPE:].astype(BF16)
            return 0

        lax.fori_loop(0, nQ, prep, 0, unroll=2)
        for i in range(nQ):
            rows = slice(i * TQ, (i + 1) * TQ)
            q = qf_s[rows, :]
            sd = _dot(q, kf_s[rows, :], NT) + bias_ref[...]
            m = jnp.max(sd, axis=-1, keepdims=True)
            if i:
                sl = _dot(q, kf_s[:i * TQ, :], NT)
                m = jnp.maximum(m, jnp.max(sl, axis=-1, keepdims=True))
            pd = jnp.exp(sd - m)
            l = jnp.sum(pd, axis=-1, keepdims=True)
            acc = _dot(pd.astype(BF16), v_s[rows, :], NN)
            if i:
                pl_ = jnp.exp(sl - m)
                l = l + jnp.sum(pl_, axis=-1, keepdims=True)
                acc = acc + _dot(pl_.astype(BF16), v_s[:i * TQ, :], NN)
            o_ref[rows, :] = (acc / l).astype(BF16)
            lse_ref[rows, :] = m + jnp.log(l)

    def vec(n):
        return _bs((1, n), lambda b, h: (0, 0))

    def cols(n):
        return _bs((S, n), lambda b, h: (b, h))

    tab = _bs((S, MLA_R2), lambda b, h: (0, 0))
    return _pcall(
        body, name="mla_fwd", grid=(B, MLA_H),
        in_specs=[cols(MLA_PAD), cols(MLA_NOPE + MLA_V), _bs((S, MLA_R2), lambda b, h: (b, 0)),
                  vec(MLA_NOPE), vec(MLA_R2), vec(MLA_NOPE), vec(MLA_R2), tab, tab,
                  _bs((TQ, TQ), lambda b, h: (0, 0))],
        out_specs=[cols(MLA_V), _bs((None, S, 1), lambda b, h: (h, b, 0)), cols(MLA_PAD), cols(MLA_PAD)],
        out_shape=[jax.ShapeDtypeStruct((T, MLA_H * MLA_V), BF16), jax.ShapeDtypeStruct((MLA_H, T, 1), F32),
                   jax.ShapeDtypeStruct((T, MLA_H * MLA_PAD), BF16), jax.ShapeDtypeStruct((T, MLA_H * MLA_PAD), BF16)],
        scratch_shapes=[pltpu.VMEM((S, MLA_V), BF16)],
        sem=("parallel", "parallel"), args=(q_raw, kv, kr, qgn, qgr, kgn, kgr, cos, sin, _diag_bias()), comm=comm)


def _mla_bwd(q_raw, kv, kr, o, lse, do, qf, kf, gains, tabs, B, S, comm=None):
    T = B * S
    TQ = ATT_TQ
    nQ = S // TQ
    qgn, qgr, kgn, kgr = gains
    cos, sin = tabs
    scale = MLA_QK ** -0.5

    def body(q_ref, kv_ref, kr_ref, o_ref, lse_ref, do_ref, qf_s, kf_s, qgn_ref, qgr_ref, kgn_ref, kgr_ref, c_ref, s_ref,
             bias_ref, dq_ref, dkv_ref, dkr_ref, dqgn_ref, dqgr_ref, dkgn_ref, dkgr_ref,
             v_s, dl_s, dq_s, dk_s, dv_s):
        b, h = pl.program_id(0), pl.program_id(1)

        def blk(t):
            return pl.ds(pl.multiple_of(t * TQ, TQ), TQ)

        def prep(t, _):
            rows = blk(t)
            v_s[rows, :] = kv_ref[rows, MLA_NOPE:].astype(BF16)
            dl_s[rows, :] = jnp.sum(do_ref[rows, :].astype(F32) * o_ref[rows, :].astype(F32), axis=-1, keepdims=True)
            dk_s[rows, :] = jnp.zeros((TQ, MLA_PAD), F32)
            dv_s[rows, :] = jnp.zeros((TQ, MLA_V), F32)
            return 0

        lax.fori_loop(0, nQ, prep, 0, unroll=2)

        gqn, gqr = jnp.zeros((1, MLA_NOPE), F32), jnp.zeros((1, MLA_R2), F32)
        for i in range(nQ):
            rows = slice(i * TQ, (i + 1) * TQ)
            q, doi, lse_i, dl_i = qf_s[rows, :], do_ref[rows, :], lse_ref[rows, :], dl_s[rows, :]

            def part(cols, bias):
                k, v = kf_s[cols, :], v_s[cols, :]
                s = _dot(q, k, NT)
                if bias is not None:
                    s = s + bias
                p = jnp.exp(s - lse_i)
                ds = (p * (_dot(doi, v, NT) - dl_i)).astype(BF16)
                dk_s[cols, :] += _dot(ds, q, TN)
                dv_s[cols, :] += _dot(p.astype(BF16), doi, TN)
                return _dot(ds, k, NN)

            dq = part(rows, bias_ref[...])
            if i:
                dq = dq + part(slice(0, i * TQ), None)
            dq_s[...] = dq
            dqn, dqr, a0, a1 = _head_norm_rope_bwd(dq_s[:, :MLA_NOPE], dq_s[:, MLA_NOPE:], q_ref[rows, :MLA_NOPE],
                                                   q_ref[rows, MLA_NOPE:], qgn_ref[...], qgr_ref[...],
                                                   c_ref[rows, :], s_ref[rows, :], scale)
            _store_pair(dq_ref, rows, dqn, dqr)
            gqn, gqr = gqn + a0, gqr + a1

        def post(t, carry):
            rows = blk(t)
            dkn, dkr, a2, a3 = _head_norm_rope_bwd(dk_s[rows, :MLA_NOPE], dk_s[rows, MLA_NOPE:],
                                                   kv_ref[rows, :MLA_NOPE], kr_ref[rows, :], kgn_ref[...], kgr_ref[...],
                                                   c_ref[rows, :], s_ref[rows, :], 1.0)
            dkv_ref[rows, :MLA_NOPE] = dkn.astype(BF16)
            dkv_ref[rows, MLA_NOPE:] = dv_s[rows, :].astype(BF16)

            @pl.when(h == 0)
            def _():
                dkr_ref[rows, :] = dkr

            @pl.when(h > 0)
            def _():
                dkr_ref[rows, :] += dkr

            return carry[0] + a2, carry[1] + a3

        gkn, gkr = lax.fori_loop(0, nQ, post, (jnp.zeros((1, MLA_NOPE), F32), jnp.zeros((1, MLA_R2), F32)), unroll=2)
        first = (b == 0) & (h == 0)

        @pl.when(first)
        def _():
            dqgn_ref[...] = gqn
            dqgr_ref[...] = gqr
            dkgn_ref[...] = gkn
            dkgr_ref[...] = gkr

        @pl.when(jnp.logical_not(first))
        def _():
            dqgn_ref[...] += gqn
            dqgr_ref[...] += gqr
            dkgn_ref[...] += gkn
            dkgr_ref[...] += gkr

    def vec(n):
        return _bs((1, n), lambda b, h: (0, 0))

    def cols(n):
        return _bs((S, n), lambda b, h: (b, h))

    tab = _bs((S, MLA_R2), lambda b, h: (0, 0))
    return _pcall(
        body, name="mla_bwd", grid=(B, MLA_H),
        in_specs=[cols(MLA_PAD), cols(MLA_NOPE + MLA_V), _bs((S, MLA_R2), lambda b, h: (b, 0)), cols(MLA_V),
                  _bs((None, S, 1), lambda b, h: (h, b, 0)), cols(MLA_V), cols(MLA_PAD), cols(MLA_PAD),
                  vec(MLA_NOPE), vec(MLA_R2), vec(MLA_NOPE), vec(MLA_R2), tab, tab,
                  _bs((TQ, TQ), lambda b, h: (0, 0))],
        out_specs=[cols(MLA_PAD), cols(MLA_NOPE + MLA_V), _bs((S, MLA_R2), lambda b, h: (b, 0)),
                   vec(MLA_NOPE), vec(MLA_R2), vec(MLA_NOPE), vec(MLA_R2)],
        out_shape=[jax.ShapeDtypeStruct((T, MLA_H * MLA_PAD), BF16),
                   jax.ShapeDtypeStruct((T, MLA_H * (MLA_NOPE + MLA_V)), BF16),
                   jax.ShapeDtypeStruct((T, MLA_R2), F32), jax.ShapeDtypeStruct((1, MLA_NOPE), F32),
                   jax.ShapeDtypeStruct((1, MLA_R2), F32), jax.ShapeDtypeStruct((1, MLA_NOPE), F32),
                   jax.ShapeDtypeStruct((1, MLA_R2), F32)],
        scratch_shapes=[pltpu.VMEM((S, MLA_V), BF16), pltpu.VMEM((S, 1), F32), pltpu.VMEM((TQ, MLA_PAD), F32),
                        pltpu.VMEM((S, MLA_PAD), F32), pltpu.VMEM((S, MLA_V), F32)],
        sem=("arbitrary", "arbitrary"),
        args=(q_raw, kv, kr, o, lse, do, qf, kf, qgn, qgr, kgn, kgr, cos, sin, _diag_bias()), comm=comm)


def _adamw(name, recvs, w, m, v, tr=None, comm=None):
    n, R, C = recvs[0].shape
    L = len(recvs)
    Lw, Rw, _ = w.shape
    assert Lw * Rw == L * R and w.shape[2] == C
    tr = R if tr is None else tr
    assert R % tr == 0 and Rw % tr == 0
    per = R // tr
    per_w = Rw // tr
    c1 = 1.0 - ADAM_B1 ** ADAM_STEP
    c2 = 1.0 - ADAM_B2 ** ADAM_STEP

    def body(*refs):
        r_refs = refs[:L]
        w_ref, m_ref, v_ref, g_ref, d_ref, nm_ref, nv_ref = refs[L:]
        layer = pl.program_id(0) // per

        def total(r_ref):
            t = r_ref[0].astype(F32)
            for k in range(1, n):
                t = t + r_ref[k].astype(F32)
            return t

        g = total(r_refs[0])
        for l in range(1, L):
            g = jnp.where(layer == l, total(r_refs[l]), g)
        mm = ADAM_B1 * m_ref[...] + (1.0 - ADAM_B1) * g
        vv = ADAM_B2 * v_ref[...] + (1.0 - ADAM_B2) * (g * g)
        g_ref[...] = g
        nm_ref[...] = mm
        nv_ref[...] = vv
        d_ref[...] = -ADAM_LR * ((mm / c1) / (jnp.sqrt(vv / c2) + ADAM_EPS) + ADAM_WD * w_ref[...])

    blk = _bs((None, tr, C), lambda i: (i // per_w, i % per_w, 0))
    r_specs = [_bs((n, tr, C), functools.partial(lambda l, i: (0, jnp.clip(i - l * per, 0, per - 1), 0), l))
               for l in range(L)]
    outs, got = _pcall(body, name=name, grid=(L * per,), in_specs=r_specs + [blk, blk, blk], out_specs=[blk] * 4,
                       out_shape=[jax.ShapeDtypeStruct(w.shape, F32)] * 4, scratch_shapes=[], sem=("arbitrary",),
                       args=(*recvs, w, m, v), comm=comm)
    return outs if comm is None else (outs, got)


def _sum8(name, a):
    n, R, C = a.shape

    def body(a_ref, o_ref):
        s = a_ref[0]
        for k in range(1, n):
            s = s + a_ref[k]
        o_ref[...] = s

    return pl.pallas_call(body, name=name, out_shape=jax.ShapeDtypeStruct((R, C), a.dtype))(a)


def _sds(shape, dt):
    return jax.ShapeDtypeStruct(shape, dt)


def _norm_proj(name, x, g, w, o_spec, out_shape, tm=1024, comm=None):
    T, K = x.shape
    J, _, n = w.shape

    def body(x_ref, g_ref, w_ref, o_ref, h_ref, hs):
        @pl.when(pl.program_id(1) == 0)
        def _():
            xf = x_ref[...]
            r = lax.rsqrt(jnp.mean(xf * xf, axis=-1, keepdims=True) + RMS_EPS)
            h = (xf * r * g_ref[...]).astype(BF16)
            hs[...] = h
            h_ref[...] = h

        o_ref[...] = _dot(hs[...], w_ref[...], NN).astype(o_ref.dtype)

    row = _bs((tm, K), lambda m, j: (m, 0))
    (out, h), got = _pcall(
        body, name=name, grid=(T // tm, J),
        in_specs=[row, _bs((1, K), lambda m, j: (0, 0)), _bs((None, K, n), lambda m, j: (j, 0, 0))],
        out_specs=[o_spec, row], out_shape=[out_shape, _sds((T, K), BF16)], scratch_shapes=[pltpu.VMEM((tm, K), BF16)],
        sem=("parallel", "arbitrary"), args=(x, g, w), comm=comm)
    return out, h, got


def _proj_shared_dx(name, d, w, tm=1024, comm=None):
    J, T, n = d.shape
    K = w.shape[1]
    return _mm(name, d, w, grid=(T // tm, J), a_spec=_bs((None, tm, n), lambda m, k: (k, m, 0)),
               b_spec=_bs((None, K, n), lambda m, k: (k, 0, 0)), o_spec=_bs((tm, K), lambda m, k: (m, 0)),
               out_shape=_sds((T, K), F32), dims=NT, kax=1, acc_shape=(tm, K), comm=comm)


def _out_proj(name, a, w, res, tm=512):
    J, T, k = a.shape
    N = w.shape[2]
    return _mm(name, a, w, grid=(T // tm,), a_spec=_bs((J, tm, k), lambda m: (0, m, 0)),
               b_spec=_bs((J, k, N), lambda m: (0, 0, 0)), o_spec=_bs((tm, N), lambda m: (m, 0)),
               out_shape=_sds((T, N), F32), dims=NN, res=res, res_spec=_bs((tm, N), lambda m: (m, 0)), jb=J)


def _out_proj_dx(name, dx, w, tm=1024, comm=None):
    T, N = dx.shape
    J, k, _ = w.shape
    return _mm(name, dx, w, grid=(T // tm, J), a_spec=_bs((tm, N), lambda m, j: (m, 0)),
               b_spec=_bs((None, k, N), lambda m, j: (j, 0, 0)), o_spec=_bs((None, tm, k), lambda m, j: (j, m, 0)),
               out_shape=_sds((J, T, k), BF16), dims=NT, comm=comm)


def _out_proj_dw(name, a, dx, tt=1024, comm=None):
    J, T, k = a.shape
    N = dx.shape[1]
    return _mm(name, a, dx, grid=(J, T // tt), a_spec=_bs((None, tt, k), lambda j, t: (j, t, 0)),
               b_spec=_bs((tt, N), lambda j, t: (t, 0)), o_spec=_bs((None, k, N), lambda j, t: (j, 0, 0)),
               out_shape=_sds((J, k, N), BF16), dims=TN, kax=1, acc_shape=(k, N), comm=comm)


def _dense(name, a, b, dims, out_dtype, tm=512, res=None, comm=None):
    if dims == TN:
        T, K = a.shape
        N = b.shape[1]
        return _mm(name, a, b, grid=(T // tm,), a_spec=_bs((tm, K), lambda t: (t, 0)),
                   b_spec=_bs((tm, N), lambda t: (t, 0)), o_spec=_bs((K, N), lambda t: (0, 0)),
                   out_shape=_sds((K, N), out_dtype), dims=TN, kax=0, acc_shape=(K, N), comm=comm)
    M, K = a.shape
    N = b.shape[1] if dims == NN else b.shape[0]
    row = _bs((tm, N), lambda m: (m, 0))
    return _mm(name, a, b, grid=(M // tm,), a_spec=_bs((tm, K), lambda m: (m, 0)), b_spec=_bs(b.shape, lambda m: (0, 0)),
               o_spec=row, out_shape=_sds((M, N), out_dtype), dims=dims, res=res,
               res_spec=row if res is not None else None, comm=comm)


def _bf16(x):
    return x.astype(BF16)


def _ffn_fwd(i, x, norm_g, w_in, cw, cb, w_out, B, S, comm_in=None):
    T = x.shape[0]
    u, h, got = _norm_proj(f"ffn{i}_in", x, norm_g, w_in, _bs((None, 1024, FSH), lambda m, j: (j, m, 0)),
                           _sds((NDEV, T, FSH), BF16), comm=comm_in)
    u4 = u.reshape(2, 4, T, FSH)
    gt = _convffn_fwd(f"ffn{i}_gate", u4, cw, cb, B, S)
    y = _out_proj(f"ffn{i}_out", gt, w_out, x)
    return y, (x, h, u4, gt), got


def _ffn_bwd(i, dy, dyb, saved, norm_g, w_in, cw, cb, w_out, B, S, first_half_early):
    x, h, u4, gt = saved
    dgt = _out_proj_dx(f"ffn{i}_out_dx", dyb, w_out)
    dw_out = _out_proj_dw(f"ffn{i}_out_dw", gt, dyb).reshape(NDEV, FSH // 2, D_MODEL)
    (du4, dcw, dcb), (r_out,) = _convffn_bwd(f"ffn{i}_gate_bwd", u4, cw, cb, dgt, B, S, comm=_Exchange([dw_out]))
    du = du4.reshape(NDEV, du4.shape[2], FSH)
    dw_in = _out_proj_dw(f"ffn{i}_in_dw", du, h)
    r_in = None
    if first_half_early:
        dh, (r_in,) = _proj_shared_dx(f"ffn{i}_in_dx", du, w_in, comm=_Exchange([dw_in], rows=[(0, FSH // 2)]))
    else:
        dh = _proj_shared_dx(f"ffn{i}_in_dx", du, w_in)
    dx, dgn, dxb = _rms_bwd(f"ffn{i}_norm_bwd", x, norm_g, dh, dres=dy, also_bf16=True)
    return dx, dxb, dict(w_in=dw_in, norm=dgn, cw=dcw, cb=dcb), r_out, r_in


def kernel(x, ret_norm, ret_w_in, ret_gn, ret_w_out, mla_norm, mla_w_in, mla_q_norm, mla_w_qb, mla_kv_norm, mla_w_kvb, mla_q_head_norm, mla_k_head_norm, mla_w_out, ffn_norm, ffn_w_in, ffn_conv_w, ffn_conv_b, ffn_w_out, loss_target, m_ret_norm, m_ret_w_in, m_ret_gn, m_ret_w_out, m_mla_norm, m_mla_w_in, m_mla_q_norm, m_mla_w_qb, m_mla_kv_norm, m_mla_w_kvb, m_mla_q_head_norm, m_mla_k_head_norm, m_mla_w_out, m_ffn_norm, m_ffn_w_in, m_ffn_conv_w, m_ffn_conv_b, m_ffn_w_out, v_ret_norm, v_ret_w_in, v_ret_gn, v_ret_w_out, v_mla_norm, v_mla_w_in, v_mla_q_norm, v_mla_w_qb, v_mla_kv_norm, v_mla_w_kvb, v_mla_q_head_norm, v_mla_k_head_norm, v_mla_w_out, v_ffn_norm, v_ffn_w_in, v_ffn_conv_w, v_ffn_conv_b, v_ffn_w_out):
    B, S, D = x.shape
    T = B * S
    w = dict(ret_norm=ret_norm, ret_w_in=ret_w_in, ret_gn=ret_gn, ret_w_out=ret_w_out, mla_norm=mla_norm,
             mla_w_in=mla_w_in, mla_q_norm=mla_q_norm, mla_w_qb=mla_w_qb, mla_kv_norm=mla_kv_norm, mla_w_kvb=mla_w_kvb,
             mla_q_head_norm=mla_q_head_norm, mla_k_head_norm=mla_k_head_norm, mla_w_out=mla_w_out, ffn_norm=ffn_norm,
             ffn_w_in=ffn_w_in, ffn_conv_w=ffn_conv_w, ffn_conv_b=ffn_conv_b, ffn_w_out=ffn_w_out)
    mom = dict(ret_norm=m_ret_norm, ret_w_in=m_ret_w_in, ret_gn=m_ret_gn, ret_w_out=m_ret_w_out, mla_norm=m_mla_norm,
               mla_w_in=m_mla_w_in, mla_q_norm=m_mla_q_norm, mla_w_qb=m_mla_w_qb, mla_kv_norm=m_mla_kv_norm,
               mla_w_kvb=m_mla_w_kvb, mla_q_head_norm=m_mla_q_head_norm, mla_k_head_norm=m_mla_k_head_norm,
               mla_w_out=m_mla_w_out, ffn_norm=m_ffn_norm, ffn_w_in=m_ffn_w_in, ffn_conv_w=m_ffn_conv_w,
               ffn_conv_b=m_ffn_conv_b, ffn_w_out=m_ffn_w_out)
    var = dict(ret_norm=v_ret_norm, ret_w_in=v_ret_w_in, ret_gn=v_ret_gn, ret_w_out=v_ret_w_out, mla_norm=v_mla_norm,
               mla_w_in=v_mla_w_in, mla_q_norm=v_mla_q_norm, mla_w_qb=v_mla_w_qb, mla_kv_norm=v_mla_kv_norm,
               mla_w_kvb=v_mla_w_kvb, mla_q_head_norm=v_mla_q_head_norm, mla_k_head_norm=v_mla_k_head_norm,
               mla_w_out=v_mla_w_out, ffn_norm=v_ffn_norm, ffn_w_in=v_ffn_w_in, ffn_conv_w=v_ffn_conv_w,
               ffn_conv_b=v_ffn_conv_b, ffn_w_out=v_ffn_w_out)
    BIG = ["ret_w_in", "ret_w_out", "mla_w_in", "mla_w_qb", "mla_w_kvb", "mla_w_out", "ffn_w_in", "ffn_w_out"]
    REPL = ["ret_norm", "ffn_norm", "mla_q_head_norm", "mla_k_head_norm", "ffn_conv_b"]
    SHARDED_SMALL = ["ffn_conv_w", "ret_gn", "mla_norm", "mla_q_norm", "mla_kv_norm"]
    dev = _idx(_place())

    def blk16(k, i=0):
        return _bf16(w[k][i])

    small_vec = jnp.concatenate([w[k].reshape(-1) for k in SHARDED_SMALL])
    n_small = small_vec.shape[0]
    small_vec = jnp.pad(small_vec, (0, 3072 - n_small)).reshape(24, 128)
    Wret_in, sg = _comm_call("gather_ret_w_in", _Gather([blk16("ret_w_in"), small_vec]))
    sg = sg.reshape(NDEV, 3072)
    o0 = 0
    conv_w_full = sg[:, o0:o0 + 2112].reshape(NDEV, 2, 3, 352).transpose(1, 2, 0, 3).reshape(2, 3, FFN)
    o0 += 2112
    ret_gn_full = sg[:, o0:o0 + 256].reshape(NDEV, RET_H, 64).transpose(1, 0, 2).reshape(RET_H, 1, RET_DV)
    o0 += 256
    mla_norm_full = sg[:, o0:o0 + 128].reshape(1, D)
    o0 += 128
    q_norm_full = sg[:, o0:o0 + 48].reshape(1, MLA_QR)
    o0 += 48
    kv_norm_full = sg[:, o0:o0 + 32].reshape(1, MLA_KVR)

    cw = [conv_w_full[i].reshape(3, 4, FSH).transpose(1, 0, 2) for i in range(2)]
    cb = [ffn_conv_b[i].reshape(4, 1, FSH) for i in range(2)]
    fnorm = [ffn_norm[i].reshape(1, D) for i in range(2)]
    rtabs = _ret_tables(S)
    mtabs = _mla_tables(S)
    qh, kh = mla_q_head_norm.reshape(1, MLA_QK), mla_k_head_norm.reshape(1, MLA_QK)
    gains = (qh[:, :MLA_NOPE], _dup(qh[:, MLA_NOPE:]), kh[:, :MLA_NOPE], _dup(kh[:, MLA_NOPE:]))

    x0 = x.reshape(T, D)
    tgt = loss_target.reshape(T, D)
    proj, h0, (Wret_out, Wffn_out0) = _norm_proj(
        "ret_in", x0, ret_norm.reshape(1, D), Wret_in, _bs((1024, 768), lambda m, j: (m, j)), _sds((T, 6144), BF16),
        comm=_Gather([blk16("ret_w_out"), blk16("ffn_w_out", 0)]))
    Wret_out = Wret_out.reshape(RET_H * RET_DV, D)
    Wffn_out0 = Wffn_out0.reshape(4, FSH, D)
    (o_raw, rgt, states), (Wffn_in0,) = _ret_fwd(proj, rtabs, ret_gn_full, B, S, comm=_Gather([blk16("ffn_w_in", 0)]))
    x1 = _dense("ret_out", rgt, Wret_out, NN, F32, res=x0)
    MLA_W = ["mla_w_in", "mla_w_qb", "mla_w_kvb", "mla_w_out"]
    x2, ffn0_saved, got = _ffn_fwd(0, x1, fnorm[0], Wffn_in0, cw[0], cb[0], Wffn_out0, B, S,
                                   comm_in=_Gather([blk16(k) for k in MLA_W]))
    Wmla_in = got[0].reshape(D, MLA_QR + MLA_KVR + MLA_ROPE)
    Wq, Wkv, Wkr = Wmla_in[:, :MLA_QR], Wmla_in[:, MLA_QR:MLA_QR + MLA_KVR], Wmla_in[:, MLA_QR + MLA_KVR:]
    Wqb, Wkvb, Wmla_out = got[1:]

    h2 = _rms_fwd("mla_norm", x2, mla_norm_full)

    c_q, c_kv, k_rope = (_dense(n, h2, wm, NN, F32) for n, wm in
                         (("mla_in_q", Wq), ("mla_in_kv", Wkv), ("mla_in_kr", _dup(Wkr))))
    cqn = _rms_fwd("mla_q_norm", c_q, q_norm_full)
    ckvn = _rms_fwd("mla_kv_norm", c_kv, kv_norm_full)
    Wqb2 = jnp.concatenate([Wqb, Wqb[:, :, MLA_NOPE:]], axis=2).transpose(1, 0, 2).reshape(MLA_QR, MLA_H * MLA_PAD)
    Wkvb2 = Wkvb.transpose(1, 0, 2).reshape(MLA_KVR, MLA_H * (MLA_NOPE + MLA_V))
    Wmla_out2 = Wmla_out.reshape(D, D)
    q_raw = _dense("mla_qb", cqn, Wqb2, NN, F32, tm=1024)
    kvh = _dense("mla_kvb", ckvn, Wkvb2, NN, F32, tm=1024)
    (att, lse, qf, kf), (Wffn_in1, Wffn_out1) = _mla_fwd(
        q_raw, kvh, k_rope, gains, mtabs, B, S, comm=_Gather([blk16("ffn_w_in", 1), blk16("ffn_w_out", 1)]))
    Wffn_out1 = Wffn_out1.reshape(4, FSH, D)
    x3 = _dense("mla_out", att, Wmla_out2, NN, F32, res=x2)
    y, ffn1_saved, _ = _ffn_fwd(1, x3, fnorm[1], Wffn_in1, cw[1], cb[1], Wffn_out1, B, S)

    dy, colsq, dyb = _loss(y, tgt)
    loss = lax.psum(0.5 * jnp.sum(colsq) / D, ("x", "y", "c"))

    dx3, dx3b, gf1, r_ffn1_out, _ = _ffn_bwd(1, dy, dyb, ffn1_saved, fnorm[1], Wffn_in1, cw[1], cb[1], Wffn_out1, B, S,
                                             first_half_early=False)
    datt = _dense("mla_out_dx", dx3b, Wmla_out2, NT, BF16)
    fh = FSH // 2
    (dq_raw, dkvh, dkr, dqgn, dqgr, dkgn, dkgr), (r_ffn1_in_a, r_ffn1_in_b) = _mla_bwd(
        q_raw, kvh, k_rope, att, lse, datt, qf, kf, gains, mtabs, B, S,
        comm=_Exchange([gf1["w_in"], gf1["w_in"]], rows=[(0, fh), (fh, fh)]))
    dcqn = _dense("mla_qb_dx", dq_raw, Wqb2, NT, F32, tm=1024)
    dckvn = _dense("mla_kvb_dx", dkvh, Wkvb2, NT, F32, tm=1024)
    dcq, dg_qn = _rms_bwd("mla_q_norm_bwd", c_q, q_norm_full, dcqn)
    dckv, dg_kvn = _rms_bwd("mla_kv_norm_bwd", c_kv, kv_norm_full, dckvn)
    dqgr, dkgr = _fold(dqgr), _fold(dkgr)
    dproj2 = _bf16(jnp.concatenate([dcq, dckv, _fold(dkr)], axis=-1))
    dh2 = _dense("mla_in_dx", dproj2, Wmla_in, NT, F32)
    dx2, dg_mla_norm, dx2b = _rms_bwd("mla_norm_bwd", x2, mla_norm_full, dh2, dres=dx3, also_bf16=True)

    dx1, dx1b, gf0, r_ffn0_out, r_ffn0_in_a = _ffn_bwd(0, dx2, dx2b, ffn0_saved, fnorm[0], Wffn_in0, cw[0], cb[0],
                                                       Wffn_out0, B, S, first_half_early=True)
    drgt = _dense("ret_out_dx", dx1b, Wret_out, NT, BF16)
    dWret_out = _dense("ret_out_dw", rgt, dx1b, TN, BF16, tm=1024).reshape(NDEV, 256, D)
    (dq, dk, dv, dg, dgn_ret), (r_ffn0_in_b,) = _ret_bwd(proj, o_raw, states, drgt, rtabs, ret_gn_full, B, S,
                                                         comm=_Exchange([gf0["w_in"]], rows=[(fh, fh)]))
    dproj = jnp.concatenate([dq, dk, dv, dg], axis=-1)
    dWret_in, (r_ret_out,) = _mm(
        "ret_in_dw", h0, dproj, grid=(NDEV, T // 1024), a_spec=_bs((1024, D), lambda j, t: (t, 0)),
        b_spec=_bs((1024, 768), lambda j, t: (t, j)), o_spec=_bs((None, D, 768), lambda j, t: (j, 0, 0)),
        out_shape=_sds((NDEV, D, 768), BF16), dims=TN, kax=1, acc_shape=(D, 768), comm=_Exchange([dWret_out]))
    qr = D // 4
    dh0, r_ret_in_ab = _mm(
        "ret_in_dx", dproj, Wret_in, grid=(T // 1024, NDEV), a_spec=_bs((1024, 768), lambda m, k: (m, k)),
        b_spec=_bs((None, D, 768), lambda m, k: (k, 0, 0)), o_spec=_bs((1024, D), lambda m, k: (m, 0)),
        out_shape=_sds((T, D), F32), dims=NT, kax=1, acc_shape=(1024, D),
        comm=_Exchange([dWret_in, dWret_in], rows=[(0, qr), (qr, qr)]))
    (dx0, dg_ret_norm), (r_ret_in_c,) = _rms_bwd("ret_norm_bwd", x0, ret_norm.reshape(1, D), dh0, dres=dx1,
                                                 comm=_Exchange([dWret_in], rows=[(2 * qr, qr)]))
    grad_x = dx0.reshape(B, S, D)
    dWmla_out, (r_ret_in_d,) = _dense("mla_out_dw", att, dx3b, TN, BF16, tm=1024,
                                      comm=_Exchange([dWret_in], rows=[(3 * qr, qr)]))
    dWmla_out = dWmla_out.reshape(NDEV, MLA_V, D)
    dWqb, (r_mla_out,) = _dense("mla_qb_dw", cqn, dq_raw, TN, BF16, tm=1024, comm=_Exchange([dWmla_out]))
    dWqb = dWqb.reshape(MLA_QR, MLA_H, MLA_PAD)
    dWqb = jnp.concatenate([dWqb[:, :, :MLA_NOPE], _fold(dWqb[:, :, MLA_NOPE:])], axis=2).transpose(1, 0, 2)
    dWkvb, (r_mla_qb,) = _dense("mla_kvb_dw", ckvn, dkvh, TN, BF16, tm=1024, comm=_Exchange([dWqb]))
    dWkvb = dWkvb.reshape(MLA_KVR, MLA_H, MLA_NOPE + MLA_V).transpose(1, 0, 2)
    dWmla_in, (r_mla_kvb,) = _dense("mla_in_dw", h2, dproj2, TN, BF16, comm=_Exchange([dWkvb]))
    dWmla_in = dWmla_in.reshape(NDEV, 128, 704)
    received = dict(ret_w_in=[*r_ret_in_ab, r_ret_in_c, r_ret_in_d], ret_w_out=[r_ret_out], mla_w_qb=[r_mla_qb],
                    mla_w_kvb=[r_mla_kvb], mla_w_out=[r_mla_out],
                    ffn_w_in=[r_ffn0_in_a, r_ffn0_in_b, r_ffn1_in_a, r_ffn1_in_b], ffn_w_out=[r_ffn0_out, r_ffn1_out])

    dconv_w = jnp.stack([g_["cw"].transpose(1, 0, 2).reshape(3, FFN) for g_ in (gf0, gf1)])
    dconv_b = jnp.stack([g_["cb"].reshape(FFN) for g_ in (gf0, gf1)])
    small_parts = [dg_ret_norm, gf0["norm"], gf1["norm"], dg_mla_norm, dg_qn, dg_kvn, dqgn, dqgr, dkgn, dkgr, dgn_ret,
                   dconv_w, dconv_b]
    small_g = jnp.concatenate([p.reshape(-1) for p in small_parts]).reshape(232, 128)
    small_all, received["mla_w_in"] = _comm_call("gather_small_grads", _Both(_Gather([small_g]), _Exchange([dWmla_in])))
    received["mla_w_in"] = [received["mla_w_in"]]
    sred = _sum8("sum_small_grads", small_all).reshape(-1)

    def take(n):
        nonlocal off
        out = sred[off:off + n]
        off += n
        return out

    off = 0
    g_small = dict(ret_norm=take(D).reshape(1, D), ffn_norm=take(2 * D).reshape(2, D), mla_norm=take(D),
                   mla_q_norm=take(MLA_QR), mla_kv_norm=take(MLA_KVR))
    g_small["mla_q_head_norm"] = take(MLA_QK).reshape(1, MLA_QK)
    g_small["mla_k_head_norm"] = take(MLA_QK).reshape(1, MLA_QK)
    g_small["ret_gn"] = take(RET_H * RET_DV).reshape(1, RET_H, RET_DV)
    g_small["ffn_conv_w"] = take(2 * 3 * FFN).reshape(2, 3, FFN)
    g_small["ffn_conv_b"] = take(2 * FFN).reshape(2, FFN)
    g_small["mla_norm"] = lax.dynamic_slice(g_small["mla_norm"], (dev * 128,), (128,)).reshape(1, 128)
    g_small["mla_q_norm"] = lax.dynamic_slice(g_small["mla_q_norm"], (dev * 48,), (48,)).reshape(1, 48)
    g_small["mla_kv_norm"] = lax.dynamic_slice(g_small["mla_kv_norm"], (dev * 32,), (32,)).reshape(1, 32)
    g_small["ret_gn"] = lax.dynamic_slice(g_small["ret_gn"], (0, 0, dev * 64), (1, RET_H, 64))
    g_small["ffn_conv_w"] = lax.dynamic_slice(g_small["ffn_conv_w"], (0, 0, dev * 352), (2, 3, 352))

    grads, delta, new_m, new_v = {}, {}, {}, {}
    for k in BIG:
        rcs = received[k]
        tr = max(t for t in range(16, 257, 16) if rcs[0].shape[1] % t == 0)
        flip = (lambda t: t.transpose(0, 2, 1)) if k == "ffn_w_in" else (lambda t: t)
        res = _adamw(f"adamw_{k}", rcs, flip(w[k]), flip(mom[k]), flip(var[k]), tr=tr)
        grads[k], delta[k], new_m[k], new_v[k] = (flip(t) for t in res)
    SMALL = REPL + SHARDED_SMALL

    def pack(d):
        vflat = jnp.concatenate([d[k].reshape(-1) for k in SMALL])
        return jnp.pad(vflat, (0, 96 * 128 - vflat.shape[0])).reshape(1, 96, 128)

    ps = _adamw("adamw_small", [pack(g_small)], pack(w), pack(mom), pack(var))
    off = 0
    for k in SMALL:
        n = w[k].size
        grads[k], delta[k], new_m[k], new_v[k] = (t.reshape(-1)[off:off + n].reshape(w[k].shape) for t in ps)
        off += n
    names = list(w)
    return (loss, grad_x, *[grads[k] for k in names], *[delta[k] for k in names], *[new_m[k] for k in names],
            *[new_v[k] for k in names])
```

```python
import functools

import jax
import jax.numpy as jnp
from jax import lax
from jax.experimental import pallas as pl
from jax.experimental.pallas import tpu as pltpu

F32, BF16 = jnp.float32, jnp.bfloat16

NDEV = 8
D_MODEL = 1024
CHUNK = 64
RMS_EPS = 1e-6
ROPE_THETA = 10000.0
RET_H, RET_DK, RET_DV = 4, 256, 512
RET_SC = 256
MLA_H, MLA_QR, MLA_KVR = 8, 384, 256
MLA_NOPE, MLA_ROPE, MLA_V = 128, 64, 128
MLA_QK = MLA_NOPE + MLA_ROPE
MASK_VALUE = -1e30
FFN = 2816
FSH = FFN * 2 // NDEV
ATT_TQ = 256
ADAM_LR, ADAM_B1, ADAM_B2, ADAM_EPS, ADAM_WD, ADAM_STEP = 0.001, 0.9, 0.999, 1e-08, 0.01, 10
MESH = pl.DeviceIdType.MESH
VMEM_LIMIT = 56 * 2 ** 20


def _cp(sem):
    return pltpu.CompilerParams(dimension_semantics=sem, vmem_limit_bytes=VMEM_LIMIT)


def _dot(a, b, dims):
    return lax.dot_general(a, b, (dims, ((), ())), preferred_element_type=F32)


NN = ((1,), (0,))
NT = ((1,), (1,))
TN = ((0,), (0,))


def _place():
    return lax.axis_index("x"), lax.axis_index("y"), lax.axis_index("c")


def _idx(d):
    return 4 * d[0] + 2 * d[1] + d[2]


ANY = pl.BlockSpec(memory_space=pl.ANY)


class _Gather:
    sem0 = 0

    def __init__(self, arrs, parts=1):
        self.srcs = list(arrs)
        self.parts = parts
        self.nsem = len(arrs) * parts
        self.out_shape = [jax.ShapeDtypeStruct((NDEV,) + a.shape, a.dtype) for a in arrs]

    def _copies(self, ins, outs, send, recv, loc):
        n = self.nsem
        s0 = self.sem0
        x, y, c = _place()
        me, sib = (x, y, c), (x, y, 1 - c)
        chips = [(1 - x, y), (x, 1 - y), (1 - x, 1 - y)]

        def piece(ref, v, *lead):
            a, p = divmod(v, self.parts)
            if self.parts > 1:
                rows = self.srcs[a].shape[0] // self.parts
                lead = (*lead, pl.ds(p * rows, rows))
            return ref[a].at[lead] if lead else ref[a]

        def cp(a, k, block, to, src=None):
            dst = piece(outs, a, _idx(block))
            return pltpu.make_async_remote_copy(src_ref=dst if src is None else src, dst_ref=dst,
                                                send_sem=send.at[s0 + a, k], recv_sem=recv.at[s0 + a, k], device_id=to,
                                                device_id_type=MESH)

        own = [piece(ins, a) for a in range(n)]
        mine = [pltpu.make_async_copy(own[a], piece(outs, a, _idx(me)), loc.at[s0 + a]) for a in range(n)]
        first = [cp(a, 0, me, sib, src=own[a]) for a in range(n)]
        first += [cp(a, 1 + j, me, (*chip, c), src=own[a]) for a in range(n) for j, chip in enumerate(chips)]
        landed = [cp(a, 1 + j, (*chip, c), me) for j, chip in enumerate(chips) for a in range(n)]
        passed = [cp(a, 4 + j, (*chip, c), sib) for j, chip in enumerate(chips) for a in range(n)]
        from_sib = [cp(a, 0, sib, me) for a in range(n)]
        from_sib += [cp(a, 4 + j, (*chip, 1 - c), me) for j, chip in enumerate(chips) for a in range(n)]
        return mine, first, landed, passed, from_sib

    def start(self, *refs):
        mine, first, _, _, _ = self._copies(*refs)
        for cp in mine + first:
            cp.start()

    def mid(self, *refs):
        _, _, landed, passed, _ = self._copies(*refs)
        for got, on in zip(landed, passed):
            got.wait_recv()
            on.start()

    def finish(self, *refs):
        mine, first, _, passed, from_sib = self._copies(*refs)
        for cp in from_sib:
            cp.wait_recv()
        for cp in first + passed:
            cp.wait_send()
        for cp in mine:
            cp.wait()


class _Exchange:
    sem0 = 0

    def __init__(self, arrs, rows=None):
        self.srcs = list(arrs)
        self.nsem = len(arrs)
        self.rows = rows if rows is not None else [None] * len(arrs)
        self.out_shape = [jax.ShapeDtypeStruct(a.shape if r is None else (a.shape[0], r[1]) + a.shape[2:], a.dtype)
                          for a, r in zip(arrs, self.rows)]

    def _copies(self, ins, outs, send, recv, loc):
        n = len(self.srcs)
        s0 = self.sem0
        x, y, c = _place()
        me = _idx((x, y, c))

        def src(a, q):
            r = self.rows[a]
            return ins[a].at[q] if r is None else ins[a].at[q, pl.ds(r[0], r[1])]

        mine = [pltpu.make_async_copy(src(a, me), outs[a].at[me], loc.at[s0 + a]) for a in range(n)]
        remote = []
        for k in range(1, NDEV):
            peer = (x ^ (k >> 2), y ^ ((k >> 1) & 1), c ^ (k & 1))
            remote += [pltpu.make_async_remote_copy(
                src_ref=src(a, _idx(peer)), dst_ref=outs[a].at[me], send_sem=send.at[s0 + a, k - 1],
                recv_sem=recv.at[s0 + a, k - 1], device_id=peer, device_id_type=MESH) for a in range(n)]
        return mine, remote

    def start(self, *refs):
        mine, remote = self._copies(*refs)
        for cp in mine + remote:
            cp.start()

    def mid(self, *refs):
        pass

    def finish(self, *refs):
        mine, remote = self._copies(*refs)
        for cp in remote + mine:
            cp.wait()


class _Both:
    def __init__(self, one, two):
        self.parts = (one, two)
        two.sem0 = one.nsem
        self.nsem = one.nsem + two.nsem
        self.srcs = one.srcs + two.srcs
        self.out_shape = one.out_shape + two.out_shape

    def _each(self, phase, ins, outs, send, recv, loc):
        n = len(self.parts[0].srcs)
        getattr(self.parts[0], phase)(ins[:n], outs[:n], send, recv, loc)
        getattr(self.parts[1], phase)(ins[n:], outs[n:], send, recv, loc)

    def start(self, *refs):
        self._each("start", *refs)

    def mid(self, *refs):
        self._each("mid", *refs)

    def finish(self, *refs):
        self._each("finish", *refs)


def _comm_scratch(n):
    return [pltpu.SemaphoreType.DMA((n, 7)), pltpu.SemaphoreType.DMA((n, 7)), pltpu.SemaphoreType.DMA((n,))]


def _comm_call(name, comm):
    n = len(comm.srcs)

    def body(*refs):
        parts = (refs[:n], refs[n:2 * n]) + tuple(refs[2 * n:])
        comm.start(*parts)
        comm.mid(*parts)
        comm.finish(*parts)

    return pl.pallas_call(body, name=name, in_specs=[ANY] * n, out_specs=[ANY] * n, out_shape=comm.out_shape,
                          scratch_shapes=_comm_scratch(comm.nsem))(*comm.srcs)


def _pcall(body, *, name, grid, in_specs, out_specs, out_shape, scratch_shapes, sem, args, comm=None):
    if comm is None:
        return pl.pallas_call(body, name=name, grid=grid, in_specs=in_specs, out_specs=out_specs, out_shape=out_shape,
                              scratch_shapes=scratch_shapes, compiler_params=_cp(sem))(*args), None
    ni, no, ns, nc = len(in_specs), len(out_shape), len(scratch_shapes), len(comm.srcs)
    total = 1
    for g in grid:
        total *= g
    middle = (4 * total) // 5

    def wrapped(*refs):
        ins, csrc = refs[:ni], refs[ni:ni + nc]
        outs, cdst = refs[ni + nc:ni + nc + no], refs[ni + nc + no:ni + 2 * nc + no]
        scr, sems = refs[ni + 2 * nc + no:ni + 2 * nc + no + ns], refs[ni + 2 * nc + no + ns:]
        step = pl.program_id(0)
        for k in range(1, len(grid)):
            step = step * grid[k] + pl.program_id(k)
        parts = (csrc, cdst) + tuple(sems)

        @pl.when(step == 0)
        def _():
            comm.start(*parts)

        body(*ins, *outs, *scr)

        @pl.when(step == middle)
        def _():
            comm.mid(*parts)

        @pl.when(step == total - 1)
        def _():
            comm.finish(*parts)

    res = pl.pallas_call(
        wrapped, name=name, grid=grid, in_specs=list(in_specs) + [ANY] * nc, out_specs=list(out_specs) + [ANY] * nc,
        out_shape=list(out_shape) + comm.out_shape, scratch_shapes=list(scratch_shapes) + _comm_scratch(comm.nsem),
        compiler_params=_cp(("arbitrary",) * len(grid)))(*args, *comm.srcs)
    return res[:no], res[no:]


def _mm(name, a, b, *, grid, a_spec, b_spec, o_spec, out_shape, dims, kax=None, res=None, res_spec=None,
        jb=0, acc_shape=None, comm=None):
    nk = grid[kax] if kax is not None else 1

    def body(*refs):
        if res is not None:
            a_ref, b_ref, r_ref, o_ref = refs[:4]
        else:
            a_ref, b_ref, o_ref = refs[:3]

        def product():
            if not jb:
                return _dot(a_ref[...], b_ref[...], dims)
            part = _dot(a_ref[0], b_ref[0], dims)
            for j in range(1, jb):
                part = part + _dot(a_ref[j], b_ref[j], dims)
            return part

        def fin(acc):
            if res is not None:
                acc = acc + r_ref[...]
            o_ref[...] = acc.astype(o_ref.dtype)

        if nk == 1:
            fin(product())
        else:
            acc_ref = refs[-1]
            k = pl.program_id(kax)

            @pl.when(k == 0)
            def _():
                acc_ref[...] = jnp.zeros_like(acc_ref)

            acc_ref[...] += product()

            @pl.when(k == nk - 1)
            def _():
                fin(acc_ref[...])

    sem = tuple("arbitrary" if i == kax else "parallel" for i in range(len(grid)))
    in_specs = [a_spec, b_spec] + ([res_spec] if res is not None else [])
    args = (a, b) + ((res,) if res is not None else ())
    scratch = [pltpu.VMEM(acc_shape, F32)] if nk > 1 else []
    (out,), got = _pcall(body, name=name, grid=grid, in_specs=in_specs, out_specs=[o_spec], out_shape=[out_shape],
                         scratch_shapes=scratch, sem=sem, args=args, comm=comm)
    return out if comm is None else (out, got)


def _bs(shape, fn):
    return pl.BlockSpec(shape, fn)


def _rms_fwd(name, x, g, tm=512):
    T, D = x.shape

    def body(x_ref, g_ref, o_ref):
        xf = x_ref[...]
        r = lax.rsqrt(jnp.mean(xf * xf, axis=-1, keepdims=True) + RMS_EPS)
        o_ref[...] = (xf * r * g_ref[...]).astype(o_ref.dtype)

    return pl.pallas_call(
        body, name=name, grid=(T // tm,),
        in_specs=[_bs((tm, D), lambda i: (i, 0)), _bs((1, D), lambda i: (0, 0))],
        out_specs=_bs((tm, D), lambda i: (i, 0)), out_shape=jax.ShapeDtypeStruct((T, D), BF16),
        compiler_params=_cp(("parallel",)))(x, g)


def _rms_bwd(name, x, g, dh, dres=None, tm=512, also_bf16=False, comm=None):
    T, D = x.shape

    def body(*refs):
        if also_bf16:
            refs, dxb_ref = refs[:-1], refs[-1]
        if dres is not None:
            x_ref, g_ref, dh_ref, dres_ref, dx_ref, dg_ref = refs
        else:
            x_ref, g_ref, dh_ref, dx_ref, dg_ref = refs
        i = pl.program_id(0)
        xf = x_ref[...]
        r = lax.rsqrt(jnp.mean(xf * xf, axis=-1, keepdims=True) + RMS_EPS)
        xh = xf * r
        d = dh_ref[...].astype(F32)
        dxh = d * g_ref[...]
        dx = r * (dxh - xh * jnp.mean(dxh * xh, axis=-1, keepdims=True))
        if dres is not None:
            dx = dx + dres_ref[...]
        dx_ref[...] = dx
        if also_bf16:
            dxb_ref[...] = dx.astype(BF16)
        part = jnp.sum(d * xh, axis=0, keepdims=True)

        @pl.when(i == 0)
        def _():
            dg_ref[...] = part

        @pl.when(i > 0)
        def _():
            dg_ref[...] += part

    row = _bs((tm, D), lambda i: (i, 0))
    vec = _bs((1, D), lambda i: (0, 0))
    in_specs = [row, vec, row] + ([row] if dres is not None else [])
    args = (x, g, dh) + ((dres,) if dres is not None else ())
    extra = [jax.ShapeDtypeStruct((T, D), BF16)] if also_bf16 else []
    outs, got = _pcall(
        body, name=name, grid=(T // tm,), in_specs=in_specs, out_specs=[row, vec] + [row] * len(extra),
        out_shape=[jax.ShapeDtypeStruct((T, D), F32), jax.ShapeDtypeStruct((1, D), F32)] + extra, scratch_shapes=[],
        sem=("arbitrary",), args=args, comm=comm)
    return outs if comm is None else (outs, got)


def _loss(y, tgt, tm=512):
    T, D = y.shape

    def body(y_ref, t_ref, dy_ref, s_ref, dyb_ref):
        i = pl.program_id(0)
        e = y_ref[...] - t_ref[...]
        dy = e * (1.0 / D)
        dy_ref[...] = dy
        dyb_ref[...] = dy.astype(BF16)
        part = jnp.sum(e * e, axis=0, keepdims=True)

        @pl.when(i == 0)
        def _():
            s_ref[...] = part

        @pl.when(i > 0)
        def _():
            s_ref[...] += part

    row = _bs((tm, D), lambda i: (i, 0))
    return pl.pallas_call(
        body, name="loss_head", grid=(T // tm,), in_specs=[row, row],
        out_specs=[row, _bs((1, D), lambda i: (0, 0)), row],
        out_shape=[jax.ShapeDtypeStruct((T, D), F32), jax.ShapeDtypeStruct((1, D), F32),
                   jax.ShapeDtypeStruct((T, D), BF16)],
        compiler_params=_cp(("arbitrary",)))(y, tgt)


def _shift_rows(t, k, row):
    return jnp.where(row >= k, pltpu.roll(t, k, 0), 0.0)


def _shift_rows_up(t, k, row, n):
    return jnp.where(row < n - k, pltpu.roll(t, n - k, 0), 0.0)


def _convffn_fwd(name, u, cw, cb, B, S):
    _, J, T, F = u.shape

    def body(u_ref, cw_ref, cb_ref, o_ref):
        a = u_ref[0].astype(F32)
        g = u_ref[1].astype(F32)
        row = lax.broadcasted_iota(jnp.int32, (S, F), 0)
        w0, w1, w2 = cw_ref[0:1, :], cw_ref[1:2, :], cw_ref[2:3, :]
        gc = _shift_rows(g, 2, row) * w0 + _shift_rows(g, 1, row) * w1 + g * w2 + cb_ref[...]
        o_ref[...] = (gc * jax.nn.sigmoid(gc) * a).astype(o_ref.dtype)

    return pl.pallas_call(
        body, name=name, grid=(J, B),
        in_specs=[_bs((2, None, S, F), lambda j, b: (0, j, b, 0)), _bs((None, 3, F), lambda j, b: (j, 0, 0)),
                  _bs((None, 1, F), lambda j, b: (j, 0, 0))],
        out_specs=_bs((None, S, F), lambda j, b: (j, b, 0)), out_shape=jax.ShapeDtypeStruct((J, T, F), BF16),
        compiler_params=_cp(("parallel", "parallel")))(u, cw, cb)


def _convffn_bwd(name, u, cw, cb, dgt, B, S, comm=None):
    _, J, T, F = u.shape

    def body(u_ref, cw_ref, cb_ref, d_ref, du_ref, dcw_ref, dcb_ref):
        b = pl.program_id(1)
        a = u_ref[0].astype(F32)
        g = u_ref[1].astype(F32)
        d = d_ref[...].astype(F32)
        row = lax.broadcasted_iota(jnp.int32, (S, F), 0)
        w0, w1, w2 = cw_ref[0:1, :], cw_ref[1:2, :], cw_ref[2:3, :]
        g1, g2 = _shift_rows(g, 1, row), _shift_rows(g, 2, row)
        gc = g2 * w0 + g1 * w1 + g * w2 + cb_ref[...]
        sg = jax.nn.sigmoid(gc)
        du_ref[0] = (d * gc * sg).astype(du_ref.dtype)
        dgc = d * a * (sg * (1.0 + gc * (1.0 - sg)))
        dg = dgc * w2 + _shift_rows_up(dgc, 1, row, S) * w1 + _shift_rows_up(dgc, 2, row, S) * w0
        du_ref[1] = dg.astype(du_ref.dtype)
        parts = [jnp.sum(dgc * g2, axis=0, keepdims=True), jnp.sum(dgc * g1, axis=0, keepdims=True),
                 jnp.sum(dgc * g, axis=0, keepdims=True)]
        pb = jnp.sum(dgc, axis=0, keepdims=True)

        @pl.when(b == 0)
        def _():
            for k in range(3):
                dcw_ref[k:k + 1, :] = parts[k]
            dcb_ref[...] = pb

        @pl.when(b > 0)
        def _():
            for k in range(3):
                dcw_ref[k:k + 1, :] += parts[k]
            dcb_ref[...] += pb

    uspec = _bs((2, None, S, F), lambda j, b: (0, j, b, 0))
    return _pcall(
        body, name=name, grid=(J, B),
        in_specs=[uspec, _bs((None, 3, F), lambda j, b: (j, 0, 0)), _bs((None, 1, F), lambda j, b: (j, 0, 0)),
                  _bs((None, S, F), lambda j, b: (j, b, 0))],
        out_specs=[uspec, _bs((None, 3, F), lambda j, b: (j, 0, 0)), _bs((None, 1, F), lambda j, b: (j, 0, 0))],
        out_shape=[jax.ShapeDtypeStruct(u.shape, BF16), jax.ShapeDtypeStruct((J, 3, F), F32),
                   jax.ShapeDtypeStruct((J, 1, F), F32)],
        scratch_shapes=[], sem=("parallel", "arbitrary"), args=(u, cw, cb, dgt), comm=comm)


def _ret_tables(S):
    half = RET_DK // 2
    inv = ROPE_THETA ** (-jnp.arange(half, dtype=F32) / half)
    ang = jnp.arange(S).astype(F32)[:, None] * inv[None, :]
    lg = jnp.log1p(-jnp.exp2(-5.0 - jnp.arange(RET_H, dtype=F32)))
    i = jnp.arange(RET_SC, dtype=F32)
    same_or_earlier = (jnp.floor(i[None, :] / CHUNK) <= jnp.floor(i[:, None] / CHUNK)).astype(F32)
    dm = jnp.exp(lg[:, None, None] * jnp.abs(i[:, None] - i[None, :])) * same_or_earlier[None]
    qd = jnp.exp(lg[:, None] * (i + 1.0))[:, :, None]
    kd = jnp.exp(lg[:, None] * (RET_SC - 1.0 - i))[:, :, None]
    cd = jnp.exp(lg * RET_SC)[:, None, None]
    return jnp.cos(ang), jnp.sin(ang), dm, qd, kd, cd


def _rope_halves(t, cs, sn):
    h = t.shape[-1] // 2
    t1, t2 = t[:, :h], t[:, h:]
    return jnp.concatenate([t1 * cs - t2 * sn, t2 * cs + t1 * sn], axis=-1)


def _unrope_halves(d, cs, sn):
    h = d.shape[-1] // 2
    d1, d2 = d[:, :h], d[:, h:]
    return jnp.concatenate([d1 * cs + d2 * sn, d2 * cs - d1 * sn], axis=-1)


def _ret_specs(nC, order):
    SC = RET_SC

    def sp(shape, fn):
        return _bs(shape, lambda *g: fn(*order(*g)))

    q = sp((SC, RET_DK), lambda b, h, c: (b * nC + c, h))
    k = sp((SC, RET_DK), lambda b, h, c: (b * nC + c, RET_H + h))
    v = sp((SC, RET_DV), lambda b, h, c: (b * nC + c, RET_H + h))
    g = sp((SC, RET_DV), lambda b, h, c: (b * nC + c, 2 * RET_H + h))
    cs = sp((SC, RET_DK // 2), lambda b, h, c: (c, 0))
    dm = sp((None, SC, SC), lambda b, h, c: (h, 0, 0))
    dv = sp((None, SC, 1), lambda b, h, c: (h, 0, 0))
    cd = sp((None, 1, 1), lambda b, h, c: (h, 0, 0))
    gn = sp((None, 1, RET_DV), lambda b, h, c: (h, 0, 0))
    wide = sp((SC, RET_DV), lambda b, h, c: (b * nC + c, h))
    narrow = sp((SC, RET_DK), lambda b, h, c: (b * nC + c, h))
    st = sp((None, None, None, RET_DK, RET_DV), lambda b, h, c: (b, h, c, 0, 0))
    return dict(q=q, k=k, v=v, g=g, cs=cs, dm=dm, dv=dv, cd=cd, gn=gn, wide=wide, narrow=narrow, st=st)


def _ret_fwd(proj, tabs, gn, B, S, comm=None):
    T = B * S
    nC = S // RET_SC
    cos, sin, dm, qd, kd, cd = tabs
    s = _ret_specs(nC, lambda b, h, c: (b, h, c))

    def body(q_ref, k_ref, v_ref, g_ref, cos_ref, sin_ref, dm_ref, qd_ref, kd_ref, cd_ref, gn_ref,
             o_ref, gt_ref, st_ref, state):
        c = pl.program_id(2)

        @pl.when(c == 0)
        def _():
            state[...] = jnp.zeros_like(state)

        cs, sn = cos_ref[...], sin_ref[...]
        qf = _rope_halves(q_ref[...].astype(F32), cs, sn)
        kf = _rope_halves(k_ref[...].astype(F32), cs, sn) * (RET_DK ** -0.5)
        v = v_ref[...]
        p = _dot(qf.astype(BF16), kf.astype(BF16), NT) * dm_ref[...]
        st = state[...]
        stb = st.astype(BF16)
        st_ref[...] = stb
        o = _dot(p.astype(BF16), v, NN) + _dot((qf * qd_ref[...]).astype(BF16), stb, NN)
        state[...] = st * cd_ref[...] + _dot((kf * kd_ref[...]).astype(BF16), v, TN)
        o_ref[...] = o
        r = lax.rsqrt(jnp.mean(o * o, axis=-1, keepdims=True) + RMS_EPS)
        gf = g_ref[...].astype(F32)
        gt_ref[...] = ((o * r * gn_ref[...]) * (gf * jax.nn.sigmoid(gf))).astype(BF16)

    return _pcall(
        body, name="ret_fwd", grid=(B, RET_H, nC),
        in_specs=[s["q"], s["k"], s["v"], s["g"], s["cs"], s["cs"], s["dm"], s["dv"], s["dv"], s["cd"], s["gn"]],
        out_specs=[s["wide"], s["wide"], s["st"]],
        out_shape=[jax.ShapeDtypeStruct((T, RET_H * RET_DV), F32), jax.ShapeDtypeStruct((T, RET_H * RET_DV), BF16),
                   jax.ShapeDtypeStruct((B, RET_H, nC, RET_DK, RET_DV), BF16)],
        scratch_shapes=[pltpu.VMEM((RET_DK, RET_DV), F32)], sem=("parallel", "parallel", "arbitrary"),
        args=(proj, proj, proj, proj, cos, sin, dm, qd, kd, cd, gn), comm=comm)


def _ret_bwd(proj, o_raw, states, dgt, tabs, gn, B, S, comm=None):
    T = B * S
    nC = S // RET_SC
    cos, sin, dm, qd, kd, cd = tabs
    s = _ret_specs(nC, lambda b, c, h: (b, h, nC - 1 - c))

    def body(q_ref, k_ref, v_ref, g_ref, o_ref, st_ref, d_ref, cos_ref, sin_ref, dm_ref, qd_ref, kd_ref, cd_ref,
             gn_ref, dproj_ref, dgn_ref, dstates):
        b, c, h = pl.program_id(0), pl.program_id(1), pl.program_id(2)
        dstate = dstates.at[h]

        @pl.when(c == 0)
        def _():
            dstate[...] = jnp.zeros_like(dstate)

        @pl.when((b == 0) & (c == 0))
        def _():
            dgn_ref[h] = jnp.zeros((1, RET_DV), F32)

        cs, sn = cos_ref[...], sin_ref[...]
        qf = _rope_halves(q_ref[...].astype(F32), cs, sn)
        kf = _rope_halves(k_ref[...].astype(F32), cs, sn) * (RET_DK ** -0.5)
        v = v_ref[...]
        gnv = gn_ref[...]
        o = o_ref[...]
        r = lax.rsqrt(jnp.mean(o * o, axis=-1, keepdims=True) + RMS_EPS)
        oh = o * r
        gf = g_ref[...].astype(F32)
        sg = jax.nn.sigmoid(gf)
        d = d_ref[...].astype(F32)
        dg = (d * (oh * gnv) * (sg * (1.0 + gf * (1.0 - sg)))).astype(BF16)
        don = d * (gf * sg)
        dgn_ref[h] += jnp.sum(don * oh, axis=0, keepdims=True)
        doh = don * gnv
        dO = (r * (doh - oh * jnp.mean(doh * oh, axis=-1, keepdims=True))).astype(BF16)
        dmv = dm_ref[...]
        qb, kb = qf.astype(BF16), kf.astype(BF16)
        p = (_dot(qb, kb, NT) * dmv).astype(BF16)
        dp = (_dot(dO, v, NT) * dmv).astype(BF16)
        st = st_ref[...]
        dsn = dstate[...]
        dsb = dsn.astype(BF16)
        qdv, kdv = qd_ref[...], kd_ref[...]
        dq = _dot(dp, kb, NN) + _dot(dO, st, NT) * qdv
        dk = _dot(dp, qb, TN) + _dot(v, dsb, NT) * kdv
        dv = _dot(p, dO, TN) + _dot((kf * kdv).astype(BF16), dsb, NN)
        dstate[...] = dsn * cd_ref[...] + _dot((qf * qdv).astype(BF16), dO, TN)
        dq = _unrope_halves(dq, cs, sn).astype(BF16)
        dk = (_unrope_halves(dk, cs, sn) * (RET_DK ** -0.5)).astype(BF16)
        dv = dv.astype(BF16)
        nq, nv = RET_H * RET_DK, RET_H * RET_DV
        for hh in range(RET_H):
            @pl.when(h == hh)
            def _():
                dproj_ref[:, hh * RET_DK:(hh + 1) * RET_DK] = dq
                dproj_ref[:, nq + hh * RET_DK:nq + (hh + 1) * RET_DK] = dk
                dproj_ref[:, 2 * nq + hh * RET_DV:2 * nq + (hh + 1) * RET_DV] = dv
                dproj_ref[:, 2 * nq + nv + hh * RET_DV:2 * nq + nv + (hh + 1) * RET_DV] = dg

    width = 2 * RET_H * (RET_DK + RET_DV)
    return _pcall(
        body, name="ret_bwd", grid=(B, nC, RET_H),
        in_specs=[s["q"], s["k"], s["v"], s["g"], s["wide"], s["st"], s["wide"], s["cs"], s["cs"], s["dm"], s["dv"],
                  s["dv"], s["cd"], s["gn"]],
        out_specs=[_bs((RET_SC, width), lambda b, c, h: (b * nC + nC - 1 - c, 0)),
                   _bs((RET_H, 1, RET_DV), lambda b, c, h: (0, 0, 0))],
        out_shape=[jax.ShapeDtypeStruct((T, width), BF16), jax.ShapeDtypeStruct((RET_H, 1, RET_DV), F32)],
        scratch_shapes=[pltpu.VMEM((RET_H, RET_DK, RET_DV), F32)], sem=("arbitrary", "arbitrary", "arbitrary"),
        args=(proj, proj, proj, proj, o_raw, states, dgt, cos, sin, dm, qd, kd, cd, gn), comm=comm)


MLA_PAD = 256
MLA_R2 = 2 * MLA_ROPE


def _dup(t):
    return jnp.concatenate([t, t], axis=-1)


def _fold(t):
    return t[..., :MLA_ROPE] + t[..., MLA_ROPE:]


def _mla_tables(S):
    half = MLA_ROPE // 2
    inv = ROPE_THETA ** (-jnp.arange(half, dtype=F32) / half)
    ang = jnp.arange(S).astype(F32)[:, None] * inv[None, :]
    cos, sin, zero = jnp.cos(ang), jnp.sin(ang), jnp.zeros((S, MLA_ROPE), F32)
    return jnp.concatenate([cos, cos, zero], axis=-1), jnp.concatenate([-sin, sin, zero], axis=-1)


def _head_norm_rope(n, r2, gn, gr2, cos, sin, scale):
    ssq = jnp.sum(n * n, axis=-1, keepdims=True) + 0.5 * jnp.sum(r2 * r2, axis=-1, keepdims=True)
    rstd = lax.rsqrt(ssq * (1.0 / MLA_QK) + RMS_EPS)
    yn = n * rstd * gn
    yr = r2 * rstd * gr2
    z = yr * cos + pltpu.roll(yr, MLA_ROPE // 2, 1) * sin
    if scale != 1.0:
        yn, z = yn * scale, z * scale
    return yn, z


def _head_norm_rope_bwd(dn, dz, n, r2, gn, gr2, cos, sin, scale):
    ssq = jnp.sum(n * n, axis=-1, keepdims=True) + 0.5 * jnp.sum(r2 * r2, axis=-1, keepdims=True)
    rstd = lax.rsqrt(ssq * (1.0 / MLA_QK) + RMS_EPS)
    hn, hr = n * rstd, r2 * rstd
    if scale != 1.0:
        dn, dz = dn * scale, dz * scale
    dyr = dz * cos + pltpu.roll(dz * sin, MLA_R2 - MLA_ROPE // 2, 1)
    dgn = jnp.sum(dn * hn, axis=0, keepdims=True)
    dgr = jnp.sum(dyr * hr, axis=0, keepdims=True)
    dhn, dhr = dn * gn, dyr * gr2
    mt = (jnp.sum(dhn * hn, axis=-1, keepdims=True) + jnp.sum(dhr * hr, axis=-1, keepdims=True)) * (1.0 / MLA_QK)
    return rstd * (dhn - hn * mt), rstd * (dhr - 0.5 * hr * mt), dgn, dgr


def _diag_bias():
    i = jnp.arange(ATT_TQ)
    return jnp.where((i[None, :] // CHUNK) <= (i[:, None] // CHUNK), 0.0, MASK_VALUE).astype(F32)


def _store_pair(dst, rows, n, r2):
    dst[rows, :MLA_NOPE] = n.astype(BF16)
    dst[rows, MLA_NOPE:] = r2.astype(BF16)


def _mla_fwd(q_raw, kv, kr, gains, tabs, B, S, comm=None):
    T = B * S
    TQ = ATT_TQ
    nQ = S // TQ
    qgn, qgr, kgn, kgr = gains
    cos, sin = tabs
    scale = MLA_QK ** -0.5

    def body(q_ref, kv_ref, kr_ref, qgn_ref, qgr_ref, kgn_ref, kgr_ref, c_ref, s_ref, bias_ref,
             o_ref, lse_ref, qf_s, kf_s, v_s):
        def prep(t, _):
            rows = pl.ds(pl.multiple_of(t * TQ, TQ), TQ)
            cs, sn = c_ref[rows, :], s_ref[rows, :]
            qn, qr = _head_norm_rope(q_ref[rows, :MLA_NOPE], q_ref[rows, MLA_NOPE:], qgn_ref[...], qgr_ref[...],
                                     cs, sn, scale)
            _store_pair(qf_s, rows, qn, qr)
            kn, krr = _head_norm_rope(kv_ref[rows, :MLA_NOPE], kr_ref[rows, :], kgn_ref[...], kgr_ref[...], cs, sn, 1.0)
            _store_pair(kf_s, rows, kn, krr)
            v_s[rows, :] = kv_ref[rows, MLA_NOPE:].astype(BF16)
            return 0

        lax.fori_loop(0, nQ, prep, 0, unroll=2)
        for i in range(nQ):
            rows = slice(i * TQ, (i + 1) * TQ)
            q = qf_s[rows, :]
            sd = _dot(q, kf_s[rows, :], NT) + bias_ref[...]
            m = jnp.max(sd, axis=-1, keepdims=True)
            if i:
                sl = _dot(q, kf_s[:i * TQ, :], NT)
                m = jnp.maximum(m, jnp.max(sl, axis=-1, keepdims=True))
            pd = jnp.exp(sd - m)
            l = jnp.sum(pd, axis=-1, keepdims=True)
            acc = _dot(pd.astype(BF16), v_s[rows, :], NN)
            if i:
                pl_ = jnp.exp(sl - m)
                l = l + jnp.sum(pl_, axis=-1, keepdims=True)
                acc = acc + _dot(pl_.astype(BF16), v_s[:i * TQ, :], NN)
            o_ref[rows, :] = (acc / l).astype(BF16)
            lse_ref[rows, :] = m + jnp.log(l)

    def vec(n):
        return _bs((1, n), lambda b, h: (0, 0))

    def cols(n):
        return _bs((S, n), lambda b, h: (b, h))

    tab = _bs((S, MLA_R2), lambda b, h: (0, 0))
    return _pcall(
        body, name="mla_fwd", grid=(B, MLA_H),
        in_specs=[cols(MLA_PAD), cols(MLA_NOPE + MLA_V), _bs((S, MLA_R2), lambda b, h: (b, 0)),
                  vec(MLA_NOPE), vec(MLA_R2), vec(MLA_NOPE), vec(MLA_R2), tab, tab,
                  _bs((TQ, TQ), lambda b, h: (0, 0))],
        out_specs=[cols(MLA_V), _bs((None, S, 1), lambda b, h: (h, b, 0)), cols(MLA_PAD), cols(MLA_PAD)],
        out_shape=[jax.ShapeDtypeStruct((T, MLA_H * MLA_V), BF16), jax.ShapeDtypeStruct((MLA_H, T, 1), F32),
                   jax.ShapeDtypeStruct((T, MLA_H * MLA_PAD), BF16), jax.ShapeDtypeStruct((T, MLA_H * MLA_PAD), BF16)],
        scratch_shapes=[pltpu.VMEM((S, MLA_V), BF16)],
        sem=("parallel", "parallel"), args=(q_raw, kv, kr, qgn, qgr, kgn, kgr, cos, sin, _diag_bias()), comm=comm)


def _mla_bwd(q_raw, kv, kr, o, lse, do, qf, kf, gains, tabs, B, S, comm=None):
    T = B * S
    TQ = ATT_TQ
    nQ = S // TQ
    qgn, qgr, kgn, kgr = gains
    cos, sin = tabs
    scale = MLA_QK ** -0.5

    def body(q_ref, kv_ref, kr_ref, o_ref, lse_ref, do_ref, qf_s, kf_s, qgn_ref, qgr_ref, kgn_ref, kgr_ref, c_ref, s_ref,
             bias_ref, dq_ref, dkv_ref, dkr_ref, dqgn_ref, dqgr_ref, dkgn_ref, dkgr_ref,
             v_s, dl_s, dq_s, dk_s, dv_s):
        b, h = pl.program_id(0), pl.program_id(1)

        def blk(t):
            return pl.ds(pl.multiple_of(t * TQ, TQ), TQ)

        def prep(t, _):
            rows = blk(t)
            v_s[rows, :] = kv_ref[rows, MLA_NOPE:].astype(BF16)
            dl_s[rows, :] = jnp.sum(do_ref[rows, :].astype(F32) * o_ref[rows, :].astype(F32), axis=-1, keepdims=True)
            dk_s[rows, :] = jnp.zeros((TQ, MLA_PAD), F32)
            dv_s[rows, :] = jnp.zeros((TQ, MLA_V), F32)
            return 0

        lax.fori_loop(0, nQ, prep, 0, unroll=2)

        gqn, gqr = jnp.zeros((1, MLA_NOPE), F32), jnp.zeros((1, MLA_R2), F32)
        for i in range(nQ):
            rows = slice(i * TQ, (i + 1) * TQ)
            q, doi, lse_i, dl_i = qf_s[rows, :], do_ref[rows, :], lse_ref[rows, :], dl_s[rows, :]

            def part(cols, bias):
                k, v = kf_s[cols, :], v_s[cols, :]
                s = _dot(q, k, NT)
                if bias is not None:
                    s = s + bias
                p = jnp.exp(s - lse_i)
                ds = (p * (_dot(doi, v, NT) - dl_i)).astype(BF16)
                dk_s[cols, :] += _dot(ds, q, TN)
                dv_s[cols, :] += _dot(p.astype(BF16), doi, TN)
                return _dot(ds, k, NN)

            dq = part(rows, bias_ref[...])
            if i:
                dq = dq + part(slice(0, i * TQ), None)
            dq_s[...] = dq
            dqn, dqr, a0, a1 = _head_norm_rope_bwd(dq_s[:, :MLA_NOPE], dq_s[:, MLA_NOPE:], q_ref[rows, :MLA_NOPE],
                                                   q_ref[rows, MLA_NOPE:], qgn_ref[...], qgr_ref[...],
                                                   c_ref[rows, :], s_ref[rows, :], scale)
            _store_pair(dq_ref, rows, dqn, dqr)
            gqn, gqr = gqn + a0, gqr + a1

        def post(t, carry):
            rows = blk(t)
            dkn, dkr, a2, a3 = _head_norm_rope_bwd(dk_s[rows, :MLA_NOPE], dk_s[rows, MLA_NOPE:],
                                                   kv_ref[rows, :MLA_NOPE], kr_ref[rows, :], kgn_ref[...], kgr_ref[...],
                                                   c_ref[rows, :], s_ref[rows, :], 1.0)
            dkv_ref[rows, :MLA_NOPE] = dkn.astype(BF16)
            dkv_ref[rows, MLA_NOPE:] = dv_s[rows, :].astype(BF16)

            @pl.when(h == 0)
            def _():
                dkr_ref[rows, :] = dkr

            @pl.when(h > 0)
            def _():
                dkr_ref[rows, :] += dkr

            return carry[0] + a2, carry[1] + a3

        gkn, gkr = lax.fori_loop(0, nQ, post, (jnp.zeros((1, MLA_NOPE), F32), jnp.zeros((1, MLA_R2), F32)), unroll=2)
        first = (b == 0) & (h == 0)

        @pl.when(first)
        def _():
            dqgn_ref[...] = gqn
            dqgr_ref[...] = gqr
            dkgn_ref[...] = gkn
            dkgr_ref[...] = gkr

        @pl.when(jnp.logical_not(first))
        def _():
            dqgn_ref[...] += gqn
            dqgr_ref[...] += gqr
            dkgn_ref[...] += gkn
            dkgr_ref[...] += gkr

    def vec(n):
        return _bs((1, n), lambda b, h: (0, 0))

    def cols(n):
        return _bs((S, n), lambda b, h: (b, h))

    tab = _bs((S, MLA_R2), lambda b, h: (0, 0))
    return _pcall(
        body, name="mla_bwd", grid=(B, MLA_H),
        in_specs=[cols(MLA_PAD), cols(MLA_NOPE + MLA_V), _bs((S, MLA_R2), lambda b, h: (b, 0)), cols(MLA_V),
                  _bs((None, S, 1), lambda b, h: (h, b, 0)), cols(MLA_V), cols(MLA_PAD), cols(MLA_PAD),
                  vec(MLA_NOPE), vec(MLA_R2), vec(MLA_NOPE), vec(MLA_R2), tab, tab,
                  _bs((TQ, TQ), lambda b, h: (0, 0))],
        out_specs=[cols(MLA_PAD), cols(MLA_NOPE + MLA_V), _bs((S, MLA_R2), lambda b, h: (b, 0)),
                   vec(MLA_NOPE), vec(MLA_R2), vec(MLA_NOPE), vec(MLA_R2)],
        out_shape=[jax.ShapeDtypeStruct((T, MLA_H * MLA_PAD), BF16),
                   jax.ShapeDtypeStruct((T, MLA_H * (MLA_NOPE + MLA_V)), BF16),
                   jax.ShapeDtypeStruct((T, MLA_R2), F32), jax.ShapeDtypeStruct((1, MLA_NOPE), F32),
                   jax.ShapeDtypeStruct((1, MLA_R2), F32), jax.ShapeDtypeStruct((1, MLA_NOPE), F32),
                   jax.ShapeDtypeStruct((1, MLA_R2), F32)],
        scratch_shapes=[pltpu.VMEM((S, MLA_V), BF16), pltpu.VMEM((S, 1), F32), pltpu.VMEM((TQ, MLA_PAD), F32),
                        pltpu.VMEM((S, MLA_PAD), F32), pltpu.VMEM((S, MLA_V), F32)],
        sem=("arbitrary", "arbitrary"),
        args=(q_raw, kv, kr, o, lse, do, qf, kf, qgn, qgr, kgn, kgr, cos, sin, _diag_bias()), comm=comm)


def _adamw(name, recvs, w, m, v, tr=None, comm=None):
    n, R, C = recvs[0].shape
    L = len(recvs)
    Lw, Rw, _ = w.shape
    assert Lw * Rw == L * R and w.shape[2] == C
    tr = R if tr is None else tr
    assert R % tr == 0 and Rw % tr == 0
    per = R // tr
    per_w = Rw // tr
    c1 = 1.0 - ADAM_B1 ** ADAM_STEP
    c2 = 1.0 - ADAM_B2 ** ADAM_STEP

    def body(*refs):
        r_refs = refs[:L]
        w_ref, m_ref, v_ref, g_ref, d_ref, nm_ref, nv_ref = refs[L:]
        layer = pl.program_id(0) // per

        def total(r_ref):
            t = r_ref[0].astype(F32)
            for k in range(1, n):
                t = t + r_ref[k].astype(F32)
            return t

        g = total(r_refs[0])
        for l in range(1, L):
            g = jnp.where(layer == l, total(r_refs[l]), g)
        mm = ADAM_B1 * m_ref[...] + (1.0 - ADAM_B1) * g
        vv = ADAM_B2 * v_ref[...] + (1.0 - ADAM_B2) * (g * g)
        g_ref[...] = g
        nm_ref[...] = mm
        nv_ref[...] = vv
        d_ref[...] = -ADAM_LR * ((mm / c1) / (jnp.sqrt(vv / c2) + ADAM_EPS) + ADAM_WD * w_ref[...])

    blk = _bs((None, tr, C), lambda i: (i // per_w, i % per_w, 0))
    r_specs = [_bs((n, tr, C), functools.partial(lambda l, i: (0, jnp.clip(i - l * per, 0, per - 1), 0), l))
               for l in range(L)]
    outs, got = _pcall(body, name=name, grid=(L * per,), in_specs=r_specs + [blk, blk, blk], out_specs=[blk] * 4,
                       out_shape=[jax.ShapeDtypeStruct(w.shape, F32)] * 4, scratch_shapes=[], sem=("arbitrary",),
                       args=(*recvs, w, m, v), comm=comm)
    return outs if comm is None else (outs, got)


def _sum8(name, a):
    n, R, C = a.shape

    def body(a_ref, o_ref):
        s = a_ref[0]
        for k in range(1, n):
            s = s + a_ref[k]
        o_ref[...] = s

    return pl.pallas_call(body, name=name, out_shape=jax.ShapeDtypeStruct((R, C), a.dtype))(a)


def _sds(shape, dt):
    return jax.ShapeDtypeStruct(shape, dt)


def _norm_proj(name, x, g, w, o_spec, out_shape, tm=1024, comm=None):
    T, K = x.shape
    J, _, n = w.shape

    def body(x_ref, g_ref, w_ref, o_ref, h_ref, hs):
        @pl.when(pl.program_id(1) == 0)
        def _():
            xf = x_ref[...]
            r = lax.rsqrt(jnp.mean(xf * xf, axis=-1, keepdims=True) + RMS_EPS)
            h = (xf * r * g_ref[...]).astype(BF16)
            hs[...] = h
            h_ref[...] = h

        o_ref[...] = _dot(hs[...], w_ref[...], NN).astype(o_ref.dtype)

    row = _bs((tm, K), lambda m, j: (m, 0))
    (out, h), got = _pcall(
        body, name=name, grid=(T // tm, J),
        in_specs=[row, _bs((1, K), lambda m, j: (0, 0)), _bs((None, K, n), lambda m, j: (j, 0, 0))],
        out_specs=[o_spec, row], out_shape=[out_shape, _sds((T, K), BF16)], scratch_shapes=[pltpu.VMEM((tm, K), BF16)],
        sem=("parallel", "arbitrary"), args=(x, g, w), comm=comm)
    return out, h, got


def _proj_shared_dx(name, d, w, tm=1024, comm=None):
    J, T, n = d.shape
    K = w.shape[1]
    return _mm(name, d, w, grid=(T // tm, J), a_spec=_bs((None, tm, n), lambda m, k: (k, m, 0)),
               b_spec=_bs((None, K, n), lambda m, k: (k, 0, 0)), o_spec=_bs((tm, K), lambda m, k: (m, 0)),
               out_shape=_sds((T, K), F32), dims=NT, kax=1, acc_shape=(tm, K), comm=comm)


def _out_proj(name, a, w, res, tm=512):
    J, T, k = a.shape
    N = w.shape[2]
    return _mm(name, a, w, grid=(T // tm,), a_spec=_bs((J, tm, k), lambda m: (0, m, 0)),
               b_spec=_bs((J, k, N), lambda m: (0, 0, 0)), o_spec=_bs((tm, N), lambda m: (m, 0)),
               out_shape=_sds((T, N), F32), dims=NN, res=res, res_spec=_bs((tm, N), lambda m: (m, 0)), jb=J)


def _out_proj_dx(name, dx, w, tm=1024, comm=None):
    T, N = dx.shape
    J, k, _ = w.shape
    return _mm(name, dx, w, grid=(T // tm, J), a_spec=_bs((tm, N), lambda m, j: (m, 0)),
               b_spec=_bs((None, k, N), lambda m, j: (j, 0, 0)), o_spec=_bs((None, tm, k), lambda m, j: (j, m, 0)),
               out_shape=_sds((J, T, k), BF16), dims=NT, comm=comm)


def _out_proj_dw(name, a, dx, tt=1024, comm=None):
    J, T, k = a.shape
    N = dx.shape[1]
    return _mm(name, a, dx, grid=(J, T // tt), a_spec=_bs((None, tt, k), lambda j, t: (j, t, 0)),
               b_spec=_bs((tt, N), lambda j, t: (t, 0)), o_spec=_bs((None, k, N), lambda j, t: (j, 0, 0)),
               out_shape=_sds((J, k, N), BF16), dims=TN, kax=1, acc_shape=(k, N), comm=comm)


def _dense(name, a, b, dims, out_dtype, tm=512, res=None, comm=None):
    if dims == TN:
        T, K = a.shape
        N = b.shape[1]
        return _mm(name, a, b, grid=(T // tm,), a_spec=_bs((tm, K), lambda t: (t, 0)),
                   b_spec=_bs((tm, N), lambda t: (t, 0)), o_spec=_bs((K, N), lambda t: (0, 0)),
                   out_shape=_sds((K, N), out_dtype), dims=TN, kax=0, acc_shape=(K, N), comm=comm)
    M, K = a.shape
    N = b.shape[1] if dims == NN else b.shape[0]
    row = _bs((tm, N), lambda m: (m, 0))
    return _mm(name, a, b, grid=(M // tm,), a_spec=_bs((tm, K), lambda m: (m, 0)), b_spec=_bs(b.shape, lambda m: (0, 0)),
               o_spec=row, out_shape=_sds((M, N), out_dtype), dims=dims, res=res,
               res_spec=row if res is not None else None, comm=comm)


def _bf16(x):
    return x.astype(BF16)


def _ffn_fwd(i, x, norm_g, w_in, cw, cb, w_out, B, S, comm_in=None):
    T = x.shape[0]
    u, h, got = _norm_proj(f"ffn{i}_in", x, norm_g, w_in, _bs((None, 1024, FSH), lambda m, j: (j, m, 0)),
                           _sds((NDEV, T, FSH), BF16), comm=comm_in)
    u4 = u.reshape(2, 4, T, FSH)
    gt = _convffn_fwd(f"ffn{i}_gate", u4, cw, cb, B, S)
    y = _out_proj(f"ffn{i}_out", gt, w_out, x)
    return y, (x, h, u4, gt), got


def _ffn_bwd(i, dy, dyb, saved, norm_g, w_in, cw, cb, w_out, B, S, first_half_early):
    x, h, u4, gt = saved
    dgt = _out_proj_dx(f"ffn{i}_out_dx", dyb, w_out)
    dw_out = _out_proj_dw(f"ffn{i}_out_dw", gt, dyb).reshape(NDEV, FSH // 2, D_MODEL)
    (du4, dcw, dcb), (r_out,) = _convffn_bwd(f"ffn{i}_gate_bwd", u4, cw, cb, dgt, B, S, comm=_Exchange([dw_out]))
    du = du4.reshape(NDEV, du4.shape[2], FSH)
    dw_in = _out_proj_dw(f"ffn{i}_in_dw", du, h)
    r_in = None
    if first_half_early:
        dh, (r_in,) = _proj_shared_dx(f"ffn{i}_in_dx", du, w_in, comm=_Exchange([dw_in], rows=[(0, FSH // 2)]))
    else:
        dh = _proj_shared_dx(f"ffn{i}_in_dx", du, w_in)
    dx, dgn, dxb = _rms_bwd(f"ffn{i}_norm_bwd", x, norm_g, dh, dres=dy, also_bf16=True)
    return dx, dxb, dict(w_in=dw_in, norm=dgn, cw=dcw, cb=dcb), r_out, r_in


def kernel(x, ret_norm, ret_w_in, ret_gn, ret_w_out, mla_norm, mla_w_in, mla_q_norm, mla_w_qb, mla_kv_norm, mla_w_kvb, mla_q_head_norm, mla_k_head_norm, mla_w_out, ffn_norm, ffn_w_in, ffn_conv_w, ffn_conv_b, ffn_w_out, loss_target, m_ret_norm, m_ret_w_in, m_ret_gn, m_ret_w_out, m_mla_norm, m_mla_w_in, m_mla_q_norm, m_mla_w_qb, m_mla_kv_norm, m_mla_w_kvb, m_mla_q_head_norm, m_mla_k_head_norm, m_mla_w_out, m_ffn_norm, m_ffn_w_in, m_ffn_conv_w, m_ffn_conv_b, m_ffn_w_out, v_ret_norm, v_ret_w_in, v_ret_gn, v_ret_w_out, v_mla_norm, v_mla_w_in, v_mla_q_norm, v_mla_w_qb, v_mla_kv_norm, v_mla_w_kvb, v_mla_q_head_norm, v_mla_k_head_norm, v_mla_w_out, v_ffn_norm, v_ffn_w_in, v_ffn_conv_w, v_ffn_conv_b, v_ffn_w_out):
    B, S, D = x.shape
    T = B * S
    w = dict(ret_norm=ret_norm, ret_w_in=ret_w_in, ret_gn=ret_gn, ret_w_out=ret_w_out, mla_norm=mla_norm,
             mla_w_in=mla_w_in, mla_q_norm=mla_q_norm, mla_w_qb=mla_w_qb, mla_kv_norm=mla_kv_norm, mla_w_kvb=mla_w_kvb,
             mla_q_head_norm=mla_q_head_norm, mla_k_head_norm=mla_k_head_norm, mla_w_out=mla_w_out, ffn_norm=ffn_norm,
             ffn_w_in=ffn_w_in, ffn_conv_w=ffn_conv_w, ffn_conv_b=ffn_conv_b, ffn_w_out=ffn_w_out)
    mom = dict(ret_norm=m_ret_norm, ret_w_in=m_ret_w_in, ret_gn=m_ret_gn, ret_w_out=m_ret_w_out, mla_norm=m_mla_norm,
               mla_w_in=m_mla_w_in, mla_q_norm=m_mla_q_norm, mla_w_qb=m_mla_w_qb, mla_kv_norm=m_mla_kv_norm,
               mla_w_kvb=m_mla_w_kvb, mla_q_head_norm=m_mla_q_head_norm, mla_k_head_norm=m_mla_k_head_norm,
               mla_w_out=m_mla_w_out, ffn_norm=m_ffn_norm, ffn_w_in=m_ffn_w_in, ffn_conv_w=m_ffn_conv_w,
               ffn_conv_b=m_ffn_conv_b, ffn_w_out=m_ffn_w_out)
    var = dict(ret_norm=v_ret_norm, ret_w_in=v_ret_w_in, ret_gn=v_ret_gn, ret_w_out=v_ret_w_out, mla_norm=v_mla_norm,
               mla_w_in=v_mla_w_in, mla_q_norm=v_mla_q_norm, mla_w_qb=v_mla_w_qb, mla_kv_norm=v_mla_kv_norm,
               mla_w_kvb=v_mla_w_kvb, mla_q_head_norm=v_mla_q_head_norm, mla_k_head_norm=v_mla_k_head_norm,
               mla_w_out=v_mla_w_out, ffn_norm=v_ffn_norm, ffn_w_in=v_ffn_w_in, ffn_conv_w=v_ffn_conv_w,
               ffn_conv_b=v_ffn_conv_b, ffn_w_out=v_ffn_w_out)
    BIG = ["ret_w_in", "ret_w_out", "mla_w_in", "mla_w_qb", "mla_w_kvb", "mla_w_out", "ffn_w_in", "ffn_w_out"]
    REPL = ["ret_norm", "ffn_norm", "mla_q_head_norm", "mla_k_head_norm", "ffn_conv_b"]
    SHARDED_SMALL = ["ffn_conv_w", "ret_gn", "mla_norm", "mla_q_norm", "mla_kv_norm"]
    dev = _idx(_place())

    def blk16(k, i=0):
        return _bf16(w[k][i])

    small_vec = jnp.concatenate([w[k].reshape(-1) for k in SHARDED_SMALL])
    n_small = small_vec.shape[0]
    small_vec = jnp.pad(small_vec, (0, 4096 - n_small)).reshape(32, 128)
    Wret_in, sg = _comm_call("gather_ret_w_in", _Gather([blk16("ret_w_in"), small_vec], parts=2))
    sg = sg.reshape(NDEV, 4096)
    o0 = 0
    conv_w_full = sg[:, o0:o0 + 2112].reshape(NDEV, 2, 3, 352).transpose(1, 2, 0, 3).reshape(2, 3, FFN)
    o0 += 2112
    ret_gn_full = sg[:, o0:o0 + 256].reshape(NDEV, RET_H, 64).transpose(1, 0, 2).reshape(RET_H, 1, RET_DV)
    o0 += 256
    mla_norm_full = sg[:, o0:o0 + 128].reshape(1, D)
    o0 += 128
    q_norm_full = sg[:, o0:o0 + 48].reshape(1, MLA_QR)
    o0 += 48
    kv_norm_full = sg[:, o0:o0 + 32].reshape(1, MLA_KVR)

    cw = [conv_w_full[i].reshape(3, 4, FSH).transpose(1, 0, 2) for i in range(2)]
    cb = [ffn_conv_b[i].reshape(4, 1, FSH) for i in range(2)]
    fnorm = [ffn_norm[i].reshape(1, D) for i in range(2)]
    rtabs = _ret_tables(S)
    mtabs = _mla_tables(S)
    qh, kh = mla_q_head_norm.reshape(1, MLA_QK), mla_k_head_norm.reshape(1, MLA_QK)
    gains = (qh[:, :MLA_NOPE], _dup(qh[:, MLA_NOPE:]), kh[:, :MLA_NOPE], _dup(kh[:, MLA_NOPE:]))

    x0 = x.reshape(T, D)
    tgt = loss_target.reshape(T, D)
    proj, h0, (Wret_out, Wffn_out0) = _norm_proj(
        "ret_in", x0, ret_norm.reshape(1, D), Wret_in, _bs((1024, 768), lambda m, j: (m, j)), _sds((T, 6144), BF16),
        comm=_Gather([blk16("ret_w_out"), blk16("ffn_w_out", 0)]))
    Wret_out = Wret_out.reshape(RET_H * RET_DV, D)
    Wffn_out0 = Wffn_out0.reshape(4, FSH, D)
    (o_raw, rgt, states), (Wffn_in0,) = _ret_fwd(proj, rtabs, ret_gn_full, B, S, comm=_Gather([blk16("ffn_w_in", 0)]))
    x1 = _dense("ret_out", rgt, Wret_out, NN, F32, res=x0)
    MLA_W = ["mla_w_in", "mla_w_qb", "mla_w_kvb", "mla_w_out"]
    x2, ffn0_saved, got = _ffn_fwd(0, x1, fnorm[0], Wffn_in0, cw[0], cb[0], Wffn_out0, B, S,
                                   comm_in=_Gather([blk16(k) for k in MLA_W]))
    Wmla_in = got[0].reshape(D, MLA_QR + MLA_KVR + MLA_ROPE)
    Wq, Wkv, Wkr = Wmla_in[:, :MLA_QR], Wmla_in[:, MLA_QR:MLA_QR + MLA_KVR], Wmla_in[:, MLA_QR + MLA_KVR:]
    Wqb, Wkvb, Wmla_out = got[1:]

    h2 = _rms_fwd("mla_norm", x2, mla_norm_full)

    c_q, c_kv, k_rope = (_dense(n, h2, wm, NN, F32) for n, wm in
                         (("mla_in_q", Wq), ("mla_in_kv", Wkv), ("mla_in_kr", _dup(Wkr))))
    cqn = _rms_fwd("mla_q_norm", c_q, q_norm_full)
    ckvn = _rms_fwd("mla_kv_norm", c_kv, kv_norm_full)
    Wqb2 = jnp.concatenate([Wqb, Wqb[:, :, MLA_NOPE:]], axis=2).transpose(1, 0, 2).reshape(MLA_QR, MLA_H * MLA_PAD)
    Wkvb2 = Wkvb.transpose(1, 0, 2).reshape(MLA_KVR, MLA_H * (MLA_NOPE + MLA_V))
    Wmla_out2 = Wmla_out.reshape(D, D)
    q_raw = _dense("mla_qb", cqn, Wqb2, NN, F32, tm=1024)
    kvh = _dense("mla_kvb", ckvn, Wkvb2, NN, F32, tm=1024)
    (att, lse, qf, kf), (Wffn_in1, Wffn_out1) = _mla_fwd(
        q_raw, kvh, k_rope, gains, mtabs, B, S, comm=_Gather([blk16("ffn_w_in", 1), blk16("ffn_w_out", 1)]))
    Wffn_out1 = Wffn_out1.reshape(4, FSH, D)
    x3 = _dense("mla_out", att, Wmla_out2, NN, F32, res=x2)
    y, ffn1_saved, _ = _ffn_fwd(1, x3, fnorm[1], Wffn_in1, cw[1], cb[1], Wffn_out1, B, S)

    dy, colsq, dyb = _loss(y, tgt)
    loss = lax.psum(0.5 * jnp.sum(colsq) / D, ("x", "y", "c"))

    dx3, dx3b, gf1, r_ffn1_out, _ = _ffn_bwd(1, dy, dyb, ffn1_saved, fnorm[1], Wffn_in1, cw[1], cb[1], Wffn_out1, B, S,
                                             first_half_early=False)
    datt = _dense("mla_out_dx", dx3b, Wmla_out2, NT, BF16)
    fh = FSH // 2
    (dq_raw, dkvh, dkr, dqgn, dqgr, dkgn, dkgr), (r_ffn1_in_a, r_ffn1_in_b) = _mla_bwd(
        q_raw, kvh, k_rope, att, lse, datt, qf, kf, gains, mtabs, B, S,
        comm=_Exchange([gf1["w_in"], gf1["w_in"]], rows=[(0, fh), (fh, fh)]))
    dcqn = _dense("mla_qb_dx", dq_raw, Wqb2, NT, F32, tm=1024)
    dckvn = _dense("mla_kvb_dx", dkvh, Wkvb2, NT, F32, tm=1024)
    dcq, dg_qn = _rms_bwd("mla_q_norm_bwd", c_q, q_norm_full, dcqn)
    dckv, dg_kvn = _rms_bwd("mla_kv_norm_bwd", c_kv, kv_norm_full, dckvn)
    dqgr, dkgr = _fold(dqgr), _fold(dkgr)
    dproj2 = _bf16(jnp.concatenate([dcq, dckv, _fold(dkr)], axis=-1))
    dh2 = _dense("mla_in_dx", dproj2, Wmla_in, NT, F32)
    dx2, dg_mla_norm, dx2b = _rms_bwd("mla_norm_bwd", x2, mla_norm_full, dh2, dres=dx3, also_bf16=True)

    dx1, dx1b, gf0, r_ffn0_out, r_ffn0_in_a = _ffn_bwd(0, dx2, dx2b, ffn0_saved, fnorm[0], Wffn_in0, cw[0], cb[0],
                                                       Wffn_out0, B, S, first_half_early=True)
    drgt = _dense("ret_out_dx", dx1b, Wret_out, NT, BF16)
    dWret_out = _dense("ret_out_dw", rgt, dx1b, TN, BF16, tm=1024).reshape(NDEV, 256, D)
    (dproj, dgn_ret), (r_ffn0_in_b,) = _ret_bwd(proj, o_raw, states, drgt, rtabs, ret_gn_full, B, S,
                                                comm=_Exchange([gf0["w_in"]], rows=[(fh, fh)]))
    dWret_in, (r_ret_out,) = _mm(
        "ret_in_dw", h0, dproj, grid=(NDEV, T // 1024), a_spec=_bs((1024, D), lambda j, t: (t, 0)),
        b_spec=_bs((1024, 768), lambda j, t: (t, j)), o_spec=_bs((None, D, 768), lambda j, t: (j, 0, 0)),
        out_shape=_sds((NDEV, D, 768), BF16), dims=TN, kax=1, acc_shape=(D, 768), comm=_Exchange([dWret_out]))
    qr = D // 4
    dh0, r_ret_in_ab = _mm(
        "ret_in_dx", dproj, Wret_in, grid=(T // 1024, NDEV), a_spec=_bs((1024, 768), lambda m, k: (m, k)),
        b_spec=_bs((None, D, 768), lambda m, k: (k, 0, 0)), o_spec=_bs((1024, D), lambda m, k: (m, 0)),
        out_shape=_sds((T, D), F32), dims=NT, kax=1, acc_shape=(1024, D),
        comm=_Exchange([dWret_in, dWret_in], rows=[(0, qr), (qr, qr)]))
    (dx0, dg_ret_norm), (r_ret_in_c,) = _rms_bwd("ret_norm_bwd", x0, ret_norm.reshape(1, D), dh0, dres=dx1,
                                                 comm=_Exchange([dWret_in], rows=[(2 * qr, qr)]))
    grad_x = dx0.reshape(B, S, D)
    dWmla_out, (r_ret_in_d,) = _dense("mla_out_dw", att, dx3b, TN, BF16, tm=1024,
                                      comm=_Exchange([dWret_in], rows=[(3 * qr, qr)]))
    dWmla_out = dWmla_out.reshape(NDEV, MLA_V, D)
    dWqb, (r_mla_out,) = _dense("mla_qb_dw", cqn, dq_raw, TN, BF16, tm=1024, comm=_Exchange([dWmla_out]))
    dWqb = dWqb.reshape(MLA_QR, MLA_H, MLA_PAD)
    dWqb = jnp.concatenate([dWqb[:, :, :MLA_NOPE], _fold(dWqb[:, :, MLA_NOPE:])], axis=2).transpose(1, 0, 2)
    dWkvb, (r_mla_qb,) = _dense("mla_kvb_dw", ckvn, dkvh, TN, BF16, tm=1024, comm=_Exchange([dWqb]))
    dWkvb = dWkvb.reshape(MLA_KVR, MLA_H, MLA_NOPE + MLA_V).transpose(1, 0, 2)
    dWmla_in, (r_mla_kvb,) = _dense("mla_in_dw", h2, dproj2, TN, BF16, comm=_Exchange([dWkvb]))
    dWmla_in = dWmla_in.reshape(NDEV, 128, 704)
    received = dict(ret_w_in=[*r_ret_in_ab, r_ret_in_c, r_ret_in_d], ret_w_out=[r_ret_out], mla_w_qb=[r_mla_qb],
                    mla_w_kvb=[r_mla_kvb], mla_w_out=[r_mla_out],
                    ffn_w_in=[r_ffn0_in_a, r_ffn0_in_b, r_ffn1_in_a, r_ffn1_in_b], ffn_w_out=[r_ffn0_out, r_ffn1_out])

    dconv_w = jnp.stack([g_["cw"].transpose(1, 0, 2).reshape(3, FFN) for g_ in (gf0, gf1)])
    dconv_b = jnp.stack([g_["cb"].reshape(FFN) for g_ in (gf0, gf1)])
    small_parts = [dg_ret_norm, gf0["norm"], gf1["norm"], dg_mla_norm, dg_qn, dg_kvn, dqgn, dqgr, dkgn, dkgr, dgn_ret,
                   dconv_w, dconv_b]
    small_g = jnp.concatenate([p.reshape(-1) for p in small_parts]).reshape(232, 128)
    small_all, received["mla_w_in"] = _comm_call("gather_small_grads", _Both(_Gather([small_g]), _Exchange([dWmla_in])))
    received["mla_w_in"] = [received["mla_w_in"]]
    sred = _sum8("sum_small_grads", small_all).reshape(-1)

    def take(n):
        nonlocal off
        out = sred[off:off + n]
        off += n
        return out

    off = 0
    g_small = dict(ret_norm=take(D).reshape(1, D), ffn_norm=take(2 * D).reshape(2, D), mla_norm=take(D),
                   mla_q_norm=take(MLA_QR), mla_kv_norm=take(MLA_KVR))
    g_small["mla_q_head_norm"] = take(MLA_QK).reshape(1, MLA_QK)
    g_small["mla_k_head_norm"] = take(MLA_QK).reshape(1, MLA_QK)
    g_small["ret_gn"] = take(RET_H * RET_DV).reshape(1, RET_H, RET_DV)
    g_small["ffn_conv_w"] = take(2 * 3 * FFN).reshape(2, 3, FFN)
    g_small["ffn_conv_b"] = take(2 * FFN).reshape(2, FFN)
    g_small["mla_norm"] = lax.dynamic_slice(g_small["mla_norm"], (dev * 128,), (128,)).reshape(1, 128)
    g_small["mla_q_norm"] = lax.dynamic_slice(g_small["mla_q_norm"], (dev * 48,), (48,)).reshape(1, 48)
    g_small["mla_kv_norm"] = lax.dynamic_slice(g_small["mla_kv_norm"], (dev * 32,), (32,)).reshape(1, 32)
    g_small["ret_gn"] = lax.dynamic_slice(g_small["ret_gn"], (0, 0, dev * 64), (1, RET_H, 64))
    g_small["ffn_conv_w"] = lax.dynamic_slice(g_small["ffn_conv_w"], (0, 0, dev * 352), (2, 3, 352))

    grads, delta, new_m, new_v = {}, {}, {}, {}
    for k in BIG:
        rcs = received[k]
        tr = max(t for t in range(16, 257, 16) if rcs[0].shape[1] % t == 0)
        flip = (lambda t: t.transpose(0, 2, 1)) if k == "ffn_w_in" else (lambda t: t)
        res = _adamw(f"adamw_{k}", rcs, flip(w[k]), flip(mom[k]), flip(var[k]), tr=tr)
        grads[k], delta[k], new_m[k], new_v[k] = (flip(t) for t in res)
    SMALL = REPL + SHARDED_SMALL

    def pack(d):
        vflat = jnp.concatenate([d[k].reshape(-1) for k in SMALL])
        return jnp.pad(vflat, (0, 96 * 128 - vflat.shape[0])).reshape(1, 96, 128)

    ps = _adamw("adamw_small", [pack(g_small)], pack(w), pack(mom), pack(var))
    off = 0
    for k in SMALL:
        n = w[k].size
        grads[k], delta[k], new_m[k], new_v[k] = (t.reshape(-1)[off:off + n].reshape(w[k].shape) for t in ps)
        off += n
    names = list(w)
    return (loss, grad_x, *[grads[k] for k in names], *[delta[k] for k in names], *[new_m[k] for k in names],
            *[new_v[k] for k in names])
```

```python
import functools

import jax
import jax.numpy as jnp
from jax import lax
from jax.experimental import pallas as pl
from jax.experimental.pallas import tpu as pltpu

F32, BF16 = jnp.float32, jnp.bfloat16

NDEV = 8
D_MODEL = 1024
CHUNK = 64
RMS_EPS = 1e-6
ROPE_THETA = 10000.0
RET_H, RET_DK, RET_DV = 4, 256, 512
RET_SC = 256
MLA_H, MLA_QR, MLA_KVR = 8, 384, 256
MLA_NOPE, MLA_ROPE, MLA_V = 128, 64, 128
MLA_QK = MLA_NOPE + MLA_ROPE
MASK_VALUE = -1e30
FFN = 2816
FSH = FFN * 2 // NDEV
ATT_TQ = 256
ADAM_LR, ADAM_B1, ADAM_B2, ADAM_EPS, ADAM_WD, ADAM_STEP = 0.001, 0.9, 0.999, 1e-08, 0.01, 10
MESH = pl.DeviceIdType.MESH
VMEM_LIMIT = 56 * 2 ** 20


def _cp(sem):
    return pltpu.CompilerParams(dimension_semantics=sem, vmem_limit_bytes=VMEM_LIMIT)


def _dot(a, b, dims):
    return lax.dot_general(a, b, (dims, ((), ())), preferred_element_type=F32)


NN = ((1,), (0,))
NT = ((1,), (1,))
TN = ((0,), (0,))


def _place():
    return lax.axis_index("x"), lax.axis_index("y"), lax.axis_index("c")


def _idx(d):
    return 4 * d[0] + 2 * d[1] + d[2]


ANY = pl.BlockSpec(memory_space=pl.ANY)


class _Gather:
    sem0 = 0

    def __init__(self, arrs, parts=1):
        self.srcs = list(arrs)
        self.parts = parts
        self.nsem = len(arrs) * parts
        self.out_shape = [jax.ShapeDtypeStruct((NDEV,) + a.shape, a.dtype) for a in arrs]

    def _copies(self, ins, outs, send, recv, loc):
        n = self.nsem
        s0 = self.sem0
        x, y, c = _place()
        me, sib = (x, y, c), (x, y, 1 - c)
        chips = [(1 - x, y), (x, 1 - y), (1 - x, 1 - y)]

        def piece(ref, v, *lead):
            a, p = divmod(v, self.parts)
            if self.parts > 1:
                rows = self.srcs[a].shape[0] // self.parts
                lead = (*lead, pl.ds(p * rows, rows))
            return ref[a].at[lead] if lead else ref[a]

        def cp(a, k, block, to, src=None):
            dst = piece(outs, a, _idx(block))
            return pltpu.make_async_remote_copy(src_ref=dst if src is None else src, dst_ref=dst,
                                                send_sem=send.at[s0 + a, k], recv_sem=recv.at[s0 + a, k], device_id=to,
                                                device_id_type=MESH)

        own = [piece(ins, a) for a in range(n)]
        mine = [pltpu.make_async_copy(own[a], piece(outs, a, _idx(me)), loc.at[s0 + a]) for a in range(n)]
        first = [cp(a, 0, me, sib, src=own[a]) for a in range(n)]
        first += [cp(a, 1 + j, me, (*chip, c), src=own[a]) for a in range(n) for j, chip in enumerate(chips)]
        landed = [cp(a, 1 + j, (*chip, c), me) for j, chip in enumerate(chips) for a in range(n)]
        passed = [cp(a, 4 + j, (*chip, c), sib) for j, chip in enumerate(chips) for a in range(n)]
        from_sib = [cp(a, 0, sib, me) for a in range(n)]
        from_sib += [cp(a, 4 + j, (*chip, 1 - c), me) for j, chip in enumerate(chips) for a in range(n)]
        return mine, first, landed, passed, from_sib

    def start(self, *refs):
        mine, first, _, _, _ = self._copies(*refs)
        for cp in mine + first:
            cp.start()

    def mid(self, *refs):
        _, _, landed, passed, _ = self._copies(*refs)
        for got, on in zip(landed, passed):
            got.wait_recv()
            on.start()

    def finish(self, *refs):
        mine, first, _, passed, from_sib = self._copies(*refs)
        for cp in from_sib:
            cp.wait_recv()
        for cp in first + passed:
            cp.wait_send()
        for cp in mine:
            cp.wait()


class _Exchange:
    sem0 = 0

    def __init__(self, arrs, rows=None):
        self.srcs = list(arrs)
        self.nsem = len(arrs)
        self.rows = rows if rows is not None else [None] * len(arrs)
        self.out_shape = [jax.ShapeDtypeStruct(a.shape if r is None else (a.shape[0], r[1]) + a.shape[2:], a.dtype)
                          for a, r in zip(arrs, self.rows)]

    def _copies(self, ins, outs, send, recv, loc):
        n = len(self.srcs)
        s0 = self.sem0
        x, y, c = _place()
        me = _idx((x, y, c))

        def src(a, q):
            r = self.rows[a]
            return ins[a].at[q] if r is None else ins[a].at[q, pl.ds(r[0], r[1])]

        mine = [pltpu.make_async_copy(src(a, me), outs[a].at[me], loc.at[s0 + a]) for a in range(n)]
        remote = []
        for k in range(1, NDEV):
            peer = (x ^ (k >> 2), y ^ ((k >> 1) & 1), c ^ (k & 1))
            remote += [pltpu.make_async_remote_copy(
                src_ref=src(a, _idx(peer)), dst_ref=outs[a].at[me], send_sem=send.at[s0 + a, k - 1],
                recv_sem=recv.at[s0 + a, k - 1], device_id=peer, device_id_type=MESH) for a in range(n)]
        return mine, remote

    def start(self, *refs):
        mine, remote = self._copies(*refs)
        for cp in mine + remote:
            cp.start()

    def mid(self, *refs):
        pass

    def finish(self, *refs):
        mine, remote = self._copies(*refs)
        for cp in remote + mine:
            cp.wait()


class _Both:
    def __init__(self, one, two):
        self.parts = (one, two)
        two.sem0 = one.nsem
        self.nsem = one.nsem + two.nsem
        self.srcs = one.srcs + two.srcs
        self.out_shape = one.out_shape + two.out_shape

    def _each(self, phase, ins, outs, send, recv, loc):
        n = len(self.parts[0].srcs)
        getattr(self.parts[0], phase)(ins[:n], outs[:n], send, recv, loc)
        getattr(self.parts[1], phase)(ins[n:], outs[n:], send, recv, loc)

    def start(self, *refs):
        self._each("start", *refs)

    def mid(self, *refs):
        self._each("mid", *refs)

    def finish(self, *refs):
        self._each("finish", *refs)


def _comm_scratch(n):
    return [pltpu.SemaphoreType.DMA((n, 7)), pltpu.SemaphoreType.DMA((n, 7)), pltpu.SemaphoreType.DMA((n,))]


def _comm_call(name, comm):
    n = len(comm.srcs)

    def body(*refs):
        parts = (refs[:n], refs[n:2 * n]) + tuple(refs[2 * n:])
        comm.start(*parts)
        comm.mid(*parts)
        comm.finish(*parts)

    return pl.pallas_call(body, name=name, in_specs=[ANY] * n, out_specs=[ANY] * n, out_shape=comm.out_shape,
                          scratch_shapes=_comm_scratch(comm.nsem))(*comm.srcs)


def _pcall(body, *, name, grid, in_specs, out_specs, out_shape, scratch_shapes, sem, args, comm=None):
    if comm is None:
        return pl.pallas_call(body, name=name, grid=grid, in_specs=in_specs, out_specs=out_specs, out_shape=out_shape,
                              scratch_shapes=scratch_shapes, compiler_params=_cp(sem))(*args), None
    ni, no, ns, nc = len(in_specs), len(out_shape), len(scratch_shapes), len(comm.srcs)
    total = 1
    for g in grid:
        total *= g
    middle = (4 * total) // 5

    def wrapped(*refs):
        ins, csrc = refs[:ni], refs[ni:ni + nc]
        outs, cdst = refs[ni + nc:ni + nc + no], refs[ni + nc + no:ni + 2 * nc + no]
        scr, sems = refs[ni + 2 * nc + no:ni + 2 * nc + no + ns], refs[ni + 2 * nc + no + ns:]
        step = pl.program_id(0)
        for k in range(1, len(grid)):
            step = step * grid[k] + pl.program_id(k)
        parts = (csrc, cdst) + tuple(sems)

        @pl.when(step == 0)
        def _():
            comm.start(*parts)

        body(*ins, *outs, *scr)

        @pl.when(step == middle)
        def _():
            comm.mid(*parts)

        @pl.when(step == total - 1)
        def _():
            comm.finish(*parts)

    res = pl.pallas_call(
        wrapped, name=name, grid=grid, in_specs=list(in_specs) + [ANY] * nc, out_specs=list(out_specs) + [ANY] * nc,
        out_shape=list(out_shape) + comm.out_shape, scratch_shapes=list(scratch_shapes) + _comm_scratch(comm.nsem),
        compiler_params=_cp(("arbitrary",) * len(grid)))(*args, *comm.srcs)
    return res[:no], res[no:]


def _mm(name, a, b, *, grid, a_spec, b_spec, o_spec, out_shape, dims, kax=None, res=None, res_spec=None,
        jb=0, acc_shape=None, comm=None):
    nk = grid[kax] if kax is not None else 1

    def body(*refs):
        if res is not None:
            a_ref, b_ref, r_ref, o_ref = refs[:4]
        else:
            a_ref, b_ref, o_ref = refs[:3]

        def product():
            if not jb:
                return _dot(a_ref[...], b_ref[...], dims)
            part = _dot(a_ref[0], b_ref[0], dims)
            for j in range(1, jb):
                part = part + _dot(a_ref[j], b_ref[j], dims)
            return part

        def fin(acc):
            if res is not None:
                acc = acc + r_ref[...]
            o_ref[...] = acc.astype(o_ref.dtype)

        if nk == 1:
            fin(product())
        else:
            acc_ref = refs[-1]
            k = pl.program_id(kax)

            @pl.when(k == 0)
            def _():
                acc_ref[...] = jnp.zeros_like(acc_ref)

            acc_ref[...] += product()

            @pl.when(k == nk - 1)
            def _():
                fin(acc_ref[...])

    sem = tuple("arbitrary" if i == kax else "parallel" for i in range(len(grid)))
    in_specs = [a_spec, b_spec] + ([res_spec] if res is not None else [])
    args = (a, b) + ((res,) if res is not None else ())
    scratch = [pltpu.VMEM(acc_shape, F32)] if nk > 1 else []
    (out,), got = _pcall(body, name=name, grid=grid, in_specs=in_specs, out_specs=[o_spec], out_shape=[out_shape],
                         scratch_shapes=scratch, sem=sem, args=args, comm=comm)
    return out if comm is None else (out, got)


def _bs(shape, fn):
    return pl.BlockSpec(shape, fn)


def _rms_fwd(name, x, g, tm=512):
    T, D = x.shape

    def body(x_ref, g_ref, o_ref):
        xf = x_ref[...]
        r = lax.rsqrt(jnp.mean(xf * xf, axis=-1, keepdims=True) + RMS_EPS)
        o_ref[...] = (xf * r * g_ref[...]).astype(o_ref.dtype)

    return pl.pallas_call(
        body, name=name, grid=(T // tm,),
        in_specs=[_bs((tm, D), lambda i: (i, 0)), _bs((1, D), lambda i: (0, 0))],
        out_specs=_bs((tm, D), lambda i: (i, 0)), out_shape=jax.ShapeDtypeStruct((T, D), BF16),
        compiler_params=_cp(("parallel",)))(x, g)


def _rms_bwd(name, x, g, dh, dres=None, tm=512, also_bf16=False, comm=None):
    T, D = x.shape

    def body(*refs):
        if also_bf16:
            refs, dxb_ref = refs[:-1], refs[-1]
        if dres is not None:
            x_ref, g_ref, dh_ref, dres_ref, dx_ref, dg_ref = refs
        else:
            x_ref, g_ref, dh_ref, dx_ref, dg_ref = refs
        i = pl.program_id(0)
        xf = x_ref[...]
        r = lax.rsqrt(jnp.mean(xf * xf, axis=-1, keepdims=True) + RMS_EPS)
        xh = xf * r
        d = dh_ref[...].astype(F32)
        dxh = d * g_ref[...]
        dx = r * (dxh - xh * jnp.mean(dxh * xh, axis=-1, keepdims=True))
        if dres is not None:
            dx = dx + dres_ref[...]
        dx_ref[...] = dx
        if also_bf16:
            dxb_ref[...] = dx.astype(BF16)
        part = jnp.sum(d * xh, axis=0, keepdims=True)

        @pl.when(i == 0)
        def _():
            dg_ref[...] = part

        @pl.when(i > 0)
        def _():
            dg_ref[...] += part

    row = _bs((tm, D), lambda i: (i, 0))
    vec = _bs((1, D), lambda i: (0, 0))
    in_specs = [row, vec, row] + ([row] if dres is not None else [])
    args = (x, g, dh) + ((dres,) if dres is not None else ())
    extra = [jax.ShapeDtypeStruct((T, D), BF16)] if also_bf16 else []
    outs, got = _pcall(
        body, name=name, grid=(T // tm,), in_specs=in_specs, out_specs=[row, vec] + [row] * len(extra),
        out_shape=[jax.ShapeDtypeStruct((T, D), F32), jax.ShapeDtypeStruct((1, D), F32)] + extra, scratch_shapes=[],
        sem=("arbitrary",), args=args, comm=comm)
    return outs if comm is None else (outs, got)


def _loss(y, tgt, tm=512):
    T, D = y.shape

    def body(y_ref, t_ref, dy_ref, s_ref, dyb_ref):
        i = pl.program_id(0)
        e = y_ref[...] - t_ref[...]
        dy = e * (1.0 / D)
        dy_ref[...] = dy
        dyb_ref[...] = dy.astype(BF16)
        part = jnp.sum(e * e, axis=0, keepdims=True)

        @pl.when(i == 0)
        def _():
            s_ref[...] = part

        @pl.when(i > 0)
        def _():
            s_ref[...] += part

    row = _bs((tm, D), lambda i: (i, 0))
    return pl.pallas_call(
        body, name="loss_head", grid=(T // tm,), in_specs=[row, row],
        out_specs=[row, _bs((1, D), lambda i: (0, 0)), row],
        out_shape=[jax.ShapeDtypeStruct((T, D), F32), jax.ShapeDtypeStruct((1, D), F32),
                   jax.ShapeDtypeStruct((T, D), BF16)],
        compiler_params=_cp(("arbitrary",)))(y, tgt)


def _shift_rows(t, k, row):
    return jnp.where(row >= k, pltpu.roll(t, k, 0), 0.0)


def _shift_rows_up(t, k, row, n):
    return jnp.where(row < n - k, pltpu.roll(t, n - k, 0), 0.0)


def _convffn_fwd(name, u, cw, cb, B, S):
    _, J, T, F = u.shape

    def body(u_ref, cw_ref, cb_ref, o_ref):
        a = u_ref[0].astype(F32)
        g = u_ref[1].astype(F32)
        row = lax.broadcasted_iota(jnp.int32, (S, F), 0)
        w0, w1, w2 = cw_ref[0:1, :], cw_ref[1:2, :], cw_ref[2:3, :]
        gc = _shift_rows(g, 2, row) * w0 + _shift_rows(g, 1, row) * w1 + g * w2 + cb_ref[...]
        o_ref[...] = (gc * jax.nn.sigmoid(gc) * a).astype(o_ref.dtype)

    return pl.pallas_call(
        body, name=name, grid=(J, B),
        in_specs=[_bs((2, None, S, F), lambda j, b: (0, j, b, 0)), _bs((None, 3, F), lambda j, b: (j, 0, 0)),
                  _bs((None, 1, F), lambda j, b: (j, 0, 0))],
        out_specs=_bs((None, S, F), lambda j, b: (j, b, 0)), out_shape=jax.ShapeDtypeStruct((J, T, F), BF16),
        compiler_params=_cp(("parallel", "parallel")))(u, cw, cb)


def _convffn_bwd(name, u, cw, cb, dgt, B, S, comm=None):
    _, J, T, F = u.shape

    def body(u_ref, cw_ref, cb_ref, d_ref, du_ref, dcw_ref, dcb_ref):
        b = pl.program_id(1)
        a = u_ref[0].astype(F32)
        g = u_ref[1].astype(F32)
        d = d_ref[...].astype(F32)
        row = lax.broadcasted_iota(jnp.int32, (S, F), 0)
        w0, w1, w2 = cw_ref[0:1, :], cw_ref[1:2, :], cw_ref[2:3, :]
        g1, g2 = _shift_rows(g, 1, row), _shift_rows(g, 2, row)
        gc = g2 * w0 + g1 * w1 + g * w2 + cb_ref[...]
        sg = jax.nn.sigmoid(gc)
        du_ref[0] = (d * gc * sg).astype(du_ref.dtype)
        dgc = d * a * (sg * (1.0 + gc * (1.0 - sg)))
        dg = dgc * w2 + _shift_rows_up(dgc, 1, row, S) * w1 + _shift_rows_up(dgc, 2, row, S) * w0
        du_ref[1] = dg.astype(du_ref.dtype)
        parts = [jnp.sum(dgc * g2, axis=0, keepdims=True), jnp.sum(dgc * g1, axis=0, keepdims=True),
                 jnp.sum(dgc * g, axis=0, keepdims=True)]
        pb = jnp.sum(dgc, axis=0, keepdims=True)

        @pl.when(b == 0)
        def _():
            for k in range(3):
                dcw_ref[k:k + 1, :] = parts[k]
            dcb_ref[...] = pb

        @pl.when(b > 0)
        def _():
            for k in range(3):
                dcw_ref[k:k + 1, :] += parts[k]
            dcb_ref[...] += pb

    uspec = _bs((2, None, S, F), lambda j, b: (0, j, b, 0))
    return _pcall(
        body, name=name, grid=(J, B),
        in_specs=[uspec, _bs((None, 3, F), lambda j, b: (j, 0, 0)), _bs((None, 1, F), lambda j, b: (j, 0, 0)),
                  _bs((None, S, F), lambda j, b: (j, b, 0))],
        out_specs=[uspec, _bs((None, 3, F), lambda j, b: (j, 0, 0)), _bs((None, 1, F), lambda j, b: (j, 0, 0))],
        out_shape=[jax.ShapeDtypeStruct(u.shape, BF16), jax.ShapeDtypeStruct((J, 3, F), F32),
                   jax.ShapeDtypeStruct((J, 1, F), F32)],
        scratch_shapes=[], sem=("parallel", "arbitrary"), args=(u, cw, cb, dgt), comm=comm)


def _ret_tables(S):
    half = RET_DK // 2
    inv = ROPE_THETA ** (-jnp.arange(half, dtype=F32) / half)
    ang = jnp.arange(S).astype(F32)[:, None] * inv[None, :]
    lg = jnp.log1p(-jnp.exp2(-5.0 - jnp.arange(RET_H, dtype=F32)))
    i = jnp.arange(RET_SC, dtype=F32)
    same_or_earlier = (jnp.floor(i[None, :] / CHUNK) <= jnp.floor(i[:, None] / CHUNK)).astype(F32)
    dm = jnp.exp(lg[:, None, None] * jnp.abs(i[:, None] - i[None, :])) * same_or_earlier[None]
    qd = jnp.exp(lg[:, None] * (i + 1.0))[:, :, None]
    kd = jnp.exp(lg[:, None] * (RET_SC - 1.0 - i))[:, :, None]
    cd = jnp.exp(lg * RET_SC)[:, None, None]
    return jnp.cos(ang), jnp.sin(ang), dm, qd, kd, cd


def _rope_halves(t, cs, sn):
    h = t.shape[-1] // 2
    t1, t2 = t[:, :h], t[:, h:]
    return jnp.concatenate([t1 * cs - t2 * sn, t2 * cs + t1 * sn], axis=-1)


def _unrope_halves(d, cs, sn):
    h = d.shape[-1] // 2
    d1, d2 = d[:, :h], d[:, h:]
    return jnp.concatenate([d1 * cs + d2 * sn, d2 * cs - d1 * sn], axis=-1)


def _ret_specs(nC, order):
    SC = RET_SC

    def sp(shape, fn):
        return _bs(shape, lambda *g: fn(*order(*g)))

    q = sp((SC, RET_DK), lambda b, h, c: (b * nC + c, h))
    k = sp((SC, RET_DK), lambda b, h, c: (b * nC + c, RET_H + h))
    v = sp((SC, RET_DV), lambda b, h, c: (b * nC + c, RET_H + h))
    g = sp((SC, RET_DV), lambda b, h, c: (b * nC + c, 2 * RET_H + h))
    cs = sp((SC, RET_DK // 2), lambda b, h, c: (c, 0))
    dm = sp((None, SC, SC), lambda b, h, c: (h, 0, 0))
    dv = sp((None, SC, 1), lambda b, h, c: (h, 0, 0))
    cd = sp((None, 1, 1), lambda b, h, c: (h, 0, 0))
    gn = sp((None, 1, RET_DV), lambda b, h, c: (h, 0, 0))
    wide = sp((SC, RET_DV), lambda b, h, c: (b * nC + c, h))
    narrow = sp((SC, RET_DK), lambda b, h, c: (b * nC + c, h))
    st = sp((None, None, None, RET_DK, RET_DV), lambda b, h, c: (b, h, c, 0, 0))
    return dict(q=q, k=k, v=v, g=g, cs=cs, dm=dm, dv=dv, cd=cd, gn=gn, wide=wide, narrow=narrow, st=st)


def _ret_fwd(proj, tabs, gn, B, S, comm=None):
    T = B * S
    nC = S // RET_SC
    cos, sin, dm, qd, kd, cd = tabs
    s = _ret_specs(nC, lambda b, h, c: (b, h, c))

    def body(q_ref, k_ref, v_ref, g_ref, cos_ref, sin_ref, dm_ref, qd_ref, kd_ref, cd_ref, gn_ref,
             o_ref, gt_ref, st_ref, state):
        c = pl.program_id(2)

        @pl.when(c == 0)
        def _():
            state[...] = jnp.zeros_like(state)

        cs, sn = cos_ref[...], sin_ref[...]
        qf = _rope_halves(q_ref[...].astype(F32), cs, sn)
        kf = _rope_halves(k_ref[...].astype(F32), cs, sn) * (RET_DK ** -0.5)
        v = v_ref[...]
        p = _dot(qf.astype(BF16), kf.astype(BF16), NT) * dm_ref[...]
        st = state[...]
        stb = st.astype(BF16)
        st_ref[...] = stb
        o = _dot(p.astype(BF16), v, NN) + _dot((qf * qd_ref[...]).astype(BF16), stb, NN)
        state[...] = st * cd_ref[...] + _dot((kf * kd_ref[...]).astype(BF16), v, TN)
        o_ref[...] = o
        r = lax.rsqrt(jnp.mean(o * o, axis=-1, keepdims=True) + RMS_EPS)
        gf = g_ref[...].astype(F32)
        gt_ref[...] = ((o * r * gn_ref[...]) * (gf * jax.nn.sigmoid(gf))).astype(BF16)

    return _pcall(
        body, name="ret_fwd", grid=(B, RET_H, nC),
        in_specs=[s["q"], s["k"], s["v"], s["g"], s["cs"], s["cs"], s["dm"], s["dv"], s["dv"], s["cd"], s["gn"]],
        out_specs=[s["wide"], s["wide"], s["st"]],
        out_shape=[jax.ShapeDtypeStruct((T, RET_H * RET_DV), F32), jax.ShapeDtypeStruct((T, RET_H * RET_DV), BF16),
                   jax.ShapeDtypeStruct((B, RET_H, nC, RET_DK, RET_DV), BF16)],
        scratch_shapes=[pltpu.VMEM((RET_DK, RET_DV), F32)], sem=("parallel", "parallel", "arbitrary"),
        args=(proj, proj, proj, proj, cos, sin, dm, qd, kd, cd, gn), comm=comm)


def _ret_bwd(proj, o_raw, states, dgt, tabs, gn, B, S, comm=None):
    T = B * S
    nC = S // RET_SC
    cos, sin, dm, qd, kd, cd = tabs
    s = _ret_specs(nC, lambda b, c, h: (b, h, nC - 1 - c))

    def body(q_ref, k_ref, v_ref, g_ref, o_ref, st_ref, d_ref, cos_ref, sin_ref, dm_ref, qd_ref, kd_ref, cd_ref,
             gn_ref, dproj_ref, dgn_ref, dstates):
        b, c, h = pl.program_id(0), pl.program_id(1), pl.program_id(2)
        dstate = dstates.at[h]

        @pl.when(c == 0)
        def _():
            dstate[...] = jnp.zeros_like(dstate)

        @pl.when((b == 0) & (c == 0))
        def _():
            dgn_ref[h] = jnp.zeros((1, RET_DV), F32)

        cs, sn = cos_ref[...], sin_ref[...]
        qf = _rope_halves(q_ref[...].astype(F32), cs, sn)
        kf = _rope_halves(k_ref[...].astype(F32), cs, sn) * (RET_DK ** -0.5)
        v = v_ref[...]
        gnv = gn_ref[...]
        o = o_ref[...]
        r = lax.rsqrt(jnp.mean(o * o, axis=-1, keepdims=True) + RMS_EPS)
        oh = o * r
        gf = g_ref[...].astype(F32)
        sg = jax.nn.sigmoid(gf)
        d = d_ref[...].astype(F32)
        dg = (d * (oh * gnv) * (sg * (1.0 + gf * (1.0 - sg)))).astype(BF16)
        don = d * (gf * sg)
        dgn_ref[h] += jnp.sum(don * oh, axis=0, keepdims=True)
        doh = don * gnv
        dO = (r * (doh - oh * jnp.mean(doh * oh, axis=-1, keepdims=True))).astype(BF16)
        dmv = dm_ref[...]
        qb, kb = qf.astype(BF16), kf.astype(BF16)
        p = (_dot(qb, kb, NT) * dmv).astype(BF16)
        dp = (_dot(dO, v, NT) * dmv).astype(BF16)
        st = st_ref[...]
        dsn = dstate[...]
        dsb = dsn.astype(BF16)
        qdv, kdv = qd_ref[...], kd_ref[...]
        dq = _dot(dp, kb, NN) + _dot(dO, st, NT) * qdv
        dk = _dot(dp, qb, TN) + _dot(v, dsb, NT) * kdv
        dv = _dot(p, dO, TN) + _dot((kf * kdv).astype(BF16), dsb, NN)
        dstate[...] = dsn * cd_ref[...] + _dot((qf * qdv).astype(BF16), dO, TN)
        dq = _unrope_halves(dq, cs, sn).astype(BF16)
        dk = (_unrope_halves(dk, cs, sn) * (RET_DK ** -0.5)).astype(BF16)
        dv = dv.astype(BF16)
        nq, nv = RET_H * RET_DK, RET_H * RET_DV
        for hh in range(RET_H):
            @pl.when(h == hh)
            def _():
                dproj_ref[:, hh * RET_DK:(hh + 1) * RET_DK] = dq
                dproj_ref[:, nq + hh * RET_DK:nq + (hh + 1) * RET_DK] = dk
                dproj_ref[:, 2 * nq + hh * RET_DV:2 * nq + (hh + 1) * RET_DV] = dv
                dproj_ref[:, 2 * nq + nv + hh * RET_DV:2 * nq + nv + (hh + 1) * RET_DV] = dg

    width = 2 * RET_H * (RET_DK + RET_DV)
    return _pcall(
        body, name="ret_bwd", grid=(B, nC, RET_H),
        in_specs=[s["q"], s["k"], s["v"], s["g"], s["wide"], s["st"], s["wide"], s["cs"], s["cs"], s["dm"], s["dv"],
                  s["dv"], s["cd"], s["gn"]],
        out_specs=[_bs((RET_SC, width), lambda b, c, h: (b * nC + nC - 1 - c, 0)),
                   _bs((RET_H, 1, RET_DV), lambda b, c, h: (0, 0, 0))],
        out_shape=[jax.ShapeDtypeStruct((T, width), BF16), jax.ShapeDtypeStruct((RET_H, 1, RET_DV), F32)],
        scratch_shapes=[pltpu.VMEM((RET_H, RET_DK, RET_DV), F32)], sem=("arbitrary", "arbitrary", "arbitrary"),
        args=(proj, proj, proj, proj, o_raw, states, dgt, cos, sin, dm, qd, kd, cd, gn), comm=comm)


MLA_PAD = 256
MLA_R2 = 2 * MLA_ROPE


def _dup(t):
    return jnp.concatenate([t, t], axis=-1)


def _fold(t):
    return t[..., :MLA_ROPE] + t[..., MLA_ROPE:]


def _mla_tables(S):
    half = MLA_ROPE // 2
    inv = ROPE_THETA ** (-jnp.arange(half, dtype=F32) / half)
    ang = jnp.arange(S).astype(F32)[:, None] * inv[None, :]
    cos, sin, zero = jnp.cos(ang), jnp.sin(ang), jnp.zeros((S, MLA_ROPE), F32)
    return jnp.concatenate([cos, cos, zero], axis=-1), jnp.concatenate([-sin, sin, zero], axis=-1)


def _head_norm_rope(n, r2, gn, gr2, cos, sin, scale):
    ssq = jnp.sum(n * n, axis=-1, keepdims=True) + 0.5 * jnp.sum(r2 * r2, axis=-1, keepdims=True)
    rstd = lax.rsqrt(ssq * (1.0 / MLA_QK) + RMS_EPS)
    yn = n * rstd * gn
    yr = r2 * rstd * gr2
    z = yr * cos + pltpu.roll(yr, MLA_ROPE // 2, 1) * sin
    if scale != 1.0:
        yn, z = yn * scale, z * scale
    return yn, z


def _head_norm_rope_bwd(dn, dz, n, r2, gn, gr2, cos, sin, scale):
    ssq = jnp.sum(n * n, axis=-1, keepdims=True) + 0.5 * jnp.sum(r2 * r2, axis=-1, keepdims=True)
    rstd = lax.rsqrt(ssq * (1.0 / MLA_QK) + RMS_EPS)
    hn, hr = n * rstd, r2 * rstd
    if scale != 1.0:
        dn, dz = dn * scale, dz * scale
    dyr = dz * cos + pltpu.roll(dz * sin, MLA_R2 - MLA_ROPE // 2, 1)
    dgn = jnp.sum(dn * hn, axis=0, keepdims=True)
    dgr = jnp.sum(dyr * hr, axis=0, keepdims=True)
    dhn, dhr = dn * gn, dyr * gr2
    mt = (jnp.sum(dhn * hn, axis=-1, keepdims=True) + jnp.sum(dhr * hr, axis=-1, keepdims=True)) * (1.0 / MLA_QK)
    return rstd * (dhn - hn * mt), rstd * (dhr - 0.5 * hr * mt), dgn, dgr


def _diag_bias():
    i = jnp.arange(ATT_TQ)
    return jnp.where((i[None, :] // CHUNK) <= (i[:, None] // CHUNK), 0.0, MASK_VALUE).astype(F32)


def _store_pair(dst, rows, n, r2):
    dst[rows, :MLA_NOPE] = n.astype(BF16)
    dst[rows, MLA_NOPE:] = r2.astype(BF16)


def _mla_fwd(q_raw, kv, kr, gains, tabs, B, S, comm=None):
    T = B * S
    TQ = ATT_TQ
    nQ = S // TQ
    qgn, qgr, kgn, kgr = gains
    cos, sin = tabs
    scale = MLA_QK ** -0.5

    def body(q_ref, kv_ref, kr_ref, qgn_ref, qgr_ref, kgn_ref, kgr_ref, c_ref, s_ref, bias_ref,
             o_ref, lse_ref, qf_s, kf_s, v_s):
        def prep(t, _):
            rows = pl.ds(pl.multiple_of(t * TQ, TQ), TQ)
            cs, sn = c_ref[rows, :], s_ref[rows, :]
            qn, qr = _head_norm_rope(q_ref[rows, :MLA_NOPE], q_ref[rows, MLA_NOPE:], qgn_ref[...], qgr_ref[...],
                                     cs, sn, scale)
            _store_pair(qf_s, rows, qn, qr)
            kn, krr = _head_norm_rope(kv_ref[rows, :MLA_NOPE], kr_ref[rows, :], kgn_ref[...], kgr_ref[...], cs, sn, 1.0)
            _store_pair(kf_s, rows, kn, krr)
            v_s[rows, :] = kv_ref[rows, MLA_NOPE:].astype(BF16)
            return 0

        lax.fori_loop(0, nQ, prep, 0, unroll=2)
        for i in range(nQ):
            rows = slice(i * TQ, (i + 1) * TQ)
            q = qf_s[rows, :]
            sd = _dot(q, kf_s[rows, :], NT) + bias_ref[...]
            m = jnp.max(sd, axis=-1, keepdims=True)
            if i:
                sl = _dot(q, kf_s[:i * TQ, :], NT)
                m = jnp.maximum(m, jnp.max(sl, axis=-1, keepdims=True))
            pd = jnp.exp(sd - m)
            l = jnp.sum(pd, axis=-1, keepdims=True)
            acc = _dot(pd.astype(BF16), v_s[rows, :], NN)
            if i:
                pl_ = jnp.exp(sl - m)
                l = l + jnp.sum(pl_, axis=-1, keepdims=True)
                acc = acc + _dot(pl_.astype(BF16), v_s[:i * TQ, :], NN)
            o_ref[rows, :] = (acc / l).astype(BF16)
            lse_ref[rows, :] = m + jnp.log(l)

    def vec(n):
        return _bs((1, n), lambda b, h: (0, 0))

    def cols(n):
        return _bs((S, n), lambda b, h: (b, h))

    tab = _bs((S, MLA_R2), lambda b, h: (0, 0))
    return _pcall(
        body, name="mla_fwd", grid=(B, MLA_H),
        in_specs=[cols(MLA_PAD), cols(MLA_NOPE + MLA_V), _bs((S, MLA_R2), lambda b, h: (b, 0)),
                  vec(MLA_NOPE), vec(MLA_R2), vec(MLA_NOPE), vec(MLA_R2), tab, tab,
                  _bs((TQ, TQ), lambda b, h: (0, 0))],
        out_specs=[cols(MLA_V), _bs((None, S, 1), lambda b, h: (h, b, 0)), cols(MLA_PAD), cols(MLA_PAD)],
        out_shape=[jax.ShapeDtypeStruct((T, MLA_H * MLA_V), BF16), jax.ShapeDtypeStruct((MLA_H, T, 1), F32),
                   jax.ShapeDtypeStruct((T, MLA_H * MLA_PAD), BF16), jax.ShapeDtypeStruct((T, MLA_H * MLA_PAD), BF16)],
        scratch_shapes=[pltpu.VMEM((S, MLA_V), BF16)],
        sem=("parallel", "parallel"), args=(q_raw, kv, kr, qgn, qgr, kgn, kgr, cos, sin, _diag_bias()), comm=comm)


def _mla_bwd(q_raw, kv, kr, o, lse, do, qf, kf, gains, tabs, B, S, comm=None):
    T = B * S
    TQ = ATT_TQ
    nQ = S // TQ
    qgn, qgr, kgn, kgr = gains
    cos, sin = tabs
    scale = MLA_QK ** -0.5

    def body(q_ref, kv_ref, kr_ref, o_ref, lse_ref, do_ref, qf_s, kf_s, qgn_ref, qgr_ref, kgn_ref, kgr_ref, c_ref, s_ref,
             bias_ref, dq_ref, dkv_ref, dkr_ref, dqgn_ref, dqgr_ref, dkgn_ref, dkgr_ref,
             v_s, dl_s, dq_s, dk_s, dv_s):
        b, h = pl.program_id(0), pl.program_id(1)

        def blk(t):
            return pl.ds(pl.multiple_of(t * TQ, TQ), TQ)

        def prep(t, _):
            rows = blk(t)
            v_s[rows, :] = kv_ref[rows, MLA_NOPE:].astype(BF16)
            dl_s[rows, :] = jnp.sum(do_ref[rows, :].astype(F32) * o_ref[rows, :].astype(F32), axis=-1, keepdims=True)
            dk_s[rows, :] = jnp.zeros((TQ, MLA_PAD), F32)
            dv_s[rows, :] = jnp.zeros((TQ, MLA_V), F32)
            return 0

        lax.fori_loop(0, nQ, prep, 0, unroll=2)

        gqn, gqr = jnp.zeros((1, MLA_NOPE), F32), jnp.zeros((1, MLA_R2), F32)
        for i in range(nQ):
            rows = slice(i * TQ, (i + 1) * TQ)
            q, doi, lse_i, dl_i = qf_s[rows, :], do_ref[rows, :], lse_ref[rows, :], dl_s[rows, :]

            def part(cols, bias):
                k, v = kf_s[cols, :], v_s[cols, :]
                s = _dot(q, k, NT)
                if bias is not None:
                    s = s + bias
                p = jnp.exp(s - lse_i)
                ds = (p * (_dot(doi, v, NT) - dl_i)).astype(BF16)
                dk_s[cols, :] += _dot(ds, q, TN)
                dv_s[cols, :] += _dot(p.astype(BF16), doi, TN)
                return _dot(ds, k, NN)

            dq = part(rows, bias_ref[...])
            if i:
                dq = dq + part(slice(0, i * TQ), None)
            dq_s[...] = dq
            dqn, dqr, a0, a1 = _head_norm_rope_bwd(dq_s[:, :MLA_NOPE], dq_s[:, MLA_NOPE:], q_ref[rows, :MLA_NOPE],
                                                   q_ref[rows, MLA_NOPE:], qgn_ref[...], qgr_ref[...],
                                                   c_ref[rows, :], s_ref[rows, :], scale)
            _store_pair(dq_ref, rows, dqn, dqr)
            gqn, gqr = gqn + a0, gqr + a1

        def post(t, carry):
            rows = blk(t)
            dkn, dkr, a2, a3 = _head_norm_rope_bwd(dk_s[rows, :MLA_NOPE], dk_s[rows, MLA_NOPE:],
                                                   kv_ref[rows, :MLA_NOPE], kr_ref[rows, :], kgn_ref[...], kgr_ref[...],
                                                   c_ref[rows, :], s_ref[rows, :], 1.0)
            dkv_ref[rows, :MLA_NOPE] = dkn.astype(BF16)
            dkv_ref[rows, MLA_NOPE:] = dv_s[rows, :].astype(BF16)

            @pl.when(h == 0)
            def _():
                dkr_ref[rows, :] = dkr

            @pl.when(h > 0)
            def _():
                dkr_ref[rows, :] += dkr

            return carry[0] + a2, carry[1] + a3

        gkn, gkr = lax.fori_loop(0, nQ, post, (jnp.zeros((1, MLA_NOPE), F32), jnp.zeros((1, MLA_R2), F32)), unroll=2)
        first = (b == 0) & (h == 0)

        @pl.when(first)
        def _():
            dqgn_ref[...] = gqn
            dqgr_ref[...] = gqr
            dkgn_ref[...] = gkn
            dkgr_ref[...] = gkr

        @pl.when(jnp.logical_not(first))
        def _():
            dqgn_ref[...] += gqn
            dqgr_ref[...] += gqr
            dkgn_ref[...] += gkn
            dkgr_ref[...] += gkr

    def vec(n):
        return _bs((1, n), lambda b, h: (0, 0))

    def cols(n):
        return _bs((S, n), lambda b, h: (b, h))

    tab = _bs((S, MLA_R2), lambda b, h: (0, 0))
    return _pcall(
        body, name="mla_bwd", grid=(B, MLA_H),
        in_specs=[cols(MLA_PAD), cols(MLA_NOPE + MLA_V), _bs((S, MLA_R2), lambda b, h: (b, 0)), cols(MLA_V),
                  _bs((None, S, 1), lambda b, h: (h, b, 0)), cols(MLA_V), cols(MLA_PAD), cols(MLA_PAD),
                  vec(MLA_NOPE), vec(MLA_R2), vec(MLA_NOPE), vec(MLA_R2), tab, tab,
                  _bs((TQ, TQ), lambda b, h: (0, 0))],
        out_specs=[cols(MLA_PAD), cols(MLA_NOPE + MLA_V), _bs((S, MLA_R2), lambda b, h: (b, 0)),
                   vec(MLA_NOPE), vec(MLA_R2), vec(MLA_NOPE), vec(MLA_R2)],
        out_shape=[jax.ShapeDtypeStruct((T, MLA_H * MLA_PAD), BF16),
                   jax.ShapeDtypeStruct((T, MLA_H * (MLA_NOPE + MLA_V)), BF16),
                   jax.ShapeDtypeStruct((T, MLA_R2), F32), jax.ShapeDtypeStruct((1, MLA_NOPE), F32),
                   jax.ShapeDtypeStruct((1, MLA_R2), F32), jax.ShapeDtypeStruct((1, MLA_NOPE), F32),
                   jax.ShapeDtypeStruct((1, MLA_R2), F32)],
        scratch_shapes=[pltpu.VMEM((S, MLA_V), BF16), pltpu.VMEM((S, 1), F32), pltpu.VMEM((TQ, MLA_PAD), F32),
                        pltpu.VMEM((S, MLA_PAD), F32), pltpu.VMEM((S, MLA_V), F32)],
        sem=("arbitrary", "arbitrary"),
        args=(q_raw, kv, kr, o, lse, do, qf, kf, qgn, qgr, kgn, kgr, cos, sin, _diag_bias()), comm=comm)


def _adamw(name, recvs, w, m, v, tr=None, comm=None):
    n, R, C = recvs[0].shape
    L = len(recvs)
    Lw, Rw, _ = w.shape
    assert Lw * Rw == L * R and w.shape[2] == C
    tr = R if tr is None else tr
    assert R % tr == 0 and Rw % tr == 0
    per = R // tr
    per_w = Rw // tr
    c1 = 1.0 - ADAM_B1 ** ADAM_STEP
    c2 = 1.0 - ADAM_B2 ** ADAM_STEP

    def body(*refs):
        r_refs = refs[:L]
        w_ref, m_ref, v_ref, g_ref, d_ref, nm_ref, nv_ref = refs[L:]
        layer = pl.program_id(0) // per

        def total(r_ref):
            t = r_ref[0].astype(F32)
            for k in range(1, n):
                t = t + r_ref[k].astype(F32)
            return t

        g = total(r_refs[0])
        for l in range(1, L):
            g = jnp.where(layer == l, total(r_refs[l]), g)
        mm = ADAM_B1 * m_ref[...] + (1.0 - ADAM_B1) * g
        vv = ADAM_B2 * v_ref[...] + (1.0 - ADAM_B2) * (g * g)
        g_ref[...] = g
        nm_ref[...] = mm
        nv_ref[...] = vv
        d_ref[...] = -ADAM_LR * ((mm / c1) / (jnp.sqrt(vv / c2) + ADAM_EPS) + ADAM_WD * w_ref[...])

    blk = _bs((None, tr, C), lambda i: (i // per_w, i % per_w, 0))
    r_specs = [_bs((n, tr, C), functools.partial(lambda l, i: (0, jnp.clip(i - l * per, 0, per - 1), 0), l))
               for l in range(L)]
    outs, got = _pcall(body, name=name, grid=(L * per,), in_specs=r_specs + [blk, blk, blk], out_specs=[blk] * 4,
                       out_shape=[jax.ShapeDtypeStruct(w.shape, F32)] * 4, scratch_shapes=[], sem=("arbitrary",),
                       args=(*recvs, w, m, v), comm=comm)
    return outs if comm is None else (outs, got)


def _sum8(name, a):
    n, R, C = a.shape

    def body(a_ref, o_ref):
        s = a_ref[0]
        for k in range(1, n):
            s = s + a_ref[k]
        o_ref[...] = s

    return pl.pallas_call(body, name=name, out_shape=jax.ShapeDtypeStruct((R, C), a.dtype))(a)


def _sds(shape, dt):
    return jax.ShapeDtypeStruct(shape, dt)


def _norm_proj(name, x, g, w, o_spec, out_shape, tm=1024, comm=None):
    T, K = x.shape
    J, _, n = w.shape

    def body(x_ref, g_ref, w_ref, o_ref, h_ref, hs):
        @pl.when(pl.program_id(1) == 0)
        def _():
            xf = x_ref[...]
            r = lax.rsqrt(jnp.mean(xf * xf, axis=-1, keepdims=True) + RMS_EPS)
            h = (xf * r * g_ref[...]).astype(BF16)
            hs[...] = h
            h_ref[...] = h

        o_ref[...] = _dot(hs[...], w_ref[...], NN).astype(o_ref.dtype)

    row = _bs((tm, K), lambda m, j: (m, 0))
    (out, h), got = _pcall(
        body, name=name, grid=(T // tm, J),
        in_specs=[row, _bs((1, K), lambda m, j: (0, 0)), _bs((None, K, n), lambda m, j: (j, 0, 0))],
        out_specs=[o_spec, row], out_shape=[out_shape, _sds((T, K), BF16)], scratch_shapes=[pltpu.VMEM((tm, K), BF16)],
        sem=("parallel", "arbitrary"), args=(x, g, w), comm=comm)
    return out, h, got


def _proj_shared_dx(name, d, w, tm=1024, comm=None):
    J, T, n = d.shape
    K = w.shape[1]
    return _mm(name, d, w, grid=(T // tm, J), a_spec=_bs((None, tm, n), lambda m, k: (k, m, 0)),
               b_spec=_bs((None, K, n), lambda m, k: (k, 0, 0)), o_spec=_bs((tm, K), lambda m, k: (m, 0)),
               out_shape=_sds((T, K), F32), dims=NT, kax=1, acc_shape=(tm, K), comm=comm)


def _out_proj(name, a, w, res, tm=512):
    J, T, k = a.shape
    N = w.shape[2]
    return _mm(name, a, w, grid=(T // tm,), a_spec=_bs((J, tm, k), lambda m: (0, m, 0)),
               b_spec=_bs((J, k, N), lambda m: (0, 0, 0)), o_spec=_bs((tm, N), lambda m: (m, 0)),
               out_shape=_sds((T, N), F32), dims=NN, res=res, res_spec=_bs((tm, N), lambda m: (m, 0)), jb=J)


def _out_proj_dx(name, dx, w, tm=1024, comm=None):
    T, N = dx.shape
    J, k, _ = w.shape
    return _mm(name, dx, w, grid=(T // tm, J), a_spec=_bs((tm, N), lambda m, j: (m, 0)),
               b_spec=_bs((None, k, N), lambda m, j: (j, 0, 0)), o_spec=_bs((None, tm, k), lambda m, j: (j, m, 0)),
               out_shape=_sds((J, T, k), BF16), dims=NT, comm=comm)


def _out_proj_dw(name, a, dx, tt=1024, comm=None):
    J, T, k = a.shape
    N = dx.shape[1]
    return _mm(name, a, dx, grid=(J, T // tt), a_spec=_bs((None, tt, k), lambda j, t: (j, t, 0)),
               b_spec=_bs((tt, N), lambda j, t: (t, 0)), o_spec=_bs((None, k, N), lambda j, t: (j, 0, 0)),
               out_shape=_sds((J, k, N), BF16), dims=TN, kax=1, acc_shape=(k, N), comm=comm)


def _dense(name, a, b, dims, out_dtype, tm=512, res=None, comm=None):
    if dims == TN:
        T, K = a.shape
        N = b.shape[1]
        return _mm(name, a, b, grid=(T // tm,), a_spec=_bs((tm, K), lambda t: (t, 0)),
                   b_spec=_bs((tm, N), lambda t: (t, 0)), o_spec=_bs((K, N), lambda t: (0, 0)),
                   out_shape=_sds((K, N), out_dtype), dims=TN, kax=0, acc_shape=(K, N), comm=comm)
    M, K = a.shape
    N = b.shape[1] if dims == NN else b.shape[0]
    row = _bs((tm, N), lambda m: (m, 0))
    return _mm(name, a, b, grid=(M // tm,), a_spec=_bs((tm, K), lambda m: (m, 0)), b_spec=_bs(b.shape, lambda m: (0, 0)),
               o_spec=row, out_shape=_sds((M, N), out_dtype), dims=dims, res=res,
               res_spec=row if res is not None else None, comm=comm)


def _bf16(x):
    return x.astype(BF16)


def _ffn_fwd(i, x, norm_g, w_in, cw, cb, w_out, B, S, comm_in=None):
    T = x.shape[0]
    u, h, got = _norm_proj(f"ffn{i}_in", x, norm_g, w_in, _bs((None, 1024, FSH), lambda m, j: (j, m, 0)),
                           _sds((NDEV, T, FSH), BF16), comm=comm_in)
    u4 = u.reshape(2, 4, T, FSH)
    gt = _convffn_fwd(f"ffn{i}_gate", u4, cw, cb, B, S)
    y = _out_proj(f"ffn{i}_out", gt, w_out, x)
    return y, (x, h, u4, gt), got


def _ffn_bwd(i, dy, dyb, saved, norm_g, w_in, cw, cb, w_out, B, S, first_half_early):
    x, h, u4, gt = saved
    dgt = _out_proj_dx(f"ffn{i}_out_dx", dyb, w_out)
    dw_out = _out_proj_dw(f"ffn{i}_out_dw", gt, dyb).reshape(NDEV, FSH // 2, D_MODEL)
    (du4, dcw, dcb), (r_out,) = _convffn_bwd(f"ffn{i}_gate_bwd", u4, cw, cb, dgt, B, S, comm=_Exchange([dw_out]))
    du = du4.reshape(NDEV, du4.shape[2], FSH)
    dw_in = _out_proj_dw(f"ffn{i}_in_dw", du, h, tt=2048)
    r_in = None
    if first_half_early:
        dh, (r_in,) = _proj_shared_dx(f"ffn{i}_in_dx", du, w_in, comm=_Exchange([dw_in], rows=[(0, FSH // 2)]))
    else:
        dh = _proj_shared_dx(f"ffn{i}_in_dx", du, w_in)
    dx, dgn, dxb = _rms_bwd(f"ffn{i}_norm_bwd", x, norm_g, dh, dres=dy, also_bf16=True)
    return dx, dxb, dict(w_in=dw_in, norm=dgn, cw=dcw, cb=dcb), r_out, r_in


def kernel(x, ret_norm, ret_w_in, ret_gn, ret_w_out, mla_norm, mla_w_in, mla_q_norm, mla_w_qb, mla_kv_norm, mla_w_kvb, mla_q_head_norm, mla_k_head_norm, mla_w_out, ffn_norm, ffn_w_in, ffn_conv_w, ffn_conv_b, ffn_w_out, loss_target, m_ret_norm, m_ret_w_in, m_ret_gn, m_ret_w_out, m_mla_norm, m_mla_w_in, m_mla_q_norm, m_mla_w_qb, m_mla_kv_norm, m_mla_w_kvb, m_mla_q_head_norm, m_mla_k_head_norm, m_mla_w_out, m_ffn_norm, m_ffn_w_in, m_ffn_conv_w, m_ffn_conv_b, m_ffn_w_out, v_ret_norm, v_ret_w_in, v_ret_gn, v_ret_w_out, v_mla_norm, v_mla_w_in, v_mla_q_norm, v_mla_w_qb, v_mla_kv_norm, v_mla_w_kvb, v_mla_q_head_norm, v_mla_k_head_norm, v_mla_w_out, v_ffn_norm, v_ffn_w_in, v_ffn_conv_w, v_ffn_conv_b, v_ffn_w_out):
    B, S, D = x.shape
    T = B * S
    w = dict(ret_norm=ret_norm, ret_w_in=ret_w_in, ret_gn=ret_gn, ret_w_out=ret_w_out, mla_norm=mla_norm,
             mla_w_in=mla_w_in, mla_q_norm=mla_q_norm, mla_w_qb=mla_w_qb, mla_kv_norm=mla_kv_norm, mla_w_kvb=mla_w_kvb,
             mla_q_head_norm=mla_q_head_norm, mla_k_head_norm=mla_k_head_norm, mla_w_out=mla_w_out, ffn_norm=ffn_norm,
             ffn_w_in=ffn_w_in, ffn_conv_w=ffn_conv_w, ffn_conv_b=ffn_conv_b, ffn_w_out=ffn_w_out)
    mom = dict(ret_norm=m_ret_norm, ret_w_in=m_ret_w_in, ret_gn=m_ret_gn, ret_w_out=m_ret_w_out, mla_norm=m_mla_norm,
               mla_w_in=m_mla_w_in, mla_q_norm=m_mla_q_norm, mla_w_qb=m_mla_w_qb, mla_kv_norm=m_mla_kv_norm,
               mla_w_kvb=m_mla_w_kvb, mla_q_head_norm=m_mla_q_head_norm, mla_k_head_norm=m_mla_k_head_norm,
               mla_w_out=m_mla_w_out, ffn_norm=m_ffn_norm, ffn_w_in=m_ffn_w_in, ffn_conv_w=m_ffn_conv_w,
               ffn_conv_b=m_ffn_conv_b, ffn_w_out=m_ffn_w_out)
    var = dict(ret_norm=v_ret_norm, ret_w_in=v_ret_w_in, ret_gn=v_ret_gn, ret_w_out=v_ret_w_out, mla_norm=v_mla_norm,
               mla_w_in=v_mla_w_in, mla_q_norm=v_mla_q_norm, mla_w_qb=v_mla_w_qb, mla_kv_norm=v_mla_kv_norm,
               mla_w_kvb=v_mla_w_kvb, mla_q_head_norm=v_mla_q_head_norm, mla_k_head_norm=v_mla_k_head_norm,
               mla_w_out=v_mla_w_out, ffn_norm=v_ffn_norm, ffn_w_in=v_ffn_w_in, ffn_conv_w=v_ffn_conv_w,
               ffn_conv_b=v_ffn_conv_b, ffn_w_out=v_ffn_w_out)
    BIG = ["ret_w_in", "ret_w_out", "mla_w_in", "mla_w_qb", "mla_w_kvb", "mla_w_out", "ffn_w_in", "ffn_w_out"]
    REPL = ["ret_norm", "ffn_norm", "mla_q_head_norm", "mla_k_head_norm", "ffn_conv_b"]
    SHARDED_SMALL = ["ffn_conv_w", "ret_gn", "mla_norm", "mla_q_norm", "mla_kv_norm"]
    dev = _idx(_place())

    def blk16(k, i=0):
        return _bf16(w[k][i])

    small_vec = jnp.concatenate([w[k].reshape(-1) for k in SHARDED_SMALL])
    n_small = small_vec.shape[0]
    small_vec = jnp.pad(small_vec, (0, 4096 - n_small)).reshape(32, 128)
    Wret_in, sg = _comm_call("gather_ret_w_in", _Gather([blk16("ret_w_in"), small_vec], parts=2))
    sg = sg.reshape(NDEV, 4096)
    o0 = 0
    conv_w_full = sg[:, o0:o0 + 2112].reshape(NDEV, 2, 3, 352).transpose(1, 2, 0, 3).reshape(2, 3, FFN)
    o0 += 2112
    ret_gn_full = sg[:, o0:o0 + 256].reshape(NDEV, RET_H, 64).transpose(1, 0, 2).reshape(RET_H, 1, RET_DV)
    o0 += 256
    mla_norm_full = sg[:, o0:o0 + 128].reshape(1, D)
    o0 += 128
    q_norm_full = sg[:, o0:o0 + 48].reshape(1, MLA_QR)
    o0 += 48
    kv_norm_full = sg[:, o0:o0 + 32].reshape(1, MLA_KVR)

    cw = [conv_w_full[i].reshape(3, 4, FSH).transpose(1, 0, 2) for i in range(2)]
    cb = [ffn_conv_b[i].reshape(4, 1, FSH) for i in range(2)]
    fnorm = [ffn_norm[i].reshape(1, D) for i in range(2)]
    rtabs = _ret_tables(S)
    mtabs = _mla_tables(S)
    qh, kh = mla_q_head_norm.reshape(1, MLA_QK), mla_k_head_norm.reshape(1, MLA_QK)
    gains = (qh[:, :MLA_NOPE], _dup(qh[:, MLA_NOPE:]), kh[:, :MLA_NOPE], _dup(kh[:, MLA_NOPE:]))

    x0 = x.reshape(T, D)
    tgt = loss_target.reshape(T, D)
    proj, h0, (Wret_out, Wffn_out0) = _norm_proj(
        "ret_in", x0, ret_norm.reshape(1, D), Wret_in, _bs((1024, 768), lambda m, j: (m, j)), _sds((T, 6144), BF16),
        comm=_Gather([blk16("ret_w_out"), blk16("ffn_w_out", 0)]))
    Wret_out = Wret_out.reshape(RET_H * RET_DV, D)
    Wffn_out0 = Wffn_out0.reshape(4, FSH, D)
    (o_raw, rgt, states), (Wffn_in0,) = _ret_fwd(proj, rtabs, ret_gn_full, B, S, comm=_Gather([blk16("ffn_w_in", 0)]))
    x1 = _dense("ret_out", rgt, Wret_out, NN, F32, res=x0)
    MLA_W = ["mla_w_in", "mla_w_qb", "mla_w_kvb", "mla_w_out"]
    x2, ffn0_saved, got = _ffn_fwd(0, x1, fnorm[0], Wffn_in0, cw[0], cb[0], Wffn_out0, B, S,
                                   comm_in=_Gather([blk16(k) for k in MLA_W]))
    Wmla_in = got[0].reshape(D, MLA_QR + MLA_KVR + MLA_ROPE)
    Wq, Wkv, Wkr = Wmla_in[:, :MLA_QR], Wmla_in[:, MLA_QR:MLA_QR + MLA_KVR], Wmla_in[:, MLA_QR + MLA_KVR:]
    Wqb, Wkvb, Wmla_out = got[1:]

    h2 = _rms_fwd("mla_norm", x2, mla_norm_full)

    c_q, c_kv, k_rope = (_dense(n, h2, wm, NN, F32) for n, wm in
                         (("mla_in_q", Wq), ("mla_in_kv", Wkv), ("mla_in_kr", _dup(Wkr))))
    cqn = _rms_fwd("mla_q_norm", c_q, q_norm_full)
    ckvn = _rms_fwd("mla_kv_norm", c_kv, kv_norm_full)
    Wqb2 = jnp.concatenate([Wqb, Wqb[:, :, MLA_NOPE:]], axis=2).transpose(1, 0, 2).reshape(MLA_QR, MLA_H * MLA_PAD)
    Wkvb2 = Wkvb.transpose(1, 0, 2).reshape(MLA_KVR, MLA_H * (MLA_NOPE + MLA_V))
    Wmla_out2 = Wmla_out.reshape(D, D)
    q_raw = _dense("mla_qb", cqn, Wqb2, NN, F32, tm=1024)
    kvh = _dense("mla_kvb", ckvn, Wkvb2, NN, F32, tm=1024)
    (att, lse, qf, kf), (Wffn_in1, Wffn_out1) = _mla_fwd(
        q_raw, kvh, k_rope, gains, mtabs, B, S, comm=_Gather([blk16("ffn_w_in", 1), blk16("ffn_w_out", 1)]))
    Wffn_out1 = Wffn_out1.reshape(4, FSH, D)
    x3 = _dense("mla_out", att, Wmla_out2, NN, F32, res=x2)
    y, ffn1_saved, _ = _ffn_fwd(1, x3, fnorm[1], Wffn_in1, cw[1], cb[1], Wffn_out1, B, S)

    dy, colsq, dyb = _loss(y, tgt)
    loss_part = 0.5 * jnp.sum(colsq) / D

    dx3, dx3b, gf1, r_ffn1_out, _ = _ffn_bwd(1, dy, dyb, ffn1_saved, fnorm[1], Wffn_in1, cw[1], cb[1], Wffn_out1, B, S,
                                             first_half_early=False)
    datt = _dense("mla_out_dx", dx3b, Wmla_out2, NT, BF16)
    fh = FSH // 2
    (dq_raw, dkvh, dkr, dqgn, dqgr, dkgn, dkgr), (r_ffn1_in_a, r_ffn1_in_b) = _mla_bwd(
        q_raw, kvh, k_rope, att, lse, datt, qf, kf, gains, mtabs, B, S,
        comm=_Exchange([gf1["w_in"], gf1["w_in"]], rows=[(0, fh), (fh, fh)]))
    dcqn = _dense("mla_qb_dx", dq_raw, Wqb2, NT, F32, tm=1024)
    dckvn = _dense("mla_kvb_dx", dkvh, Wkvb2, NT, F32, tm=1024)
    dcq, dg_qn = _rms_bwd("mla_q_norm_bwd", c_q, q_norm_full, dcqn)
    dckv, dg_kvn = _rms_bwd("mla_kv_norm_bwd", c_kv, kv_norm_full, dckvn)
    dqgr, dkgr = _fold(dqgr), _fold(dkgr)
    dproj2 = _bf16(jnp.concatenate([dcq, dckv, _fold(dkr)], axis=-1))
    dh2 = _dense("mla_in_dx", dproj2, Wmla_in, NT, F32)
    dx2, dg_mla_norm, dx2b = _rms_bwd("mla_norm_bwd", x2, mla_norm_full, dh2, dres=dx3, also_bf16=True)

    dx1, dx1b, gf0, r_ffn0_out, r_ffn0_in_a = _ffn_bwd(0, dx2, dx2b, ffn0_saved, fnorm[0], Wffn_in0, cw[0], cb[0],
                                                       Wffn_out0, B, S, first_half_early=True)
    drgt = _dense("ret_out_dx", dx1b, Wret_out, NT, BF16)
    dWret_out = _dense("ret_out_dw", rgt, dx1b, TN, BF16, tm=1024).reshape(NDEV, 256, D)
    (dproj, dgn_ret), (r_ffn0_in_b,) = _ret_bwd(proj, o_raw, states, drgt, rtabs, ret_gn_full, B, S,
                                                comm=_Exchange([gf0["w_in"]], rows=[(fh, fh)]))
    dWret_in, (r_ret_out,) = _mm(
        "ret_in_dw", h0, dproj, grid=(NDEV, T // 1024), a_spec=_bs((1024, D), lambda j, t: (t, 0)),
        b_spec=_bs((1024, 768), lambda j, t: (t, j)), o_spec=_bs((None, D, 768), lambda j, t: (j, 0, 0)),
        out_shape=_sds((NDEV, D, 768), BF16), dims=TN, kax=1, acc_shape=(D, 768), comm=_Exchange([dWret_out]))
    qr = D // 4
    dh0, r_ret_in_ab = _mm(
        "ret_in_dx", dproj, Wret_in, grid=(T // 1024, NDEV), a_spec=_bs((1024, 768), lambda m, k: (m, k)),
        b_spec=_bs((None, D, 768), lambda m, k: (k, 0, 0)), o_spec=_bs((1024, D), lambda m, k: (m, 0)),
        out_shape=_sds((T, D), F32), dims=NT, kax=1, acc_shape=(1024, D),
        comm=_Exchange([dWret_in, dWret_in], rows=[(0, qr), (qr, qr)]))
    (dx0, dg_ret_norm), (r_ret_in_c,) = _rms_bwd("ret_norm_bwd", x0, ret_norm.reshape(1, D), dh0, dres=dx1,
                                                 comm=_Exchange([dWret_in], rows=[(2 * qr, qr)]))
    grad_x = dx0.reshape(B, S, D)
    dWmla_out, (r_ret_in_d,) = _dense("mla_out_dw", att, dx3b, TN, BF16, tm=1024,
                                      comm=_Exchange([dWret_in], rows=[(3 * qr, qr)]))
    dWmla_out = dWmla_out.reshape(NDEV, MLA_V, D)
    dWqb, (r_mla_out,) = _dense("mla_qb_dw", dq_raw, cqn, TN, BF16, tm=1024, comm=_Exchange([dWmla_out]))
    dWqb = dWqb.reshape(MLA_H, MLA_PAD, MLA_QR)
    dWqb = jnp.concatenate([dWqb[:, :MLA_NOPE], dWqb[:, MLA_NOPE:MLA_QK] + dWqb[:, MLA_QK:]], axis=1)
    dWkvb, (r_mla_qb,) = _dense("mla_kvb_dw", ckvn, dkvh, TN, BF16, tm=1024, comm=_Exchange([dWqb]))
    dWkvb = dWkvb.reshape(MLA_KVR, MLA_H, MLA_NOPE + MLA_V).transpose(1, 0, 2)
    dWmla_in, (r_mla_kvb,) = _dense("mla_in_dw", h2, dproj2, TN, BF16, comm=_Exchange([dWkvb]))
    dWmla_in = dWmla_in.reshape(NDEV, 128, 704)
    received = dict(ret_w_in=[*r_ret_in_ab, r_ret_in_c, r_ret_in_d], ret_w_out=[r_ret_out], mla_w_qb=[r_mla_qb],
                    mla_w_kvb=[r_mla_kvb], mla_w_out=[r_mla_out],
                    ffn_w_in=[r_ffn0_in_a, r_ffn0_in_b, r_ffn1_in_a, r_ffn1_in_b], ffn_w_out=[r_ffn0_out, r_ffn1_out])

    dconv_w = jnp.stack([g_["cw"].transpose(1, 0, 2).reshape(3, FFN) for g_ in (gf0, gf1)])
    dconv_b = jnp.stack([g_["cb"].reshape(FFN) for g_ in (gf0, gf1)])
    small_parts = [dg_ret_norm, gf0["norm"], gf1["norm"], dg_mla_norm, dg_qn, dg_kvn, dqgn, dqgr, dkgn, dkgr, dgn_ret,
                   dconv_w, dconv_b, loss_part]
    small_g = jnp.concatenate([p.reshape(-1) for p in small_parts])
    n_grads = small_g.shape[0] - 1
    small_g = jnp.pad(small_g, (0, 240 * 128 - small_g.shape[0])).reshape(240, 128)
    small_all, received["mla_w_in"] = _comm_call("gather_small_grads", _Both(_Gather([small_g]), _Exchange([dWmla_in])))
    received["mla_w_in"] = [received["mla_w_in"]]
    sred = _sum8("sum_small_grads", small_all).reshape(-1)
    loss = sred[n_grads]

    def take(n):
        nonlocal off
        out = sred[off:off + n]
        off += n
        return out

    off = 0
    g_small = dict(ret_norm=take(D).reshape(1, D), ffn_norm=take(2 * D).reshape(2, D), mla_norm=take(D),
                   mla_q_norm=take(MLA_QR), mla_kv_norm=take(MLA_KVR))
    g_small["mla_q_head_norm"] = take(MLA_QK).reshape(1, MLA_QK)
    g_small["mla_k_head_norm"] = take(MLA_QK).reshape(1, MLA_QK)
    g_small["ret_gn"] = take(RET_H * RET_DV).reshape(1, RET_H, RET_DV)
    g_small["ffn_conv_w"] = take(2 * 3 * FFN).reshape(2, 3, FFN)
    g_small["ffn_conv_b"] = take(2 * FFN).reshape(2, FFN)
    g_small["mla_norm"] = lax.dynamic_slice(g_small["mla_norm"], (dev * 128,), (128,)).reshape(1, 128)
    g_small["mla_q_norm"] = lax.dynamic_slice(g_small["mla_q_norm"], (dev * 48,), (48,)).reshape(1, 48)
    g_small["mla_kv_norm"] = lax.dynamic_slice(g_small["mla_kv_norm"], (dev * 32,), (32,)).reshape(1, 32)
    g_small["ret_gn"] = lax.dynamic_slice(g_small["ret_gn"], (0, 0, dev * 64), (1, RET_H, 64))
    g_small["ffn_conv_w"] = lax.dynamic_slice(g_small["ffn_conv_w"], (0, 0, dev * 352), (2, 3, 352))

    grads, delta, new_m, new_v = {}, {}, {}, {}
    for k in BIG:
        rcs = received[k]
        tr = max(t for t in range(16, 257, 16) if rcs[0].shape[1] % t == 0)
        flip = (lambda t: t.transpose(0, 2, 1)) if k in ("ffn_w_in", "mla_w_qb") else (lambda t: t)
        res = _adamw(f"adamw_{k}", rcs, flip(w[k]), flip(mom[k]), flip(var[k]), tr=tr)
        grads[k], delta[k], new_m[k], new_v[k] = (flip(t) for t in res)
    SMALL = REPL + SHARDED_SMALL

    def pack(d):
        vflat = jnp.concatenate([d[k].reshape(-1) for k in SMALL])
        return jnp.pad(vflat, (0, 96 * 128 - vflat.shape[0])).reshape(1, 96, 128)

    ps = _adamw("adamw_small", [pack(g_small)], pack(w), pack(mom), pack(var))
    off = 0
    for k in SMALL:
        n = w[k].size
        grads[k], delta[k], new_m[k], new_v[k] = (t.reshape(-1)[off:off + n].reshape(w[k].shape) for t in ps)
        off += n
    names = list(w)
    return (loss, grad_x, *[grads[k] for k in names], *[delta[k] for k in names], *[new_m[k] for k in names],
            *[new_v[k] for k in names])
```

```python
import functools

import jax
import jax.numpy as jnp
from jax import lax
from jax.experimental import pallas as pl
from jax.experimental.pallas import tpu as pltpu

F32, BF16 = jnp.float32, jnp.bfloat16

NDEV = 8
D_MODEL = 1024
CHUNK = 64
RMS_EPS = 1e-6
ROPE_THETA = 10000.0
RET_H, RET_DK, RET_DV = 4, 256, 512
RET_SC = 256
MLA_H, MLA_QR, MLA_KVR = 8, 384, 256
MLA_NOPE, MLA_ROPE, MLA_V = 128, 64, 128
MLA_QK = MLA_NOPE + MLA_ROPE
MASK_VALUE = -1e30
FFN = 2816
FSH = FFN * 2 // NDEV
ATT_TQ = 256
ADAM_LR, ADAM_B1, ADAM_B2, ADAM_EPS, ADAM_WD, ADAM_STEP = 0.001, 0.9, 0.999, 1e-08, 0.01, 10
MESH = pl.DeviceIdType.MESH
VMEM_LIMIT = 56 * 2 ** 20


def _cp(sem):
    return pltpu.CompilerParams(dimension_semantics=sem, vmem_limit_bytes=VMEM_LIMIT)


def _dot(a, b, dims):
    return lax.dot_general(a, b, (dims, ((), ())), preferred_element_type=F32)


NN = ((1,), (0,))
NT = ((1,), (1,))
TN = ((0,), (0,))


def _place():
    return lax.axis_index("x"), lax.axis_index("y"), lax.axis_index("c")


def _idx(d):
    return 4 * d[0] + 2 * d[1] + d[2]


ANY = pl.BlockSpec(memory_space=pl.ANY)


class _Gather:
    sem0 = 0

    def __init__(self, arrs, parts=1):
        self.srcs = list(arrs)
        self.parts = parts
        self.nsem = len(arrs) * parts
        self.out_shape = [jax.ShapeDtypeStruct((NDEV,) + a.shape, a.dtype) for a in arrs]

    def _copies(self, ins, outs, send, recv, loc):
        n = self.nsem
        s0 = self.sem0
        x, y, c = _place()
        me, sib = (x, y, c), (x, y, 1 - c)
        chips = [(1 - x, y), (x, 1 - y), (1 - x, 1 - y)]

        def piece(ref, v, *lead):
            a, p = divmod(v, self.parts)
            if self.parts > 1:
                rows = self.srcs[a].shape[0] // self.parts
                lead = (*lead, pl.ds(p * rows, rows))
            return ref[a].at[lead] if lead else ref[a]

        def cp(a, k, block, to, src=None):
            dst = piece(outs, a, _idx(block))
            return pltpu.make_async_remote_copy(src_ref=dst if src is None else src, dst_ref=dst,
                                                send_sem=send.at[s0 + a, k], recv_sem=recv.at[s0 + a, k], device_id=to,
                                                device_id_type=MESH)

        own = [piece(ins, a) for a in range(n)]
        mine = [pltpu.make_async_copy(own[a], piece(outs, a, _idx(me)), loc.at[s0 + a]) for a in range(n)]
        first = [cp(a, 0, me, sib, src=own[a]) for a in range(n)]
        first += [cp(a, 1 + j, me, (*chip, c), src=own[a]) for a in range(n) for j, chip in enumerate(chips)]
        landed = [cp(a, 1 + j, (*chip, c), me) for j, chip in enumerate(chips) for a in range(n)]
        passed = [cp(a, 4 + j, (*chip, c), sib) for j, chip in enumerate(chips) for a in range(n)]
        from_sib = [cp(a, 0, sib, me) for a in range(n)]
        from_sib += [cp(a, 4 + j, (*chip, 1 - c), me) for j, chip in enumerate(chips) for a in range(n)]
        return mine, first, landed, passed, from_sib

    def start(self, *refs):
        mine, first, _, _, _ = self._copies(*refs)
        for cp in mine + first:
            cp.start()

    def mid(self, *refs):
        _, _, landed, passed, _ = self._copies(*refs)
        for got, on in zip(landed, passed):
            got.wait_recv()
            on.start()

    def finish(self, *refs):
        mine, first, _, passed, from_sib = self._copies(*refs)
        for cp in from_sib:
            cp.wait_recv()
        for cp in first + passed:
            cp.wait_send()
        for cp in mine:
            cp.wait()


class _Exchange:
    sem0 = 0

    def __init__(self, arrs, rows=None):
        self.srcs = list(arrs)
        self.nsem = len(arrs)
        self.rows = rows if rows is not None else [None] * len(arrs)
        self.out_shape = [jax.ShapeDtypeStruct(a.shape if r is None else (a.shape[0], r[1]) + a.shape[2:], a.dtype)
                          for a, r in zip(arrs, self.rows)]

    def _copies(self, ins, outs, send, recv, loc):
        n = len(self.srcs)
        s0 = self.sem0
        x, y, c = _place()
        me = _idx((x, y, c))

        def src(a, q):
            r = self.rows[a]
            return ins[a].at[q] if r is None else ins[a].at[q, pl.ds(r[0], r[1])]

        mine = [pltpu.make_async_copy(src(a, me), outs[a].at[me], loc.at[s0 + a]) for a in range(n)]
        remote = []
        for k in range(1, NDEV):
            peer = (x ^ (k >> 2), y ^ ((k >> 1) & 1), c ^ (k & 1))
            remote += [pltpu.make_async_remote_copy(
                src_ref=src(a, _idx(peer)), dst_ref=outs[a].at[me], send_sem=send.at[s0 + a, k - 1],
                recv_sem=recv.at[s0 + a, k - 1], device_id=peer, device_id_type=MESH) for a in range(n)]
        return mine, remote

    def start(self, *refs):
        mine, remote = self._copies(*refs)
        for cp in mine + remote:
            cp.start()

    def mid(self, *refs):
        pass

    def finish(self, *refs):
        mine, remote = self._copies(*refs)
        for cp in remote + mine:
            cp.wait()


class _Both:
    def __init__(self, one, two):
        self.parts = (one, two)
        two.sem0 = one.nsem
        self.nsem = one.nsem + two.nsem
        self.srcs = one.srcs + two.srcs
        self.out_shape = one.out_shape + two.out_shape

    def _each(self, phase, ins, outs, send, recv, loc):
        n = len(self.parts[0].srcs)
        getattr(self.parts[0], phase)(ins[:n], outs[:n], send, recv, loc)
        getattr(self.parts[1], phase)(ins[n:], outs[n:], send, recv, loc)

    def start(self, *refs):
        self._each("start", *refs)

    def mid(self, *refs):
        self._each("mid", *refs)

    def finish(self, *refs):
        self._each("finish", *refs)


def _comm_scratch(n):
    return [pltpu.SemaphoreType.DMA((n, 7)), pltpu.SemaphoreType.DMA((n, 7)), pltpu.SemaphoreType.DMA((n,))]


def _comm_call(name, comm):
    n = len(comm.srcs)

    def body(*refs):
        parts = (refs[:n], refs[n:2 * n]) + tuple(refs[2 * n:])
        comm.start(*parts)
        comm.mid(*parts)
        comm.finish(*parts)

    return pl.pallas_call(body, name=name, in_specs=[ANY] * n, out_specs=[ANY] * n, out_shape=comm.out_shape,
                          scratch_shapes=_comm_scratch(comm.nsem))(*comm.srcs)


def _pcall(body, *, name, grid, in_specs, out_specs, out_shape, scratch_shapes, sem, args, comm=None):
    if comm is None:
        return pl.pallas_call(body, name=name, grid=grid, in_specs=in_specs, out_specs=out_specs, out_shape=out_shape,
                              scratch_shapes=scratch_shapes, compiler_params=_cp(sem))(*args), None
    ni, no, ns, nc = len(in_specs), len(out_shape), len(scratch_shapes), len(comm.srcs)
    total = 1
    for g in grid:
        total *= g
    middle = (4 * total) // 5

    def wrapped(*refs):
        ins, csrc = refs[:ni], refs[ni:ni + nc]
        outs, cdst = refs[ni + nc:ni + nc + no], refs[ni + nc + no:ni + 2 * nc + no]
        scr, sems = refs[ni + 2 * nc + no:ni + 2 * nc + no + ns], refs[ni + 2 * nc + no + ns:]
        step = pl.program_id(0)
        for k in range(1, len(grid)):
            step = step * grid[k] + pl.program_id(k)
        parts = (csrc, cdst) + tuple(sems)

        @pl.when(step == 0)
        def _():
            comm.start(*parts)

        body(*ins, *outs, *scr)

        @pl.when(step == middle)
        def _():
            comm.mid(*parts)

        @pl.when(step == total - 1)
        def _():
            comm.finish(*parts)

    res = pl.pallas_call(
        wrapped, name=name, grid=grid, in_specs=list(in_specs) + [ANY] * nc, out_specs=list(out_specs) + [ANY] * nc,
        out_shape=list(out_shape) + comm.out_shape, scratch_shapes=list(scratch_shapes) + _comm_scratch(comm.nsem),
        compiler_params=_cp(("arbitrary",) * len(grid)))(*args, *comm.srcs)
    return res[:no], res[no:]


def _mm(name, a, b, *, grid, a_spec, b_spec, o_spec, out_shape, dims, kax=None, res=None, res_spec=None,
        jb=0, acc_shape=None, comm=None):
    nk = grid[kax] if kax is not None else 1

    def body(*refs):
        if res is not None:
            a_ref, b_ref, r_ref, o_ref = refs[:4]
        else:
            a_ref, b_ref, o_ref = refs[:3]

        def product():
            if not jb:
                return _dot(a_ref[...], b_ref[...], dims)
            part = _dot(a_ref[0], b_ref[0], dims)
            for j in range(1, jb):
                part = part + _dot(a_ref[j], b_ref[j], dims)
            return part

        def fin(acc):
            if res is not None:
                acc = acc + r_ref[...]
            o_ref[...] = acc.astype(o_ref.dtype)

        if nk == 1:
            fin(product())
        else:
            acc_ref = refs[-1]
            k = pl.program_id(kax)

            @pl.when(k == 0)
            def _():
                acc_ref[...] = jnp.zeros_like(acc_ref)

            acc_ref[...] += product()

            @pl.when(k == nk - 1)
            def _():
                fin(acc_ref[...])

    sem = tuple("arbitrary" if i == kax else "parallel" for i in range(len(grid)))
    in_specs = [a_spec, b_spec] + ([res_spec] if res is not None else [])
    args = (a, b) + ((res,) if res is not None else ())
    scratch = [pltpu.VMEM(acc_shape, F32)] if nk > 1 else []
    (out,), got = _pcall(body, name=name, grid=grid, in_specs=in_specs, out_specs=[o_spec], out_shape=[out_shape],
                         scratch_shapes=scratch, sem=sem, args=args, comm=comm)
    return out if comm is None else (out, got)


def _bs(shape, fn):
    return pl.BlockSpec(shape, fn)


def _rms_fwd(name, x, g, tm=512):
    T, D = x.shape

    def body(x_ref, g_ref, o_ref):
        xf = x_ref[...]
        r = lax.rsqrt(jnp.mean(xf * xf, axis=-1, keepdims=True) + RMS_EPS)
        o_ref[...] = (xf * r * g_ref[...]).astype(o_ref.dtype)

    return pl.pallas_call(
        body, name=name, grid=(T // tm,),
        in_specs=[_bs((tm, D), lambda i: (i, 0)), _bs((1, D), lambda i: (0, 0))],
        out_specs=_bs((tm, D), lambda i: (i, 0)), out_shape=jax.ShapeDtypeStruct((T, D), BF16),
        compiler_params=_cp(("parallel",)))(x, g)


def _rms_bwd(name, x, g, dh, dres=None, tm=512, also_bf16=False, comm=None):
    T, D = x.shape

    def body(*refs):
        if also_bf16:
            refs, dxb_ref = refs[:-1], refs[-1]
        if dres is not None:
            x_ref, g_ref, dh_ref, dres_ref, dx_ref, dg_ref = refs
        else:
            x_ref, g_ref, dh_ref, dx_ref, dg_ref = refs
        i = pl.program_id(0)
        xf = x_ref[...]
        r = lax.rsqrt(jnp.mean(xf * xf, axis=-1, keepdims=True) + RMS_EPS)
        xh = xf * r
        d = dh_ref[...].astype(F32)
        dxh = d * g_ref[...]
        dx = r * (dxh - xh * jnp.mean(dxh * xh, axis=-1, keepdims=True))
        if dres is not None:
            dx = dx + dres_ref[...]
        dx_ref[...] = dx
        if also_bf16:
            dxb_ref[...] = dx.astype(BF16)
        part = jnp.sum(d * xh, axis=0, keepdims=True)

        @pl.when(i == 0)
        def _():
            dg_ref[...] = part

        @pl.when(i > 0)
        def _():
            dg_ref[...] += part

    row = _bs((tm, D), lambda i: (i, 0))
    vec = _bs((1, D), lambda i: (0, 0))
    in_specs = [row, vec, row] + ([row] if dres is not None else [])
    args = (x, g, dh) + ((dres,) if dres is not None else ())
    extra = [jax.ShapeDtypeStruct((T, D), BF16)] if also_bf16 else []
    outs, got = _pcall(
        body, name=name, grid=(T // tm,), in_specs=in_specs, out_specs=[row, vec] + [row] * len(extra),
        out_shape=[jax.ShapeDtypeStruct((T, D), F32), jax.ShapeDtypeStruct((1, D), F32)] + extra, scratch_shapes=[],
        sem=("arbitrary",), args=args, comm=comm)
    return outs if comm is None else (outs, got)


def _loss(y, tgt, tm=512):
    T, D = y.shape

    def body(y_ref, t_ref, dy_ref, s_ref, dyb_ref):
        i = pl.program_id(0)
        e = y_ref[...] - t_ref[...]
        dy = e * (1.0 / D)
        dy_ref[...] = dy
        dyb_ref[...] = dy.astype(BF16)
        part = jnp.sum(e * e, axis=0, keepdims=True)

        @pl.when(i == 0)
        def _():
            s_ref[...] = part

        @pl.when(i > 0)
        def _():
            s_ref[...] += part

    row = _bs((tm, D), lambda i: (i, 0))
    return pl.pallas_call(
        body, name="loss_head", grid=(T // tm,), in_specs=[row, row],
        out_specs=[row, _bs((1, D), lambda i: (0, 0)), row],
        out_shape=[jax.ShapeDtypeStruct((T, D), F32), jax.ShapeDtypeStruct((1, D), F32),
                   jax.ShapeDtypeStruct((T, D), BF16)],
        compiler_params=_cp(("arbitrary",)))(y, tgt)


def _shift_rows(t, k, row):
    return jnp.where(row >= k, pltpu.roll(t, k, 0), 0.0)


def _shift_rows_up(t, k, row, n):
    return jnp.where(row < n - k, pltpu.roll(t, n - k, 0), 0.0)


def _convffn_fwd(name, u, cw, cb, B, S):
    _, J, T, F = u.shape

    def body(u_ref, cw_ref, cb_ref, o_ref):
        a = u_ref[0].astype(F32)
        g = u_ref[1].astype(F32)
        row = lax.broadcasted_iota(jnp.int32, (S, F), 0)
        w0, w1, w2 = cw_ref[0:1, :], cw_ref[1:2, :], cw_ref[2:3, :]
        gc = _shift_rows(g, 2, row) * w0 + _shift_rows(g, 1, row) * w1 + g * w2 + cb_ref[...]
        o_ref[...] = (gc * jax.nn.sigmoid(gc) * a).astype(o_ref.dtype)

    return pl.pallas_call(
        body, name=name, grid=(J, B),
        in_specs=[_bs((2, None, S, F), lambda j, b: (0, j, b, 0)), _bs((None, 3, F), lambda j, b: (j, 0, 0)),
                  _bs((None, 1, F), lambda j, b: (j, 0, 0))],
        out_specs=_bs((None, S, F), lambda j, b: (j, b, 0)), out_shape=jax.ShapeDtypeStruct((J, T, F), BF16),
        compiler_params=_cp(("parallel", "parallel")))(u, cw, cb)


def _convffn_bwd(name, u, cw, cb, dgt, B, S, comm=None):
    _, J, T, F = u.shape

    def body(u_ref, cw_ref, cb_ref, d_ref, du_ref, dcw_ref, dcb_ref):
        b = pl.program_id(1)
        a = u_ref[0].astype(F32)
        g = u_ref[1].astype(F32)
        d = d_ref[...].astype(F32)
        row = lax.broadcasted_iota(jnp.int32, (S, F), 0)
        w0, w1, w2 = cw_ref[0:1, :], cw_ref[1:2, :], cw_ref[2:3, :]
        g1, g2 = _shift_rows(g, 1, row), _shift_rows(g, 2, row)
        gc = g2 * w0 + g1 * w1 + g * w2 + cb_ref[...]
        sg = jax.nn.sigmoid(gc)
        du_ref[0] = (d * gc * sg).astype(du_ref.dtype)
        dgc = d * a * (sg * (1.0 + gc * (1.0 - sg)))
        dg = dgc * w2 + _shift_rows_up(dgc, 1, row, S) * w1 + _shift_rows_up(dgc, 2, row, S) * w0
        du_ref[1] = dg.astype(du_ref.dtype)
        parts = [jnp.sum(dgc * g2, axis=0, keepdims=True), jnp.sum(dgc * g1, axis=0, keepdims=True),
                 jnp.sum(dgc * g, axis=0, keepdims=True)]
        pb = jnp.sum(dgc, axis=0, keepdims=True)

        @pl.when(b == 0)
        def _():
            for k in range(3):
                dcw_ref[k:k + 1, :] = parts[k]
            dcb_ref[...] = pb

        @pl.when(b > 0)
        def _():
            for k in range(3):
                dcw_ref[k:k + 1, :] += parts[k]
            dcb_ref[...] += pb

    uspec = _bs((2, None, S, F), lambda j, b: (0, j, b, 0))
    return _pcall(
        body, name=name, grid=(J, B),
        in_specs=[uspec, _bs((None, 3, F), lambda j, b: (j, 0, 0)), _bs((None, 1, F), lambda j, b: (j, 0, 0)),
                  _bs((None, S, F), lambda j, b: (j, b, 0))],
        out_specs=[uspec, _bs((None, 3, F), lambda j, b: (j, 0, 0)), _bs((None, 1, F), lambda j, b: (j, 0, 0))],
        out_shape=[jax.ShapeDtypeStruct(u.shape, BF16), jax.ShapeDtypeStruct((J, 3, F), F32),
                   jax.ShapeDtypeStruct((J, 1, F), F32)],
        scratch_shapes=[], sem=("parallel", "arbitrary"), args=(u, cw, cb, dgt), comm=comm)


def _ret_tables(S):
    half = RET_DK // 2
    inv = ROPE_THETA ** (-jnp.arange(half, dtype=F32) / half)
    ang = jnp.arange(S).astype(F32)[:, None] * inv[None, :]
    lg = jnp.log1p(-jnp.exp2(-5.0 - jnp.arange(RET_H, dtype=F32)))
    i = jnp.arange(RET_SC, dtype=F32)
    same_or_earlier = (jnp.floor(i[None, :] / CHUNK) <= jnp.floor(i[:, None] / CHUNK)).astype(F32)
    dm = jnp.exp(lg[:, None, None] * jnp.abs(i[:, None] - i[None, :])) * same_or_earlier[None]
    qd = jnp.exp(lg[:, None] * (i + 1.0))[:, :, None]
    kd = jnp.exp(lg[:, None] * (RET_SC - 1.0 - i))[:, :, None]
    cd = jnp.exp(lg * RET_SC)[:, None, None]
    return jnp.cos(ang), jnp.sin(ang), dm, qd, kd, cd


def _rope_halves(t, cs, sn):
    h = t.shape[-1] // 2
    t1, t2 = t[:, :h], t[:, h:]
    return jnp.concatenate([t1 * cs - t2 * sn, t2 * cs + t1 * sn], axis=-1)


def _unrope_halves(d, cs, sn):
    h = d.shape[-1] // 2
    d1, d2 = d[:, :h], d[:, h:]
    return jnp.concatenate([d1 * cs + d2 * sn, d2 * cs - d1 * sn], axis=-1)


def _ret_specs(nC, order):
    SC = RET_SC

    def sp(shape, fn):
        return _bs(shape, lambda *g: fn(*order(*g)))

    q = sp((SC, RET_DK), lambda b, h, c: (b * nC + c, h))
    k = sp((SC, RET_DK), lambda b, h, c: (b * nC + c, RET_H + h))
    v = sp((SC, RET_DV), lambda b, h, c: (b * nC + c, RET_H + h))
    g = sp((SC, RET_DV), lambda b, h, c: (b * nC + c, 2 * RET_H + h))
    cs = sp((SC, RET_DK // 2), lambda b, h, c: (c, 0))
    dm = sp((None, SC, SC), lambda b, h, c: (h, 0, 0))
    dv = sp((None, SC, 1), lambda b, h, c: (h, 0, 0))
    cd = sp((None, 1, 1), lambda b, h, c: (h, 0, 0))
    gn = sp((None, 1, RET_DV), lambda b, h, c: (h, 0, 0))
    wide = sp((SC, RET_DV), lambda b, h, c: (b * nC + c, h))
    narrow = sp((SC, RET_DK), lambda b, h, c: (b * nC + c, h))
    st = sp((None, None, None, RET_DK, RET_DV), lambda b, h, c: (b, h, c, 0, 0))
    return dict(q=q, k=k, v=v, g=g, cs=cs, dm=dm, dv=dv, cd=cd, gn=gn, wide=wide, narrow=narrow, st=st)


def _ret_fwd(proj, tabs, gn, B, S, comm=None):
    T = B * S
    nC = S // RET_SC
    cos, sin, dm, qd, kd, cd = tabs
    s = _ret_specs(nC, lambda b, h, c: (b, h, c))

    def body(q_ref, k_ref, v_ref, g_ref, cos_ref, sin_ref, dm_ref, qd_ref, kd_ref, cd_ref, gn_ref,
             o_ref, gt_ref, st_ref, state):
        c = pl.program_id(2)

        @pl.when(c == 0)
        def _():
            state[...] = jnp.zeros_like(state)

        cs, sn = cos_ref[...], sin_ref[...]
        qf = _rope_halves(q_ref[...].astype(F32), cs, sn)
        kf = _rope_halves(k_ref[...].astype(F32), cs, sn) * (RET_DK ** -0.5)
        v = v_ref[...]
        p = _dot(qf.astype(BF16), kf.astype(BF16), NT) * dm_ref[...]
        st = state[...]
        stb = st.astype(BF16)
        st_ref[...] = stb
        o = _dot(p.astype(BF16), v, NN) + _dot((qf * qd_ref[...]).astype(BF16), stb, NN)
        state[...] = st * cd_ref[...] + _dot((kf * kd_ref[...]).astype(BF16), v, TN)
        o_ref[...] = o
        r = lax.rsqrt(jnp.mean(o * o, axis=-1, keepdims=True) + RMS_EPS)
        gf = g_ref[...].astype(F32)
        gt_ref[...] = ((o * r * gn_ref[...]) * (gf * jax.nn.sigmoid(gf))).astype(BF16)

    return _pcall(
        body, name="ret_fwd", grid=(B, RET_H, nC),
        in_specs=[s["q"], s["k"], s["v"], s["g"], s["cs"], s["cs"], s["dm"], s["dv"], s["dv"], s["cd"], s["gn"]],
        out_specs=[s["wide"], s["wide"], s["st"]],
        out_shape=[jax.ShapeDtypeStruct((T, RET_H * RET_DV), F32), jax.ShapeDtypeStruct((T, RET_H * RET_DV), BF16),
                   jax.ShapeDtypeStruct((B, RET_H, nC, RET_DK, RET_DV), BF16)],
        scratch_shapes=[pltpu.VMEM((RET_DK, RET_DV), F32)], sem=("parallel", "parallel", "arbitrary"),
        args=(proj, proj, proj, proj, cos, sin, dm, qd, kd, cd, gn), comm=comm)


def _ret_bwd(proj, o_raw, states, dgt, tabs, gn, B, S, comm=None):
    T = B * S
    nC = S // RET_SC
    cos, sin, dm, qd, kd, cd = tabs
    s = _ret_specs(nC, lambda b, c, h: (b, h, nC - 1 - c))

    def body(q_ref, k_ref, v_ref, g_ref, o_ref, st_ref, d_ref, cos_ref, sin_ref, dm_ref, qd_ref, kd_ref, cd_ref,
             gn_ref, dproj_ref, dgn_ref, dstates):
        b, c, h = pl.program_id(0), pl.program_id(1), pl.program_id(2)
        dstate = dstates.at[h]

        @pl.when(c == 0)
        def _():
            dstate[...] = jnp.zeros_like(dstate)

        @pl.when((b == 0) & (c == 0))
        def _():
            dgn_ref[h] = jnp.zeros((1, RET_DV), F32)

        cs, sn = cos_ref[...], sin_ref[...]
        qf = _rope_halves(q_ref[...].astype(F32), cs, sn)
        kf = _rope_halves(k_ref[...].astype(F32), cs, sn) * (RET_DK ** -0.5)
        v = v_ref[...]
        gnv = gn_ref[h]
        o = o_ref[...]
        r = lax.rsqrt(jnp.mean(o * o, axis=-1, keepdims=True) + RMS_EPS)
        oh = o * r
        gf = g_ref[...].astype(F32)
        sg = jax.nn.sigmoid(gf)
        d = d_ref[...].astype(F32)
        dg = (d * (oh * gnv) * (sg * (1.0 + gf * (1.0 - sg)))).astype(BF16)
        don = d * (gf * sg)
        dgn_ref[h] += jnp.sum(don * oh, axis=0, keepdims=True)
        doh = don * gnv
        dO = (r * (doh - oh * jnp.mean(doh * oh, axis=-1, keepdims=True))).astype(BF16)
        dmv = dm_ref[h]
        qb, kb = qf.astype(BF16), kf.astype(BF16)
        p = (_dot(qb, kb, NT) * dmv).astype(BF16)
        dp = (_dot(dO, v, NT) * dmv).astype(BF16)
        st = st_ref[...]
        dsn = dstate[...]
        dsb = dsn.astype(BF16)
        qdv, kdv = qd_ref[h], kd_ref[h]
        dq = _dot(dp, kb, NN) + _dot(dO, st, NT) * qdv
        dk = _dot(dp, qb, TN) + _dot(v, dsb, NT) * kdv
        dv = _dot(p, dO, TN) + _dot((kf * kdv).astype(BF16), dsb, NN)
        dstate[...] = dsn * cd_ref[h] + _dot((qf * qdv).astype(BF16), dO, TN)
        dq = _unrope_halves(dq, cs, sn).astype(BF16)
        dk = (_unrope_halves(dk, cs, sn) * (RET_DK ** -0.5)).astype(BF16)
        dv = dv.astype(BF16)
        nq, nv = RET_H * RET_DK, RET_H * RET_DV
        for hh in range(RET_H):
            @pl.when(h == hh)
            def _():
                dproj_ref[:, hh * RET_DK:(hh + 1) * RET_DK] = dq
                dproj_ref[:, nq + hh * RET_DK:nq + (hh + 1) * RET_DK] = dk
                dproj_ref[:, 2 * nq + hh * RET_DV:2 * nq + (hh + 1) * RET_DV] = dv
                dproj_ref[:, 2 * nq + nv + hh * RET_DV:2 * nq + nv + (hh + 1) * RET_DV] = dg

    width = 2 * RET_H * (RET_DK + RET_DV)

    def all_heads(*shape):
        return _bs((RET_H,) + shape, lambda b, c, h: (0,) * (1 + len(shape)))

    return _pcall(
        body, name="ret_bwd", grid=(B, nC, RET_H),
        in_specs=[s["q"], s["k"], s["v"], s["g"], s["wide"], s["st"], s["wide"], s["cs"], s["cs"],
                  all_heads(RET_SC, RET_SC), all_heads(RET_SC, 1), all_heads(RET_SC, 1), all_heads(1, 1),
                  all_heads(1, RET_DV)],
        out_specs=[_bs((RET_SC, width), lambda b, c, h: (b * nC + nC - 1 - c, 0)),
                   _bs((RET_H, 1, RET_DV), lambda b, c, h: (0, 0, 0))],
        out_shape=[jax.ShapeDtypeStruct((T, width), BF16), jax.ShapeDtypeStruct((RET_H, 1, RET_DV), F32)],
        scratch_shapes=[pltpu.VMEM((RET_H, RET_DK, RET_DV), F32)], sem=("arbitrary", "arbitrary", "arbitrary"),
        args=(proj, proj, proj, proj, o_raw, states, dgt, cos, sin, dm, qd, kd, cd, gn), comm=comm)


MLA_PAD = 256
MLA_R2 = 2 * MLA_ROPE


def _dup(t):
    return jnp.concatenate([t, t], axis=-1)


def _fold(t):
    return t[..., :MLA_ROPE] + t[..., MLA_ROPE:]


def _mla_tables(S):
    half = MLA_ROPE // 2
    inv = ROPE_THETA ** (-jnp.arange(half, dtype=F32) / half)
    ang = jnp.arange(S).astype(F32)[:, None] * inv[None, :]
    cos, sin, zero = jnp.cos(ang), jnp.sin(ang), jnp.zeros((S, MLA_ROPE), F32)
    return jnp.concatenate([cos, cos, zero], axis=-1), jnp.concatenate([-sin, sin, zero], axis=-1)


def _head_norm_rope(n, r2, gn, gr2, cos, sin, scale):
    ssq = jnp.sum(n * n, axis=-1, keepdims=True) + 0.5 * jnp.sum(r2 * r2, axis=-1, keepdims=True)
    rstd = lax.rsqrt(ssq * (1.0 / MLA_QK) + RMS_EPS)
    yn = n * rstd * gn
    yr = r2 * rstd * gr2
    z = yr * cos + pltpu.roll(yr, MLA_ROPE // 2, 1) * sin
    if scale != 1.0:
        yn, z = yn * scale, z * scale
    return yn, z


def _head_norm_rope_bwd(dn, dz, n, r2, gn, gr2, cos, sin, scale):
    ssq = jnp.sum(n * n, axis=-1, keepdims=True) + 0.5 * jnp.sum(r2 * r2, axis=-1, keepdims=True)
    rstd = lax.rsqrt(ssq * (1.0 / MLA_QK) + RMS_EPS)
    hn, hr = n * rstd, r2 * rstd
    if scale != 1.0:
        dn, dz = dn * scale, dz * scale
    dyr = dz * cos + pltpu.roll(dz * sin, MLA_R2 - MLA_ROPE // 2, 1)
    dgn = jnp.sum(dn * hn, axis=0, keepdims=True)
    dgr = jnp.sum(dyr * hr, axis=0, keepdims=True)
    dhn, dhr = dn * gn, dyr * gr2
    mt = (jnp.sum(dhn * hn, axis=-1, keepdims=True) + jnp.sum(dhr * hr, axis=-1, keepdims=True)) * (1.0 / MLA_QK)
    return rstd * (dhn - hn * mt), rstd * (dhr - 0.5 * hr * mt), dgn, dgr


def _diag_bias():
    i = jnp.arange(ATT_TQ)
    return jnp.where((i[None, :] // CHUNK) <= (i[:, None] // CHUNK), 0.0, MASK_VALUE).astype(F32)


def _store_pair(dst, rows, n, r2):
    dst[rows, :MLA_NOPE] = n.astype(BF16)
    dst[rows, MLA_NOPE:] = r2.astype(BF16)


def _mla_fwd(q_raw, kv, kr, gains, tabs, B, S, comm=None):
    T = B * S
    TQ = ATT_TQ
    nQ = S // TQ
    qgn, qgr, kgn, kgr = gains
    cos, sin = tabs
    scale = MLA_QK ** -0.5

    def body(q_ref, kv_ref, kr_ref, qgn_ref, qgr_ref, kgn_ref, kgr_ref, c_ref, s_ref, bias_ref,
             o_ref, lse_ref, qf_s, kf_s, v_s):
        def prep(t, _):
            rows = pl.ds(pl.multiple_of(t * TQ, TQ), TQ)
            cs, sn = c_ref[rows, :], s_ref[rows, :]
            qn, qr = _head_norm_rope(q_ref[rows, :MLA_NOPE], q_ref[rows, MLA_NOPE:], qgn_ref[...], qgr_ref[...],
                                     cs, sn, scale)
            _store_pair(qf_s, rows, qn, qr)
            kn, krr = _head_norm_rope(kv_ref[rows, :MLA_NOPE], kr_ref[rows, :], kgn_ref[...], kgr_ref[...], cs, sn, 1.0)
            _store_pair(kf_s, rows, kn, krr)
            v_s[rows, :] = kv_ref[rows, MLA_NOPE:].astype(BF16)
            return 0

        lax.fori_loop(0, nQ, prep, 0, unroll=2)
        for i in range(nQ):
            rows = slice(i * TQ, (i + 1) * TQ)
            q = qf_s[rows, :]
            sd = _dot(q, kf_s[rows, :], NT) + bias_ref[...]
            m = jnp.max(sd, axis=-1, keepdims=True)
            if i:
                sl = _dot(q, kf_s[:i * TQ, :], NT)
                m = jnp.maximum(m, jnp.max(sl, axis=-1, keepdims=True))
            pd = jnp.exp(sd - m)
            l = jnp.sum(pd, axis=-1, keepdims=True)
            acc = _dot(pd.astype(BF16), v_s[rows, :], NN)
            if i:
                pl_ = jnp.exp(sl - m)
                l = l + jnp.sum(pl_, axis=-1, keepdims=True)
                acc = acc + _dot(pl_.astype(BF16), v_s[:i * TQ, :], NN)
            o_ref[rows, :] = (acc / l).astype(BF16)
            lse_ref[rows, :] = m + jnp.log(l)

    def vec(n):
        return _bs((1, n), lambda b, h: (0, 0))

    def cols(n):
        return _bs((S, n), lambda b, h: (b, h))

    tab = _bs((S, MLA_R2), lambda b, h: (0, 0))
    return _pcall(
        body, name="mla_fwd", grid=(B, MLA_H),
        in_specs=[cols(MLA_PAD), cols(MLA_NOPE + MLA_V), _bs((S, MLA_R2), lambda b, h: (b, 0)),
                  vec(MLA_NOPE), vec(MLA_R2), vec(MLA_NOPE), vec(MLA_R2), tab, tab,
                  _bs((TQ, TQ), lambda b, h: (0, 0))],
        out_specs=[cols(MLA_V), _bs((None, S, 1), lambda b, h: (h, b, 0)), cols(MLA_PAD), cols(MLA_PAD)],
        out_shape=[jax.ShapeDtypeStruct((T, MLA_H * MLA_V), BF16), jax.ShapeDtypeStruct((MLA_H, T, 1), F32),
                   jax.ShapeDtypeStruct((T, MLA_H * MLA_PAD), BF16), jax.ShapeDtypeStruct((T, MLA_H * MLA_PAD), BF16)],
        scratch_shapes=[pltpu.VMEM((S, MLA_V), BF16)],
        sem=("parallel", "parallel"), args=(q_raw, kv, kr, qgn, qgr, kgn, kgr, cos, sin, _diag_bias()), comm=comm)


def _mla_bwd(q_raw, kv, kr, o, lse, do, qf, kf, gains, tabs, B, S, comm=None):
    T = B * S
    TQ = ATT_TQ
    nQ = S // TQ
    qgn, qgr, kgn, kgr = gains
    cos, sin = tabs
    scale = MLA_QK ** -0.5

    def body(q_ref, kv_ref, kr_ref, o_ref, lse_ref, do_ref, qf_s, kf_s, qgn_ref, qgr_ref, kgn_ref, kgr_ref, c_ref, s_ref,
             bias_ref, dq_ref, dkv_ref, dkr_ref, dqgn_ref, dqgr_ref, dkgn_ref, dkgr_ref,
             v_s, dl_s, dq_s, dk_s, dv_s):
        b, h = pl.program_id(0), pl.program_id(1)

        def blk(t):
            return pl.ds(pl.multiple_of(t * TQ, TQ), TQ)

        def prep(t, _):
            rows = blk(t)
            v_s[rows, :] = kv_ref[rows, MLA_NOPE:].astype(BF16)
            dl_s[rows, :] = jnp.sum(do_ref[rows, :].astype(F32) * o_ref[rows, :].astype(F32), axis=-1, keepdims=True)
            dk_s[rows, :] = jnp.zeros((TQ, MLA_PAD), F32)
            dv_s[rows, :] = jnp.zeros((TQ, MLA_V), F32)
            return 0

        lax.fori_loop(0, nQ, prep, 0, unroll=2)

        gqn, gqr = jnp.zeros((1, MLA_NOPE), F32), jnp.zeros((1, MLA_R2), F32)
        for i in range(nQ):
            rows = slice(i * TQ, (i + 1) * TQ)
            q, doi, lse_i, dl_i = qf_s[rows, :], do_ref[rows, :], lse_ref[rows, :], dl_s[rows, :]

            def part(cols, bias):
                k, v = kf_s[cols, :], v_s[cols, :]
                s = _dot(q, k, NT)
                if bias is not None:
                    s = s + bias
                p = jnp.exp(s - lse_i)
                ds = (p * (_dot(doi, v, NT) - dl_i)).astype(BF16)
                dk_s[cols, :] += _dot(ds, q, TN)
                dv_s[cols, :] += _dot(p.astype(BF16), doi, TN)
                return _dot(ds, k, NN)

            dq = part(rows, bias_ref[...])
            if i:
                dq = dq + part(slice(0, i * TQ), None)
            dq_s[...] = dq
            dqn, dqr, a0, a1 = _head_norm_rope_bwd(dq_s[:, :MLA_NOPE], dq_s[:, MLA_NOPE:], q_ref[rows, :MLA_NOPE],
                                                   q_ref[rows, MLA_NOPE:], qgn_ref[...], qgr_ref[...],
                                                   c_ref[rows, :], s_ref[rows, :], scale)
            _store_pair(dq_ref, rows, dqn, dqr)
            gqn, gqr = gqn + a0, gqr + a1

        def post(t, carry):
            rows = blk(t)
            dkn, dkr, a2, a3 = _head_norm_rope_bwd(dk_s[rows, :MLA_NOPE], dk_s[rows, MLA_NOPE:],
                                                   kv_ref[rows, :MLA_NOPE], kr_ref[rows, :], kgn_ref[...], kgr_ref[...],
                                                   c_ref[rows, :], s_ref[rows, :], 1.0)
            dkv_ref[rows, :MLA_NOPE] = dkn.astype(BF16)
            dkv_ref[rows, MLA_NOPE:] = dv_s[rows, :].astype(BF16)

            @pl.when(h == 0)
            def _():
                dkr_ref[rows, :] = dkr

            @pl.when(h > 0)
            def _():
                dkr_ref[rows, :] += dkr

            return carry[0] + a2, carry[1] + a3

        gkn, gkr = lax.fori_loop(0, nQ, post, (jnp.zeros((1, MLA_NOPE), F32), jnp.zeros((1, MLA_R2), F32)), unroll=2)
        first = (b == 0) & (h == 0)

        @pl.when(first)
        def _():
            dqgn_ref[...] = gqn
            dqgr_ref[...] = gqr
            dkgn_ref[...] = gkn
            dkgr_ref[...] = gkr

        @pl.when(jnp.logical_not(first))
        def _():
            dqgn_ref[...] += gqn
            dqgr_ref[...] += gqr
            dkgn_ref[...] += gkn
            dkgr_ref[...] += gkr

    def vec(n):
        return _bs((1, n), lambda b, h: (0, 0))

    def cols(n):
        return _bs((S, n), lambda b, h: (b, h))

    tab = _bs((S, MLA_R2), lambda b, h: (0, 0))
    return _pcall(
        body, name="mla_bwd", grid=(B, MLA_H),
        in_specs=[cols(MLA_PAD), cols(MLA_NOPE + MLA_V), _bs((S, MLA_R2), lambda b, h: (b, 0)), cols(MLA_V),
                  _bs((None, S, 1), lambda b, h: (h, b, 0)), cols(MLA_V), cols(MLA_PAD), cols(MLA_PAD),
                  vec(MLA_NOPE), vec(MLA_R2), vec(MLA_NOPE), vec(MLA_R2), tab, tab,
                  _bs((TQ, TQ), lambda b, h: (0, 0))],
        out_specs=[cols(MLA_PAD), cols(MLA_NOPE + MLA_V), _bs((S, MLA_R2), lambda b, h: (b, 0)),
                   vec(MLA_NOPE), vec(MLA_R2), vec(MLA_NOPE), vec(MLA_R2)],
        out_shape=[jax.ShapeDtypeStruct((T, MLA_H * MLA_PAD), BF16),
                   jax.ShapeDtypeStruct((T, MLA_H * (MLA_NOPE + MLA_V)), BF16),
                   jax.ShapeDtypeStruct((T, MLA_R2), F32), jax.ShapeDtypeStruct((1, MLA_NOPE), F32),
                   jax.ShapeDtypeStruct((1, MLA_R2), F32), jax.ShapeDtypeStruct((1, MLA_NOPE), F32),
                   jax.ShapeDtypeStruct((1, MLA_R2), F32)],
        scratch_shapes=[pltpu.VMEM((S, MLA_V), BF16), pltpu.VMEM((S, 1), F32), pltpu.VMEM((TQ, MLA_PAD), F32),
                        pltpu.VMEM((S, MLA_PAD), F32), pltpu.VMEM((S, MLA_V), F32)],
        sem=("arbitrary", "arbitrary"),
        args=(q_raw, kv, kr, o, lse, do, qf, kf, qgn, qgr, kgn, kgr, cos, sin, _diag_bias()), comm=comm)


def _adamw(name, recvs, w, m, v, tr=None, comm=None):
    n, R, C = recvs[0].shape
    L = len(recvs)
    Lw, Rw, _ = w.shape
    assert Lw * Rw == L * R and w.shape[2] == C
    tr = R if tr is None else tr
    assert R % tr == 0 and Rw % tr == 0
    per = R // tr
    per_w = Rw // tr
    c1 = 1.0 - ADAM_B1 ** ADAM_STEP
    c2 = 1.0 - ADAM_B2 ** ADAM_STEP

    def body(*refs):
        r_refs = refs[:L]
        w_ref, m_ref, v_ref, g_ref, d_ref, nm_ref, nv_ref = refs[L:]
        layer = pl.program_id(0) // per

        def total(r_ref):
            t = r_ref[0].astype(F32)
            for k in range(1, n):
                t = t + r_ref[k].astype(F32)
            return t

        g = total(r_refs[0]) if L == 1 else lax.switch(layer, [functools.partial(total, r) for r in r_refs])
        mm = ADAM_B1 * m_ref[...] + (1.0 - ADAM_B1) * g
        vv = ADAM_B2 * v_ref[...] + (1.0 - ADAM_B2) * (g * g)
        g_ref[...] = g
        nm_ref[...] = mm
        nv_ref[...] = vv
        d_ref[...] = -ADAM_LR * ((mm / c1) / (jnp.sqrt(vv / c2) + ADAM_EPS) + ADAM_WD * w_ref[...])

    blk = _bs((None, tr, C), lambda i: (i // per_w, i % per_w, 0))
    r_specs = [_bs((n, tr, C), functools.partial(lambda l, i: (0, jnp.clip(i - l * per, 0, per - 1), 0), l))
               for l in range(L)]
    outs, got = _pcall(body, name=name, grid=(L * per,), in_specs=r_specs + [blk, blk, blk], out_specs=[blk] * 4,
                       out_shape=[jax.ShapeDtypeStruct(w.shape, F32)] * 4, scratch_shapes=[], sem=("arbitrary",),
                       args=(*recvs, w, m, v), comm=comm)
    return outs if comm is None else (outs, got)


def _sum8(name, a):
    n, R, C = a.shape

    def body(a_ref, o_ref):
        s = a_ref[0]
        for k in range(1, n):
            s = s + a_ref[k]
        o_ref[...] = s

    return pl.pallas_call(body, name=name, out_shape=jax.ShapeDtypeStruct((R, C), a.dtype))(a)


def _sds(shape, dt):
    return jax.ShapeDtypeStruct(shape, dt)


def _norm_proj(name, x, g, w, o_spec, out_shape, tm=1024, comm=None):
    T, K = x.shape
    J, _, n = w.shape

    def body(x_ref, g_ref, w_ref, o_ref, h_ref, hs):
        @pl.when(pl.program_id(1) == 0)
        def _():
            xf = x_ref[...]
            r = lax.rsqrt(jnp.mean(xf * xf, axis=-1, keepdims=True) + RMS_EPS)
            h = (xf * r * g_ref[...]).astype(BF16)
            hs[...] = h
            h_ref[...] = h

        o_ref[...] = _dot(hs[...], w_ref[...], NN).astype(o_ref.dtype)

    row = _bs((tm, K), lambda m, j: (m, 0))
    (out, h), got = _pcall(
        body, name=name, grid=(T // tm, J),
        in_specs=[row, _bs((1, K), lambda m, j: (0, 0)), _bs((None, K, n), lambda m, j: (j, 0, 0))],
        out_specs=[o_spec, row], out_shape=[out_shape, _sds((T, K), BF16)], scratch_shapes=[pltpu.VMEM((tm, K), BF16)],
        sem=("parallel", "arbitrary"), args=(x, g, w), comm=comm)
    return out, h, got


def _proj_shared_dx(name, d, w, tm=1024, comm=None):
    J, T, n = d.shape
    K = w.shape[1]
    return _mm(name, d, w, grid=(T // tm, J), a_spec=_bs((None, tm, n), lambda m, k: (k, m, 0)),
               b_spec=_bs((None, K, n), lambda m, k: (k, 0, 0)), o_spec=_bs((tm, K), lambda m, k: (m, 0)),
               out_shape=_sds((T, K), F32), dims=NT, kax=1, acc_shape=(tm, K), comm=comm)


def _out_proj(name, a, w, res, tm=512):
    J, T, k = a.shape
    N = w.shape[2]
    return _mm(name, a, w, grid=(T // tm,), a_spec=_bs((J, tm, k), lambda m: (0, m, 0)),
               b_spec=_bs((J, k, N), lambda m: (0, 0, 0)), o_spec=_bs((tm, N), lambda m: (m, 0)),
               out_shape=_sds((T, N), F32), dims=NN, res=res, res_spec=_bs((tm, N), lambda m: (m, 0)), jb=J)


def _out_proj_dx(name, dx, w, tm=1024, comm=None):
    T, N = dx.shape
    J, k, _ = w.shape
    return _mm(name, dx, w, grid=(T // tm, J), a_spec=_bs((tm, N), lambda m, j: (m, 0)),
               b_spec=_bs((None, k, N), lambda m, j: (j, 0, 0)), o_spec=_bs((None, tm, k), lambda m, j: (j, m, 0)),
               out_shape=_sds((J, T, k), BF16), dims=NT, comm=comm)


def _out_proj_dw(name, a, dx, tt=1024, comm=None):
    J, T, k = a.shape
    N = dx.shape[1]
    tt = min(tt, T)
    return _mm(name, a, dx, grid=(J, T // tt), a_spec=_bs((None, tt, k), lambda j, t: (j, t, 0)),
               b_spec=_bs((tt, N), lambda j, t: (t, 0)), o_spec=_bs((None, k, N), lambda j, t: (j, 0, 0)),
               out_shape=_sds((J, k, N), BF16), dims=TN, kax=1, acc_shape=(k, N), comm=comm)


def _dense(name, a, b, dims, out_dtype, tm=512, res=None, comm=None):
    if dims == TN:
        T, K = a.shape
        N = b.shape[1]
        return _mm(name, a, b, grid=(T // tm,), a_spec=_bs((tm, K), lambda t: (t, 0)),
                   b_spec=_bs((tm, N), lambda t: (t, 0)), o_spec=_bs((K, N), lambda t: (0, 0)),
                   out_shape=_sds((K, N), out_dtype), dims=TN, kax=0, acc_shape=(K, N), comm=comm)
    M, K = a.shape
    N = b.shape[1] if dims == NN else b.shape[0]
    row = _bs((tm, N), lambda m: (m, 0))
    return _mm(name, a, b, grid=(M // tm,), a_spec=_bs((tm, K), lambda m: (m, 0)), b_spec=_bs(b.shape, lambda m: (0, 0)),
               o_spec=row, out_shape=_sds((M, N), out_dtype), dims=dims, res=res,
               res_spec=row if res is not None else None, comm=comm)


def _bf16(x):
    return x.astype(BF16)


def _ffn_fwd(i, x, norm_g, w_in, cw, cb, w_out, B, S, comm_in=None):
    T = x.shape[0]
    u, h, got = _norm_proj(f"ffn{i}_in", x, norm_g, w_in, _bs((None, 1024, FSH), lambda m, j: (j, m, 0)),
                           _sds((NDEV, T, FSH), BF16), comm=comm_in)
    u4 = u.reshape(2, 4, T, FSH)
    gt = _convffn_fwd(f"ffn{i}_gate", u4, cw, cb, B, S)
    y = _out_proj(f"ffn{i}_out", gt, w_out, x)
    return y, (x, h, u4, gt), got


def _ffn_bwd(i, dy, dyb, saved, norm_g, w_in, cw, cb, w_out, B, S, first_half_early):
    x, h, u4, gt = saved
    dgt = _out_proj_dx(f"ffn{i}_out_dx", dyb, w_out)
    dw_out = _out_proj_dw(f"ffn{i}_out_dw", gt, dyb).reshape(NDEV, FSH // 2, D_MODEL)
    (du4, dcw, dcb), (r_out,) = _convffn_bwd(f"ffn{i}_gate_bwd", u4, cw, cb, dgt, B, S, comm=_Exchange([dw_out]))
    du = du4.reshape(NDEV, du4.shape[2], FSH)
    dw_in = _out_proj_dw(f"ffn{i}_in_dw", du, h, tt=2048)
    r_in = None
    if first_half_early:
        dh, (r_in,) = _proj_shared_dx(f"ffn{i}_in_dx", du, w_in, comm=_Exchange([dw_in], rows=[(0, FSH // 2)]))
    else:
        dh = _proj_shared_dx(f"ffn{i}_in_dx", du, w_in)
    dx, dgn, dxb = _rms_bwd(f"ffn{i}_norm_bwd", x, norm_g, dh, dres=dy, also_bf16=True)
    return dx, dxb, dict(w_in=dw_in, norm=dgn, cw=dcw, cb=dcb), r_out, r_in


def kernel(x, ret_norm, ret_w_in, ret_gn, ret_w_out, mla_norm, mla_w_in, mla_q_norm, mla_w_qb, mla_kv_norm, mla_w_kvb, mla_q_head_norm, mla_k_head_norm, mla_w_out, ffn_norm, ffn_w_in, ffn_conv_w, ffn_conv_b, ffn_w_out, loss_target, m_ret_norm, m_ret_w_in, m_ret_gn, m_ret_w_out, m_mla_norm, m_mla_w_in, m_mla_q_norm, m_mla_w_qb, m_mla_kv_norm, m_mla_w_kvb, m_mla_q_head_norm, m_mla_k_head_norm, m_mla_w_out, m_ffn_norm, m_ffn_w_in, m_ffn_conv_w, m_ffn_conv_b, m_ffn_w_out, v_ret_norm, v_ret_w_in, v_ret_gn, v_ret_w_out, v_mla_norm, v_mla_w_in, v_mla_q_norm, v_mla_w_qb, v_mla_kv_norm, v_mla_w_kvb, v_mla_q_head_norm, v_mla_k_head_norm, v_mla_w_out, v_ffn_norm, v_ffn_w_in, v_ffn_conv_w, v_ffn_conv_b, v_ffn_w_out):
    B, S, D = x.shape
    T = B * S
    w = dict(ret_norm=ret_norm, ret_w_in=ret_w_in, ret_gn=ret_gn, ret_w_out=ret_w_out, mla_norm=mla_norm,
             mla_w_in=mla_w_in, mla_q_norm=mla_q_norm, mla_w_qb=mla_w_qb, mla_kv_norm=mla_kv_norm, mla_w_kvb=mla_w_kvb,
             mla_q_head_norm=mla_q_head_norm, mla_k_head_norm=mla_k_head_norm, mla_w_out=mla_w_out, ffn_norm=ffn_norm,
             ffn_w_in=ffn_w_in, ffn_conv_w=ffn_conv_w, ffn_conv_b=ffn_conv_b, ffn_w_out=ffn_w_out)
    mom = dict(ret_norm=m_ret_norm, ret_w_in=m_ret_w_in, ret_gn=m_ret_gn, ret_w_out=m_ret_w_out, mla_norm=m_mla_norm,
               mla_w_in=m_mla_w_in, mla_q_norm=m_mla_q_norm, mla_w_qb=m_mla_w_qb, mla_kv_norm=m_mla_kv_norm,
               mla_w_kvb=m_mla_w_kvb, mla_q_head_norm=m_mla_q_head_norm, mla_k_head_norm=m_mla_k_head_norm,
               mla_w_out=m_mla_w_out, ffn_norm=m_ffn_norm, ffn_w_in=m_ffn_w_in, ffn_conv_w=m_ffn_conv_w,
               ffn_conv_b=m_ffn_conv_b, ffn_w_out=m_ffn_w_out)
    var = dict(ret_norm=v_ret_norm, ret_w_in=v_ret_w_in, ret_gn=v_ret_gn, ret_w_out=v_ret_w_out, mla_norm=v_mla_norm,
               mla_w_in=v_mla_w_in, mla_q_norm=v_mla_q_norm, mla_w_qb=v_mla_w_qb, mla_kv_norm=v_mla_kv_norm,
               mla_w_kvb=v_mla_w_kvb, mla_q_head_norm=v_mla_q_head_norm, mla_k_head_norm=v_mla_k_head_norm,
               mla_w_out=v_mla_w_out, ffn_norm=v_ffn_norm, ffn_w_in=v_ffn_w_in, ffn_conv_w=v_ffn_conv_w,
               ffn_conv_b=v_ffn_conv_b, ffn_w_out=v_ffn_w_out)
    BIG = ["ret_w_in", "ret_w_out", "mla_w_in", "mla_w_qb", "mla_w_kvb", "mla_w_out", "ffn_w_in", "ffn_w_out"]
    REPL = ["ret_norm", "ffn_norm", "mla_q_head_norm", "mla_k_head_norm", "ffn_conv_b"]
    SHARDED_SMALL = ["ffn_conv_w", "ret_gn", "mla_norm", "mla_q_norm", "mla_kv_norm"]
    dev = _idx(_place())

    def blk16(k, i=0):
        return _bf16(w[k][i])

    small_vec = jnp.concatenate([w[k].reshape(-1) for k in SHARDED_SMALL])
    n_small = small_vec.shape[0]
    small_vec = jnp.pad(small_vec, (0, 4096 - n_small)).reshape(32, 128)
    Wret_in, sg = _comm_call("gather_ret_w_in", _Gather([blk16("ret_w_in"), small_vec], parts=2))
    sg = sg.reshape(NDEV, 4096)
    o0 = 0
    conv_w_full = sg[:, o0:o0 + 2112].reshape(NDEV, 2, 3, 352).transpose(1, 2, 0, 3).reshape(2, 3, FFN)
    o0 += 2112
    ret_gn_full = sg[:, o0:o0 + 256].reshape(NDEV, RET_H, 64).transpose(1, 0, 2).reshape(RET_H, 1, RET_DV)
    o0 += 256
    mla_norm_full = sg[:, o0:o0 + 128].reshape(1, D)
    o0 += 128
    q_norm_full = sg[:, o0:o0 + 48].reshape(1, MLA_QR)
    o0 += 48
    kv_norm_full = sg[:, o0:o0 + 32].reshape(1, MLA_KVR)

    cw = [conv_w_full[i].reshape(3, 4, FSH).transpose(1, 0, 2) for i in range(2)]
    cb = [ffn_conv_b[i].reshape(4, 1, FSH) for i in range(2)]
    fnorm = [ffn_norm[i].reshape(1, D) for i in range(2)]
    rtabs = _ret_tables(S)
    mtabs = _mla_tables(S)
    qh, kh = mla_q_head_norm.reshape(1, MLA_QK), mla_k_head_norm.reshape(1, MLA_QK)
    gains = (qh[:, :MLA_NOPE], _dup(qh[:, MLA_NOPE:]), kh[:, :MLA_NOPE], _dup(kh[:, MLA_NOPE:]))

    x0 = x.reshape(T, D)
    tgt = loss_target.reshape(T, D)
    proj, h0, (Wret_out, Wffn_out0) = _norm_proj(
        "ret_in", x0, ret_norm.reshape(1, D), Wret_in, _bs((1024, 768), lambda m, j: (m, j)), _sds((T, 6144), BF16),
        comm=_Gather([blk16("ret_w_out"), blk16("ffn_w_out", 0)]))
    Wret_out = Wret_out.reshape(RET_H * RET_DV, D)
    Wffn_out0 = Wffn_out0.reshape(4, FSH, D)
    (o_raw, rgt, states), (Wffn_in0,) = _ret_fwd(proj, rtabs, ret_gn_full, B, S, comm=_Gather([blk16("ffn_w_in", 0)]))
    x1 = _dense("ret_out", rgt, Wret_out, NN, F32, res=x0)
    MLA_W = ["mla_w_in", "mla_w_qb", "mla_w_kvb", "mla_w_out"]
    x2, ffn0_saved, got = _ffn_fwd(0, x1, fnorm[0], Wffn_in0, cw[0], cb[0], Wffn_out0, B, S,
                                   comm_in=_Gather([blk16(k) for k in MLA_W]))
    Wmla_in = got[0].reshape(D, MLA_QR + MLA_KVR + MLA_ROPE)
    Wq, Wkv, Wkr = Wmla_in[:, :MLA_QR], Wmla_in[:, MLA_QR:MLA_QR + MLA_KVR], Wmla_in[:, MLA_QR + MLA_KVR:]
    Wqb, Wkvb, Wmla_out = got[1:]

    h2 = _rms_fwd("mla_norm", x2, mla_norm_full)

    c_q, c_kv, k_rope = (_dense(n, h2, wm, NN, F32) for n, wm in
                         (("mla_in_q", Wq), ("mla_in_kv", Wkv), ("mla_in_kr", _dup(Wkr))))
    cqn = _rms_fwd("mla_q_norm", c_q, q_norm_full)
    ckvn = _rms_fwd("mla_kv_norm", c_kv, kv_norm_full)
    Wqb2 = jnp.concatenate([Wqb, Wqb[:, :, MLA_NOPE:]], axis=2).transpose(1, 0, 2).reshape(MLA_QR, MLA_H * MLA_PAD)
    Wkvb2 = Wkvb.transpose(1, 0, 2).reshape(MLA_KVR, MLA_H * (MLA_NOPE + MLA_V))
    Wmla_out2 = Wmla_out.reshape(D, D)
    q_raw = _dense("mla_qb", cqn, Wqb2, NN, F32, tm=1024)
    kvh = _dense("mla_kvb", ckvn, Wkvb2, NN, F32, tm=1024)
    (att, lse, qf, kf), (Wffn_in1, Wffn_out1) = _mla_fwd(
        q_raw, kvh, k_rope, gains, mtabs, B, S, comm=_Gather([blk16("ffn_w_in", 1), blk16("ffn_w_out", 1)]))
    Wffn_out1 = Wffn_out1.reshape(4, FSH, D)
    x3 = _dense("mla_out", att, Wmla_out2, NN, F32, res=x2)
    y, ffn1_saved, _ = _ffn_fwd(1, x3, fnorm[1], Wffn_in1, cw[1], cb[1], Wffn_out1, B, S)

    dy, colsq, dyb = _loss(y, tgt)
    loss_part = 0.5 * jnp.sum(colsq) / D

    dx3, dx3b, gf1, r_ffn1_out, _ = _ffn_bwd(1, dy, dyb, ffn1_saved, fnorm[1], Wffn_in1, cw[1], cb[1], Wffn_out1, B, S,
                                             first_half_early=False)
    datt = _dense("mla_out_dx", dx3b, Wmla_out2, NT, BF16)
    fh = FSH // 2
    (dq_raw, dkvh, dkr, dqgn, dqgr, dkgn, dkgr), (r_ffn1_in_a, r_ffn1_in_b) = _mla_bwd(
        q_raw, kvh, k_rope, att, lse, datt, qf, kf, gains, mtabs, B, S,
        comm=_Exchange([gf1["w_in"], gf1["w_in"]], rows=[(0, fh), (fh, fh)]))
    dcqn = _dense("mla_qb_dx", dq_raw, Wqb2, NT, F32, tm=1024)
    dckvn = _dense("mla_kvb_dx", dkvh, Wkvb2, NT, F32, tm=1024)
    dcq, dg_qn = _rms_bwd("mla_q_norm_bwd", c_q, q_norm_full, dcqn)
    dckv, dg_kvn = _rms_bwd("mla_kv_norm_bwd", c_kv, kv_norm_full, dckvn)
    dqgr, dkgr = _fold(dqgr), _fold(dkgr)
    dproj2 = _bf16(jnp.concatenate([dcq, dckv, _fold(dkr)], axis=-1))
    dh2 = _dense("mla_in_dx", dproj2, Wmla_in, NT, F32)
    dx2, dg_mla_norm, dx2b = _rms_bwd("mla_norm_bwd", x2, mla_norm_full, dh2, dres=dx3, also_bf16=True)

    dx1, dx1b, gf0, r_ffn0_out, r_ffn0_in_a = _ffn_bwd(0, dx2, dx2b, ffn0_saved, fnorm[0], Wffn_in0, cw[0], cb[0],
                                                       Wffn_out0, B, S, first_half_early=True)
    drgt = _dense("ret_out_dx", dx1b, Wret_out, NT, BF16)
    dWret_out = _dense("ret_out_dw", rgt, dx1b, TN, BF16, tm=1024).reshape(NDEV, 256, D)
    (dproj, dgn_ret), (r_ffn0_in_b,) = _ret_bwd(proj, o_raw, states, drgt, rtabs, ret_gn_full, B, S,
                                                comm=_Exchange([gf0["w_in"]], rows=[(fh, fh)]))
    tt = min(2048, T)
    dWret_in, (r_ret_out,) = _mm(
        "ret_in_dw", h0, dproj, grid=(NDEV, T // tt), a_spec=_bs((tt, D), lambda j, t: (t, 0)),
        b_spec=_bs((tt, 768), lambda j, t: (t, j)), o_spec=_bs((None, D, 768), lambda j, t: (j, 0, 0)),
        out_shape=_sds((NDEV, D, 768), BF16), dims=TN, kax=1, acc_shape=(D, 768), comm=_Exchange([dWret_out]))
    qr = D // 4
    dh0, r_ret_in_ab = _mm(
        "ret_in_dx", dproj, Wret_in, grid=(T // 1024, NDEV), a_spec=_bs((1024, 768), lambda m, k: (m, k)),
        b_spec=_bs((None, D, 768), lambda m, k: (k, 0, 0)), o_spec=_bs((1024, D), lambda m, k: (m, 0)),
        out_shape=_sds((T, D), F32), dims=NT, kax=1, acc_shape=(1024, D),
        comm=_Exchange([dWret_in, dWret_in], rows=[(0, qr), (qr, qr)]))
    (dx0, dg_ret_norm), (r_ret_in_c,) = _rms_bwd("ret_norm_bwd", x0, ret_norm.reshape(1, D), dh0, dres=dx1,
                                                 comm=_Exchange([dWret_in], rows=[(2 * qr, qr)]))
    grad_x = dx0.reshape(B, S, D)
    dWmla_out, (r_ret_in_d,) = _dense("mla_out_dw", att, dx3b, TN, BF16, tm=1024,
                                      comm=_Exchange([dWret_in], rows=[(3 * qr, qr)]))
    dWmla_out = dWmla_out.reshape(NDEV, MLA_V, D)
    dWqb, (r_mla_out,) = _dense("mla_qb_dw", dq_raw, cqn, TN, BF16, tm=1024, comm=_Exchange([dWmla_out]))
    dWqb = dWqb.reshape(MLA_H, MLA_PAD, MLA_QR)
    dWqb = jnp.concatenate([dWqb[:, :MLA_NOPE], dWqb[:, MLA_NOPE:MLA_QK] + dWqb[:, MLA_QK:]], axis=1)
    dWkvb, (r_mla_qb,) = _dense("mla_kvb_dw", ckvn, dkvh, TN, BF16, tm=1024, comm=_Exchange([dWqb]))
    dWkvb = dWkvb.reshape(MLA_KVR, MLA_H, MLA_NOPE + MLA_V).transpose(1, 0, 2)
    dWmla_in, (r_mla_kvb,) = _dense("mla_in_dw", h2, dproj2, TN, BF16, comm=_Exchange([dWkvb]))
    dWmla_in = dWmla_in.reshape(NDEV, 128, 704)
    received = dict(ret_w_in=[*r_ret_in_ab, r_ret_in_c, r_ret_in_d], ret_w_out=[r_ret_out], mla_w_qb=[r_mla_qb],
                    mla_w_kvb=[r_mla_kvb], mla_w_out=[r_mla_out],
                    ffn_w_in=[r_ffn0_in_a, r_ffn0_in_b, r_ffn1_in_a, r_ffn1_in_b], ffn_w_out=[r_ffn0_out, r_ffn1_out])

    dconv_w = jnp.stack([g_["cw"].transpose(1, 0, 2).reshape(3, FFN) for g_ in (gf0, gf1)])
    dconv_b = jnp.stack([g_["cb"].reshape(FFN) for g_ in (gf0, gf1)])
    small_parts = [dg_ret_norm, gf0["norm"], gf1["norm"], dg_mla_norm, dg_qn, dg_kvn, dqgn, dqgr, dkgn, dkgr, dgn_ret,
                   dconv_w, dconv_b, loss_part]
    small_g = jnp.concatenate([p.reshape(-1) for p in small_parts])
    n_grads = small_g.shape[0] - 1
    small_g = jnp.pad(small_g, (0, 240 * 128 - small_g.shape[0])).reshape(240, 128)
    small_all, received["mla_w_in"] = _comm_call("gather_small_grads", _Both(_Gather([small_g]), _Exchange([dWmla_in])))
    received["mla_w_in"] = [received["mla_w_in"]]
    sred = _sum8("sum_small_grads", small_all).reshape(-1)
    loss = sred[n_grads]

    def take(n):
        nonlocal off
        out = sred[off:off + n]
        off += n
        return out

    off = 0
    g_small = dict(ret_norm=take(D).reshape(1, D), ffn_norm=take(2 * D).reshape(2, D), mla_norm=take(D),
                   mla_q_norm=take(MLA_QR), mla_kv_norm=take(MLA_KVR))
    g_small["mla_q_head_norm"] = take(MLA_QK).reshape(1, MLA_QK)
    g_small["mla_k_head_norm"] = take(MLA_QK).reshape(1, MLA_QK)
    g_small["ret_gn"] = take(RET_H * RET_DV).reshape(1, RET_H, RET_DV)
    g_small["ffn_conv_w"] = take(2 * 3 * FFN).reshape(2, 3, FFN)
    g_small["ffn_conv_b"] = take(2 * FFN).reshape(2, FFN)
    g_small["mla_norm"] = lax.dynamic_slice(g_small["mla_norm"], (dev * 128,), (128,)).reshape(1, 128)
    g_small["mla_q_norm"] = lax.dynamic_slice(g_small["mla_q_norm"], (dev * 48,), (48,)).reshape(1, 48)
    g_small["mla_kv_norm"] = lax.dynamic_slice(g_small["mla_kv_norm"], (dev * 32,), (32,)).reshape(1, 32)
    g_small["ret_gn"] = lax.dynamic_slice(g_small["ret_gn"], (0, 0, dev * 64), (1, RET_H, 64))
    g_small["ffn_conv_w"] = lax.dynamic_slice(g_small["ffn_conv_w"], (0, 0, dev * 352), (2, 3, 352))

    grads, delta, new_m, new_v = {}, {}, {}, {}
    for k in BIG:
        rcs = received[k]
        tr = max(t for t in range(16, 257, 16) if rcs[0].shape[1] % t == 0)
        flip = (lambda t: t.transpose(0, 2, 1)) if k in ("ffn_w_in", "mla_w_qb") else (lambda t: t)
        res = _adamw(f"adamw_{k}", rcs, flip(w[k]), flip(mom[k]), flip(var[k]), tr=tr)
        grads[k], delta[k], new_m[k], new_v[k] = (flip(t) for t in res)
    SMALL = REPL + SHARDED_SMALL

    def pack(d):
        vflat = jnp.concatenate([d[k].reshape(-1) for k in SMALL])
        return jnp.pad(vflat, (0, 96 * 128 - vflat.shape[0])).reshape(1, 96, 128)

    ps = _adamw("adamw_small", [pack(g_small)], pack(w), pack(mom), pack(var))
    off = 0
    for k in SMALL:
        n = w[k].size
        grads[k], delta[k], new_m[k], new_v[k] = (t.reshape(-1)[off:off + n].reshape(w[k].shape) for t in ps)
        off += n
    names = list(w)
    return (loss, grad_x, *[grads[k] for k in names], *[delta[k] for k in names], *[new_m[k] for k in names],
            *[new_v[k] for k in names])
```

```python
import functools

import jax
import jax.numpy as jnp
from jax import lax
from jax.experimental import pallas as pl
from jax.experimental.pallas import tpu as pltpu

F32, BF16 = jnp.float32, jnp.bfloat16

NDEV = 8
D_MODEL = 1024
CHUNK = 64
RMS_EPS = 1e-6
ROPE_THETA = 10000.0
RET_H, RET_DK, RET_DV = 4, 256, 512
RET_SC = 256
MLA_H, MLA_QR, MLA_KVR = 8, 384, 256
MLA_NOPE, MLA_ROPE, MLA_V = 128, 64, 128
MLA_QK = MLA_NOPE + MLA_ROPE
MASK_VALUE = -1e30
FFN = 2816
FSH = FFN * 2 // NDEV
ATT_TQ = 256
ADAM_LR, ADAM_B1, ADAM_B2, ADAM_EPS, ADAM_WD, ADAM_STEP = 0.001, 0.9, 0.999, 1e-08, 0.01, 10
MESH = pl.DeviceIdType.MESH
VMEM_LIMIT = 56 * 2 ** 20


def _cp(sem):
    return pltpu.CompilerParams(dimension_semantics=sem, vmem_limit_bytes=VMEM_LIMIT)


def _dot(a, b, dims):
    return lax.dot_general(a, b, (dims, ((), ())), preferred_element_type=F32)


NN = ((1,), (0,))
NT = ((1,), (1,))
TN = ((0,), (0,))


def _place():
    return lax.axis_index("x"), lax.axis_index("y"), lax.axis_index("c")


def _idx(d):
    return 4 * d[0] + 2 * d[1] + d[2]


ANY = pl.BlockSpec(memory_space=pl.ANY)


class _Gather:
    sem0 = 0

    def __init__(self, arrs, parts=1):
        self.srcs = list(arrs)
        self.parts = parts
        self.nsem = len(arrs) * parts
        self.out_shape = [jax.ShapeDtypeStruct((NDEV,) + a.shape, a.dtype) for a in arrs]

    def _copies(self, ins, outs, send, recv, loc):
        n = self.nsem
        s0 = self.sem0
        x, y, c = _place()
        me, sib = (x, y, c), (x, y, 1 - c)
        chips = [(1 - x, y), (x, 1 - y), (1 - x, 1 - y)]

        def piece(ref, v, *lead):
            a, p = divmod(v, self.parts)
            if self.parts > 1:
                rows = self.srcs[a].shape[0] // self.parts
                lead = (*lead, pl.ds(p * rows, rows))
            return ref[a].at[lead] if lead else ref[a]

        def cp(a, k, block, to, src=None):
            dst = piece(outs, a, _idx(block))
            return pltpu.make_async_remote_copy(src_ref=dst if src is None else src, dst_ref=dst,
                                                send_sem=send.at[s0 + a, k], recv_sem=recv.at[s0 + a, k], device_id=to,
                                                device_id_type=MESH)

        own = [piece(ins, a) for a in range(n)]
        mine = [pltpu.make_async_copy(own[a], piece(outs, a, _idx(me)), loc.at[s0 + a]) for a in range(n)]
        first = [cp(a, 0, me, sib, src=own[a]) for a in range(n)]
        first += [cp(a, 1 + j, me, (*chip, c), src=own[a]) for a in range(n) for j, chip in enumerate(chips)]
        landed = [cp(a, 1 + j, (*chip, c), me) for j, chip in enumerate(chips) for a in range(n)]
        passed = [cp(a, 4 + j, (*chip, c), sib) for j, chip in enumerate(chips) for a in range(n)]
        from_sib = [cp(a, 0, sib, me) for a in range(n)]
        from_sib += [cp(a, 4 + j, (*chip, 1 - c), me) for j, chip in enumerate(chips) for a in range(n)]
        return mine, first, landed, passed, from_sib

    def start(self, *refs):
        mine, first, _, _, _ = self._copies(*refs)
        for cp in mine + first:
            cp.start()

    def mid(self, *refs):
        _, _, landed, passed, _ = self._copies(*refs)
        for got, on in zip(landed, passed):
            got.wait_recv()
            on.start()

    def finish(self, *refs):
        mine, first, _, passed, from_sib = self._copies(*refs)
        for cp in from_sib:
            cp.wait_recv()
        for cp in first + passed:
            cp.wait_send()
        for cp in mine:
            cp.wait()


class _Exchange:
    sem0 = 0

    def __init__(self, arrs, rows=None):
        self.srcs = list(arrs)
        self.nsem = len(arrs)
        self.rows = rows if rows is not None else [None] * len(arrs)
        self.out_shape = [jax.ShapeDtypeStruct(a.shape if r is None else (a.shape[0], r[1]) + a.shape[2:], a.dtype)
                          for a, r in zip(arrs, self.rows)]

    def _copies(self, ins, outs, send, recv, loc):
        n = len(self.srcs)
        s0 = self.sem0
        x, y, c = _place()
        me = _idx((x, y, c))

        def src(a, q):
            r = self.rows[a]
            return ins[a].at[q] if r is None else ins[a].at[q, pl.ds(r[0], r[1])]

        mine = [pltpu.make_async_copy(src(a, me), outs[a].at[me], loc.at[s0 + a]) for a in range(n)]
        remote = []
        for k in range(1, NDEV):
            peer = (x ^ (k >> 2), y ^ ((k >> 1) & 1), c ^ (k & 1))
            remote += [pltpu.make_async_remote_copy(
                src_ref=src(a, _idx(peer)), dst_ref=outs[a].at[me], send_sem=send.at[s0 + a, k - 1],
                recv_sem=recv.at[s0 + a, k - 1], device_id=peer, device_id_type=MESH) for a in range(n)]
        return mine, remote

    def start(self, *refs):
        mine, remote = self._copies(*refs)
        for cp in mine + remote:
            cp.start()

    def mid(self, *refs):
        pass

    def finish(self, *refs):
        mine, remote = self._copies(*refs)
        for cp in remote + mine:
            cp.wait()


def _comm_scratch(n):
    return [pltpu.SemaphoreType.DMA((n, 7)), pltpu.SemaphoreType.DMA((n, 7)), pltpu.SemaphoreType.DMA((n,))]


def _comm_call(name, comm):
    n = len(comm.srcs)

    def body(*refs):
        parts = (refs[:n], refs[n:2 * n]) + tuple(refs[2 * n:])
        comm.start(*parts)
        comm.mid(*parts)
        comm.finish(*parts)

    return pl.pallas_call(body, name=name, in_specs=[ANY] * n, out_specs=[ANY] * n, out_shape=comm.out_shape,
                          scratch_shapes=_comm_scratch(comm.nsem))(*comm.srcs)


def _pcall(body, *, name, grid, in_specs, out_specs, out_shape, scratch_shapes, sem, args, comm=None):
    if comm is None:
        return pl.pallas_call(body, name=name, grid=grid, in_specs=in_specs, out_specs=out_specs, out_shape=out_shape,
                              scratch_shapes=scratch_shapes, compiler_params=_cp(sem))(*args), None
    ni, no, ns, nc = len(in_specs), len(out_shape), len(scratch_shapes), len(comm.srcs)
    total = 1
    for g in grid:
        total *= g
    middle = (4 * total) // 5

    def wrapped(*refs):
        ins, csrc = refs[:ni], refs[ni:ni + nc]
        outs, cdst = refs[ni + nc:ni + nc + no], refs[ni + nc + no:ni + 2 * nc + no]
        scr, sems = refs[ni + 2 * nc + no:ni + 2 * nc + no + ns], refs[ni + 2 * nc + no + ns:]
        step = pl.program_id(0)
        for k in range(1, len(grid)):
            step = step * grid[k] + pl.program_id(k)
        parts = (csrc, cdst) + tuple(sems)

        @pl.when(step == 0)
        def _():
            comm.start(*parts)

        body(*ins, *outs, *scr)

        @pl.when(step == middle)
        def _():
            comm.mid(*parts)

        @pl.when(step == total - 1)
        def _():
            comm.finish(*parts)

    res = pl.pallas_call(
        wrapped, name=name, grid=grid, in_specs=list(in_specs) + [ANY] * nc, out_specs=list(out_specs) + [ANY] * nc,
        out_shape=list(out_shape) + comm.out_shape, scratch_shapes=list(scratch_shapes) + _comm_scratch(comm.nsem),
        compiler_params=_cp(("arbitrary",) * len(grid)))(*args, *comm.srcs)
    return res[:no], res[no:]


def _mm(name, a, b, *, grid, a_spec, b_spec, o_spec, out_shape, dims, kax=None, res=None, res_spec=None,
        jb=0, acc_shape=None, comm=None):
    nk = grid[kax] if kax is not None else 1

    def body(*refs):
        if res is not None:
            a_ref, b_ref, r_ref, o_ref = refs[:4]
        else:
            a_ref, b_ref, o_ref = refs[:3]

        def product():
            if not jb:
                return _dot(a_ref[...], b_ref[...], dims)
            part = _dot(a_ref[0], b_ref[0], dims)
            for j in range(1, jb):
                part = part + _dot(a_ref[j], b_ref[j], dims)
            return part

        def fin(acc):
            if res is not None:
                acc = acc + r_ref[...]
            o_ref[...] = acc.astype(o_ref.dtype)

        if nk == 1:
            fin(product())
        else:
            acc_ref = refs[-1]
            k = pl.program_id(kax)

            @pl.when(k == 0)
            def _():
                acc_ref[...] = jnp.zeros_like(acc_ref)

            acc_ref[...] += product()

            @pl.when(k == nk - 1)
            def _():
                fin(acc_ref[...])

    sem = tuple("arbitrary" if i == kax else "parallel" for i in range(len(grid)))
    in_specs = [a_spec, b_spec] + ([res_spec] if res is not None else [])
    args = (a, b) + ((res,) if res is not None else ())
    scratch = [pltpu.VMEM(acc_shape, F32)] if nk > 1 else []
    (out,), got = _pcall(body, name=name, grid=grid, in_specs=in_specs, out_specs=[o_spec], out_shape=[out_shape],
                         scratch_shapes=scratch, sem=sem, args=args, comm=comm)
    return out if comm is None else (out, got)


def _bs(shape, fn):
    return pl.BlockSpec(shape, fn)


def _rms_fwd(name, x, g, tm=512):
    T, D = x.shape

    def body(x_ref, g_ref, o_ref):
        xf = x_ref[...]
        r = lax.rsqrt(jnp.mean(xf * xf, axis=-1, keepdims=True) + RMS_EPS)
        o_ref[...] = (xf * r * g_ref[...]).astype(o_ref.dtype)

    return pl.pallas_call(
        body, name=name, grid=(T // tm,),
        in_specs=[_bs((tm, D), lambda i: (i, 0)), _bs((1, D), lambda i: (0, 0))],
        out_specs=_bs((tm, D), lambda i: (i, 0)), out_shape=jax.ShapeDtypeStruct((T, D), BF16),
        compiler_params=_cp(("parallel",)))(x, g)


def _rms_bwd(name, x, g, dh, dres=None, tm=512, also_bf16=False, comm=None):
    T, D = x.shape

    def body(*refs):
        if also_bf16:
            refs, dxb_ref = refs[:-1], refs[-1]
        if dres is not None:
            x_ref, g_ref, dh_ref, dres_ref, dx_ref, dg_ref = refs
        else:
            x_ref, g_ref, dh_ref, dx_ref, dg_ref = refs
        i = pl.program_id(0)
        xf = x_ref[...]
        r = lax.rsqrt(jnp.mean(xf * xf, axis=-1, keepdims=True) + RMS_EPS)
        xh = xf * r
        d = dh_ref[...].astype(F32)
        dxh = d * g_ref[...]
        dx = r * (dxh - xh * jnp.mean(dxh * xh, axis=-1, keepdims=True))
        if dres is not None:
            dx = dx + dres_ref[...]
        dx_ref[...] = dx
        if also_bf16:
            dxb_ref[...] = dx.astype(BF16)
        part = jnp.sum(d * xh, axis=0, keepdims=True)

        @pl.when(i == 0)
        def _():
            dg_ref[...] = part

        @pl.when(i > 0)
        def _():
            dg_ref[...] += part

    row = _bs((tm, D), lambda i: (i, 0))
    vec = _bs((1, D), lambda i: (0, 0))
    in_specs = [row, vec, row] + ([row] if dres is not None else [])
    args = (x, g, dh) + ((dres,) if dres is not None else ())
    extra = [jax.ShapeDtypeStruct((T, D), BF16)] if also_bf16 else []
    outs, got = _pcall(
        body, name=name, grid=(T // tm,), in_specs=in_specs, out_specs=[row, vec] + [row] * len(extra),
        out_shape=[jax.ShapeDtypeStruct((T, D), F32), jax.ShapeDtypeStruct((1, D), F32)] + extra, scratch_shapes=[],
        sem=("arbitrary",), args=args, comm=comm)
    return outs if comm is None else (outs, got)


def _loss(y, tgt, tm=512):
    T, D = y.shape

    def body(y_ref, t_ref, dy_ref, s_ref, dyb_ref):
        i = pl.program_id(0)
        e = y_ref[...] - t_ref[...]
        dy = e * (1.0 / D)
        dy_ref[...] = dy
        dyb_ref[...] = dy.astype(BF16)
        part = jnp.sum(e * e, axis=0, keepdims=True)

        @pl.when(i == 0)
        def _():
            s_ref[...] = part

        @pl.when(i > 0)
        def _():
            s_ref[...] += part

    row = _bs((tm, D), lambda i: (i, 0))
    return pl.pallas_call(
        body, name="loss_head", grid=(T // tm,), in_specs=[row, row],
        out_specs=[row, _bs((1, D), lambda i: (0, 0)), row],
        out_shape=[jax.ShapeDtypeStruct((T, D), F32), jax.ShapeDtypeStruct((1, D), F32),
                   jax.ShapeDtypeStruct((T, D), BF16)],
        compiler_params=_cp(("arbitrary",)))(y, tgt)


def _shift_rows(t, k, row):
    return jnp.where(row >= k, pltpu.roll(t, k, 0), 0.0)


def _shift_rows_up(t, k, row, n):
    return jnp.where(row < n - k, pltpu.roll(t, n - k, 0), 0.0)


def _convffn_fwd(name, u, cw, cb, B, S):
    _, J, T, F = u.shape

    def body(u_ref, cw_ref, cb_ref, o_ref):
        a = u_ref[0].astype(F32)
        g = u_ref[1].astype(F32)
        row = lax.broadcasted_iota(jnp.int32, (S, F), 0)
        w0, w1, w2 = cw_ref[0:1, :], cw_ref[1:2, :], cw_ref[2:3, :]
        gc = _shift_rows(g, 2, row) * w0 + _shift_rows(g, 1, row) * w1 + g * w2 + cb_ref[...]
        o_ref[...] = (gc * jax.nn.sigmoid(gc) * a).astype(o_ref.dtype)

    return pl.pallas_call(
        body, name=name, grid=(J, B),
        in_specs=[_bs((2, None, S, F), lambda j, b: (0, j, b, 0)), _bs((None, 3, F), lambda j, b: (j, 0, 0)),
                  _bs((None, 1, F), lambda j, b: (j, 0, 0))],
        out_specs=_bs((None, S, F), lambda j, b: (j, b, 0)), out_shape=jax.ShapeDtypeStruct((J, T, F), BF16),
        compiler_params=_cp(("parallel", "parallel")))(u, cw, cb)


def _convffn_bwd(name, u, cw, cb, dgt, B, S, comm=None):
    _, J, T, F = u.shape

    def body(u_ref, cw_ref, cb_ref, d_ref, du_ref, dcw_ref, dcb_ref):
        b = pl.program_id(1)
        a = u_ref[0].astype(F32)
        g = u_ref[1].astype(F32)
        d = d_ref[...].astype(F32)
        row = lax.broadcasted_iota(jnp.int32, (S, F), 0)
        w0, w1, w2 = cw_ref[0:1, :], cw_ref[1:2, :], cw_ref[2:3, :]
        g1, g2 = _shift_rows(g, 1, row), _shift_rows(g, 2, row)
        gc = g2 * w0 + g1 * w1 + g * w2 + cb_ref[...]
        sg = jax.nn.sigmoid(gc)
        du_ref[0] = (d * gc * sg).astype(du_ref.dtype)
        dgc = d * a * (sg * (1.0 + gc * (1.0 - sg)))
        dg = dgc * w2 + _shift_rows_up(dgc, 1, row, S) * w1 + _shift_rows_up(dgc, 2, row, S) * w0
        du_ref[1] = dg.astype(du_ref.dtype)
        parts = [jnp.sum(dgc * g2, axis=0, keepdims=True), jnp.sum(dgc * g1, axis=0, keepdims=True),
                 jnp.sum(dgc * g, axis=0, keepdims=True)]
        pb = jnp.sum(dgc, axis=0, keepdims=True)

        @pl.when(b == 0)
        def _():
            for k in range(3):
                dcw_ref[k:k + 1, :] = parts[k]
            dcb_ref[...] = pb

        @pl.when(b > 0)
        def _():
            for k in range(3):
                dcw_ref[k:k + 1, :] += parts[k]
            dcb_ref[...] += pb

    uspec = _bs((2, None, S, F), lambda j, b: (0, j, b, 0))
    return _pcall(
        body, name=name, grid=(J, B),
        in_specs=[uspec, _bs((None, 3, F), lambda j, b: (j, 0, 0)), _bs((None, 1, F), lambda j, b: (j, 0, 0)),
                  _bs((None, S, F), lambda j, b: (j, b, 0))],
        out_specs=[uspec, _bs((None, 3, F), lambda j, b: (j, 0, 0)), _bs((None, 1, F), lambda j, b: (j, 0, 0))],
        out_shape=[jax.ShapeDtypeStruct(u.shape, BF16), jax.ShapeDtypeStruct((J, 3, F), F32),
                   jax.ShapeDtypeStruct((J, 1, F), F32)],
        scratch_shapes=[], sem=("parallel", "arbitrary"), args=(u, cw, cb, dgt), comm=comm)


def _ret_tables(S):
    half = RET_DK // 2
    inv = ROPE_THETA ** (-jnp.arange(half, dtype=F32) / half)
    ang = jnp.arange(S).astype(F32)[:, None] * inv[None, :]
    lg = jnp.log1p(-jnp.exp2(-5.0 - jnp.arange(RET_H, dtype=F32)))
    i = jnp.arange(RET_SC, dtype=F32)
    same_or_earlier = (jnp.floor(i[None, :] / CHUNK) <= jnp.floor(i[:, None] / CHUNK)).astype(F32)
    dm = jnp.exp(lg[:, None, None] * jnp.abs(i[:, None] - i[None, :])) * same_or_earlier[None]
    qd = jnp.exp(lg[:, None] * (i + 1.0))[:, :, None]
    kd = jnp.exp(lg[:, None] * (RET_SC - 1.0 - i))[:, :, None]
    cd = jnp.exp(lg * RET_SC)[:, None, None]
    return jnp.cos(ang), jnp.sin(ang), dm, qd, kd, cd


def _rope_halves(t, cs, sn):
    h = t.shape[-1] // 2
    t1, t2 = t[:, :h], t[:, h:]
    return jnp.concatenate([t1 * cs - t2 * sn, t2 * cs + t1 * sn], axis=-1)


def _unrope_halves(d, cs, sn):
    h = d.shape[-1] // 2
    d1, d2 = d[:, :h], d[:, h:]
    return jnp.concatenate([d1 * cs + d2 * sn, d2 * cs - d1 * sn], axis=-1)


def _ret_specs(nC, order):
    SC = RET_SC

    def sp(shape, fn):
        return _bs(shape, lambda *g: fn(*order(*g)))

    q = sp((SC, RET_DK), lambda b, h, c: (b * nC + c, h))
    k = sp((SC, RET_DK), lambda b, h, c: (b * nC + c, RET_H + h))
    v = sp((SC, RET_DV), lambda b, h, c: (b * nC + c, RET_H + h))
    g = sp((SC, RET_DV), lambda b, h, c: (b * nC + c, 2 * RET_H + h))
    cs = sp((SC, RET_DK // 2), lambda b, h, c: (c, 0))
    dm = sp((None, SC, SC), lambda b, h, c: (h, 0, 0))
    dv = sp((None, SC, 1), lambda b, h, c: (h, 0, 0))
    cd = sp((None, 1, 1), lambda b, h, c: (h, 0, 0))
    gn = sp((None, 1, RET_DV), lambda b, h, c: (h, 0, 0))
    wide = sp((SC, RET_DV), lambda b, h, c: (b * nC + c, h))
    narrow = sp((SC, RET_DK), lambda b, h, c: (b * nC + c, h))
    st = sp((None, None, None, RET_DK, RET_DV), lambda b, h, c: (b, h, c, 0, 0))
    return dict(q=q, k=k, v=v, g=g, cs=cs, dm=dm, dv=dv, cd=cd, gn=gn, wide=wide, narrow=narrow, st=st)


def _ret_fwd(proj, tabs, gn, B, S, comm=None):
    T = B * S
    nC = S // RET_SC
    cos, sin, dm, qd, kd, cd = tabs
    s = _ret_specs(nC, lambda b, h, c: (b, h, c))

    def body(q_ref, k_ref, v_ref, g_ref, cos_ref, sin_ref, dm_ref, qd_ref, kd_ref, cd_ref, gn_ref,
             o_ref, gt_ref, st_ref, state):
        c = pl.program_id(2)

        @pl.when(c == 0)
        def _():
            state[...] = jnp.zeros_like(state)

        cs, sn = cos_ref[...], sin_ref[...]
        qf = _rope_halves(q_ref[...].astype(F32), cs, sn)
        kf = _rope_halves(k_ref[...].astype(F32), cs, sn) * (RET_DK ** -0.5)
        v = v_ref[...]
        p = _dot(qf.astype(BF16), kf.astype(BF16), NT) * dm_ref[...]
        st = state[...]
        stb = st.astype(BF16)
        st_ref[...] = stb
        o = _dot(p.astype(BF16), v, NN) + _dot((qf * qd_ref[...]).astype(BF16), stb, NN)
        state[...] = st * cd_ref[...] + _dot((kf * kd_ref[...]).astype(BF16), v, TN)
        o_ref[...] = o
        r = lax.rsqrt(jnp.mean(o * o, axis=-1, keepdims=True) + RMS_EPS)
        gf = g_ref[...].astype(F32)
        gt_ref[...] = ((o * r * gn_ref[...]) * (gf * jax.nn.sigmoid(gf))).astype(BF16)

    return _pcall(
        body, name="ret_fwd", grid=(B, RET_H, nC),
        in_specs=[s["q"], s["k"], s["v"], s["g"], s["cs"], s["cs"], s["dm"], s["dv"], s["dv"], s["cd"], s["gn"]],
        out_specs=[s["wide"], s["wide"], s["st"]],
        out_shape=[jax.ShapeDtypeStruct((T, RET_H * RET_DV), F32), jax.ShapeDtypeStruct((T, RET_H * RET_DV), BF16),
                   jax.ShapeDtypeStruct((B, RET_H, nC, RET_DK, RET_DV), BF16)],
        scratch_shapes=[pltpu.VMEM((RET_DK, RET_DV), F32)], sem=("parallel", "parallel", "arbitrary"),
        args=(proj, proj, proj, proj, cos, sin, dm, qd, kd, cd, gn), comm=comm)


def _ret_bwd(proj, o_raw, states, dgt, tabs, gn, B, S, comm=None):
    T = B * S
    nC = S // RET_SC
    cos, sin, dm, qd, kd, cd = tabs
    s = _ret_specs(nC, lambda b, c, h: (b, h, nC - 1 - c))

    def body(q_ref, k_ref, v_ref, g_ref, o_ref, st_ref, d_ref, cos_ref, sin_ref, dm_ref, qd_ref, kd_ref, cd_ref,
             gn_ref, dproj_ref, dgn_ref, dstates):
        b, c, h = pl.program_id(0), pl.program_id(1), pl.program_id(2)
        dstate = dstates.at[h]

        @pl.when(c == 0)
        def _():
            dstate[...] = jnp.zeros_like(dstate)

        @pl.when((b == 0) & (c == 0))
        def _():
            dgn_ref[h] = jnp.zeros((1, RET_DV), F32)

        cs, sn = cos_ref[...], sin_ref[...]
        qf = _rope_halves(q_ref[...].astype(F32), cs, sn)
        kf = _rope_halves(k_ref[...].astype(F32), cs, sn) * (RET_DK ** -0.5)
        v = v_ref[...]
        gnv = gn_ref[h]
        o = o_ref[...]
        r = lax.rsqrt(jnp.mean(o * o, axis=-1, keepdims=True) + RMS_EPS)
        oh = o * r
        gf = g_ref[...].astype(F32)
        sg = jax.nn.sigmoid(gf)
        d = d_ref[...].astype(F32)
        dg = (d * (oh * gnv) * (sg * (1.0 + gf * (1.0 - sg)))).astype(BF16)
        don = d * (gf * sg)
        dgn_ref[h] += jnp.sum(don * oh, axis=0, keepdims=True)
        doh = don * gnv
        dO = (r * (doh - oh * jnp.mean(doh * oh, axis=-1, keepdims=True))).astype(BF16)
        dmv = dm_ref[h]
        qb, kb = qf.astype(BF16), kf.astype(BF16)
        p = (_dot(qb, kb, NT) * dmv).astype(BF16)
        dp = (_dot(dO, v, NT) * dmv).astype(BF16)
        st = st_ref[...]
        dsn = dstate[...]
        dsb = dsn.astype(BF16)
        qdv, kdv = qd_ref[h], kd_ref[h]
        dq = _dot(dp, kb, NN) + _dot(dO, st, NT) * qdv
        dk = _dot(dp, qb, TN) + _dot(v, dsb, NT) * kdv
        dv = _dot(p, dO, TN) + _dot((kf * kdv).astype(BF16), dsb, NN)
        dstate[...] = dsn * cd_ref[h] + _dot((qf * qdv).astype(BF16), dO, TN)
        dq = _unrope_halves(dq, cs, sn).astype(BF16)
        dk = (_unrope_halves(dk, cs, sn) * (RET_DK ** -0.5)).astype(BF16)
        dv = dv.astype(BF16)
        nq, nv = RET_H * RET_DK, RET_H * RET_DV
        for hh in range(RET_H):
            @pl.when(h == hh)
            def _():
                dproj_ref[:, hh * RET_DK:(hh + 1) * RET_DK] = dq
                dproj_ref[:, nq + hh * RET_DK:nq + (hh + 1) * RET_DK] = dk
                dproj_ref[:, 2 * nq + hh * RET_DV:2 * nq + (hh + 1) * RET_DV] = dv
                dproj_ref[:, 2 * nq + nv + hh * RET_DV:2 * nq + nv + (hh + 1) * RET_DV] = dg

    width = 2 * RET_H * (RET_DK + RET_DV)

    def all_heads(*shape):
        return _bs((RET_H,) + shape, lambda b, c, h: (0,) * (1 + len(shape)))

    return _pcall(
        body, name="ret_bwd", grid=(B, nC, RET_H),
        in_specs=[s["q"], s["k"], s["v"], s["g"], s["wide"], s["st"], s["wide"], s["cs"], s["cs"],
                  all_heads(RET_SC, RET_SC), all_heads(RET_SC, 1), all_heads(RET_SC, 1), all_heads(1, 1),
                  all_heads(1, RET_DV)],
        out_specs=[_bs((RET_SC, width), lambda b, c, h: (b * nC + nC - 1 - c, 0)),
                   _bs((RET_H, 1, RET_DV), lambda b, c, h: (0, 0, 0))],
        out_shape=[jax.ShapeDtypeStruct((T, width), BF16), jax.ShapeDtypeStruct((RET_H, 1, RET_DV), F32)],
        scratch_shapes=[pltpu.VMEM((RET_H, RET_DK, RET_DV), F32)], sem=("arbitrary", "arbitrary", "arbitrary"),
        args=(proj, proj, proj, proj, o_raw, states, dgt, cos, sin, dm, qd, kd, cd, gn), comm=comm)


MLA_PAD = 256
MLA_R2 = 2 * MLA_ROPE


def _dup(t):
    return jnp.concatenate([t, t], axis=-1)


def _fold(t):
    return t[..., :MLA_ROPE] + t[..., MLA_ROPE:]


def _mla_tables(S):
    half = MLA_ROPE // 2
    inv = ROPE_THETA ** (-jnp.arange(half, dtype=F32) / half)
    ang = jnp.arange(S).astype(F32)[:, None] * inv[None, :]
    cos, sin, zero = jnp.cos(ang), jnp.sin(ang), jnp.zeros((S, MLA_ROPE), F32)
    return jnp.concatenate([cos, cos, zero], axis=-1), jnp.concatenate([-sin, sin, zero], axis=-1)


def _head_norm_rope(n, r2, gn, gr2, cos, sin, scale):
    ssq = jnp.sum(n * n, axis=-1, keepdims=True) + 0.5 * jnp.sum(r2 * r2, axis=-1, keepdims=True)
    rstd = lax.rsqrt(ssq * (1.0 / MLA_QK) + RMS_EPS)
    yn = n * rstd * gn
    yr = r2 * rstd * gr2
    z = yr * cos + pltpu.roll(yr, MLA_ROPE // 2, 1) * sin
    if scale != 1.0:
        yn, z = yn * scale, z * scale
    return yn, z


def _head_norm_rope_bwd(dn, dz, n, r2, gn, gr2, cos, sin, scale):
    ssq = jnp.sum(n * n, axis=-1, keepdims=True) + 0.5 * jnp.sum(r2 * r2, axis=-1, keepdims=True)
    rstd = lax.rsqrt(ssq * (1.0 / MLA_QK) + RMS_EPS)
    hn, hr = n * rstd, r2 * rstd
    if scale != 1.0:
        dn, dz = dn * scale, dz * scale
    dyr = dz * cos + pltpu.roll(dz * sin, MLA_R2 - MLA_ROPE // 2, 1)
    dgn = jnp.sum(dn * hn, axis=0, keepdims=True)
    dgr = jnp.sum(dyr * hr, axis=0, keepdims=True)
    dhn, dhr = dn * gn, dyr * gr2
    mt = (jnp.sum(dhn * hn, axis=-1, keepdims=True) + jnp.sum(dhr * hr, axis=-1, keepdims=True)) * (1.0 / MLA_QK)
    return rstd * (dhn - hn * mt), rstd * (dhr - 0.5 * hr * mt), dgn, dgr


def _diag_bias():
    i = jnp.arange(ATT_TQ)
    return jnp.where((i[None, :] // CHUNK) <= (i[:, None] // CHUNK), 0.0, MASK_VALUE).astype(F32)


def _store_pair(dst, rows, n, r2):
    dst[rows, :MLA_NOPE] = n.astype(BF16)
    dst[rows, MLA_NOPE:] = r2.astype(BF16)


def _mla_fwd(q_raw, kv, kr, gains, tabs, B, S, comm=None):
    T = B * S
    TQ = ATT_TQ
    nQ = S // TQ
    qgn, qgr, kgn, kgr = gains
    cos, sin = tabs
    scale = MLA_QK ** -0.5

    def body(q_ref, kv_ref, kr_ref, qgn_ref, qgr_ref, kgn_ref, kgr_ref, c_ref, s_ref, bias_ref,
             o_ref, lse_ref, qf_s, kf_s, v_s):
        def prep(t, _):
            rows = pl.ds(pl.multiple_of(t * TQ, TQ), TQ)
            cs, sn = c_ref[rows, :], s_ref[rows, :]
            qn, qr = _head_norm_rope(q_ref[rows, :MLA_NOPE], q_ref[rows, MLA_NOPE:], qgn_ref[...], qgr_ref[...],
                                     cs, sn, scale)
            _store_pair(qf_s, rows, qn, qr)
            kn, krr = _head_norm_rope(kv_ref[rows, :MLA_NOPE], kr_ref[rows, :], kgn_ref[...], kgr_ref[...], cs, sn, 1.0)
            _store_pair(kf_s, rows, kn, krr)
            v_s[rows, :] = kv_ref[rows, MLA_NOPE:].astype(BF16)
            return 0

        lax.fori_loop(0, nQ, prep, 0, unroll=2)
        for i in range(nQ):
            rows = slice(i * TQ, (i + 1) * TQ)
            q = qf_s[rows, :]
            sd = _dot(q, kf_s[rows, :], NT) + bias_ref[...]
            m = jnp.max(sd, axis=-1, keepdims=True)
            if i:
                sl = _dot(q, kf_s[:i * TQ, :], NT)
                m = jnp.maximum(m, jnp.max(sl, axis=-1, keepdims=True))
            pd = jnp.exp(sd - m)
            l = jnp.sum(pd, axis=-1, keepdims=True)
            acc = _dot(pd.astype(BF16), v_s[rows, :], NN)
            if i:
                pl_ = jnp.exp(sl - m)
                l = l + jnp.sum(pl_, axis=-1, keepdims=True)
                acc = acc + _dot(pl_.astype(BF16), v_s[:i * TQ, :], NN)
            o_ref[rows, :] = (acc / l).astype(BF16)
            lse_ref[rows, :] = m + jnp.log(l)

    def vec(n):
        return _bs((1, n), lambda b, h: (0, 0))

    def cols(n):
        return _bs((S, n), lambda b, h: (b, h))

    tab = _bs((S, MLA_R2), lambda b, h: (0, 0))
    return _pcall(
        body, name="mla_fwd", grid=(B, MLA_H),
        in_specs=[cols(MLA_PAD), cols(MLA_NOPE + MLA_V), _bs((S, MLA_R2), lambda b, h: (b, 0)),
                  vec(MLA_NOPE), vec(MLA_R2), vec(MLA_NOPE), vec(MLA_R2), tab, tab,
                  _bs((TQ, TQ), lambda b, h: (0, 0))],
        out_specs=[cols(MLA_V), _bs((None, S, 1), lambda b, h: (h, b, 0)), cols(MLA_PAD), cols(MLA_PAD)],
        out_shape=[jax.ShapeDtypeStruct((T, MLA_H * MLA_V), BF16), jax.ShapeDtypeStruct((MLA_H, T, 1), F32),
                   jax.ShapeDtypeStruct((T, MLA_H * MLA_PAD), BF16), jax.ShapeDtypeStruct((T, MLA_H * MLA_PAD), BF16)],
        scratch_shapes=[pltpu.VMEM((S, MLA_V), BF16)],
        sem=("parallel", "parallel"), args=(q_raw, kv, kr, qgn, qgr, kgn, kgr, cos, sin, _diag_bias()), comm=comm)


def _mla_bwd(q_raw, kv, kr, o, lse, do, qf, kf, gains, tabs, B, S, comm=None):
    T = B * S
    TQ = ATT_TQ
    nQ = S // TQ
    qgn, qgr, kgn, kgr = gains
    cos, sin = tabs
    scale = MLA_QK ** -0.5

    def body(q_ref, kv_ref, kr_ref, o_ref, lse_ref, do_ref, qf_s, kf_s, qgn_ref, qgr_ref, kgn_ref, kgr_ref, c_ref, s_ref,
             bias_ref, dq_ref, dkv_ref, dkr_ref, dqgn_ref, dqgr_ref, dkgn_ref, dkgr_ref,
             v_s, dl_s, dq_s, dk_s, dv_s):
        b, h = pl.program_id(0), pl.program_id(1)

        def blk(t):
            return pl.ds(pl.multiple_of(t * TQ, TQ), TQ)

        def prep(t, _):
            rows = blk(t)
            v_s[rows, :] = kv_ref[rows, MLA_NOPE:].astype(BF16)
            dl_s[rows, :] = jnp.sum(do_ref[rows, :].astype(F32) * o_ref[rows, :].astype(F32), axis=-1, keepdims=True)
            dk_s[rows, :] = jnp.zeros((TQ, MLA_PAD), F32)
            dv_s[rows, :] = jnp.zeros((TQ, MLA_V), F32)
            return 0

        lax.fori_loop(0, nQ, prep, 0, unroll=2)

        gqn, gqr = jnp.zeros((1, MLA_NOPE), F32), jnp.zeros((1, MLA_R2), F32)
        for i in range(nQ):
            rows = slice(i * TQ, (i + 1) * TQ)
            q, doi, lse_i, dl_i = qf_s[rows, :], do_ref[rows, :], lse_ref[rows, :], dl_s[rows, :]

            def part(cols, bias):
                k, v = kf_s[cols, :], v_s[cols, :]
                s = _dot(q, k, NT)
                if bias is not None:
                    s = s + bias
                p = jnp.exp(s - lse_i)
                ds = (p * (_dot(doi, v, NT) - dl_i)).astype(BF16)
                dk_s[cols, :] += _dot(ds, q, TN)
                dv_s[cols, :] += _dot(p.astype(BF16), doi, TN)
                return _dot(ds, k, NN)

            dq = part(rows, bias_ref[...])
            if i:
                dq = dq + part(slice(0, i * TQ), None)
            dq_s[...] = dq
            dqn, dqr, a0, a1 = _head_norm_rope_bwd(dq_s[:, :MLA_NOPE], dq_s[:, MLA_NOPE:], q_ref[rows, :MLA_NOPE],
                                                   q_ref[rows, MLA_NOPE:], qgn_ref[...], qgr_ref[...],
                                                   c_ref[rows, :], s_ref[rows, :], scale)
            _store_pair(dq_ref, rows, dqn, dqr)
            gqn, gqr = gqn + a0, gqr + a1

        def post(t, carry):
            rows = blk(t)
            dkn, dkr, a2, a3 = _head_norm_rope_bwd(dk_s[rows, :MLA_NOPE], dk_s[rows, MLA_NOPE:],
                                                   kv_ref[rows, :MLA_NOPE], kr_ref[rows, :], kgn_ref[...], kgr_ref[...],
                                                   c_ref[rows, :], s_ref[rows, :], 1.0)
            dkv_ref[rows, :MLA_NOPE] = dkn.astype(BF16)
            dkv_ref[rows, MLA_NOPE:] = dv_s[rows, :].astype(BF16)

            @pl.when(h == 0)
            def _():
                dkr_ref[rows, :] = dkr

            @pl.when(h > 0)
            def _():
                dkr_ref[rows, :] += dkr

            return carry[0] + a2, carry[1] + a3

        gkn, gkr = lax.fori_loop(0, nQ, post, (jnp.zeros((1, MLA_NOPE), F32), jnp.zeros((1, MLA_R2), F32)), unroll=2)
        first = (b == 0) & (h == 0)

        @pl.when(first)
        def _():
            dqgn_ref[...] = gqn
            dqgr_ref[...] = gqr
            dkgn_ref[...] = gkn
            dkgr_ref[...] = gkr

        @pl.when(jnp.logical_not(first))
        def _():
            dqgn_ref[...] += gqn
            dqgr_ref[...] += gqr
            dkgn_ref[...] += gkn
            dkgr_ref[...] += gkr

    def vec(n):
        return _bs((1, n), lambda b, h: (0, 0))

    def cols(n):
        return _bs((S, n), lambda b, h: (b, h))

    tab = _bs((S, MLA_R2), lambda b, h: (0, 0))
    return _pcall(
        body, name="mla_bwd", grid=(B, MLA_H),
        in_specs=[cols(MLA_PAD), cols(MLA_NOPE + MLA_V), _bs((S, MLA_R2), lambda b, h: (b, 0)), cols(MLA_V),
                  _bs((None, S, 1), lambda b, h: (h, b, 0)), cols(MLA_V), cols(MLA_PAD), cols(MLA_PAD),
                  vec(MLA_NOPE), vec(MLA_R2), vec(MLA_NOPE), vec(MLA_R2), tab, tab,
                  _bs((TQ, TQ), lambda b, h: (0, 0))],
        out_specs=[cols(MLA_PAD), cols(MLA_NOPE + MLA_V), _bs((S, MLA_R2), lambda b, h: (b, 0)),
                   vec(MLA_NOPE), vec(MLA_R2), vec(MLA_NOPE), vec(MLA_R2)],
        out_shape=[jax.ShapeDtypeStruct((T, MLA_H * MLA_PAD), BF16),
                   jax.ShapeDtypeStruct((T, MLA_H * (MLA_NOPE + MLA_V)), BF16),
                   jax.ShapeDtypeStruct((T, MLA_R2), F32), jax.ShapeDtypeStruct((1, MLA_NOPE), F32),
                   jax.ShapeDtypeStruct((1, MLA_R2), F32), jax.ShapeDtypeStruct((1, MLA_NOPE), F32),
                   jax.ShapeDtypeStruct((1, MLA_R2), F32)],
        scratch_shapes=[pltpu.VMEM((S, MLA_V), BF16), pltpu.VMEM((S, 1), F32), pltpu.VMEM((TQ, MLA_PAD), F32),
                        pltpu.VMEM((S, MLA_PAD), F32), pltpu.VMEM((S, MLA_V), F32)],
        sem=("arbitrary", "arbitrary"),
        args=(q_raw, kv, kr, o, lse, do, qf, kf, qgn, qgr, kgn, kgr, cos, sin, _diag_bias()), comm=comm)


def _adamw(name, recvs, w, m, v, tr=None, comm=None):
    n, R, C = recvs[0].shape
    L = len(recvs)
    Lw, Rw, _ = w.shape
    assert Lw * Rw == L * R and w.shape[2] == C
    tr = R if tr is None else tr
    assert R % tr == 0 and Rw % tr == 0
    per = R // tr
    per_w = Rw // tr
    c1 = 1.0 - ADAM_B1 ** ADAM_STEP
    c2 = 1.0 - ADAM_B2 ** ADAM_STEP

    def body(*refs):
        r_refs = refs[:L]
        w_ref, m_ref, v_ref, g_ref, d_ref, nm_ref, nv_ref = refs[L:]
        layer = pl.program_id(0) // per

        def total(r_ref):
            t = r_ref[0].astype(F32)
            for k in range(1, n):
                t = t + r_ref[k].astype(F32)
            return t

        g = total(r_refs[0]) if L == 1 else lax.switch(layer, [functools.partial(total, r) for r in r_refs])
        mm = ADAM_B1 * m_ref[...] + (1.0 - ADAM_B1) * g
        vv = ADAM_B2 * v_ref[...] + (1.0 - ADAM_B2) * (g * g)
        g_ref[...] = g
        nm_ref[...] = mm
        nv_ref[...] = vv
        d_ref[...] = -ADAM_LR * ((mm / c1) / (jnp.sqrt(vv / c2) + ADAM_EPS) + ADAM_WD * w_ref[...])

    blk = _bs((None, tr, C), lambda i: (i // per_w, i % per_w, 0))
    r_specs = [_bs((n, tr, C), functools.partial(lambda l, i: (0, jnp.clip(i - l * per, 0, per - 1), 0), l))
               for l in range(L)]
    outs, got = _pcall(body, name=name, grid=(L * per,), in_specs=r_specs + [blk, blk, blk], out_specs=[blk] * 4,
                       out_shape=[jax.ShapeDtypeStruct(w.shape, F32)] * 4, scratch_shapes=[], sem=("arbitrary",),
                       args=(*recvs, w, m, v), comm=comm)
    return outs if comm is None else (outs, got)


def _sum8(name, a):
    n, R, C = a.shape

    def body(a_ref, o_ref):
        s = a_ref[0]
        for k in range(1, n):
            s = s + a_ref[k]
        o_ref[...] = s

    return pl.pallas_call(body, name=name, out_shape=jax.ShapeDtypeStruct((R, C), a.dtype))(a)


def _sds(shape, dt):
    return jax.ShapeDtypeStruct(shape, dt)


def _norm_proj(name, x, g, w, o_spec, out_shape, tm=1024, comm=None):
    T, K = x.shape
    J, _, n = w.shape

    def body(x_ref, g_ref, w_ref, o_ref, h_ref, hs):
        @pl.when(pl.program_id(1) == 0)
        def _():
            xf = x_ref[...]
            r = lax.rsqrt(jnp.mean(xf * xf, axis=-1, keepdims=True) + RMS_EPS)
            h = (xf * r * g_ref[...]).astype(BF16)
            hs[...] = h
            h_ref[...] = h

        o_ref[...] = _dot(hs[...], w_ref[...], NN).astype(o_ref.dtype)

    row = _bs((tm, K), lambda m, j: (m, 0))
    (out, h), got = _pcall(
        body, name=name, grid=(T // tm, J),
        in_specs=[row, _bs((1, K), lambda m, j: (0, 0)), _bs((None, K, n), lambda m, j: (j, 0, 0))],
        out_specs=[o_spec, row], out_shape=[out_shape, _sds((T, K), BF16)], scratch_shapes=[pltpu.VMEM((tm, K), BF16)],
        sem=("parallel", "arbitrary"), args=(x, g, w), comm=comm)
    return out, h, got


def _proj_shared_dx(name, d, w, tm=1024, comm=None):
    J, T, n = d.shape
    K = w.shape[1]
    return _mm(name, d, w, grid=(T // tm, J), a_spec=_bs((None, tm, n), lambda m, k: (k, m, 0)),
               b_spec=_bs((None, K, n), lambda m, k: (k, 0, 0)), o_spec=_bs((tm, K), lambda m, k: (m, 0)),
               out_shape=_sds((T, K), F32), dims=NT, kax=1, acc_shape=(tm, K), comm=comm)


def _out_proj(name, a, w, res, tm=512):
    J, T, k = a.shape
    N = w.shape[2]
    return _mm(name, a, w, grid=(T // tm,), a_spec=_bs((J, tm, k), lambda m: (0, m, 0)),
               b_spec=_bs((J, k, N), lambda m: (0, 0, 0)), o_spec=_bs((tm, N), lambda m: (m, 0)),
               out_shape=_sds((T, N), F32), dims=NN, res=res, res_spec=_bs((tm, N), lambda m: (m, 0)), jb=J)


def _out_proj_dx(name, dx, w, tm=1024, comm=None):
    T, N = dx.shape
    J, k, _ = w.shape
    return _mm(name, dx, w, grid=(T // tm, J), a_spec=_bs((tm, N), lambda m, j: (m, 0)),
               b_spec=_bs((None, k, N), lambda m, j: (j, 0, 0)), o_spec=_bs((None, tm, k), lambda m, j: (j, m, 0)),
               out_shape=_sds((J, T, k), BF16), dims=NT, comm=comm)


def _out_proj_dw(name, a, dx, tt=1024, comm=None):
    J, T, k = a.shape
    N = dx.shape[1]
    tt = min(tt, T)
    return _mm(name, a, dx, grid=(J, T // tt), a_spec=_bs((None, tt, k), lambda j, t: (j, t, 0)),
               b_spec=_bs((tt, N), lambda j, t: (t, 0)), o_spec=_bs((None, k, N), lambda j, t: (j, 0, 0)),
               out_shape=_sds((J, k, N), BF16), dims=TN, kax=1, acc_shape=(k, N), comm=comm)


def _dense(name, a, b, dims, out_dtype, tm=512, res=None, comm=None):
    if dims == TN:
        T, K = a.shape
        N = b.shape[1]
        return _mm(name, a, b, grid=(T // tm,), a_spec=_bs((tm, K), lambda t: (t, 0)),
                   b_spec=_bs((tm, N), lambda t: (t, 0)), o_spec=_bs((K, N), lambda t: (0, 0)),
                   out_shape=_sds((K, N), out_dtype), dims=TN, kax=0, acc_shape=(K, N), comm=comm)
    M, K = a.shape
    N = b.shape[1] if dims == NN else b.shape[0]
    row = _bs((tm, N), lambda m: (m, 0))
    return _mm(name, a, b, grid=(M // tm,), a_spec=_bs((tm, K), lambda m: (m, 0)), b_spec=_bs(b.shape, lambda m: (0, 0)),
               o_spec=row, out_shape=_sds((M, N), out_dtype), dims=dims, res=res,
               res_spec=row if res is not None else None, comm=comm)


def _bf16(x):
    return x.astype(BF16)


def _ffn_fwd(i, x, norm_g, w_in, cw, cb, w_out, B, S, comm_in=None):
    T = x.shape[0]
    u, h, got = _norm_proj(f"ffn{i}_in", x, norm_g, w_in, _bs((None, 1024, FSH), lambda m, j: (j, m, 0)),
                           _sds((NDEV, T, FSH), BF16), comm=comm_in)
    u4 = u.reshape(2, 4, T, FSH)
    gt = _convffn_fwd(f"ffn{i}_gate", u4, cw, cb, B, S)
    y = _out_proj(f"ffn{i}_out", gt, w_out, x)
    return y, (x, h, u4, gt), got


def _ffn_bwd(i, dy, dyb, saved, norm_g, w_in, cw, cb, w_out, B, S, first_half_early):
    x, h, u4, gt = saved
    dgt = _out_proj_dx(f"ffn{i}_out_dx", dyb, w_out)
    dw_out = _out_proj_dw(f"ffn{i}_out_dw", gt, dyb).reshape(NDEV, FSH // 2, D_MODEL)
    (du4, dcw, dcb), (r_out,) = _convffn_bwd(f"ffn{i}_gate_bwd", u4, cw, cb, dgt, B, S, comm=_Exchange([dw_out]))
    du = du4.reshape(NDEV, du4.shape[2], FSH)
    dw_in = _out_proj_dw(f"ffn{i}_in_dw", du, h, tt=2048)
    r_in = None
    if first_half_early:
        dh, (r_in,) = _proj_shared_dx(f"ffn{i}_in_dx", du, w_in, comm=_Exchange([dw_in], rows=[(0, FSH // 2)]))
    else:
        dh = _proj_shared_dx(f"ffn{i}_in_dx", du, w_in)
    dx, dgn, dxb = _rms_bwd(f"ffn{i}_norm_bwd", x, norm_g, dh, dres=dy, also_bf16=True)
    return dx, dxb, dict(w_in=dw_in, norm=dgn, cw=dcw, cb=dcb), r_out, r_in


def kernel(x, ret_norm, ret_w_in, ret_gn, ret_w_out, mla_norm, mla_w_in, mla_q_norm, mla_w_qb, mla_kv_norm, mla_w_kvb, mla_q_head_norm, mla_k_head_norm, mla_w_out, ffn_norm, ffn_w_in, ffn_conv_w, ffn_conv_b, ffn_w_out, loss_target, m_ret_norm, m_ret_w_in, m_ret_gn, m_ret_w_out, m_mla_norm, m_mla_w_in, m_mla_q_norm, m_mla_w_qb, m_mla_kv_norm, m_mla_w_kvb, m_mla_q_head_norm, m_mla_k_head_norm, m_mla_w_out, m_ffn_norm, m_ffn_w_in, m_ffn_conv_w, m_ffn_conv_b, m_ffn_w_out, v_ret_norm, v_ret_w_in, v_ret_gn, v_ret_w_out, v_mla_norm, v_mla_w_in, v_mla_q_norm, v_mla_w_qb, v_mla_kv_norm, v_mla_w_kvb, v_mla_q_head_norm, v_mla_k_head_norm, v_mla_w_out, v_ffn_norm, v_ffn_w_in, v_ffn_conv_w, v_ffn_conv_b, v_ffn_w_out):
    B, S, D = x.shape
    T = B * S
    w = dict(ret_norm=ret_norm, ret_w_in=ret_w_in, ret_gn=ret_gn, ret_w_out=ret_w_out, mla_norm=mla_norm,
             mla_w_in=mla_w_in, mla_q_norm=mla_q_norm, mla_w_qb=mla_w_qb, mla_kv_norm=mla_kv_norm, mla_w_kvb=mla_w_kvb,
             mla_q_head_norm=mla_q_head_norm, mla_k_head_norm=mla_k_head_norm, mla_w_out=mla_w_out, ffn_norm=ffn_norm,
             ffn_w_in=ffn_w_in, ffn_conv_w=ffn_conv_w, ffn_conv_b=ffn_conv_b, ffn_w_out=ffn_w_out)
    mom = dict(ret_norm=m_ret_norm, ret_w_in=m_ret_w_in, ret_gn=m_ret_gn, ret_w_out=m_ret_w_out, mla_norm=m_mla_norm,
               mla_w_in=m_mla_w_in, mla_q_norm=m_mla_q_norm, mla_w_qb=m_mla_w_qb, mla_kv_norm=m_mla_kv_norm,
               mla_w_kvb=m_mla_w_kvb, mla_q_head_norm=m_mla_q_head_norm, mla_k_head_norm=m_mla_k_head_norm,
               mla_w_out=m_mla_w_out, ffn_norm=m_ffn_norm, ffn_w_in=m_ffn_w_in, ffn_conv_w=m_ffn_conv_w,
               ffn_conv_b=m_ffn_conv_b, ffn_w_out=m_ffn_w_out)
    var = dict(ret_norm=v_ret_norm, ret_w_in=v_ret_w_in, ret_gn=v_ret_gn, ret_w_out=v_ret_w_out, mla_norm=v_mla_norm,
               mla_w_in=v_mla_w_in, mla_q_norm=v_mla_q_norm, mla_w_qb=v_mla_w_qb, mla_kv_norm=v_mla_kv_norm,
               mla_w_kvb=v_mla_w_kvb, mla_q_head_norm=v_mla_q_head_norm, mla_k_head_norm=v_mla_k_head_norm,
               mla_w_out=v_mla_w_out, ffn_norm=v_ffn_norm, ffn_w_in=v_ffn_w_in, ffn_conv_w=v_ffn_conv_w,
               ffn_conv_b=v_ffn_conv_b, ffn_w_out=v_ffn_w_out)
    BIG = ["ret_w_in", "ret_w_out", "mla_w_in", "mla_w_qb", "mla_w_kvb", "mla_w_out", "ffn_w_in", "ffn_w_out"]
    REPL = ["ret_norm", "ffn_norm", "mla_q_head_norm", "mla_k_head_norm", "ffn_conv_b"]
    SHARDED_SMALL = ["ffn_conv_w", "ret_gn", "mla_norm", "mla_q_norm", "mla_kv_norm"]
    dev = _idx(_place())

    def blk16(k, i=0):
        return _bf16(w[k][i])

    small_vec = jnp.concatenate([w[k].reshape(-1) for k in SHARDED_SMALL])
    n_small = small_vec.shape[0]
    small_vec = jnp.pad(small_vec, (0, 4096 - n_small)).reshape(32, 128)
    Wret_in, sg = _comm_call("gather_ret_w_in", _Gather([blk16("ret_w_in"), small_vec], parts=2))
    sg = sg.reshape(NDEV, 4096)
    o0 = 0
    conv_w_full = sg[:, o0:o0 + 2112].reshape(NDEV, 2, 3, 352).transpose(1, 2, 0, 3).reshape(2, 3, FFN)
    o0 += 2112
    ret_gn_full = sg[:, o0:o0 + 256].reshape(NDEV, RET_H, 64).transpose(1, 0, 2).reshape(RET_H, 1, RET_DV)
    o0 += 256
    mla_norm_full = sg[:, o0:o0 + 128].reshape(1, D)
    o0 += 128
    q_norm_full = sg[:, o0:o0 + 48].reshape(1, MLA_QR)
    o0 += 48
    kv_norm_full = sg[:, o0:o0 + 32].reshape(1, MLA_KVR)

    cw = [conv_w_full[i].reshape(3, 4, FSH).transpose(1, 0, 2) for i in range(2)]
    cb = [ffn_conv_b[i].reshape(4, 1, FSH) for i in range(2)]
    fnorm = [ffn_norm[i].reshape(1, D) for i in range(2)]
    rtabs = _ret_tables(S)
    mtabs = _mla_tables(S)
    qh, kh = mla_q_head_norm.reshape(1, MLA_QK), mla_k_head_norm.reshape(1, MLA_QK)
    gains = (qh[:, :MLA_NOPE], _dup(qh[:, MLA_NOPE:]), kh[:, :MLA_NOPE], _dup(kh[:, MLA_NOPE:]))

    x0 = x.reshape(T, D)
    tgt = loss_target.reshape(T, D)
    proj, h0, (Wret_out, Wffn_out0) = _norm_proj(
        "ret_in", x0, ret_norm.reshape(1, D), Wret_in, _bs((1024, 768), lambda m, j: (m, j)), _sds((T, 6144), BF16),
        comm=_Gather([blk16("ret_w_out"), blk16("ffn_w_out", 0)]))
    Wret_out = Wret_out.reshape(RET_H * RET_DV, D)
    Wffn_out0 = Wffn_out0.reshape(4, FSH, D)
    (o_raw, rgt, states), (Wffn_in0,) = _ret_fwd(proj, rtabs, ret_gn_full, B, S, comm=_Gather([blk16("ffn_w_in", 0)]))
    x1 = _dense("ret_out", rgt, Wret_out, NN, F32, res=x0)
    MLA_W = ["mla_w_in", "mla_w_qb", "mla_w_kvb", "mla_w_out"]
    x2, ffn0_saved, got = _ffn_fwd(0, x1, fnorm[0], Wffn_in0, cw[0], cb[0], Wffn_out0, B, S,
                                   comm_in=_Gather([blk16(k) for k in MLA_W]))
    Wmla_in = got[0].reshape(D, MLA_QR + MLA_KVR + MLA_ROPE)
    Wq, Wkv, Wkr = Wmla_in[:, :MLA_QR], Wmla_in[:, MLA_QR:MLA_QR + MLA_KVR], Wmla_in[:, MLA_QR + MLA_KVR:]
    Wqb, Wkvb, Wmla_out = got[1:]

    h2 = _rms_fwd("mla_norm", x2, mla_norm_full)

    c_q, c_kv, k_rope = (_dense(n, h2, wm, NN, F32) for n, wm in
                         (("mla_in_q", Wq), ("mla_in_kv", Wkv), ("mla_in_kr", _dup(Wkr))))
    cqn = _rms_fwd("mla_q_norm", c_q, q_norm_full)
    ckvn = _rms_fwd("mla_kv_norm", c_kv, kv_norm_full)
    Wqb2 = jnp.concatenate([Wqb, Wqb[:, :, MLA_NOPE:]], axis=2).transpose(1, 0, 2).reshape(MLA_QR, MLA_H * MLA_PAD)
    Wkvb2 = Wkvb.transpose(1, 0, 2).reshape(MLA_KVR, MLA_H * (MLA_NOPE + MLA_V))
    Wmla_out2 = Wmla_out.reshape(D, D)
    q_raw = _dense("mla_qb", cqn, Wqb2, NN, F32, tm=1024)
    kvh = _dense("mla_kvb", ckvn, Wkvb2, NN, F32, tm=1024)
    (att, lse, qf, kf), (Wffn_in1, Wffn_out1) = _mla_fwd(
        q_raw, kvh, k_rope, gains, mtabs, B, S, comm=_Gather([blk16("ffn_w_in", 1), blk16("ffn_w_out", 1)]))
    Wffn_out1 = Wffn_out1.reshape(4, FSH, D)
    x3 = _dense("mla_out", att, Wmla_out2, NN, F32, res=x2)
    y, ffn1_saved, _ = _ffn_fwd(1, x3, fnorm[1], Wffn_in1, cw[1], cb[1], Wffn_out1, B, S)

    dy, colsq, dyb = _loss(y, tgt)
    loss_part = 0.5 * jnp.sum(colsq) / D

    dx3, dx3b, gf1, r_ffn1_out, _ = _ffn_bwd(1, dy, dyb, ffn1_saved, fnorm[1], Wffn_in1, cw[1], cb[1], Wffn_out1, B, S,
                                             first_half_early=False)
    datt = _dense("mla_out_dx", dx3b, Wmla_out2, NT, BF16)
    fh = FSH // 2
    (dq_raw, dkvh, dkr, dqgn, dqgr, dkgn, dkgr), (r_ffn1_in_a, r_ffn1_in_b) = _mla_bwd(
        q_raw, kvh, k_rope, att, lse, datt, qf, kf, gains, mtabs, B, S,
        comm=_Exchange([gf1["w_in"], gf1["w_in"]], rows=[(0, fh), (fh, fh)]))
    dcqn = _dense("mla_qb_dx", dq_raw, Wqb2, NT, F32, tm=1024)
    dckvn = _dense("mla_kvb_dx", dkvh, Wkvb2, NT, F32, tm=1024)
    dcq, dg_qn = _rms_bwd("mla_q_norm_bwd", c_q, q_norm_full, dcqn)
    dckv, dg_kvn = _rms_bwd("mla_kv_norm_bwd", c_kv, kv_norm_full, dckvn)
    dqgr, dkgr = _fold(dqgr), _fold(dkgr)
    dproj2 = _bf16(jnp.concatenate([dcq, dckv, _fold(dkr)], axis=-1))
    dh2 = _dense("mla_in_dx", dproj2, Wmla_in, NT, F32)
    dx2, dg_mla_norm, dx2b = _rms_bwd("mla_norm_bwd", x2, mla_norm_full, dh2, dres=dx3, also_bf16=True)

    dx1, dx1b, gf0, r_ffn0_out, r_ffn0_in_a = _ffn_bwd(0, dx2, dx2b, ffn0_saved, fnorm[0], Wffn_in0, cw[0], cb[0],
                                                       Wffn_out0, B, S, first_half_early=True)
    drgt = _dense("ret_out_dx", dx1b, Wret_out, NT, BF16)
    dWret_out = _dense("ret_out_dw", rgt, dx1b, TN, BF16, tm=1024).reshape(NDEV, 256, D)
    (dproj, dgn_ret), (r_ffn0_in_b, r_ret_out) = _ret_bwd(
        proj, o_raw, states, drgt, rtabs, ret_gn_full, B, S,
        comm=_Exchange([gf0["w_in"], dWret_out], rows=[(fh, fh), None]))
    dWmla_out = _dense("mla_out_dw", att, dx3b, TN, BF16, tm=1024).reshape(NDEV, MLA_V, D)
    dWqb = _dense("mla_qb_dw", dq_raw, cqn, TN, BF16, tm=1024).reshape(MLA_H, MLA_PAD, MLA_QR)
    dWqb = jnp.concatenate([dWqb[:, :MLA_NOPE], dWqb[:, MLA_NOPE:MLA_QK] + dWqb[:, MLA_QK:]], axis=1)
    dWkvb = _dense("mla_kvb_dw", ckvn, dkvh, TN, BF16, tm=1024)
    dWkvb = dWkvb.reshape(MLA_KVR, MLA_H, MLA_NOPE + MLA_V).transpose(1, 0, 2)
    dWmla_in = _dense("mla_in_dw", h2, dproj2, TN, BF16).reshape(NDEV, 128, 704)

    tt, hk, qr = min(2048, T), D // 2, D // 4

    def ret_in_dw(name, half, comm):
        return _mm(name, h0, dproj, grid=(NDEV, T // tt), a_spec=_bs((tt, hk), lambda j, t: (t, half)),
                   b_spec=_bs((tt, 768), lambda j, t: (t, j)), o_spec=_bs((None, hk, 768), lambda j, t: (j, 0, 0)),
                   out_shape=_sds((NDEV, hk, 768), BF16), dims=TN, kax=1, acc_shape=(hk, 768), comm=comm)

    dW_top, r_mla = ret_in_dw("ret_in_dw_top", 0, _Exchange([dWmla_out, dWqb, dWkvb, dWmla_in]))
    dW_bot, (r_q0,) = ret_in_dw("ret_in_dw_bot", 1, _Exchange([dW_top], rows=[(0, qr)]))
    dh0, (r_q1, r_q2) = _mm(
        "ret_in_dx", dproj, Wret_in, grid=(T // 1024, NDEV), a_spec=_bs((1024, 768), lambda m, k: (m, k)),
        b_spec=_bs((None, D, 768), lambda m, k: (k, 0, 0)), o_spec=_bs((1024, D), lambda m, k: (m, 0)),
        out_shape=_sds((T, D), F32), dims=NT, kax=1, acc_shape=(1024, D),
        comm=_Exchange([dW_top, dW_bot], rows=[(qr, qr), (0, qr)]))
    (dx0, dg_ret_norm), (r_q3,) = _rms_bwd("ret_norm_bwd", x0, ret_norm.reshape(1, D), dh0, dres=dx1,
                                           comm=_Exchange([dW_bot], rows=[(qr, qr)]))
    grad_x = dx0.reshape(B, S, D)
    received = dict(ret_w_in=[r_q0, r_q1, r_q2, r_q3], ret_w_out=[r_ret_out], mla_w_out=[r_mla[0]], mla_w_qb=[r_mla[1]],
                    mla_w_kvb=[r_mla[2]], mla_w_in=[r_mla[3]],
                    ffn_w_in=[r_ffn0_in_a, r_ffn0_in_b, r_ffn1_in_a, r_ffn1_in_b], ffn_w_out=[r_ffn0_out, r_ffn1_out])

    dconv_w = jnp.stack([g_["cw"].transpose(1, 0, 2).reshape(3, FFN) for g_ in (gf0, gf1)])
    dconv_b = jnp.stack([g_["cb"].reshape(FFN) for g_ in (gf0, gf1)])
    small_parts = [dg_ret_norm, gf0["norm"], gf1["norm"], dg_mla_norm, dg_qn, dg_kvn, dqgn, dqgr, dkgn, dkgr, dgn_ret,
                   dconv_w, dconv_b, loss_part]
    small_g = jnp.concatenate([p.reshape(-1) for p in small_parts])
    n_grads = small_g.shape[0] - 1
    small_g = jnp.pad(small_g, (0, 240 * 128 - small_g.shape[0])).reshape(240, 128)
    small_all = _comm_call("gather_small_grads", _Gather([small_g]))[0]
    sred = _sum8("sum_small_grads", small_all).reshape(-1)
    loss = sred[n_grads]

    def take(n):
        nonlocal off
        out = sred[off:off + n]
        off += n
        return out

    off = 0
    g_small = dict(ret_norm=take(D).reshape(1, D), ffn_norm=take(2 * D).reshape(2, D), mla_norm=take(D),
                   mla_q_norm=take(MLA_QR), mla_kv_norm=take(MLA_KVR))
    g_small["mla_q_head_norm"] = take(MLA_QK).reshape(1, MLA_QK)
    g_small["mla_k_head_norm"] = take(MLA_QK).reshape(1, MLA_QK)
    g_small["ret_gn"] = take(RET_H * RET_DV).reshape(1, RET_H, RET_DV)
    g_small["ffn_conv_w"] = take(2 * 3 * FFN).reshape(2, 3, FFN)
    g_small["ffn_conv_b"] = take(2 * FFN).reshape(2, FFN)
    g_small["mla_norm"] = lax.dynamic_slice(g_small["mla_norm"], (dev * 128,), (128,)).reshape(1, 128)
    g_small["mla_q_norm"] = lax.dynamic_slice(g_small["mla_q_norm"], (dev * 48,), (48,)).reshape(1, 48)
    g_small["mla_kv_norm"] = lax.dynamic_slice(g_small["mla_kv_norm"], (dev * 32,), (32,)).reshape(1, 32)
    g_small["ret_gn"] = lax.dynamic_slice(g_small["ret_gn"], (0, 0, dev * 64), (1, RET_H, 64))
    g_small["ffn_conv_w"] = lax.dynamic_slice(g_small["ffn_conv_w"], (0, 0, dev * 352), (2, 3, 352))

    grads, delta, new_m, new_v = {}, {}, {}, {}
    for k in BIG:
        rcs = received[k]
        tr = max(t for t in range(16, 257, 16) if rcs[0].shape[1] % t == 0)
        flip = (lambda t: t.transpose(0, 2, 1)) if k in ("ffn_w_in", "mla_w_qb") else (lambda t: t)
        res = _adamw(f"adamw_{k}", rcs, flip(w[k]), flip(mom[k]), flip(var[k]), tr=tr)
        grads[k], delta[k], new_m[k], new_v[k] = (flip(t) for t in res)
    SMALL = REPL + SHARDED_SMALL

    def pack(d):
        vflat = jnp.concatenate([d[k].reshape(-1) for k in SMALL])
        return jnp.pad(vflat, (0, 96 * 128 - vflat.shape[0])).reshape(1, 96, 128)

    ps = _adamw("adamw_small", [pack(g_small)], pack(w), pack(mom), pack(var))
    off = 0
    for k in SMALL:
        n = w[k].size
        grads[k], delta[k], new_m[k], new_v[k] = (t.reshape(-1)[off:off + n].reshape(w[k].shape) for t in ps)
        off += n
    names = list(w)
    return (loss, grad_x, *[grads[k] for k in names], *[delta[k] for k in names], *[new_m[k] for k in names],
            *[new_v[k] for k in names])
```

```python
import functools

import jax
import jax.numpy as jnp
from jax import lax
from jax.experimental import pallas as pl
from jax.experimental.pallas import tpu as pltpu

F32, BF16 = jnp.float32, jnp.bfloat16

NDEV = 8
D_MODEL = 1024
CHUNK = 64
RMS_EPS = 1e-6
ROPE_THETA = 10000.0
RET_H, RET_DK, RET_DV = 4, 256, 512
RET_SC = 256
MLA_H, MLA_QR, MLA_KVR = 8, 384, 256
MLA_NOPE, MLA_ROPE, MLA_V = 128, 64, 128
MLA_QK = MLA_NOPE + MLA_ROPE
MASK_VALUE = -1e30
FFN = 2816
FSH = FFN * 2 // NDEV
ATT_TQ = 256
ADAM_LR, ADAM_B1, ADAM_B2, ADAM_EPS, ADAM_WD, ADAM_STEP = 0.001, 0.9, 0.999, 1e-08, 0.01, 10
MESH = pl.DeviceIdType.MESH
VMEM_LIMIT = 56 * 2 ** 20


def _cp(sem):
    return pltpu.CompilerParams(dimension_semantics=sem, vmem_limit_bytes=VMEM_LIMIT)


def _dot(a, b, dims):
    return lax.dot_general(a, b, (dims, ((), ())), preferred_element_type=F32)


NN = ((1,), (0,))
NT = ((1,), (1,))
TN = ((0,), (0,))


def _place():
    return lax.axis_index("x"), lax.axis_index("y"), lax.axis_index("c")


def _idx(d):
    return 4 * d[0] + 2 * d[1] + d[2]


ANY = pl.BlockSpec(memory_space=pl.ANY)


class _Gather:
    sem0 = 0

    def __init__(self, arrs, parts=1):
        self.srcs = list(arrs)
        self.parts = parts
        self.nsem = len(arrs) * parts
        self.out_shape = [jax.ShapeDtypeStruct((NDEV,) + a.shape, a.dtype) for a in arrs]

    def _copies(self, ins, outs, send, recv, loc):
        n = self.nsem
        s0 = self.sem0
        x, y, c = _place()
        me, sib = (x, y, c), (x, y, 1 - c)
        chips = [(1 - x, y), (x, 1 - y), (1 - x, 1 - y)]

        def piece(ref, v, *lead):
            a, p = divmod(v, self.parts)
            if self.parts > 1:
                rows = self.srcs[a].shape[0] // self.parts
                lead = (*lead, pl.ds(p * rows, rows))
            return ref[a].at[lead] if lead else ref[a]

        def cp(a, k, block, to, src=None):
            dst = piece(outs, a, _idx(block))
            return pltpu.make_async_remote_copy(src_ref=dst if src is None else src, dst_ref=dst,
                                                send_sem=send.at[s0 + a, k], recv_sem=recv.at[s0 + a, k], device_id=to,
                                                device_id_type=MESH)

        own = [piece(ins, a) for a in range(n)]
        mine = [pltpu.make_async_copy(own[a], piece(outs, a, _idx(me)), loc.at[s0 + a]) for a in range(n)]
        first = [cp(a, 0, me, sib, src=own[a]) for a in range(n)]
        first += [cp(a, 1 + j, me, (*chip, c), src=own[a]) for a in range(n) for j, chip in enumerate(chips)]
        landed = [cp(a, 1 + j, (*chip, c), me) for j, chip in enumerate(chips) for a in range(n)]
        passed = [cp(a, 4 + j, (*chip, c), sib) for j, chip in enumerate(chips) for a in range(n)]
        from_sib = [cp(a, 0, sib, me) for a in range(n)]
        from_sib += [cp(a, 4 + j, (*chip, 1 - c), me) for j, chip in enumerate(chips) for a in range(n)]
        return mine, first, landed, passed, from_sib

    def start(self, *refs):
        mine, first, _, _, _ = self._copies(*refs)
        for cp in mine + first:
            cp.start()

    def mid(self, *refs):
        _, _, landed, passed, _ = self._copies(*refs)
        for got, on in zip(landed, passed):
            got.wait_recv()
            on.start()

    def finish(self, *refs):
        mine, first, _, passed, from_sib = self._copies(*refs)
        for cp in from_sib:
            cp.wait_recv()
        for cp in first + passed:
            cp.wait_send()
        for cp in mine:
            cp.wait()


class _Exchange:
    sem0 = 0

    def __init__(self, arrs, rows=None):
        self.srcs = list(arrs)
        self.nsem = len(arrs)
        self.rows = rows if rows is not None else [None] * len(arrs)
        self.out_shape = [jax.ShapeDtypeStruct(a.shape if r is None else (a.shape[0], r[1]) + a.shape[2:], a.dtype)
                          for a, r in zip(arrs, self.rows)]

    def _copies(self, ins, outs, send, recv, loc):
        n = len(self.srcs)
        s0 = self.sem0
        x, y, c = _place()
        me = _idx((x, y, c))

        def src(a, q):
            r = self.rows[a]
            return ins[a].at[q] if r is None else ins[a].at[q, pl.ds(r[0], r[1])]

        mine = [pltpu.make_async_copy(src(a, me), outs[a].at[me], loc.at[s0 + a]) for a in range(n)]
        remote = []
        for k in range(1, NDEV):
            peer = (x ^ (k >> 2), y ^ ((k >> 1) & 1), c ^ (k & 1))
            remote += [pltpu.make_async_remote_copy(
                src_ref=src(a, _idx(peer)), dst_ref=outs[a].at[me], send_sem=send.at[s0 + a, k - 1],
                recv_sem=recv.at[s0 + a, k - 1], device_id=peer, device_id_type=MESH) for a in range(n)]
        return mine, remote

    def start(self, *refs):
        mine, remote = self._copies(*refs)
        for cp in mine + remote:
            cp.start()

    def mid(self, *refs):
        pass

    def finish(self, *refs):
        mine, remote = self._copies(*refs)
        for cp in remote + mine:
            cp.wait()


def _comm_scratch(n):
    return [pltpu.SemaphoreType.DMA((n, 7)), pltpu.SemaphoreType.DMA((n, 7)), pltpu.SemaphoreType.DMA((n,))]


def _comm_call(name, comm):
    n = len(comm.srcs)

    def body(*refs):
        parts = (refs[:n], refs[n:2 * n]) + tuple(refs[2 * n:])
        comm.start(*parts)
        comm.mid(*parts)
        comm.finish(*parts)

    return pl.pallas_call(body, name=name, in_specs=[ANY] * n, out_specs=[ANY] * n, out_shape=comm.out_shape,
                          scratch_shapes=_comm_scratch(comm.nsem))(*comm.srcs)


def _pcall(body, *, name, grid, in_specs, out_specs, out_shape, scratch_shapes, sem, args, comm=None):
    if comm is None:
        return pl.pallas_call(body, name=name, grid=grid, in_specs=in_specs, out_specs=out_specs, out_shape=out_shape,
                              scratch_shapes=scratch_shapes, compiler_params=_cp(sem))(*args), None
    ni, no, ns, nc = len(in_specs), len(out_shape), len(scratch_shapes), len(comm.srcs)
    total = 1
    for g in grid:
        total *= g
    middle = (4 * total) // 5

    def wrapped(*refs):
        ins, csrc = refs[:ni], refs[ni:ni + nc]
        outs, cdst = refs[ni + nc:ni + nc + no], refs[ni + nc + no:ni + 2 * nc + no]
        scr, sems = refs[ni + 2 * nc + no:ni + 2 * nc + no + ns], refs[ni + 2 * nc + no + ns:]
        step = pl.program_id(0)
        for k in range(1, len(grid)):
            step = step * grid[k] + pl.program_id(k)
        parts = (csrc, cdst) + tuple(sems)

        @pl.when(step == 0)
        def _():
            comm.start(*parts)

        body(*ins, *outs, *scr)

        @pl.when(step == middle)
        def _():
            comm.mid(*parts)

        @pl.when(step == total - 1)
        def _():
            comm.finish(*parts)

    res = pl.pallas_call(
        wrapped, name=name, grid=grid, in_specs=list(in_specs) + [ANY] * nc, out_specs=list(out_specs) + [ANY] * nc,
        out_shape=list(out_shape) + comm.out_shape, scratch_shapes=list(scratch_shapes) + _comm_scratch(comm.nsem),
        compiler_params=_cp(("arbitrary",) * len(grid)))(*args, *comm.srcs)
    return res[:no], res[no:]


def _mm(name, a, b, *, grid, a_spec, b_spec, o_spec, out_shape, dims, kax=None, res=None, res_spec=None,
        jb=0, acc_shape=None, comm=None):
    nk = grid[kax] if kax is not None else 1

    def body(*refs):
        if res is not None:
            a_ref, b_ref, r_ref, o_ref = refs[:4]
        else:
            a_ref, b_ref, o_ref = refs[:3]

        def product():
            if not jb:
                return _dot(a_ref[...], b_ref[...], dims)
            part = _dot(a_ref[0], b_ref[0], dims)
            for j in range(1, jb):
                part = part + _dot(a_ref[j], b_ref[j], dims)
            return part

        def fin(acc):
            if res is not None:
                acc = acc + r_ref[...]
            o_ref[...] = acc.astype(o_ref.dtype)

        if nk == 1:
            fin(product())
        else:
            acc_ref = refs[-1]
            k = pl.program_id(kax)

            @pl.when(k == 0)
            def _():
                acc_ref[...] = jnp.zeros_like(acc_ref)

            acc_ref[...] += product()

            @pl.when(k == nk - 1)
            def _():
                fin(acc_ref[...])

    sem = tuple("arbitrary" if i == kax else "parallel" for i in range(len(grid)))
    in_specs = [a_spec, b_spec] + ([res_spec] if res is not None else [])
    args = (a, b) + ((res,) if res is not None else ())
    scratch = [pltpu.VMEM(acc_shape, F32)] if nk > 1 else []
    (out,), got = _pcall(body, name=name, grid=grid, in_specs=in_specs, out_specs=[o_spec], out_shape=[out_shape],
                         scratch_shapes=scratch, sem=sem, args=args, comm=comm)
    return out if comm is None else (out, got)


def _bs(shape, fn):
    return pl.BlockSpec(shape, fn)


def _rms_fwd(name, x, g, tm=512):
    T, D = x.shape

    def body(x_ref, g_ref, o_ref):
        xf = x_ref[...]
        r = lax.rsqrt(jnp.mean(xf * xf, axis=-1, keepdims=True) + RMS_EPS)
        o_ref[...] = (xf * r * g_ref[...]).astype(o_ref.dtype)

    return pl.pallas_call(
        body, name=name, grid=(T // tm,),
        in_specs=[_bs((tm, D), lambda i: (i, 0)), _bs((1, D), lambda i: (0, 0))],
        out_specs=_bs((tm, D), lambda i: (i, 0)), out_shape=jax.ShapeDtypeStruct((T, D), BF16),
        compiler_params=_cp(("parallel",)))(x, g)


def _rms_bwd(name, x, g, dh, dres=None, tm=512, also_bf16=False, comm=None):
    T, D = x.shape

    def body(*refs):
        if also_bf16:
            refs, dxb_ref = refs[:-1], refs[-1]
        if dres is not None:
            x_ref, g_ref, dh_ref, dres_ref, dx_ref, dg_ref = refs
        else:
            x_ref, g_ref, dh_ref, dx_ref, dg_ref = refs
        i = pl.program_id(0)
        xf = x_ref[...]
        r = lax.rsqrt(jnp.mean(xf * xf, axis=-1, keepdims=True) + RMS_EPS)
        xh = xf * r
        d = dh_ref[...].astype(F32)
        dxh = d * g_ref[...]
        dx = r * (dxh - xh * jnp.mean(dxh * xh, axis=-1, keepdims=True))
        if dres is not None:
            dx = dx + dres_ref[...]
        dx_ref[...] = dx
        if also_bf16:
            dxb_ref[...] = dx.astype(BF16)
        part = jnp.sum(d * xh, axis=0, keepdims=True)

        @pl.when(i == 0)
        def _():
            dg_ref[...] = part

        @pl.when(i > 0)
        def _():
            dg_ref[...] += part

    row = _bs((tm, D), lambda i: (i, 0))
    vec = _bs((1, D), lambda i: (0, 0))
    in_specs = [row, vec, row] + ([row] if dres is not None else [])
    args = (x, g, dh) + ((dres,) if dres is not None else ())
    extra = [jax.ShapeDtypeStruct((T, D), BF16)] if also_bf16 else []
    outs, got = _pcall(
        body, name=name, grid=(T // tm,), in_specs=in_specs, out_specs=[row, vec] + [row] * len(extra),
        out_shape=[jax.ShapeDtypeStruct((T, D), F32), jax.ShapeDtypeStruct((1, D), F32)] + extra, scratch_shapes=[],
        sem=("arbitrary",), args=args, comm=comm)
    return outs if comm is None else (outs, got)


def _loss(y, tgt, tm=512):
    T, D = y.shape

    def body(y_ref, t_ref, dy_ref, s_ref, dyb_ref):
        i = pl.program_id(0)
        e = y_ref[...] - t_ref[...]
        dy = e * (1.0 / D)
        dy_ref[...] = dy
        dyb_ref[...] = dy.astype(BF16)
        part = jnp.sum(e * e, axis=0, keepdims=True)

        @pl.when(i == 0)
        def _():
            s_ref[...] = part

        @pl.when(i > 0)
        def _():
            s_ref[...] += part

    row = _bs((tm, D), lambda i: (i, 0))
    return pl.pallas_call(
        body, name="loss_head", grid=(T // tm,), in_specs=[row, row],
        out_specs=[row, _bs((1, D), lambda i: (0, 0)), row],
        out_shape=[jax.ShapeDtypeStruct((T, D), F32), jax.ShapeDtypeStruct((1, D), F32),
                   jax.ShapeDtypeStruct((T, D), BF16)],
        compiler_params=_cp(("arbitrary",)))(y, tgt)


def _shift_rows(t, k, row):
    return jnp.where(row >= k, pltpu.roll(t, k, 0), 0.0)


def _shift_rows_up(t, k, row, n):
    return jnp.where(row < n - k, pltpu.roll(t, n - k, 0), 0.0)


def _convffn_fwd(name, u, cw, cb, B, S):
    _, J, T, F = u.shape

    def body(u_ref, cw_ref, cb_ref, o_ref):
        a = u_ref[0].astype(F32)
        g = u_ref[1].astype(F32)
        row = lax.broadcasted_iota(jnp.int32, (S, F), 0)
        w0, w1, w2 = cw_ref[0:1, :], cw_ref[1:2, :], cw_ref[2:3, :]
        gc = _shift_rows(g, 2, row) * w0 + _shift_rows(g, 1, row) * w1 + g * w2 + cb_ref[...]
        o_ref[...] = (gc * jax.nn.sigmoid(gc) * a).astype(o_ref.dtype)

    return pl.pallas_call(
        body, name=name, grid=(J, B),
        in_specs=[_bs((2, None, S, F), lambda j, b: (0, j, b, 0)), _bs((None, 3, F), lambda j, b: (j, 0, 0)),
                  _bs((None, 1, F), lambda j, b: (j, 0, 0))],
        out_specs=_bs((None, S, F), lambda j, b: (j, b, 0)), out_shape=jax.ShapeDtypeStruct((J, T, F), BF16),
        compiler_params=_cp(("parallel", "parallel")))(u, cw, cb)


def _convffn_bwd(name, u, cw, cb, dgt, B, S, comm=None):
    _, J, T, F = u.shape

    def body(u_ref, cw_ref, cb_ref, d_ref, du_ref, dcw_ref, dcb_ref):
        b = pl.program_id(1)
        a = u_ref[0].astype(F32)
        g = u_ref[1].astype(F32)
        d = d_ref[...].astype(F32)
        row = lax.broadcasted_iota(jnp.int32, (S, F), 0)
        w0, w1, w2 = cw_ref[0:1, :], cw_ref[1:2, :], cw_ref[2:3, :]
        g1, g2 = _shift_rows(g, 1, row), _shift_rows(g, 2, row)
        gc = g2 * w0 + g1 * w1 + g * w2 + cb_ref[...]
        sg = jax.nn.sigmoid(gc)
        du_ref[0] = (d * gc * sg).astype(du_ref.dtype)
        dgc = d * a * (sg * (1.0 + gc * (1.0 - sg)))
        dg = dgc * w2 + _shift_rows_up(dgc, 1, row, S) * w1 + _shift_rows_up(dgc, 2, row, S) * w0
        du_ref[1] = dg.astype(du_ref.dtype)
        parts = [jnp.sum(dgc * g2, axis=0, keepdims=True), jnp.sum(dgc * g1, axis=0, keepdims=True),
                 jnp.sum(dgc * g, axis=0, keepdims=True)]
        pb = jnp.sum(dgc, axis=0, keepdims=True)

        @pl.when(b == 0)
        def _():
            for k in range(3):
                dcw_ref[k:k + 1, :] = parts[k]
            dcb_ref[...] = pb

        @pl.when(b > 0)
        def _():
            for k in range(3):
                dcw_ref[k:k + 1, :] += parts[k]
            dcb_ref[...] += pb

    uspec = _bs((2, None, S, F), lambda j, b: (0, j, b, 0))
    return _pcall(
        body, name=name, grid=(J, B),
        in_specs=[uspec, _bs((None, 3, F), lambda j, b: (j, 0, 0)), _bs((None, 1, F), lambda j, b: (j, 0, 0)),
                  _bs((None, S, F), lambda j, b: (j, b, 0))],
        out_specs=[uspec, _bs((None, 3, F), lambda j, b: (j, 0, 0)), _bs((None, 1, F), lambda j, b: (j, 0, 0))],
        out_shape=[jax.ShapeDtypeStruct(u.shape, BF16), jax.ShapeDtypeStruct((J, 3, F), F32),
                   jax.ShapeDtypeStruct((J, 1, F), F32)],
        scratch_shapes=[], sem=("parallel", "arbitrary"), args=(u, cw, cb, dgt), comm=comm)


def _ret_tables(S):
    half = RET_DK // 2
    inv = ROPE_THETA ** (-jnp.arange(half, dtype=F32) / half)
    ang = jnp.arange(S).astype(F32)[:, None] * inv[None, :]
    lg = jnp.log1p(-jnp.exp2(-5.0 - jnp.arange(RET_H, dtype=F32)))
    i = jnp.arange(RET_SC, dtype=F32)
    same_or_earlier = (jnp.floor(i[None, :] / CHUNK) <= jnp.floor(i[:, None] / CHUNK)).astype(F32)
    dm = jnp.exp(lg[:, None, None] * jnp.abs(i[:, None] - i[None, :])) * same_or_earlier[None]
    qd = jnp.exp(lg[:, None] * (i + 1.0))[:, :, None]
    kd = jnp.exp(lg[:, None] * (RET_SC - 1.0 - i))[:, :, None]
    cd = jnp.exp(lg * RET_SC)[:, None, None]
    return jnp.cos(ang), jnp.sin(ang), dm, qd, kd, cd


def _rope_halves(t, cs, sn):
    h = t.shape[-1] // 2
    t1, t2 = t[:, :h], t[:, h:]
    return jnp.concatenate([t1 * cs - t2 * sn, t2 * cs + t1 * sn], axis=-1)


def _unrope_halves(d, cs, sn):
    h = d.shape[-1] // 2
    d1, d2 = d[:, :h], d[:, h:]
    return jnp.concatenate([d1 * cs + d2 * sn, d2 * cs - d1 * sn], axis=-1)


def _ret_specs(nC, order):
    SC = RET_SC

    def sp(shape, fn):
        return _bs(shape, lambda *g: fn(*order(*g)))

    q = sp((SC, RET_DK), lambda b, h, c: (b * nC + c, h))
    k = sp((SC, RET_DK), lambda b, h, c: (b * nC + c, RET_H + h))
    v = sp((SC, RET_DV), lambda b, h, c: (b * nC + c, RET_H + h))
    g = sp((SC, RET_DV), lambda b, h, c: (b * nC + c, 2 * RET_H + h))
    cs = sp((SC, RET_DK // 2), lambda b, h, c: (c, 0))
    dm = sp((None, SC, SC), lambda b, h, c: (h, 0, 0))
    dv = sp((None, SC, 1), lambda b, h, c: (h, 0, 0))
    cd = sp((None, 1, 1), lambda b, h, c: (h, 0, 0))
    gn = sp((None, 1, RET_DV), lambda b, h, c: (h, 0, 0))
    wide = sp((SC, RET_DV), lambda b, h, c: (b * nC + c, h))
    narrow = sp((SC, RET_DK), lambda b, h, c: (b * nC + c, h))
    st = sp((None, None, None, RET_DK, RET_DV), lambda b, h, c: (b, h, c, 0, 0))
    return dict(q=q, k=k, v=v, g=g, cs=cs, dm=dm, dv=dv, cd=cd, gn=gn, wide=wide, narrow=narrow, st=st)


def _ret_fwd(proj, tabs, gn, B, S, comm=None):
    T = B * S
    nC = S // RET_SC
    cos, sin, dm, qd, kd, cd = tabs
    s = _ret_specs(nC, lambda b, h, c: (b, h, c))

    def body(q_ref, k_ref, v_ref, g_ref, cos_ref, sin_ref, dm_ref, qd_ref, kd_ref, cd_ref, gn_ref,
             o_ref, gt_ref, st_ref, state):
        c = pl.program_id(2)

        @pl.when(c == 0)
        def _():
            state[...] = jnp.zeros_like(state)

        cs, sn = cos_ref[...], sin_ref[...]
        qf = _rope_halves(q_ref[...].astype(F32), cs, sn)
        kf = _rope_halves(k_ref[...].astype(F32), cs, sn) * (RET_DK ** -0.5)
        v = v_ref[...]
        p = _dot(qf.astype(BF16), kf.astype(BF16), NT) * dm_ref[...]
        st = state[...]
        stb = st.astype(BF16)
        st_ref[...] = stb
        o = _dot(p.astype(BF16), v, NN) + _dot((qf * qd_ref[...]).astype(BF16), stb, NN)
        state[...] = st * cd_ref[...] + _dot((kf * kd_ref[...]).astype(BF16), v, TN)
        o_ref[...] = o
        r = lax.rsqrt(jnp.mean(o * o, axis=-1, keepdims=True) + RMS_EPS)
        gf = g_ref[...].astype(F32)
        gt_ref[...] = ((o * r * gn_ref[...]) * (gf * jax.nn.sigmoid(gf))).astype(BF16)

    return _pcall(
        body, name="ret_fwd", grid=(B, RET_H, nC),
        in_specs=[s["q"], s["k"], s["v"], s["g"], s["cs"], s["cs"], s["dm"], s["dv"], s["dv"], s["cd"], s["gn"]],
        out_specs=[s["wide"], s["wide"], s["st"]],
        out_shape=[jax.ShapeDtypeStruct((T, RET_H * RET_DV), F32), jax.ShapeDtypeStruct((T, RET_H * RET_DV), BF16),
                   jax.ShapeDtypeStruct((B, RET_H, nC, RET_DK, RET_DV), BF16)],
        scratch_shapes=[pltpu.VMEM((RET_DK, RET_DV), F32)], sem=("parallel", "parallel", "arbitrary"),
        args=(proj, proj, proj, proj, cos, sin, dm, qd, kd, cd, gn), comm=comm)


def _ret_bwd(proj, o_raw, states, dgt, tabs, gn, B, S, comm=None):
    T = B * S
    nC = S // RET_SC
    cos, sin, dm, qd, kd, cd = tabs
    s = _ret_specs(nC, lambda b, c, h: (b, h, nC - 1 - c))

    def body(q_ref, k_ref, v_ref, g_ref, o_ref, st_ref, d_ref, cos_ref, sin_ref, dm_ref, qd_ref, kd_ref, cd_ref,
             gn_ref, dproj_ref, dgn_ref, dstates):
        b, c, h = pl.program_id(0), pl.program_id(1), pl.program_id(2)
        dstate = dstates.at[h]

        @pl.when(c == 0)
        def _():
            dstate[...] = jnp.zeros_like(dstate)

        @pl.when((b == 0) & (c == 0))
        def _():
            dgn_ref[h] = jnp.zeros((1, RET_DV), F32)

        cs, sn = cos_ref[...], sin_ref[...]
        qf = _rope_halves(q_ref[...].astype(F32), cs, sn)
        kf = _rope_halves(k_ref[...].astype(F32), cs, sn) * (RET_DK ** -0.5)
        v = v_ref[...]
        gnv = gn_ref[h]
        o = o_ref[...]
        r = lax.rsqrt(jnp.mean(o * o, axis=-1, keepdims=True) + RMS_EPS)
        oh = o * r
        gf = g_ref[...].astype(F32)
        sg = jax.nn.sigmoid(gf)
        d = d_ref[...].astype(F32)
        dg = (d * (oh * gnv) * (sg * (1.0 + gf * (1.0 - sg)))).astype(BF16)
        don = d * (gf * sg)
        dgn_ref[h] += jnp.sum(don * oh, axis=0, keepdims=True)
        doh = don * gnv
        dO = (r * (doh - oh * jnp.mean(doh * oh, axis=-1, keepdims=True))).astype(BF16)
        dmv = dm_ref[h]
        qb, kb = qf.astype(BF16), kf.astype(BF16)
        p = (_dot(qb, kb, NT) * dmv).astype(BF16)
        dp = (_dot(dO, v, NT) * dmv).astype(BF16)
        st = st_ref[...]
        dsn = dstate[...]
        dsb = dsn.astype(BF16)
        qdv, kdv = qd_ref[h], kd_ref[h]
        dq = _dot(dp, kb, NN) + _dot(dO, st, NT) * qdv
        dk = _dot(dp, qb, TN) + _dot(v, dsb, NT) * kdv
        dv = _dot(p, dO, TN) + _dot((kf * kdv).astype(BF16), dsb, NN)
        dstate[...] = dsn * cd_ref[h] + _dot((qf * qdv).astype(BF16), dO, TN)
        dq = _unrope_halves(dq, cs, sn).astype(BF16)
        dk = (_unrope_halves(dk, cs, sn) * (RET_DK ** -0.5)).astype(BF16)
        dv = dv.astype(BF16)
        nq, nv = RET_H * RET_DK, RET_H * RET_DV
        for hh in range(RET_H):
            @pl.when(h == hh)
            def _():
                dproj_ref[:, hh * RET_DK:(hh + 1) * RET_DK] = dq
                dproj_ref[:, nq + hh * RET_DK:nq + (hh + 1) * RET_DK] = dk
                dproj_ref[:, 2 * nq + hh * RET_DV:2 * nq + (hh + 1) * RET_DV] = dv
                dproj_ref[:, 2 * nq + nv + hh * RET_DV:2 * nq + nv + (hh + 1) * RET_DV] = dg

    width = 2 * RET_H * (RET_DK + RET_DV)

    def all_heads(*shape):
        return _bs((RET_H,) + shape, lambda b, c, h: (0,) * (1 + len(shape)))

    return _pcall(
        body, name="ret_bwd", grid=(B, nC, RET_H),
        in_specs=[s["q"], s["k"], s["v"], s["g"], s["wide"], s["st"], s["wide"], s["cs"], s["cs"],
                  all_heads(RET_SC, RET_SC), all_heads(RET_SC, 1), all_heads(RET_SC, 1), all_heads(1, 1),
                  all_heads(1, RET_DV)],
        out_specs=[_bs((RET_SC, width), lambda b, c, h: (b * nC + nC - 1 - c, 0)),
                   _bs((RET_H, 1, RET_DV), lambda b, c, h: (0, 0, 0))],
        out_shape=[jax.ShapeDtypeStruct((T, width), BF16), jax.ShapeDtypeStruct((RET_H, 1, RET_DV), F32)],
        scratch_shapes=[pltpu.VMEM((RET_H, RET_DK, RET_DV), F32)], sem=("arbitrary", "arbitrary", "arbitrary"),
        args=(proj, proj, proj, proj, o_raw, states, dgt, cos, sin, dm, qd, kd, cd, gn), comm=comm)


MLA_PAD = 256
MLA_R2 = 2 * MLA_ROPE


def _dup(t):
    return jnp.concatenate([t, t], axis=-1)


def _fold(t):
    return t[..., :MLA_ROPE] + t[..., MLA_ROPE:]


def _mla_tables(S):
    half = MLA_ROPE // 2
    inv = ROPE_THETA ** (-jnp.arange(half, dtype=F32) / half)
    ang = jnp.arange(S).astype(F32)[:, None] * inv[None, :]
    cos, sin, zero = jnp.cos(ang), jnp.sin(ang), jnp.zeros((S, MLA_ROPE), F32)
    return jnp.concatenate([cos, cos, zero], axis=-1), jnp.concatenate([-sin, sin, zero], axis=-1)


def _head_norm_rope(n, r2, gn, gr2, cos, sin, scale):
    ssq = jnp.sum(n * n, axis=-1, keepdims=True) + 0.5 * jnp.sum(r2 * r2, axis=-1, keepdims=True)
    rstd = lax.rsqrt(ssq * (1.0 / MLA_QK) + RMS_EPS)
    yn = n * rstd * gn
    yr = r2 * rstd * gr2
    z = yr * cos + pltpu.roll(yr, MLA_ROPE // 2, 1) * sin
    if scale != 1.0:
        yn, z = yn * scale, z * scale
    return yn, z


def _head_norm_rope_bwd(dn, dz, n, r2, gn, gr2, cos, sin, scale):
    ssq = jnp.sum(n * n, axis=-1, keepdims=True) + 0.5 * jnp.sum(r2 * r2, axis=-1, keepdims=True)
    rstd = lax.rsqrt(ssq * (1.0 / MLA_QK) + RMS_EPS)
    hn, hr = n * rstd, r2 * rstd
    if scale != 1.0:
        dn, dz = dn * scale, dz * scale
    dyr = dz * cos + pltpu.roll(dz * sin, MLA_R2 - MLA_ROPE // 2, 1)
    dgn = jnp.sum(dn * hn, axis=0, keepdims=True)
    dgr = jnp.sum(dyr * hr, axis=0, keepdims=True)
    dhn, dhr = dn * gn, dyr * gr2
    mt = (jnp.sum(dhn * hn, axis=-1, keepdims=True) + jnp.sum(dhr * hr, axis=-1, keepdims=True)) * (1.0 / MLA_QK)
    return rstd * (dhn - hn * mt), rstd * (dhr - 0.5 * hr * mt), dgn, dgr


def _diag_bias():
    i = jnp.arange(ATT_TQ)
    return jnp.where((i[None, :] // CHUNK) <= (i[:, None] // CHUNK), 0.0, MASK_VALUE).astype(F32)


def _store_pair(dst, rows, n, r2):
    dst[rows, :MLA_NOPE] = n.astype(BF16)
    dst[rows, MLA_NOPE:] = r2.astype(BF16)


def _mla_fwd(q_raw, kv, kr, gains, tabs, B, S, comm=None):
    T = B * S
    TQ = ATT_TQ
    nQ = S // TQ
    qgn, qgr, kgn, kgr = gains
    cos, sin = tabs
    scale = MLA_QK ** -0.5

    def body(q_ref, kv_ref, kr_ref, qgn_ref, qgr_ref, kgn_ref, kgr_ref, c_ref, s_ref, bias_ref,
             o_ref, lse_ref, qf_s, kf_s, v_s):
        def prep(t, _):
            rows = pl.ds(pl.multiple_of(t * TQ, TQ), TQ)
            cs, sn = c_ref[rows, :], s_ref[rows, :]
            qn, qr = _head_norm_rope(q_ref[rows, :MLA_NOPE], q_ref[rows, MLA_NOPE:], qgn_ref[...], qgr_ref[...],
                                     cs, sn, scale)
            _store_pair(qf_s, rows, qn, qr)
            kn, krr = _head_norm_rope(kv_ref[rows, :MLA_NOPE], kr_ref[rows, :], kgn_ref[...], kgr_ref[...], cs, sn, 1.0)
            _store_pair(kf_s, rows, kn, krr)
            v_s[rows, :] = kv_ref[rows, MLA_NOPE:].astype(BF16)
            return 0

        lax.fori_loop(0, nQ, prep, 0, unroll=2)
        for i in range(nQ):
            rows = slice(i * TQ, (i + 1) * TQ)
            q = qf_s[rows, :]
            sd = _dot(q, kf_s[rows, :], NT) + bias_ref[...]
            m = jnp.max(sd, axis=-1, keepdims=True)
            if i:
                sl = _dot(q, kf_s[:i * TQ, :], NT)
                m = jnp.maximum(m, jnp.max(sl, axis=-1, keepdims=True))
            pd = jnp.exp(sd - m)
            l = jnp.sum(pd, axis=-1, keepdims=True)
            acc = _dot(pd.astype(BF16), v_s[rows, :], NN)
            if i:
                pl_ = jnp.exp(sl - m)
                l = l + jnp.sum(pl_, axis=-1, keepdims=True)
                acc = acc + _dot(pl_.astype(BF16), v_s[:i * TQ, :], NN)
            o_ref[rows, :] = (acc / l).astype(BF16)
            lse_ref[rows, :] = m + jnp.log(l)

    def vec(n):
        return _bs((1, n), lambda b, h: (0, 0))

    def cols(n):
        return _bs((S, n), lambda b, h: (b, h))

    tab = _bs((S, MLA_R2), lambda b, h: (0, 0))
    return _pcall(
        body, name="mla_fwd", grid=(B, MLA_H),
        in_specs=[cols(MLA_PAD), cols(MLA_NOPE + MLA_V), _bs((S, MLA_R2), lambda b, h: (b, 0)),
                  vec(MLA_NOPE), vec(MLA_R2), vec(MLA_NOPE), vec(MLA_R2), tab, tab,
                  _bs((TQ, TQ), lambda b, h: (0, 0))],
        out_specs=[cols(MLA_V), _bs((None, S, 1), lambda b, h: (h, b, 0)), cols(MLA_PAD), cols(MLA_PAD)],
        out_shape=[jax.ShapeDtypeStruct((T, MLA_H * MLA_V), BF16), jax.ShapeDtypeStruct((MLA_H, T, 1), F32),
                   jax.ShapeDtypeStruct((T, MLA_H * MLA_PAD), BF16), jax.ShapeDtypeStruct((T, MLA_H * MLA_PAD), BF16)],
        scratch_shapes=[pltpu.VMEM((S, MLA_V), BF16)],
        sem=("parallel", "parallel"), args=(q_raw, kv, kr, qgn, qgr, kgn, kgr, cos, sin, _diag_bias()), comm=comm)


def _mla_bwd(q_raw, kv, kr, o, lse, do, qf, kf, gains, tabs, B, S, comm=None):
    T = B * S
    TQ = ATT_TQ
    nQ = S // TQ
    qgn, qgr, kgn, kgr = gains
    cos, sin = tabs
    scale = MLA_QK ** -0.5

    def body(q_ref, kv_ref, kr_ref, o_ref, lse_ref, do_ref, qf_s, kf_s, qgn_ref, qgr_ref, kgn_ref, kgr_ref, c_ref, s_ref,
             bias_ref, dq_ref, dkv_ref, dkr_ref, dqgn_ref, dqgr_ref, dkgn_ref, dkgr_ref,
             v_s, dl_s, dq_s, dk_s, dv_s):
        b, h = pl.program_id(0), pl.program_id(1)

        def blk(t):
            return pl.ds(pl.multiple_of(t * TQ, TQ), TQ)

        def prep(t, _):
            rows = blk(t)
            v_s[rows, :] = kv_ref[rows, MLA_NOPE:].astype(BF16)
            dl_s[rows, :] = jnp.sum(do_ref[rows, :].astype(F32) * o_ref[rows, :].astype(F32), axis=-1, keepdims=True)
            dk_s[rows, :] = jnp.zeros((TQ, MLA_PAD), F32)
            dv_s[rows, :] = jnp.zeros((TQ, MLA_V), F32)
            return 0

        lax.fori_loop(0, nQ, prep, 0, unroll=2)

        gqn, gqr = jnp.zeros((1, MLA_NOPE), F32), jnp.zeros((1, MLA_R2), F32)
        for i in range(nQ):
            rows = slice(i * TQ, (i + 1) * TQ)
            q, doi, lse_i, dl_i = qf_s[rows, :], do_ref[rows, :], lse_ref[rows, :], dl_s[rows, :]

            def part(cols, bias):
                k, v = kf_s[cols, :], v_s[cols, :]
                s = _dot(q, k, NT)
                if bias is not None:
                    s = s + bias
                p = jnp.exp(s - lse_i)
                ds = (p * (_dot(doi, v, NT) - dl_i)).astype(BF16)
                dk_s[cols, :] += _dot(ds, q, TN)
                dv_s[cols, :] += _dot(p.astype(BF16), doi, TN)
                return _dot(ds, k, NN)

            dq = part(rows, bias_ref[...])
            if i:
                dq = dq + part(slice(0, i * TQ), None)
            dq_s[...] = dq
            dqn, dqr, a0, a1 = _head_norm_rope_bwd(dq_s[:, :MLA_NOPE], dq_s[:, MLA_NOPE:], q_ref[rows, :MLA_NOPE],
                                                   q_ref[rows, MLA_NOPE:], qgn_ref[...], qgr_ref[...],
                                                   c_ref[rows, :], s_ref[rows, :], scale)
            _store_pair(dq_ref, rows, dqn, dqr)
            gqn, gqr = gqn + a0, gqr + a1

        def post(t, carry):
            rows = blk(t)
            dkn, dkr, a2, a3 = _head_norm_rope_bwd(dk_s[rows, :MLA_NOPE], dk_s[rows, MLA_NOPE:],
                                                   kv_ref[rows, :MLA_NOPE], kr_ref[rows, :], kgn_ref[...], kgr_ref[...],
                                                   c_ref[rows, :], s_ref[rows, :], 1.0)
            dkv_ref[rows, :MLA_NOPE] = dkn.astype(BF16)
            dkv_ref[rows, MLA_NOPE:] = dv_s[rows, :].astype(BF16)

            @pl.when(h == 0)
            def _():
                dkr_ref[rows, :] = dkr

            @pl.when(h > 0)
            def _():
                dkr_ref[rows, :] += dkr

            return carry[0] + a2, carry[1] + a3

        gkn, gkr = lax.fori_loop(0, nQ, post, (jnp.zeros((1, MLA_NOPE), F32), jnp.zeros((1, MLA_R2), F32)), unroll=2)
        first = (b == 0) & (h == 0)

        @pl.when(first)
        def _():
            dqgn_ref[...] = gqn
            dqgr_ref[...] = gqr
            dkgn_ref[...] = gkn
            dkgr_ref[...] = gkr

        @pl.when(jnp.logical_not(first))
        def _():
            dqgn_ref[...] += gqn
            dqgr_ref[...] += gqr
            dkgn_ref[...] += gkn
            dkgr_ref[...] += gkr

    def vec(n):
        return _bs((1, n), lambda b, h: (0, 0))

    def cols(n):
        return _bs((S, n), lambda b, h: (b, h))

    tab = _bs((S, MLA_R2), lambda b, h: (0, 0))
    return _pcall(
        body, name="mla_bwd", grid=(B, MLA_H),
        in_specs=[cols(MLA_PAD), cols(MLA_NOPE + MLA_V), _bs((S, MLA_R2), lambda b, h: (b, 0)), cols(MLA_V),
                  _bs((None, S, 1), lambda b, h: (h, b, 0)), cols(MLA_V), cols(MLA_PAD), cols(MLA_PAD),
                  vec(MLA_NOPE), vec(MLA_R2), vec(MLA_NOPE), vec(MLA_R2), tab, tab,
                  _bs((TQ, TQ), lambda b, h: (0, 0))],
        out_specs=[cols(MLA_PAD), cols(MLA_NOPE + MLA_V), _bs((S, MLA_R2), lambda b, h: (b, 0)),
                   vec(MLA_NOPE), vec(MLA_R2), vec(MLA_NOPE), vec(MLA_R2)],
        out_shape=[jax.ShapeDtypeStruct((T, MLA_H * MLA_PAD), BF16),
                   jax.ShapeDtypeStruct((T, MLA_H * (MLA_NOPE + MLA_V)), BF16),
                   jax.ShapeDtypeStruct((T, MLA_R2), F32), jax.ShapeDtypeStruct((1, MLA_NOPE), F32),
                   jax.ShapeDtypeStruct((1, MLA_R2), F32), jax.ShapeDtypeStruct((1, MLA_NOPE), F32),
                   jax.ShapeDtypeStruct((1, MLA_R2), F32)],
        scratch_shapes=[pltpu.VMEM((S, MLA_V), BF16), pltpu.VMEM((S, 1), F32), pltpu.VMEM((TQ, MLA_PAD), F32),
                        pltpu.VMEM((S, MLA_PAD), F32), pltpu.VMEM((S, MLA_V), F32)],
        sem=("arbitrary", "arbitrary"),
        args=(q_raw, kv, kr, o, lse, do, qf, kf, qgn, qgr, kgn, kgr, cos, sin, _diag_bias()), comm=comm)


def _adamw(name, recvs, w, m, v, tr=None, comm=None):
    n, R, C = recvs[0].shape
    L = len(recvs)
    Lw, Rw, _ = w.shape
    assert Lw * Rw == L * R and w.shape[2] == C
    tr = R if tr is None else tr
    assert R % tr == 0 and Rw % tr == 0
    per = R // tr
    per_w = Rw // tr
    c1 = 1.0 - ADAM_B1 ** ADAM_STEP
    c2 = 1.0 - ADAM_B2 ** ADAM_STEP

    def body(*refs):
        r_refs = refs[:L]
        w_ref, m_ref, v_ref, g_ref, d_ref, nm_ref, nv_ref = refs[L:]
        layer = pl.program_id(0) // per

        def total(r_ref):
            t = r_ref[0].astype(F32)
            for k in range(1, n):
                t = t + r_ref[k].astype(F32)
            return t

        g = total(r_refs[0]) if L == 1 else lax.switch(layer, [functools.partial(total, r) for r in r_refs])
        mm = ADAM_B1 * m_ref[...] + (1.0 - ADAM_B1) * g
        vv = ADAM_B2 * v_ref[...] + (1.0 - ADAM_B2) * (g * g)
        g_ref[...] = g
        nm_ref[...] = mm
        nv_ref[...] = vv
        d_ref[...] = -ADAM_LR * ((mm / c1) / (jnp.sqrt(vv / c2) + ADAM_EPS) + ADAM_WD * w_ref[...])

    blk = _bs((None, tr, C), lambda i: (i // per_w, i % per_w, 0))
    r_specs = [_bs((n, tr, C), functools.partial(lambda l, i: (0, jnp.clip(i - l * per, 0, per - 1), 0), l))
               for l in range(L)]
    outs, got = _pcall(body, name=name, grid=(L * per,), in_specs=r_specs + [blk, blk, blk], out_specs=[blk] * 4,
                       out_shape=[jax.ShapeDtypeStruct(w.shape, F32)] * 4, scratch_shapes=[], sem=("arbitrary",),
                       args=(*recvs, w, m, v), comm=comm)
    return outs if comm is None else (outs, got)


def _sum8(name, a):
    n, R, C = a.shape

    def body(a_ref, o_ref):
        s = a_ref[0]
        for k in range(1, n):
            s = s + a_ref[k]
        o_ref[...] = s

    return pl.pallas_call(body, name=name, out_shape=jax.ShapeDtypeStruct((R, C), a.dtype))(a)


def _sds(shape, dt):
    return jax.ShapeDtypeStruct(shape, dt)


def _norm_proj(name, x, g, w, o_spec, out_shape, tm=1024, comm=None):
    T, K = x.shape
    J, _, n = w.shape

    def body(x_ref, g_ref, w_ref, o_ref, h_ref, hs):
        @pl.when(pl.program_id(1) == 0)
        def _():
            xf = x_ref[...]
            r = lax.rsqrt(jnp.mean(xf * xf, axis=-1, keepdims=True) + RMS_EPS)
            h = (xf * r * g_ref[...]).astype(BF16)
            hs[...] = h
            h_ref[...] = h

        o_ref[...] = _dot(hs[...], w_ref[...], NN).astype(o_ref.dtype)

    row = _bs((tm, K), lambda m, j: (m, 0))
    (out, h), got = _pcall(
        body, name=name, grid=(T // tm, J),
        in_specs=[row, _bs((1, K), lambda m, j: (0, 0)), _bs((None, K, n), lambda m, j: (j, 0, 0))],
        out_specs=[o_spec, row], out_shape=[out_shape, _sds((T, K), BF16)], scratch_shapes=[pltpu.VMEM((tm, K), BF16)],
        sem=("parallel", "arbitrary"), args=(x, g, w), comm=comm)
    return out, h, got


def _proj_shared_dx(name, d, w, tm=1024, comm=None):
    J, T, n = d.shape
    K = w.shape[1]
    return _mm(name, d, w, grid=(T // tm, J), a_spec=_bs((None, tm, n), lambda m, k: (k, m, 0)),
               b_spec=_bs((None, K, n), lambda m, k: (k, 0, 0)), o_spec=_bs((tm, K), lambda m, k: (m, 0)),
               out_shape=_sds((T, K), BF16), dims=NT, kax=1, acc_shape=(tm, K), comm=comm)


def _out_proj(name, a, w, res, tm=512):
    J, T, k = a.shape
    N = w.shape[2]
    return _mm(name, a, w, grid=(T // tm,), a_spec=_bs((J, tm, k), lambda m: (0, m, 0)),
               b_spec=_bs((J, k, N), lambda m: (0, 0, 0)), o_spec=_bs((tm, N), lambda m: (m, 0)),
               out_shape=_sds((T, N), F32), dims=NN, res=res, res_spec=_bs((tm, N), lambda m: (m, 0)), jb=J)


def _out_proj_dx(name, dx, w, tm=1024, comm=None):
    T, N = dx.shape
    J, k, _ = w.shape
    return _mm(name, dx, w, grid=(T // tm, J), a_spec=_bs((tm, N), lambda m, j: (m, 0)),
               b_spec=_bs((None, k, N), lambda m, j: (j, 0, 0)), o_spec=_bs((None, tm, k), lambda m, j: (j, m, 0)),
               out_shape=_sds((J, T, k), BF16), dims=NT, comm=comm)


def _out_proj_dw(name, a, dx, tt=1024, comm=None):
    J, T, k = a.shape
    N = dx.shape[1]
    tt = min(tt, T)
    return _mm(name, a, dx, grid=(J, T // tt), a_spec=_bs((None, tt, k), lambda j, t: (j, t, 0)),
               b_spec=_bs((tt, N), lambda j, t: (t, 0)), o_spec=_bs((None, k, N), lambda j, t: (j, 0, 0)),
               out_shape=_sds((J, k, N), BF16), dims=TN, kax=1, acc_shape=(k, N), comm=comm)


def _dense(name, a, b, dims, out_dtype, tm=512, res=None, comm=None):
    if dims == TN:
        T, K = a.shape
        N = b.shape[1]
        return _mm(name, a, b, grid=(T // tm,), a_spec=_bs((tm, K), lambda t: (t, 0)),
                   b_spec=_bs((tm, N), lambda t: (t, 0)), o_spec=_bs((K, N), lambda t: (0, 0)),
                   out_shape=_sds((K, N), out_dtype), dims=TN, kax=0, acc_shape=(K, N), comm=comm)
    M, K = a.shape
    N = b.shape[1] if dims == NN else b.shape[0]
    row = _bs((tm, N), lambda m: (m, 0))
    return _mm(name, a, b, grid=(M // tm,), a_spec=_bs((tm, K), lambda m: (m, 0)), b_spec=_bs(b.shape, lambda m: (0, 0)),
               o_spec=row, out_shape=_sds((M, N), out_dtype), dims=dims, res=res,
               res_spec=row if res is not None else None, comm=comm)


def _bf16(x):
    return x.astype(BF16)


def _ffn_fwd(i, x, norm_g, w_in, cw, cb, w_out, B, S, comm_in=None):
    T = x.shape[0]
    u, h, got = _norm_proj(f"ffn{i}_in", x, norm_g, w_in, _bs((None, 1024, FSH), lambda m, j: (j, m, 0)),
                           _sds((NDEV, T, FSH), BF16), comm=comm_in)
    u4 = u.reshape(2, 4, T, FSH)
    gt = _convffn_fwd(f"ffn{i}_gate", u4, cw, cb, B, S)
    y = _out_proj(f"ffn{i}_out", gt, w_out, x)
    return y, (x, h, u4, gt), got


def _ffn_bwd(i, dy, dyb, saved, norm_g, w_in, cw, cb, w_out, B, S, first_half_early, riders=(None, None)):
    x, h, u4, gt = saved
    dgt = _out_proj_dx(f"ffn{i}_out_dx", dyb, w_out, comm=riders[0])
    dw_out = _out_proj_dw(f"ffn{i}_out_dw", gt, dyb, comm=riders[1])
    dgt, got0 = dgt if riders[0] is not None else (dgt, None)
    dw_out, got1 = dw_out if riders[1] is not None else (dw_out, None)
    dw_out = dw_out.reshape(NDEV, FSH // 2, D_MODEL)
    (du4, dcw, dcb), (r_out,) = _convffn_bwd(f"ffn{i}_gate_bwd", u4, cw, cb, dgt, B, S, comm=_Exchange([dw_out]))
    du = du4.reshape(NDEV, du4.shape[2], FSH)
    dw_in = _out_proj_dw(f"ffn{i}_in_dw", du, h, tt=2048)
    r_in = None
    if first_half_early:
        dh, (r_in,) = _proj_shared_dx(f"ffn{i}_in_dx", du, w_in, comm=_Exchange([dw_in], rows=[(0, FSH // 2)]))
    else:
        dh = _proj_shared_dx(f"ffn{i}_in_dx", du, w_in)
    dx, dgn, dxb = _rms_bwd(f"ffn{i}_norm_bwd", x, norm_g, dh, dres=dy, also_bf16=True)
    return dx, dxb, dict(w_in=dw_in, norm=dgn, cw=dcw, cb=dcb), r_out, r_in, (got0, got1)


def kernel(x, ret_norm, ret_w_in, ret_gn, ret_w_out, mla_norm, mla_w_in, mla_q_norm, mla_w_qb, mla_kv_norm, mla_w_kvb, mla_q_head_norm, mla_k_head_norm, mla_w_out, ffn_norm, ffn_w_in, ffn_conv_w, ffn_conv_b, ffn_w_out, loss_target, m_ret_norm, m_ret_w_in, m_ret_gn, m_ret_w_out, m_mla_norm, m_mla_w_in, m_mla_q_norm, m_mla_w_qb, m_mla_kv_norm, m_mla_w_kvb, m_mla_q_head_norm, m_mla_k_head_norm, m_mla_w_out, m_ffn_norm, m_ffn_w_in, m_ffn_conv_w, m_ffn_conv_b, m_ffn_w_out, v_ret_norm, v_ret_w_in, v_ret_gn, v_ret_w_out, v_mla_norm, v_mla_w_in, v_mla_q_norm, v_mla_w_qb, v_mla_kv_norm, v_mla_w_kvb, v_mla_q_head_norm, v_mla_k_head_norm, v_mla_w_out, v_ffn_norm, v_ffn_w_in, v_ffn_conv_w, v_ffn_conv_b, v_ffn_w_out):
    B, S, D = x.shape
    T = B * S
    w = dict(ret_norm=ret_norm, ret_w_in=ret_w_in, ret_gn=ret_gn, ret_w_out=ret_w_out, mla_norm=mla_norm,
             mla_w_in=mla_w_in, mla_q_norm=mla_q_norm, mla_w_qb=mla_w_qb, mla_kv_norm=mla_kv_norm, mla_w_kvb=mla_w_kvb,
             mla_q_head_norm=mla_q_head_norm, mla_k_head_norm=mla_k_head_norm, mla_w_out=mla_w_out, ffn_norm=ffn_norm,
             ffn_w_in=ffn_w_in, ffn_conv_w=ffn_conv_w, ffn_conv_b=ffn_conv_b, ffn_w_out=ffn_w_out)
    mom = dict(ret_norm=m_ret_norm, ret_w_in=m_ret_w_in, ret_gn=m_ret_gn, ret_w_out=m_ret_w_out, mla_norm=m_mla_norm,
               mla_w_in=m_mla_w_in, mla_q_norm=m_mla_q_norm, mla_w_qb=m_mla_w_qb, mla_kv_norm=m_mla_kv_norm,
               mla_w_kvb=m_mla_w_kvb, mla_q_head_norm=m_mla_q_head_norm, mla_k_head_norm=m_mla_k_head_norm,
               mla_w_out=m_mla_w_out, ffn_norm=m_ffn_norm, ffn_w_in=m_ffn_w_in, ffn_conv_w=m_ffn_conv_w,
               ffn_conv_b=m_ffn_conv_b, ffn_w_out=m_ffn_w_out)
    var = dict(ret_norm=v_ret_norm, ret_w_in=v_ret_w_in, ret_gn=v_ret_gn, ret_w_out=v_ret_w_out, mla_norm=v_mla_norm,
               mla_w_in=v_mla_w_in, mla_q_norm=v_mla_q_norm, mla_w_qb=v_mla_w_qb, mla_kv_norm=v_mla_kv_norm,
               mla_w_kvb=v_mla_w_kvb, mla_q_head_norm=v_mla_q_head_norm, mla_k_head_norm=v_mla_k_head_norm,
               mla_w_out=v_mla_w_out, ffn_norm=v_ffn_norm, ffn_w_in=v_ffn_w_in, ffn_conv_w=v_ffn_conv_w,
               ffn_conv_b=v_ffn_conv_b, ffn_w_out=v_ffn_w_out)
    BIG = ["ret_w_in", "ret_w_out", "mla_w_in", "mla_w_qb", "mla_w_kvb", "mla_w_out", "ffn_w_in", "ffn_w_out"]
    REPL = ["ret_norm", "ffn_norm", "mla_q_head_norm", "mla_k_head_norm", "ffn_conv_b"]
    SHARDED_SMALL = ["ffn_conv_w", "ret_gn", "mla_norm", "mla_q_norm", "mla_kv_norm"]
    dev = _idx(_place())

    def blk16(k, i=0):
        return _bf16(w[k][i])

    small_vec = jnp.concatenate([w[k].reshape(-1) for k in SHARDED_SMALL])
    n_small = small_vec.shape[0]
    small_vec = jnp.pad(small_vec, (0, 4096 - n_small)).reshape(32, 128)
    Wret_in, sg = _comm_call("gather_ret_w_in", _Gather([blk16("ret_w_in"), small_vec], parts=2))
    sg = sg.reshape(NDEV, 4096)
    o0 = 0
    conv_w_full = sg[:, o0:o0 + 2112].reshape(NDEV, 2, 3, 352).transpose(1, 2, 0, 3).reshape(2, 3, FFN)
    o0 += 2112
    ret_gn_full = sg[:, o0:o0 + 256].reshape(NDEV, RET_H, 64).transpose(1, 0, 2).reshape(RET_H, 1, RET_DV)
    o0 += 256
    mla_norm_full = sg[:, o0:o0 + 128].reshape(1, D)
    o0 += 128
    q_norm_full = sg[:, o0:o0 + 48].reshape(1, MLA_QR)
    o0 += 48
    kv_norm_full = sg[:, o0:o0 + 32].reshape(1, MLA_KVR)

    cw = [conv_w_full[i].reshape(3, 4, FSH).transpose(1, 0, 2) for i in range(2)]
    cb = [ffn_conv_b[i].reshape(4, 1, FSH) for i in range(2)]
    fnorm = [ffn_norm[i].reshape(1, D) for i in range(2)]
    rtabs = _ret_tables(S)
    mtabs = _mla_tables(S)
    qh, kh = mla_q_head_norm.reshape(1, MLA_QK), mla_k_head_norm.reshape(1, MLA_QK)
    gains = (qh[:, :MLA_NOPE], _dup(qh[:, MLA_NOPE:]), kh[:, :MLA_NOPE], _dup(kh[:, MLA_NOPE:]))

    x0 = x.reshape(T, D)
    tgt = loss_target.reshape(T, D)
    proj, h0, (Wret_out, Wffn_out0) = _norm_proj(
        "ret_in", x0, ret_norm.reshape(1, D), Wret_in, _bs((1024, 768), lambda m, j: (m, j)), _sds((T, 6144), BF16),
        comm=_Gather([blk16("ret_w_out"), blk16("ffn_w_out", 0)]))
    Wret_out = Wret_out.reshape(RET_H * RET_DV, D)
    Wffn_out0 = Wffn_out0.reshape(4, FSH, D)
    (o_raw, rgt, states), (Wffn_in0,) = _ret_fwd(proj, rtabs, ret_gn_full, B, S, comm=_Gather([blk16("ffn_w_in", 0)]))
    x1 = _dense("ret_out", rgt, Wret_out, NN, F32, res=x0)
    MLA_W = ["mla_w_in", "mla_w_qb", "mla_w_kvb", "mla_w_out"]
    x2, ffn0_saved, got = _ffn_fwd(0, x1, fnorm[0], Wffn_in0, cw[0], cb[0], Wffn_out0, B, S,
                                   comm_in=_Gather([blk16(k) for k in MLA_W]))
    Wmla_in = got[0].reshape(D, MLA_QR + MLA_KVR + MLA_ROPE)
    Wq, Wkv, Wkr = Wmla_in[:, :MLA_QR], Wmla_in[:, MLA_QR:MLA_QR + MLA_KVR], Wmla_in[:, MLA_QR + MLA_KVR:]
    Wqb, Wkvb, Wmla_out = got[1:]

    h2 = _rms_fwd("mla_norm", x2, mla_norm_full)

    c_q, c_kv, k_rope = (_dense(n, h2, wm, NN, F32) for n, wm in
                         (("mla_in_q", Wq), ("mla_in_kv", Wkv), ("mla_in_kr", _dup(Wkr))))
    cqn = _rms_fwd("mla_q_norm", c_q, q_norm_full)
    ckvn = _rms_fwd("mla_kv_norm", c_kv, kv_norm_full)
    Wqb2 = jnp.concatenate([Wqb, Wqb[:, :, MLA_NOPE:]], axis=2).transpose(1, 0, 2).reshape(MLA_QR, MLA_H * MLA_PAD)
    Wkvb2 = Wkvb.transpose(1, 0, 2).reshape(MLA_KVR, MLA_H * (MLA_NOPE + MLA_V))
    Wmla_out2 = Wmla_out.reshape(D, D)
    q_raw = _dense("mla_qb", cqn, Wqb2, NN, F32, tm=1024)
    kvh = _dense("mla_kvb", ckvn, Wkvb2, NN, F32, tm=1024)
    (att, lse, qf, kf), (Wffn_in1, Wffn_out1) = _mla_fwd(
        q_raw, kvh, k_rope, gains, mtabs, B, S, comm=_Gather([blk16("ffn_w_in", 1), blk16("ffn_w_out", 1)]))
    Wffn_out1 = Wffn_out1.reshape(4, FSH, D)
    x3 = _dense("mla_out", att, Wmla_out2, NN, F32, res=x2)
    y, ffn1_saved, _ = _ffn_fwd(1, x3, fnorm[1], Wffn_in1, cw[1], cb[1], Wffn_out1, B, S)

    dy, colsq, dyb = _loss(y, tgt)
    loss_part = 0.5 * jnp.sum(colsq) / D

    dx3, dx3b, gf1, r_ffn1_out, _, _ = _ffn_bwd(1, dy, dyb, ffn1_saved, fnorm[1], Wffn_in1, cw[1], cb[1], Wffn_out1,
                                                B, S, first_half_early=False)
    datt = _dense("mla_out_dx", dx3b, Wmla_out2, NT, BF16)
    fh = FSH // 2
    (dq_raw, dkvh, dkr, dqgn, dqgr, dkgn, dkgr), (r_ffn1_in_a, r_ffn1_in_b) = _mla_bwd(
        q_raw, kvh, k_rope, att, lse, datt, qf, kf, gains, mtabs, B, S,
        comm=_Exchange([gf1["w_in"], gf1["w_in"]], rows=[(0, fh), (fh, fh)]))
    dcqn = _dense("mla_qb_dx", dq_raw, Wqb2, NT, F32, tm=1024)
    dckvn = _dense("mla_kvb_dx", dkvh, Wkvb2, NT, F32, tm=1024)
    dcq, dg_qn = _rms_bwd("mla_q_norm_bwd", c_q, q_norm_full, dcqn)
    dckv, dg_kvn = _rms_bwd("mla_kv_norm_bwd", c_kv, kv_norm_full, dckvn)
    dqgr, dkgr = _fold(dqgr), _fold(dkgr)
    dproj2 = _bf16(jnp.concatenate([dcq, dckv, _fold(dkr)], axis=-1))
    dh2 = _dense("mla_in_dx", dproj2, Wmla_in, NT, BF16)
    dx2, dg_mla_norm, dx2b = _rms_bwd("mla_norm_bwd", x2, mla_norm_full, dh2, dres=dx3, also_bf16=True)
    dWmla_out = _dense("mla_out_dw", att, dx3b, TN, BF16, tm=1024).reshape(NDEV, MLA_V, D)
    dWqb = _dense("mla_qb_dw", dq_raw, cqn, TN, BF16, tm=1024).reshape(MLA_H, MLA_PAD, MLA_QR)
    dWqb = jnp.concatenate([dWqb[:, :MLA_NOPE], dWqb[:, MLA_NOPE:MLA_QK] + dWqb[:, MLA_QK:]], axis=1)
    dWkvb = _dense("mla_kvb_dw", ckvn, dkvh, TN, BF16, tm=1024)
    dWkvb = dWkvb.reshape(MLA_KVR, MLA_H, MLA_NOPE + MLA_V).transpose(1, 0, 2)
    dWmla_in = _dense("mla_in_dw", h2, dproj2, TN, BF16).reshape(NDEV, 128, 704)

    dx1, dx1b, gf0, r_ffn0_out, r_ffn0_in_a, (r_mla_a, r_mla_b) = _ffn_bwd(
        0, dx2, dx2b, ffn0_saved, fnorm[0], Wffn_in0, cw[0], cb[0], Wffn_out0, B, S, first_half_early=True,
        riders=(_Exchange([dWmla_out, dWqb]), _Exchange([dWkvb, dWmla_in])))
    r_mla = [*r_mla_a, *r_mla_b]
    drgt = _dense("ret_out_dx", dx1b, Wret_out, NT, BF16)
    dWret_out = _dense("ret_out_dw", rgt, dx1b, TN, BF16, tm=1024).reshape(NDEV, 256, D)
    (dproj, dgn_ret), (r_ffn0_in_b, r_ret_out) = _ret_bwd(
        proj, o_raw, states, drgt, rtabs, ret_gn_full, B, S,
        comm=_Exchange([gf0["w_in"], dWret_out], rows=[(fh, fh), None]))

    tt, hk, qr = min(2048, T), D // 2, D // 4

    def ret_in_dw(name, half, comm):
        return _mm(name, h0, dproj, grid=(NDEV, T // tt), a_spec=_bs((tt, hk), lambda j, t: (t, half)),
                   b_spec=_bs((tt, 768), lambda j, t: (t, j)), o_spec=_bs((None, hk, 768), lambda j, t: (j, 0, 0)),
                   out_shape=_sds((NDEV, hk, 768), BF16), dims=TN, kax=1, acc_shape=(hk, 768), comm=comm)

    dW_top = ret_in_dw("ret_in_dw_top", 0, None)
    dW_bot, (r_q0,) = ret_in_dw("ret_in_dw_bot", 1, _Exchange([dW_top], rows=[(0, qr)]))
    dh0, (r_q1, r_q2) = _mm(
        "ret_in_dx", dproj, Wret_in, grid=(T // 1024, NDEV), a_spec=_bs((1024, 768), lambda m, k: (m, k)),
        b_spec=_bs((None, D, 768), lambda m, k: (k, 0, 0)), o_spec=_bs((1024, D), lambda m, k: (m, 0)),
        out_shape=_sds((T, D), BF16), dims=NT, kax=1, acc_shape=(1024, D),
        comm=_Exchange([dW_top, dW_bot], rows=[(qr, qr), (0, qr)]))
    (dx0, dg_ret_norm), (r_q3,) = _rms_bwd("ret_norm_bwd", x0, ret_norm.reshape(1, D), dh0, dres=dx1,
                                           comm=_Exchange([dW_bot], rows=[(qr, qr)]))
    grad_x = dx0.reshape(B, S, D)
    received = dict(ret_w_in=[r_q0, r_q1, r_q2, r_q3], ret_w_out=[r_ret_out], mla_w_out=[r_mla[0]], mla_w_qb=[r_mla[1]],
                    mla_w_kvb=[r_mla[2]], mla_w_in=[r_mla[3]],
                    ffn_w_in=[r_ffn0_in_a, r_ffn0_in_b, r_ffn1_in_a, r_ffn1_in_b], ffn_w_out=[r_ffn0_out, r_ffn1_out])

    dconv_w = jnp.stack([g_["cw"].transpose(1, 0, 2).reshape(3, FFN) for g_ in (gf0, gf1)])
    dconv_b = jnp.stack([g_["cb"].reshape(FFN) for g_ in (gf0, gf1)])
    small_parts = [dg_ret_norm, gf0["norm"], gf1["norm"], dg_mla_norm, dg_qn, dg_kvn, dqgn, dqgr, dkgn, dkgr, dgn_ret,
                   dconv_w, dconv_b, loss_part]
    small_g = jnp.concatenate([p.reshape(-1) for p in small_parts])
    n_grads = small_g.shape[0] - 1
    small_g = jnp.pad(small_g, (0, 240 * 128 - small_g.shape[0])).reshape(240, 128)
    small_all = _comm_call("gather_small_grads", _Gather([small_g]))[0]
    sred = _sum8("sum_small_grads", small_all).reshape(-1)
    loss = sred[n_grads]

    def take(n):
        nonlocal off
        out = sred[off:off + n]
        off += n
        return out

    off = 0
    g_small = dict(ret_norm=take(D).reshape(1, D), ffn_norm=take(2 * D).reshape(2, D), mla_norm=take(D),
                   mla_q_norm=take(MLA_QR), mla_kv_norm=take(MLA_KVR))
    g_small["mla_q_head_norm"] = take(MLA_QK).reshape(1, MLA_QK)
    g_small["mla_k_head_norm"] = take(MLA_QK).reshape(1, MLA_QK)
    g_small["ret_gn"] = take(RET_H * RET_DV).reshape(1, RET_H, RET_DV)
    g_small["ffn_conv_w"] = take(2 * 3 * FFN).reshape(2, 3, FFN)
    g_small["ffn_conv_b"] = take(2 * FFN).reshape(2, FFN)
    g_small["mla_norm"] = lax.dynamic_slice(g_small["mla_norm"], (dev * 128,), (128,)).reshape(1, 128)
    g_small["mla_q_norm"] = lax.dynamic_slice(g_small["mla_q_norm"], (dev * 48,), (48,)).reshape(1, 48)
    g_small["mla_kv_norm"] = lax.dynamic_slice(g_small["mla_kv_norm"], (dev * 32,), (32,)).reshape(1, 32)
    g_small["ret_gn"] = lax.dynamic_slice(g_small["ret_gn"], (0, 0, dev * 64), (1, RET_H, 64))
    g_small["ffn_conv_w"] = lax.dynamic_slice(g_small["ffn_conv_w"], (0, 0, dev * 352), (2, 3, 352))

    grads, delta, new_m, new_v = {}, {}, {}, {}
    for k in BIG:
        rcs = received[k]
        tr = max(t for t in range(16, 257, 16) if rcs[0].shape[1] % t == 0)
        flip = (lambda t: t.transpose(0, 2, 1)) if k in ("ffn_w_in", "mla_w_qb") else (lambda t: t)
        res = _adamw(f"adamw_{k}", rcs, flip(w[k]), flip(mom[k]), flip(var[k]), tr=tr)
        grads[k], delta[k], new_m[k], new_v[k] = (flip(t) for t in res)
    SMALL = REPL + SHARDED_SMALL

    def pack(d):
        vflat = jnp.concatenate([d[k].reshape(-1) for k in SMALL])
        return jnp.pad(vflat, (0, 96 * 128 - vflat.shape[0])).reshape(1, 96, 128)

    ps = _adamw("adamw_small", [pack(g_small)], pack(w), pack(mom), pack(var))
    off = 0
    for k in SMALL:
        n = w[k].size
        grads[k], delta[k], new_m[k], new_v[k] = (t.reshape(-1)[off:off + n].reshape(w[k].shape) for t in ps)
        off += n
    names = list(w)
    return (loss, grad_x, *[grads[k] for k in names], *[delta[k] for k in names], *[new_m[k] for k in names],
            *[new_v[k] for k in names])
```

```python
import functools

import jax
import jax.numpy as jnp
from jax import lax
from jax.experimental import pallas as pl
from jax.experimental.pallas import tpu as pltpu

F32, BF16 = jnp.float32, jnp.bfloat16

NDEV = 8
D_MODEL = 1024
CHUNK = 64
RMS_EPS = 1e-6
ROPE_THETA = 10000.0
RET_H, RET_DK, RET_DV = 4, 256, 512
RET_SC = 256
MLA_H, MLA_QR, MLA_KVR = 8, 384, 256
MLA_NOPE, MLA_ROPE, MLA_V = 128, 64, 128
MLA_QK = MLA_NOPE + MLA_ROPE
MASK_VALUE = -1e30
FFN = 2816
FSH = FFN * 2 // NDEV
ATT_TQ = 256
ADAM_LR, ADAM_B1, ADAM_B2, ADAM_EPS, ADAM_WD, ADAM_STEP = 0.001, 0.9, 0.999, 1e-08, 0.01, 10
MESH = pl.DeviceIdType.MESH
VMEM_LIMIT = 56 * 2 ** 20


def _cp(sem):
    return pltpu.CompilerParams(dimension_semantics=sem, vmem_limit_bytes=VMEM_LIMIT)


def _dot(a, b, dims):
    return lax.dot_general(a, b, (dims, ((), ())), preferred_element_type=F32)


NN = ((1,), (0,))
NT = ((1,), (1,))
TN = ((0,), (0,))


def _place():
    return lax.axis_index("x"), lax.axis_index("y"), lax.axis_index("c")


def _idx(d):
    return 4 * d[0] + 2 * d[1] + d[2]


ANY = pl.BlockSpec(memory_space=pl.ANY)


class _Gather:
    sem0 = 0

    def __init__(self, arrs, parts=1):
        self.srcs = list(arrs)
        self.parts = parts
        self.nsem = len(arrs) * parts
        self.out_shape = [jax.ShapeDtypeStruct((NDEV,) + a.shape, a.dtype) for a in arrs]

    def _copies(self, ins, outs, send, recv, loc):
        n = self.nsem
        s0 = self.sem0
        x, y, c = _place()
        me, sib = (x, y, c), (x, y, 1 - c)
        chips = [(1 - x, y), (x, 1 - y), (1 - x, 1 - y)]

        def piece(ref, v, *lead):
            a, p = divmod(v, self.parts)
            if self.parts > 1:
                rows = self.srcs[a].shape[0] // self.parts
                lead = (*lead, pl.ds(p * rows, rows))
            return ref[a].at[lead] if lead else ref[a]

        def cp(a, k, block, to, src=None):
            dst = piece(outs, a, _idx(block))
            return pltpu.make_async_remote_copy(src_ref=dst if src is None else src, dst_ref=dst,
                                                send_sem=send.at[s0 + a, k], recv_sem=recv.at[s0 + a, k], device_id=to,
                                                device_id_type=MESH)

        own = [piece(ins, a) for a in range(n)]
        mine = [pltpu.make_async_copy(own[a], piece(outs, a, _idx(me)), loc.at[s0 + a]) for a in range(n)]
        first = [cp(a, 0, me, sib, src=own[a]) for a in range(n)]
        first += [cp(a, 1 + j, me, (*chip, c), src=own[a]) for a in range(n) for j, chip in enumerate(chips)]
        landed = [cp(a, 1 + j, (*chip, c), me) for j, chip in enumerate(chips) for a in range(n)]
        passed = [cp(a, 4 + j, (*chip, c), sib) for j, chip in enumerate(chips) for a in range(n)]
        from_sib = [cp(a, 0, sib, me) for a in range(n)]
        from_sib += [cp(a, 4 + j, (*chip, 1 - c), me) for j, chip in enumerate(chips) for a in range(n)]
        return mine, first, landed, passed, from_sib

    def start(self, *refs):
        mine, first, _, _, _ = self._copies(*refs)
        for cp in mine + first:
            cp.start()

    def mid(self, *refs):
        _, _, landed, passed, _ = self._copies(*refs)
        for got, on in zip(landed, passed):
            got.wait_recv()
            on.start()

    def finish(self, *refs):
        mine, first, _, passed, from_sib = self._copies(*refs)
        for cp in from_sib:
            cp.wait_recv()
        for cp in first + passed:
            cp.wait_send()
        for cp in mine:
            cp.wait()


class _Exchange:
    sem0 = 0

    def __init__(self, arrs, rows=None):
        self.srcs = list(arrs)
        self.nsem = len(arrs)
        self.rows = rows if rows is not None else [None] * len(arrs)
        self.out_shape = [jax.ShapeDtypeStruct(a.shape if r is None else (a.shape[0], r[1]) + a.shape[2:], a.dtype)
                          for a, r in zip(arrs, self.rows)]

    def _copies(self, ins, outs, send, recv, loc):
        n = len(self.srcs)
        s0 = self.sem0
        x, y, c = _place()
        me = _idx((x, y, c))

        def src(a, q):
            r = self.rows[a]
            return ins[a].at[q] if r is None else ins[a].at[q, pl.ds(r[0], r[1])]

        mine = [pltpu.make_async_copy(src(a, me), outs[a].at[me], loc.at[s0 + a]) for a in range(n)]
        remote = []
        for k in range(1, NDEV):
            peer = (x ^ (k >> 2), y ^ ((k >> 1) & 1), c ^ (k & 1))
            remote += [pltpu.make_async_remote_copy(
                src_ref=src(a, _idx(peer)), dst_ref=outs[a].at[me], send_sem=send.at[s0 + a, k - 1],
                recv_sem=recv.at[s0 + a, k - 1], device_id=peer, device_id_type=MESH) for a in range(n)]
        return mine, remote

    def start(self, *refs):
        mine, remote = self._copies(*refs)
        for cp in mine + remote:
            cp.start()

    def mid(self, *refs):
        pass

    def finish(self, *refs):
        mine, remote = self._copies(*refs)
        for cp in remote + mine:
            cp.wait()


def _comm_scratch(n):
    return [pltpu.SemaphoreType.DMA((n, 7)), pltpu.SemaphoreType.DMA((n, 7)), pltpu.SemaphoreType.DMA((n,))]


def _comm_call(name, comm):
    n = len(comm.srcs)

    def body(*refs):
        parts = (refs[:n], refs[n:2 * n]) + tuple(refs[2 * n:])
        comm.start(*parts)
        comm.mid(*parts)
        comm.finish(*parts)

    return pl.pallas_call(body, name=name, in_specs=[ANY] * n, out_specs=[ANY] * n, out_shape=comm.out_shape,
                          scratch_shapes=_comm_scratch(comm.nsem))(*comm.srcs)


def _pcall(body, *, name, grid, in_specs, out_specs, out_shape, scratch_shapes, sem, args, comm=None):
    if comm is None:
        return pl.pallas_call(body, name=name, grid=grid, in_specs=in_specs, out_specs=out_specs, out_shape=out_shape,
                              scratch_shapes=scratch_shapes, compiler_params=_cp(sem))(*args), None
    ni, no, ns, nc = len(in_specs), len(out_shape), len(scratch_shapes), len(comm.srcs)
    total = 1
    for g in grid:
        total *= g
    middle = (4 * total) // 5

    def wrapped(*refs):
        ins, csrc = refs[:ni], refs[ni:ni + nc]
        outs, cdst = refs[ni + nc:ni + nc + no], refs[ni + nc + no:ni + 2 * nc + no]
        scr, sems = refs[ni + 2 * nc + no:ni + 2 * nc + no + ns], refs[ni + 2 * nc + no + ns:]
        step = pl.program_id(0)
        for k in range(1, len(grid)):
            step = step * grid[k] + pl.program_id(k)
        parts = (csrc, cdst) + tuple(sems)

        @pl.when(step == 0)
        def _():
            comm.start(*parts)

        body(*ins, *outs, *scr)

        @pl.when(step == middle)
        def _():
            comm.mid(*parts)

        @pl.when(step == total - 1)
        def _():
            comm.finish(*parts)

    res = pl.pallas_call(
        wrapped, name=name, grid=grid, in_specs=list(in_specs) + [ANY] * nc, out_specs=list(out_specs) + [ANY] * nc,
        out_shape=list(out_shape) + comm.out_shape, scratch_shapes=list(scratch_shapes) + _comm_scratch(comm.nsem),
        compiler_params=_cp(("arbitrary",) * len(grid)))(*args, *comm.srcs)
    return res[:no], res[no:]


def _mm(name, a, b, *, grid, a_spec, b_spec, o_spec, out_shape, dims, kax=None, res=None, res_spec=None,
        jb=0, acc_shape=None, comm=None):
    nk = grid[kax] if kax is not None else 1

    def body(*refs):
        if res is not None:
            a_ref, b_ref, r_ref, o_ref = refs[:4]
        else:
            a_ref, b_ref, o_ref = refs[:3]

        def product():
            if not jb:
                return _dot(a_ref[...], b_ref[...], dims)
            part = _dot(a_ref[0], b_ref[0], dims)
            for j in range(1, jb):
                part = part + _dot(a_ref[j], b_ref[j], dims)
            return part

        def fin(acc):
            if res is not None:
                acc = acc + r_ref[...]
            o_ref[...] = acc.astype(o_ref.dtype)

        if nk == 1:
            fin(product())
        else:
            acc_ref = refs[-1]
            k = pl.program_id(kax)

            @pl.when(k == 0)
            def _():
                acc_ref[...] = jnp.zeros_like(acc_ref)

            acc_ref[...] += product()

            @pl.when(k == nk - 1)
            def _():
                fin(acc_ref[...])

    sem = tuple("arbitrary" if i == kax else "parallel" for i in range(len(grid)))
    in_specs = [a_spec, b_spec] + ([res_spec] if res is not None else [])
    args = (a, b) + ((res,) if res is not None else ())
    scratch = [pltpu.VMEM(acc_shape, F32)] if nk > 1 else []
    (out,), got = _pcall(body, name=name, grid=grid, in_specs=in_specs, out_specs=[o_spec], out_shape=[out_shape],
                         scratch_shapes=scratch, sem=sem, args=args, comm=comm)
    return out if comm is None else (out, got)


def _bs(shape, fn):
    return pl.BlockSpec(shape, fn)


def _rms_fwd(name, x, g, tm=512):
    T, D = x.shape

    def body(x_ref, g_ref, o_ref):
        xf = x_ref[...]
        r = lax.rsqrt(jnp.mean(xf * xf, axis=-1, keepdims=True) + RMS_EPS)
        o_ref[...] = (xf * r * g_ref[...]).astype(o_ref.dtype)

    return pl.pallas_call(
        body, name=name, grid=(T // tm,),
        in_specs=[_bs((tm, D), lambda i: (i, 0)), _bs((1, D), lambda i: (0, 0))],
        out_specs=_bs((tm, D), lambda i: (i, 0)), out_shape=jax.ShapeDtypeStruct((T, D), BF16),
        compiler_params=_cp(("parallel",)))(x, g)


def _rms_bwd(name, x, g, dh, dres=None, tm=512, also_bf16=False, comm=None):
    T, D = x.shape

    def body(*refs):
        if also_bf16:
            refs, dxb_ref = refs[:-1], refs[-1]
        if dres is not None:
            x_ref, g_ref, dh_ref, dres_ref, dx_ref, dg_ref = refs
        else:
            x_ref, g_ref, dh_ref, dx_ref, dg_ref = refs
        i = pl.program_id(0)
        xf = x_ref[...]
        r = lax.rsqrt(jnp.mean(xf * xf, axis=-1, keepdims=True) + RMS_EPS)
        xh = xf * r
        d = dh_ref[...].astype(F32)
        dxh = d * g_ref[...]
        dx = r * (dxh - xh * jnp.mean(dxh * xh, axis=-1, keepdims=True))
        if dres is not None:
            dx = dx + dres_ref[...]
        dx_ref[...] = dx
        if also_bf16:
            dxb_ref[...] = dx.astype(BF16)
        part = jnp.sum(d * xh, axis=0, keepdims=True)

        @pl.when(i == 0)
        def _():
            dg_ref[...] = part

        @pl.when(i > 0)
        def _():
            dg_ref[...] += part

    row = _bs((tm, D), lambda i: (i, 0))
    vec = _bs((1, D), lambda i: (0, 0))
    in_specs = [row, vec, row] + ([row] if dres is not None else [])
    args = (x, g, dh) + ((dres,) if dres is not None else ())
    extra = [jax.ShapeDtypeStruct((T, D), BF16)] if also_bf16 else []
    outs, got = _pcall(
        body, name=name, grid=(T // tm,), in_specs=in_specs, out_specs=[row, vec] + [row] * len(extra),
        out_shape=[jax.ShapeDtypeStruct((T, D), F32), jax.ShapeDtypeStruct((1, D), F32)] + extra, scratch_shapes=[],
        sem=("arbitrary",), args=args, comm=comm)
    return outs if comm is None else (outs, got)


def _loss(y, tgt, tm=512):
    T, D = y.shape

    def body(y_ref, t_ref, dy_ref, s_ref, dyb_ref):
        i = pl.program_id(0)
        e = y_ref[...] - t_ref[...]
        dy = e * (1.0 / D)
        dy_ref[...] = dy
        dyb_ref[...] = dy.astype(BF16)
        part = jnp.sum(e * e, axis=0, keepdims=True)

        @pl.when(i == 0)
        def _():
            s_ref[...] = part

        @pl.when(i > 0)
        def _():
            s_ref[...] += part

    row = _bs((tm, D), lambda i: (i, 0))
    return pl.pallas_call(
        body, name="loss_head", grid=(T // tm,), in_specs=[row, row],
        out_specs=[row, _bs((1, D), lambda i: (0, 0)), row],
        out_shape=[jax.ShapeDtypeStruct((T, D), F32), jax.ShapeDtypeStruct((1, D), F32),
                   jax.ShapeDtypeStruct((T, D), BF16)],
        compiler_params=_cp(("arbitrary",)))(y, tgt)


def _shift_rows(t, k, row):
    return jnp.where(row >= k, pltpu.roll(t, k, 0), 0.0)


def _shift_rows_up(t, k, row, n):
    return jnp.where(row < n - k, pltpu.roll(t, n - k, 0), 0.0)


def _convffn_fwd(name, u, cw, cb, B, S):
    _, J, T, F = u.shape

    def body(u_ref, cw_ref, cb_ref, o_ref, gc_ref):
        a = u_ref[0].astype(F32)
        g = u_ref[1].astype(F32)
        row = lax.broadcasted_iota(jnp.int32, (S, F), 0)
        w0, w1, w2 = cw_ref[0:1, :], cw_ref[1:2, :], cw_ref[2:3, :]
        gc = _shift_rows(g, 2, row) * w0 + _shift_rows(g, 1, row) * w1 + g * w2 + cb_ref[...]
        gc_ref[...] = gc.astype(gc_ref.dtype)
        o_ref[...] = (gc * jax.nn.sigmoid(gc) * a).astype(o_ref.dtype)

    blk = _bs((None, S, F), lambda j, b: (j, b, 0))
    return pl.pallas_call(
        body, name=name, grid=(J, B),
        in_specs=[_bs((2, None, S, F), lambda j, b: (0, j, b, 0)), _bs((None, 3, F), lambda j, b: (j, 0, 0)),
                  _bs((None, 1, F), lambda j, b: (j, 0, 0))],
        out_specs=[blk, blk], out_shape=[jax.ShapeDtypeStruct((J, T, F), BF16)] * 2,
        compiler_params=_cp(("parallel", "parallel")))(u, cw, cb)


def _convffn_bwd(name, u, gcb, cw, dgt, B, S, comm=None):
    _, J, T, F = u.shape

    def body(u_ref, gc_ref, cw_ref, d_ref, du_ref, dcw_ref, dcb_ref):
        b = pl.program_id(1)
        a = u_ref[0].astype(F32)
        g = u_ref[1].astype(F32)
        gc = gc_ref[...].astype(F32)
        d = d_ref[...].astype(F32)
        row = lax.broadcasted_iota(jnp.int32, (S, F), 0)
        w0, w1, w2 = cw_ref[0:1, :], cw_ref[1:2, :], cw_ref[2:3, :]
        sg = jax.nn.sigmoid(gc)
        du_ref[0] = (d * gc * sg).astype(du_ref.dtype)
        dgc = d * a * (sg * (1.0 + gc * (1.0 - sg)))
        up1, up2 = _shift_rows_up(dgc, 1, row, S), _shift_rows_up(dgc, 2, row, S)
        du_ref[1] = (dgc * w2 + up1 * w1 + up2 * w0).astype(du_ref.dtype)
        parts = [jnp.sum(up2 * g, axis=0, keepdims=True), jnp.sum(up1 * g, axis=0, keepdims=True),
                 jnp.sum(dgc * g, axis=0, keepdims=True)]
        pb = jnp.sum(dgc, axis=0, keepdims=True)

        @pl.when(b == 0)
        def _():
            for k in range(3):
                dcw_ref[k:k + 1, :] = parts[k]
            dcb_ref[...] = pb

        @pl.when(b > 0)
        def _():
            for k in range(3):
                dcw_ref[k:k + 1, :] += parts[k]
            dcb_ref[...] += pb

    uspec = _bs((2, None, S, F), lambda j, b: (0, j, b, 0))
    blk = _bs((None, S, F), lambda j, b: (j, b, 0))
    return _pcall(
        body, name=name, grid=(J, B),
        in_specs=[uspec, blk, _bs((None, 3, F), lambda j, b: (j, 0, 0)), blk],
        out_specs=[uspec, _bs((None, 3, F), lambda j, b: (j, 0, 0)), _bs((None, 1, F), lambda j, b: (j, 0, 0))],
        out_shape=[jax.ShapeDtypeStruct(u.shape, BF16), jax.ShapeDtypeStruct((J, 3, F), F32),
                   jax.ShapeDtypeStruct((J, 1, F), F32)],
        scratch_shapes=[], sem=("parallel", "arbitrary"), args=(u, gcb, cw, dgt), comm=comm)


def _ret_tables(S):
    half = RET_DK // 2
    inv = ROPE_THETA ** (-jnp.arange(half, dtype=F32) / half)
    ang = jnp.arange(S).astype(F32)[:, None] * inv[None, :]
    lg = jnp.log1p(-jnp.exp2(-5.0 - jnp.arange(RET_H, dtype=F32)))
    i = jnp.arange(RET_SC, dtype=F32)
    same_or_earlier = (jnp.floor(i[None, :] / CHUNK) <= jnp.floor(i[:, None] / CHUNK)).astype(F32)
    dm = jnp.exp(lg[:, None, None] * jnp.abs(i[:, None] - i[None, :])) * same_or_earlier[None]
    qd = jnp.exp(lg[:, None] * (i + 1.0))[:, :, None]
    kd = jnp.exp(lg[:, None] * (RET_SC - 1.0 - i))[:, :, None]
    cd = jnp.exp(lg * RET_SC)[:, None, None]
    return jnp.cos(ang), jnp.sin(ang), dm, qd, kd, cd


def _rope_halves(t, cs, sn):
    h = t.shape[-1] // 2
    t1, t2 = t[:, :h], t[:, h:]
    return jnp.concatenate([t1 * cs - t2 * sn, t2 * cs + t1 * sn], axis=-1)


def _unrope_halves(d, cs, sn):
    h = d.shape[-1] // 2
    d1, d2 = d[:, :h], d[:, h:]
    return jnp.concatenate([d1 * cs + d2 * sn, d2 * cs - d1 * sn], axis=-1)


def _ret_specs(nC, order):
    SC = RET_SC

    def sp(shape, fn):
        return _bs(shape, lambda *g: fn(*order(*g)))

    q = sp((SC, RET_DK), lambda b, h, c: (b * nC + c, h))
    k = sp((SC, RET_DK), lambda b, h, c: (b * nC + c, RET_H + h))
    v = sp((SC, RET_DV), lambda b, h, c: (b * nC + c, RET_H + h))
    g = sp((SC, RET_DV), lambda b, h, c: (b * nC + c, 2 * RET_H + h))
    cs = sp((SC, RET_DK // 2), lambda b, h, c: (c, 0))
    dm = sp((None, SC, SC), lambda b, h, c: (h, 0, 0))
    dv = sp((None, SC, 1), lambda b, h, c: (h, 0, 0))
    cd = sp((None, 1, 1), lambda b, h, c: (h, 0, 0))
    gn = sp((None, 1, RET_DV), lambda b, h, c: (h, 0, 0))
    wide = sp((SC, RET_DV), lambda b, h, c: (b * nC + c, h))
    narrow = sp((SC, RET_DK), lambda b, h, c: (b * nC + c, h))
    st = sp((None, None, None, RET_DK, RET_DV), lambda b, h, c: (b, h, c, 0, 0))
    return dict(q=q, k=k, v=v, g=g, cs=cs, dm=dm, dv=dv, cd=cd, gn=gn, wide=wide, narrow=narrow, st=st)


def _ret_fwd(proj, tabs, gn, B, S, comm=None):
    T = B * S
    nC = S // RET_SC
    cos, sin, dm, qd, kd, cd = tabs
    s = _ret_specs(nC, lambda b, h, c: (b, h, c))

    def body(q_ref, k_ref, v_ref, g_ref, cos_ref, sin_ref, dm_ref, qd_ref, kd_ref, cd_ref, gn_ref,
             o_ref, gt_ref, st_ref, state):
        c = pl.program_id(2)

        @pl.when(c == 0)
        def _():
            state[...] = jnp.zeros_like(state)

        cs, sn = cos_ref[...], sin_ref[...]
        qf = _rope_halves(q_ref[...].astype(F32), cs, sn)
        kf = _rope_halves(k_ref[...].astype(F32), cs, sn) * (RET_DK ** -0.5)
        v = v_ref[...]
        p = _dot(qf.astype(BF16), kf.astype(BF16), NT) * dm_ref[...]
        st = state[...]
        stb = st.astype(BF16)
        st_ref[...] = stb
        o = _dot(p.astype(BF16), v, NN) + _dot((qf * qd_ref[...]).astype(BF16), stb, NN)
        state[...] = st * cd_ref[...] + _dot((kf * kd_ref[...]).astype(BF16), v, TN)
        o_ref[...] = o
        r = lax.rsqrt(jnp.mean(o * o, axis=-1, keepdims=True) + RMS_EPS)
        gf = g_ref[...].astype(F32)
        gt_ref[...] = ((o * r * gn_ref[...]) * (gf * jax.nn.sigmoid(gf))).astype(BF16)

    return _pcall(
        body, name="ret_fwd", grid=(B, RET_H, nC),
        in_specs=[s["q"], s["k"], s["v"], s["g"], s["cs"], s["cs"], s["dm"], s["dv"], s["dv"], s["cd"], s["gn"]],
        out_specs=[s["wide"], s["wide"], s["st"]],
        out_shape=[jax.ShapeDtypeStruct((T, RET_H * RET_DV), F32), jax.ShapeDtypeStruct((T, RET_H * RET_DV), BF16),
                   jax.ShapeDtypeStruct((B, RET_H, nC, RET_DK, RET_DV), BF16)],
        scratch_shapes=[pltpu.VMEM((RET_DK, RET_DV), F32)], sem=("parallel", "parallel", "arbitrary"),
        args=(proj, proj, proj, proj, cos, sin, dm, qd, kd, cd, gn), comm=comm)


def _ret_bwd(proj, o_raw, states, dgt, tabs, gn, B, S, comm=None):
    T = B * S
    nC = S // RET_SC
    cos, sin, dm, qd, kd, cd = tabs
    s = _ret_specs(nC, lambda b, c, h: (b, h, nC - 1 - c))

    def body(q_ref, k_ref, v_ref, g_ref, o_ref, st_ref, d_ref, cos_ref, sin_ref, dm_ref, qd_ref, kd_ref, cd_ref,
             gn_ref, dproj_ref, dgn_ref, dstates):
        b, c, h = pl.program_id(0), pl.program_id(1), pl.program_id(2)
        dstate = dstates.at[h]

        @pl.when(c == 0)
        def _():
            dstate[...] = jnp.zeros_like(dstate)

        @pl.when((b == 0) & (c == 0))
        def _():
            dgn_ref[h] = jnp.zeros((1, RET_DV), F32)

        cs, sn = cos_ref[...], sin_ref[...]
        qf = _rope_halves(q_ref[...].astype(F32), cs, sn)
        kf = _rope_halves(k_ref[...].astype(F32), cs, sn) * (RET_DK ** -0.5)
        v = v_ref[...]
        gnv = gn_ref[h]
        o = o_ref[...]
        r = lax.rsqrt(jnp.mean(o * o, axis=-1, keepdims=True) + RMS_EPS)
        oh = o * r
        gf = g_ref[...].astype(F32)
        sg = jax.nn.sigmoid(gf)
        d = d_ref[...].astype(F32)
        dg = (d * (oh * gnv) * (sg * (1.0 + gf * (1.0 - sg)))).astype(BF16)
        don = d * (gf * sg)
        dgn_ref[h] += jnp.sum(don * oh, axis=0, keepdims=True)
        doh = don * gnv
        dO = (r * (doh - oh * jnp.mean(doh * oh, axis=-1, keepdims=True))).astype(BF16)
        dmv = dm_ref[h]
        qb, kb = qf.astype(BF16), kf.astype(BF16)
        p = (_dot(qb, kb, NT) * dmv).astype(BF16)
        dp = (_dot(dO, v, NT) * dmv).astype(BF16)
        st = st_ref[...]
        dsn = dstate[...]
        dsb = dsn.astype(BF16)
        qdv, kdv = qd_ref[h], kd_ref[h]
        dq = _dot(dp, kb, NN) + _dot(dO, st, NT) * qdv
        dk = _dot(dp, qb, TN) + _dot(v, dsb, NT) * kdv
        dv = _dot(p, dO, TN) + _dot((kf * kdv).astype(BF16), dsb, NN)
        dstate[...] = dsn * cd_ref[h] + _dot((qf * qdv).astype(BF16), dO, TN)
        dq = _unrope_halves(dq, cs, sn).astype(BF16)
        dk = (_unrope_halves(dk, cs, sn) * (RET_DK ** -0.5)).astype(BF16)
        dv = dv.astype(BF16)
        nq, nv = RET_H * RET_DK, RET_H * RET_DV
        for hh in range(RET_H):
            @pl.when(h == hh)
            def _():
                dproj_ref[:, hh * RET_DK:(hh + 1) * RET_DK] = dq
                dproj_ref[:, nq + hh * RET_DK:nq + (hh + 1) * RET_DK] = dk
                dproj_ref[:, 2 * nq + hh * RET_DV:2 * nq + (hh + 1) * RET_DV] = dv
                dproj_ref[:, 2 * nq + nv + hh * RET_DV:2 * nq + nv + (hh + 1) * RET_DV] = dg

    width = 2 * RET_H * (RET_DK + RET_DV)

    def all_heads(*shape):
        return _bs((RET_H,) + shape, lambda b, c, h: (0,) * (1 + len(shape)))

    return _pcall(
        body, name="ret_bwd", grid=(B, nC, RET_H),
        in_specs=[s["q"], s["k"], s["v"], s["g"], s["wide"], s["st"], s["wide"], s["cs"], s["cs"],
                  all_heads(RET_SC, RET_SC), all_heads(RET_SC, 1), all_heads(RET_SC, 1), all_heads(1, 1),
                  all_heads(1, RET_DV)],
        out_specs=[_bs((RET_SC, width), lambda b, c, h: (b * nC + nC - 1 - c, 0)),
                   _bs((RET_H, 1, RET_DV), lambda b, c, h: (0, 0, 0))],
        out_shape=[jax.ShapeDtypeStruct((T, width), BF16), jax.ShapeDtypeStruct((RET_H, 1, RET_DV), F32)],
        scratch_shapes=[pltpu.VMEM((RET_H, RET_DK, RET_DV), F32)], sem=("arbitrary", "arbitrary", "arbitrary"),
        args=(proj, proj, proj, proj, o_raw, states, dgt, cos, sin, dm, qd, kd, cd, gn), comm=comm)


MLA_PAD = 256
MLA_R2 = 2 * MLA_ROPE


def _dup(t):
    return jnp.concatenate([t, t], axis=-1)


def _fold(t):
    return t[..., :MLA_ROPE] + t[..., MLA_ROPE:]


def _mla_tables(S):
    half = MLA_ROPE // 2
    inv = ROPE_THETA ** (-jnp.arange(half, dtype=F32) / half)
    ang = jnp.arange(S).astype(F32)[:, None] * inv[None, :]
    cos, sin, zero = jnp.cos(ang), jnp.sin(ang), jnp.zeros((S, MLA_ROPE), F32)
    return jnp.concatenate([cos, cos, zero], axis=-1), jnp.concatenate([-sin, sin, zero], axis=-1)


def _head_norm_rope(n, r2, gn, gr2, cos, sin, scale):
    ssq = jnp.sum(n * n, axis=-1, keepdims=True) + 0.5 * jnp.sum(r2 * r2, axis=-1, keepdims=True)
    rstd = lax.rsqrt(ssq * (1.0 / MLA_QK) + RMS_EPS)
    yn = n * rstd * gn
    yr = r2 * rstd * gr2
    z = yr * cos + pltpu.roll(yr, MLA_ROPE // 2, 1) * sin
    if scale != 1.0:
        yn, z = yn * scale, z * scale
    return yn, z


def _head_norm_rope_bwd(dn, dz, n, r2, gn, gr2, cos, sin, scale):
    ssq = jnp.sum(n * n, axis=-1, keepdims=True) + 0.5 * jnp.sum(r2 * r2, axis=-1, keepdims=True)
    rstd = lax.rsqrt(ssq * (1.0 / MLA_QK) + RMS_EPS)
    hn, hr = n * rstd, r2 * rstd
    if scale != 1.0:
        dn, dz = dn * scale, dz * scale
    dyr = dz * cos + pltpu.roll(dz * sin, MLA_R2 - MLA_ROPE // 2, 1)
    dgn = jnp.sum(dn * hn, axis=0, keepdims=True)
    dgr = jnp.sum(dyr * hr, axis=0, keepdims=True)
    dhn, dhr = dn * gn, dyr * gr2
    mt = (jnp.sum(dhn * hn, axis=-1, keepdims=True) + jnp.sum(dhr * hr, axis=-1, keepdims=True)) * (1.0 / MLA_QK)
    return rstd * (dhn - hn * mt), rstd * (dhr - 0.5 * hr * mt), dgn, dgr


def _diag_bias():
    i = jnp.arange(ATT_TQ)
    return jnp.where((i[None, :] // CHUNK) <= (i[:, None] // CHUNK), 0.0, MASK_VALUE).astype(F32)


def _store_pair(dst, rows, n, r2):
    dst[rows, :MLA_NOPE] = n.astype(BF16)
    dst[rows, MLA_NOPE:] = r2.astype(BF16)


def _mla_fwd(q_raw, kv, kr, gains, tabs, B, S, comm=None):
    T = B * S
    TQ = ATT_TQ
    nQ = S // TQ
    qgn, qgr, kgn, kgr = gains
    cos, sin = tabs
    scale = MLA_QK ** -0.5

    def body(q_ref, kv_ref, kr_ref, qgn_ref, qgr_ref, kgn_ref, kgr_ref, c_ref, s_ref, bias_ref,
             o_ref, lse_ref, qf_s, kf_s, v_s):
        def prep(t, _):
            rows = pl.ds(pl.multiple_of(t * TQ, TQ), TQ)
            cs, sn = c_ref[rows, :], s_ref[rows, :]
            qn, qr = _head_norm_rope(q_ref[rows, :MLA_NOPE], q_ref[rows, MLA_NOPE:], qgn_ref[...], qgr_ref[...],
                                     cs, sn, scale)
            _store_pair(qf_s, rows, qn, qr)
            kn, krr = _head_norm_rope(kv_ref[rows, :MLA_NOPE], kr_ref[rows, :], kgn_ref[...], kgr_ref[...], cs, sn, 1.0)
            _store_pair(kf_s, rows, kn, krr)
            v_s[rows, :] = kv_ref[rows, MLA_NOPE:].astype(BF16)
            return 0

        lax.fori_loop(0, nQ, prep, 0, unroll=2)
        for i in range(nQ):
            rows = slice(i * TQ, (i + 1) * TQ)
            q = qf_s[rows, :]
            sd = _dot(q, kf_s[rows, :], NT) + bias_ref[...]
            m = jnp.max(sd, axis=-1, keepdims=True)
            if i:
                sl = _dot(q, kf_s[:i * TQ, :], NT)
                m = jnp.maximum(m, jnp.max(sl, axis=-1, keepdims=True))
            pd = jnp.exp(sd - m)
            l = jnp.sum(pd, axis=-1, keepdims=True)
            acc = _dot(pd.astype(BF16), v_s[rows, :], NN)
            if i:
                pl_ = jnp.exp(sl - m)
                l = l + jnp.sum(pl_, axis=-1, keepdims=True)
                acc = acc + _dot(pl_.astype(BF16), v_s[:i * TQ, :], NN)
            o_ref[rows, :] = (acc / l).astype(BF16)
            lse_ref[rows, :] = m + jnp.log(l)

    def vec(n):
        return _bs((1, n), lambda b, h: (0, 0))

    def cols(n):
        return _bs((S, n), lambda b, h: (b, h))

    tab = _bs((S, MLA_R2), lambda b, h: (0, 0))
    return _pcall(
        body, name="mla_fwd", grid=(B, MLA_H),
        in_specs=[cols(MLA_PAD), cols(MLA_NOPE + MLA_V), _bs((S, MLA_R2), lambda b, h: (b, 0)),
                  vec(MLA_NOPE), vec(MLA_R2), vec(MLA_NOPE), vec(MLA_R2), tab, tab,
                  _bs((TQ, TQ), lambda b, h: (0, 0))],
        out_specs=[cols(MLA_V), _bs((None, S, 1), lambda b, h: (h, b, 0)), cols(MLA_PAD), cols(MLA_PAD)],
        out_shape=[jax.ShapeDtypeStruct((T, MLA_H * MLA_V), BF16), jax.ShapeDtypeStruct((MLA_H, T, 1), F32),
                   jax.ShapeDtypeStruct((T, MLA_H * MLA_PAD), BF16), jax.ShapeDtypeStruct((T, MLA_H * MLA_PAD), BF16)],
        scratch_shapes=[pltpu.VMEM((S, MLA_V), BF16)],
        sem=("parallel", "parallel"), args=(q_raw, kv, kr, qgn, qgr, kgn, kgr, cos, sin, _diag_bias()), comm=comm)


def _mla_bwd(q_raw, kv, kr, o, lse, do, qf, kf, gains, tabs, B, S, comm=None):
    T = B * S
    TQ = ATT_TQ
    nQ = S // TQ
    qgn, qgr, kgn, kgr = gains
    cos, sin = tabs
    scale = MLA_QK ** -0.5

    def body(q_ref, kv_ref, kr_ref, o_ref, lse_ref, do_ref, qf_s, kf_s, qgn_ref, qgr_ref, kgn_ref, kgr_ref, c_ref, s_ref,
             bias_ref, dq_ref, dkv_ref, dkr_ref, dqgn_ref, dqgr_ref, dkgn_ref, dkgr_ref,
             v_s, dl_s, dq_s, dk_s, dv_s):
        b, h = pl.program_id(0), pl.program_id(1)

        def blk(t):
            return pl.ds(pl.multiple_of(t * TQ, TQ), TQ)

        def prep(t, _):
            rows = blk(t)
            v_s[rows, :] = kv_ref[rows, MLA_NOPE:].astype(BF16)
            dl_s[rows, :] = jnp.sum(do_ref[rows, :].astype(F32) * o_ref[rows, :].astype(F32), axis=-1, keepdims=True)
            dk_s[rows, :] = jnp.zeros((TQ, MLA_PAD), F32)
            dv_s[rows, :] = jnp.zeros((TQ, MLA_V), F32)
            return 0

        lax.fori_loop(0, nQ, prep, 0, unroll=2)

        gqn, gqr = jnp.zeros((1, MLA_NOPE), F32), jnp.zeros((1, MLA_R2), F32)
        for i in range(nQ):
            rows = slice(i * TQ, (i + 1) * TQ)
            q, doi, lse_i, dl_i = qf_s[rows, :], do_ref[rows, :], lse_ref[rows, :], dl_s[rows, :]

            def part(cols, bias):
                k, v = kf_s[cols, :], v_s[cols, :]
                s = _dot(q, k, NT)
                if bias is not None:
                    s = s + bias
                p = jnp.exp(s - lse_i)
                ds = (p * (_dot(doi, v, NT) - dl_i)).astype(BF16)
                dk_s[cols, :] += _dot(ds, q, TN)
                dv_s[cols, :] += _dot(p.astype(BF16), doi, TN)
                return _dot(ds, k, NN)

            dq = part(rows, bias_ref[...])
            if i:
                dq = dq + part(slice(0, i * TQ), None)
            dq_s[...] = dq
            dqn, dqr, a0, a1 = _head_norm_rope_bwd(dq_s[:, :MLA_NOPE], dq_s[:, MLA_NOPE:], q_ref[rows, :MLA_NOPE],
                                                   q_ref[rows, MLA_NOPE:], qgn_ref[...], qgr_ref[...],
                                                   c_ref[rows, :], s_ref[rows, :], scale)
            _store_pair(dq_ref, rows, dqn, dqr)
            gqn, gqr = gqn + a0, gqr + a1

        def post(t, carry):
            rows = blk(t)
            dkn, dkr, a2, a3 = _head_norm_rope_bwd(dk_s[rows, :MLA_NOPE], dk_s[rows, MLA_NOPE:],
                                                   kv_ref[rows, :MLA_NOPE], kr_ref[rows, :], kgn_ref[...], kgr_ref[...],
                                                   c_ref[rows, :], s_ref[rows, :], 1.0)
            dkv_ref[rows, :MLA_NOPE] = dkn.astype(BF16)
            dkv_ref[rows, MLA_NOPE:] = dv_s[rows, :].astype(BF16)

            @pl.when(h == 0)
            def _():
                dkr_ref[rows, :] = dkr

            @pl.when(h > 0)
            def _():
                dkr_ref[rows, :] += dkr

            return carry[0] + a2, carry[1] + a3

        gkn, gkr = lax.fori_loop(0, nQ, post, (jnp.zeros((1, MLA_NOPE), F32), jnp.zeros((1, MLA_R2), F32)), unroll=2)
        first = (b == 0) & (h == 0)

        @pl.when(first)
        def _():
            dqgn_ref[...] = gqn
            dqgr_ref[...] = gqr
            dkgn_ref[...] = gkn
            dkgr_ref[...] = gkr

        @pl.when(jnp.logical_not(first))
        def _():
            dqgn_ref[...] += gqn
            dqgr_ref[...] += gqr
            dkgn_ref[...] += gkn
            dkgr_ref[...] += gkr

    def vec(n):
        return _bs((1, n), lambda b, h: (0, 0))

    def cols(n):
        return _bs((S, n), lambda b, h: (b, h))

    tab = _bs((S, MLA_R2), lambda b, h: (0, 0))
    return _pcall(
        body, name="mla_bwd", grid=(B, MLA_H),
        in_specs=[cols(MLA_PAD), cols(MLA_NOPE + MLA_V), _bs((S, MLA_R2), lambda b, h: (b, 0)), cols(MLA_V),
                  _bs((None, S, 1), lambda b, h: (h, b, 0)), cols(MLA_V), cols(MLA_PAD), cols(MLA_PAD),
                  vec(MLA_NOPE), vec(MLA_R2), vec(MLA_NOPE), vec(MLA_R2), tab, tab,
                  _bs((TQ, TQ), lambda b, h: (0, 0))],
        out_specs=[cols(MLA_PAD), cols(MLA_NOPE + MLA_V), _bs((S, MLA_R2), lambda b, h: (b, 0)),
                   vec(MLA_NOPE), vec(MLA_R2), vec(MLA_NOPE), vec(MLA_R2)],
        out_shape=[jax.ShapeDtypeStruct((T, MLA_H * MLA_PAD), BF16),
                   jax.ShapeDtypeStruct((T, MLA_H * (MLA_NOPE + MLA_V)), BF16),
                   jax.ShapeDtypeStruct((T, MLA_R2), F32), jax.ShapeDtypeStruct((1, MLA_NOPE), F32),
                   jax.ShapeDtypeStruct((1, MLA_R2), F32), jax.ShapeDtypeStruct((1, MLA_NOPE), F32),
                   jax.ShapeDtypeStruct((1, MLA_R2), F32)],
        scratch_shapes=[pltpu.VMEM((S, MLA_V), BF16), pltpu.VMEM((S, 1), F32), pltpu.VMEM((TQ, MLA_PAD), F32),
                        pltpu.VMEM((S, MLA_PAD), F32), pltpu.VMEM((S, MLA_V), F32)],
        sem=("arbitrary", "arbitrary"),
        args=(q_raw, kv, kr, o, lse, do, qf, kf, qgn, qgr, kgn, kgr, cos, sin, _diag_bias()), comm=comm)


def _adamw(name, recvs, w, m, v, tr=None, comm=None):
    n, R, C = recvs[0].shape
    L = len(recvs)
    Lw, Rw, _ = w.shape
    assert Lw * Rw == L * R and w.shape[2] == C
    tr = R if tr is None else tr
    assert R % tr == 0 and Rw % tr == 0
    per = R // tr
    per_w = Rw // tr
    c1 = 1.0 - ADAM_B1 ** ADAM_STEP
    c2 = 1.0 - ADAM_B2 ** ADAM_STEP

    def body(*refs):
        r_refs = refs[:L]
        w_ref, m_ref, v_ref, g_ref, d_ref, nm_ref, nv_ref = refs[L:]
        layer = pl.program_id(0) // per

        def total(r_ref):
            t = r_ref[0].astype(F32)
            for k in range(1, n):
                t = t + r_ref[k].astype(F32)
            return t

        g = total(r_refs[0]) if L == 1 else lax.switch(layer, [functools.partial(total, r) for r in r_refs])
        mm = ADAM_B1 * m_ref[...] + (1.0 - ADAM_B1) * g
        vv = ADAM_B2 * v_ref[...] + (1.0 - ADAM_B2) * (g * g)
        g_ref[...] = g
        nm_ref[...] = mm
        nv_ref[...] = vv
        d_ref[...] = -ADAM_LR * ((mm / c1) / (jnp.sqrt(vv / c2) + ADAM_EPS) + ADAM_WD * w_ref[...])

    blk = _bs((None, tr, C), lambda i: (i // per_w, i % per_w, 0))
    r_specs = [_bs((n, tr, C), functools.partial(lambda l, i: (0, jnp.clip(i - l * per, 0, per - 1), 0), l))
               for l in range(L)]
    outs, got = _pcall(body, name=name, grid=(L * per,), in_specs=r_specs + [blk, blk, blk], out_specs=[blk] * 4,
                       out_shape=[jax.ShapeDtypeStruct(w.shape, F32)] * 4, scratch_shapes=[], sem=("arbitrary",),
                       args=(*recvs, w, m, v), comm=comm)
    return outs if comm is None else (outs, got)


def _sum8(name, a):
    n, R, C = a.shape

    def body(a_ref, o_ref):
        s = a_ref[0]
        for k in range(1, n):
            s = s + a_ref[k]
        o_ref[...] = s

    return pl.pallas_call(body, name=name, out_shape=jax.ShapeDtypeStruct((R, C), a.dtype))(a)


def _sds(shape, dt):
    return jax.ShapeDtypeStruct(shape, dt)


def _norm_proj(name, x, g, w, o_spec, out_shape, tm=1024, comm=None):
    T, K = x.shape
    J, _, n = w.shape

    def body(x_ref, g_ref, w_ref, o_ref, h_ref, hs):
        @pl.when(pl.program_id(1) == 0)
        def _():
            xf = x_ref[...]
            r = lax.rsqrt(jnp.mean(xf * xf, axis=-1, keepdims=True) + RMS_EPS)
            h = (xf * r * g_ref[...]).astype(BF16)
            hs[...] = h
            h_ref[...] = h

        o_ref[...] = _dot(hs[...], w_ref[...], NN).astype(o_ref.dtype)

    row = _bs((tm, K), lambda m, j: (m, 0))
    (out, h), got = _pcall(
        body, name=name, grid=(T // tm, J),
        in_specs=[row, _bs((1, K), lambda m, j: (0, 0)), _bs((None, K, n), lambda m, j: (j, 0, 0))],
        out_specs=[o_spec, row], out_shape=[out_shape, _sds((T, K), BF16)], scratch_shapes=[pltpu.VMEM((tm, K), BF16)],
        sem=("parallel", "arbitrary"), args=(x, g, w), comm=comm)
    return out, h, got


def _proj_shared_dx(name, d, w, tm=1024, comm=None):
    J, T, n = d.shape
    K = w.shape[1]
    return _mm(name, d, w, grid=(T // tm, J), a_spec=_bs((None, tm, n), lambda m, k: (k, m, 0)),
               b_spec=_bs((None, K, n), lambda m, k: (k, 0, 0)), o_spec=_bs((tm, K), lambda m, k: (m, 0)),
               out_shape=_sds((T, K), BF16), dims=NT, kax=1, acc_shape=(tm, K), comm=comm)


def _out_proj(name, a, w, res, tm=512):
    J, T, k = a.shape
    N = w.shape[2]
    return _mm(name, a, w, grid=(T // tm,), a_spec=_bs((J, tm, k), lambda m: (0, m, 0)),
               b_spec=_bs((J, k, N), lambda m: (0, 0, 0)), o_spec=_bs((tm, N), lambda m: (m, 0)),
               out_shape=_sds((T, N), F32), dims=NN, res=res, res_spec=_bs((tm, N), lambda m: (m, 0)), jb=J)


def _out_proj_dx(name, dx, w, tm=1024, comm=None):
    T, N = dx.shape
    J, k, _ = w.shape
    return _mm(name, dx, w, grid=(T // tm, J), a_spec=_bs((tm, N), lambda m, j: (m, 0)),
               b_spec=_bs((None, k, N), lambda m, j: (j, 0, 0)), o_spec=_bs((None, tm, k), lambda m, j: (j, m, 0)),
               out_shape=_sds((J, T, k), BF16), dims=NT, comm=comm)


def _out_proj_dw(name, a, dx, tt=1024, comm=None):
    J, T, k = a.shape
    N = dx.shape[1]
    tt = min(tt, T)
    return _mm(name, a, dx, grid=(J, T // tt), a_spec=_bs((None, tt, k), lambda j, t: (j, t, 0)),
               b_spec=_bs((tt, N), lambda j, t: (t, 0)), o_spec=_bs((None, k, N), lambda j, t: (j, 0, 0)),
               out_shape=_sds((J, k, N), BF16), dims=TN, kax=1, acc_shape=(k, N), comm=comm)


def _dense(name, a, b, dims, out_dtype, tm=512, res=None, comm=None):
    if dims == TN:
        T, K = a.shape
        N = b.shape[1]
        return _mm(name, a, b, grid=(T // tm,), a_spec=_bs((tm, K), lambda t: (t, 0)),
                   b_spec=_bs((tm, N), lambda t: (t, 0)), o_spec=_bs((K, N), lambda t: (0, 0)),
                   out_shape=_sds((K, N), out_dtype), dims=TN, kax=0, acc_shape=(K, N), comm=comm)
    M, K = a.shape
    N = b.shape[1] if dims == NN else b.shape[0]
    row = _bs((tm, N), lambda m: (m, 0))
    return _mm(name, a, b, grid=(M // tm,), a_spec=_bs((tm, K), lambda m: (m, 0)), b_spec=_bs(b.shape, lambda m: (0, 0)),
               o_spec=row, out_shape=_sds((M, N), out_dtype), dims=dims, res=res,
               res_spec=row if res is not None else None, comm=comm)


def _bf16(x):
    return x.astype(BF16)


def _ffn_fwd(i, x, norm_g, w_in, cw, cb, w_out, B, S, comm_in=None):
    T = x.shape[0]
    u, h, got = _norm_proj(f"ffn{i}_in", x, norm_g, w_in, _bs((None, 1024, FSH), lambda m, j: (j, m, 0)),
                           _sds((NDEV, T, FSH), BF16), comm=comm_in)
    u4 = u.reshape(2, 4, T, FSH)
    gt, gcb = _convffn_fwd(f"ffn{i}_gate", u4, cw, cb, B, S)
    y = _out_proj(f"ffn{i}_out", gt, w_out, x)
    return y, (x, h, u4, gt, gcb), got


def _ffn_bwd(i, dy, dyb, saved, norm_g, w_in, cw, cb, w_out, B, S, first_half_early, riders=(None, None)):
    x, h, u4, gt, gcb = saved
    dgt = _out_proj_dx(f"ffn{i}_out_dx", dyb, w_out, comm=riders[0])
    dw_out = _out_proj_dw(f"ffn{i}_out_dw", gt, dyb, comm=riders[1])
    dgt, got0 = dgt if riders[0] is not None else (dgt, None)
    dw_out, got1 = dw_out if riders[1] is not None else (dw_out, None)
    dw_out = dw_out.reshape(NDEV, FSH // 2, D_MODEL)
    (du4, dcw, dcb), (r_out,) = _convffn_bwd(f"ffn{i}_gate_bwd", u4, gcb, cw, dgt, B, S, comm=_Exchange([dw_out]))
    du = du4.reshape(NDEV, du4.shape[2], FSH)
    dw_in = _out_proj_dw(f"ffn{i}_in_dw", du, h, tt=2048)
    r_in = None
    if first_half_early:
        dh, (r_in,) = _proj_shared_dx(f"ffn{i}_in_dx", du, w_in, comm=_Exchange([dw_in], rows=[(0, FSH // 2)]))
    else:
        dh = _proj_shared_dx(f"ffn{i}_in_dx", du, w_in)
    dx, dgn, dxb = _rms_bwd(f"ffn{i}_norm_bwd", x, norm_g, dh, dres=dy, also_bf16=True)
    return dx, dxb, dict(w_in=dw_in, norm=dgn, cw=dcw, cb=dcb), r_out, r_in, (got0, got1)


def kernel(x, ret_norm, ret_w_in, ret_gn, ret_w_out, mla_norm, mla_w_in, mla_q_norm, mla_w_qb, mla_kv_norm, mla_w_kvb, mla_q_head_norm, mla_k_head_norm, mla_w_out, ffn_norm, ffn_w_in, ffn_conv_w, ffn_conv_b, ffn_w_out, loss_target, m_ret_norm, m_ret_w_in, m_ret_gn, m_ret_w_out, m_mla_norm, m_mla_w_in, m_mla_q_norm, m_mla_w_qb, m_mla_kv_norm, m_mla_w_kvb, m_mla_q_head_norm, m_mla_k_head_norm, m_mla_w_out, m_ffn_norm, m_ffn_w_in, m_ffn_conv_w, m_ffn_conv_b, m_ffn_w_out, v_ret_norm, v_ret_w_in, v_ret_gn, v_ret_w_out, v_mla_norm, v_mla_w_in, v_mla_q_norm, v_mla_w_qb, v_mla_kv_norm, v_mla_w_kvb, v_mla_q_head_norm, v_mla_k_head_norm, v_mla_w_out, v_ffn_norm, v_ffn_w_in, v_ffn_conv_w, v_ffn_conv_b, v_ffn_w_out):
    B, S, D = x.shape
    T = B * S
    w = dict(ret_norm=ret_norm, ret_w_in=ret_w_in, ret_gn=ret_gn, ret_w_out=ret_w_out, mla_norm=mla_norm,
             mla_w_in=mla_w_in, mla_q_norm=mla_q_norm, mla_w_qb=mla_w_qb, mla_kv_norm=mla_kv_norm, mla_w_kvb=mla_w_kvb,
             mla_q_head_norm=mla_q_head_norm, mla_k_head_norm=mla_k_head_norm, mla_w_out=mla_w_out, ffn_norm=ffn_norm,
             ffn_w_in=ffn_w_in, ffn_conv_w=ffn_conv_w, ffn_conv_b=ffn_conv_b, ffn_w_out=ffn_w_out)
    mom = dict(ret_norm=m_ret_norm, ret_w_in=m_ret_w_in, ret_gn=m_ret_gn, ret_w_out=m_ret_w_out, mla_norm=m_mla_norm,
               mla_w_in=m_mla_w_in, mla_q_norm=m_mla_q_norm, mla_w_qb=m_mla_w_qb, mla_kv_norm=m_mla_kv_norm,
               mla_w_kvb=m_mla_w_kvb, mla_q_head_norm=m_mla_q_head_norm, mla_k_head_norm=m_mla_k_head_norm,
               mla_w_out=m_mla_w_out, ffn_norm=m_ffn_norm, ffn_w_in=m_ffn_w_in, ffn_conv_w=m_ffn_conv_w,
               ffn_conv_b=m_ffn_conv_b, ffn_w_out=m_ffn_w_out)
    var = dict(ret_norm=v_ret_norm, ret_w_in=v_ret_w_in, ret_gn=v_ret_gn, ret_w_out=v_ret_w_out, mla_norm=v_mla_norm,
               mla_w_in=v_mla_w_in, mla_q_norm=v_mla_q_norm, mla_w_qb=v_mla_w_qb, mla_kv_norm=v_mla_kv_norm,
               mla_w_kvb=v_mla_w_kvb, mla_q_head_norm=v_mla_q_head_norm, mla_k_head_norm=v_mla_k_head_norm,
               mla_w_out=v_mla_w_out, ffn_norm=v_ffn_norm, ffn_w_in=v_ffn_w_in, ffn_conv_w=v_ffn_conv_w,
               ffn_conv_b=v_ffn_conv_b, ffn_w_out=v_ffn_w_out)
    BIG = ["ret_w_in", "ret_w_out", "mla_w_in", "mla_w_qb", "mla_w_kvb", "mla_w_out", "ffn_w_in", "ffn_w_out"]
    REPL = ["ret_norm", "ffn_norm", "mla_q_head_norm", "mla_k_head_norm", "ffn_conv_b"]
    SHARDED_SMALL = ["ffn_conv_w", "ret_gn", "mla_norm", "mla_q_norm", "mla_kv_norm"]
    dev = _idx(_place())

    def blk16(k, i=0):
        return _bf16(w[k][i])

    small_vec = jnp.concatenate([w[k].reshape(-1) for k in SHARDED_SMALL])
    n_small = small_vec.shape[0]
    small_vec = jnp.pad(small_vec, (0, 4096 - n_small)).reshape(32, 128)
    Wret_in, sg = _comm_call("gather_ret_w_in", _Gather([blk16("ret_w_in"), small_vec], parts=2))
    sg = sg.reshape(NDEV, 4096)
    o0 = 0
    conv_w_full = sg[:, o0:o0 + 2112].reshape(NDEV, 2, 3, 352).transpose(1, 2, 0, 3).reshape(2, 3, FFN)
    o0 += 2112
    ret_gn_full = sg[:, o0:o0 + 256].reshape(NDEV, RET_H, 64).transpose(1, 0, 2).reshape(RET_H, 1, RET_DV)
    o0 += 256
    mla_norm_full = sg[:, o0:o0 + 128].reshape(1, D)
    o0 += 128
    q_norm_full = sg[:, o0:o0 + 48].reshape(1, MLA_QR)
    o0 += 48
    kv_norm_full = sg[:, o0:o0 + 32].reshape(1, MLA_KVR)

    cw = [conv_w_full[i].reshape(3, 4, FSH).transpose(1, 0, 2) for i in range(2)]
    cb = [ffn_conv_b[i].reshape(4, 1, FSH) for i in range(2)]
    fnorm = [ffn_norm[i].reshape(1, D) for i in range(2)]
    rtabs = _ret_tables(S)
    mtabs = _mla_tables(S)
    qh, kh = mla_q_head_norm.reshape(1, MLA_QK), mla_k_head_norm.reshape(1, MLA_QK)
    gains = (qh[:, :MLA_NOPE], _dup(qh[:, MLA_NOPE:]), kh[:, :MLA_NOPE], _dup(kh[:, MLA_NOPE:]))

    x0 = x.reshape(T, D)
    tgt = loss_target.reshape(T, D)
    proj, h0, (Wret_out, Wffn_out0) = _norm_proj(
        "ret_in", x0, ret_norm.reshape(1, D), Wret_in, _bs((1024, 768), lambda m, j: (m, j)), _sds((T, 6144), BF16),
        comm=_Gather([blk16("ret_w_out"), blk16("ffn_w_out", 0)]))
    Wret_out = Wret_out.reshape(RET_H * RET_DV, D)
    Wffn_out0 = Wffn_out0.reshape(4, FSH, D)
    (o_raw, rgt, states), (Wffn_in0,) = _ret_fwd(proj, rtabs, ret_gn_full, B, S, comm=_Gather([blk16("ffn_w_in", 0)]))
    x1 = _dense("ret_out", rgt, Wret_out, NN, F32, res=x0)
    MLA_W = ["mla_w_in", "mla_w_qb", "mla_w_kvb", "mla_w_out"]
    x2, ffn0_saved, got = _ffn_fwd(0, x1, fnorm[0], Wffn_in0, cw[0], cb[0], Wffn_out0, B, S,
                                   comm_in=_Gather([blk16(k) for k in MLA_W]))
    Wmla_in = got[0].reshape(D, MLA_QR + MLA_KVR + MLA_ROPE)
    Wq, Wkv, Wkr = Wmla_in[:, :MLA_QR], Wmla_in[:, MLA_QR:MLA_QR + MLA_KVR], Wmla_in[:, MLA_QR + MLA_KVR:]
    Wqb, Wkvb, Wmla_out = got[1:]

    h2 = _rms_fwd("mla_norm", x2, mla_norm_full)

    c_q, c_kv, k_rope = (_dense(n, h2, wm, NN, F32) for n, wm in
                         (("mla_in_q", Wq), ("mla_in_kv", Wkv), ("mla_in_kr", _dup(Wkr))))
    cqn = _rms_fwd("mla_q_norm", c_q, q_norm_full)
    ckvn = _rms_fwd("mla_kv_norm", c_kv, kv_norm_full)
    Wqb2 = jnp.concatenate([Wqb, Wqb[:, :, MLA_NOPE:]], axis=2).transpose(1, 0, 2).reshape(MLA_QR, MLA_H * MLA_PAD)
    Wkvb2 = Wkvb.transpose(1, 0, 2).reshape(MLA_KVR, MLA_H * (MLA_NOPE + MLA_V))
    Wmla_out2 = Wmla_out.reshape(D, D)
    q_raw = _dense("mla_qb", cqn, Wqb2, NN, F32, tm=1024)
    kvh = _dense("mla_kvb", ckvn, Wkvb2, NN, F32, tm=1024)
    (att, lse, qf, kf), (Wffn_in1, Wffn_out1) = _mla_fwd(
        q_raw, kvh, k_rope, gains, mtabs, B, S, comm=_Gather([blk16("ffn_w_in", 1), blk16("ffn_w_out", 1)]))
    Wffn_out1 = Wffn_out1.reshape(4, FSH, D)
    x3 = _dense("mla_out", att, Wmla_out2, NN, F32, res=x2)
    y, ffn1_saved, _ = _ffn_fwd(1, x3, fnorm[1], Wffn_in1, cw[1], cb[1], Wffn_out1, B, S)

    dy, colsq, dyb = _loss(y, tgt)
    loss_part = 0.5 * jnp.sum(colsq) / D

    dx3, dx3b, gf1, r_ffn1_out, _, _ = _ffn_bwd(1, dy, dyb, ffn1_saved, fnorm[1], Wffn_in1, cw[1], cb[1], Wffn_out1,
                                                B, S, first_half_early=False)
    datt = _dense("mla_out_dx", dx3b, Wmla_out2, NT, BF16)
    fh = FSH // 2
    (dq_raw, dkvh, dkr, dqgn, dqgr, dkgn, dkgr), (r_ffn1_in_a, r_ffn1_in_b) = _mla_bwd(
        q_raw, kvh, k_rope, att, lse, datt, qf, kf, gains, mtabs, B, S,
        comm=_Exchange([gf1["w_in"], gf1["w_in"]], rows=[(0, fh), (fh, fh)]))
    dcqn = _dense("mla_qb_dx", dq_raw, Wqb2, NT, F32, tm=1024)
    dckvn = _dense("mla_kvb_dx", dkvh, Wkvb2, NT, F32, tm=1024)
    dcq, dg_qn = _rms_bwd("mla_q_norm_bwd", c_q, q_norm_full, dcqn)
    dckv, dg_kvn = _rms_bwd("mla_kv_norm_bwd", c_kv, kv_norm_full, dckvn)
    dqgr, dkgr = _fold(dqgr), _fold(dkgr)
    dproj2 = _bf16(jnp.concatenate([dcq, dckv, _fold(dkr)], axis=-1))
    dh2 = _dense("mla_in_dx", dproj2, Wmla_in, NT, BF16)
    dx2, dg_mla_norm, dx2b = _rms_bwd("mla_norm_bwd", x2, mla_norm_full, dh2, dres=dx3, also_bf16=True)
    dWmla_out = _dense("mla_out_dw", att, dx3b, TN, BF16, tm=1024).reshape(NDEV, MLA_V, D)
    dWqb = _dense("mla_qb_dw", dq_raw, cqn, TN, BF16, tm=1024).reshape(MLA_H, MLA_PAD, MLA_QR)
    dWqb = jnp.concatenate([dWqb[:, :MLA_NOPE], dWqb[:, MLA_NOPE:MLA_QK] + dWqb[:, MLA_QK:]], axis=1)
    dWkvb = _dense("mla_kvb_dw", ckvn, dkvh, TN, BF16, tm=1024)
    dWkvb = dWkvb.reshape(MLA_KVR, MLA_H, MLA_NOPE + MLA_V).transpose(1, 0, 2)
    dWmla_in = _dense("mla_in_dw", h2, dproj2, TN, BF16).reshape(NDEV, 128, 704)

    dx1, dx1b, gf0, r_ffn0_out, r_ffn0_in_a, (r_mla_a, r_mla_b) = _ffn_bwd(
        0, dx2, dx2b, ffn0_saved, fnorm[0], Wffn_in0, cw[0], cb[0], Wffn_out0, B, S, first_half_early=True,
        riders=(_Exchange([dWmla_out, dWqb]), _Exchange([dWkvb, dWmla_in])))
    r_mla = [*r_mla_a, *r_mla_b]
    drgt = _dense("ret_out_dx", dx1b, Wret_out, NT, BF16)
    dWret_out = _dense("ret_out_dw", rgt, dx1b, TN, BF16, tm=1024).reshape(NDEV, 256, D)
    (dproj, dgn_ret), (r_ffn0_in_b, r_ret_out) = _ret_bwd(
        proj, o_raw, states, drgt, rtabs, ret_gn_full, B, S,
        comm=_Exchange([gf0["w_in"], dWret_out], rows=[(fh, fh), None]))

    tt, hk, qr = min(2048, T), D // 2, D // 4

    def ret_in_dw(name, half, comm):
        return _mm(name, h0, dproj, grid=(NDEV, T // tt), a_spec=_bs((tt, hk), lambda j, t: (t, half)),
                   b_spec=_bs((tt, 768), lambda j, t: (t, j)), o_spec=_bs((None, hk, 768), lambda j, t: (j, 0, 0)),
                   out_shape=_sds((NDEV, hk, 768), BF16), dims=TN, kax=1, acc_shape=(hk, 768), comm=comm)

    dW_top = ret_in_dw("ret_in_dw_top", 0, None)
    dW_bot, (r_q0,) = ret_in_dw("ret_in_dw_bot", 1, _Exchange([dW_top], rows=[(0, qr)]))
    dh0, (r_q1, r_q2) = _mm(
        "ret_in_dx", dproj, Wret_in, grid=(T // 1024, NDEV), a_spec=_bs((1024, 768), lambda m, k: (m, k)),
        b_spec=_bs((None, D, 768), lambda m, k: (k, 0, 0)), o_spec=_bs((1024, D), lambda m, k: (m, 0)),
        out_shape=_sds((T, D), BF16), dims=NT, kax=1, acc_shape=(1024, D),
        comm=_Exchange([dW_top, dW_bot], rows=[(qr, qr), (0, qr)]))
    (dx0, dg_ret_norm), (r_q3,) = _rms_bwd("ret_norm_bwd", x0, ret_norm.reshape(1, D), dh0, dres=dx1,
                                           comm=_Exchange([dW_bot], rows=[(qr, qr)]))
    grad_x = dx0.reshape(B, S, D)
    received = dict(ret_w_in=[r_q0, r_q1, r_q2, r_q3], ret_w_out=[r_ret_out], mla_w_out=[r_mla[0]], mla_w_qb=[r_mla[1]],
                    mla_w_kvb=[r_mla[2]], mla_w_in=[r_mla[3]],
                    ffn_w_in=[r_ffn0_in_a, r_ffn0_in_b, r_ffn1_in_a, r_ffn1_in_b], ffn_w_out=[r_ffn0_out, r_ffn1_out])

    dconv_w = jnp.stack([g_["cw"].transpose(1, 0, 2).reshape(3, FFN) for g_ in (gf0, gf1)])
    dconv_b = jnp.stack([g_["cb"].reshape(FFN) for g_ in (gf0, gf1)])
    small_parts = [dg_ret_norm, gf0["norm"], gf1["norm"], dg_mla_norm, dg_qn, dg_kvn, dqgn, dqgr, dkgn, dkgr, dgn_ret,
                   dconv_w, dconv_b, loss_part]
    small_g = jnp.concatenate([p.reshape(-1) for p in small_parts])
    n_grads = small_g.shape[0] - 1
    small_g = jnp.pad(small_g, (0, 240 * 128 - small_g.shape[0])).reshape(240, 128)
    small_all = _comm_call("gather_small_grads", _Gather([small_g]))[0]
    sred = _sum8("sum_small_grads", small_all).reshape(-1)
    loss = sred[n_grads]

    def take(n):
        nonlocal off
        out = sred[off:off + n]
        off += n
        return out

    off = 0
    g_small = dict(ret_norm=take(D).reshape(1, D), ffn_norm=take(2 * D).reshape(2, D), mla_norm=take(D),
                   mla_q_norm=take(MLA_QR), mla_kv_norm=take(MLA_KVR))
    g_small["mla_q_head_norm"] = take(MLA_QK).reshape(1, MLA_QK)
    g_small["mla_k_head_norm"] = take(MLA_QK).reshape(1, MLA_QK)
    g_small["ret_gn"] = take(RET_H * RET_DV).reshape(1, RET_H, RET_DV)
    g_small["ffn_conv_w"] = take(2 * 3 * FFN).reshape(2, 3, FFN)
    g_small["ffn_conv_b"] = take(2 * FFN).reshape(2, FFN)
    g_small["mla_norm"] = lax.dynamic_slice(g_small["mla_norm"], (dev * 128,), (128,)).reshape(1, 128)
    g_small["mla_q_norm"] = lax.dynamic_slice(g_small["mla_q_norm"], (dev * 48,), (48,)).reshape(1, 48)
    g_small["mla_kv_norm"] = lax.dynamic_slice(g_small["mla_kv_norm"], (dev * 32,), (32,)).reshape(1, 32)
    g_small["ret_gn"] = lax.dynamic_slice(g_small["ret_gn"], (0, 0, dev * 64), (1, RET_H, 64))
    g_small["ffn_conv_w"] = lax.dynamic_slice(g_small["ffn_conv_w"], (0, 0, dev * 352), (2, 3, 352))

    grads, delta, new_m, new_v = {}, {}, {}, {}
    for k in BIG:
        rcs = received[k]
        tr = max(t for t in range(16, 257, 16) if rcs[0].shape[1] % t == 0)
        flip = (lambda t: t.transpose(0, 2, 1)) if k in ("ffn_w_in", "mla_w_qb") else (lambda t: t)
        res = _adamw(f"adamw_{k}", rcs, flip(w[k]), flip(mom[k]), flip(var[k]), tr=tr)
        grads[k], delta[k], new_m[k], new_v[k] = (flip(t) for t in res)
    SMALL = REPL + SHARDED_SMALL

    def pack(d):
        vflat = jnp.concatenate([d[k].reshape(-1) for k in SMALL])
        return jnp.pad(vflat, (0, 96 * 128 - vflat.shape[0])).reshape(1, 96, 128)

    ps = _adamw("adamw_small", [pack(g_small)], pack(w), pack(mom), pack(var))
    off = 0
    for k in SMALL:
        n = w[k].size
        grads[k], delta[k], new_m[k], new_v[k] = (t.reshape(-1)[off:off + n].reshape(w[k].shape) for t in ps)
        off += n
    names = list(w)
    return (loss, grad_x, *[grads[k] for k in names], *[delta[k] for k in names], *[new_m[k] for k in names],
            *[new_v[k] for k in names])
```

```python
import functools

import jax
import jax.numpy as jnp
from jax import lax
from jax.experimental import pallas as pl
from jax.experimental.pallas import tpu as pltpu

F32, BF16 = jnp.float32, jnp.bfloat16

NDEV = 8
D_MODEL = 1024
CHUNK = 64
RMS_EPS = 1e-6
ROPE_THETA = 10000.0
RET_H, RET_DK, RET_DV = 4, 256, 512
RET_SC = 256
MLA_H, MLA_QR, MLA_KVR = 8, 384, 256
MLA_NOPE, MLA_ROPE, MLA_V = 128, 64, 128
MLA_QK = MLA_NOPE + MLA_ROPE
MASK_VALUE = -1e30
FFN = 2816
FSH = FFN * 2 // NDEV
ATT_TQ = 256
ADAM_LR, ADAM_B1, ADAM_B2, ADAM_EPS, ADAM_WD, ADAM_STEP = 0.001, 0.9, 0.999, 1e-08, 0.01, 10
MESH = pl.DeviceIdType.MESH
VMEM_LIMIT = 56 * 2 ** 20


def _cp(sem):
    return pltpu.CompilerParams(dimension_semantics=sem, vmem_limit_bytes=VMEM_LIMIT)


def _dot(a, b, dims):
    return lax.dot_general(a, b, (dims, ((), ())), preferred_element_type=F32)


NN = ((1,), (0,))
NT = ((1,), (1,))
TN = ((0,), (0,))


def _place():
    return lax.axis_index("x"), lax.axis_index("y"), lax.axis_index("c")


def _idx(d):
    return 4 * d[0] + 2 * d[1] + d[2]


ANY = pl.BlockSpec(memory_space=pl.ANY)


class _Gather:
    def __init__(self, arrs, parts=1):
        self.srcs = list(arrs)
        self.parts = parts
        self.nsem = len(arrs) * parts
        self.out_shape = [jax.ShapeDtypeStruct((NDEV,) + a.shape, a.dtype) for a in arrs]

    def _copies(self, ins, outs, send, recv, loc):
        n = self.nsem
        x, y, c = _place()
        me, sib = (x, y, c), (x, y, 1 - c)
        chips = [(1 - x, y), (x, 1 - y), (1 - x, 1 - y)]

        def piece(ref, v, *lead):
            a, p = divmod(v, self.parts)
            if self.parts > 1:
                rows = self.srcs[a].shape[0] // self.parts
                lead = (*lead, pl.ds(p * rows, rows))
            return ref[a].at[lead] if lead else ref[a]

        def cp(a, k, block, to, src=None):
            dst = piece(outs, a, _idx(block))
            return pltpu.make_async_remote_copy(src_ref=dst if src is None else src, dst_ref=dst,
                                                send_sem=send.at[a, k], recv_sem=recv.at[a, k], device_id=to,
                                                device_id_type=MESH)

        own = [piece(ins, a) for a in range(n)]
        mine = [pltpu.make_async_copy(own[a], piece(outs, a, _idx(me)), loc.at[a]) for a in range(n)]
        first = [cp(a, 0, me, sib, src=own[a]) for a in range(n)]
        first += [cp(a, 1 + j, me, (*chip, c), src=own[a]) for a in range(n) for j, chip in enumerate(chips)]
        landed = [cp(a, 1 + j, (*chip, c), me) for j, chip in enumerate(chips) for a in range(n)]
        passed = [cp(a, 4 + j, (*chip, c), sib) for j, chip in enumerate(chips) for a in range(n)]
        from_sib = [cp(a, 0, sib, me) for a in range(n)]
        from_sib += [cp(a, 4 + j, (*chip, 1 - c), me) for j, chip in enumerate(chips) for a in range(n)]
        return mine, first, landed, passed, from_sib

    def start(self, *refs):
        mine, first, _, _, _ = self._copies(*refs)
        for cp in mine + first:
            cp.start()

    def mid(self, *refs):
        _, _, landed, passed, _ = self._copies(*refs)
        for got, on in zip(landed, passed):
            got.wait_recv()
            on.start()

    def finish(self, *refs):
        mine, first, _, passed, from_sib = self._copies(*refs)
        for cp in from_sib:
            cp.wait_recv()
        for cp in first + passed:
            cp.wait_send()
        for cp in mine:
            cp.wait()


class _Exchange:
    def __init__(self, arrs, rows=None):
        self.srcs = list(arrs)
        self.nsem = len(arrs)
        self.rows = rows if rows is not None else [None] * len(arrs)
        self.out_shape = [jax.ShapeDtypeStruct(a.shape if r is None else (a.shape[0], r[1]) + a.shape[2:], a.dtype)
                          for a, r in zip(arrs, self.rows)]

    def _copies(self, ins, outs, send, recv, loc):
        n = len(self.srcs)
        x, y, c = _place()
        me = _idx((x, y, c))

        def src(a, q):
            r = self.rows[a]
            return ins[a].at[q] if r is None else ins[a].at[q, pl.ds(r[0], r[1])]

        mine = [pltpu.make_async_copy(src(a, me), outs[a].at[me], loc.at[a]) for a in range(n)]
        remote = []
        for k in range(1, NDEV):
            peer = (x ^ (k >> 2), y ^ ((k >> 1) & 1), c ^ (k & 1))
            remote += [pltpu.make_async_remote_copy(
                src_ref=src(a, _idx(peer)), dst_ref=outs[a].at[me], send_sem=send.at[a, k - 1],
                recv_sem=recv.at[a, k - 1], device_id=peer, device_id_type=MESH) for a in range(n)]
        return mine, remote

    def start(self, *refs):
        mine, remote = self._copies(*refs)
        for cp in mine + remote:
            cp.start()

    def mid(self, *refs):
        pass

    def finish(self, *refs):
        mine, remote = self._copies(*refs)
        for cp in remote + mine:
            cp.wait()


def _comm_scratch(n):
    return [pltpu.SemaphoreType.DMA((n, 7)), pltpu.SemaphoreType.DMA((n, 7)), pltpu.SemaphoreType.DMA((n,))]


def _comm_call(name, comm):
    n = len(comm.srcs)

    def body(*refs):
        parts = (refs[:n], refs[n:2 * n]) + tuple(refs[2 * n:])
        comm.start(*parts)
        comm.mid(*parts)
        comm.finish(*parts)

    return pl.pallas_call(body, name=name, in_specs=[ANY] * n, out_specs=[ANY] * n, out_shape=comm.out_shape,
                          scratch_shapes=_comm_scratch(comm.nsem))(*comm.srcs)


def _pcall(body, *, name, grid, in_specs, out_specs, out_shape, scratch_shapes, sem, args, comm=None):
    if comm is None:
        return pl.pallas_call(body, name=name, grid=grid, in_specs=in_specs, out_specs=out_specs, out_shape=out_shape,
                              scratch_shapes=scratch_shapes, compiler_params=_cp(sem))(*args), None
    ni, no, ns, nc = len(in_specs), len(out_shape), len(scratch_shapes), len(comm.srcs)
    total = 1
    for g in grid:
        total *= g
    middle = (4 * total) // 5

    def wrapped(*refs):
        ins, csrc = refs[:ni], refs[ni:ni + nc]
        outs, cdst = refs[ni + nc:ni + nc + no], refs[ni + nc + no:ni + 2 * nc + no]
        scr, sems = refs[ni + 2 * nc + no:ni + 2 * nc + no + ns], refs[ni + 2 * nc + no + ns:]
        step = pl.program_id(0)
        for k in range(1, len(grid)):
            step = step * grid[k] + pl.program_id(k)
        parts = (csrc, cdst) + tuple(sems)

        @pl.when(step == 0)
        def _():
            comm.start(*parts)

        body(*ins, *outs, *scr)

        @pl.when(step == middle)
        def _():
            comm.mid(*parts)

        @pl.when(step == total - 1)
        def _():
            comm.finish(*parts)

    res = pl.pallas_call(
        wrapped, name=name, grid=grid, in_specs=list(in_specs) + [ANY] * nc, out_specs=list(out_specs) + [ANY] * nc,
        out_shape=list(out_shape) + comm.out_shape, scratch_shapes=list(scratch_shapes) + _comm_scratch(comm.nsem),
        compiler_params=_cp(("arbitrary",) * len(grid)))(*args, *comm.srcs)
    return res[:no], res[no:]


def _mm(name, a, b, *, grid, a_spec, b_spec, o_spec, out_shape, dims, kax=None, res=None, res_spec=None,
        jb=0, acc_shape=None, comm=None):
    nk = grid[kax] if kax is not None else 1

    def body(*refs):
        if res is not None:
            a_ref, b_ref, r_ref, o_ref = refs[:4]
        else:
            a_ref, b_ref, o_ref = refs[:3]

        def product():
            if not jb:
                return _dot(a_ref[...], b_ref[...], dims)
            part = _dot(a_ref[0], b_ref[0], dims)
            for j in range(1, jb):
                part = part + _dot(a_ref[j], b_ref[j], dims)
            return part

        def fin(acc):
            if res is not None:
                acc = acc + r_ref[...]
            o_ref[...] = acc.astype(o_ref.dtype)

        if nk == 1:
            fin(product())
        else:
            acc_ref = refs[-1]
            k = pl.program_id(kax)

            @pl.when(k == 0)
            def _():
                acc_ref[...] = jnp.zeros_like(acc_ref)

            acc_ref[...] += product()

            @pl.when(k == nk - 1)
            def _():
                fin(acc_ref[...])

    sem = tuple("arbitrary" if i == kax else "parallel" for i in range(len(grid)))
    in_specs = [a_spec, b_spec] + ([res_spec] if res is not None else [])
    args = (a, b) + ((res,) if res is not None else ())
    scratch = [pltpu.VMEM(acc_shape, F32)] if nk > 1 else []
    (out,), got = _pcall(body, name=name, grid=grid, in_specs=in_specs, out_specs=[o_spec], out_shape=[out_shape],
                         scratch_shapes=scratch, sem=sem, args=args, comm=comm)
    return out if comm is None else (out, got)


def _bs(shape, fn):
    return pl.BlockSpec(shape, fn)


def _rms_fwd(name, x, g, tm=512):
    T, D = x.shape

    def body(x_ref, g_ref, o_ref):
        xf = x_ref[...]
        r = lax.rsqrt(jnp.mean(xf * xf, axis=-1, keepdims=True) + RMS_EPS)
        o_ref[...] = (xf * r * g_ref[...]).astype(o_ref.dtype)

    return pl.pallas_call(
        body, name=name, grid=(T // tm,),
        in_specs=[_bs((tm, D), lambda i: (i, 0)), _bs((1, D), lambda i: (0, 0))],
        out_specs=_bs((tm, D), lambda i: (i, 0)), out_shape=jax.ShapeDtypeStruct((T, D), BF16),
        compiler_params=_cp(("parallel",)))(x, g)


def _rms_bwd(name, x, g, dh, dres=None, tm=512, also_bf16=False, comm=None):
    T, D = x.shape

    def body(*refs):
        if also_bf16:
            refs, dxb_ref = refs[:-1], refs[-1]
        if dres is not None:
            x_ref, g_ref, dh_ref, dres_ref, dx_ref, dg_ref = refs
        else:
            x_ref, g_ref, dh_ref, dx_ref, dg_ref = refs
        i = pl.program_id(0)
        xf = x_ref[...]
        r = lax.rsqrt(jnp.mean(xf * xf, axis=-1, keepdims=True) + RMS_EPS)
        xh = xf * r
        d = dh_ref[...].astype(F32)
        dxh = d * g_ref[...]
        dx = r * (dxh - xh * jnp.mean(dxh * xh, axis=-1, keepdims=True))
        if dres is not None:
            dx = dx + dres_ref[...]
        dx_ref[...] = dx
        if also_bf16:
            dxb_ref[...] = dx.astype(BF16)
        part = jnp.sum(d * xh, axis=0, keepdims=True)

        @pl.when(i == 0)
        def _():
            dg_ref[...] = part

        @pl.when(i > 0)
        def _():
            dg_ref[...] += part

    row = _bs((tm, D), lambda i: (i, 0))
    vec = _bs((1, D), lambda i: (0, 0))
    in_specs = [row, vec, row] + ([row] if dres is not None else [])
    args = (x, g, dh) + ((dres,) if dres is not None else ())
    extra = [jax.ShapeDtypeStruct((T, D), BF16)] if also_bf16 else []
    outs, got = _pcall(
        body, name=name, grid=(T // tm,), in_specs=in_specs, out_specs=[row, vec] + [row] * len(extra),
        out_shape=[jax.ShapeDtypeStruct((T, D), F32), jax.ShapeDtypeStruct((1, D), F32)] + extra, scratch_shapes=[],
        sem=("arbitrary",), args=args, comm=comm)
    return outs if comm is None else (outs, got)


def _loss(y, tgt, tm=512):
    T, D = y.shape

    def body(y_ref, t_ref, dy_ref, s_ref, dyb_ref):
        i = pl.program_id(0)
        e = y_ref[...] - t_ref[...]
        dy = e * (1.0 / D)
        dy_ref[...] = dy
        dyb_ref[...] = dy.astype(BF16)
        part = jnp.sum(e * e, axis=0, keepdims=True)

        @pl.when(i == 0)
        def _():
            s_ref[...] = part

        @pl.when(i > 0)
        def _():
            s_ref[...] += part

    row = _bs((tm, D), lambda i: (i, 0))
    return pl.pallas_call(
        body, name="loss_head", grid=(T // tm,), in_specs=[row, row],
        out_specs=[row, _bs((1, D), lambda i: (0, 0)), row],
        out_shape=[jax.ShapeDtypeStruct((T, D), F32), jax.ShapeDtypeStruct((1, D), F32),
                   jax.ShapeDtypeStruct((T, D), BF16)],
        compiler_params=_cp(("arbitrary",)))(y, tgt)


def _shift_rows(t, k, row):
    return jnp.where(row >= k, pltpu.roll(t, k, 0), 0.0)


def _shift_rows_up(t, k, row, n):
    return jnp.where(row < n - k, pltpu.roll(t, n - k, 0), 0.0)


def _convffn_fwd(name, u, cw, cb, B, S):
    _, J, T, F = u.shape

    def body(u_ref, cw_ref, cb_ref, o_ref, gc_ref):
        a = u_ref[0].astype(F32)
        g = u_ref[1].astype(F32)
        row = lax.broadcasted_iota(jnp.int32, (S, F), 0)
        w0, w1, w2 = cw_ref[0:1, :], cw_ref[1:2, :], cw_ref[2:3, :]
        gc = _shift_rows(g, 2, row) * w0 + _shift_rows(g, 1, row) * w1 + g * w2 + cb_ref[...]
        gc_ref[...] = gc.astype(gc_ref.dtype)
        o_ref[...] = (gc * jax.nn.sigmoid(gc) * a).astype(o_ref.dtype)

    blk = _bs((None, S, F), lambda j, b: (j, b, 0))
    return pl.pallas_call(
        body, name=name, grid=(J, B),
        in_specs=[_bs((2, None, S, F), lambda j, b: (0, j, b, 0)), _bs((None, 3, F), lambda j, b: (j, 0, 0)),
                  _bs((None, 1, F), lambda j, b: (j, 0, 0))],
        out_specs=[blk, blk], out_shape=[jax.ShapeDtypeStruct((J, T, F), BF16)] * 2,
        compiler_params=_cp(("parallel", "parallel")))(u, cw, cb)


def _convffn_bwd(name, u, gcb, cw, dgt, B, S, comm=None):
    _, J, T, F = u.shape

    def body(u_ref, gc_ref, cw_ref, d_ref, du_ref, dcw_ref, dcb_ref):
        b = pl.program_id(1)
        a = u_ref[0].astype(F32)
        g = u_ref[1].astype(F32)
        gc = gc_ref[...].astype(F32)
        d = d_ref[...].astype(F32)
        row = lax.broadcasted_iota(jnp.int32, (S, F), 0)
        w0, w1, w2 = cw_ref[0:1, :], cw_ref[1:2, :], cw_ref[2:3, :]
        sg = jax.nn.sigmoid(gc)
        du_ref[0] = (d * gc * sg).astype(du_ref.dtype)
        dgc = d * a * (sg * (1.0 + gc * (1.0 - sg)))
        up1, up2 = _shift_rows_up(dgc, 1, row, S), _shift_rows_up(dgc, 2, row, S)
        du_ref[1] = (dgc * w2 + up1 * w1 + up2 * w0).astype(du_ref.dtype)
        parts = [jnp.sum(up2 * g, axis=0, keepdims=True), jnp.sum(up1 * g, axis=0, keepdims=True),
                 jnp.sum(dgc * g, axis=0, keepdims=True)]
        pb = jnp.sum(dgc, axis=0, keepdims=True)

        @pl.when(b == 0)
        def _():
            for k in range(3):
                dcw_ref[k:k + 1, :] = parts[k]
            dcb_ref[...] = pb

        @pl.when(b > 0)
        def _():
            for k in range(3):
                dcw_ref[k:k + 1, :] += parts[k]
            dcb_ref[...] += pb

    uspec = _bs((2, None, S, F), lambda j, b: (0, j, b, 0))
    blk = _bs((None, S, F), lambda j, b: (j, b, 0))
    return _pcall(
        body, name=name, grid=(J, B),
        in_specs=[uspec, blk, _bs((None, 3, F), lambda j, b: (j, 0, 0)), blk],
        out_specs=[uspec, _bs((None, 3, F), lambda j, b: (j, 0, 0)), _bs((None, 1, F), lambda j, b: (j, 0, 0))],
        out_shape=[jax.ShapeDtypeStruct(u.shape, BF16), jax.ShapeDtypeStruct((J, 3, F), F32),
                   jax.ShapeDtypeStruct((J, 1, F), F32)],
        scratch_shapes=[], sem=("parallel", "arbitrary"), args=(u, gcb, cw, dgt), comm=comm)


def _ret_tables(S):
    half = RET_DK // 2
    inv = ROPE_THETA ** (-jnp.arange(half, dtype=F32) / half)
    ang = jnp.arange(S).astype(F32)[:, None] * inv[None, :]
    lg = jnp.log1p(-jnp.exp2(-5.0 - jnp.arange(RET_H, dtype=F32)))
    i = jnp.arange(RET_SC, dtype=F32)
    same_or_earlier = (jnp.floor(i[None, :] / CHUNK) <= jnp.floor(i[:, None] / CHUNK)).astype(F32)
    dm = jnp.exp(lg[:, None, None] * jnp.abs(i[:, None] - i[None, :])) * same_or_earlier[None]
    qd = jnp.exp(lg[:, None] * (i + 1.0))[:, :, None]
    kd = jnp.exp(lg[:, None] * (RET_SC - 1.0 - i))[:, :, None]
    cd = jnp.exp(lg * RET_SC)[:, None, None]
    return jnp.cos(ang), jnp.sin(ang), dm, qd, kd, cd


def _rope_halves(t, cs, sn):
    h = t.shape[-1] // 2
    t1, t2 = t[:, :h], t[:, h:]
    return jnp.concatenate([t1 * cs - t2 * sn, t2 * cs + t1 * sn], axis=-1)


def _unrope_halves(d, cs, sn):
    h = d.shape[-1] // 2
    d1, d2 = d[:, :h], d[:, h:]
    return jnp.concatenate([d1 * cs + d2 * sn, d2 * cs - d1 * sn], axis=-1)


def _ret_specs(nC, order):
    SC = RET_SC

    def sp(shape, fn):
        return _bs(shape, lambda *g: fn(*order(*g)))

    q = sp((SC, RET_DK), lambda b, h, c: (b * nC + c, h))
    k = sp((SC, RET_DK), lambda b, h, c: (b * nC + c, RET_H + h))
    v = sp((SC, RET_DV), lambda b, h, c: (b * nC + c, RET_H + h))
    g = sp((SC, RET_DV), lambda b, h, c: (b * nC + c, 2 * RET_H + h))
    cs = sp((SC, RET_DK // 2), lambda b, h, c: (c, 0))
    dm = sp((None, SC, SC), lambda b, h, c: (h, 0, 0))
    dv = sp((None, SC, 1), lambda b, h, c: (h, 0, 0))
    cd = sp((None, 1, 1), lambda b, h, c: (h, 0, 0))
    gn = sp((None, 1, RET_DV), lambda b, h, c: (h, 0, 0))
    wide = sp((SC, RET_DV), lambda b, h, c: (b * nC + c, h))
    narrow = sp((SC, RET_DK), lambda b, h, c: (b * nC + c, h))
    st = sp((None, None, None, RET_DK, RET_DV), lambda b, h, c: (b, h, c, 0, 0))
    return dict(q=q, k=k, v=v, g=g, cs=cs, dm=dm, dv=dv, cd=cd, gn=gn, wide=wide, narrow=narrow, st=st)


def _ret_fwd(proj, tabs, gn, B, S, comm=None):
    T = B * S
    nC = S // RET_SC
    cos, sin, dm, qd, kd, cd = tabs
    s = _ret_specs(nC, lambda b, h, c: (b, h, c))

    def body(q_ref, k_ref, v_ref, g_ref, cos_ref, sin_ref, dm_ref, qd_ref, kd_ref, cd_ref, gn_ref,
             o_ref, gt_ref, st_ref, state):
        c = pl.program_id(2)

        @pl.when(c == 0)
        def _():
            state[...] = jnp.zeros_like(state)

        cs, sn = cos_ref[...], sin_ref[...]
        qf = _rope_halves(q_ref[...].astype(F32), cs, sn)
        kf = _rope_halves(k_ref[...].astype(F32), cs, sn) * (RET_DK ** -0.5)
        v = v_ref[...]
        p = _dot(qf.astype(BF16), kf.astype(BF16), NT) * dm_ref[...]
        st = state[...]
        stb = st.astype(BF16)
        st_ref[...] = stb
        o = _dot(p.astype(BF16), v, NN) + _dot((qf * qd_ref[...]).astype(BF16), stb, NN)
        state[...] = st * cd_ref[...] + _dot((kf * kd_ref[...]).astype(BF16), v, TN)
        o_ref[...] = o
        r = lax.rsqrt(jnp.mean(o * o, axis=-1, keepdims=True) + RMS_EPS)
        gf = g_ref[...].astype(F32)
        gt_ref[...] = ((o * r * gn_ref[...]) * (gf * jax.nn.sigmoid(gf))).astype(BF16)

    return _pcall(
        body, name="ret_fwd", grid=(B, RET_H, nC),
        in_specs=[s["q"], s["k"], s["v"], s["g"], s["cs"], s["cs"], s["dm"], s["dv"], s["dv"], s["cd"], s["gn"]],
        out_specs=[s["wide"], s["wide"], s["st"]],
        out_shape=[jax.ShapeDtypeStruct((T, RET_H * RET_DV), F32), jax.ShapeDtypeStruct((T, RET_H * RET_DV), BF16),
                   jax.ShapeDtypeStruct((B, RET_H, nC, RET_DK, RET_DV), BF16)],
        scratch_shapes=[pltpu.VMEM((RET_DK, RET_DV), F32)], sem=("parallel", "parallel", "arbitrary"),
        args=(proj, proj, proj, proj, cos, sin, dm, qd, kd, cd, gn), comm=comm)


def _ret_bwd(proj, o_raw, states, dgt, tabs, gn, B, S, comm=None):
    T = B * S
    nC = S // RET_SC
    cos, sin, dm, qd, kd, cd = tabs
    s = _ret_specs(nC, lambda b, c, h: (b, h, nC - 1 - c))

    def body(q_ref, k_ref, v_ref, g_ref, o_ref, st_ref, d_ref, cos_ref, sin_ref, dm_ref, qd_ref, kd_ref, cd_ref,
             gn_ref, dproj_ref, dgn_ref, dstates):
        b, c, h = pl.program_id(0), pl.program_id(1), pl.program_id(2)
        dstate = dstates.at[h]

        @pl.when(c == 0)
        def _():
            dstate[...] = jnp.zeros_like(dstate)

        @pl.when((b == 0) & (c == 0))
        def _():
            dgn_ref[h] = jnp.zeros((1, RET_DV), F32)

        cs, sn = cos_ref[...], sin_ref[...]
        qf = _rope_halves(q_ref[...].astype(F32), cs, sn)
        kf = _rope_halves(k_ref[...].astype(F32), cs, sn) * (RET_DK ** -0.5)
        v = v_ref[...]
        gnv = gn_ref[h]
        o = o_ref[...]
        r = lax.rsqrt(jnp.mean(o * o, axis=-1, keepdims=True) + RMS_EPS)
        oh = o * r
        gf = g_ref[...].astype(F32)
        sg = jax.nn.sigmoid(gf)
        d = d_ref[...].astype(F32)
        dg = (d * (oh * gnv) * (sg * (1.0 + gf * (1.0 - sg)))).astype(BF16)
        don = d * (gf * sg)
        dgn_ref[h] += jnp.sum(don * oh, axis=0, keepdims=True)
        doh = don * gnv
        dO = (r * (doh - oh * jnp.mean(doh * oh, axis=-1, keepdims=True))).astype(BF16)
        dmv = dm_ref[h]
        qb, kb = qf.astype(BF16), kf.astype(BF16)
        p = (_dot(qb, kb, NT) * dmv).astype(BF16)
        dp = (_dot(dO, v, NT) * dmv).astype(BF16)
        st = st_ref[...]
        dsn = dstate[...]
        dsb = dsn.astype(BF16)
        qdv, kdv = qd_ref[h], kd_ref[h]
        dq = _dot(dp, kb, NN) + _dot(dO, st, NT) * qdv
        dk = _dot(dp, qb, TN) + _dot(v, dsb, NT) * kdv
        dv = _dot(p, dO, TN) + _dot((kf * kdv).astype(BF16), dsb, NN)
        dstate[...] = dsn * cd_ref[h] + _dot((qf * qdv).astype(BF16), dO, TN)
        dq = _unrope_halves(dq, cs, sn).astype(BF16)
        dk = (_unrope_halves(dk, cs, sn) * (RET_DK ** -0.5)).astype(BF16)
        dv = dv.astype(BF16)
        nq, nv = RET_H * RET_DK, RET_H * RET_DV
        for hh in range(RET_H):
            @pl.when(h == hh)
            def _():
                dproj_ref[:, hh * RET_DK:(hh + 1) * RET_DK] = dq
                dproj_ref[:, nq + hh * RET_DK:nq + (hh + 1) * RET_DK] = dk
                dproj_ref[:, 2 * nq + hh * RET_DV:2 * nq + (hh + 1) * RET_DV] = dv
                dproj_ref[:, 2 * nq + nv + hh * RET_DV:2 * nq + nv + (hh + 1) * RET_DV] = dg

    width = 2 * RET_H * (RET_DK + RET_DV)

    def all_heads(*shape):
        return _bs((RET_H,) + shape, lambda b, c, h: (0,) * (1 + len(shape)))

    return _pcall(
        body, name="ret_bwd", grid=(B, nC, RET_H),
        in_specs=[s["q"], s["k"], s["v"], s["g"], s["wide"], s["st"], s["wide"], s["cs"], s["cs"],
                  all_heads(RET_SC, RET_SC), all_heads(RET_SC, 1), all_heads(RET_SC, 1), all_heads(1, 1),
                  all_heads(1, RET_DV)],
        out_specs=[_bs((RET_SC, width), lambda b, c, h: (b * nC + nC - 1 - c, 0)),
                   _bs((RET_H, 1, RET_DV), lambda b, c, h: (0, 0, 0))],
        out_shape=[jax.ShapeDtypeStruct((T, width), BF16), jax.ShapeDtypeStruct((RET_H, 1, RET_DV), F32)],
        scratch_shapes=[pltpu.VMEM((RET_H, RET_DK, RET_DV), F32)], sem=("arbitrary", "arbitrary", "arbitrary"),
        args=(proj, proj, proj, proj, o_raw, states, dgt, cos, sin, dm, qd, kd, cd, gn), comm=comm)


MLA_PAD = 256
MLA_R2 = 2 * MLA_ROPE


def _dup(t):
    return jnp.concatenate([t, t], axis=-1)


def _fold(t):
    return t[..., :MLA_ROPE] + t[..., MLA_ROPE:]


def _mla_tables(S):
    half = MLA_ROPE // 2
    inv = ROPE_THETA ** (-jnp.arange(half, dtype=F32) / half)
    ang = jnp.arange(S).astype(F32)[:, None] * inv[None, :]
    cos, sin, zero = jnp.cos(ang), jnp.sin(ang), jnp.zeros((S, MLA_ROPE), F32)
    return jnp.concatenate([cos, cos, zero], axis=-1), jnp.concatenate([-sin, sin, zero], axis=-1)


def _head_norm_rope(n, r2, gn, gr2, cos, sin, scale):
    ssq = jnp.sum(n * n, axis=-1, keepdims=True) + 0.5 * jnp.sum(r2 * r2, axis=-1, keepdims=True)
    rstd = lax.rsqrt(ssq * (1.0 / MLA_QK) + RMS_EPS)
    yn = n * rstd * gn
    yr = r2 * rstd * gr2
    z = yr * cos + pltpu.roll(yr, MLA_ROPE // 2, 1) * sin
    if scale != 1.0:
        yn, z = yn * scale, z * scale
    return yn, z


def _head_norm_rope_bwd(dn, dz, n, r2, gn, gr2, cos, sin, scale):
    ssq = jnp.sum(n * n, axis=-1, keepdims=True) + 0.5 * jnp.sum(r2 * r2, axis=-1, keepdims=True)
    rstd = lax.rsqrt(ssq * (1.0 / MLA_QK) + RMS_EPS)
    hn, hr = n * rstd, r2 * rstd
    if scale != 1.0:
        dn, dz = dn * scale, dz * scale
    dyr = dz * cos + pltpu.roll(dz * sin, MLA_R2 - MLA_ROPE // 2, 1)
    dgn = jnp.sum(dn * hn, axis=0, keepdims=True)
    dgr = jnp.sum(dyr * hr, axis=0, keepdims=True)
    dhn, dhr = dn * gn, dyr * gr2
    mt = (jnp.sum(dhn * hn, axis=-1, keepdims=True) + jnp.sum(dhr * hr, axis=-1, keepdims=True)) * (1.0 / MLA_QK)
    return rstd * (dhn - hn * mt), rstd * (dhr - 0.5 * hr * mt), dgn, dgr


def _diag_bias():
    i = jnp.arange(ATT_TQ)
    return jnp.where((i[None, :] // CHUNK) <= (i[:, None] // CHUNK), 0.0, MASK_VALUE).astype(F32)


def _store_pair(dst, rows, n, r2):
    dst[rows, :MLA_NOPE] = n.astype(BF16)
    dst[rows, MLA_NOPE:] = r2.astype(BF16)


def _mla_fwd(q_raw, kv, kr, gains, tabs, B, S, comm=None):
    T = B * S
    TQ = ATT_TQ
    nQ = S // TQ
    qgn, qgr, kgn, kgr = gains
    cos, sin = tabs
    scale = MLA_QK ** -0.5

    def body(q_ref, kv_ref, kr_ref, qgn_ref, qgr_ref, kgn_ref, kgr_ref, c_ref, s_ref, bias_ref,
             o_ref, lse_ref, qf_s, kf_s, v_s):
        def prep(t, _):
            rows = pl.ds(pl.multiple_of(t * TQ, TQ), TQ)
            cs, sn = c_ref[rows, :], s_ref[rows, :]
            qn, qr = _head_norm_rope(q_ref[rows, :MLA_NOPE], q_ref[rows, MLA_NOPE:], qgn_ref[...], qgr_ref[...],
                                     cs, sn, scale)
            _store_pair(qf_s, rows, qn, qr)
            kn, krr = _head_norm_rope(kv_ref[rows, :MLA_NOPE], kr_ref[rows, :], kgn_ref[...], kgr_ref[...], cs, sn, 1.0)
            _store_pair(kf_s, rows, kn, krr)
            v_s[rows, :] = kv_ref[rows, MLA_NOPE:].astype(BF16)
            return 0

        lax.fori_loop(0, nQ, prep, 0, unroll=2)
        for i in range(nQ):
            rows = slice(i * TQ, (i + 1) * TQ)
            q = qf_s[rows, :]
            sd = _dot(q, kf_s[rows, :], NT) + bias_ref[...]
            m = jnp.max(sd, axis=-1, keepdims=True)
            if i:
                sl = _dot(q, kf_s[:i * TQ, :], NT)
                m = jnp.maximum(m, jnp.max(sl, axis=-1, keepdims=True))
            pd = jnp.exp(sd - m)
            l = jnp.sum(pd, axis=-1, keepdims=True)
            acc = _dot(pd.astype(BF16), v_s[rows, :], NN)
            if i:
                pl_ = jnp.exp(sl - m)
                l = l + jnp.sum(pl_, axis=-1, keepdims=True)
                acc = acc + _dot(pl_.astype(BF16), v_s[:i * TQ, :], NN)
            o_ref[rows, :] = (acc / l).astype(BF16)
            lse_ref[rows, :] = m + jnp.log(l)

    def vec(n):
        return _bs((1, n), lambda b, h: (0, 0))

    def cols(n):
        return _bs((S, n), lambda b, h: (b, h))

    tab = _bs((S, MLA_R2), lambda b, h: (0, 0))
    return _pcall(
        body, name="mla_fwd", grid=(B, MLA_H),
        in_specs=[cols(MLA_PAD), cols(MLA_NOPE + MLA_V), _bs((S, MLA_R2), lambda b, h: (b, 0)),
                  vec(MLA_NOPE), vec(MLA_R2), vec(MLA_NOPE), vec(MLA_R2), tab, tab,
                  _bs((TQ, TQ), lambda b, h: (0, 0))],
        out_specs=[cols(MLA_V), _bs((None, S, 1), lambda b, h: (h, b, 0)), cols(MLA_PAD), cols(MLA_PAD)],
        out_shape=[jax.ShapeDtypeStruct((T, MLA_H * MLA_V), BF16), jax.ShapeDtypeStruct((MLA_H, T, 1), F32),
                   jax.ShapeDtypeStruct((T, MLA_H * MLA_PAD), BF16), jax.ShapeDtypeStruct((T, MLA_H * MLA_PAD), BF16)],
        scratch_shapes=[pltpu.VMEM((S, MLA_V), BF16)],
        sem=("parallel", "parallel"), args=(q_raw, kv, kr, qgn, qgr, kgn, kgr, cos, sin, _diag_bias()), comm=comm)


def _mla_bwd(q_raw, kv, kr, o, lse, do, qf, kf, gains, tabs, B, S, comm=None):
    T = B * S
    TQ = ATT_TQ
    nQ = S // TQ
    qgn, qgr, kgn, kgr = gains
    cos, sin = tabs
    scale = MLA_QK ** -0.5

    def body(q_ref, kv_ref, kr_ref, o_ref, lse_ref, do_ref, qf_s, kf_s, qgn_ref, qgr_ref, kgn_ref, kgr_ref, c_ref, s_ref,
             bias_ref, dq_ref, dkv_ref, dkr_ref, dqgn_ref, dqgr_ref, dkgn_ref, dkgr_ref,
             v_s, dl_s, dq_s, dk_s, dv_s):
        b, h = pl.program_id(0), pl.program_id(1)

        def blk(t):
            return pl.ds(pl.multiple_of(t * TQ, TQ), TQ)

        def prep(t, _):
            rows = blk(t)
            v_s[rows, :] = kv_ref[rows, MLA_NOPE:].astype(BF16)
            dl_s[rows, :] = jnp.sum(do_ref[rows, :].astype(F32) * o_ref[rows, :].astype(F32), axis=-1, keepdims=True)
            dk_s[rows, :] = jnp.zeros((TQ, MLA_PAD), F32)
            dv_s[rows, :] = jnp.zeros((TQ, MLA_V), F32)
            return 0

        lax.fori_loop(0, nQ, prep, 0, unroll=2)

        gqn, gqr = jnp.zeros((1, MLA_NOPE), F32), jnp.zeros((1, MLA_R2), F32)
        for i in range(nQ):
            rows = slice(i * TQ, (i + 1) * TQ)
            q, doi, lse_i, dl_i = qf_s[rows, :], do_ref[rows, :], lse_ref[rows, :], dl_s[rows, :]

            def part(cols, bias):
                k, v = kf_s[cols, :], v_s[cols, :]
                s = _dot(q, k, NT)
                if bias is not None:
                    s = s + bias
                p = jnp.exp(s - lse_i)
                ds = (p * (_dot(doi, v, NT) - dl_i)).astype(BF16)
                dk_s[cols, :] += _dot(ds, q, TN)
                dv_s[cols, :] += _dot(p.astype(BF16), doi, TN)
                return _dot(ds, k, NN)

            dq = part(rows, bias_ref[...])
            if i:
                dq = dq + part(slice(0, i * TQ), None)
            dq_s[...] = dq
            dqn, dqr, a0, a1 = _head_norm_rope_bwd(dq_s[:, :MLA_NOPE], dq_s[:, MLA_NOPE:], q_ref[rows, :MLA_NOPE],
                                                   q_ref[rows, MLA_NOPE:], qgn_ref[...], qgr_ref[...],
                                                   c_ref[rows, :], s_ref[rows, :], scale)
            _store_pair(dq_ref, rows, dqn, dqr)
            gqn, gqr = gqn + a0, gqr + a1

        def post(t, carry):
            rows = blk(t)
            dkn, dkr, a2, a3 = _head_norm_rope_bwd(dk_s[rows, :MLA_NOPE], dk_s[rows, MLA_NOPE:],
                                                   kv_ref[rows, :MLA_NOPE], kr_ref[rows, :], kgn_ref[...], kgr_ref[...],
                                                   c_ref[rows, :], s_ref[rows, :], 1.0)
            dkv_ref[rows, :MLA_NOPE] = dkn.astype(BF16)
            dkv_ref[rows, MLA_NOPE:] = dv_s[rows, :].astype(BF16)

            @pl.when(h == 0)
            def _():
                dkr_ref[rows, :] = dkr

            @pl.when(h > 0)
            def _():
                dkr_ref[rows, :] += dkr

            return carry[0] + a2, carry[1] + a3

        gkn, gkr = lax.fori_loop(0, nQ, post, (jnp.zeros((1, MLA_NOPE), F32), jnp.zeros((1, MLA_R2), F32)), unroll=2)
        first = (b == 0) & (h == 0)

        @pl.when(first)
        def _():
            dqgn_ref[...] = gqn
            dqgr_ref[...] = gqr
            dkgn_ref[...] = gkn
            dkgr_ref[...] = gkr

        @pl.when(jnp.logical_not(first))
        def _():
            dqgn_ref[...] += gqn
            dqgr_ref[...] += gqr
            dkgn_ref[...] += gkn
            dkgr_ref[...] += gkr

    def vec(n):
        return _bs((1, n), lambda b, h: (0, 0))

    def cols(n):
        return _bs((S, n), lambda b, h: (b, h))

    tab = _bs((S, MLA_R2), lambda b, h: (0, 0))
    return _pcall(
        body, name="mla_bwd", grid=(B, MLA_H),
        in_specs=[cols(MLA_PAD), cols(MLA_NOPE + MLA_V), _bs((S, MLA_R2), lambda b, h: (b, 0)), cols(MLA_V),
                  _bs((None, S, 1), lambda b, h: (h, b, 0)), cols(MLA_V), cols(MLA_PAD), cols(MLA_PAD),
                  vec(MLA_NOPE), vec(MLA_R2), vec(MLA_NOPE), vec(MLA_R2), tab, tab,
                  _bs((TQ, TQ), lambda b, h: (0, 0))],
        out_specs=[cols(MLA_PAD), cols(MLA_NOPE + MLA_V), _bs((S, MLA_R2), lambda b, h: (b, 0)),
                   vec(MLA_NOPE), vec(MLA_R2), vec(MLA_NOPE), vec(MLA_R2)],
        out_shape=[jax.ShapeDtypeStruct((T, MLA_H * MLA_PAD), BF16),
                   jax.ShapeDtypeStruct((T, MLA_H * (MLA_NOPE + MLA_V)), BF16),
                   jax.ShapeDtypeStruct((T, MLA_R2), F32), jax.ShapeDtypeStruct((1, MLA_NOPE), F32),
                   jax.ShapeDtypeStruct((1, MLA_R2), F32), jax.ShapeDtypeStruct((1, MLA_NOPE), F32),
                   jax.ShapeDtypeStruct((1, MLA_R2), F32)],
        scratch_shapes=[pltpu.VMEM((S, MLA_V), BF16), pltpu.VMEM((S, 1), F32), pltpu.VMEM((TQ, MLA_PAD), F32),
                        pltpu.VMEM((S, MLA_PAD), F32), pltpu.VMEM((S, MLA_V), F32)],
        sem=("arbitrary", "arbitrary"),
        args=(q_raw, kv, kr, o, lse, do, qf, kf, qgn, qgr, kgn, kgr, cos, sin, _diag_bias()), comm=comm)


def _adamw(name, recvs, w, m, v, tr=None, comm=None):
    n, R, C = recvs[0].shape
    L = len(recvs)
    Lw, Rw, _ = w.shape
    assert Lw * Rw == L * R and w.shape[2] == C
    tr = R if tr is None else tr
    assert R % tr == 0 and Rw % tr == 0
    per = R // tr
    per_w = Rw // tr
    c1 = 1.0 - ADAM_B1 ** ADAM_STEP
    c2 = 1.0 - ADAM_B2 ** ADAM_STEP

    def body(*refs):
        r_refs = refs[:L]
        w_ref, m_ref, v_ref, g_ref, d_ref, nm_ref, nv_ref = refs[L:]
        layer = pl.program_id(0) // per

        def total(r_ref):
            t = r_ref[0].astype(F32)
            for k in range(1, n):
                t = t + r_ref[k].astype(F32)
            return t

        g = total(r_refs[0]) if L == 1 else lax.switch(layer, [functools.partial(total, r) for r in r_refs])
        mm = ADAM_B1 * m_ref[...] + (1.0 - ADAM_B1) * g
        vv = ADAM_B2 * v_ref[...] + (1.0 - ADAM_B2) * (g * g)
        g_ref[...] = g
        nm_ref[...] = mm
        nv_ref[...] = vv
        d_ref[...] = -ADAM_LR * ((mm / c1) / (jnp.sqrt(vv / c2) + ADAM_EPS) + ADAM_WD * w_ref[...])

    blk = _bs((None, tr, C), lambda i: (i // per_w, i % per_w, 0))
    r_specs = [_bs((n, tr, C), functools.partial(lambda l, i: (0, jnp.clip(i - l * per, 0, per - 1), 0), l))
               for l in range(L)]
    outs, got = _pcall(body, name=name, grid=(L * per,), in_specs=r_specs + [blk, blk, blk], out_specs=[blk] * 4,
                       out_shape=[jax.ShapeDtypeStruct(w.shape, F32)] * 4, scratch_shapes=[], sem=("arbitrary",),
                       args=(*recvs, w, m, v), comm=comm)
    return outs if comm is None else (outs, got)


def _sum8(name, a):
    n, R, C = a.shape

    def body(a_ref, o_ref):
        s = a_ref[0]
        for k in range(1, n):
            s = s + a_ref[k]
        o_ref[...] = s

    return pl.pallas_call(body, name=name, out_shape=jax.ShapeDtypeStruct((R, C), a.dtype))(a)


def _sds(shape, dt):
    return jax.ShapeDtypeStruct(shape, dt)


def _norm_proj(name, x, g, w, o_spec, out_shape, tm=1024, comm=None):
    T, K = x.shape
    J, _, n = w.shape

    def body(x_ref, g_ref, w_ref, o_ref, h_ref, hs):
        @pl.when(pl.program_id(1) == 0)
        def _():
            xf = x_ref[...]
            r = lax.rsqrt(jnp.mean(xf * xf, axis=-1, keepdims=True) + RMS_EPS)
            h = (xf * r * g_ref[...]).astype(BF16)
            hs[...] = h
            h_ref[...] = h

        o_ref[...] = _dot(hs[...], w_ref[...], NN).astype(o_ref.dtype)

    row = _bs((tm, K), lambda m, j: (m, 0))
    (out, h), got = _pcall(
        body, name=name, grid=(T // tm, J),
        in_specs=[row, _bs((1, K), lambda m, j: (0, 0)), _bs((None, K, n), lambda m, j: (j, 0, 0))],
        out_specs=[o_spec, row], out_shape=[out_shape, _sds((T, K), BF16)], scratch_shapes=[pltpu.VMEM((tm, K), BF16)],
        sem=("parallel", "arbitrary"), args=(x, g, w), comm=comm)
    return out, h, got


def _proj_shared_dx(name, d, w, tm=1024, comm=None):
    J, T, n = d.shape
    K = w.shape[1]
    return _mm(name, d, w, grid=(T // tm, J), a_spec=_bs((None, tm, n), lambda m, k: (k, m, 0)),
               b_spec=_bs((None, K, n), lambda m, k: (k, 0, 0)), o_spec=_bs((tm, K), lambda m, k: (m, 0)),
               out_shape=_sds((T, K), BF16), dims=NT, kax=1, acc_shape=(tm, K), comm=comm)


def _out_proj(name, a, w, res, tm=512):
    J, T, k = a.shape
    N = w.shape[2]
    return _mm(name, a, w, grid=(T // tm,), a_spec=_bs((J, tm, k), lambda m: (0, m, 0)),
               b_spec=_bs((J, k, N), lambda m: (0, 0, 0)), o_spec=_bs((tm, N), lambda m: (m, 0)),
               out_shape=_sds((T, N), F32), dims=NN, res=res, res_spec=_bs((tm, N), lambda m: (m, 0)), jb=J)


def _out_proj_dx(name, dx, w, tm=1024, comm=None):
    T, N = dx.shape
    J, k, _ = w.shape
    return _mm(name, dx, w, grid=(T // tm, J), a_spec=_bs((tm, N), lambda m, j: (m, 0)),
               b_spec=_bs((None, k, N), lambda m, j: (j, 0, 0)), o_spec=_bs((None, tm, k), lambda m, j: (j, m, 0)),
               out_shape=_sds((J, T, k), BF16), dims=NT, comm=comm)


def _out_proj_dw(name, a, dx, tt=1024, comm=None):
    J, T, k = a.shape
    N = dx.shape[1]
    tt = min(tt, T)
    return _mm(name, a, dx, grid=(J, T // tt), a_spec=_bs((None, tt, k), lambda j, t: (j, t, 0)),
               b_spec=_bs((tt, N), lambda j, t: (t, 0)), o_spec=_bs((None, k, N), lambda j, t: (j, 0, 0)),
               out_shape=_sds((J, k, N), BF16), dims=TN, kax=1, acc_shape=(k, N), comm=comm)


def _dense(name, a, b, dims, out_dtype, tm=512, res=None, comm=None):
    if dims == TN:
        T, K = a.shape
        N = b.shape[1]
        return _mm(name, a, b, grid=(T // tm,), a_spec=_bs((tm, K), lambda t: (t, 0)),
                   b_spec=_bs((tm, N), lambda t: (t, 0)), o_spec=_bs((K, N), lambda t: (0, 0)),
                   out_shape=_sds((K, N), out_dtype), dims=TN, kax=0, acc_shape=(K, N), comm=comm)
    M, K = a.shape
    N = b.shape[1] if dims == NN else b.shape[0]
    row = _bs((tm, N), lambda m: (m, 0))
    return _mm(name, a, b, grid=(M // tm,), a_spec=_bs((tm, K), lambda m: (m, 0)), b_spec=_bs(b.shape, lambda m: (0, 0)),
               o_spec=row, out_shape=_sds((M, N), out_dtype), dims=dims, res=res,
               res_spec=row if res is not None else None, comm=comm)


def _bf16(x):
    return x.astype(BF16)


def _ffn_fwd(i, x, norm_g, w_in, cw, cb, w_out, B, S, comm_in=None):
    T = x.shape[0]
    u, h, got = _norm_proj(f"ffn{i}_in", x, norm_g, w_in, _bs((None, 1024, FSH), lambda m, j: (j, m, 0)),
                           _sds((NDEV, T, FSH), BF16), comm=comm_in)
    u4 = u.reshape(2, 4, T, FSH)
    gt, gcb = _convffn_fwd(f"ffn{i}_gate", u4, cw, cb, B, S)
    y = _out_proj(f"ffn{i}_out", gt, w_out, x)
    return y, (x, h, u4, gt, gcb), got


def _ffn_bwd(i, dy, dyb, saved, norm_g, w_in, cw, w_out, B, S, first_half_early, riders=(None, None)):
    x, h, u4, gt, gcb = saved
    dgt = _out_proj_dx(f"ffn{i}_out_dx", dyb, w_out, comm=riders[0])
    dw_out = _out_proj_dw(f"ffn{i}_out_dw", gt, dyb, comm=riders[1])
    dgt, got0 = dgt if riders[0] is not None else (dgt, None)
    dw_out, got1 = dw_out if riders[1] is not None else (dw_out, None)
    dw_out = dw_out.reshape(NDEV, FSH // 2, D_MODEL)
    (du4, dcw, dcb), (r_out,) = _convffn_bwd(f"ffn{i}_gate_bwd", u4, gcb, cw, dgt, B, S, comm=_Exchange([dw_out]))
    du = du4.reshape(NDEV, du4.shape[2], FSH)
    dw_in = _out_proj_dw(f"ffn{i}_in_dw", du, h, tt=2048)
    r_in = None
    if first_half_early:
        dh, (r_in,) = _proj_shared_dx(f"ffn{i}_in_dx", du, w_in, comm=_Exchange([dw_in], rows=[(0, FSH // 2)]))
    else:
        dh = _proj_shared_dx(f"ffn{i}_in_dx", du, w_in)
    dx, dgn, dxb = _rms_bwd(f"ffn{i}_norm_bwd", x, norm_g, dh, dres=dy, also_bf16=True)
    return dx, dxb, dict(w_in=dw_in, norm=dgn, cw=dcw, cb=dcb), r_out, r_in, (got0, got1)


def kernel(x, ret_norm, ret_w_in, ret_gn, ret_w_out, mla_norm, mla_w_in, mla_q_norm, mla_w_qb, mla_kv_norm, mla_w_kvb, mla_q_head_norm, mla_k_head_norm, mla_w_out, ffn_norm, ffn_w_in, ffn_conv_w, ffn_conv_b, ffn_w_out, loss_target, m_ret_norm, m_ret_w_in, m_ret_gn, m_ret_w_out, m_mla_norm, m_mla_w_in, m_mla_q_norm, m_mla_w_qb, m_mla_kv_norm, m_mla_w_kvb, m_mla_q_head_norm, m_mla_k_head_norm, m_mla_w_out, m_ffn_norm, m_ffn_w_in, m_ffn_conv_w, m_ffn_conv_b, m_ffn_w_out, v_ret_norm, v_ret_w_in, v_ret_gn, v_ret_w_out, v_mla_norm, v_mla_w_in, v_mla_q_norm, v_mla_w_qb, v_mla_kv_norm, v_mla_w_kvb, v_mla_q_head_norm, v_mla_k_head_norm, v_mla_w_out, v_ffn_norm, v_ffn_w_in, v_ffn_conv_w, v_ffn_conv_b, v_ffn_w_out):
    B, S, D = x.shape
    T = B * S
    w = dict(ret_norm=ret_norm, ret_w_in=ret_w_in, ret_gn=ret_gn, ret_w_out=ret_w_out, mla_norm=mla_norm,
             mla_w_in=mla_w_in, mla_q_norm=mla_q_norm, mla_w_qb=mla_w_qb, mla_kv_norm=mla_kv_norm, mla_w_kvb=mla_w_kvb,
             mla_q_head_norm=mla_q_head_norm, mla_k_head_norm=mla_k_head_norm, mla_w_out=mla_w_out, ffn_norm=ffn_norm,
             ffn_w_in=ffn_w_in, ffn_conv_w=ffn_conv_w, ffn_conv_b=ffn_conv_b, ffn_w_out=ffn_w_out)
    mom = dict(ret_norm=m_ret_norm, ret_w_in=m_ret_w_in, ret_gn=m_ret_gn, ret_w_out=m_ret_w_out, mla_norm=m_mla_norm,
               mla_w_in=m_mla_w_in, mla_q_norm=m_mla_q_norm, mla_w_qb=m_mla_w_qb, mla_kv_norm=m_mla_kv_norm,
               mla_w_kvb=m_mla_w_kvb, mla_q_head_norm=m_mla_q_head_norm, mla_k_head_norm=m_mla_k_head_norm,
               mla_w_out=m_mla_w_out, ffn_norm=m_ffn_norm, ffn_w_in=m_ffn_w_in, ffn_conv_w=m_ffn_conv_w,
               ffn_conv_b=m_ffn_conv_b, ffn_w_out=m_ffn_w_out)
    var = dict(ret_norm=v_ret_norm, ret_w_in=v_ret_w_in, ret_gn=v_ret_gn, ret_w_out=v_ret_w_out, mla_norm=v_mla_norm,
               mla_w_in=v_mla_w_in, mla_q_norm=v_mla_q_norm, mla_w_qb=v_mla_w_qb, mla_kv_norm=v_mla_kv_norm,
               mla_w_kvb=v_mla_w_kvb, mla_q_head_norm=v_mla_q_head_norm, mla_k_head_norm=v_mla_k_head_norm,
               mla_w_out=v_mla_w_out, ffn_norm=v_ffn_norm, ffn_w_in=v_ffn_w_in, ffn_conv_w=v_ffn_conv_w,
               ffn_conv_b=v_ffn_conv_b, ffn_w_out=v_ffn_w_out)
    BIG = ["ret_w_in", "ret_w_out", "mla_w_in", "mla_w_qb", "mla_w_kvb", "mla_w_out", "ffn_w_in", "ffn_w_out"]
    REPL = ["ret_norm", "ffn_norm", "mla_q_head_norm", "mla_k_head_norm", "ffn_conv_b"]
    SHARDED_SMALL = ["ffn_conv_w", "ret_gn", "mla_norm", "mla_q_norm", "mla_kv_norm"]
    dev = _idx(_place())

    def blk16(k, i=0):
        return _bf16(w[k][i])

    small_vec = jnp.concatenate([w[k].reshape(-1) for k in SHARDED_SMALL])
    n_small = small_vec.shape[0]
    small_vec = jnp.pad(small_vec, (0, 4096 - n_small)).reshape(32, 128)
    Wret_in, sg = _comm_call("gather_ret_w_in", _Gather([blk16("ret_w_in"), small_vec], parts=2))
    sg = sg.reshape(NDEV, 4096)
    o0 = 0
    conv_w_full = sg[:, o0:o0 + 2112].reshape(NDEV, 2, 3, 352).transpose(1, 2, 0, 3).reshape(2, 3, FFN)
    o0 += 2112
    ret_gn_full = sg[:, o0:o0 + 256].reshape(NDEV, RET_H, 64).transpose(1, 0, 2).reshape(RET_H, 1, RET_DV)
    o0 += 256
    mla_norm_full = sg[:, o0:o0 + 128].reshape(1, D)
    o0 += 128
    q_norm_full = sg[:, o0:o0 + 48].reshape(1, MLA_QR)
    o0 += 48
    kv_norm_full = sg[:, o0:o0 + 32].reshape(1, MLA_KVR)

    cw = [conv_w_full[i].reshape(3, 4, FSH).transpose(1, 0, 2) for i in range(2)]
    cb = [ffn_conv_b[i].reshape(4, 1, FSH) for i in range(2)]
    fnorm = [ffn_norm[i].reshape(1, D) for i in range(2)]
    rtabs = _ret_tables(S)
    mtabs = _mla_tables(S)
    qh, kh = mla_q_head_norm.reshape(1, MLA_QK), mla_k_head_norm.reshape(1, MLA_QK)
    gains = (qh[:, :MLA_NOPE], _dup(qh[:, MLA_NOPE:]), kh[:, :MLA_NOPE], _dup(kh[:, MLA_NOPE:]))

    x0 = x.reshape(T, D)
    tgt = loss_target.reshape(T, D)
    proj, h0, (Wret_out, Wffn_out0) = _norm_proj(
        "ret_in", x0, ret_norm.reshape(1, D), Wret_in, _bs((1024, 768), lambda m, j: (m, j)), _sds((T, 6144), BF16),
        comm=_Gather([blk16("ret_w_out"), blk16("ffn_w_out", 0)]))
    Wret_out = Wret_out.reshape(RET_H * RET_DV, D)
    Wffn_out0 = Wffn_out0.reshape(4, FSH, D)
    (o_raw, rgt, states), (Wffn_in0,) = _ret_fwd(proj, rtabs, ret_gn_full, B, S, comm=_Gather([blk16("ffn_w_in", 0)]))
    x1 = _dense("ret_out", rgt, Wret_out, NN, F32, res=x0)
    MLA_W = ["mla_w_in", "mla_w_qb", "mla_w_kvb", "mla_w_out"]
    x2, ffn0_saved, got = _ffn_fwd(0, x1, fnorm[0], Wffn_in0, cw[0], cb[0], Wffn_out0, B, S,
                                   comm_in=_Gather([blk16(k) for k in MLA_W]))
    Wmla_in = got[0].reshape(D, MLA_QR + MLA_KVR + MLA_ROPE)
    Wq, Wkv, Wkr = Wmla_in[:, :MLA_QR], Wmla_in[:, MLA_QR:MLA_QR + MLA_KVR], Wmla_in[:, MLA_QR + MLA_KVR:]
    Wqb, Wkvb, Wmla_out = got[1:]

    h2 = _rms_fwd("mla_norm", x2, mla_norm_full)

    c_q, c_kv, k_rope = (_dense(n, h2, wm, NN, F32) for n, wm in
                         (("mla_in_q", Wq), ("mla_in_kv", Wkv), ("mla_in_kr", _dup(Wkr))))
    cqn = _rms_fwd("mla_q_norm", c_q, q_norm_full)
    ckvn = _rms_fwd("mla_kv_norm", c_kv, kv_norm_full)
    Wqb2 = jnp.concatenate([Wqb, Wqb[:, :, MLA_NOPE:]], axis=2).transpose(1, 0, 2).reshape(MLA_QR, MLA_H * MLA_PAD)
    Wkvb2 = Wkvb.transpose(1, 0, 2).reshape(MLA_KVR, MLA_H * (MLA_NOPE + MLA_V))
    Wmla_out2 = Wmla_out.reshape(D, D)
    q_raw = _dense("mla_qb", cqn, Wqb2, NN, F32, tm=1024)
    kvh = _dense("mla_kvb", ckvn, Wkvb2, NN, F32, tm=1024)
    (att, lse, qf, kf), (Wffn_in1, Wffn_out1) = _mla_fwd(
        q_raw, kvh, k_rope, gains, mtabs, B, S, comm=_Gather([blk16("ffn_w_in", 1), blk16("ffn_w_out", 1)]))
    Wffn_out1 = Wffn_out1.reshape(4, FSH, D)
    x3 = _dense("mla_out", att, Wmla_out2, NN, F32, res=x2)
    y, ffn1_saved, _ = _ffn_fwd(1, x3, fnorm[1], Wffn_in1, cw[1], cb[1], Wffn_out1, B, S)

    dy, colsq, dyb = _loss(y, tgt)
    loss_part = 0.5 * jnp.sum(colsq) / D

    dx3, dx3b, gf1, r_ffn1_out, _, _ = _ffn_bwd(1, dy, dyb, ffn1_saved, fnorm[1], Wffn_in1, cw[1], Wffn_out1, B, S,
                                                first_half_early=False)
    datt = _dense("mla_out_dx", dx3b, Wmla_out2, NT, BF16)
    fh = FSH // 2
    (dq_raw, dkvh, dkr, dqgn, dqgr, dkgn, dkgr), (r_ffn1_in_a, r_ffn1_in_b) = _mla_bwd(
        q_raw, kvh, k_rope, att, lse, datt, qf, kf, gains, mtabs, B, S,
        comm=_Exchange([gf1["w_in"], gf1["w_in"]], rows=[(0, fh), (fh, fh)]))
    dcqn = _dense("mla_qb_dx", dq_raw, Wqb2, NT, F32, tm=1024)
    dckvn = _dense("mla_kvb_dx", dkvh, Wkvb2, NT, F32, tm=1024)
    dcq, dg_qn = _rms_bwd("mla_q_norm_bwd", c_q, q_norm_full, dcqn)
    dckv, dg_kvn = _rms_bwd("mla_kv_norm_bwd", c_kv, kv_norm_full, dckvn)
    dqgr, dkgr = _fold(dqgr), _fold(dkgr)
    dproj2 = _bf16(jnp.concatenate([dcq, dckv, _fold(dkr)], axis=-1))
    dh2 = _dense("mla_in_dx", dproj2, Wmla_in, NT, BF16)
    dx2, dg_mla_norm, dx2b = _rms_bwd("mla_norm_bwd", x2, mla_norm_full, dh2, dres=dx3, also_bf16=True)
    dWmla_out = _dense("mla_out_dw", att, dx3b, TN, BF16, tm=1024).reshape(NDEV, MLA_V, D)
    dWqb = _dense("mla_qb_dw", dq_raw, cqn, TN, BF16, tm=1024).reshape(MLA_H, MLA_PAD, MLA_QR)
    dWqb = jnp.concatenate([dWqb[:, :MLA_NOPE], dWqb[:, MLA_NOPE:MLA_QK] + dWqb[:, MLA_QK:]], axis=1)
    dWkvb = _dense("mla_kvb_dw", ckvn, dkvh, TN, BF16, tm=1024)
    dWkvb = dWkvb.reshape(MLA_KVR, MLA_H, MLA_NOPE + MLA_V).transpose(1, 0, 2)
    dWmla_in = _dense("mla_in_dw", h2, dproj2, TN, BF16).reshape(NDEV, 128, 704)

    dx1, dx1b, gf0, r_ffn0_out, r_ffn0_in_a, (r_mla_a, r_mla_b) = _ffn_bwd(
        0, dx2, dx2b, ffn0_saved, fnorm[0], Wffn_in0, cw[0], Wffn_out0, B, S, first_half_early=True,
        riders=(_Exchange([dWmla_out, dWqb]), _Exchange([dWkvb, dWmla_in])))
    r_mla = [*r_mla_a, *r_mla_b]
    drgt = _dense("ret_out_dx", dx1b, Wret_out, NT, BF16)
    dWret_out = _dense("ret_out_dw", rgt, dx1b, TN, BF16, tm=1024).reshape(NDEV, 256, D)
    (dproj, dgn_ret), (r_ffn0_in_b, r_ret_out) = _ret_bwd(
        proj, o_raw, states, drgt, rtabs, ret_gn_full, B, S,
        comm=_Exchange([gf0["w_in"], dWret_out], rows=[(fh, fh), None]))

    tt, hk, er = min(2048, T), D // 2, D // 8

    def eighths(arr, *which):
        return _Exchange([arr] * len(which), rows=[(w * er, er) for w in which])

    def ret_in_dw(name, half, comm):
        return _mm(name, h0, dproj, grid=(NDEV, T // tt), a_spec=_bs((tt, hk), lambda j, t: (t, half)),
                   b_spec=_bs((tt, 768), lambda j, t: (t, j)), o_spec=_bs((None, hk, 768), lambda j, t: (j, 0, 0)),
                   out_shape=_sds((NDEV, hk, 768), BF16), dims=TN, kax=1, acc_shape=(hk, 768), comm=comm)

    dW_top = ret_in_dw("ret_in_dw_top", 0, None)
    dW_bot, r_e01 = ret_in_dw("ret_in_dw_bot", 1, eighths(dW_top, 0, 1))
    both = _Exchange([dW_top, dW_top, dW_bot, dW_bot, dW_bot], rows=[(w * er, er) for w in (2, 3, 0, 1, 2)])
    dh0, r_e23456 = _mm(
        "ret_in_dx", dproj, Wret_in, grid=(T // 1024, NDEV), a_spec=_bs((1024, 768), lambda m, k: (m, k)),
        b_spec=_bs((None, D, 768), lambda m, k: (k, 0, 0)), o_spec=_bs((1024, D), lambda m, k: (m, 0)),
        out_shape=_sds((T, D), BF16), dims=NT, kax=1, acc_shape=(1024, D), comm=both)
    (dx0, dg_ret_norm), r_e7 = _rms_bwd("ret_norm_bwd", x0, ret_norm.reshape(1, D), dh0, dres=dx1,
                                        comm=eighths(dW_bot, 3))
    grad_x = dx0.reshape(B, S, D)
    received = dict(ret_w_in=[*r_e01, *r_e23456, *r_e7], ret_w_out=[r_ret_out], mla_w_out=[r_mla[0]], mla_w_qb=[r_mla[1]],
                    mla_w_kvb=[r_mla[2]], mla_w_in=[r_mla[3]],
                    ffn_w_in=[r_ffn0_in_a, r_ffn0_in_b, r_ffn1_in_a, r_ffn1_in_b], ffn_w_out=[r_ffn0_out, r_ffn1_out])

    dconv_w = jnp.stack([g_["cw"].transpose(1, 0, 2).reshape(3, FFN) for g_ in (gf0, gf1)])
    dconv_b = jnp.stack([g_["cb"].reshape(FFN) for g_ in (gf0, gf1)])
    small_parts = [dg_ret_norm, gf0["norm"], gf1["norm"], dg_mla_norm, dg_qn, dg_kvn, dqgn, dqgr, dkgn, dkgr, dgn_ret,
                   dconv_w, dconv_b, loss_part]
    small_g = jnp.concatenate([p.reshape(-1) for p in small_parts])
    n_grads = small_g.shape[0] - 1
    small_g = jnp.pad(small_g, (0, 240 * 128 - small_g.shape[0])).reshape(240, 128)
    small_all = _comm_call("gather_small_grads", _Gather([small_g]))[0]
    sred = _sum8("sum_small_grads", small_all).reshape(-1)
    loss = sred[n_grads]

    def take(n):
        nonlocal off
        out = sred[off:off + n]
        off += n
        return out

    off = 0
    g_small = dict(ret_norm=take(D).reshape(1, D), ffn_norm=take(2 * D).reshape(2, D), mla_norm=take(D),
                   mla_q_norm=take(MLA_QR), mla_kv_norm=take(MLA_KVR))
    g_small["mla_q_head_norm"] = take(MLA_QK).reshape(1, MLA_QK)
    g_small["mla_k_head_norm"] = take(MLA_QK).reshape(1, MLA_QK)
    g_small["ret_gn"] = take(RET_H * RET_DV).reshape(1, RET_H, RET_DV)
    g_small["ffn_conv_w"] = take(2 * 3 * FFN).reshape(2, 3, FFN)
    g_small["ffn_conv_b"] = take(2 * FFN).reshape(2, FFN)
    g_small["mla_norm"] = lax.dynamic_slice(g_small["mla_norm"], (dev * 128,), (128,)).reshape(1, 128)
    g_small["mla_q_norm"] = lax.dynamic_slice(g_small["mla_q_norm"], (dev * 48,), (48,)).reshape(1, 48)
    g_small["mla_kv_norm"] = lax.dynamic_slice(g_small["mla_kv_norm"], (dev * 32,), (32,)).reshape(1, 32)
    g_small["ret_gn"] = lax.dynamic_slice(g_small["ret_gn"], (0, 0, dev * 64), (1, RET_H, 64))
    g_small["ffn_conv_w"] = lax.dynamic_slice(g_small["ffn_conv_w"], (0, 0, dev * 352), (2, 3, 352))

    grads, delta, new_m, new_v = {}, {}, {}, {}
    for k in BIG:
        rcs = received[k]
        tr = max(t for t in range(16, 257, 16) if rcs[0].shape[1] % t == 0)
        flip = (lambda t: t.transpose(0, 2, 1)) if k in ("ffn_w_in", "mla_w_qb") else (lambda t: t)
        res = _adamw(f"adamw_{k}", rcs, flip(w[k]), flip(mom[k]), flip(var[k]), tr=tr)
        grads[k], delta[k], new_m[k], new_v[k] = (flip(t) for t in res)
    SMALL = REPL + SHARDED_SMALL

    def pack(d):
        vflat = jnp.concatenate([d[k].reshape(-1) for k in SMALL])
        return jnp.pad(vflat, (0, 96 * 128 - vflat.shape[0])).reshape(1, 96, 128)

    ps = _adamw("adamw_small", [pack(g_small)], pack(w), pack(mom), pack(var))
    off = 0
    for k in SMALL:
        n = w[k].size
        grads[k], delta[k], new_m[k], new_v[k] = (t.reshape(-1)[off:off + n].reshape(w[k].shape) for t in ps)
        off += n
    names = list(w)
    return (loss, grad_x, *[grads[k] for k in names], *[delta[k] for k in names], *[new_m[k] for k in names],
            *[new_v[k] for k in names])
```

```python
import functools

import jax
import jax.numpy as jnp
from jax import lax
from jax.experimental import pallas as pl
from jax.experimental.pallas import tpu as pltpu

F32, BF16 = jnp.float32, jnp.bfloat16

NDEV = 8
D_MODEL = 1024
CHUNK = 64
RMS_EPS = 1e-6
ROPE_THETA = 10000.0
RET_H, RET_DK, RET_DV = 4, 256, 512
RET_SC = 256
MLA_H, MLA_QR, MLA_KVR = 8, 384, 256
MLA_NOPE, MLA_ROPE, MLA_V = 128, 64, 128
MLA_QK = MLA_NOPE + MLA_ROPE
MASK_VALUE = -1e30
FFN = 2816
FSH = FFN * 2 // NDEV
ATT_TQ = 256
ADAM_LR, ADAM_B1, ADAM_B2, ADAM_EPS, ADAM_WD, ADAM_STEP = 0.001, 0.9, 0.999, 1e-08, 0.01, 10
MESH = pl.DeviceIdType.MESH
VMEM_LIMIT = 56 * 2 ** 20


def _cp(sem):
    return pltpu.CompilerParams(dimension_semantics=sem, vmem_limit_bytes=VMEM_LIMIT)


def _dot(a, b, dims):
    return lax.dot_general(a, b, (dims, ((), ())), preferred_element_type=F32)


NN = ((1,), (0,))
NT = ((1,), (1,))
TN = ((0,), (0,))


def _place():
    return lax.axis_index("x"), lax.axis_index("y"), lax.axis_index("c")


def _idx(d):
    return 4 * d[0] + 2 * d[1] + d[2]


ANY = pl.BlockSpec(memory_space=pl.ANY)


class _Gather:
    def __init__(self, arrs, parts=1):
        self.srcs = list(arrs)
        self.parts = parts
        self.nsem = len(arrs) * parts
        self.out_shape = [jax.ShapeDtypeStruct((NDEV,) + a.shape, a.dtype) for a in arrs]

    def _copies(self, ins, outs, send, recv, loc):
        n = self.nsem
        x, y, c = _place()
        me, sib = (x, y, c), (x, y, 1 - c)
        chips = [(1 - x, y), (x, 1 - y), (1 - x, 1 - y)]

        def piece(ref, v, *lead):
            a, p = divmod(v, self.parts)
            if self.parts > 1:
                rows = self.srcs[a].shape[0] // self.parts
                lead = (*lead, pl.ds(p * rows, rows))
            return ref[a].at[lead] if lead else ref[a]

        def cp(a, k, block, to, src=None):
            dst = piece(outs, a, _idx(block))
            return pltpu.make_async_remote_copy(src_ref=dst if src is None else src, dst_ref=dst,
                                                send_sem=send.at[a, k], recv_sem=recv.at[a, k], device_id=to,
                                                device_id_type=MESH)

        own = [piece(ins, a) for a in range(n)]
        mine = [pltpu.make_async_copy(own[a], piece(outs, a, _idx(me)), loc.at[a]) for a in range(n)]
        first = [cp(a, 0, me, sib, src=own[a]) for a in range(n)]
        first += [cp(a, 1 + j, me, (*chip, c), src=own[a]) for a in range(n) for j, chip in enumerate(chips)]
        landed = [cp(a, 1 + j, (*chip, c), me) for j, chip in enumerate(chips) for a in range(n)]
        passed = [cp(a, 4 + j, (*chip, c), sib) for j, chip in enumerate(chips) for a in range(n)]
        from_sib = [cp(a, 0, sib, me) for a in range(n)]
        from_sib += [cp(a, 4 + j, (*chip, 1 - c), me) for j, chip in enumerate(chips) for a in range(n)]
        return mine, first, landed, passed, from_sib

    def start(self, *refs):
        mine, first, _, _, _ = self._copies(*refs)
        for cp in mine + first:
            cp.start()

    def mid(self, *refs):
        _, _, landed, passed, _ = self._copies(*refs)
        for got, on in zip(landed, passed):
            got.wait_recv()
            on.start()

    def finish(self, *refs):
        mine, first, _, passed, from_sib = self._copies(*refs)
        for cp in from_sib:
            cp.wait_recv()
        for cp in first + passed:
            cp.wait_send()
        for cp in mine:
            cp.wait()


class _Exchange:
    def __init__(self, arrs, rows=None):
        self.srcs = list(arrs)
        self.nsem = len(arrs)
        self.rows = rows if rows is not None else [None] * len(arrs)
        self.out_shape = [jax.ShapeDtypeStruct(a.shape if r is None else (a.shape[0], r[1]) + a.shape[2:], a.dtype)
                          for a, r in zip(arrs, self.rows)]

    def _copies(self, ins, outs, send, recv, loc):
        n = len(self.srcs)
        x, y, c = _place()
        me = _idx((x, y, c))

        def src(a, q):
            r = self.rows[a]
            return ins[a].at[q] if r is None else ins[a].at[q, pl.ds(r[0], r[1])]

        mine = [pltpu.make_async_copy(src(a, me), outs[a].at[me], loc.at[a]) for a in range(n)]
        remote = []
        for k in range(1, NDEV):
            peer = (x ^ (k >> 2), y ^ ((k >> 1) & 1), c ^ (k & 1))
            remote += [pltpu.make_async_remote_copy(
                src_ref=src(a, _idx(peer)), dst_ref=outs[a].at[me], send_sem=send.at[a, k - 1],
                recv_sem=recv.at[a, k - 1], device_id=peer, device_id_type=MESH) for a in range(n)]
        return mine, remote

    def start(self, *refs):
        mine, remote = self._copies(*refs)
        for cp in mine + remote:
            cp.start()

    def mid(self, *refs):
        pass

    def finish(self, *refs):
        mine, remote = self._copies(*refs)
        for cp in remote + mine:
            cp.wait()


def _comm_scratch(n):
    return [pltpu.SemaphoreType.DMA((n, 7)), pltpu.SemaphoreType.DMA((n, 7)), pltpu.SemaphoreType.DMA((n,))]


def _comm_call(name, comm):
    n = len(comm.srcs)

    def body(*refs):
        parts = (refs[:n], refs[n:2 * n]) + tuple(refs[2 * n:])
        comm.start(*parts)
        comm.mid(*parts)
        comm.finish(*parts)

    return pl.pallas_call(body, name=name, in_specs=[ANY] * n, out_specs=[ANY] * n, out_shape=comm.out_shape,
                          scratch_shapes=_comm_scratch(comm.nsem))(*comm.srcs)


def _pcall(body, *, name, grid, in_specs, out_specs, out_shape, scratch_shapes, sem, args, comm=None):
    if comm is None:
        return pl.pallas_call(body, name=name, grid=grid, in_specs=in_specs, out_specs=out_specs, out_shape=out_shape,
                              scratch_shapes=scratch_shapes, compiler_params=_cp(sem))(*args), None
    ni, no, ns, nc = len(in_specs), len(out_shape), len(scratch_shapes), len(comm.srcs)
    total = 1
    for g in grid:
        total *= g
    middle = (4 * total) // 5

    def wrapped(*refs):
        ins, csrc = refs[:ni], refs[ni:ni + nc]
        outs, cdst = refs[ni + nc:ni + nc + no], refs[ni + nc + no:ni + 2 * nc + no]
        scr, sems = refs[ni + 2 * nc + no:ni + 2 * nc + no + ns], refs[ni + 2 * nc + no + ns:]
        step = pl.program_id(0)
        for k in range(1, len(grid)):
            step = step * grid[k] + pl.program_id(k)
        parts = (csrc, cdst) + tuple(sems)

        @pl.when(step == 0)
        def _():
            comm.start(*parts)

        body(*ins, *outs, *scr)

        @pl.when(step == middle)
        def _():
            comm.mid(*parts)

        @pl.when(step == total - 1)
        def _():
            comm.finish(*parts)

    res = pl.pallas_call(
        wrapped, name=name, grid=grid, in_specs=list(in_specs) + [ANY] * nc, out_specs=list(out_specs) + [ANY] * nc,
        out_shape=list(out_shape) + comm.out_shape, scratch_shapes=list(scratch_shapes) + _comm_scratch(comm.nsem),
        compiler_params=_cp(("arbitrary",) * len(grid)))(*args, *comm.srcs)
    return res[:no], res[no:]


def _mm(name, a, b, *, grid, a_spec, b_spec, o_spec, out_shape, dims, kax=None, res=None, res_spec=None,
        jb=0, acc_shape=None, comm=None):
    nk = grid[kax] if kax is not None else 1

    def body(*refs):
        if res is not None:
            a_ref, b_ref, r_ref, o_ref = refs[:4]
        else:
            a_ref, b_ref, o_ref = refs[:3]

        def product():
            if not jb:
                return _dot(a_ref[...], b_ref[...], dims)
            part = _dot(a_ref[0], b_ref[0], dims)
            for j in range(1, jb):
                part = part + _dot(a_ref[j], b_ref[j], dims)
            return part

        def fin(acc):
            if res is not None:
                acc = acc + r_ref[...]
            o_ref[...] = acc.astype(o_ref.dtype)

        if nk == 1:
            fin(product())
        else:
            acc_ref = refs[-1]
            k = pl.program_id(kax)

            @pl.when(k == 0)
            def _():
                acc_ref[...] = jnp.zeros_like(acc_ref)

            acc_ref[...] += product()

            @pl.when(k == nk - 1)
            def _():
                fin(acc_ref[...])

    sem = tuple("arbitrary" if i == kax else "parallel" for i in range(len(grid)))
    in_specs = [a_spec, b_spec] + ([res_spec] if res is not None else [])
    args = (a, b) + ((res,) if res is not None else ())
    scratch = [pltpu.VMEM(acc_shape, F32)] if nk > 1 else []
    (out,), got = _pcall(body, name=name, grid=grid, in_specs=in_specs, out_specs=[o_spec], out_shape=[out_shape],
                         scratch_shapes=scratch, sem=sem, args=args, comm=comm)
    return out if comm is None else (out, got)


def _bs(shape, fn):
    return pl.BlockSpec(shape, fn)


def _rms_fwd(name, x, g, tm=512):
    T, D = x.shape

    def body(x_ref, g_ref, o_ref):
        xf = x_ref[...]
        r = lax.rsqrt(jnp.mean(xf * xf, axis=-1, keepdims=True) + RMS_EPS)
        o_ref[...] = (xf * r * g_ref[...]).astype(o_ref.dtype)

    return pl.pallas_call(
        body, name=name, grid=(T // tm,),
        in_specs=[_bs((tm, D), lambda i: (i, 0)), _bs((1, D), lambda i: (0, 0))],
        out_specs=_bs((tm, D), lambda i: (i, 0)), out_shape=jax.ShapeDtypeStruct((T, D), BF16),
        compiler_params=_cp(("parallel",)))(x, g)


def _rms_bwd(name, x, g, dh, dres=None, tm=512, also_bf16=False, comm=None):
    T, D = x.shape

    def body(*refs):
        if also_bf16:
            refs, dxb_ref = refs[:-1], refs[-1]
        if dres is not None:
            x_ref, g_ref, dh_ref, dres_ref, dx_ref, dg_ref = refs
        else:
            x_ref, g_ref, dh_ref, dx_ref, dg_ref = refs
        i = pl.program_id(0)
        xf = x_ref[...]
        r = lax.rsqrt(jnp.mean(xf * xf, axis=-1, keepdims=True) + RMS_EPS)
        xh = xf * r
        d = dh_ref[...].astype(F32)
        dxh = d * g_ref[...]
        dx = r * (dxh - xh * jnp.mean(dxh * xh, axis=-1, keepdims=True))
        if dres is not None:
            dx = dx + dres_ref[...]
        dx_ref[...] = dx
        if also_bf16:
            dxb_ref[...] = dx.astype(BF16)
        part = jnp.sum(d * xh, axis=0, keepdims=True)

        @pl.when(i == 0)
        def _():
            dg_ref[...] = part

        @pl.when(i > 0)
        def _():
            dg_ref[...] += part

    row = _bs((tm, D), lambda i: (i, 0))
    vec = _bs((1, D), lambda i: (0, 0))
    in_specs = [row, vec, row] + ([row] if dres is not None else [])
    args = (x, g, dh) + ((dres,) if dres is not None else ())
    extra = [jax.ShapeDtypeStruct((T, D), BF16)] if also_bf16 else []
    outs, got = _pcall(
        body, name=name, grid=(T // tm,), in_specs=in_specs, out_specs=[row, vec] + [row] * len(extra),
        out_shape=[jax.ShapeDtypeStruct((T, D), F32), jax.ShapeDtypeStruct((1, D), F32)] + extra, scratch_shapes=[],
        sem=("arbitrary",), args=args, comm=comm)
    return outs if comm is None else (outs, got)


def _out_proj_loss(name, a, w, res, tgt, tm=512):
    J, T, k = a.shape
    N = w.shape[2]

    def body(a_ref, w_ref, r_ref, t_ref, dy_ref, s_ref, dyb_ref):
        i = pl.program_id(0)
        y = _dot(a_ref[0], w_ref[0], NN)
        for j in range(1, J):
            y = y + _dot(a_ref[j], w_ref[j], NN)
        e = (y + r_ref[...]) - t_ref[...]
        dy = e * (1.0 / N)
        dy_ref[...] = dy
        dyb_ref[...] = dy.astype(BF16)
        part = jnp.sum(e * e, axis=0, keepdims=True)

        @pl.when(i == 0)
        def _():
            s_ref[...] = part

        @pl.when(i > 0)
        def _():
            s_ref[...] += part

    row = _bs((tm, N), lambda i: (i, 0))
    return pl.pallas_call(
        body, name=name, grid=(T // tm,),
        in_specs=[_bs((J, tm, k), lambda i: (0, i, 0)), _bs((J, k, N), lambda i: (0, 0, 0)), row, row],
        out_specs=[row, _bs((1, N), lambda i: (0, 0)), row],
        out_shape=[jax.ShapeDtypeStruct((T, N), F32), jax.ShapeDtypeStruct((1, N), F32),
                   jax.ShapeDtypeStruct((T, N), BF16)],
        compiler_params=_cp(("arbitrary",)))(a, w, res, tgt)


def _shift_rows(t, k, row):
    return jnp.where(row >= k, pltpu.roll(t, k, 0), 0.0)


def _shift_rows_up(t, k, row, n):
    return jnp.where(row < n - k, pltpu.roll(t, n - k, 0), 0.0)


def _convffn_fwd(name, u, cw, cb, B, S):
    _, J, T, F = u.shape

    def body(u_ref, cw_ref, cb_ref, o_ref, gc_ref):
        a = u_ref[0].astype(F32)
        g = u_ref[1].astype(F32)
        row = lax.broadcasted_iota(jnp.int32, (S, F), 0)
        w0, w1, w2 = cw_ref[0:1, :], cw_ref[1:2, :], cw_ref[2:3, :]
        gc = _shift_rows(g, 2, row) * w0 + _shift_rows(g, 1, row) * w1 + g * w2 + cb_ref[...]
        gc_ref[...] = gc.astype(gc_ref.dtype)
        o_ref[...] = (gc * jax.nn.sigmoid(gc) * a).astype(o_ref.dtype)

    blk = _bs((None, S, F), lambda j, b: (j, b, 0))
    return pl.pallas_call(
        body, name=name, grid=(J, B),
        in_specs=[_bs((2, None, S, F), lambda j, b: (0, j, b, 0)), _bs((None, 3, F), lambda j, b: (j, 0, 0)),
                  _bs((None, 1, F), lambda j, b: (j, 0, 0))],
        out_specs=[blk, blk], out_shape=[jax.ShapeDtypeStruct((J, T, F), BF16)] * 2,
        compiler_params=_cp(("parallel", "parallel")))(u, cw, cb)


def _convffn_bwd(name, u, gcb, cw, dgt, B, S, comm=None):
    _, J, T, F = u.shape

    def body(u_ref, gc_ref, cw_ref, d_ref, du_ref, dcw_ref, dcb_ref):
        b = pl.program_id(1)
        a = u_ref[0].astype(F32)
        g = u_ref[1].astype(F32)
        gc = gc_ref[...].astype(F32)
        d = d_ref[...].astype(F32)
        row = lax.broadcasted_iota(jnp.int32, (S, F), 0)
        w0, w1, w2 = cw_ref[0:1, :], cw_ref[1:2, :], cw_ref[2:3, :]
        sg = jax.nn.sigmoid(gc)
        du_ref[0] = (d * gc * sg).astype(du_ref.dtype)
        dgc = d * a * (sg * (1.0 + gc * (1.0 - sg)))
        up1, up2 = _shift_rows_up(dgc, 1, row, S), _shift_rows_up(dgc, 2, row, S)
        du_ref[1] = (dgc * w2 + up1 * w1 + up2 * w0).astype(du_ref.dtype)
        parts = [jnp.sum(up2 * g, axis=0, keepdims=True), jnp.sum(up1 * g, axis=0, keepdims=True),
                 jnp.sum(dgc * g, axis=0, keepdims=True)]
        pb = jnp.sum(dgc, axis=0, keepdims=True)

        @pl.when(b == 0)
        def _():
            for k in range(3):
                dcw_ref[k:k + 1, :] = parts[k]
            dcb_ref[...] = pb

        @pl.when(b > 0)
        def _():
            for k in range(3):
                dcw_ref[k:k + 1, :] += parts[k]
            dcb_ref[...] += pb

    uspec = _bs((2, None, S, F), lambda j, b: (0, j, b, 0))
    blk = _bs((None, S, F), lambda j, b: (j, b, 0))
    return _pcall(
        body, name=name, grid=(J, B),
        in_specs=[uspec, blk, _bs((None, 3, F), lambda j, b: (j, 0, 0)), blk],
        out_specs=[uspec, _bs((None, 3, F), lambda j, b: (j, 0, 0)), _bs((None, 1, F), lambda j, b: (j, 0, 0))],
        out_shape=[jax.ShapeDtypeStruct(u.shape, BF16), jax.ShapeDtypeStruct((J, 3, F), F32),
                   jax.ShapeDtypeStruct((J, 1, F), F32)],
        scratch_shapes=[], sem=("parallel", "arbitrary"), args=(u, gcb, cw, dgt), comm=comm)


def _ret_tables(S):
    half = RET_DK // 2
    inv = ROPE_THETA ** (-jnp.arange(half, dtype=F32) / half)
    ang = jnp.arange(S).astype(F32)[:, None] * inv[None, :]
    lg = jnp.log1p(-jnp.exp2(-5.0 - jnp.arange(RET_H, dtype=F32)))
    i = jnp.arange(RET_SC, dtype=F32)
    same_or_earlier = (jnp.floor(i[None, :] / CHUNK) <= jnp.floor(i[:, None] / CHUNK)).astype(F32)
    dm = jnp.exp(lg[:, None, None] * jnp.abs(i[:, None] - i[None, :])) * same_or_earlier[None]
    qd = jnp.exp(lg[:, None] * (i + 1.0))[:, :, None]
    kd = jnp.exp(lg[:, None] * (RET_SC - 1.0 - i))[:, :, None]
    cd = jnp.exp(lg * RET_SC)[:, None, None]
    return jnp.cos(ang), jnp.sin(ang), dm, qd, kd, cd


def _rope_halves(t, cs, sn):
    h = t.shape[-1] // 2
    t1, t2 = t[:, :h], t[:, h:]
    return jnp.concatenate([t1 * cs - t2 * sn, t2 * cs + t1 * sn], axis=-1)


def _unrope_halves(d, cs, sn):
    h = d.shape[-1] // 2
    d1, d2 = d[:, :h], d[:, h:]
    return jnp.concatenate([d1 * cs + d2 * sn, d2 * cs - d1 * sn], axis=-1)


def _ret_specs(nC, order):
    SC = RET_SC

    def sp(shape, fn):
        return _bs(shape, lambda *g: fn(*order(*g)))

    q = sp((SC, RET_DK), lambda b, h, c: (b * nC + c, h))
    k = sp((SC, RET_DK), lambda b, h, c: (b * nC + c, RET_H + h))
    v = sp((SC, RET_DV), lambda b, h, c: (b * nC + c, RET_H + h))
    g = sp((SC, RET_DV), lambda b, h, c: (b * nC + c, 2 * RET_H + h))
    cs = sp((SC, RET_DK // 2), lambda b, h, c: (c, 0))
    dm = sp((None, SC, SC), lambda b, h, c: (h, 0, 0))
    dv = sp((None, SC, 1), lambda b, h, c: (h, 0, 0))
    cd = sp((None, 1, 1), lambda b, h, c: (h, 0, 0))
    gn = sp((None, 1, RET_DV), lambda b, h, c: (h, 0, 0))
    wide = sp((SC, RET_DV), lambda b, h, c: (b * nC + c, h))
    narrow = sp((SC, RET_DK), lambda b, h, c: (b * nC + c, h))
    st = sp((None, None, None, RET_DK, RET_DV), lambda b, h, c: (b, h, c, 0, 0))
    return dict(q=q, k=k, v=v, g=g, cs=cs, dm=dm, dv=dv, cd=cd, gn=gn, wide=wide, narrow=narrow, st=st)


def _ret_fwd(proj, tabs, gn, B, S, comm=None):
    T = B * S
    nC = S // RET_SC
    cos, sin, dm, qd, kd, cd = tabs
    s = _ret_specs(nC, lambda b, h, c: (b, h, c))

    def body(q_ref, k_ref, v_ref, g_ref, cos_ref, sin_ref, dm_ref, qd_ref, kd_ref, cd_ref, gn_ref,
             o_ref, gt_ref, st_ref, state):
        c = pl.program_id(2)

        @pl.when(c == 0)
        def _():
            state[...] = jnp.zeros_like(state)

        cs, sn = cos_ref[...], sin_ref[...]
        qf = _rope_halves(q_ref[...].astype(F32), cs, sn)
        kf = _rope_halves(k_ref[...].astype(F32), cs, sn) * (RET_DK ** -0.5)
        v = v_ref[...]
        p = _dot(qf.astype(BF16), kf.astype(BF16), NT) * dm_ref[...]
        st = state[...]
        stb = st.astype(BF16)
        st_ref[...] = stb
        o = _dot(p.astype(BF16), v, NN) + _dot((qf * qd_ref[...]).astype(BF16), stb, NN)
        state[...] = st * cd_ref[...] + _dot((kf * kd_ref[...]).astype(BF16), v, TN)
        o_ref[...] = o
        r = lax.rsqrt(jnp.mean(o * o, axis=-1, keepdims=True) + RMS_EPS)
        gf = g_ref[...].astype(F32)
        gt_ref[...] = ((o * r * gn_ref[...]) * (gf * jax.nn.sigmoid(gf))).astype(BF16)

    return _pcall(
        body, name="ret_fwd", grid=(B, RET_H, nC),
        in_specs=[s["q"], s["k"], s["v"], s["g"], s["cs"], s["cs"], s["dm"], s["dv"], s["dv"], s["cd"], s["gn"]],
        out_specs=[s["wide"], s["wide"], s["st"]],
        out_shape=[jax.ShapeDtypeStruct((T, RET_H * RET_DV), F32), jax.ShapeDtypeStruct((T, RET_H * RET_DV), BF16),
                   jax.ShapeDtypeStruct((B, RET_H, nC, RET_DK, RET_DV), BF16)],
        scratch_shapes=[pltpu.VMEM((RET_DK, RET_DV), F32)], sem=("parallel", "parallel", "arbitrary"),
        args=(proj, proj, proj, proj, cos, sin, dm, qd, kd, cd, gn), comm=comm)


def _ret_bwd(proj, o_raw, states, dgt, tabs, gn, B, S, comm=None):
    T = B * S
    nC = S // RET_SC
    cos, sin, dm, qd, kd, cd = tabs
    s = _ret_specs(nC, lambda b, c, h: (b, h, nC - 1 - c))

    def body(q_ref, k_ref, v_ref, g_ref, o_ref, st_ref, d_ref, cos_ref, sin_ref, dm_ref, qd_ref, kd_ref, cd_ref,
             gn_ref, dproj_ref, dgn_ref, dstates):
        b, c, h = pl.program_id(0), pl.program_id(1), pl.program_id(2)
        dstate = dstates.at[h]

        @pl.when(c == 0)
        def _():
            dstate[...] = jnp.zeros_like(dstate)

        @pl.when((b == 0) & (c == 0))
        def _():
            dgn_ref[h] = jnp.zeros((1, RET_DV), F32)

        cs, sn = cos_ref[...], sin_ref[...]
        qf = _rope_halves(q_ref[...].astype(F32), cs, sn)
        kf = _rope_halves(k_ref[...].astype(F32), cs, sn) * (RET_DK ** -0.5)
        v = v_ref[...]
        gnv = gn_ref[h]
        o = o_ref[...]
        r = lax.rsqrt(jnp.mean(o * o, axis=-1, keepdims=True) + RMS_EPS)
        oh = o * r
        gf = g_ref[...].astype(F32)
        sg = jax.nn.sigmoid(gf)
        d = d_ref[...].astype(F32)
        dg = (d * (oh * gnv) * (sg * (1.0 + gf * (1.0 - sg)))).astype(BF16)
        don = d * (gf * sg)
        dgn_ref[h] += jnp.sum(don * oh, axis=0, keepdims=True)
        doh = don * gnv
        dO = (r * (doh - oh * jnp.mean(doh * oh, axis=-1, keepdims=True))).astype(BF16)
        dmv = dm_ref[h]
        qb, kb = qf.astype(BF16), kf.astype(BF16)
        p = (_dot(qb, kb, NT) * dmv).astype(BF16)
        dp = (_dot(dO, v, NT) * dmv).astype(BF16)
        st = st_ref[...]
        dsn = dstate[...]
        dsb = dsn.astype(BF16)
        qdv, kdv = qd_ref[h], kd_ref[h]
        dq = _dot(dp, kb, NN) + _dot(dO, st, NT) * qdv
        dk = _dot(dp, qb, TN) + _dot(v, dsb, NT) * kdv
        dv = _dot(p, dO, TN) + _dot((kf * kdv).astype(BF16), dsb, NN)
        dstate[...] = dsn * cd_ref[h] + _dot((qf * qdv).astype(BF16), dO, TN)
        dq = _unrope_halves(dq, cs, sn).astype(BF16)
        dk = (_unrope_halves(dk, cs, sn) * (RET_DK ** -0.5)).astype(BF16)
        dv = dv.astype(BF16)
        nq, nv = RET_H * RET_DK, RET_H * RET_DV
        for hh in range(RET_H):
            @pl.when(h == hh)
            def _():
                dproj_ref[:, hh * RET_DK:(hh + 1) * RET_DK] = dq
                dproj_ref[:, nq + hh * RET_DK:nq + (hh + 1) * RET_DK] = dk
                dproj_ref[:, 2 * nq + hh * RET_DV:2 * nq + (hh + 1) * RET_DV] = dv
                dproj_ref[:, 2 * nq + nv + hh * RET_DV:2 * nq + nv + (hh + 1) * RET_DV] = dg

    width = 2 * RET_H * (RET_DK + RET_DV)

    def all_heads(*shape):
        return _bs((RET_H,) + shape, lambda b, c, h: (0,) * (1 + len(shape)))

    return _pcall(
        body, name="ret_bwd", grid=(B, nC, RET_H),
        in_specs=[s["q"], s["k"], s["v"], s["g"], s["wide"], s["st"], s["wide"], s["cs"], s["cs"],
                  all_heads(RET_SC, RET_SC), all_heads(RET_SC, 1), all_heads(RET_SC, 1), all_heads(1, 1),
                  all_heads(1, RET_DV)],
        out_specs=[_bs((RET_SC, width), lambda b, c, h: (b * nC + nC - 1 - c, 0)),
                   _bs((RET_H, 1, RET_DV), lambda b, c, h: (0, 0, 0))],
        out_shape=[jax.ShapeDtypeStruct((T, width), BF16), jax.ShapeDtypeStruct((RET_H, 1, RET_DV), F32)],
        scratch_shapes=[pltpu.VMEM((RET_H, RET_DK, RET_DV), F32)], sem=("arbitrary", "arbitrary", "arbitrary"),
        args=(proj, proj, proj, proj, o_raw, states, dgt, cos, sin, dm, qd, kd, cd, gn), comm=comm)


MLA_PAD = 256
MLA_R2 = 2 * MLA_ROPE


def _dup(t):
    return jnp.concatenate([t, t], axis=-1)


def _fold(t):
    return t[..., :MLA_ROPE] + t[..., MLA_ROPE:]


def _mla_tables(S):
    half = MLA_ROPE // 2
    inv = ROPE_THETA ** (-jnp.arange(half, dtype=F32) / half)
    ang = jnp.arange(S).astype(F32)[:, None] * inv[None, :]
    cos, sin, zero = jnp.cos(ang), jnp.sin(ang), jnp.zeros((S, MLA_ROPE), F32)
    return jnp.concatenate([cos, cos, zero], axis=-1), jnp.concatenate([-sin, sin, zero], axis=-1)


def _head_norm_rope(n, r2, gn, gr2, cos, sin, scale):
    ssq = jnp.sum(n * n, axis=-1, keepdims=True) + 0.5 * jnp.sum(r2 * r2, axis=-1, keepdims=True)
    rstd = lax.rsqrt(ssq * (1.0 / MLA_QK) + RMS_EPS)
    yn = n * rstd * gn
    yr = r2 * rstd * gr2
    z = yr * cos + pltpu.roll(yr, MLA_ROPE // 2, 1) * sin
    if scale != 1.0:
        yn, z = yn * scale, z * scale
    return yn, z


def _head_norm_rope_bwd(dn, dz, n, r2, gn, gr2, cos, sin, scale):
    ssq = jnp.sum(n * n, axis=-1, keepdims=True) + 0.5 * jnp.sum(r2 * r2, axis=-1, keepdims=True)
    rstd = lax.rsqrt(ssq * (1.0 / MLA_QK) + RMS_EPS)
    hn, hr = n * rstd, r2 * rstd
    if scale != 1.0:
        dn, dz = dn * scale, dz * scale
    dyr = dz * cos + pltpu.roll(dz * sin, MLA_R2 - MLA_ROPE // 2, 1)
    dgn = jnp.sum(dn * hn, axis=0, keepdims=True)
    dgr = jnp.sum(dyr * hr, axis=0, keepdims=True)
    dhn, dhr = dn * gn, dyr * gr2
    mt = (jnp.sum(dhn * hn, axis=-1, keepdims=True) + jnp.sum(dhr * hr, axis=-1, keepdims=True)) * (1.0 / MLA_QK)
    return rstd * (dhn - hn * mt), rstd * (dhr - 0.5 * hr * mt), dgn, dgr


def _diag_bias():
    i = jnp.arange(ATT_TQ)
    return jnp.where((i[None, :] // CHUNK) <= (i[:, None] // CHUNK), 0.0, MASK_VALUE).astype(F32)


def _store_pair(dst, rows, n, r2):
    dst[rows, :MLA_NOPE] = n.astype(BF16)
    dst[rows, MLA_NOPE:] = r2.astype(BF16)


def _mla_fwd(q_raw, kv, kr, gains, tabs, B, S, comm=None):
    T = B * S
    TQ = ATT_TQ
    nQ = S // TQ
    qgn, qgr, kgn, kgr = gains
    cos, sin = tabs
    scale = MLA_QK ** -0.5

    def body(q_ref, kv_ref, kr_ref, qgn_ref, qgr_ref, kgn_ref, kgr_ref, c_ref, s_ref, bias_ref,
             o_ref, lse_ref, qf_s, kf_s, v_s):
        def prep(t, _):
            rows = pl.ds(pl.multiple_of(t * TQ, TQ), TQ)
            cs, sn = c_ref[rows, :], s_ref[rows, :]
            qn, qr = _head_norm_rope(q_ref[rows, :MLA_NOPE], q_ref[rows, MLA_NOPE:], qgn_ref[...], qgr_ref[...],
                                     cs, sn, scale)
            _store_pair(qf_s, rows, qn, qr)
            kn, krr = _head_norm_rope(kv_ref[rows, :MLA_NOPE], kr_ref[rows, :], kgn_ref[...], kgr_ref[...], cs, sn, 1.0)
            _store_pair(kf_s, rows, kn, krr)
            v_s[rows, :] = kv_ref[rows, MLA_NOPE:].astype(BF16)
            return 0

        lax.fori_loop(0, nQ, prep, 0, unroll=2)
        for i in range(nQ):
            rows = slice(i * TQ, (i + 1) * TQ)
            q = qf_s[rows, :]
            sd = _dot(q, kf_s[rows, :], NT) + bias_ref[...]
            m = jnp.max(sd, axis=-1, keepdims=True)
            if i:
                sl = _dot(q, kf_s[:i * TQ, :], NT)
                m = jnp.maximum(m, jnp.max(sl, axis=-1, keepdims=True))
            pd = jnp.exp(sd - m)
            l = jnp.sum(pd, axis=-1, keepdims=True)
            acc = _dot(pd.astype(BF16), v_s[rows, :], NN)
            if i:
                pl_ = jnp.exp(sl - m)
                l = l + jnp.sum(pl_, axis=-1, keepdims=True)
                acc = acc + _dot(pl_.astype(BF16), v_s[:i * TQ, :], NN)
            o_ref[rows, :] = (acc / l).astype(BF16)
            lse_ref[rows, :] = m + jnp.log(l)

    def vec(n):
        return _bs((1, n), lambda b, h: (0, 0))

    def cols(n):
        return _bs((S, n), lambda b, h: (b, h))

    tab = _bs((S, MLA_R2), lambda b, h: (0, 0))
    return _pcall(
        body, name="mla_fwd", grid=(B, MLA_H),
        in_specs=[cols(MLA_PAD), cols(MLA_NOPE + MLA_V), _bs((S, MLA_R2), lambda b, h: (b, 0)),
                  vec(MLA_NOPE), vec(MLA_R2), vec(MLA_NOPE), vec(MLA_R2), tab, tab,
                  _bs((TQ, TQ), lambda b, h: (0, 0))],
        out_specs=[cols(MLA_V), _bs((None, S, 1), lambda b, h: (h, b, 0)), cols(MLA_PAD), cols(MLA_PAD)],
        out_shape=[jax.ShapeDtypeStruct((T, MLA_H * MLA_V), BF16), jax.ShapeDtypeStruct((MLA_H, T, 1), F32),
                   jax.ShapeDtypeStruct((T, MLA_H * MLA_PAD), BF16), jax.ShapeDtypeStruct((T, MLA_H * MLA_PAD), BF16)],
        scratch_shapes=[pltpu.VMEM((S, MLA_V), BF16)],
        sem=("parallel", "parallel"), args=(q_raw, kv, kr, qgn, qgr, kgn, kgr, cos, sin, _diag_bias()), comm=comm)


def _mla_bwd(q_raw, kv, kr, o, lse, do, qf, kf, gains, tabs, B, S, comm=None):
    T = B * S
    TQ = ATT_TQ
    nQ = S // TQ
    qgn, qgr, kgn, kgr = gains
    cos, sin = tabs
    scale = MLA_QK ** -0.5

    def body(q_ref, kv_ref, kr_ref, o_ref, lse_ref, do_ref, qf_s, kf_s, qgn_ref, qgr_ref, kgn_ref, kgr_ref, c_ref, s_ref,
             bias_ref, dq_ref, dkv_ref, dkr_ref, dqgn_ref, dqgr_ref, dkgn_ref, dkgr_ref,
             v_s, dl_s, dq_s, dk_s, dv_s):
        b, h = pl.program_id(0), pl.program_id(1)

        def blk(t):
            return pl.ds(pl.multiple_of(t * TQ, TQ), TQ)

        def prep(t, _):
            rows = blk(t)
            v_s[rows, :] = kv_ref[rows, MLA_NOPE:].astype(BF16)
            dl_s[rows, :] = jnp.sum(do_ref[rows, :].astype(F32) * o_ref[rows, :].astype(F32), axis=-1, keepdims=True)
            dk_s[rows, :] = jnp.zeros((TQ, MLA_PAD), F32)
            dv_s[rows, :] = jnp.zeros((TQ, MLA_V), F32)
            return 0

        lax.fori_loop(0, nQ, prep, 0, unroll=2)

        gqn, gqr = jnp.zeros((1, MLA_NOPE), F32), jnp.zeros((1, MLA_R2), F32)
        for i in range(nQ):
            rows = slice(i * TQ, (i + 1) * TQ)
            q, doi, lse_i, dl_i = qf_s[rows, :], do_ref[rows, :], lse_ref[rows, :], dl_s[rows, :]

            def part(cols, bias):
                k, v = kf_s[cols, :], v_s[cols, :]
                s = _dot(q, k, NT)
                if bias is not None:
                    s = s + bias
                p = jnp.exp(s - lse_i)
                ds = (p * (_dot(doi, v, NT) - dl_i)).astype(BF16)
                dk_s[cols, :] += _dot(ds, q, TN)
                dv_s[cols, :] += _dot(p.astype(BF16), doi, TN)
                return _dot(ds, k, NN)

            dq = part(rows, bias_ref[...])
            if i:
                dq = dq + part(slice(0, i * TQ), None)
            dq_s[...] = dq
            dqn, dqr, a0, a1 = _head_norm_rope_bwd(dq_s[:, :MLA_NOPE], dq_s[:, MLA_NOPE:], q_ref[rows, :MLA_NOPE],
                                                   q_ref[rows, MLA_NOPE:], qgn_ref[...], qgr_ref[...],
                                                   c_ref[rows, :], s_ref[rows, :], scale)
            _store_pair(dq_ref, rows, dqn, dqr)
            gqn, gqr = gqn + a0, gqr + a1

        def post(t, carry):
            rows = blk(t)
            dkn, dkr, a2, a3 = _head_norm_rope_bwd(dk_s[rows, :MLA_NOPE], dk_s[rows, MLA_NOPE:],
                                                   kv_ref[rows, :MLA_NOPE], kr_ref[rows, :], kgn_ref[...], kgr_ref[...],
                                                   c_ref[rows, :], s_ref[rows, :], 1.0)
            dkv_ref[rows, :MLA_NOPE] = dkn.astype(BF16)
            dkv_ref[rows, MLA_NOPE:] = dv_s[rows, :].astype(BF16)

            @pl.when(h == 0)
            def _():
                dkr_ref[rows, :] = dkr

            @pl.when(h > 0)
            def _():
                dkr_ref[rows, :] += dkr

            return carry[0] + a2, carry[1] + a3

        gkn, gkr = lax.fori_loop(0, nQ, post, (jnp.zeros((1, MLA_NOPE), F32), jnp.zeros((1, MLA_R2), F32)), unroll=2)
        first = (b == 0) & (h == 0)

        @pl.when(first)
        def _():
            dqgn_ref[...] = gqn
            dqgr_ref[...] = gqr
            dkgn_ref[...] = gkn
            dkgr_ref[...] = gkr

        @pl.when(jnp.logical_not(first))
        def _():
            dqgn_ref[...] += gqn
            dqgr_ref[...] += gqr
            dkgn_ref[...] += gkn
            dkgr_ref[...] += gkr

    def vec(n):
        return _bs((1, n), lambda b, h: (0, 0))

    def cols(n):
        return _bs((S, n), lambda b, h: (b, h))

    tab = _bs((S, MLA_R2), lambda b, h: (0, 0))
    return _pcall(
        body, name="mla_bwd", grid=(B, MLA_H),
        in_specs=[cols(MLA_PAD), cols(MLA_NOPE + MLA_V), _bs((S, MLA_R2), lambda b, h: (b, 0)), cols(MLA_V),
                  _bs((None, S, 1), lambda b, h: (h, b, 0)), cols(MLA_V), cols(MLA_PAD), cols(MLA_PAD),
                  vec(MLA_NOPE), vec(MLA_R2), vec(MLA_NOPE), vec(MLA_R2), tab, tab,
                  _bs((TQ, TQ), lambda b, h: (0, 0))],
        out_specs=[cols(MLA_PAD), cols(MLA_NOPE + MLA_V), _bs((S, MLA_R2), lambda b, h: (b, 0)),
                   vec(MLA_NOPE), vec(MLA_R2), vec(MLA_NOPE), vec(MLA_R2)],
        out_shape=[jax.ShapeDtypeStruct((T, MLA_H * MLA_PAD), BF16),
                   jax.ShapeDtypeStruct((T, MLA_H * (MLA_NOPE + MLA_V)), BF16),
                   jax.ShapeDtypeStruct((T, MLA_R2), F32), jax.ShapeDtypeStruct((1, MLA_NOPE), F32),
                   jax.ShapeDtypeStruct((1, MLA_R2), F32), jax.ShapeDtypeStruct((1, MLA_NOPE), F32),
                   jax.ShapeDtypeStruct((1, MLA_R2), F32)],
        scratch_shapes=[pltpu.VMEM((S, MLA_V), BF16), pltpu.VMEM((S, 1), F32), pltpu.VMEM((TQ, MLA_PAD), F32),
                        pltpu.VMEM((S, MLA_PAD), F32), pltpu.VMEM((S, MLA_V), F32)],
        sem=("arbitrary", "arbitrary"),
        args=(q_raw, kv, kr, o, lse, do, qf, kf, qgn, qgr, kgn, kgr, cos, sin, _diag_bias()), comm=comm)


def _adamw(name, recvs, w, m, v, tr=None, comm=None):
    n, R, C = recvs[0].shape
    L = len(recvs)
    Lw, Rw, _ = w.shape
    assert Lw * Rw == L * R and w.shape[2] == C
    tr = R if tr is None else tr
    assert R % tr == 0 and Rw % tr == 0
    per = R // tr
    per_w = Rw // tr
    c1 = 1.0 - ADAM_B1 ** ADAM_STEP
    c2 = 1.0 - ADAM_B2 ** ADAM_STEP

    def body(*refs):
        r_refs = refs[:L]
        w_ref, m_ref, v_ref, g_ref, d_ref, nm_ref, nv_ref = refs[L:]
        layer = pl.program_id(0) // per

        def total(r_ref):
            t = r_ref[0].astype(F32)
            for k in range(1, n):
                t = t + r_ref[k].astype(F32)
            return t

        g = total(r_refs[0]) if L == 1 else lax.switch(layer, [functools.partial(total, r) for r in r_refs])
        mm = ADAM_B1 * m_ref[...] + (1.0 - ADAM_B1) * g
        vv = ADAM_B2 * v_ref[...] + (1.0 - ADAM_B2) * (g * g)
        g_ref[...] = g
        nm_ref[...] = mm
        nv_ref[...] = vv
        d_ref[...] = -ADAM_LR * ((mm / c1) / (jnp.sqrt(vv / c2) + ADAM_EPS) + ADAM_WD * w_ref[...])

    blk = _bs((None, tr, C), lambda i: (i // per_w, i % per_w, 0))
    r_specs = [_bs((n, tr, C), functools.partial(lambda l, i: (0, jnp.clip(i - l * per, 0, per - 1), 0), l))
               for l in range(L)]
    outs, got = _pcall(body, name=name, grid=(L * per,), in_specs=r_specs + [blk, blk, blk], out_specs=[blk] * 4,
                       out_shape=[jax.ShapeDtypeStruct(w.shape, F32)] * 4, scratch_shapes=[], sem=("arbitrary",),
                       args=(*recvs, w, m, v), comm=comm)
    return outs if comm is None else (outs, got)


def _sum8(name, a):
    n, R, C = a.shape

    def body(a_ref, o_ref):
        s = a_ref[0]
        for k in range(1, n):
            s = s + a_ref[k]
        o_ref[...] = s

    return pl.pallas_call(body, name=name, out_shape=jax.ShapeDtypeStruct((R, C), a.dtype))(a)


def _sds(shape, dt):
    return jax.ShapeDtypeStruct(shape, dt)


def _norm_proj(name, x, g, w, o_spec, out_shape, tm=1024, comm=None):
    T, K = x.shape
    J, _, n = w.shape

    def body(x_ref, g_ref, w_ref, o_ref, h_ref, hs):
        @pl.when(pl.program_id(1) == 0)
        def _():
            xf = x_ref[...]
            r = lax.rsqrt(jnp.mean(xf * xf, axis=-1, keepdims=True) + RMS_EPS)
            h = (xf * r * g_ref[...]).astype(BF16)
            hs[...] = h
            h_ref[...] = h

        o_ref[...] = _dot(hs[...], w_ref[...], NN).astype(o_ref.dtype)

    row = _bs((tm, K), lambda m, j: (m, 0))
    (out, h), got = _pcall(
        body, name=name, grid=(T // tm, J),
        in_specs=[row, _bs((1, K), lambda m, j: (0, 0)), _bs((None, K, n), lambda m, j: (j, 0, 0))],
        out_specs=[o_spec, row], out_shape=[out_shape, _sds((T, K), BF16)], scratch_shapes=[pltpu.VMEM((tm, K), BF16)],
        sem=("parallel", "arbitrary"), args=(x, g, w), comm=comm)
    return out, h, got


def _proj_shared_dx(name, d, w, tm=1024, comm=None):
    J, T, n = d.shape
    K = w.shape[1]
    return _mm(name, d, w, grid=(T // tm, J), a_spec=_bs((None, tm, n), lambda m, k: (k, m, 0)),
               b_spec=_bs((None, K, n), lambda m, k: (k, 0, 0)), o_spec=_bs((tm, K), lambda m, k: (m, 0)),
               out_shape=_sds((T, K), BF16), dims=NT, kax=1, acc_shape=(tm, K), comm=comm)


def _out_proj(name, a, w, res, tm=512):
    J, T, k = a.shape
    N = w.shape[2]
    return _mm(name, a, w, grid=(T // tm,), a_spec=_bs((J, tm, k), lambda m: (0, m, 0)),
               b_spec=_bs((J, k, N), lambda m: (0, 0, 0)), o_spec=_bs((tm, N), lambda m: (m, 0)),
               out_shape=_sds((T, N), F32), dims=NN, res=res, res_spec=_bs((tm, N), lambda m: (m, 0)), jb=J)


def _out_proj_dx(name, dx, w, tm=1024, comm=None):
    T, N = dx.shape
    J, k, _ = w.shape
    return _mm(name, dx, w, grid=(T // tm, J), a_spec=_bs((tm, N), lambda m, j: (m, 0)),
               b_spec=_bs((None, k, N), lambda m, j: (j, 0, 0)), o_spec=_bs((None, tm, k), lambda m, j: (j, m, 0)),
               out_shape=_sds((J, T, k), BF16), dims=NT, comm=comm)


def _out_proj_dw(name, a, dx, tt=1024, comm=None):
    J, T, k = a.shape
    N = dx.shape[1]
    tt = min(tt, T)
    return _mm(name, a, dx, grid=(J, T // tt), a_spec=_bs((None, tt, k), lambda j, t: (j, t, 0)),
               b_spec=_bs((tt, N), lambda j, t: (t, 0)), o_spec=_bs((None, k, N), lambda j, t: (j, 0, 0)),
               out_shape=_sds((J, k, N), BF16), dims=TN, kax=1, acc_shape=(k, N), comm=comm)


def _dense(name, a, b, dims, out_dtype, tm=512, res=None, comm=None):
    if dims == TN:
        T, K = a.shape
        N = b.shape[1]
        return _mm(name, a, b, grid=(T // tm,), a_spec=_bs((tm, K), lambda t: (t, 0)),
                   b_spec=_bs((tm, N), lambda t: (t, 0)), o_spec=_bs((K, N), lambda t: (0, 0)),
                   out_shape=_sds((K, N), out_dtype), dims=TN, kax=0, acc_shape=(K, N), comm=comm)
    M, K = a.shape
    N = b.shape[1] if dims == NN else b.shape[0]
    row = _bs((tm, N), lambda m: (m, 0))
    return _mm(name, a, b, grid=(M // tm,), a_spec=_bs((tm, K), lambda m: (m, 0)), b_spec=_bs(b.shape, lambda m: (0, 0)),
               o_spec=row, out_shape=_sds((M, N), out_dtype), dims=dims, res=res,
               res_spec=row if res is not None else None, comm=comm)


def _bf16(x):
    return x.astype(BF16)


def _ffn_fwd(i, x, norm_g, w_in, cw, cb, w_out, B, S, comm_in=None, tgt=None):
    T = x.shape[0]
    u, h, got = _norm_proj(f"ffn{i}_in", x, norm_g, w_in, _bs((None, 1024, FSH), lambda m, j: (j, m, 0)),
                           _sds((NDEV, T, FSH), BF16), comm=comm_in)
    u4 = u.reshape(2, 4, T, FSH)
    gt, gcb = _convffn_fwd(f"ffn{i}_gate", u4, cw, cb, B, S)
    if tgt is None:
        y = _out_proj(f"ffn{i}_out", gt, w_out, x)
    else:
        y = _out_proj_loss(f"ffn{i}_out_loss", gt, w_out, x, tgt)
    return y, (x, h, u4, gt, gcb), got


def _ffn_bwd(i, dy, dyb, saved, norm_g, w_in, cw, w_out, B, S, first_half_early, riders=(None, None)):
    x, h, u4, gt, gcb = saved
    dgt = _out_proj_dx(f"ffn{i}_out_dx", dyb, w_out, comm=riders[0])
    dw_out = _out_proj_dw(f"ffn{i}_out_dw", gt, dyb, comm=riders[1])
    dgt, got0 = dgt if riders[0] is not None else (dgt, None)
    dw_out, got1 = dw_out if riders[1] is not None else (dw_out, None)
    dw_out = dw_out.reshape(NDEV, FSH // 2, D_MODEL)
    (du4, dcw, dcb), (r_out,) = _convffn_bwd(f"ffn{i}_gate_bwd", u4, gcb, cw, dgt, B, S, comm=_Exchange([dw_out]))
    du = du4.reshape(NDEV, du4.shape[2], FSH)
    dw_in = _out_proj_dw(f"ffn{i}_in_dw", du, h, tt=2048)
    r_in = None
    if first_half_early:
        dh, (r_in,) = _proj_shared_dx(f"ffn{i}_in_dx", du, w_in, comm=_Exchange([dw_in], rows=[(0, FSH // 2)]))
    else:
        dh = _proj_shared_dx(f"ffn{i}_in_dx", du, w_in)
    dx, dgn, dxb = _rms_bwd(f"ffn{i}_norm_bwd", x, norm_g, dh, dres=dy, also_bf16=True)
    return dx, dxb, dict(w_in=dw_in, norm=dgn, cw=dcw, cb=dcb), r_out, r_in, (got0, got1)


def kernel(x, ret_norm, ret_w_in, ret_gn, ret_w_out, mla_norm, mla_w_in, mla_q_norm, mla_w_qb, mla_kv_norm, mla_w_kvb, mla_q_head_norm, mla_k_head_norm, mla_w_out, ffn_norm, ffn_w_in, ffn_conv_w, ffn_conv_b, ffn_w_out, loss_target, m_ret_norm, m_ret_w_in, m_ret_gn, m_ret_w_out, m_mla_norm, m_mla_w_in, m_mla_q_norm, m_mla_w_qb, m_mla_kv_norm, m_mla_w_kvb, m_mla_q_head_norm, m_mla_k_head_norm, m_mla_w_out, m_ffn_norm, m_ffn_w_in, m_ffn_conv_w, m_ffn_conv_b, m_ffn_w_out, v_ret_norm, v_ret_w_in, v_ret_gn, v_ret_w_out, v_mla_norm, v_mla_w_in, v_mla_q_norm, v_mla_w_qb, v_mla_kv_norm, v_mla_w_kvb, v_mla_q_head_norm, v_mla_k_head_norm, v_mla_w_out, v_ffn_norm, v_ffn_w_in, v_ffn_conv_w, v_ffn_conv_b, v_ffn_w_out):
    B, S, D = x.shape
    T = B * S
    w = dict(ret_norm=ret_norm, ret_w_in=ret_w_in, ret_gn=ret_gn, ret_w_out=ret_w_out, mla_norm=mla_norm,
             mla_w_in=mla_w_in, mla_q_norm=mla_q_norm, mla_w_qb=mla_w_qb, mla_kv_norm=mla_kv_norm, mla_w_kvb=mla_w_kvb,
             mla_q_head_norm=mla_q_head_norm, mla_k_head_norm=mla_k_head_norm, mla_w_out=mla_w_out, ffn_norm=ffn_norm,
             ffn_w_in=ffn_w_in, ffn_conv_w=ffn_conv_w, ffn_conv_b=ffn_conv_b, ffn_w_out=ffn_w_out)
    mom = dict(ret_norm=m_ret_norm, ret_w_in=m_ret_w_in, ret_gn=m_ret_gn, ret_w_out=m_ret_w_out, mla_norm=m_mla_norm,
               mla_w_in=m_mla_w_in, mla_q_norm=m_mla_q_norm, mla_w_qb=m_mla_w_qb, mla_kv_norm=m_mla_kv_norm,
               mla_w_kvb=m_mla_w_kvb, mla_q_head_norm=m_mla_q_head_norm, mla_k_head_norm=m_mla_k_head_norm,
               mla_w_out=m_mla_w_out, ffn_norm=m_ffn_norm, ffn_w_in=m_ffn_w_in, ffn_conv_w=m_ffn_conv_w,
               ffn_conv_b=m_ffn_conv_b, ffn_w_out=m_ffn_w_out)
    var = dict(ret_norm=v_ret_norm, ret_w_in=v_ret_w_in, ret_gn=v_ret_gn, ret_w_out=v_ret_w_out, mla_norm=v_mla_norm,
               mla_w_in=v_mla_w_in, mla_q_norm=v_mla_q_norm, mla_w_qb=v_mla_w_qb, mla_kv_norm=v_mla_kv_norm,
               mla_w_kvb=v_mla_w_kvb, mla_q_head_norm=v_mla_q_head_norm, mla_k_head_norm=v_mla_k_head_norm,
               mla_w_out=v_mla_w_out, ffn_norm=v_ffn_norm, ffn_w_in=v_ffn_w_in, ffn_conv_w=v_ffn_conv_w,
               ffn_conv_b=v_ffn_conv_b, ffn_w_out=v_ffn_w_out)
    BIG = ["ret_w_in", "ret_w_out", "mla_w_in", "mla_w_qb", "mla_w_kvb", "mla_w_out", "ffn_w_in", "ffn_w_out"]
    REPL = ["ret_norm", "ffn_norm", "mla_q_head_norm", "mla_k_head_norm", "ffn_conv_b"]
    SHARDED_SMALL = ["ffn_conv_w", "ret_gn", "mla_norm", "mla_q_norm", "mla_kv_norm"]
    dev = _idx(_place())

    def blk16(k, i=0):
        return _bf16(w[k][i])

    small_vec = jnp.concatenate([w[k].reshape(-1) for k in SHARDED_SMALL])
    n_small = small_vec.shape[0]
    small_vec = jnp.pad(small_vec, (0, 4096 - n_small)).reshape(32, 128)
    Wret_in, sg = _comm_call("gather_ret_w_in", _Gather([blk16("ret_w_in"), small_vec], parts=2))
    sg = sg.reshape(NDEV, 4096)
    o0 = 0
    conv_w_full = sg[:, o0:o0 + 2112].reshape(NDEV, 2, 3, 352).transpose(1, 2, 0, 3).reshape(2, 3, FFN)
    o0 += 2112
    ret_gn_full = sg[:, o0:o0 + 256].reshape(NDEV, RET_H, 64).transpose(1, 0, 2).reshape(RET_H, 1, RET_DV)
    o0 += 256
    mla_norm_full = sg[:, o0:o0 + 128].reshape(1, D)
    o0 += 128
    q_norm_full = sg[:, o0:o0 + 48].reshape(1, MLA_QR)
    o0 += 48
    kv_norm_full = sg[:, o0:o0 + 32].reshape(1, MLA_KVR)

    cw = [conv_w_full[i].reshape(3, 4, FSH).transpose(1, 0, 2) for i in range(2)]
    cb = [ffn_conv_b[i].reshape(4, 1, FSH) for i in range(2)]
    fnorm = [ffn_norm[i].reshape(1, D) for i in range(2)]
    rtabs = _ret_tables(S)
    mtabs = _mla_tables(S)
    qh, kh = mla_q_head_norm.reshape(1, MLA_QK), mla_k_head_norm.reshape(1, MLA_QK)
    gains = (qh[:, :MLA_NOPE], _dup(qh[:, MLA_NOPE:]), kh[:, :MLA_NOPE], _dup(kh[:, MLA_NOPE:]))

    x0 = x.reshape(T, D)
    tgt = loss_target.reshape(T, D)
    proj, h0, (Wret_out, Wffn_out0) = _norm_proj(
        "ret_in", x0, ret_norm.reshape(1, D), Wret_in, _bs((1024, 768), lambda m, j: (m, j)), _sds((T, 6144), BF16),
        comm=_Gather([blk16("ret_w_out"), blk16("ffn_w_out", 0)]))
    Wret_out = Wret_out.reshape(RET_H * RET_DV, D)
    Wffn_out0 = Wffn_out0.reshape(4, FSH, D)
    (o_raw, rgt, states), (Wffn_in0,) = _ret_fwd(proj, rtabs, ret_gn_full, B, S, comm=_Gather([blk16("ffn_w_in", 0)]))
    x1 = _dense("ret_out", rgt, Wret_out, NN, F32, res=x0)
    MLA_W = ["mla_w_in", "mla_w_qb", "mla_w_kvb", "mla_w_out"]
    x2, ffn0_saved, got = _ffn_fwd(0, x1, fnorm[0], Wffn_in0, cw[0], cb[0], Wffn_out0, B, S,
                                   comm_in=_Gather([blk16(k) for k in MLA_W]))
    Wmla_in = got[0].reshape(D, MLA_QR + MLA_KVR + MLA_ROPE)
    Wq, Wkv, Wkr = Wmla_in[:, :MLA_QR], Wmla_in[:, MLA_QR:MLA_QR + MLA_KVR], Wmla_in[:, MLA_QR + MLA_KVR:]
    Wqb, Wkvb, Wmla_out = got[1:]

    h2 = _rms_fwd("mla_norm", x2, mla_norm_full)

    c_all = _dense("mla_in", h2, jnp.concatenate([Wq, Wkv, _dup(Wkr)], axis=1), NN, F32)
    c_q, c_kv, k_rope = c_all[:, :MLA_QR], c_all[:, MLA_QR:MLA_QR + MLA_KVR], c_all[:, MLA_QR + MLA_KVR:]
    cqn = _rms_fwd("mla_q_norm", c_q, q_norm_full)
    ckvn = _rms_fwd("mla_kv_norm", c_kv, kv_norm_full)
    Wqb2 = jnp.concatenate([Wqb, Wqb[:, :, MLA_NOPE:]], axis=2).transpose(1, 0, 2).reshape(MLA_QR, MLA_H * MLA_PAD)
    Wkvb2 = Wkvb.transpose(1, 0, 2).reshape(MLA_KVR, MLA_H * (MLA_NOPE + MLA_V))
    Wmla_out2 = Wmla_out.reshape(D, D)
    q_raw = _dense("mla_qb", cqn, Wqb2, NN, F32, tm=1024)
    kvh = _dense("mla_kvb", ckvn, Wkvb2, NN, F32, tm=1024)
    (att, lse, qf, kf), (Wffn_in1, Wffn_out1) = _mla_fwd(
        q_raw, kvh, k_rope, gains, mtabs, B, S, comm=_Gather([blk16("ffn_w_in", 1), blk16("ffn_w_out", 1)]))
    Wffn_out1 = Wffn_out1.reshape(4, FSH, D)
    x3 = _dense("mla_out", att, Wmla_out2, NN, F32, res=x2)
    (dy, colsq, dyb), ffn1_saved, _ = _ffn_fwd(1, x3, fnorm[1], Wffn_in1, cw[1], cb[1], Wffn_out1, B, S, tgt=tgt)
    loss_part = 0.5 * jnp.sum(colsq) / D

    dx3, dx3b, gf1, r_ffn1_out, _, _ = _ffn_bwd(1, dy, dyb, ffn1_saved, fnorm[1], Wffn_in1, cw[1], Wffn_out1, B, S,
                                                first_half_early=False)
    datt = _dense("mla_out_dx", dx3b, Wmla_out2, NT, BF16)
    fh = FSH // 2
    (dq_raw, dkvh, dkr, dqgn, dqgr, dkgn, dkgr), (r_ffn1_in_a, r_ffn1_in_b) = _mla_bwd(
        q_raw, kvh, k_rope, att, lse, datt, qf, kf, gains, mtabs, B, S,
        comm=_Exchange([gf1["w_in"], gf1["w_in"]], rows=[(0, fh), (fh, fh)]))
    dcqn = _dense("mla_qb_dx", dq_raw, Wqb2, NT, F32, tm=1024)
    dckvn = _dense("mla_kvb_dx", dkvh, Wkvb2, NT, F32, tm=1024)
    dcq, dg_qn = _rms_bwd("mla_q_norm_bwd", c_q, q_norm_full, dcqn)
    dckv, dg_kvn = _rms_bwd("mla_kv_norm_bwd", c_kv, kv_norm_full, dckvn)
    dqgr, dkgr = _fold(dqgr), _fold(dkgr)
    dproj2 = _bf16(jnp.concatenate([dcq, dckv, _fold(dkr)], axis=-1))
    dh2 = _dense("mla_in_dx", dproj2, Wmla_in, NT, BF16)
    dx2, dg_mla_norm, dx2b = _rms_bwd("mla_norm_bwd", x2, mla_norm_full, dh2, dres=dx3, also_bf16=True)
    dWmla_out = _dense("mla_out_dw", att, dx3b, TN, BF16, tm=1024).reshape(NDEV, MLA_V, D)
    dWqb = _dense("mla_qb_dw", dq_raw, cqn, TN, BF16, tm=1024).reshape(MLA_H, MLA_PAD, MLA_QR)
    dWqb = jnp.concatenate([dWqb[:, :MLA_NOPE], dWqb[:, MLA_NOPE:MLA_QK] + dWqb[:, MLA_QK:]], axis=1)
    dWkvb = _dense("mla_kvb_dw", ckvn, dkvh, TN, BF16, tm=1024)
    dWkvb = dWkvb.reshape(MLA_KVR, MLA_H, MLA_NOPE + MLA_V).transpose(1, 0, 2)
    dWmla_in = _dense("mla_in_dw", h2, dproj2, TN, BF16).reshape(NDEV, 128, 704)

    dx1, dx1b, gf0, r_ffn0_out, r_ffn0_in_a, (r_mla_a, r_mla_b) = _ffn_bwd(
        0, dx2, dx2b, ffn0_saved, fnorm[0], Wffn_in0, cw[0], Wffn_out0, B, S, first_half_early=True,
        riders=(_Exchange([dWmla_out, dWqb]), _Exchange([dWkvb, dWmla_in])))
    r_mla = [*r_mla_a, *r_mla_b]
    drgt = _dense("ret_out_dx", dx1b, Wret_out, NT, BF16)
    dWret_out = _dense("ret_out_dw", rgt, dx1b, TN, BF16, tm=1024).reshape(NDEV, 256, D)
    (dproj, dgn_ret), (r_ffn0_in_b, r_ret_out) = _ret_bwd(
        proj, o_raw, states, drgt, rtabs, ret_gn_full, B, S,
        comm=_Exchange([gf0["w_in"], dWret_out], rows=[(fh, fh), None]))

    tt, hk, er = min(2048, T), D // 2, D // 8

    def eighths(arr, *which):
        return _Exchange([arr] * len(which), rows=[(w * er, er) for w in which])

    def ret_in_dw(name, half, comm):
        return _mm(name, h0, dproj, grid=(NDEV, T // tt), a_spec=_bs((tt, hk), lambda j, t: (t, half)),
                   b_spec=_bs((tt, 768), lambda j, t: (t, j)), o_spec=_bs((None, hk, 768), lambda j, t: (j, 0, 0)),
                   out_shape=_sds((NDEV, hk, 768), BF16), dims=TN, kax=1, acc_shape=(hk, 768), comm=comm)

    dW_top = ret_in_dw("ret_in_dw_top", 0, None)
    dW_bot, r_e01 = ret_in_dw("ret_in_dw_bot", 1, eighths(dW_top, 0, 1))
    both = _Exchange([dW_top, dW_top, dW_bot, dW_bot, dW_bot], rows=[(w * er, er) for w in (2, 3, 0, 1, 2)])
    dh0, r_e23456 = _mm(
        "ret_in_dx", dproj, Wret_in, grid=(T // 1024, NDEV), a_spec=_bs((1024, 768), lambda m, k: (m, k)),
        b_spec=_bs((None, D, 768), lambda m, k: (k, 0, 0)), o_spec=_bs((1024, D), lambda m, k: (m, 0)),
        out_shape=_sds((T, D), BF16), dims=NT, kax=1, acc_shape=(1024, D), comm=both)
    (dx0, dg_ret_norm), r_e7 = _rms_bwd("ret_norm_bwd", x0, ret_norm.reshape(1, D), dh0, dres=dx1,
                                        comm=eighths(dW_bot, 3))
    grad_x = dx0.reshape(B, S, D)
    received = dict(ret_w_in=[*r_e01, *r_e23456, *r_e7], ret_w_out=[r_ret_out], mla_w_out=[r_mla[0]], mla_w_qb=[r_mla[1]],
                    mla_w_kvb=[r_mla[2]], mla_w_in=[r_mla[3]],
                    ffn_w_in=[r_ffn0_in_a, r_ffn0_in_b, r_ffn1_in_a, r_ffn1_in_b], ffn_w_out=[r_ffn0_out, r_ffn1_out])

    dconv_w = jnp.stack([g_["cw"].transpose(1, 0, 2).reshape(3, FFN) for g_ in (gf0, gf1)])
    dconv_b = jnp.stack([g_["cb"].reshape(FFN) for g_ in (gf0, gf1)])
    small_parts = [dg_ret_norm, gf0["norm"], gf1["norm"], dg_mla_norm, dg_qn, dg_kvn, dqgn, dqgr, dkgn, dkgr, dgn_ret,
                   dconv_w, dconv_b, loss_part]
    small_g = jnp.concatenate([p.reshape(-1) for p in small_parts])
    n_grads = small_g.shape[0] - 1
    small_g = jnp.pad(small_g, (0, 240 * 128 - small_g.shape[0])).reshape(240, 128)
    small_all = _comm_call("gather_small_grads", _Gather([small_g]))[0]
    sred = _sum8("sum_small_grads", small_all).reshape(-1)
    loss = sred[n_grads]

    def take(n):
        nonlocal off
        out = sred[off:off + n]
        off += n
        return out

    off = 0
    g_small = dict(ret_norm=take(D).reshape(1, D), ffn_norm=take(2 * D).reshape(2, D), mla_norm=take(D),
                   mla_q_norm=take(MLA_QR), mla_kv_norm=take(MLA_KVR))
    g_small["mla_q_head_norm"] = take(MLA_QK).reshape(1, MLA_QK)
    g_small["mla_k_head_norm"] = take(MLA_QK).reshape(1, MLA_QK)
    g_small["ret_gn"] = take(RET_H * RET_DV).reshape(1, RET_H, RET_DV)
    g_small["ffn_conv_w"] = take(2 * 3 * FFN).reshape(2, 3, FFN)
    g_small["ffn_conv_b"] = take(2 * FFN).reshape(2, FFN)
    g_small["mla_norm"] = lax.dynamic_slice(g_small["mla_norm"], (dev * 128,), (128,)).reshape(1, 128)
    g_small["mla_q_norm"] = lax.dynamic_slice(g_small["mla_q_norm"], (dev * 48,), (48,)).reshape(1, 48)
    g_small["mla_kv_norm"] = lax.dynamic_slice(g_small["mla_kv_norm"], (dev * 32,), (32,)).reshape(1, 32)
    g_small["ret_gn"] = lax.dynamic_slice(g_small["ret_gn"], (0, 0, dev * 64), (1, RET_H, 64))
    g_small["ffn_conv_w"] = lax.dynamic_slice(g_small["ffn_conv_w"], (0, 0, dev * 352), (2, 3, 352))

    grads, delta, new_m, new_v = {}, {}, {}, {}
    for k in BIG:
        rcs = received[k]
        tr = max(t for t in range(16, 257, 16) if rcs[0].shape[1] % t == 0)
        flip = (lambda t: t.transpose(0, 2, 1)) if k in ("ffn_w_in", "mla_w_qb") else (lambda t: t)
        res = _adamw(f"adamw_{k}", rcs, flip(w[k]), flip(mom[k]), flip(var[k]), tr=tr)
        grads[k], delta[k], new_m[k], new_v[k] = (flip(t) for t in res)
    SMALL = REPL + SHARDED_SMALL

    def pack(d):
        vflat = jnp.concatenate([d[k].reshape(-1) for k in SMALL])
        return jnp.pad(vflat, (0, 96 * 128 - vflat.shape[0])).reshape(1, 96, 128)

    ps = _adamw("adamw_small", [pack(g_small)], pack(w), pack(mom), pack(var))
    off = 0
    for k in SMALL:
        n = w[k].size
        grads[k], delta[k], new_m[k], new_v[k] = (t.reshape(-1)[off:off + n].reshape(w[k].shape) for t in ps)
        off += n
    names = list(w)
    return (loss, grad_x, *[grads[k] for k in names], *[delta[k] for k in names], *[new_m[k] for k in names],
            *[new_v[k] for k in names])
```

```python
import functools

import jax
import jax.numpy as jnp
from jax import lax
from jax.experimental import pallas as pl
from jax.experimental.pallas import tpu as pltpu

F32, BF16 = jnp.float32, jnp.bfloat16

NDEV = 8
D_MODEL = 1024
CHUNK = 64
RMS_EPS = 1e-6
ROPE_THETA = 10000.0
RET_H, RET_DK, RET_DV = 4, 256, 512
RET_SC = 256
MLA_H, MLA_QR, MLA_KVR = 8, 384, 256
MLA_NOPE, MLA_ROPE, MLA_V = 128, 64, 128
MLA_QK = MLA_NOPE + MLA_ROPE
MASK_VALUE = -1e30
FFN = 2816
FSH = FFN * 2 // NDEV
ATT_TQ = 256
ADAM_LR, ADAM_B1, ADAM_B2, ADAM_EPS, ADAM_WD, ADAM_STEP = 0.001, 0.9, 0.999, 1e-08, 0.01, 10
MESH = pl.DeviceIdType.MESH
VMEM_LIMIT = 56 * 2 ** 20


def _cp(sem):
    return pltpu.CompilerParams(dimension_semantics=sem, vmem_limit_bytes=VMEM_LIMIT)


def _dot(a, b, dims):
    return lax.dot_general(a, b, (dims, ((), ())), preferred_element_type=F32)


NN = ((1,), (0,))
NT = ((1,), (1,))
TN = ((0,), (0,))


def _place():
    return lax.axis_index("x"), lax.axis_index("y"), lax.axis_index("c")


def _idx(d):
    return 4 * d[0] + 2 * d[1] + d[2]


ANY = pl.BlockSpec(memory_space=pl.ANY)


class _Gather:
    def __init__(self, arrs, parts=1):
        self.srcs = list(arrs)
        self.parts = parts
        self.nsem = len(arrs) * parts
        self.out_shape = [jax.ShapeDtypeStruct((NDEV,) + a.shape, a.dtype) for a in arrs]

    def _copies(self, ins, outs, send, recv, loc):
        n = self.nsem
        x, y, c = _place()
        me, sib = (x, y, c), (x, y, 1 - c)
        chips = [(1 - x, y), (x, 1 - y), (1 - x, 1 - y)]

        def piece(ref, v, *lead):
            a, p = divmod(v, self.parts)
            if self.parts > 1:
                rows = self.srcs[a].shape[0] // self.parts
                lead = (*lead, pl.ds(p * rows, rows))
            return ref[a].at[lead] if lead else ref[a]

        def cp(a, k, block, to, src=None):
            dst = piece(outs, a, _idx(block))
            return pltpu.make_async_remote_copy(src_ref=dst if src is None else src, dst_ref=dst,
                                                send_sem=send.at[a, k], recv_sem=recv.at[a, k], device_id=to,
                                                device_id_type=MESH)

        own = [piece(ins, a) for a in range(n)]
        mine = [pltpu.make_async_copy(own[a], piece(outs, a, _idx(me)), loc.at[a]) for a in range(n)]
        first = [cp(a, 0, me, sib, src=own[a]) for a in range(n)]
        first += [cp(a, 1 + j, me, (*chip, c), src=own[a]) for a in range(n) for j, chip in enumerate(chips)]
        landed = [cp(a, 1 + j, (*chip, c), me) for j, chip in enumerate(chips) for a in range(n)]
        passed = [cp(a, 4 + j, (*chip, c), sib) for j, chip in enumerate(chips) for a in range(n)]
        from_sib = [cp(a, 0, sib, me) for a in range(n)]
        from_sib += [cp(a, 4 + j, (*chip, 1 - c), me) for j, chip in enumerate(chips) for a in range(n)]
        return mine, first, landed, passed, from_sib

    def start(self, *refs):
        mine, first, _, _, _ = self._copies(*refs)
        for cp in mine + first:
            cp.start()

    def mid(self, *refs):
        _, _, landed, passed, _ = self._copies(*refs)
        for got, on in zip(landed, passed):
            got.wait_recv()
            on.start()

    def finish(self, *refs):
        mine, first, _, passed, from_sib = self._copies(*refs)
        for cp in from_sib:
            cp.wait_recv()
        for cp in first + passed:
            cp.wait_send()
        for cp in mine:
            cp.wait()


class _Exchange:
    def __init__(self, arrs, rows=None):
        self.srcs = list(arrs)
        self.nsem = len(arrs)
        self.rows = rows if rows is not None else [None] * len(arrs)
        self.out_shape = [jax.ShapeDtypeStruct(a.shape if r is None else (a.shape[0], r[1]) + a.shape[2:], a.dtype)
                          for a, r in zip(arrs, self.rows)]

    def _copies(self, ins, outs, send, recv, loc):
        n = len(self.srcs)
        x, y, c = _place()
        me = _idx((x, y, c))

        def src(a, q):
            r = self.rows[a]
            return ins[a].at[q] if r is None else ins[a].at[q, pl.ds(r[0], r[1])]

        mine = [pltpu.make_async_copy(src(a, me), outs[a].at[me], loc.at[a]) for a in range(n)]
        remote = []
        for k in range(1, NDEV):
            peer = (x ^ (k >> 2), y ^ ((k >> 1) & 1), c ^ (k & 1))
            remote += [pltpu.make_async_remote_copy(
                src_ref=src(a, _idx(peer)), dst_ref=outs[a].at[me], send_sem=send.at[a, k - 1],
                recv_sem=recv.at[a, k - 1], device_id=peer, device_id_type=MESH) for a in range(n)]
        return mine, remote

    def start(self, *refs):
        mine, remote = self._copies(*refs)
        for cp in mine + remote:
            cp.start()

    def mid(self, *refs):
        pass

    def finish(self, *refs):
        mine, remote = self._copies(*refs)
        for cp in remote + mine:
            cp.wait()


def _comm_scratch(n):
    return [pltpu.SemaphoreType.DMA((n, 7)), pltpu.SemaphoreType.DMA((n, 7)), pltpu.SemaphoreType.DMA((n,))]


def _comm_call(name, comm):
    n = len(comm.srcs)

    def body(*refs):
        parts = (refs[:n], refs[n:2 * n]) + tuple(refs[2 * n:])
        comm.start(*parts)
        comm.mid(*parts)
        comm.finish(*parts)

    return pl.pallas_call(body, name=name, in_specs=[ANY] * n, out_specs=[ANY] * n, out_shape=comm.out_shape,
                          scratch_shapes=_comm_scratch(comm.nsem))(*comm.srcs)


def _pcall(body, *, name, grid, in_specs, out_specs, out_shape, scratch_shapes, sem, args, comm=None):
    if comm is None:
        return pl.pallas_call(body, name=name, grid=grid, in_specs=in_specs, out_specs=out_specs, out_shape=out_shape,
                              scratch_shapes=scratch_shapes, compiler_params=_cp(sem))(*args), None
    ni, no, ns, nc = len(in_specs), len(out_shape), len(scratch_shapes), len(comm.srcs)
    total = 1
    for g in grid:
        total *= g
    middle = (4 * total) // 5

    def wrapped(*refs):
        ins, csrc = refs[:ni], refs[ni:ni + nc]
        outs, cdst = refs[ni + nc:ni + nc + no], refs[ni + nc + no:ni + 2 * nc + no]
        scr, sems = refs[ni + 2 * nc + no:ni + 2 * nc + no + ns], refs[ni + 2 * nc + no + ns:]
        step = pl.program_id(0)
        for k in range(1, len(grid)):
            step = step * grid[k] + pl.program_id(k)
        parts = (csrc, cdst) + tuple(sems)

        @pl.when(step == 0)
        def _():
            comm.start(*parts)

        body(*ins, *outs, *scr)

        @pl.when(step == middle)
        def _():
            comm.mid(*parts)

        @pl.when(step == total - 1)
        def _():
            comm.finish(*parts)

    res = pl.pallas_call(
        wrapped, name=name, grid=grid, in_specs=list(in_specs) + [ANY] * nc, out_specs=list(out_specs) + [ANY] * nc,
        out_shape=list(out_shape) + comm.out_shape, scratch_shapes=list(scratch_shapes) + _comm_scratch(comm.nsem),
        compiler_params=_cp(("arbitrary",) * len(grid)))(*args, *comm.srcs)
    return res[:no], res[no:]


def _mm(name, a, b, *, grid, a_spec, b_spec, o_spec, out_shape, dims, kax=None, res=None, res_spec=None,
        jb=0, acc_shape=None, comm=None):
    nk = grid[kax] if kax is not None else 1

    def body(*refs):
        if res is not None:
            a_ref, b_ref, r_ref, o_ref = refs[:4]
        else:
            a_ref, b_ref, o_ref = refs[:3]

        def product():
            if not jb:
                return _dot(a_ref[...], b_ref[...], dims)
            part = _dot(a_ref[0], b_ref[0], dims)
            for j in range(1, jb):
                part = part + _dot(a_ref[j], b_ref[j], dims)
            return part

        def fin(acc):
            if res is not None:
                acc = acc + r_ref[...]
            o_ref[...] = acc.astype(o_ref.dtype)

        if nk == 1:
            fin(product())
        else:
            acc_ref = refs[-1]
            k = pl.program_id(kax)

            @pl.when(k == 0)
            def _():
                acc_ref[...] = jnp.zeros_like(acc_ref)

            acc_ref[...] += product()

            @pl.when(k == nk - 1)
            def _():
                fin(acc_ref[...])

    sem = tuple("arbitrary" if i == kax else "parallel" for i in range(len(grid)))
    in_specs = [a_spec, b_spec] + ([res_spec] if res is not None else [])
    args = (a, b) + ((res,) if res is not None else ())
    scratch = [pltpu.VMEM(acc_shape, F32)] if nk > 1 else []
    (out,), got = _pcall(body, name=name, grid=grid, in_specs=in_specs, out_specs=[o_spec], out_shape=[out_shape],
                         scratch_shapes=scratch, sem=sem, args=args, comm=comm)
    return out if comm is None else (out, got)


def _bs(shape, fn):
    return pl.BlockSpec(shape, fn)


def _rms_bwd(name, x, g, dh, dres=None, tm=512, also_bf16=False, comm=None):
    T, D = x.shape

    def body(*refs):
        if also_bf16:
            refs, dxb_ref = refs[:-1], refs[-1]
        if dres is not None:
            x_ref, g_ref, dh_ref, dres_ref, dx_ref, dg_ref = refs
        else:
            x_ref, g_ref, dh_ref, dx_ref, dg_ref = refs
        i = pl.program_id(0)
        xf = x_ref[...]
        r = lax.rsqrt(jnp.mean(xf * xf, axis=-1, keepdims=True) + RMS_EPS)
        xh = xf * r
        d = dh_ref[...].astype(F32)
        dxh = d * g_ref[...]
        dx = r * (dxh - xh * jnp.mean(dxh * xh, axis=-1, keepdims=True))
        if dres is not None:
            dx = dx + dres_ref[...]
        dx_ref[...] = dx
        if also_bf16:
            dxb_ref[...] = dx.astype(BF16)
        part = jnp.sum(d * xh, axis=0, keepdims=True)

        @pl.when(i == 0)
        def _():
            dg_ref[...] = part

        @pl.when(i > 0)
        def _():
            dg_ref[...] += part

    row = _bs((tm, D), lambda i: (i, 0))
    vec = _bs((1, D), lambda i: (0, 0))
    in_specs = [row, vec, row] + ([row] if dres is not None else [])
    args = (x, g, dh) + ((dres,) if dres is not None else ())
    extra = [jax.ShapeDtypeStruct((T, D), BF16)] if also_bf16 else []
    outs, got = _pcall(
        body, name=name, grid=(T // tm,), in_specs=in_specs, out_specs=[row, vec] + [row] * len(extra),
        out_shape=[jax.ShapeDtypeStruct((T, D), F32), jax.ShapeDtypeStruct((1, D), F32)] + extra, scratch_shapes=[],
        sem=("arbitrary",), args=args, comm=comm)
    return outs if comm is None else (outs, got)


def _out_proj_loss(name, a, w, res, tgt, tm=512):
    J, T, k = a.shape
    N = w.shape[2]

    def body(a_ref, w_ref, r_ref, t_ref, dy_ref, s_ref, dyb_ref):
        i = pl.program_id(0)
        y = _dot(a_ref[0], w_ref[0], NN)
        for j in range(1, J):
            y = y + _dot(a_ref[j], w_ref[j], NN)
        e = (y + r_ref[...]) - t_ref[...]
        dy = e * (1.0 / N)
        dy_ref[...] = dy
        dyb_ref[...] = dy.astype(BF16)
        part = jnp.sum(e * e, axis=0, keepdims=True)

        @pl.when(i == 0)
        def _():
            s_ref[...] = part

        @pl.when(i > 0)
        def _():
            s_ref[...] += part

    row = _bs((tm, N), lambda i: (i, 0))
    return pl.pallas_call(
        body, name=name, grid=(T // tm,),
        in_specs=[_bs((J, tm, k), lambda i: (0, i, 0)), _bs((J, k, N), lambda i: (0, 0, 0)), row, row],
        out_specs=[row, _bs((1, N), lambda i: (0, 0)), row],
        out_shape=[jax.ShapeDtypeStruct((T, N), F32), jax.ShapeDtypeStruct((1, N), F32),
                   jax.ShapeDtypeStruct((T, N), BF16)],
        compiler_params=_cp(("arbitrary",)))(a, w, res, tgt)


def _shift_rows(t, k, row):
    return jnp.where(row >= k, pltpu.roll(t, k, 0), 0.0)


def _shift_rows_up(t, k, row, n):
    return jnp.where(row < n - k, pltpu.roll(t, n - k, 0), 0.0)


def _convffn_fwd(name, u, cw, cb, B, S):
    _, J, T, F = u.shape

    def body(u_ref, cw_ref, cb_ref, o_ref, gc_ref):
        a = u_ref[0].astype(F32)
        g = u_ref[1].astype(F32)
        row = lax.broadcasted_iota(jnp.int32, (S, F), 0)
        w0, w1, w2 = cw_ref[0:1, :], cw_ref[1:2, :], cw_ref[2:3, :]
        gc = _shift_rows(g, 2, row) * w0 + _shift_rows(g, 1, row) * w1 + g * w2 + cb_ref[...]
        gc_ref[...] = gc.astype(gc_ref.dtype)
        o_ref[...] = (gc * jax.nn.sigmoid(gc) * a).astype(o_ref.dtype)

    blk = _bs((None, S, F), lambda j, b: (j, b, 0))
    return pl.pallas_call(
        body, name=name, grid=(J, B),
        in_specs=[_bs((2, None, S, F), lambda j, b: (0, j, b, 0)), _bs((None, 3, F), lambda j, b: (j, 0, 0)),
                  _bs((None, 1, F), lambda j, b: (j, 0, 0))],
        out_specs=[blk, blk], out_shape=[jax.ShapeDtypeStruct((J, T, F), BF16)] * 2,
        compiler_params=_cp(("parallel", "parallel")))(u, cw, cb)


def _convffn_bwd(name, u, gcb, cw, dgt, B, S, comm=None):
    _, J, T, F = u.shape

    def body(u_ref, gc_ref, cw_ref, d_ref, du_ref, dcw_ref, dcb_ref):
        b = pl.program_id(1)
        a = u_ref[0].astype(F32)
        g = u_ref[1].astype(F32)
        gc = gc_ref[...].astype(F32)
        d = d_ref[...].astype(F32)
        row = lax.broadcasted_iota(jnp.int32, (S, F), 0)
        w0, w1, w2 = cw_ref[0:1, :], cw_ref[1:2, :], cw_ref[2:3, :]
        sg = jax.nn.sigmoid(gc)
        du_ref[0] = (d * gc * sg).astype(du_ref.dtype)
        dgc = d * a * (sg * (1.0 + gc * (1.0 - sg)))
        up1, up2 = _shift_rows_up(dgc, 1, row, S), _shift_rows_up(dgc, 2, row, S)
        du_ref[1] = (dgc * w2 + up1 * w1 + up2 * w0).astype(du_ref.dtype)
        parts = [jnp.sum(up2 * g, axis=0, keepdims=True), jnp.sum(up1 * g, axis=0, keepdims=True),
                 jnp.sum(dgc * g, axis=0, keepdims=True)]
        pb = jnp.sum(dgc, axis=0, keepdims=True)

        @pl.when(b == 0)
        def _():
            for k in range(3):
                dcw_ref[k:k + 1, :] = parts[k]
            dcb_ref[...] = pb

        @pl.when(b > 0)
        def _():
            for k in range(3):
                dcw_ref[k:k + 1, :] += parts[k]
            dcb_ref[...] += pb

    uspec = _bs((2, None, S, F), lambda j, b: (0, j, b, 0))
    blk = _bs((None, S, F), lambda j, b: (j, b, 0))
    return _pcall(
        body, name=name, grid=(J, B),
        in_specs=[uspec, blk, _bs((None, 3, F), lambda j, b: (j, 0, 0)), blk],
        out_specs=[uspec, _bs((None, 3, F), lambda j, b: (j, 0, 0)), _bs((None, 1, F), lambda j, b: (j, 0, 0))],
        out_shape=[jax.ShapeDtypeStruct(u.shape, BF16), jax.ShapeDtypeStruct((J, 3, F), F32),
                   jax.ShapeDtypeStruct((J, 1, F), F32)],
        scratch_shapes=[], sem=("parallel", "arbitrary"), args=(u, gcb, cw, dgt), comm=comm)


def _ret_tables(S):
    half = RET_DK // 2
    inv = ROPE_THETA ** (-jnp.arange(half, dtype=F32) / half)
    ang = jnp.arange(S).astype(F32)[:, None] * inv[None, :]
    lg = jnp.log1p(-jnp.exp2(-5.0 - jnp.arange(RET_H, dtype=F32)))
    i = jnp.arange(RET_SC, dtype=F32)
    same_or_earlier = (jnp.floor(i[None, :] / CHUNK) <= jnp.floor(i[:, None] / CHUNK)).astype(F32)
    dm = jnp.exp(lg[:, None, None] * jnp.abs(i[:, None] - i[None, :])) * same_or_earlier[None]
    qd = jnp.exp(lg[:, None] * (i + 1.0))[:, :, None]
    kd = jnp.exp(lg[:, None] * (RET_SC - 1.0 - i))[:, :, None]
    cd = jnp.exp(lg * RET_SC)[:, None, None]
    return jnp.cos(ang), jnp.sin(ang), dm, qd, kd, cd


def _rope_halves(t, cs, sn):
    h = t.shape[-1] // 2
    t1, t2 = t[:, :h], t[:, h:]
    return jnp.concatenate([t1 * cs - t2 * sn, t2 * cs + t1 * sn], axis=-1)


def _unrope_halves(d, cs, sn):
    h = d.shape[-1] // 2
    d1, d2 = d[:, :h], d[:, h:]
    return jnp.concatenate([d1 * cs + d2 * sn, d2 * cs - d1 * sn], axis=-1)


def _ret_specs(nC, order):
    SC = RET_SC

    def sp(shape, fn):
        return _bs(shape, lambda *g: fn(*order(*g)))

    q = sp((SC, RET_DK), lambda b, h, c: (b * nC + c, h))
    k = sp((SC, RET_DK), lambda b, h, c: (b * nC + c, RET_H + h))
    v = sp((SC, RET_DV), lambda b, h, c: (b * nC + c, RET_H + h))
    g = sp((SC, RET_DV), lambda b, h, c: (b * nC + c, 2 * RET_H + h))
    cs = sp((SC, RET_DK // 2), lambda b, h, c: (c, 0))
    dm = sp((None, SC, SC), lambda b, h, c: (h, 0, 0))
    dv = sp((None, SC, 1), lambda b, h, c: (h, 0, 0))
    cd = sp((None, 1, 1), lambda b, h, c: (h, 0, 0))
    gn = sp((None, 1, RET_DV), lambda b, h, c: (h, 0, 0))
    wide = sp((SC, RET_DV), lambda b, h, c: (b * nC + c, h))
    narrow = sp((SC, RET_DK), lambda b, h, c: (b * nC + c, h))
    st = sp((None, None, None, RET_DK, RET_DV), lambda b, h, c: (b, h, c, 0, 0))
    return dict(q=q, k=k, v=v, g=g, cs=cs, dm=dm, dv=dv, cd=cd, gn=gn, wide=wide, narrow=narrow, st=st)


def _ret_fwd(proj, tabs, gn, B, S, comm=None):
    T = B * S
    nC = S // RET_SC
    cos, sin, dm, qd, kd, cd = tabs
    s = _ret_specs(nC, lambda b, h, c: (b, h, c))

    def body(q_ref, k_ref, v_ref, g_ref, cos_ref, sin_ref, dm_ref, qd_ref, kd_ref, cd_ref, gn_ref,
             o_ref, gt_ref, st_ref, state):
        c = pl.program_id(2)

        @pl.when(c == 0)
        def _():
            state[...] = jnp.zeros_like(state)

        cs, sn = cos_ref[...], sin_ref[...]
        qf = _rope_halves(q_ref[...].astype(F32), cs, sn)
        kf = _rope_halves(k_ref[...].astype(F32), cs, sn) * (RET_DK ** -0.5)
        v = v_ref[...]
        p = _dot(qf.astype(BF16), kf.astype(BF16), NT) * dm_ref[...]
        st = state[...]
        stb = st.astype(BF16)
        st_ref[...] = stb
        o = _dot(p.astype(BF16), v, NN) + _dot((qf * qd_ref[...]).astype(BF16), stb, NN)
        state[...] = st * cd_ref[...] + _dot((kf * kd_ref[...]).astype(BF16), v, TN)
        o_ref[...] = o
        r = lax.rsqrt(jnp.mean(o * o, axis=-1, keepdims=True) + RMS_EPS)
        gf = g_ref[...].astype(F32)
        gt_ref[...] = ((o * r * gn_ref[...]) * (gf * jax.nn.sigmoid(gf))).astype(BF16)

    return _pcall(
        body, name="ret_fwd", grid=(B, RET_H, nC),
        in_specs=[s["q"], s["k"], s["v"], s["g"], s["cs"], s["cs"], s["dm"], s["dv"], s["dv"], s["cd"], s["gn"]],
        out_specs=[s["wide"], s["wide"], s["st"]],
        out_shape=[jax.ShapeDtypeStruct((T, RET_H * RET_DV), F32), jax.ShapeDtypeStruct((T, RET_H * RET_DV), BF16),
                   jax.ShapeDtypeStruct((B, RET_H, nC, RET_DK, RET_DV), BF16)],
        scratch_shapes=[pltpu.VMEM((RET_DK, RET_DV), F32)], sem=("parallel", "parallel", "arbitrary"),
        args=(proj, proj, proj, proj, cos, sin, dm, qd, kd, cd, gn), comm=comm)


def _ret_bwd(proj, o_raw, states, dgt, tabs, gn, B, S, comm=None):
    T = B * S
    nC = S // RET_SC
    cos, sin, dm, qd, kd, cd = tabs
    s = _ret_specs(nC, lambda b, c, h: (b, h, nC - 1 - c))

    def body(q_ref, k_ref, v_ref, g_ref, o_ref, st_ref, d_ref, cos_ref, sin_ref, dm_ref, qd_ref, kd_ref, cd_ref,
             gn_ref, dproj_ref, dgn_ref, dstates):
        b, c, h = pl.program_id(0), pl.program_id(1), pl.program_id(2)
        dstate = dstates.at[h]

        @pl.when(c == 0)
        def _():
            dstate[...] = jnp.zeros_like(dstate)

        @pl.when((b == 0) & (c == 0))
        def _():
            dgn_ref[h] = jnp.zeros((1, RET_DV), F32)

        cs, sn = cos_ref[...], sin_ref[...]
        qf = _rope_halves(q_ref[...].astype(F32), cs, sn)
        kf = _rope_halves(k_ref[...].astype(F32), cs, sn) * (RET_DK ** -0.5)
        v = v_ref[...]
        gnv = gn_ref[h]
        o = o_ref[...]
        r = lax.rsqrt(jnp.mean(o * o, axis=-1, keepdims=True) + RMS_EPS)
        oh = o * r
        gf = g_ref[...].astype(F32)
        sg = jax.nn.sigmoid(gf)
        d = d_ref[...].astype(F32)
        dg = (d * (oh * gnv) * (sg * (1.0 + gf * (1.0 - sg)))).astype(BF16)
        don = d * (gf * sg)
        dgn_ref[h] += jnp.sum(don * oh, axis=0, keepdims=True)
        doh = don * gnv
        dO = (r * (doh - oh * jnp.mean(doh * oh, axis=-1, keepdims=True))).astype(BF16)
        dmv = dm_ref[h]
        qb, kb = qf.astype(BF16), kf.astype(BF16)
        p = (_dot(qb, kb, NT) * dmv).astype(BF16)
        dp = (_dot(dO, v, NT) * dmv).astype(BF16)
        st = st_ref[...]
        dsn = dstate[...]
        dsb = dsn.astype(BF16)
        qdv, kdv = qd_ref[h], kd_ref[h]
        dq = _dot(dp, kb, NN) + _dot(dO, st, NT) * qdv
        dk = _dot(dp, qb, TN) + _dot(v, dsb, NT) * kdv
        dv = _dot(p, dO, TN) + _dot((kf * kdv).astype(BF16), dsb, NN)
        dstate[...] = dsn * cd_ref[h] + _dot((qf * qdv).astype(BF16), dO, TN)
        dq = _unrope_halves(dq, cs, sn).astype(BF16)
        dk = (_unrope_halves(dk, cs, sn) * (RET_DK ** -0.5)).astype(BF16)
        dv = dv.astype(BF16)
        nq, nv = RET_H * RET_DK, RET_H * RET_DV
        for hh in range(RET_H):
            @pl.when(h == hh)
            def _():
                dproj_ref[:, hh * RET_DK:(hh + 1) * RET_DK] = dq
                dproj_ref[:, nq + hh * RET_DK:nq + (hh + 1) * RET_DK] = dk
                dproj_ref[:, 2 * nq + hh * RET_DV:2 * nq + (hh + 1) * RET_DV] = dv
                dproj_ref[:, 2 * nq + nv + hh * RET_DV:2 * nq + nv + (hh + 1) * RET_DV] = dg

    width = 2 * RET_H * (RET_DK + RET_DV)

    def all_heads(*shape):
        return _bs((RET_H,) + shape, lambda b, c, h: (0,) * (1 + len(shape)))

    return _pcall(
        body, name="ret_bwd", grid=(B, nC, RET_H),
        in_specs=[s["q"], s["k"], s["v"], s["g"], s["wide"], s["st"], s["wide"], s["cs"], s["cs"],
                  all_heads(RET_SC, RET_SC), all_heads(RET_SC, 1), all_heads(RET_SC, 1), all_heads(1, 1),
                  all_heads(1, RET_DV)],
        out_specs=[_bs((RET_SC, width), lambda b, c, h: (b * nC + nC - 1 - c, 0)),
                   _bs((RET_H, 1, RET_DV), lambda b, c, h: (0, 0, 0))],
        out_shape=[jax.ShapeDtypeStruct((T, width), BF16), jax.ShapeDtypeStruct((RET_H, 1, RET_DV), F32)],
        scratch_shapes=[pltpu.VMEM((RET_H, RET_DK, RET_DV), F32)], sem=("arbitrary", "arbitrary", "arbitrary"),
        args=(proj, proj, proj, proj, o_raw, states, dgt, cos, sin, dm, qd, kd, cd, gn), comm=comm)


MLA_PAD = 256
MLA_R2 = 2 * MLA_ROPE


def _dup(t):
    return jnp.concatenate([t, t], axis=-1)


def _fold(t):
    return t[..., :MLA_ROPE] + t[..., MLA_ROPE:]


def _mla_tables(S):
    half = MLA_ROPE // 2
    inv = ROPE_THETA ** (-jnp.arange(half, dtype=F32) / half)
    ang = jnp.arange(S).astype(F32)[:, None] * inv[None, :]
    cos, sin, zero = jnp.cos(ang), jnp.sin(ang), jnp.zeros((S, MLA_ROPE), F32)
    return jnp.concatenate([cos, cos, zero], axis=-1), jnp.concatenate([-sin, sin, zero], axis=-1)


def _head_norm_rope(n, r2, gn, gr2, cos, sin, scale):
    ssq = jnp.sum(n * n, axis=-1, keepdims=True) + 0.5 * jnp.sum(r2 * r2, axis=-1, keepdims=True)
    rstd = lax.rsqrt(ssq * (1.0 / MLA_QK) + RMS_EPS)
    yn = n * rstd * gn
    yr = r2 * rstd * gr2
    z = yr * cos + pltpu.roll(yr, MLA_ROPE // 2, 1) * sin
    if scale != 1.0:
        yn, z = yn * scale, z * scale
    return yn, z


def _head_norm_rope_bwd(dn, dz, n, r2, gn, gr2, cos, sin, scale):
    ssq = jnp.sum(n * n, axis=-1, keepdims=True) + 0.5 * jnp.sum(r2 * r2, axis=-1, keepdims=True)
    rstd = lax.rsqrt(ssq * (1.0 / MLA_QK) + RMS_EPS)
    hn, hr = n * rstd, r2 * rstd
    if scale != 1.0:
        dn, dz = dn * scale, dz * scale
    dyr = dz * cos + pltpu.roll(dz * sin, MLA_R2 - MLA_ROPE // 2, 1)
    dgn = jnp.sum(dn * hn, axis=0, keepdims=True)
    dgr = jnp.sum(dyr * hr, axis=0, keepdims=True)
    dhn, dhr = dn * gn, dyr * gr2
    mt = (jnp.sum(dhn * hn, axis=-1, keepdims=True) + jnp.sum(dhr * hr, axis=-1, keepdims=True)) * (1.0 / MLA_QK)
    return rstd * (dhn - hn * mt), rstd * (dhr - 0.5 * hr * mt), dgn, dgr


def _diag_bias():
    i = jnp.arange(ATT_TQ)
    return jnp.where((i[None, :] // CHUNK) <= (i[:, None] // CHUNK), 0.0, MASK_VALUE).astype(F32)


def _store_pair(dst, rows, n, r2):
    dst[rows, :MLA_NOPE] = n.astype(BF16)
    dst[rows, MLA_NOPE:] = r2.astype(BF16)


def _mla_fwd(q_raw, kv, kr, gains, tabs, B, S, comm=None):
    T = B * S
    TQ = ATT_TQ
    nQ = S // TQ
    qgn, qgr, kgn, kgr = gains
    cos, sin = tabs
    scale = MLA_QK ** -0.5

    def body(q_ref, kv_ref, kr_ref, qgn_ref, qgr_ref, kgn_ref, kgr_ref, c_ref, s_ref, bias_ref,
             o_ref, lse_ref, qf_s, kf_s, v_s):
        def prep(t, _):
            rows = pl.ds(pl.multiple_of(t * TQ, TQ), TQ)
            cs, sn = c_ref[rows, :], s_ref[rows, :]
            qn, qr = _head_norm_rope(q_ref[rows, :MLA_NOPE], q_ref[rows, MLA_NOPE:], qgn_ref[...], qgr_ref[...],
                                     cs, sn, scale)
            _store_pair(qf_s, rows, qn, qr)
            kn, krr = _head_norm_rope(kv_ref[rows, :MLA_NOPE], kr_ref[rows, :], kgn_ref[...], kgr_ref[...], cs, sn, 1.0)
            _store_pair(kf_s, rows, kn, krr)
            v_s[rows, :] = kv_ref[rows, MLA_NOPE:].astype(BF16)
            return 0

        lax.fori_loop(0, nQ, prep, 0, unroll=2)
        for i in range(nQ):
            rows = slice(i * TQ, (i + 1) * TQ)
            q = qf_s[rows, :]
            sd = _dot(q, kf_s[rows, :], NT) + bias_ref[...]
            m = jnp.max(sd, axis=-1, keepdims=True)
            if i:
                sl = _dot(q, kf_s[:i * TQ, :], NT)
                m = jnp.maximum(m, jnp.max(sl, axis=-1, keepdims=True))
            pd = jnp.exp(sd - m)
            l = jnp.sum(pd, axis=-1, keepdims=True)
            acc = _dot(pd.astype(BF16), v_s[rows, :], NN)
            if i:
                pl_ = jnp.exp(sl - m)
                l = l + jnp.sum(pl_, axis=-1, keepdims=True)
                acc = acc + _dot(pl_.astype(BF16), v_s[:i * TQ, :], NN)
            o_ref[rows, :] = (acc / l).astype(BF16)
            lse_ref[rows, :] = m + jnp.log(l)

    def vec(n):
        return _bs((1, n), lambda b, h: (0, 0))

    def cols(n):
        return _bs((S, n), lambda b, h: (b, h))

    tab = _bs((S, MLA_R2), lambda b, h: (0, 0))
    return _pcall(
        body, name="mla_fwd", grid=(B, MLA_H),
        in_specs=[cols(MLA_PAD), cols(MLA_NOPE + MLA_V), _bs((S, MLA_R2), lambda b, h: (b, 0)),
                  vec(MLA_NOPE), vec(MLA_R2), vec(MLA_NOPE), vec(MLA_R2), tab, tab,
                  _bs((TQ, TQ), lambda b, h: (0, 0))],
        out_specs=[cols(MLA_V), _bs((None, S, 1), lambda b, h: (h, b, 0)), cols(MLA_PAD), cols(MLA_PAD)],
        out_shape=[jax.ShapeDtypeStruct((T, MLA_H * MLA_V), BF16), jax.ShapeDtypeStruct((MLA_H, T, 1), F32),
                   jax.ShapeDtypeStruct((T, MLA_H * MLA_PAD), BF16), jax.ShapeDtypeStruct((T, MLA_H * MLA_PAD), BF16)],
        scratch_shapes=[pltpu.VMEM((S, MLA_V), BF16)],
        sem=("parallel", "parallel"), args=(q_raw, kv, kr, qgn, qgr, kgn, kgr, cos, sin, _diag_bias()), comm=comm)


def _mla_bwd(q_raw, kv, kr, o, lse, do, qf, kf, gains, tabs, B, S, comm=None):
    T = B * S
    TQ = ATT_TQ
    nQ = S // TQ
    qgn, qgr, kgn, kgr = gains
    cos, sin = tabs
    scale = MLA_QK ** -0.5

    def body(q_ref, kv_ref, kr_ref, o_ref, lse_ref, do_ref, qf_s, kf_s, qgn_ref, qgr_ref, kgn_ref, kgr_ref, c_ref, s_ref,
             bias_ref, dq_ref, dkv_ref, dkr_ref, dqgn_ref, dqgr_ref, dkgn_ref, dkgr_ref,
             v_s, dl_s, dq_s, dk_s, dv_s):
        b, h = pl.program_id(0), pl.program_id(1)

        def blk(t):
            return pl.ds(pl.multiple_of(t * TQ, TQ), TQ)

        def prep(t, _):
            rows = blk(t)
            v_s[rows, :] = kv_ref[rows, MLA_NOPE:].astype(BF16)
            dl_s[rows, :] = jnp.sum(do_ref[rows, :].astype(F32) * o_ref[rows, :].astype(F32), axis=-1, keepdims=True)
            dk_s[rows, :] = jnp.zeros((TQ, MLA_PAD), F32)
            dv_s[rows, :] = jnp.zeros((TQ, MLA_V), F32)
            return 0

        lax.fori_loop(0, nQ, prep, 0, unroll=2)

        gqn, gqr = jnp.zeros((1, MLA_NOPE), F32), jnp.zeros((1, MLA_R2), F32)
        for i in range(nQ):
            rows = slice(i * TQ, (i + 1) * TQ)
            q, doi, lse_i, dl_i = qf_s[rows, :], do_ref[rows, :], lse_ref[rows, :], dl_s[rows, :]

            def part(cols, bias):
                k, v = kf_s[cols, :], v_s[cols, :]
                s = _dot(q, k, NT)
                if bias is not None:
                    s = s + bias
                p = jnp.exp(s - lse_i)
                ds = (p * (_dot(doi, v, NT) - dl_i)).astype(BF16)
                dk_s[cols, :] += _dot(ds, q, TN)
                dv_s[cols, :] += _dot(p.astype(BF16), doi, TN)
                return _dot(ds, k, NN)

            dq = part(rows, bias_ref[...])
            if i:
                dq = dq + part(slice(0, i * TQ), None)
            dq_s[...] = dq
            dqn, dqr, a0, a1 = _head_norm_rope_bwd(dq_s[:, :MLA_NOPE], dq_s[:, MLA_NOPE:], q_ref[rows, :MLA_NOPE],
                                                   q_ref[rows, MLA_NOPE:], qgn_ref[...], qgr_ref[...],
                                                   c_ref[rows, :], s_ref[rows, :], scale)
            _store_pair(dq_ref, rows, dqn, dqr)
            gqn, gqr = gqn + a0, gqr + a1

        def post(t, carry):
            rows = blk(t)
            dkn, dkr, a2, a3 = _head_norm_rope_bwd(dk_s[rows, :MLA_NOPE], dk_s[rows, MLA_NOPE:],
                                                   kv_ref[rows, :MLA_NOPE], kr_ref[rows, :], kgn_ref[...], kgr_ref[...],
                                                   c_ref[rows, :], s_ref[rows, :], 1.0)
            dkv_ref[rows, :MLA_NOPE] = dkn.astype(BF16)
            dkv_ref[rows, MLA_NOPE:] = dv_s[rows, :].astype(BF16)

            @pl.when(h == 0)
            def _():
                dkr_ref[rows, :] = dkr

            @pl.when(h > 0)
            def _():
                dkr_ref[rows, :] += dkr

            return carry[0] + a2, carry[1] + a3

        gkn, gkr = lax.fori_loop(0, nQ, post, (jnp.zeros((1, MLA_NOPE), F32), jnp.zeros((1, MLA_R2), F32)), unroll=2)
        first = (b == 0) & (h == 0)

        @pl.when(first)
        def _():
            dqgn_ref[...] = gqn
            dqgr_ref[...] = gqr
            dkgn_ref[...] = gkn
            dkgr_ref[...] = gkr

        @pl.when(jnp.logical_not(first))
        def _():
            dqgn_ref[...] += gqn
            dqgr_ref[...] += gqr
            dkgn_ref[...] += gkn
            dkgr_ref[...] += gkr

    def vec(n):
        return _bs((1, n), lambda b, h: (0, 0))

    def cols(n):
        return _bs((S, n), lambda b, h: (b, h))

    tab = _bs((S, MLA_R2), lambda b, h: (0, 0))
    return _pcall(
        body, name="mla_bwd", grid=(B, MLA_H),
        in_specs=[cols(MLA_PAD), cols(MLA_NOPE + MLA_V), _bs((S, MLA_R2), lambda b, h: (b, 0)), cols(MLA_V),
                  _bs((None, S, 1), lambda b, h: (h, b, 0)), cols(MLA_V), cols(MLA_PAD), cols(MLA_PAD),
                  vec(MLA_NOPE), vec(MLA_R2), vec(MLA_NOPE), vec(MLA_R2), tab, tab,
                  _bs((TQ, TQ), lambda b, h: (0, 0))],
        out_specs=[cols(MLA_PAD), cols(MLA_NOPE + MLA_V), _bs((S, MLA_R2), lambda b, h: (b, 0)),
                   vec(MLA_NOPE), vec(MLA_R2), vec(MLA_NOPE), vec(MLA_R2)],
        out_shape=[jax.ShapeDtypeStruct((T, MLA_H * MLA_PAD), BF16),
                   jax.ShapeDtypeStruct((T, MLA_H * (MLA_NOPE + MLA_V)), BF16),
                   jax.ShapeDtypeStruct((T, MLA_R2), F32), jax.ShapeDtypeStruct((1, MLA_NOPE), F32),
                   jax.ShapeDtypeStruct((1, MLA_R2), F32), jax.ShapeDtypeStruct((1, MLA_NOPE), F32),
                   jax.ShapeDtypeStruct((1, MLA_R2), F32)],
        scratch_shapes=[pltpu.VMEM((S, MLA_V), BF16), pltpu.VMEM((S, 1), F32), pltpu.VMEM((TQ, MLA_PAD), F32),
                        pltpu.VMEM((S, MLA_PAD), F32), pltpu.VMEM((S, MLA_V), F32)],
        sem=("arbitrary", "arbitrary"),
        args=(q_raw, kv, kr, o, lse, do, qf, kf, qgn, qgr, kgn, kgr, cos, sin, _diag_bias()), comm=comm)


def _adamw(name, recvs, w, m, v, tr=None, comm=None):
    n, R, C = recvs[0].shape
    L = len(recvs)
    Lw, Rw, _ = w.shape
    assert Lw * Rw == L * R and w.shape[2] == C
    tr = R if tr is None else tr
    assert R % tr == 0 and Rw % tr == 0
    per = R // tr
    per_w = Rw // tr
    c1 = 1.0 - ADAM_B1 ** ADAM_STEP
    c2 = 1.0 - ADAM_B2 ** ADAM_STEP

    def body(*refs):
        r_refs = refs[:L]
        w_ref, m_ref, v_ref, g_ref, d_ref, nm_ref, nv_ref = refs[L:]
        layer = pl.program_id(0) // per

        def total(r_ref):
            t = r_ref[0].astype(F32)
            for k in range(1, n):
                t = t + r_ref[k].astype(F32)
            return t

        g = total(r_refs[0]) if L == 1 else lax.switch(layer, [functools.partial(total, r) for r in r_refs])
        mm = ADAM_B1 * m_ref[...] + (1.0 - ADAM_B1) * g
        vv = ADAM_B2 * v_ref[...] + (1.0 - ADAM_B2) * (g * g)
        g_ref[...] = g
        nm_ref[...] = mm
        nv_ref[...] = vv
        d_ref[...] = -ADAM_LR * ((mm / c1) / (jnp.sqrt(vv / c2) + ADAM_EPS) + ADAM_WD * w_ref[...])

    blk = _bs((None, tr, C), lambda i: (i // per_w, i % per_w, 0))
    r_specs = [_bs((n, tr, C), functools.partial(lambda l, i: (0, jnp.clip(i - l * per, 0, per - 1), 0), l))
               for l in range(L)]
    outs, got = _pcall(body, name=name, grid=(L * per,), in_specs=r_specs + [blk, blk, blk], out_specs=[blk] * 4,
                       out_shape=[jax.ShapeDtypeStruct(w.shape, F32)] * 4, scratch_shapes=[], sem=("arbitrary",),
                       args=(*recvs, w, m, v), comm=comm)
    return outs if comm is None else (outs, got)


def _sum8(name, a):
    n, R, C = a.shape

    def body(a_ref, o_ref):
        s = a_ref[0]
        for k in range(1, n):
            s = s + a_ref[k]
        o_ref[...] = s

    return pl.pallas_call(body, name=name, out_shape=jax.ShapeDtypeStruct((R, C), a.dtype))(a)


def _sds(shape, dt):
    return jax.ShapeDtypeStruct(shape, dt)


def _norm_proj(name, x, g, w, o_spec, out_shape, tm=1024, comm=None):
    T, K = x.shape
    J, _, n = w.shape

    def body(x_ref, g_ref, w_ref, o_ref, h_ref, hs):
        @pl.when(pl.program_id(1) == 0)
        def _():
            xf = x_ref[...]
            r = lax.rsqrt(jnp.mean(xf * xf, axis=-1, keepdims=True) + RMS_EPS)
            h = (xf * r * g_ref[...]).astype(BF16)
            hs[...] = h
            h_ref[...] = h

        o_ref[...] = _dot(hs[...], w_ref[...], NN).astype(o_ref.dtype)

    row = _bs((tm, K), lambda m, j: (m, 0))
    (out, h), got = _pcall(
        body, name=name, grid=(T // tm, J),
        in_specs=[row, _bs((1, K), lambda m, j: (0, 0)), _bs((None, K, n), lambda m, j: (j, 0, 0))],
        out_specs=[o_spec, row], out_shape=[out_shape, _sds((T, K), BF16)], scratch_shapes=[pltpu.VMEM((tm, K), BF16)],
        sem=("parallel", "arbitrary"), args=(x, g, w), comm=comm)
    return out, h, got


def _proj_shared_dx(name, d, w, tm=1024, comm=None):
    J, T, n = d.shape
    K = w.shape[1]
    return _mm(name, d, w, grid=(T // tm, J), a_spec=_bs((None, tm, n), lambda m, k: (k, m, 0)),
               b_spec=_bs((None, K, n), lambda m, k: (k, 0, 0)), o_spec=_bs((tm, K), lambda m, k: (m, 0)),
               out_shape=_sds((T, K), BF16), dims=NT, kax=1, acc_shape=(tm, K), comm=comm)


def _out_proj(name, a, w, res, tm=512):
    J, T, k = a.shape
    N = w.shape[2]
    return _mm(name, a, w, grid=(T // tm,), a_spec=_bs((J, tm, k), lambda m: (0, m, 0)),
               b_spec=_bs((J, k, N), lambda m: (0, 0, 0)), o_spec=_bs((tm, N), lambda m: (m, 0)),
               out_shape=_sds((T, N), F32), dims=NN, res=res, res_spec=_bs((tm, N), lambda m: (m, 0)), jb=J)


def _out_proj_dx(name, dx, w, tm=1024, comm=None):
    T, N = dx.shape
    J, k, _ = w.shape
    return _mm(name, dx, w, grid=(T // tm, J), a_spec=_bs((tm, N), lambda m, j: (m, 0)),
               b_spec=_bs((None, k, N), lambda m, j: (j, 0, 0)), o_spec=_bs((None, tm, k), lambda m, j: (j, m, 0)),
               out_shape=_sds((J, T, k), BF16), dims=NT, comm=comm)


def _out_proj_dw(name, a, dx, tt=1024, comm=None):
    J, T, k = a.shape
    N = dx.shape[1]
    tt = min(tt, T)
    return _mm(name, a, dx, grid=(J, T // tt), a_spec=_bs((None, tt, k), lambda j, t: (j, t, 0)),
               b_spec=_bs((tt, N), lambda j, t: (t, 0)), o_spec=_bs((None, k, N), lambda j, t: (j, 0, 0)),
               out_shape=_sds((J, k, N), BF16), dims=TN, kax=1, acc_shape=(k, N), comm=comm)


def _dense(name, a, b, dims, out_dtype, tm=512, res=None, comm=None):
    if dims == TN:
        T, K = a.shape
        N = b.shape[1]
        return _mm(name, a, b, grid=(T // tm,), a_spec=_bs((tm, K), lambda t: (t, 0)),
                   b_spec=_bs((tm, N), lambda t: (t, 0)), o_spec=_bs((K, N), lambda t: (0, 0)),
                   out_shape=_sds((K, N), out_dtype), dims=TN, kax=0, acc_shape=(K, N), comm=comm)
    M, K = a.shape
    N = b.shape[1] if dims == NN else b.shape[0]
    row = _bs((tm, N), lambda m: (m, 0))
    return _mm(name, a, b, grid=(M // tm,), a_spec=_bs((tm, K), lambda m: (m, 0)), b_spec=_bs(b.shape, lambda m: (0, 0)),
               o_spec=row, out_shape=_sds((M, N), out_dtype), dims=dims, res=res,
               res_spec=row if res is not None else None, comm=comm)


def _bf16(x):
    return x.astype(BF16)


def _ffn_fwd(i, x, norm_g, w_in, cw, cb, w_out, B, S, comm_in=None, tgt=None):
    T = x.shape[0]
    u, h, got = _norm_proj(f"ffn{i}_in", x, norm_g, w_in, _bs((None, 1024, FSH), lambda m, j: (j, m, 0)),
                           _sds((NDEV, T, FSH), BF16), comm=comm_in)
    u4 = u.reshape(2, 4, T, FSH)
    gt, gcb = _convffn_fwd(f"ffn{i}_gate", u4, cw, cb, B, S)
    if tgt is None:
        y = _out_proj(f"ffn{i}_out", gt, w_out, x)
    else:
        y = _out_proj_loss(f"ffn{i}_out_loss", gt, w_out, x, tgt)
    return y, (x, h, u4, gt, gcb), got


def _ffn_bwd(i, dy, dyb, saved, norm_g, w_in, cw, w_out, B, S, first_half_early, riders=(None, None)):
    x, h, u4, gt, gcb = saved
    dgt = _out_proj_dx(f"ffn{i}_out_dx", dyb, w_out, comm=riders[0])
    dw_out = _out_proj_dw(f"ffn{i}_out_dw", gt, dyb, comm=riders[1])
    dgt, got0 = dgt if riders[0] is not None else (dgt, None)
    dw_out, got1 = dw_out if riders[1] is not None else (dw_out, None)
    dw_out = dw_out.reshape(NDEV, FSH // 2, D_MODEL)
    (du4, dcw, dcb), (r_out,) = _convffn_bwd(f"ffn{i}_gate_bwd", u4, gcb, cw, dgt, B, S, comm=_Exchange([dw_out]))
    du = du4.reshape(NDEV, du4.shape[2], FSH)
    dw_in = _out_proj_dw(f"ffn{i}_in_dw", du, h, tt=2048)
    r_in = None
    if first_half_early:
        dh, (r_in,) = _proj_shared_dx(f"ffn{i}_in_dx", du, w_in, comm=_Exchange([dw_in], rows=[(0, FSH // 2)]))
    else:
        dh = _proj_shared_dx(f"ffn{i}_in_dx", du, w_in)
    dx, dgn, dxb = _rms_bwd(f"ffn{i}_norm_bwd", x, norm_g, dh, dres=dy, also_bf16=True)
    return dx, dxb, dict(w_in=dw_in, norm=dgn, cw=dcw, cb=dcb), r_out, r_in, (got0, got1)


def kernel(x, ret_norm, ret_w_in, ret_gn, ret_w_out, mla_norm, mla_w_in, mla_q_norm, mla_w_qb, mla_kv_norm, mla_w_kvb, mla_q_head_norm, mla_k_head_norm, mla_w_out, ffn_norm, ffn_w_in, ffn_conv_w, ffn_conv_b, ffn_w_out, loss_target, m_ret_norm, m_ret_w_in, m_ret_gn, m_ret_w_out, m_mla_norm, m_mla_w_in, m_mla_q_norm, m_mla_w_qb, m_mla_kv_norm, m_mla_w_kvb, m_mla_q_head_norm, m_mla_k_head_norm, m_mla_w_out, m_ffn_norm, m_ffn_w_in, m_ffn_conv_w, m_ffn_conv_b, m_ffn_w_out, v_ret_norm, v_ret_w_in, v_ret_gn, v_ret_w_out, v_mla_norm, v_mla_w_in, v_mla_q_norm, v_mla_w_qb, v_mla_kv_norm, v_mla_w_kvb, v_mla_q_head_norm, v_mla_k_head_norm, v_mla_w_out, v_ffn_norm, v_ffn_w_in, v_ffn_conv_w, v_ffn_conv_b, v_ffn_w_out):
    B, S, D = x.shape
    T = B * S
    w = dict(ret_norm=ret_norm, ret_w_in=ret_w_in, ret_gn=ret_gn, ret_w_out=ret_w_out, mla_norm=mla_norm,
             mla_w_in=mla_w_in, mla_q_norm=mla_q_norm, mla_w_qb=mla_w_qb, mla_kv_norm=mla_kv_norm, mla_w_kvb=mla_w_kvb,
             mla_q_head_norm=mla_q_head_norm, mla_k_head_norm=mla_k_head_norm, mla_w_out=mla_w_out, ffn_norm=ffn_norm,
             ffn_w_in=ffn_w_in, ffn_conv_w=ffn_conv_w, ffn_conv_b=ffn_conv_b, ffn_w_out=ffn_w_out)
    mom = dict(ret_norm=m_ret_norm, ret_w_in=m_ret_w_in, ret_gn=m_ret_gn, ret_w_out=m_ret_w_out, mla_norm=m_mla_norm,
               mla_w_in=m_mla_w_in, mla_q_norm=m_mla_q_norm, mla_w_qb=m_mla_w_qb, mla_kv_norm=m_mla_kv_norm,
               mla_w_kvb=m_mla_w_kvb, mla_q_head_norm=m_mla_q_head_norm, mla_k_head_norm=m_mla_k_head_norm,
               mla_w_out=m_mla_w_out, ffn_norm=m_ffn_norm, ffn_w_in=m_ffn_w_in, ffn_conv_w=m_ffn_conv_w,
               ffn_conv_b=m_ffn_conv_b, ffn_w_out=m_ffn_w_out)
    var = dict(ret_norm=v_ret_norm, ret_w_in=v_ret_w_in, ret_gn=v_ret_gn, ret_w_out=v_ret_w_out, mla_norm=v_mla_norm,
               mla_w_in=v_mla_w_in, mla_q_norm=v_mla_q_norm, mla_w_qb=v_mla_w_qb, mla_kv_norm=v_mla_kv_norm,
               mla_w_kvb=v_mla_w_kvb, mla_q_head_norm=v_mla_q_head_norm, mla_k_head_norm=v_mla_k_head_norm,
               mla_w_out=v_mla_w_out, ffn_norm=v_ffn_norm, ffn_w_in=v_ffn_w_in, ffn_conv_w=v_ffn_conv_w,
               ffn_conv_b=v_ffn_conv_b, ffn_w_out=v_ffn_w_out)
    BIG = ["ret_w_in", "ret_w_out", "mla_w_in", "mla_w_qb", "mla_w_kvb", "mla_w_out", "ffn_w_in", "ffn_w_out"]
    REPL = ["ret_norm", "ffn_norm", "mla_q_head_norm", "mla_k_head_norm", "ffn_conv_b"]
    SHARDED_SMALL = ["ffn_conv_w", "ret_gn", "mla_norm", "mla_q_norm", "mla_kv_norm"]
    dev = _idx(_place())

    def blk16(k, i=0):
        return _bf16(w[k][i])

    small_vec = jnp.concatenate([w[k].reshape(-1) for k in SHARDED_SMALL])
    n_small = small_vec.shape[0]
    small_vec = jnp.pad(small_vec, (0, 4096 - n_small)).reshape(32, 128)
    Wret_in, sg = _comm_call("gather_ret_w_in", _Gather([blk16("ret_w_in"), small_vec], parts=2))
    sg = sg.reshape(NDEV, 4096)
    o0 = 0
    conv_w_full = sg[:, o0:o0 + 2112].reshape(NDEV, 2, 3, 352).transpose(1, 2, 0, 3).reshape(2, 3, FFN)
    o0 += 2112
    ret_gn_full = sg[:, o0:o0 + 256].reshape(NDEV, RET_H, 64).transpose(1, 0, 2).reshape(RET_H, 1, RET_DV)
    o0 += 256
    mla_norm_full = sg[:, o0:o0 + 128].reshape(1, D)
    o0 += 128
    q_norm_full = sg[:, o0:o0 + 48].reshape(1, MLA_QR)
    o0 += 48
    kv_norm_full = sg[:, o0:o0 + 32].reshape(1, MLA_KVR)

    cw = [conv_w_full[i].reshape(3, 4, FSH).transpose(1, 0, 2) for i in range(2)]
    cb = [ffn_conv_b[i].reshape(4, 1, FSH) for i in range(2)]
    fnorm = [ffn_norm[i].reshape(1, D) for i in range(2)]
    rtabs = _ret_tables(S)
    mtabs = _mla_tables(S)
    qh, kh = mla_q_head_norm.reshape(1, MLA_QK), mla_k_head_norm.reshape(1, MLA_QK)
    gains = (qh[:, :MLA_NOPE], _dup(qh[:, MLA_NOPE:]), kh[:, :MLA_NOPE], _dup(kh[:, MLA_NOPE:]))

    x0 = x.reshape(T, D)
    tgt = loss_target.reshape(T, D)
    proj, h0, (Wret_out, Wffn_out0) = _norm_proj(
        "ret_in", x0, ret_norm.reshape(1, D), Wret_in, _bs((1024, 768), lambda m, j: (m, j)), _sds((T, 6144), BF16),
        comm=_Gather([blk16("ret_w_out"), blk16("ffn_w_out", 0)]))
    Wret_out = Wret_out.reshape(RET_H * RET_DV, D)
    Wffn_out0 = Wffn_out0.reshape(4, FSH, D)
    (o_raw, rgt, states), (Wffn_in0,) = _ret_fwd(proj, rtabs, ret_gn_full, B, S, comm=_Gather([blk16("ffn_w_in", 0)]))
    x1 = _dense("ret_out", rgt, Wret_out, NN, F32, res=x0)
    MLA_W = ["mla_w_in", "mla_w_qb", "mla_w_kvb", "mla_w_out"]
    x2, ffn0_saved, got = _ffn_fwd(0, x1, fnorm[0], Wffn_in0, cw[0], cb[0], Wffn_out0, B, S,
                                   comm_in=_Gather([blk16(k) for k in MLA_W]))
    Wmla_in = got[0].reshape(D, MLA_QR + MLA_KVR + MLA_ROPE)
    Wq, Wkv, Wkr = Wmla_in[:, :MLA_QR], Wmla_in[:, MLA_QR:MLA_QR + MLA_KVR], Wmla_in[:, MLA_QR + MLA_KVR:]
    Wqb, Wkvb, Wmla_out = got[1:]

    def normed(name, xin, gain, wmat):
        n = wmat.shape[1]
        out, hn, _ = _norm_proj(name, xin, gain, wmat[None], _bs((1024, n), lambda m, j: (m, 0)), _sds((T, n), F32))
        return out, hn

    c_all, h2 = normed("mla_in", x2, mla_norm_full, jnp.concatenate([Wq, Wkv, _dup(Wkr)], axis=1))
    c_q, c_kv, k_rope = c_all[:, :MLA_QR], c_all[:, MLA_QR:MLA_QR + MLA_KVR], c_all[:, MLA_QR + MLA_KVR:]
    Wqb2 = jnp.concatenate([Wqb, Wqb[:, :, MLA_NOPE:]], axis=2).transpose(1, 0, 2).reshape(MLA_QR, MLA_H * MLA_PAD)
    Wkvb2 = Wkvb.transpose(1, 0, 2).reshape(MLA_KVR, MLA_H * (MLA_NOPE + MLA_V))
    Wmla_out2 = Wmla_out.reshape(D, D)
    q_raw, cqn = normed("mla_qb", c_q, q_norm_full, Wqb2)
    kvh, ckvn = normed("mla_kvb", c_kv, kv_norm_full, Wkvb2)
    (att, lse, qf, kf), (Wffn_in1, Wffn_out1) = _mla_fwd(
        q_raw, kvh, k_rope, gains, mtabs, B, S, comm=_Gather([blk16("ffn_w_in", 1), blk16("ffn_w_out", 1)]))
    Wffn_out1 = Wffn_out1.reshape(4, FSH, D)
    x3 = _dense("mla_out", att, Wmla_out2, NN, F32, res=x2)
    (dy, colsq, dyb), ffn1_saved, _ = _ffn_fwd(1, x3, fnorm[1], Wffn_in1, cw[1], cb[1], Wffn_out1, B, S, tgt=tgt)
    loss_part = 0.5 * jnp.sum(colsq) / D

    dx3, dx3b, gf1, r_ffn1_out, _, _ = _ffn_bwd(1, dy, dyb, ffn1_saved, fnorm[1], Wffn_in1, cw[1], Wffn_out1, B, S,
                                                first_half_early=False)
    datt = _dense("mla_out_dx", dx3b, Wmla_out2, NT, BF16)
    fh = FSH // 2
    (dq_raw, dkvh, dkr, dqgn, dqgr, dkgn, dkgr), (r_ffn1_in_a, r_ffn1_in_b) = _mla_bwd(
        q_raw, kvh, k_rope, att, lse, datt, qf, kf, gains, mtabs, B, S,
        comm=_Exchange([gf1["w_in"], gf1["w_in"]], rows=[(0, fh), (fh, fh)]))
    dcqn = _dense("mla_qb_dx", dq_raw, Wqb2, NT, F32, tm=1024)
    dckvn = _dense("mla_kvb_dx", dkvh, Wkvb2, NT, F32, tm=1024)
    dcq, dg_qn = _rms_bwd("mla_q_norm_bwd", c_q, q_norm_full, dcqn)
    dckv, dg_kvn = _rms_bwd("mla_kv_norm_bwd", c_kv, kv_norm_full, dckvn)
    dqgr, dkgr = _fold(dqgr), _fold(dkgr)
    dproj2 = _bf16(jnp.concatenate([dcq, dckv, _fold(dkr)], axis=-1))
    dh2 = _dense("mla_in_dx", dproj2, Wmla_in, NT, BF16)
    dx2, dg_mla_norm, dx2b = _rms_bwd("mla_norm_bwd", x2, mla_norm_full, dh2, dres=dx3, also_bf16=True)
    dWmla_out = _dense("mla_out_dw", att, dx3b, TN, BF16, tm=1024).reshape(NDEV, MLA_V, D)
    dWqb = _dense("mla_qb_dw", dq_raw, cqn, TN, BF16, tm=1024).reshape(MLA_H, MLA_PAD, MLA_QR)
    dWqb = jnp.concatenate([dWqb[:, :MLA_NOPE], dWqb[:, MLA_NOPE:MLA_QK] + dWqb[:, MLA_QK:]], axis=1)
    dWkvb = _dense("mla_kvb_dw", ckvn, dkvh, TN, BF16, tm=1024)
    dWkvb = dWkvb.reshape(MLA_KVR, MLA_H, MLA_NOPE + MLA_V).transpose(1, 0, 2)
    dWmla_in = _dense("mla_in_dw", h2, dproj2, TN, BF16).reshape(NDEV, 128, 704)

    dx1, dx1b, gf0, r_ffn0_out, r_ffn0_in_a, (r_mla_a, r_mla_b) = _ffn_bwd(
        0, dx2, dx2b, ffn0_saved, fnorm[0], Wffn_in0, cw[0], Wffn_out0, B, S, first_half_early=True,
        riders=(_Exchange([dWmla_out, dWqb]), _Exchange([dWkvb, dWmla_in])))
    r_mla = [*r_mla_a, *r_mla_b]
    drgt = _dense("ret_out_dx", dx1b, Wret_out, NT, BF16)
    dWret_out = _dense("ret_out_dw", rgt, dx1b, TN, BF16, tm=1024).reshape(NDEV, 256, D)
    (dproj, dgn_ret), (r_ffn0_in_b, r_ret_out) = _ret_bwd(
        proj, o_raw, states, drgt, rtabs, ret_gn_full, B, S,
        comm=_Exchange([gf0["w_in"], dWret_out], rows=[(fh, fh), None]))

    tt, hk, er = min(2048, T), D // 2, D // 8

    def eighths(arr, *which):
        return _Exchange([arr] * len(which), rows=[(w * er, er) for w in which])

    def ret_in_dw(name, half, comm):
        return _mm(name, h0, dproj, grid=(NDEV, T // tt), a_spec=_bs((tt, hk), lambda j, t: (t, half)),
                   b_spec=_bs((tt, 768), lambda j, t: (t, j)), o_spec=_bs((None, hk, 768), lambda j, t: (j, 0, 0)),
                   out_shape=_sds((NDEV, hk, 768), BF16), dims=TN, kax=1, acc_shape=(hk, 768), comm=comm)

    dW_top = ret_in_dw("ret_in_dw_top", 0, None)
    dW_bot, r_e01 = ret_in_dw("ret_in_dw_bot", 1, eighths(dW_top, 0, 1))
    both = _Exchange([dW_top, dW_top, dW_bot, dW_bot, dW_bot], rows=[(w * er, er) for w in (2, 3, 0, 1, 2)])
    dh0, r_e23456 = _mm(
        "ret_in_dx", dproj, Wret_in, grid=(T // 1024, NDEV), a_spec=_bs((1024, 768), lambda m, k: (m, k)),
        b_spec=_bs((None, D, 768), lambda m, k: (k, 0, 0)), o_spec=_bs((1024, D), lambda m, k: (m, 0)),
        out_shape=_sds((T, D), BF16), dims=NT, kax=1, acc_shape=(1024, D), comm=both)
    (dx0, dg_ret_norm), r_e7 = _rms_bwd("ret_norm_bwd", x0, ret_norm.reshape(1, D), dh0, dres=dx1,
                                        comm=eighths(dW_bot, 3))
    grad_x = dx0.reshape(B, S, D)
    received = dict(ret_w_in=[*r_e01, *r_e23456, *r_e7], ret_w_out=[r_ret_out], mla_w_out=[r_mla[0]], mla_w_qb=[r_mla[1]],
                    mla_w_kvb=[r_mla[2]], mla_w_in=[r_mla[3]],
                    ffn_w_in=[r_ffn0_in_a, r_ffn0_in_b, r_ffn1_in_a, r_ffn1_in_b], ffn_w_out=[r_ffn0_out, r_ffn1_out])

    dconv_w = jnp.stack([g_["cw"].transpose(1, 0, 2).reshape(3, FFN) for g_ in (gf0, gf1)])
    dconv_b = jnp.stack([g_["cb"].reshape(FFN) for g_ in (gf0, gf1)])
    small_parts = [dg_ret_norm, gf0["norm"], gf1["norm"], dg_mla_norm, dg_qn, dg_kvn, dqgn, dqgr, dkgn, dkgr, dgn_ret,
                   dconv_w, dconv_b, loss_part]
    small_g = jnp.concatenate([p.reshape(-1) for p in small_parts])
    n_grads = small_g.shape[0] - 1
    small_g = jnp.pad(small_g, (0, 240 * 128 - small_g.shape[0])).reshape(240, 128)
    small_all = _comm_call("gather_small_grads", _Gather([small_g]))[0]
    sred = _sum8("sum_small_grads", small_all).reshape(-1)
    loss = sred[n_grads]

    def take(n):
        nonlocal off
        out = sred[off:off + n]
        off += n
        return out

    off = 0
    g_small = dict(ret_norm=take(D).reshape(1, D), ffn_norm=take(2 * D).reshape(2, D), mla_norm=take(D),
                   mla_q_norm=take(MLA_QR), mla_kv_norm=take(MLA_KVR))
    g_small["mla_q_head_norm"] = take(MLA_QK).reshape(1, MLA_QK)
    g_small["mla_k_head_norm"] = take(MLA_QK).reshape(1, MLA_QK)
    g_small["ret_gn"] = take(RET_H * RET_DV).reshape(1, RET_H, RET_DV)
    g_small["ffn_conv_w"] = take(2 * 3 * FFN).reshape(2, 3, FFN)
    g_small["ffn_conv_b"] = take(2 * FFN).reshape(2, FFN)
    g_small["mla_norm"] = lax.dynamic_slice(g_small["mla_norm"], (dev * 128,), (128,)).reshape(1, 128)
    g_small["mla_q_norm"] = lax.dynamic_slice(g_small["mla_q_norm"], (dev * 48,), (48,)).reshape(1, 48)
    g_small["mla_kv_norm"] = lax.dynamic_slice(g_small["mla_kv_norm"], (dev * 32,), (32,)).reshape(1, 32)
    g_small["ret_gn"] = lax.dynamic_slice(g_small["ret_gn"], (0, 0, dev * 64), (1, RET_H, 64))
    g_small["ffn_conv_w"] = lax.dynamic_slice(g_small["ffn_conv_w"], (0, 0, dev * 352), (2, 3, 352))

    grads, delta, new_m, new_v = {}, {}, {}, {}
    for k in BIG:
        rcs = received[k]
        tr = max(t for t in range(16, 257, 16) if rcs[0].shape[1] % t == 0)
        flip = (lambda t: t.transpose(0, 2, 1)) if k in ("ffn_w_in", "mla_w_qb") else (lambda t: t)
        res = _adamw(f"adamw_{k}", rcs, flip(w[k]), flip(mom[k]), flip(var[k]), tr=tr)
        grads[k], delta[k], new_m[k], new_v[k] = (flip(t) for t in res)
    SMALL = REPL + SHARDED_SMALL

    def pack(d):
        vflat = jnp.concatenate([d[k].reshape(-1) for k in SMALL])
        return jnp.pad(vflat, (0, 96 * 128 - vflat.shape[0])).reshape(1, 96, 128)

    ps = _adamw("adamw_small", [pack(g_small)], pack(w), pack(mom), pack(var))
    off = 0
    for k in SMALL:
        n = w[k].size
        grads[k], delta[k], new_m[k], new_v[k] = (t.reshape(-1)[off:off + n].reshape(w[k].shape) for t in ps)
        off += n
    names = list(w)
    return (loss, grad_x, *[grads[k] for k in names], *[delta[k] for k in names], *[new_m[k] for k in names],
            *[new_v[k] for k in names])
```

```python
import functools

import jax
import jax.numpy as jnp
from jax import lax
from jax.experimental import pallas as pl
from jax.experimental.pallas import tpu as pltpu

F32, BF16 = jnp.float32, jnp.bfloat16

NDEV = 8
D_MODEL = 1024
CHUNK = 64
RMS_EPS = 1e-6
ROPE_THETA = 10000.0
RET_H, RET_DK, RET_DV = 4, 256, 512
RET_SC = 256
MLA_H, MLA_QR, MLA_KVR = 8, 384, 256
MLA_NOPE, MLA_ROPE, MLA_V = 128, 64, 128
MLA_QK = MLA_NOPE + MLA_ROPE
MASK_VALUE = -1e30
FFN = 2816
FSH = FFN * 2 // NDEV
ATT_TQ = 256
ADAM_LR, ADAM_B1, ADAM_B2, ADAM_EPS, ADAM_WD, ADAM_STEP = 0.001, 0.9, 0.999, 1e-08, 0.01, 10
MESH = pl.DeviceIdType.MESH
VMEM_LIMIT = 56 * 2 ** 20


def _cp(sem):
    return pltpu.CompilerParams(dimension_semantics=sem, vmem_limit_bytes=VMEM_LIMIT)


def _dot(a, b, dims):
    return lax.dot_general(a, b, (dims, ((), ())), preferred_element_type=F32)


NN = ((1,), (0,))
NT = ((1,), (1,))
TN = ((0,), (0,))


def _place():
    return lax.axis_index("x"), lax.axis_index("y"), lax.axis_index("c")


def _idx(d):
    return 4 * d[0] + 2 * d[1] + d[2]


ANY = pl.BlockSpec(memory_space=pl.ANY)


class _Gather:
    def __init__(self, arrs, parts=1):
        self.srcs = list(arrs)
        self.parts = parts
        self.nsem = len(arrs) * parts
        self.out_shape = [jax.ShapeDtypeStruct((NDEV,) + a.shape, a.dtype) for a in arrs]

    def _copies(self, ins, outs, send, recv, loc):
        n = self.nsem
        x, y, c = _place()
        me, sib = (x, y, c), (x, y, 1 - c)
        chips = [(1 - x, y), (x, 1 - y), (1 - x, 1 - y)]

        def piece(ref, v, *lead):
            a, p = divmod(v, self.parts)
            if self.parts > 1:
                rows = self.srcs[a].shape[0] // self.parts
                lead = (*lead, pl.ds(p * rows, rows))
            return ref[a].at[lead] if lead else ref[a]

        def cp(a, k, block, to, src=None):
            dst = piece(outs, a, _idx(block))
            return pltpu.make_async_remote_copy(src_ref=dst if src is None else src, dst_ref=dst,
                                                send_sem=send.at[a, k], recv_sem=recv.at[a, k], device_id=to,
                                                device_id_type=MESH)

        own = [piece(ins, a) for a in range(n)]
        mine = [pltpu.make_async_copy(own[a], piece(outs, a, _idx(me)), loc.at[a]) for a in range(n)]
        first = [cp(a, 0, me, sib, src=own[a]) for a in range(n)]
        first += [cp(a, 1 + j, me, (*chip, c), src=own[a]) for a in range(n) for j, chip in enumerate(chips)]
        landed = [cp(a, 1 + j, (*chip, c), me) for j, chip in enumerate(chips) for a in range(n)]
        passed = [cp(a, 4 + j, (*chip, c), sib) for j, chip in enumerate(chips) for a in range(n)]
        from_sib = [cp(a, 0, sib, me) for a in range(n)]
        from_sib += [cp(a, 4 + j, (*chip, 1 - c), me) for j, chip in enumerate(chips) for a in range(n)]
        return mine, first, landed, passed, from_sib

    def start(self, *refs):
        mine, first, _, _, _ = self._copies(*refs)
        for cp in mine + first:
            cp.start()

    def mid(self, *refs):
        _, _, landed, passed, _ = self._copies(*refs)
        for got, on in zip(landed, passed):
            got.wait_recv()
            on.start()

    def finish(self, *refs):
        mine, first, _, passed, from_sib = self._copies(*refs)
        for cp in from_sib:
            cp.wait_recv()
        for cp in first + passed:
            cp.wait_send()
        for cp in mine:
            cp.wait()


class _Exchange:
    def __init__(self, arrs, rows=None):
        self.srcs = list(arrs)
        self.nsem = len(arrs)
        self.rows = rows if rows is not None else [None] * len(arrs)
        self.out_shape = [jax.ShapeDtypeStruct(a.shape if r is None else (a.shape[0], r[1]) + a.shape[2:], a.dtype)
                          for a, r in zip(arrs, self.rows)]

    def _copies(self, ins, outs, send, recv, loc):
        n = len(self.srcs)
        x, y, c = _place()
        me = _idx((x, y, c))

        def src(a, q):
            r = self.rows[a]
            return ins[a].at[q] if r is None else ins[a].at[q, pl.ds(r[0], r[1])]

        mine = [pltpu.make_async_copy(src(a, me), outs[a].at[me], loc.at[a]) for a in range(n)]
        remote = []
        for k in range(1, NDEV):
            peer = (x ^ (k >> 2), y ^ ((k >> 1) & 1), c ^ (k & 1))
            remote += [pltpu.make_async_remote_copy(
                src_ref=src(a, _idx(peer)), dst_ref=outs[a].at[me], send_sem=send.at[a, k - 1],
                recv_sem=recv.at[a, k - 1], device_id=peer, device_id_type=MESH) for a in range(n)]
        return mine, remote

    def start(self, *refs):
        mine, remote = self._copies(*refs)
        for cp in mine + remote:
            cp.start()

    def mid(self, *refs):
        pass

    def finish(self, *refs):
        mine, remote = self._copies(*refs)
        for cp in remote + mine:
            cp.wait()


def _comm_scratch(n):
    return [pltpu.SemaphoreType.DMA((n, 7)), pltpu.SemaphoreType.DMA((n, 7)), pltpu.SemaphoreType.DMA((n,))]


def _comm_call(name, comm):
    n = len(comm.srcs)

    def body(*refs):
        parts = (refs[:n], refs[n:2 * n]) + tuple(refs[2 * n:])
        comm.start(*parts)
        comm.mid(*parts)
        comm.finish(*parts)

    return pl.pallas_call(body, name=name, in_specs=[ANY] * n, out_specs=[ANY] * n, out_shape=comm.out_shape,
                          scratch_shapes=_comm_scratch(comm.nsem))(*comm.srcs)


def _pcall(body, *, name, grid, in_specs, out_specs, out_shape, scratch_shapes, sem, args, comm=None):
    if comm is None:
        return pl.pallas_call(body, name=name, grid=grid, in_specs=in_specs, out_specs=out_specs, out_shape=out_shape,
                              scratch_shapes=scratch_shapes, compiler_params=_cp(sem))(*args), None
    ni, no, ns, nc = len(in_specs), len(out_shape), len(scratch_shapes), len(comm.srcs)
    total = 1
    for g in grid:
        total *= g
    middle = (4 * total) // 5

    def wrapped(*refs):
        ins, csrc = refs[:ni], refs[ni:ni + nc]
        outs, cdst = refs[ni + nc:ni + nc + no], refs[ni + nc + no:ni + 2 * nc + no]
        scr, sems = refs[ni + 2 * nc + no:ni + 2 * nc + no + ns], refs[ni + 2 * nc + no + ns:]
        step = pl.program_id(0)
        for k in range(1, len(grid)):
            step = step * grid[k] + pl.program_id(k)
        parts = (csrc, cdst) + tuple(sems)

        @pl.when(step == 0)
        def _():
            comm.start(*parts)

        body(*ins, *outs, *scr)

        @pl.when(step == middle)
        def _():
            comm.mid(*parts)

        @pl.when(step == total - 1)
        def _():
            comm.finish(*parts)

    res = pl.pallas_call(
        wrapped, name=name, grid=grid, in_specs=list(in_specs) + [ANY] * nc, out_specs=list(out_specs) + [ANY] * nc,
        out_shape=list(out_shape) + comm.out_shape, scratch_shapes=list(scratch_shapes) + _comm_scratch(comm.nsem),
        compiler_params=_cp(("arbitrary",) * len(grid)))(*args, *comm.srcs)
    return res[:no], res[no:]


def _mm(name, a, b, *, grid, a_spec, b_spec, o_spec, out_shape, dims, kax=None, res=None, res_spec=None,
        jb=0, acc_shape=None, comm=None):
    nk = grid[kax] if kax is not None else 1

    def body(*refs):
        if res is not None:
            a_ref, b_ref, r_ref, o_ref = refs[:4]
        else:
            a_ref, b_ref, o_ref = refs[:3]

        def product():
            if not jb:
                return _dot(a_ref[...], b_ref[...], dims)
            part = _dot(a_ref[0], b_ref[0], dims)
            for j in range(1, jb):
                part = part + _dot(a_ref[j], b_ref[j], dims)
            return part

        def fin(acc):
            if res is not None:
                acc = acc + r_ref[...]
            o_ref[...] = acc.astype(o_ref.dtype)

        if nk == 1:
            fin(product())
        else:
            acc_ref = refs[-1]
            k = pl.program_id(kax)

            @pl.when(k == 0)
            def _():
                acc_ref[...] = jnp.zeros_like(acc_ref)

            acc_ref[...] += product()

            @pl.when(k == nk - 1)
            def _():
                fin(acc_ref[...])

    sem = tuple("arbitrary" if i == kax else "parallel" for i in range(len(grid)))
    in_specs = [a_spec, b_spec] + ([res_spec] if res is not None else [])
    args = (a, b) + ((res,) if res is not None else ())
    scratch = [pltpu.VMEM(acc_shape, F32)] if nk > 1 else []
    (out,), got = _pcall(body, name=name, grid=grid, in_specs=in_specs, out_specs=[o_spec], out_shape=[out_shape],
                         scratch_shapes=scratch, sem=sem, args=args, comm=comm)
    return out if comm is None else (out, got)


def _bs(shape, fn):
    return pl.BlockSpec(shape, fn)


def _rms_bwd(name, x, g, dh, dres=None, tm=512, also_bf16=False, comm=None):
    T, D = x.shape

    def body(*refs):
        if also_bf16:
            refs, dxb_ref = refs[:-1], refs[-1]
        if dres is not None:
            x_ref, g_ref, dh_ref, dres_ref, dx_ref, dg_ref = refs
        else:
            x_ref, g_ref, dh_ref, dx_ref, dg_ref = refs
        i = pl.program_id(0)
        xf = x_ref[...]
        r = lax.rsqrt(jnp.mean(xf * xf, axis=-1, keepdims=True) + RMS_EPS)
        xh = xf * r
        d = dh_ref[...].astype(F32)
        dxh = d * g_ref[...]
        dx = r * (dxh - xh * jnp.mean(dxh * xh, axis=-1, keepdims=True))
        if dres is not None:
            dx = dx + dres_ref[...]
        dx_ref[...] = dx
        if also_bf16:
            dxb_ref[...] = dx.astype(BF16)
        part = jnp.sum(d * xh, axis=0, keepdims=True)

        @pl.when(i == 0)
        def _():
            dg_ref[...] = part

        @pl.when(i > 0)
        def _():
            dg_ref[...] += part

    row = _bs((tm, D), lambda i: (i, 0))
    vec = _bs((1, D), lambda i: (0, 0))
    in_specs = [row, vec, row] + ([row] if dres is not None else [])
    args = (x, g, dh) + ((dres,) if dres is not None else ())
    extra = [jax.ShapeDtypeStruct((T, D), BF16)] if also_bf16 else []
    outs, got = _pcall(
        body, name=name, grid=(T // tm,), in_specs=in_specs, out_specs=[row, vec] + [row] * len(extra),
        out_shape=[jax.ShapeDtypeStruct((T, D), F32), jax.ShapeDtypeStruct((1, D), F32)] + extra, scratch_shapes=[],
        sem=("arbitrary",), args=args, comm=comm)
    return outs if comm is None else (outs, got)


def _out_proj_loss(name, a, w, res, tgt, tm=512):
    J, T, k = a.shape
    N = w.shape[2]

    def body(a_ref, w_ref, r_ref, t_ref, dy_ref, s_ref, dyb_ref):
        i = pl.program_id(0)
        y = _dot(a_ref[0], w_ref[0], NN)
        for j in range(1, J):
            y = y + _dot(a_ref[j], w_ref[j], NN)
        e = (y + r_ref[...]) - t_ref[...]
        dy = e * (1.0 / N)
        dy_ref[...] = dy
        dyb_ref[...] = dy.astype(BF16)
        part = jnp.sum(e * e, axis=0, keepdims=True)

        @pl.when(i == 0)
        def _():
            s_ref[...] = part

        @pl.when(i > 0)
        def _():
            s_ref[...] += part

    row = _bs((tm, N), lambda i: (i, 0))
    return pl.pallas_call(
        body, name=name, grid=(T // tm,),
        in_specs=[_bs((J, tm, k), lambda i: (0, i, 0)), _bs((J, k, N), lambda i: (0, 0, 0)), row, row],
        out_specs=[row, _bs((1, N), lambda i: (0, 0)), row],
        out_shape=[jax.ShapeDtypeStruct((T, N), F32), jax.ShapeDtypeStruct((1, N), F32),
                   jax.ShapeDtypeStruct((T, N), BF16)],
        compiler_params=_cp(("arbitrary",)))(a, w, res, tgt)


def _shift_rows(t, k, row):
    return jnp.where(row >= k, pltpu.roll(t, k, 0), 0.0)


def _shift_rows_up(t, k, row, n):
    return jnp.where(row < n - k, pltpu.roll(t, n - k, 0), 0.0)


def _convffn_fwd(name, u, cw, cb, B, S):
    _, J, T, F = u.shape

    def body(u_ref, cw_ref, cb_ref, o_ref, gc_ref):
        a = u_ref[0].astype(F32)
        g = u_ref[1].astype(F32)
        row = lax.broadcasted_iota(jnp.int32, (S, F), 0)
        w0, w1, w2 = cw_ref[0:1, :], cw_ref[1:2, :], cw_ref[2:3, :]
        gc = _shift_rows(g, 2, row) * w0 + _shift_rows(g, 1, row) * w1 + g * w2 + cb_ref[...]
        gc_ref[...] = gc.astype(gc_ref.dtype)
        o_ref[...] = (gc * jax.nn.sigmoid(gc) * a).astype(o_ref.dtype)

    blk = _bs((None, S, F), lambda j, b: (j, b, 0))
    return pl.pallas_call(
        body, name=name, grid=(J, B),
        in_specs=[_bs((2, None, S, F), lambda j, b: (0, j, b, 0)), _bs((None, 3, F), lambda j, b: (j, 0, 0)),
                  _bs((None, 1, F), lambda j, b: (j, 0, 0))],
        out_specs=[blk, blk], out_shape=[jax.ShapeDtypeStruct((J, T, F), BF16)] * 2,
        compiler_params=_cp(("parallel", "parallel")))(u, cw, cb)


def _convffn_bwd(name, u, gcb, cw, dgt, B, S, comm=None):
    _, J, T, F = u.shape

    def body(u_ref, gc_ref, cw_ref, d_ref, du_ref, dcw_ref, dcb_ref):
        b = pl.program_id(1)
        a = u_ref[0].astype(F32)
        g = u_ref[1].astype(F32)
        gc = gc_ref[...].astype(F32)
        d = d_ref[...].astype(F32)
        row = lax.broadcasted_iota(jnp.int32, (S, F), 0)
        w0, w1, w2 = cw_ref[0:1, :], cw_ref[1:2, :], cw_ref[2:3, :]
        sg = jax.nn.sigmoid(gc)
        du_ref[0] = (d * gc * sg).astype(du_ref.dtype)
        dgc = d * a * (sg * (1.0 + gc * (1.0 - sg)))
        up1, up2 = _shift_rows_up(dgc, 1, row, S), _shift_rows_up(dgc, 2, row, S)
        du_ref[1] = (dgc * w2 + up1 * w1 + up2 * w0).astype(du_ref.dtype)
        parts = [jnp.sum(up2 * g, axis=0, keepdims=True), jnp.sum(up1 * g, axis=0, keepdims=True),
                 jnp.sum(dgc * g, axis=0, keepdims=True)]
        pb = jnp.sum(dgc, axis=0, keepdims=True)

        @pl.when(b == 0)
        def _():
            for k in range(3):
                dcw_ref[k:k + 1, :] = parts[k]
            dcb_ref[...] = pb

        @pl.when(b > 0)
        def _():
            for k in range(3):
                dcw_ref[k:k + 1, :] += parts[k]
            dcb_ref[...] += pb

    uspec = _bs((2, None, S, F), lambda j, b: (0, j, b, 0))
    blk = _bs((None, S, F), lambda j, b: (j, b, 0))
    return _pcall(
        body, name=name, grid=(J, B),
        in_specs=[uspec, blk, _bs((None, 3, F), lambda j, b: (j, 0, 0)), blk],
        out_specs=[uspec, _bs((None, 3, F), lambda j, b: (j, 0, 0)), _bs((None, 1, F), lambda j, b: (j, 0, 0))],
        out_shape=[jax.ShapeDtypeStruct(u.shape, BF16), jax.ShapeDtypeStruct((J, 3, F), F32),
                   jax.ShapeDtypeStruct((J, 1, F), F32)],
        scratch_shapes=[], sem=("parallel", "arbitrary"), args=(u, gcb, cw, dgt), comm=comm)


def _ret_tables(S):
    half = RET_DK // 2
    inv = ROPE_THETA ** (-jnp.arange(half, dtype=F32) / half)
    ang = jnp.arange(S).astype(F32)[:, None] * inv[None, :]
    lg = jnp.log1p(-jnp.exp2(-5.0 - jnp.arange(RET_H, dtype=F32)))
    i = jnp.arange(RET_SC, dtype=F32)
    same_or_earlier = (jnp.floor(i[None, :] / CHUNK) <= jnp.floor(i[:, None] / CHUNK)).astype(F32)
    dm = jnp.exp(lg[:, None, None] * jnp.abs(i[:, None] - i[None, :])) * same_or_earlier[None]
    qd = jnp.exp(lg[:, None] * (i + 1.0))[:, :, None]
    kd = jnp.exp(lg[:, None] * (RET_SC - 1.0 - i))[:, :, None]
    cd = jnp.exp(lg * RET_SC)[:, None, None]
    return jnp.cos(ang), jnp.sin(ang), dm, qd, kd, cd


def _rope_halves(t, cs, sn):
    h = t.shape[-1] // 2
    t1, t2 = t[:, :h], t[:, h:]
    return jnp.concatenate([t1 * cs - t2 * sn, t2 * cs + t1 * sn], axis=-1)


def _unrope_halves(d, cs, sn):
    h = d.shape[-1] // 2
    d1, d2 = d[:, :h], d[:, h:]
    return jnp.concatenate([d1 * cs + d2 * sn, d2 * cs - d1 * sn], axis=-1)


def _ret_specs(nC, order):
    SC = RET_SC

    def sp(shape, fn):
        return _bs(shape, lambda *g: fn(*order(*g)))

    q = sp((SC, RET_DK), lambda b, h, c: (b * nC + c, h))
    k = sp((SC, RET_DK), lambda b, h, c: (b * nC + c, RET_H + h))
    v = sp((SC, RET_DV), lambda b, h, c: (b * nC + c, RET_H + h))
    g = sp((SC, RET_DV), lambda b, h, c: (b * nC + c, 2 * RET_H + h))
    cs = sp((SC, RET_DK // 2), lambda b, h, c: (c, 0))
    dm = sp((None, SC, SC), lambda b, h, c: (h, 0, 0))
    dv = sp((None, SC, 1), lambda b, h, c: (h, 0, 0))
    cd = sp((None, 1, 1), lambda b, h, c: (h, 0, 0))
    gn = sp((None, 1, RET_DV), lambda b, h, c: (h, 0, 0))
    wide = sp((SC, RET_DV), lambda b, h, c: (b * nC + c, h))
    narrow = sp((SC, RET_DK), lambda b, h, c: (b * nC + c, h))
    st = sp((None, None, None, RET_DK, RET_DV), lambda b, h, c: (b, h, c, 0, 0))
    return dict(q=q, k=k, v=v, g=g, cs=cs, dm=dm, dv=dv, cd=cd, gn=gn, wide=wide, narrow=narrow, st=st)


def _ret_fwd(proj, tabs, gn, B, S, comm=None):
    T = B * S
    nC = S // RET_SC
    cos, sin, dm, qd, kd, cd = tabs
    s = _ret_specs(nC, lambda b, h, c: (b, h, c))

    def body(q_ref, k_ref, v_ref, g_ref, cos_ref, sin_ref, dm_ref, qd_ref, kd_ref, cd_ref, gn_ref,
             o_ref, gt_ref, st_ref, state):
        c = pl.program_id(2)

        @pl.when(c == 0)
        def _():
            state[...] = jnp.zeros_like(state)

        cs, sn = cos_ref[...], sin_ref[...]
        qf = _rope_halves(q_ref[...].astype(F32), cs, sn)
        kf = _rope_halves(k_ref[...].astype(F32), cs, sn) * (RET_DK ** -0.5)
        v = v_ref[...]
        p = _dot(qf.astype(BF16), kf.astype(BF16), NT) * dm_ref[...]
        st = state[...]
        stb = st.astype(BF16)
        st_ref[...] = stb
        o = _dot(p.astype(BF16), v, NN) + _dot((qf * qd_ref[...]).astype(BF16), stb, NN)
        state[...] = st * cd_ref[...] + _dot((kf * kd_ref[...]).astype(BF16), v, TN)
        o_ref[...] = o
        r = lax.rsqrt(jnp.mean(o * o, axis=-1, keepdims=True) + RMS_EPS)
        gf = g_ref[...].astype(F32)
        gt_ref[...] = ((o * r * gn_ref[...]) * (gf * jax.nn.sigmoid(gf))).astype(BF16)

    return _pcall(
        body, name="ret_fwd", grid=(B, RET_H, nC),
        in_specs=[s["q"], s["k"], s["v"], s["g"], s["cs"], s["cs"], s["dm"], s["dv"], s["dv"], s["cd"], s["gn"]],
        out_specs=[s["wide"], s["wide"], s["st"]],
        out_shape=[jax.ShapeDtypeStruct((T, RET_H * RET_DV), F32), jax.ShapeDtypeStruct((T, RET_H * RET_DV), BF16),
                   jax.ShapeDtypeStruct((B, RET_H, nC, RET_DK, RET_DV), BF16)],
        scratch_shapes=[pltpu.VMEM((RET_DK, RET_DV), F32)], sem=("parallel", "parallel", "arbitrary"),
        args=(proj, proj, proj, proj, cos, sin, dm, qd, kd, cd, gn), comm=comm)


def _ret_bwd(proj, o_raw, states, dgt, tabs, gn, B, S, comm=None):
    T = B * S
    nC = S // RET_SC
    cos, sin, dm, qd, kd, cd = tabs
    s = _ret_specs(nC, lambda b, c, h: (b, h, nC - 1 - c))

    def body(q_ref, k_ref, v_ref, g_ref, o_ref, st_ref, d_ref, cos_ref, sin_ref, dm_ref, qd_ref, kd_ref, cd_ref,
             gn_ref, dproj_ref, dgn_ref, dstates):
        b, c, h = pl.program_id(0), pl.program_id(1), pl.program_id(2)
        dstate = dstates.at[h]

        @pl.when(c == 0)
        def _():
            dstate[...] = jnp.zeros_like(dstate)

        @pl.when((b == 0) & (c == 0))
        def _():
            dgn_ref[h] = jnp.zeros((1, RET_DV), F32)

        cs, sn = cos_ref[...], sin_ref[...]
        qf = _rope_halves(q_ref[...].astype(F32), cs, sn)
        kf = _rope_halves(k_ref[...].astype(F32), cs, sn) * (RET_DK ** -0.5)
        v = v_ref[...]
        gnv = gn_ref[h]
        o = o_ref[...]
        r = lax.rsqrt(jnp.mean(o * o, axis=-1, keepdims=True) + RMS_EPS)
        oh = o * r
        gf = g_ref[...].astype(F32)
        sg = jax.nn.sigmoid(gf)
        d = d_ref[...].astype(F32)
        dg = (d * (oh * gnv) * (sg * (1.0 + gf * (1.0 - sg)))).astype(BF16)
        don = d * (gf * sg)
        dgn_ref[h] += jnp.sum(don * oh, axis=0, keepdims=True)
        doh = don * gnv
        dO = (r * (doh - oh * jnp.mean(doh * oh, axis=-1, keepdims=True))).astype(BF16)
        dmv = dm_ref[h]
        qb, kb = qf.astype(BF16), kf.astype(BF16)
        p = (_dot(qb, kb, NT) * dmv).astype(BF16)
        dp = (_dot(dO, v, NT) * dmv).astype(BF16)
        st = st_ref[...]
        dsn = dstate[...]
        dsb = dsn.astype(BF16)
        qdv, kdv = qd_ref[h], kd_ref[h]
        dq = _dot(dp, kb, NN) + _dot(dO, st, NT) * qdv
        dk = _dot(dp, qb, TN) + _dot(v, dsb, NT) * kdv
        dv = _dot(p, dO, TN) + _dot((kf * kdv).astype(BF16), dsb, NN)
        dstate[...] = dsn * cd_ref[h] + _dot((qf * qdv).astype(BF16), dO, TN)
        dq = _unrope_halves(dq, cs, sn).astype(BF16)
        dk = (_unrope_halves(dk, cs, sn) * (RET_DK ** -0.5)).astype(BF16)
        dv = dv.astype(BF16)
        nq, nv = RET_H * RET_DK, RET_H * RET_DV
        for hh in range(RET_H):
            @pl.when(h == hh)
            def _():
                dproj_ref[:, hh * RET_DK:(hh + 1) * RET_DK] = dq
                dproj_ref[:, nq + hh * RET_DK:nq + (hh + 1) * RET_DK] = dk
                dproj_ref[:, 2 * nq + hh * RET_DV:2 * nq + (hh + 1) * RET_DV] = dv
                dproj_ref[:, 2 * nq + nv + hh * RET_DV:2 * nq + nv + (hh + 1) * RET_DV] = dg

    width = 2 * RET_H * (RET_DK + RET_DV)

    def all_heads(*shape):
        return _bs((RET_H,) + shape, lambda b, c, h: (0,) * (1 + len(shape)))

    return _pcall(
        body, name="ret_bwd", grid=(B, nC, RET_H),
        in_specs=[s["q"], s["k"], s["v"], s["g"], s["wide"], s["st"], s["wide"], s["cs"], s["cs"],
                  all_heads(RET_SC, RET_SC), all_heads(RET_SC, 1), all_heads(RET_SC, 1), all_heads(1, 1),
                  all_heads(1, RET_DV)],
        out_specs=[_bs((RET_SC, width), lambda b, c, h: (b * nC + nC - 1 - c, 0)),
                   _bs((RET_H, 1, RET_DV), lambda b, c, h: (0, 0, 0))],
        out_shape=[jax.ShapeDtypeStruct((T, width), BF16), jax.ShapeDtypeStruct((RET_H, 1, RET_DV), F32)],
        scratch_shapes=[pltpu.VMEM((RET_H, RET_DK, RET_DV), F32)], sem=("arbitrary", "arbitrary", "arbitrary"),
        args=(proj, proj, proj, proj, o_raw, states, dgt, cos, sin, dm, qd, kd, cd, gn), comm=comm)


MLA_PAD = 256
MLA_R2 = 2 * MLA_ROPE


def _dup(t):
    return jnp.concatenate([t, t], axis=-1)


def _fold(t):
    return t[..., :MLA_ROPE] + t[..., MLA_ROPE:]


def _mla_tables(S):
    half = MLA_ROPE // 2
    inv = ROPE_THETA ** (-jnp.arange(half, dtype=F32) / half)
    ang = jnp.arange(S).astype(F32)[:, None] * inv[None, :]
    cos, sin, zero = jnp.cos(ang), jnp.sin(ang), jnp.zeros((S, MLA_ROPE), F32)
    return jnp.concatenate([cos, cos, zero], axis=-1), jnp.concatenate([-sin, sin, zero], axis=-1)


def _head_norm_rope(n, r2, gn, gr2, cos, sin, scale):
    ssq = jnp.sum(n * n, axis=-1, keepdims=True) + 0.5 * jnp.sum(r2 * r2, axis=-1, keepdims=True)
    rstd = lax.rsqrt(ssq * (1.0 / MLA_QK) + RMS_EPS)
    yn = n * rstd * gn
    yr = r2 * rstd * gr2
    z = yr * cos + pltpu.roll(yr, MLA_ROPE // 2, 1) * sin
    if scale != 1.0:
        yn, z = yn * scale, z * scale
    return yn, z


def _head_norm_rope_bwd(dn, dz, n, r2, gn, gr2, cos, sin, scale):
    ssq = jnp.sum(n * n, axis=-1, keepdims=True) + 0.5 * jnp.sum(r2 * r2, axis=-1, keepdims=True)
    rstd = lax.rsqrt(ssq * (1.0 / MLA_QK) + RMS_EPS)
    hn, hr = n * rstd, r2 * rstd
    if scale != 1.0:
        dn, dz = dn * scale, dz * scale
    dyr = dz * cos + pltpu.roll(dz * sin, MLA_R2 - MLA_ROPE // 2, 1)
    dgn = jnp.sum(dn * hn, axis=0, keepdims=True)
    dgr = jnp.sum(dyr * hr, axis=0, keepdims=True)
    dhn, dhr = dn * gn, dyr * gr2
    mt = (jnp.sum(dhn * hn, axis=-1, keepdims=True) + jnp.sum(dhr * hr, axis=-1, keepdims=True)) * (1.0 / MLA_QK)
    return rstd * (dhn - hn * mt), rstd * (dhr - 0.5 * hr * mt), dgn, dgr


def _diag_bias():
    i = jnp.arange(ATT_TQ)
    return jnp.where((i[None, :] // CHUNK) <= (i[:, None] // CHUNK), 0.0, MASK_VALUE).astype(F32)


def _store_pair(dst, rows, n, r2):
    dst[rows, :MLA_NOPE] = n.astype(BF16)
    dst[rows, MLA_NOPE:] = r2.astype(BF16)


def _mla_fwd(q_raw, kv, kr, gains, tabs, B, S, comm=None):
    T = B * S
    TQ = ATT_TQ
    nQ = S // TQ
    qgn, qgr, kgn, kgr = gains
    cos, sin = tabs
    scale = MLA_QK ** -0.5

    def body(q_ref, kv_ref, kr_ref, qgn_ref, qgr_ref, kgn_ref, kgr_ref, c_ref, s_ref, bias_ref,
             o_ref, lse_ref, qf_s, kf_s, v_s):
        def prep(t, _):
            rows = pl.ds(pl.multiple_of(t * TQ, TQ), TQ)
            cs, sn = c_ref[rows, :], s_ref[rows, :]
            qn, qr = _head_norm_rope(q_ref[rows, :MLA_NOPE], q_ref[rows, MLA_NOPE:], qgn_ref[...], qgr_ref[...],
                                     cs, sn, scale)
            _store_pair(qf_s, rows, qn, qr)
            kn, krr = _head_norm_rope(kv_ref[rows, :MLA_NOPE], kr_ref[rows, :], kgn_ref[...], kgr_ref[...], cs, sn, 1.0)
            _store_pair(kf_s, rows, kn, krr)
            v_s[rows, :] = kv_ref[rows, MLA_NOPE:].astype(BF16)
            return 0

        lax.fori_loop(0, nQ, prep, 0, unroll=2)
        for i in range(nQ):
            rows = slice(i * TQ, (i + 1) * TQ)
            q = qf_s[rows, :]
            sd = _dot(q, kf_s[rows, :], NT) + bias_ref[...]
            m = jnp.max(sd, axis=-1, keepdims=True)
            if i:
                sl = _dot(q, kf_s[:i * TQ, :], NT)
                m = jnp.maximum(m, jnp.max(sl, axis=-1, keepdims=True))
            pd = jnp.exp(sd - m)
            l = jnp.sum(pd, axis=-1, keepdims=True)
            acc = _dot(pd.astype(BF16), v_s[rows, :], NN)
            if i:
                pl_ = jnp.exp(sl - m)
                l = l + jnp.sum(pl_, axis=-1, keepdims=True)
                acc = acc + _dot(pl_.astype(BF16), v_s[:i * TQ, :], NN)
            o_ref[rows, :] = (acc / l).astype(BF16)
            lse_ref[rows, :] = m + jnp.log(l)

    def vec(n):
        return _bs((1, n), lambda b, h: (0, 0))

    def cols(n):
        return _bs((S, n), lambda b, h: (b, h))

    tab = _bs((S, MLA_R2), lambda b, h: (0, 0))
    return _pcall(
        body, name="mla_fwd", grid=(B, MLA_H),
        in_specs=[cols(MLA_PAD), cols(MLA_NOPE + MLA_V), _bs((S, MLA_R2), lambda b, h: (b, 0)),
                  vec(MLA_NOPE), vec(MLA_R2), vec(MLA_NOPE), vec(MLA_R2), tab, tab,
                  _bs((TQ, TQ), lambda b, h: (0, 0))],
        out_specs=[cols(MLA_V), _bs((None, S, 1), lambda b, h: (h, b, 0)), cols(MLA_PAD), cols(MLA_PAD)],
        out_shape=[jax.ShapeDtypeStruct((T, MLA_H * MLA_V), BF16), jax.ShapeDtypeStruct((MLA_H, T, 1), F32),
                   jax.ShapeDtypeStruct((T, MLA_H * MLA_PAD), BF16), jax.ShapeDtypeStruct((T, MLA_H * MLA_PAD), BF16)],
        scratch_shapes=[pltpu.VMEM((S, MLA_V), BF16)],
        sem=("parallel", "parallel"), args=(q_raw, kv, kr, qgn, qgr, kgn, kgr, cos, sin, _diag_bias()), comm=comm)


def _mla_bwd(q_raw, kv, kr, o, lse, do, qf, kf, gains, tabs, B, S, comm=None):
    T = B * S
    TQ = ATT_TQ
    nQ = S // TQ
    qgn, qgr, kgn, kgr = gains
    cos, sin = tabs
    scale = MLA_QK ** -0.5

    def body(q_ref, kv_ref, kr_ref, o_ref, lse_ref, do_ref, qf_s, kf_s, qgn_ref, qgr_ref, kgn_ref, kgr_ref, c_ref, s_ref,
             bias_ref, dq_ref, dkv_ref, dkr_ref, dqgn_ref, dqgr_ref, dkgn_ref, dkgr_ref,
             v_s, dl_s, dq_s, dk_s, dv_s):
        b, h = pl.program_id(0), pl.program_id(1)

        def blk(t):
            return pl.ds(pl.multiple_of(t * TQ, TQ), TQ)

        def prep(t, _):
            rows = blk(t)
            v_s[rows, :] = kv_ref[rows, MLA_NOPE:].astype(BF16)
            dl_s[rows, :] = jnp.sum(do_ref[rows, :].astype(F32) * o_ref[rows, :].astype(F32), axis=-1, keepdims=True)
            dk_s[rows, :] = jnp.zeros((TQ, MLA_PAD), F32)
            dv_s[rows, :] = jnp.zeros((TQ, MLA_V), F32)
            return 0

        lax.fori_loop(0, nQ, prep, 0, unroll=2)

        gqn, gqr = jnp.zeros((1, MLA_NOPE), F32), jnp.zeros((1, MLA_R2), F32)
        for i in range(nQ):
            rows = slice(i * TQ, (i + 1) * TQ)
            q, doi, lse_i, dl_i = qf_s[rows, :], do_ref[rows, :], lse_ref[rows, :], dl_s[rows, :]

            def part(cols, bias):
                k, v = kf_s[cols, :], v_s[cols, :]
                s = _dot(q, k, NT)
                if bias is not None:
                    s = s + bias
                p = jnp.exp(s - lse_i)
                ds = (p * (_dot(doi, v, NT) - dl_i)).astype(BF16)
                dk_s[cols, :] += _dot(ds, q, TN)
                dv_s[cols, :] += _dot(p.astype(BF16), doi, TN)
                return _dot(ds, k, NN)

            dq = part(rows, bias_ref[...])
            if i:
                dq = dq + part(slice(0, i * TQ), None)
            dq_s[...] = dq
            dqn, dqr, a0, a1 = _head_norm_rope_bwd(dq_s[:, :MLA_NOPE], dq_s[:, MLA_NOPE:], q_ref[rows, :MLA_NOPE],
                                                   q_ref[rows, MLA_NOPE:], qgn_ref[...], qgr_ref[...],
                                                   c_ref[rows, :], s_ref[rows, :], scale)
            _store_pair(dq_ref, rows, dqn, dqr)
            gqn, gqr = gqn + a0, gqr + a1

        def post(t, carry):
            rows = blk(t)
            dkn, dkr, a2, a3 = _head_norm_rope_bwd(dk_s[rows, :MLA_NOPE], dk_s[rows, MLA_NOPE:],
                                                   kv_ref[rows, :MLA_NOPE], kr_ref[rows, :], kgn_ref[...], kgr_ref[...],
                                                   c_ref[rows, :], s_ref[rows, :], 1.0)
            dkv_ref[rows, :MLA_NOPE] = dkn.astype(BF16)
            dkv_ref[rows, MLA_NOPE:] = dv_s[rows, :].astype(BF16)

            @pl.when(h == 0)
            def _():
                dkr_ref[rows, :] = dkr

            @pl.when(h > 0)
            def _():
                dkr_ref[rows, :] += dkr

            return carry[0] + a2, carry[1] + a3

        gkn, gkr = lax.fori_loop(0, nQ, post, (jnp.zeros((1, MLA_NOPE), F32), jnp.zeros((1, MLA_R2), F32)), unroll=2)
        first = (b == 0) & (h == 0)

        @pl.when(first)
        def _():
            dqgn_ref[...] = gqn
            dqgr_ref[...] = gqr
            dkgn_ref[...] = gkn
            dkgr_ref[...] = gkr

        @pl.when(jnp.logical_not(first))
        def _():
            dqgn_ref[...] += gqn
            dqgr_ref[...] += gqr
            dkgn_ref[...] += gkn
            dkgr_ref[...] += gkr

    def vec(n):
        return _bs((1, n), lambda b, h: (0, 0))

    def cols(n):
        return _bs((S, n), lambda b, h: (b, h))

    tab = _bs((S, MLA_R2), lambda b, h: (0, 0))
    return _pcall(
        body, name="mla_bwd", grid=(B, MLA_H),
        in_specs=[cols(MLA_PAD), cols(MLA_NOPE + MLA_V), _bs((S, MLA_R2), lambda b, h: (b, 0)), cols(MLA_V),
                  _bs((None, S, 1), lambda b, h: (h, b, 0)), cols(MLA_V), cols(MLA_PAD), cols(MLA_PAD),
                  vec(MLA_NOPE), vec(MLA_R2), vec(MLA_NOPE), vec(MLA_R2), tab, tab,
                  _bs((TQ, TQ), lambda b, h: (0, 0))],
        out_specs=[cols(MLA_PAD), cols(MLA_NOPE + MLA_V), _bs((S, MLA_R2), lambda b, h: (b, 0)),
                   vec(MLA_NOPE), vec(MLA_R2), vec(MLA_NOPE), vec(MLA_R2)],
        out_shape=[jax.ShapeDtypeStruct((T, MLA_H * MLA_PAD), BF16),
                   jax.ShapeDtypeStruct((T, MLA_H * (MLA_NOPE + MLA_V)), BF16),
                   jax.ShapeDtypeStruct((T, MLA_R2), F32), jax.ShapeDtypeStruct((1, MLA_NOPE), F32),
                   jax.ShapeDtypeStruct((1, MLA_R2), F32), jax.ShapeDtypeStruct((1, MLA_NOPE), F32),
                   jax.ShapeDtypeStruct((1, MLA_R2), F32)],
        scratch_shapes=[pltpu.VMEM((S, MLA_V), BF16), pltpu.VMEM((S, 1), F32), pltpu.VMEM((TQ, MLA_PAD), F32),
                        pltpu.VMEM((S, MLA_PAD), F32), pltpu.VMEM((S, MLA_V), F32)],
        sem=("arbitrary", "arbitrary"),
        args=(q_raw, kv, kr, o, lse, do, qf, kf, qgn, qgr, kgn, kgr, cos, sin, _diag_bias()), comm=comm)


def _adamw(name, recvs, w, m, v, tr=None, comm=None):
    n, R, C = recvs[0].shape
    L = len(recvs)
    Lw, Rw, _ = w.shape
    assert Lw * Rw == L * R and w.shape[2] == C
    tr = R if tr is None else tr
    assert R % tr == 0 and Rw % tr == 0
    per = R // tr
    per_w = Rw // tr
    c1 = 1.0 - ADAM_B1 ** ADAM_STEP
    c2 = 1.0 - ADAM_B2 ** ADAM_STEP

    def body(*refs):
        r_refs = refs[:L]
        w_ref, m_ref, v_ref, g_ref, d_ref, nm_ref, nv_ref = refs[L:]
        layer = pl.program_id(0) // per

        def total(r_ref):
            t = r_ref[0].astype(F32)
            for k in range(1, n):
                t = t + r_ref[k].astype(F32)
            return t

        g = total(r_refs[0]) if L == 1 else lax.switch(layer, [functools.partial(total, r) for r in r_refs])
        mm = ADAM_B1 * m_ref[...] + (1.0 - ADAM_B1) * g
        vv = ADAM_B2 * v_ref[...] + (1.0 - ADAM_B2) * (g * g)
        g_ref[...] = g
        nm_ref[...] = mm
        nv_ref[...] = vv
        d_ref[...] = -ADAM_LR * ((mm / c1) / (jnp.sqrt(vv / c2) + ADAM_EPS) + ADAM_WD * w_ref[...])

    blk = _bs((None, tr, C), lambda i: (i // per_w, i % per_w, 0))
    r_specs = [_bs((n, tr, C), functools.partial(lambda l, i: (0, jnp.clip(i - l * per, 0, per - 1), 0), l))
               for l in range(L)]
    outs, got = _pcall(body, name=name, grid=(L * per,), in_specs=r_specs + [blk, blk, blk], out_specs=[blk] * 4,
                       out_shape=[jax.ShapeDtypeStruct(w.shape, F32)] * 4, scratch_shapes=[], sem=("arbitrary",),
                       args=(*recvs, w, m, v), comm=comm)
    return outs if comm is None else (outs, got)


def _sum8(name, a):
    n, R, C = a.shape

    def body(a_ref, o_ref):
        s = a_ref[0]
        for k in range(1, n):
            s = s + a_ref[k]
        o_ref[...] = s

    return pl.pallas_call(body, name=name, out_shape=jax.ShapeDtypeStruct((R, C), a.dtype))(a)


def _sds(shape, dt):
    return jax.ShapeDtypeStruct(shape, dt)


def _norm_proj(name, x, g, w, o_spec, out_shape, tm=1024, comm=None):
    T, K = x.shape
    J, _, n = w.shape

    def body(x_ref, g_ref, w_ref, o_ref, h_ref, hs):
        @pl.when(pl.program_id(1) == 0)
        def _():
            xf = x_ref[...]
            r = lax.rsqrt(jnp.mean(xf * xf, axis=-1, keepdims=True) + RMS_EPS)
            h = (xf * r * g_ref[...]).astype(BF16)
            hs[...] = h
            h_ref[...] = h

        o_ref[...] = _dot(hs[...], w_ref[...], NN).astype(o_ref.dtype)

    row = _bs((tm, K), lambda m, j: (m, 0))
    (out, h), got = _pcall(
        body, name=name, grid=(T // tm, J),
        in_specs=[row, _bs((1, K), lambda m, j: (0, 0)), _bs((None, K, n), lambda m, j: (j, 0, 0))],
        out_specs=[o_spec, row], out_shape=[out_shape, _sds((T, K), BF16)], scratch_shapes=[pltpu.VMEM((tm, K), BF16)],
        sem=("parallel", "arbitrary"), args=(x, g, w), comm=comm)
    return out, h, got


def _proj_shared_dx(name, d, w, tm=1024, comm=None):
    J, T, n = d.shape
    K = w.shape[1]
    return _mm(name, d, w, grid=(T // tm, J), a_spec=_bs((None, tm, n), lambda m, k: (k, m, 0)),
               b_spec=_bs((None, K, n), lambda m, k: (k, 0, 0)), o_spec=_bs((tm, K), lambda m, k: (m, 0)),
               out_shape=_sds((T, K), BF16), dims=NT, kax=1, acc_shape=(tm, K), comm=comm)


def _out_proj(name, a, w, res, tm=512):
    J, T, k = a.shape
    N = w.shape[2]
    return _mm(name, a, w, grid=(T // tm,), a_spec=_bs((J, tm, k), lambda m: (0, m, 0)),
               b_spec=_bs((J, k, N), lambda m: (0, 0, 0)), o_spec=_bs((tm, N), lambda m: (m, 0)),
               out_shape=_sds((T, N), F32), dims=NN, res=res, res_spec=_bs((tm, N), lambda m: (m, 0)), jb=J)


def _out_proj_dx(name, dx, w, tm=1024, comm=None):
    T, N = dx.shape
    J, k, _ = w.shape
    return _mm(name, dx, w, grid=(T // tm, J), a_spec=_bs((tm, N), lambda m, j: (m, 0)),
               b_spec=_bs((None, k, N), lambda m, j: (j, 0, 0)), o_spec=_bs((None, tm, k), lambda m, j: (j, m, 0)),
               out_shape=_sds((J, T, k), BF16), dims=NT, comm=comm)


def _out_proj_dw(name, a, dx, tt=1024, comm=None):
    J, T, k = a.shape
    N = dx.shape[1]
    tt = min(tt, T)
    return _mm(name, a, dx, grid=(J, T // tt), a_spec=_bs((None, tt, k), lambda j, t: (j, t, 0)),
               b_spec=_bs((tt, N), lambda j, t: (t, 0)), o_spec=_bs((None, k, N), lambda j, t: (j, 0, 0)),
               out_shape=_sds((J, k, N), BF16), dims=TN, kax=1, acc_shape=(k, N), comm=comm)


def _dense(name, a, b, dims, out_dtype, tm=512, res=None, comm=None):
    if dims == TN:
        T, K = a.shape
        N = b.shape[1]
        return _mm(name, a, b, grid=(T // tm,), a_spec=_bs((tm, K), lambda t: (t, 0)),
                   b_spec=_bs((tm, N), lambda t: (t, 0)), o_spec=_bs((K, N), lambda t: (0, 0)),
                   out_shape=_sds((K, N), out_dtype), dims=TN, kax=0, acc_shape=(K, N), comm=comm)
    M, K = a.shape
    N = b.shape[1] if dims == NN else b.shape[0]
    row = _bs((tm, N), lambda m: (m, 0))
    return _mm(name, a, b, grid=(M // tm,), a_spec=_bs((tm, K), lambda m: (m, 0)), b_spec=_bs(b.shape, lambda m: (0, 0)),
               o_spec=row, out_shape=_sds((M, N), out_dtype), dims=dims, res=res,
               res_spec=row if res is not None else None, comm=comm)


def _bf16(x):
    return x.astype(BF16)


def _ffn_fwd(i, x, norm_g, w_in, cw, cb, w_out, B, S, comm_in=None, tgt=None):
    T = x.shape[0]
    u, h, got = _norm_proj(f"ffn{i}_in", x, norm_g, w_in, _bs((None, 1024, FSH), lambda m, j: (j, m, 0)),
                           _sds((NDEV, T, FSH), BF16), comm=comm_in)
    u4 = u.reshape(2, 4, T, FSH)
    gt, gcb = _convffn_fwd(f"ffn{i}_gate", u4, cw, cb, B, S)
    if tgt is None:
        y = _out_proj(f"ffn{i}_out", gt, w_out, x)
    else:
        y = _out_proj_loss(f"ffn{i}_out_loss", gt, w_out, x, tgt)
    return y, (x, h, u4, gt, gcb), got


def _ffn_bwd(i, dy, dyb, saved, norm_g, w_in, cw, w_out, B, S, first_half_early, riders=(None, None)):
    x, h, u4, gt, gcb = saved
    dgt = _out_proj_dx(f"ffn{i}_out_dx", dyb, w_out, comm=riders[0])
    dw_out = _out_proj_dw(f"ffn{i}_out_dw", gt, dyb, comm=riders[1])
    dgt, got0 = dgt if riders[0] is not None else (dgt, None)
    dw_out, got1 = dw_out if riders[1] is not None else (dw_out, None)
    dw_out = dw_out.reshape(NDEV, FSH // 2, D_MODEL)
    gate_comm = _Exchange([dw_out]) if first_half_early else None
    (du4, dcw, dcb), got = _convffn_bwd(f"ffn{i}_gate_bwd", u4, gcb, cw, dgt, B, S, comm=gate_comm)
    du = du4.reshape(NDEV, du4.shape[2], FSH)
    dw_in = _out_proj_dw(f"ffn{i}_in_dw", du, h, tt=2048)
    r_in = None
    if first_half_early:
        (r_out,) = got
        dh, (r_in,) = _proj_shared_dx(f"ffn{i}_in_dx", du, w_in, comm=_Exchange([dw_in], rows=[(0, FSH // 2)]))
    else:
        dh, (r_out,) = _proj_shared_dx(f"ffn{i}_in_dx", du, w_in, comm=_Exchange([dw_out]))
    dx, dgn, dxb = _rms_bwd(f"ffn{i}_norm_bwd", x, norm_g, dh, dres=dy, also_bf16=True)
    return dx, dxb, dict(w_in=dw_in, norm=dgn, cw=dcw, cb=dcb), r_out, r_in, (got0, got1)


def kernel(x, ret_norm, ret_w_in, ret_gn, ret_w_out, mla_norm, mla_w_in, mla_q_norm, mla_w_qb, mla_kv_norm, mla_w_kvb, mla_q_head_norm, mla_k_head_norm, mla_w_out, ffn_norm, ffn_w_in, ffn_conv_w, ffn_conv_b, ffn_w_out, loss_target, m_ret_norm, m_ret_w_in, m_ret_gn, m_ret_w_out, m_mla_norm, m_mla_w_in, m_mla_q_norm, m_mla_w_qb, m_mla_kv_norm, m_mla_w_kvb, m_mla_q_head_norm, m_mla_k_head_norm, m_mla_w_out, m_ffn_norm, m_ffn_w_in, m_ffn_conv_w, m_ffn_conv_b, m_ffn_w_out, v_ret_norm, v_ret_w_in, v_ret_gn, v_ret_w_out, v_mla_norm, v_mla_w_in, v_mla_q_norm, v_mla_w_qb, v_mla_kv_norm, v_mla_w_kvb, v_mla_q_head_norm, v_mla_k_head_norm, v_mla_w_out, v_ffn_norm, v_ffn_w_in, v_ffn_conv_w, v_ffn_conv_b, v_ffn_w_out):
    B, S, D = x.shape
    T = B * S
    w = dict(ret_norm=ret_norm, ret_w_in=ret_w_in, ret_gn=ret_gn, ret_w_out=ret_w_out, mla_norm=mla_norm,
             mla_w_in=mla_w_in, mla_q_norm=mla_q_norm, mla_w_qb=mla_w_qb, mla_kv_norm=mla_kv_norm, mla_w_kvb=mla_w_kvb,
             mla_q_head_norm=mla_q_head_norm, mla_k_head_norm=mla_k_head_norm, mla_w_out=mla_w_out, ffn_norm=ffn_norm,
             ffn_w_in=ffn_w_in, ffn_conv_w=ffn_conv_w, ffn_conv_b=ffn_conv_b, ffn_w_out=ffn_w_out)
    mom = dict(ret_norm=m_ret_norm, ret_w_in=m_ret_w_in, ret_gn=m_ret_gn, ret_w_out=m_ret_w_out, mla_norm=m_mla_norm,
               mla_w_in=m_mla_w_in, mla_q_norm=m_mla_q_norm, mla_w_qb=m_mla_w_qb, mla_kv_norm=m_mla_kv_norm,
               mla_w_kvb=m_mla_w_kvb, mla_q_head_norm=m_mla_q_head_norm, mla_k_head_norm=m_mla_k_head_norm,
               mla_w_out=m_mla_w_out, ffn_norm=m_ffn_norm, ffn_w_in=m_ffn_w_in, ffn_conv_w=m_ffn_conv_w,
               ffn_conv_b=m_ffn_conv_b, ffn_w_out=m_ffn_w_out)
    var = dict(ret_norm=v_ret_norm, ret_w_in=v_ret_w_in, ret_gn=v_ret_gn, ret_w_out=v_ret_w_out, mla_norm=v_mla_norm,
               mla_w_in=v_mla_w_in, mla_q_norm=v_mla_q_norm, mla_w_qb=v_mla_w_qb, mla_kv_norm=v_mla_kv_norm,
               mla_w_kvb=v_mla_w_kvb, mla_q_head_norm=v_mla_q_head_norm, mla_k_head_norm=v_mla_k_head_norm,
               mla_w_out=v_mla_w_out, ffn_norm=v_ffn_norm, ffn_w_in=v_ffn_w_in, ffn_conv_w=v_ffn_conv_w,
               ffn_conv_b=v_ffn_conv_b, ffn_w_out=v_ffn_w_out)
    BIG = ["ret_w_in", "ret_w_out", "mla_w_in", "mla_w_qb", "mla_w_kvb", "mla_w_out", "ffn_w_in", "ffn_w_out"]
    REPL = ["ret_norm", "ffn_norm", "mla_q_head_norm", "mla_k_head_norm", "ffn_conv_b"]
    SHARDED_SMALL = ["ffn_conv_w", "ret_gn", "mla_norm", "mla_q_norm", "mla_kv_norm"]
    dev = _idx(_place())

    def blk16(k, i=0):
        return _bf16(w[k][i])

    small_vec = jnp.concatenate([w[k].reshape(-1) for k in SHARDED_SMALL])
    n_small = small_vec.shape[0]
    small_vec = jnp.pad(small_vec, (0, 4096 - n_small)).reshape(32, 128)
    Wret_in, sg = _comm_call("gather_ret_w_in", _Gather([blk16("ret_w_in"), small_vec], parts=2))
    sg = sg.reshape(NDEV, 4096)
    o0 = 0
    conv_w_full = sg[:, o0:o0 + 2112].reshape(NDEV, 2, 3, 352).transpose(1, 2, 0, 3).reshape(2, 3, FFN)
    o0 += 2112
    ret_gn_full = sg[:, o0:o0 + 256].reshape(NDEV, RET_H, 64).transpose(1, 0, 2).reshape(RET_H, 1, RET_DV)
    o0 += 256
    mla_norm_full = sg[:, o0:o0 + 128].reshape(1, D)
    o0 += 128
    q_norm_full = sg[:, o0:o0 + 48].reshape(1, MLA_QR)
    o0 += 48
    kv_norm_full = sg[:, o0:o0 + 32].reshape(1, MLA_KVR)

    cw = [conv_w_full[i].reshape(3, 4, FSH).transpose(1, 0, 2) for i in range(2)]
    cb = [ffn_conv_b[i].reshape(4, 1, FSH) for i in range(2)]
    fnorm = [ffn_norm[i].reshape(1, D) for i in range(2)]
    rtabs = _ret_tables(S)
    mtabs = _mla_tables(S)
    qh, kh = mla_q_head_norm.reshape(1, MLA_QK), mla_k_head_norm.reshape(1, MLA_QK)
    gains = (qh[:, :MLA_NOPE], _dup(qh[:, MLA_NOPE:]), kh[:, :MLA_NOPE], _dup(kh[:, MLA_NOPE:]))

    x0 = x.reshape(T, D)
    tgt = loss_target.reshape(T, D)
    proj, h0, (Wret_out, Wffn_out0) = _norm_proj(
        "ret_in", x0, ret_norm.reshape(1, D), Wret_in, _bs((1024, 768), lambda m, j: (m, j)), _sds((T, 6144), BF16),
        comm=_Gather([blk16("ret_w_out"), blk16("ffn_w_out", 0)]))
    Wret_out = Wret_out.reshape(RET_H * RET_DV, D)
    Wffn_out0 = Wffn_out0.reshape(4, FSH, D)
    (o_raw, rgt, states), (Wffn_in0,) = _ret_fwd(proj, rtabs, ret_gn_full, B, S, comm=_Gather([blk16("ffn_w_in", 0)]))
    x1 = _dense("ret_out", rgt, Wret_out, NN, F32, res=x0)
    MLA_W = ["mla_w_in", "mla_w_qb", "mla_w_kvb", "mla_w_out"]
    x2, ffn0_saved, got = _ffn_fwd(0, x1, fnorm[0], Wffn_in0, cw[0], cb[0], Wffn_out0, B, S,
                                   comm_in=_Gather([blk16(k) for k in MLA_W]))
    Wmla_in = got[0].reshape(D, MLA_QR + MLA_KVR + MLA_ROPE)
    Wq, Wkv, Wkr = Wmla_in[:, :MLA_QR], Wmla_in[:, MLA_QR:MLA_QR + MLA_KVR], Wmla_in[:, MLA_QR + MLA_KVR:]
    Wqb, Wkvb, Wmla_out = got[1:]

    def normed(name, xin, gain, wmat):
        n = wmat.shape[1]
        out, hn, _ = _norm_proj(name, xin, gain, wmat[None], _bs((1024, n), lambda m, j: (m, 0)), _sds((T, n), F32))
        return out, hn

    c_all, h2 = normed("mla_in", x2, mla_norm_full, jnp.concatenate([Wq, Wkv, _dup(Wkr)], axis=1))
    c_q, c_kv, k_rope = c_all[:, :MLA_QR], c_all[:, MLA_QR:MLA_QR + MLA_KVR], c_all[:, MLA_QR + MLA_KVR:]
    Wqb2 = jnp.concatenate([Wqb, Wqb[:, :, MLA_NOPE:]], axis=2).transpose(1, 0, 2).reshape(MLA_QR, MLA_H * MLA_PAD)
    Wkvb2 = Wkvb.transpose(1, 0, 2).reshape(MLA_KVR, MLA_H * (MLA_NOPE + MLA_V))
    Wmla_out2 = Wmla_out.reshape(D, D)
    q_raw, cqn = normed("mla_qb", c_q, q_norm_full, Wqb2)
    kvh, ckvn = normed("mla_kvb", c_kv, kv_norm_full, Wkvb2)
    (att, lse, qf, kf), (Wffn_in1, Wffn_out1) = _mla_fwd(
        q_raw, kvh, k_rope, gains, mtabs, B, S, comm=_Gather([blk16("ffn_w_in", 1), blk16("ffn_w_out", 1)]))
    Wffn_out1 = Wffn_out1.reshape(4, FSH, D)
    x3 = _dense("mla_out", att, Wmla_out2, NN, F32, res=x2)
    (dy, colsq, dyb), ffn1_saved, _ = _ffn_fwd(1, x3, fnorm[1], Wffn_in1, cw[1], cb[1], Wffn_out1, B, S, tgt=tgt)
    loss_part = 0.5 * jnp.sum(colsq) / D

    dx3, dx3b, gf1, r_ffn1_out, _, _ = _ffn_bwd(1, dy, dyb, ffn1_saved, fnorm[1], Wffn_in1, cw[1], Wffn_out1, B, S,
                                                first_half_early=False)
    datt = _dense("mla_out_dx", dx3b, Wmla_out2, NT, BF16)
    fh = FSH // 2
    (dq_raw, dkvh, dkr, dqgn, dqgr, dkgn, dkgr), (r_ffn1_in_a, r_ffn1_in_b) = _mla_bwd(
        q_raw, kvh, k_rope, att, lse, datt, qf, kf, gains, mtabs, B, S,
        comm=_Exchange([gf1["w_in"], gf1["w_in"]], rows=[(0, fh), (fh, fh)]))
    dcqn = _dense("mla_qb_dx", dq_raw, Wqb2, NT, F32, tm=1024)
    dckvn = _dense("mla_kvb_dx", dkvh, Wkvb2, NT, F32, tm=1024)
    dcq, dg_qn = _rms_bwd("mla_q_norm_bwd", c_q, q_norm_full, dcqn)
    dckv, dg_kvn = _rms_bwd("mla_kv_norm_bwd", c_kv, kv_norm_full, dckvn)
    dqgr, dkgr = _fold(dqgr), _fold(dkgr)
    dproj2 = _bf16(jnp.concatenate([dcq, dckv, _fold(dkr)], axis=-1))
    dh2 = _dense("mla_in_dx", dproj2, Wmla_in, NT, BF16)
    dx2, dg_mla_norm, dx2b = _rms_bwd("mla_norm_bwd", x2, mla_norm_full, dh2, dres=dx3, also_bf16=True)
    dWmla_out = _dense("mla_out_dw", att, dx3b, TN, BF16, tm=1024).reshape(NDEV, MLA_V, D)
    dWqb = _dense("mla_qb_dw", dq_raw, cqn, TN, BF16, tm=1024).reshape(MLA_H, MLA_PAD, MLA_QR)
    dWqb = jnp.concatenate([dWqb[:, :MLA_NOPE], dWqb[:, MLA_NOPE:MLA_QK] + dWqb[:, MLA_QK:]], axis=1)
    dWkvb = _dense("mla_kvb_dw", ckvn, dkvh, TN, BF16, tm=1024)
    dWkvb = dWkvb.reshape(MLA_KVR, MLA_H, MLA_NOPE + MLA_V).transpose(1, 0, 2)
    dWmla_in = _dense("mla_in_dw", h2, dproj2, TN, BF16).reshape(NDEV, 128, 704)

    dx1, dx1b, gf0, r_ffn0_out, r_ffn0_in_a, (r_mla_a, r_mla_b) = _ffn_bwd(
        0, dx2, dx2b, ffn0_saved, fnorm[0], Wffn_in0, cw[0], Wffn_out0, B, S, first_half_early=True,
        riders=(_Exchange([dWmla_out, dWqb]), _Exchange([dWkvb, dWmla_in])))
    r_mla = [*r_mla_a, *r_mla_b]
    drgt = _dense("ret_out_dx", dx1b, Wret_out, NT, BF16)
    dWret_out = _dense("ret_out_dw", rgt, dx1b, TN, BF16, tm=1024).reshape(NDEV, 256, D)
    (dproj, dgn_ret), (r_ffn0_in_b, r_ret_out) = _ret_bwd(
        proj, o_raw, states, drgt, rtabs, ret_gn_full, B, S,
        comm=_Exchange([gf0["w_in"], dWret_out], rows=[(fh, fh), None]))

    tt, hk, er = min(2048, T), D // 2, D // 8

    def eighths(arr, *which):
        return _Exchange([arr] * len(which), rows=[(w * er, er) for w in which])

    def ret_in_dw(name, half, comm):
        return _mm(name, h0, dproj, grid=(NDEV, T // tt), a_spec=_bs((tt, hk), lambda j, t: (t, half)),
                   b_spec=_bs((tt, 768), lambda j, t: (t, j)), o_spec=_bs((None, hk, 768), lambda j, t: (j, 0, 0)),
                   out_shape=_sds((NDEV, hk, 768), BF16), dims=TN, kax=1, acc_shape=(hk, 768), comm=comm)

    dW_top = ret_in_dw("ret_in_dw_top", 0, None)
    dW_bot, r_e01 = ret_in_dw("ret_in_dw_bot", 1, eighths(dW_top, 0, 1))
    both = _Exchange([dW_top, dW_top, dW_bot, dW_bot, dW_bot], rows=[(w * er, er) for w in (2, 3, 0, 1, 2)])
    dh0, r_e23456 = _mm(
        "ret_in_dx", dproj, Wret_in, grid=(T // 1024, NDEV), a_spec=_bs((1024, 768), lambda m, k: (m, k)),
        b_spec=_bs((None, D, 768), lambda m, k: (k, 0, 0)), o_spec=_bs((1024, D), lambda m, k: (m, 0)),
        out_shape=_sds((T, D), BF16), dims=NT, kax=1, acc_shape=(1024, D), comm=both)
    (dx0, dg_ret_norm), r_e7 = _rms_bwd("ret_norm_bwd", x0, ret_norm.reshape(1, D), dh0, dres=dx1,
                                        comm=eighths(dW_bot, 3))
    grad_x = dx0.reshape(B, S, D)
    received = dict(ret_w_in=[*r_e01, *r_e23456, *r_e7], ret_w_out=[r_ret_out], mla_w_out=[r_mla[0]], mla_w_qb=[r_mla[1]],
                    mla_w_kvb=[r_mla[2]], mla_w_in=[r_mla[3]],
                    ffn_w_in=[r_ffn0_in_a, r_ffn0_in_b, r_ffn1_in_a, r_ffn1_in_b], ffn_w_out=[r_ffn0_out, r_ffn1_out])

    dconv_w = jnp.stack([g_["cw"].transpose(1, 0, 2).reshape(3, FFN) for g_ in (gf0, gf1)])
    dconv_b = jnp.stack([g_["cb"].reshape(FFN) for g_ in (gf0, gf1)])
    small_parts = [dg_ret_norm, gf0["norm"], gf1["norm"], dg_mla_norm, dg_qn, dg_kvn, dqgn, dqgr, dkgn, dkgr, dgn_ret,
                   dconv_w, dconv_b, loss_part]
    small_g = jnp.concatenate([p.reshape(-1) for p in small_parts])
    n_grads = small_g.shape[0] - 1
    small_g = jnp.pad(small_g, (0, 240 * 128 - small_g.shape[0])).reshape(240, 128)
    small_all = _comm_call("gather_small_grads", _Gather([small_g]))[0]
    sred = _sum8("sum_small_grads", small_all).reshape(-1)
    loss = sred[n_grads]

    def take(n):
        nonlocal off
        out = sred[off:off + n]
        off += n
        return out

    off = 0
    g_small = dict(ret_norm=take(D).reshape(1, D), ffn_norm=take(2 * D).reshape(2, D), mla_norm=take(D),
                   mla_q_norm=take(MLA_QR), mla_kv_norm=take(MLA_KVR))
    g_small["mla_q_head_norm"] = take(MLA_QK).reshape(1, MLA_QK)
    g_small["mla_k_head_norm"] = take(MLA_QK).reshape(1, MLA_QK)
    g_small["ret_gn"] = take(RET_H * RET_DV).reshape(1, RET_H, RET_DV)
    g_small["ffn_conv_w"] = take(2 * 3 * FFN).reshape(2, 3, FFN)
    g_small["ffn_conv_b"] = take(2 * FFN).reshape(2, FFN)
    g_small["mla_norm"] = lax.dynamic_slice(g_small["mla_norm"], (dev * 128,), (128,)).reshape(1, 128)
    g_small["mla_q_norm"] = lax.dynamic_slice(g_small["mla_q_norm"], (dev * 48,), (48,)).reshape(1, 48)
    g_small["mla_kv_norm"] = lax.dynamic_slice(g_small["mla_kv_norm"], (dev * 32,), (32,)).reshape(1, 32)
    g_small["ret_gn"] = lax.dynamic_slice(g_small["ret_gn"], (0, 0, dev * 64), (1, RET_H, 64))
    g_small["ffn_conv_w"] = lax.dynamic_slice(g_small["ffn_conv_w"], (0, 0, dev * 352), (2, 3, 352))

    grads, delta, new_m, new_v = {}, {}, {}, {}
    for k in BIG:
        rcs = received[k]
        tr = max(t for t in range(16, 257, 16) if rcs[0].shape[1] % t == 0)
        flip = (lambda t: t.transpose(0, 2, 1)) if k in ("ffn_w_in", "mla_w_qb") else (lambda t: t)
        res = _adamw(f"adamw_{k}", rcs, flip(w[k]), flip(mom[k]), flip(var[k]), tr=tr)
        grads[k], delta[k], new_m[k], new_v[k] = (flip(t) for t in res)
    SMALL = REPL + SHARDED_SMALL

    def pack(d):
        vflat = jnp.concatenate([d[k].reshape(-1) for k in SMALL])
        return jnp.pad(vflat, (0, 96 * 128 - vflat.shape[0])).reshape(1, 96, 128)

    ps = _adamw("adamw_small", [pack(g_small)], pack(w), pack(mom), pack(var))
    off = 0
    for k in SMALL:
        n = w[k].size
        grads[k], delta[k], new_m[k], new_v[k] = (t.reshape(-1)[off:off + n].reshape(w[k].shape) for t in ps)
        off += n
    names = list(w)
    return (loss, grad_x, *[grads[k] for k in names], *[delta[k] for k in names], *[new_m[k] for k in names],
            *[new_v[k] for k in names])
```

```python
import functools

import jax
import jax.numpy as jnp
from jax import lax
from jax.experimental import pallas as pl
from jax.experimental.pallas import tpu as pltpu

F32, BF16 = jnp.float32, jnp.bfloat16

NDEV = 8
D_MODEL = 1024
CHUNK = 64
RMS_EPS = 1e-6
ROPE_THETA = 10000.0
RET_H, RET_DK, RET_DV = 4, 256, 512
RET_SC = 256
MLA_H, MLA_QR, MLA_KVR = 8, 384, 256
MLA_NOPE, MLA_ROPE, MLA_V = 128, 64, 128
MLA_QK = MLA_NOPE + MLA_ROPE
MASK_VALUE = -1e30
FFN = 2816
FSH = FFN * 2 // NDEV
ATT_TQ = 256
ADAM_LR, ADAM_B1, ADAM_B2, ADAM_EPS, ADAM_WD, ADAM_STEP = 0.001, 0.9, 0.999, 1e-08, 0.01, 10
MESH = pl.DeviceIdType.MESH
VMEM_LIMIT = 56 * 2 ** 20


def _cp(sem):
    return pltpu.CompilerParams(dimension_semantics=sem, vmem_limit_bytes=VMEM_LIMIT)


def _dot(a, b, dims):
    return lax.dot_general(a, b, (dims, ((), ())), preferred_element_type=F32)


NN = ((1,), (0,))
NT = ((1,), (1,))
TN = ((0,), (0,))


def _place():
    return lax.axis_index("x"), lax.axis_index("y"), lax.axis_index("c")


def _idx(d):
    return 4 * d[0] + 2 * d[1] + d[2]


ANY = pl.BlockSpec(memory_space=pl.ANY)


class _Gather:
    def __init__(self, arrs, parts=1):
        self.srcs = list(arrs)
        self.parts = parts
        self.nsem = len(arrs) * parts
        self.out_shape = [jax.ShapeDtypeStruct((NDEV,) + a.shape, a.dtype) for a in arrs]

    def _copies(self, ins, outs, send, recv, loc):
        n = self.nsem
        x, y, c = _place()
        me, sib = (x, y, c), (x, y, 1 - c)
        chips = [(1 - x, y), (x, 1 - y), (1 - x, 1 - y)]

        def piece(ref, v, *lead):
            a, p = divmod(v, self.parts)
            if self.parts > 1:
                rows = self.srcs[a].shape[0] // self.parts
                lead = (*lead, pl.ds(p * rows, rows))
            return ref[a].at[lead] if lead else ref[a]

        def cp(a, k, block, to, src=None):
            dst = piece(outs, a, _idx(block))
            return pltpu.make_async_remote_copy(src_ref=dst if src is None else src, dst_ref=dst,
                                                send_sem=send.at[a, k], recv_sem=recv.at[a, k], device_id=to,
                                                device_id_type=MESH)

        own = [piece(ins, a) for a in range(n)]
        mine = [pltpu.make_async_copy(own[a], piece(outs, a, _idx(me)), loc.at[a]) for a in range(n)]
        first = [cp(a, 0, me, sib, src=own[a]) for a in range(n)]
        first += [cp(a, 1 + j, me, (*chip, c), src=own[a]) for a in range(n) for j, chip in enumerate(chips)]
        landed = [cp(a, 1 + j, (*chip, c), me) for j, chip in enumerate(chips) for a in range(n)]
        passed = [cp(a, 4 + j, (*chip, c), sib) for j, chip in enumerate(chips) for a in range(n)]
        from_sib = [cp(a, 0, sib, me) for a in range(n)]
        from_sib += [cp(a, 4 + j, (*chip, 1 - c), me) for j, chip in enumerate(chips) for a in range(n)]
        return mine, first, landed, passed, from_sib

    def start(self, *refs):
        mine, first, _, _, _ = self._copies(*refs)
        for cp in mine + first:
            cp.start()

    def mid(self, *refs):
        _, _, landed, passed, _ = self._copies(*refs)
        for got, on in zip(landed, passed):
            got.wait_recv()
            on.start()

    def finish(self, *refs):
        mine, first, _, passed, from_sib = self._copies(*refs)
        for cp in from_sib:
            cp.wait_recv()
        for cp in first + passed:
            cp.wait_send()
        for cp in mine:
            cp.wait()


class _Exchange:
    def __init__(self, arrs, rows=None):
        self.srcs = list(arrs)
        self.nsem = len(arrs)
        self.rows = rows if rows is not None else [None] * len(arrs)
        self.out_shape = [jax.ShapeDtypeStruct(a.shape if r is None else (a.shape[0], r[1]) + a.shape[2:], a.dtype)
                          for a, r in zip(arrs, self.rows)]

    def _copies(self, ins, outs, send, recv, loc):
        n = len(self.srcs)
        x, y, c = _place()
        me = _idx((x, y, c))

        def src(a, q):
            r = self.rows[a]
            return ins[a].at[q] if r is None else ins[a].at[q, pl.ds(r[0], r[1])]

        mine = [pltpu.make_async_copy(src(a, me), outs[a].at[me], loc.at[a]) for a in range(n)]
        remote = []
        for k in range(1, NDEV):
            peer = (x ^ (k >> 2), y ^ ((k >> 1) & 1), c ^ (k & 1))
            remote += [pltpu.make_async_remote_copy(
                src_ref=src(a, _idx(peer)), dst_ref=outs[a].at[me], send_sem=send.at[a, k - 1],
                recv_sem=recv.at[a, k - 1], device_id=peer, device_id_type=MESH) for a in range(n)]
        return mine, remote

    def start(self, *refs):
        mine, remote = self._copies(*refs)
        for cp in mine + remote:
            cp.start()

    def mid(self, *refs):
        pass

    def finish(self, *refs):
        mine, remote = self._copies(*refs)
        for cp in remote + mine:
            cp.wait()


def _comm_scratch(n):
    return [pltpu.SemaphoreType.DMA((n, 7)), pltpu.SemaphoreType.DMA((n, 7)), pltpu.SemaphoreType.DMA((n,))]


def _comm_call(name, comm):
    n = len(comm.srcs)

    def body(*refs):
        parts = (refs[:n], refs[n:2 * n]) + tuple(refs[2 * n:])
        comm.start(*parts)
        comm.mid(*parts)
        comm.finish(*parts)

    return pl.pallas_call(body, name=name, in_specs=[ANY] * n, out_specs=[ANY] * n, out_shape=comm.out_shape,
                          scratch_shapes=_comm_scratch(comm.nsem))(*comm.srcs)


def _pcall(body, *, name, grid, in_specs, out_specs, out_shape, scratch_shapes, sem, args, comm=None):
    if comm is None:
        return pl.pallas_call(body, name=name, grid=grid, in_specs=in_specs, out_specs=out_specs, out_shape=out_shape,
                              scratch_shapes=scratch_shapes, compiler_params=_cp(sem))(*args), None
    ni, no, ns, nc = len(in_specs), len(out_shape), len(scratch_shapes), len(comm.srcs)
    total = 1
    for g in grid:
        total *= g
    middle = (4 * total) // 5

    def wrapped(*refs):
        ins, csrc = refs[:ni], refs[ni:ni + nc]
        outs, cdst = refs[ni + nc:ni + nc + no], refs[ni + nc + no:ni + 2 * nc + no]
        scr, sems = refs[ni + 2 * nc + no:ni + 2 * nc + no + ns], refs[ni + 2 * nc + no + ns:]
        step = pl.program_id(0)
        for k in range(1, len(grid)):
            step = step * grid[k] + pl.program_id(k)
        parts = (csrc, cdst) + tuple(sems)

        @pl.when(step == 0)
        def _():
            comm.start(*parts)

        body(*ins, *outs, *scr)

        @pl.when(step == middle)
        def _():
            comm.mid(*parts)

        @pl.when(step == total - 1)
        def _():
            comm.finish(*parts)

    res = pl.pallas_call(
        wrapped, name=name, grid=grid, in_specs=list(in_specs) + [ANY] * nc, out_specs=list(out_specs) + [ANY] * nc,
        out_shape=list(out_shape) + comm.out_shape, scratch_shapes=list(scratch_shapes) + _comm_scratch(comm.nsem),
        compiler_params=_cp(("arbitrary",) * len(grid)))(*args, *comm.srcs)
    return res[:no], res[no:]


def _mm(name, a, b, *, grid, a_spec, b_spec, o_spec, out_shape, dims, kax=None, res=None, res_spec=None,
        jb=0, acc_shape=None, comm=None):
    nk = grid[kax] if kax is not None else 1

    def body(*refs):
        if res is not None:
            a_ref, b_ref, r_ref, o_ref = refs[:4]
        else:
            a_ref, b_ref, o_ref = refs[:3]

        def product():
            if not jb:
                return _dot(a_ref[...], b_ref[...], dims)
            part = _dot(a_ref[0], b_ref[0], dims)
            for j in range(1, jb):
                part = part + _dot(a_ref[j], b_ref[j], dims)
            return part

        def fin(acc):
            if res is not None:
                acc = acc + r_ref[...]
            o_ref[...] = acc.astype(o_ref.dtype)

        if nk == 1:
            fin(product())
        else:
            acc_ref = refs[-1]
            k = pl.program_id(kax)

            @pl.when(k == 0)
            def _():
                acc_ref[...] = jnp.zeros_like(acc_ref)

            acc_ref[...] += product()

            @pl.when(k == nk - 1)
            def _():
                fin(acc_ref[...])

    sem = tuple("arbitrary" if i == kax else "parallel" for i in range(len(grid)))
    in_specs = [a_spec, b_spec] + ([res_spec] if res is not None else [])
    args = (a, b) + ((res,) if res is not None else ())
    scratch = [pltpu.VMEM(acc_shape, F32)] if nk > 1 else []
    (out,), got = _pcall(body, name=name, grid=grid, in_specs=in_specs, out_specs=[o_spec], out_shape=[out_shape],
                         scratch_shapes=scratch, sem=sem, args=args, comm=comm)
    return out if comm is None else (out, got)


def _bs(shape, fn):
    return pl.BlockSpec(shape, fn)


def _rms_bwd(name, x, g, dh, dres=None, tm=512, also_bf16=False, comm=None):
    T, D = x.shape

    def body(*refs):
        if also_bf16:
            refs, dxb_ref = refs[:-1], refs[-1]
        if dres is not None:
            x_ref, g_ref, dh_ref, dres_ref, dx_ref, dg_ref = refs
        else:
            x_ref, g_ref, dh_ref, dx_ref, dg_ref = refs
        i = pl.program_id(0)
        xf = x_ref[...]
        r = lax.rsqrt(jnp.mean(xf * xf, axis=-1, keepdims=True) + RMS_EPS)
        xh = xf * r
        d = dh_ref[...].astype(F32)
        dxh = d * g_ref[...]
        dx = r * (dxh - xh * jnp.mean(dxh * xh, axis=-1, keepdims=True))
        if dres is not None:
            dx = dx + dres_ref[...]
        dx_ref[...] = dx
        if also_bf16:
            dxb_ref[...] = dx.astype(BF16)
        part = jnp.sum(d * xh, axis=0, keepdims=True)

        @pl.when(i == 0)
        def _():
            dg_ref[...] = part

        @pl.when(i > 0)
        def _():
            dg_ref[...] += part

    row = _bs((tm, D), lambda i: (i, 0))
    vec = _bs((1, D), lambda i: (0, 0))
    in_specs = [row, vec, row] + ([row] if dres is not None else [])
    args = (x, g, dh) + ((dres,) if dres is not None else ())
    extra = [jax.ShapeDtypeStruct((T, D), BF16)] if also_bf16 else []
    outs, got = _pcall(
        body, name=name, grid=(T // tm,), in_specs=in_specs, out_specs=[row, vec] + [row] * len(extra),
        out_shape=[jax.ShapeDtypeStruct((T, D), F32), jax.ShapeDtypeStruct((1, D), F32)] + extra, scratch_shapes=[],
        sem=("arbitrary",), args=args, comm=comm)
    return outs if comm is None else (outs, got)


def _out_proj_loss(name, a, w, res, tgt, tm=512):
    J, T, k = a.shape
    N = w.shape[2]

    def body(a_ref, w_ref, r_ref, t_ref, dy_ref, s_ref, dyb_ref):
        i = pl.program_id(0)
        y = _dot(a_ref[0], w_ref[0], NN)
        for j in range(1, J):
            y = y + _dot(a_ref[j], w_ref[j], NN)
        e = (y + r_ref[...]) - t_ref[...]
        dy = e * (1.0 / N)
        dy_ref[...] = dy
        dyb_ref[...] = dy.astype(BF16)
        part = jnp.sum(e * e, axis=0, keepdims=True)

        @pl.when(i == 0)
        def _():
            s_ref[...] = part

        @pl.when(i > 0)
        def _():
            s_ref[...] += part

    row = _bs((tm, N), lambda i: (i, 0))
    return pl.pallas_call(
        body, name=name, grid=(T // tm,),
        in_specs=[_bs((J, tm, k), lambda i: (0, i, 0)), _bs((J, k, N), lambda i: (0, 0, 0)), row, row],
        out_specs=[row, _bs((1, N), lambda i: (0, 0)), row],
        out_shape=[jax.ShapeDtypeStruct((T, N), F32), jax.ShapeDtypeStruct((1, N), F32),
                   jax.ShapeDtypeStruct((T, N), BF16)],
        compiler_params=_cp(("arbitrary",)))(a, w, res, tgt)


def _shift_rows(t, k, row):
    return jnp.where(row >= k, pltpu.roll(t, k, 0), 0.0)


def _shift_rows_up(t, k, row, n):
    return jnp.where(row < n - k, pltpu.roll(t, n - k, 0), 0.0)


def _convffn_fwd(name, u, cw, cb, B, S):
    _, J, T, F = u.shape

    def body(u_ref, cw_ref, cb_ref, o_ref, gc_ref):
        a = u_ref[0].astype(F32)
        g = u_ref[1].astype(F32)
        row = lax.broadcasted_iota(jnp.int32, (S, F), 0)
        w0, w1, w2 = cw_ref[0:1, :], cw_ref[1:2, :], cw_ref[2:3, :]
        gc = _shift_rows(g, 2, row) * w0 + _shift_rows(g, 1, row) * w1 + g * w2 + cb_ref[...]
        gc_ref[...] = gc.astype(gc_ref.dtype)
        o_ref[...] = (gc * jax.nn.sigmoid(gc) * a).astype(o_ref.dtype)

    blk = _bs((None, S, F), lambda j, b: (j, b, 0))
    return pl.pallas_call(
        body, name=name, grid=(J, B),
        in_specs=[_bs((2, None, S, F), lambda j, b: (0, j, b, 0)), _bs((None, 3, F), lambda j, b: (j, 0, 0)),
                  _bs((None, 1, F), lambda j, b: (j, 0, 0))],
        out_specs=[blk, blk], out_shape=[jax.ShapeDtypeStruct((J, T, F), BF16)] * 2,
        compiler_params=_cp(("parallel", "parallel")))(u, cw, cb)


def _convffn_bwd(name, u, gcb, cw, dgt, B, S, comm=None):
    _, J, T, F = u.shape

    def body(u_ref, gc_ref, cw_ref, d_ref, du_ref, dcw_ref, dcb_ref):
        b = pl.program_id(1)
        a = u_ref[0].astype(F32)
        g = u_ref[1].astype(F32)
        gc = gc_ref[...].astype(F32)
        d = d_ref[...].astype(F32)
        row = lax.broadcasted_iota(jnp.int32, (S, F), 0)
        w0, w1, w2 = cw_ref[0:1, :], cw_ref[1:2, :], cw_ref[2:3, :]
        sg = jax.nn.sigmoid(gc)
        du_ref[0] = (d * gc * sg).astype(du_ref.dtype)
        dgc = d * a * (sg * (1.0 + gc * (1.0 - sg)))
        up1, up2 = _shift_rows_up(dgc, 1, row, S), _shift_rows_up(dgc, 2, row, S)
        du_ref[1] = (dgc * w2 + up1 * w1 + up2 * w0).astype(du_ref.dtype)
        parts = [jnp.sum(up2 * g, axis=0, keepdims=True), jnp.sum(up1 * g, axis=0, keepdims=True),
                 jnp.sum(dgc * g, axis=0, keepdims=True)]
        pb = jnp.sum(dgc, axis=0, keepdims=True)

        @pl.when(b == 0)
        def _():
            for k in range(3):
                dcw_ref[k:k + 1, :] = parts[k]
            dcb_ref[...] = pb

        @pl.when(b > 0)
        def _():
            for k in range(3):
                dcw_ref[k:k + 1, :] += parts[k]
            dcb_ref[...] += pb

    uspec = _bs((2, None, S, F), lambda j, b: (0, j, b, 0))
    blk = _bs((None, S, F), lambda j, b: (j, b, 0))
    return _pcall(
        body, name=name, grid=(J, B),
        in_specs=[uspec, blk, _bs((None, 3, F), lambda j, b: (j, 0, 0)), blk],
        out_specs=[uspec, _bs((None, 3, F), lambda j, b: (j, 0, 0)), _bs((None, 1, F), lambda j, b: (j, 0, 0))],
        out_shape=[jax.ShapeDtypeStruct(u.shape, BF16), jax.ShapeDtypeStruct((J, 3, F), F32),
                   jax.ShapeDtypeStruct((J, 1, F), F32)],
        scratch_shapes=[], sem=("parallel", "arbitrary"), args=(u, gcb, cw, dgt), comm=comm)


def _ret_tables(S):
    half = RET_DK // 2
    inv = ROPE_THETA ** (-jnp.arange(half, dtype=F32) / half)
    ang = jnp.arange(S).astype(F32)[:, None] * inv[None, :]
    lg = jnp.log1p(-jnp.exp2(-5.0 - jnp.arange(RET_H, dtype=F32)))
    i = jnp.arange(RET_SC, dtype=F32)
    same_or_earlier = (jnp.floor(i[None, :] / CHUNK) <= jnp.floor(i[:, None] / CHUNK)).astype(F32)
    dm = jnp.exp(lg[:, None, None] * jnp.abs(i[:, None] - i[None, :])) * same_or_earlier[None]
    qd = jnp.exp(lg[:, None] * (i + 1.0))[:, :, None]
    kd = jnp.exp(lg[:, None] * (RET_SC - 1.0 - i))[:, :, None]
    cd = jnp.exp(lg * RET_SC)[:, None, None]
    return jnp.cos(ang), jnp.sin(ang), dm, qd, kd, cd


def _rope_halves(t, cs, sn):
    h = t.shape[-1] // 2
    t1, t2 = t[:, :h], t[:, h:]
    return jnp.concatenate([t1 * cs - t2 * sn, t2 * cs + t1 * sn], axis=-1)


def _unrope_halves(d, cs, sn):
    h = d.shape[-1] // 2
    d1, d2 = d[:, :h], d[:, h:]
    return jnp.concatenate([d1 * cs + d2 * sn, d2 * cs - d1 * sn], axis=-1)


def _ret_specs(nC, order):
    SC = RET_SC

    def sp(shape, fn):
        return _bs(shape, lambda *g: fn(*order(*g)))

    q = sp((SC, RET_DK), lambda b, h, c: (b * nC + c, h))
    k = sp((SC, RET_DK), lambda b, h, c: (b * nC + c, RET_H + h))
    v = sp((SC, RET_DV), lambda b, h, c: (b * nC + c, RET_H + h))
    g = sp((SC, RET_DV), lambda b, h, c: (b * nC + c, 2 * RET_H + h))
    cs = sp((SC, RET_DK // 2), lambda b, h, c: (c, 0))
    dm = sp((None, SC, SC), lambda b, h, c: (h, 0, 0))
    dv = sp((None, SC, 1), lambda b, h, c: (h, 0, 0))
    cd = sp((None, 1, 1), lambda b, h, c: (h, 0, 0))
    gn = sp((None, 1, RET_DV), lambda b, h, c: (h, 0, 0))
    wide = sp((SC, RET_DV), lambda b, h, c: (b * nC + c, h))
    narrow = sp((SC, RET_DK), lambda b, h, c: (b * nC + c, h))
    st = sp((None, None, None, RET_DK, RET_DV), lambda b, h, c: (b, h, c, 0, 0))
    return dict(q=q, k=k, v=v, g=g, cs=cs, dm=dm, dv=dv, cd=cd, gn=gn, wide=wide, narrow=narrow, st=st)


def _ret_fwd(proj, tabs, gn, B, S, comm=None):
    T = B * S
    nC = S // RET_SC
    cos, sin, dm, qd, kd, cd = tabs
    s = _ret_specs(nC, lambda b, h, c: (b, h, c))

    def body(q_ref, k_ref, v_ref, g_ref, cos_ref, sin_ref, dm_ref, qd_ref, kd_ref, cd_ref, gn_ref,
             o_ref, gt_ref, st_ref, state):
        c = pl.program_id(2)

        @pl.when(c == 0)
        def _():
            state[...] = jnp.zeros_like(state)

        cs, sn = cos_ref[...], sin_ref[...]
        qf = _rope_halves(q_ref[...].astype(F32), cs, sn)
        kf = _rope_halves(k_ref[...].astype(F32), cs, sn) * (RET_DK ** -0.5)
        v = v_ref[...]
        p = _dot(qf.astype(BF16), kf.astype(BF16), NT) * dm_ref[...]
        st = state[...]
        stb = st.astype(BF16)
        st_ref[...] = stb
        o = _dot(p.astype(BF16), v, NN) + _dot((qf * qd_ref[...]).astype(BF16), stb, NN)
        state[...] = st * cd_ref[...] + _dot((kf * kd_ref[...]).astype(BF16), v, TN)
        o_ref[...] = o
        r = lax.rsqrt(jnp.mean(o * o, axis=-1, keepdims=True) + RMS_EPS)
        gf = g_ref[...].astype(F32)
        gt_ref[...] = ((o * r * gn_ref[...]) * (gf * jax.nn.sigmoid(gf))).astype(BF16)

    return _pcall(
        body, name="ret_fwd", grid=(B, RET_H, nC),
        in_specs=[s["q"], s["k"], s["v"], s["g"], s["cs"], s["cs"], s["dm"], s["dv"], s["dv"], s["cd"], s["gn"]],
        out_specs=[s["wide"], s["wide"], s["st"]],
        out_shape=[jax.ShapeDtypeStruct((T, RET_H * RET_DV), F32), jax.ShapeDtypeStruct((T, RET_H * RET_DV), BF16),
                   jax.ShapeDtypeStruct((B, RET_H, nC, RET_DK, RET_DV), BF16)],
        scratch_shapes=[pltpu.VMEM((RET_DK, RET_DV), F32)], sem=("parallel", "parallel", "arbitrary"),
        args=(proj, proj, proj, proj, cos, sin, dm, qd, kd, cd, gn), comm=comm)


def _ret_bwd(proj, o_raw, states, dgt, tabs, gn, B, S, comm=None):
    T = B * S
    nC = S // RET_SC
    cos, sin, dm, qd, kd, cd = tabs
    s = _ret_specs(nC, lambda b, c, h: (b, h, nC - 1 - c))

    def body(q_ref, k_ref, v_ref, g_ref, o_ref, st_ref, d_ref, cos_ref, sin_ref, dm_ref, qd_ref, kd_ref, cd_ref,
             gn_ref, dproj_ref, dgn_ref, dstates):
        b, c, h = pl.program_id(0), pl.program_id(1), pl.program_id(2)
        dstate = dstates.at[h]

        @pl.when(c == 0)
        def _():
            dstate[...] = jnp.zeros_like(dstate)

        @pl.when((b == 0) & (c == 0))
        def _():
            dgn_ref[h] = jnp.zeros((1, RET_DV), F32)

        cs, sn = cos_ref[...], sin_ref[...]
        qf = _rope_halves(q_ref[...].astype(F32), cs, sn)
        kf = _rope_halves(k_ref[...].astype(F32), cs, sn) * (RET_DK ** -0.5)
        v = v_ref[...]
        gnv = gn_ref[h]
        o = o_ref[...]
        r = lax.rsqrt(jnp.mean(o * o, axis=-1, keepdims=True) + RMS_EPS)
        oh = o * r
        gf = g_ref[...].astype(F32)
        sg = jax.nn.sigmoid(gf)
        d = d_ref[...].astype(F32)
        dg = (d * (oh * gnv) * (sg * (1.0 + gf * (1.0 - sg)))).astype(BF16)
        don = d * (gf * sg)
        dgn_ref[h] += jnp.sum(don * oh, axis=0, keepdims=True)
        doh = don * gnv
        dO = (r * (doh - oh * jnp.mean(doh * oh, axis=-1, keepdims=True))).astype(BF16)
        dmv = dm_ref[h]
        qb, kb = qf.astype(BF16), kf.astype(BF16)
        p = (_dot(qb, kb, NT) * dmv).astype(BF16)
        dp = (_dot(dO, v, NT) * dmv).astype(BF16)
        st = st_ref[...]
        dsn = dstate[...]
        dsb = dsn.astype(BF16)
        qdv, kdv = qd_ref[h], kd_ref[h]
        dq = _dot(dp, kb, NN) + _dot(dO, st, NT) * qdv
        dk = _dot(dp, qb, TN) + _dot(v, dsb, NT) * kdv
        dv = _dot(p, dO, TN) + _dot((kf * kdv).astype(BF16), dsb, NN)
        dstate[...] = dsn * cd_ref[h] + _dot((qf * qdv).astype(BF16), dO, TN)
        dq = _unrope_halves(dq, cs, sn).astype(BF16)
        dk = (_unrope_halves(dk, cs, sn) * (RET_DK ** -0.5)).astype(BF16)
        dv = dv.astype(BF16)
        nq, nv = RET_H * RET_DK, RET_H * RET_DV
        for hh in range(RET_H):
            @pl.when(h == hh)
            def _():
                dproj_ref[:, hh * RET_DK:(hh + 1) * RET_DK] = dq
                dproj_ref[:, nq + hh * RET_DK:nq + (hh + 1) * RET_DK] = dk
                dproj_ref[:, 2 * nq + hh * RET_DV:2 * nq + (hh + 1) * RET_DV] = dv
                dproj_ref[:, 2 * nq + nv + hh * RET_DV:2 * nq + nv + (hh + 1) * RET_DV] = dg

    width = 2 * RET_H * (RET_DK + RET_DV)

    def all_heads(*shape):
        return _bs((RET_H,) + shape, lambda b, c, h: (0,) * (1 + len(shape)))

    return _pcall(
        body, name="ret_bwd", grid=(B, nC, RET_H),
        in_specs=[s["q"], s["k"], s["v"], s["g"], s["wide"], s["st"], s["wide"], s["cs"], s["cs"],
                  all_heads(RET_SC, RET_SC), all_heads(RET_SC, 1), all_heads(RET_SC, 1), all_heads(1, 1),
                  all_heads(1, RET_DV)],
        out_specs=[_bs((RET_SC, width), lambda b, c, h: (b * nC + nC - 1 - c, 0)),
                   _bs((RET_H, 1, RET_DV), lambda b, c, h: (0, 0, 0))],
        out_shape=[jax.ShapeDtypeStruct((T, width), BF16), jax.ShapeDtypeStruct((RET_H, 1, RET_DV), F32)],
        scratch_shapes=[pltpu.VMEM((RET_H, RET_DK, RET_DV), F32)], sem=("arbitrary", "arbitrary", "arbitrary"),
        args=(proj, proj, proj, proj, o_raw, states, dgt, cos, sin, dm, qd, kd, cd, gn), comm=comm)


MLA_PAD = 256
MLA_R2 = 2 * MLA_ROPE


def _dup(t):
    return jnp.concatenate([t, t], axis=-1)


def _fold(t):
    return t[..., :MLA_ROPE] + t[..., MLA_ROPE:]


def _mla_tables(S):
    half = MLA_ROPE // 2
    inv = ROPE_THETA ** (-jnp.arange(half, dtype=F32) / half)
    ang = jnp.arange(S).astype(F32)[:, None] * inv[None, :]
    cos, sin, zero = jnp.cos(ang), jnp.sin(ang), jnp.zeros((S, MLA_ROPE), F32)
    return jnp.concatenate([cos, cos, zero], axis=-1), jnp.concatenate([-sin, sin, zero], axis=-1)


def _head_norm_rope(n, r2, gn, gr2, cos, sin, scale):
    ssq = jnp.sum(n * n, axis=-1, keepdims=True) + 0.5 * jnp.sum(r2 * r2, axis=-1, keepdims=True)
    rstd = lax.rsqrt(ssq * (1.0 / MLA_QK) + RMS_EPS)
    yn = n * rstd * gn
    yr = r2 * rstd * gr2
    z = yr * cos + pltpu.roll(yr, MLA_ROPE // 2, 1) * sin
    if scale != 1.0:
        yn, z = yn * scale, z * scale
    return yn, z


def _head_norm_rope_bwd(dn, dz, n, r2, gn, gr2, cos, sin, scale):
    ssq = jnp.sum(n * n, axis=-1, keepdims=True) + 0.5 * jnp.sum(r2 * r2, axis=-1, keepdims=True)
    rstd = lax.rsqrt(ssq * (1.0 / MLA_QK) + RMS_EPS)
    hn, hr = n * rstd, r2 * rstd
    if scale != 1.0:
        dn, dz = dn * scale, dz * scale
    dyr = dz * cos + pltpu.roll(dz * sin, MLA_R2 - MLA_ROPE // 2, 1)
    dgn = jnp.sum(dn * hn, axis=0, keepdims=True)
    dgr = jnp.sum(dyr * hr, axis=0, keepdims=True)
    dhn, dhr = dn * gn, dyr * gr2
    mt = (jnp.sum(dhn * hn, axis=-1, keepdims=True) + jnp.sum(dhr * hr, axis=-1, keepdims=True)) * (1.0 / MLA_QK)
    return rstd * (dhn - hn * mt), rstd * (dhr - 0.5 * hr * mt), dgn, dgr


def _diag_bias():
    i = jnp.arange(ATT_TQ)
    return jnp.where((i[None, :] // CHUNK) <= (i[:, None] // CHUNK), 0.0, MASK_VALUE).astype(F32)


def _store_pair(dst, rows, n, r2):
    dst[rows, :MLA_NOPE] = n.astype(BF16)
    dst[rows, MLA_NOPE:] = r2.astype(BF16)


def _mla_fwd(q_raw, kv, kr, gains, tabs, B, S, comm=None):
    T = B * S
    TQ = ATT_TQ
    nQ = S // TQ
    qgn, qgr, kgn, kgr = gains
    cos, sin = tabs
    scale = MLA_QK ** -0.5

    def body(q_ref, kv_ref, kr_ref, qgn_ref, qgr_ref, kgn_ref, kgr_ref, c_ref, s_ref, bias_ref,
             o_ref, lse_ref, qf_s, kf_s, v_s):
        def prep(t, _):
            rows = pl.ds(pl.multiple_of(t * TQ, TQ), TQ)
            cs, sn = c_ref[rows, :], s_ref[rows, :]
            qn, qr = _head_norm_rope(q_ref[rows, :MLA_NOPE], q_ref[rows, MLA_NOPE:], qgn_ref[...], qgr_ref[...],
                                     cs, sn, scale)
            _store_pair(qf_s, rows, qn, qr)
            kn, krr = _head_norm_rope(kv_ref[rows, :MLA_NOPE], kr_ref[rows, :], kgn_ref[...], kgr_ref[...], cs, sn, 1.0)
            _store_pair(kf_s, rows, kn, krr)
            v_s[rows, :] = kv_ref[rows, MLA_NOPE:].astype(BF16)
            return 0

        lax.fori_loop(0, nQ, prep, 0, unroll=2)
        for i in range(nQ):
            rows = slice(i * TQ, (i + 1) * TQ)
            q = qf_s[rows, :]
            sd = _dot(q, kf_s[rows, :], NT) + bias_ref[...]
            m = jnp.max(sd, axis=-1, keepdims=True)
            if i:
                sl = _dot(q, kf_s[:i * TQ, :], NT)
                m = jnp.maximum(m, jnp.max(sl, axis=-1, keepdims=True))
            pd = jnp.exp(sd - m)
            l = jnp.sum(pd, axis=-1, keepdims=True)
            acc = _dot(pd.astype(BF16), v_s[rows, :], NN)
            if i:
                pl_ = jnp.exp(sl - m)
                l = l + jnp.sum(pl_, axis=-1, keepdims=True)
                acc = acc + _dot(pl_.astype(BF16), v_s[:i * TQ, :], NN)
            o_ref[rows, :] = (acc / l).astype(BF16)
            lse_ref[rows, :] = m + jnp.log(l)

    def vec(n):
        return _bs((1, n), lambda b, h: (0, 0))

    def cols(n):
        return _bs((S, n), lambda b, h: (b, h))

    tab = _bs((S, MLA_R2), lambda b, h: (0, 0))
    return _pcall(
        body, name="mla_fwd", grid=(B, MLA_H),
        in_specs=[cols(MLA_PAD), cols(MLA_NOPE + MLA_V), _bs((S, MLA_R2), lambda b, h: (b, 0)),
                  vec(MLA_NOPE), vec(MLA_R2), vec(MLA_NOPE), vec(MLA_R2), tab, tab,
                  _bs((TQ, TQ), lambda b, h: (0, 0))],
        out_specs=[cols(MLA_V), _bs((None, S, 1), lambda b, h: (h, b, 0)), cols(MLA_PAD), cols(MLA_PAD)],
        out_shape=[jax.ShapeDtypeStruct((T, MLA_H * MLA_V), BF16), jax.ShapeDtypeStruct((MLA_H, T, 1), F32),
                   jax.ShapeDtypeStruct((T, MLA_H * MLA_PAD), BF16), jax.ShapeDtypeStruct((T, MLA_H * MLA_PAD), BF16)],
        scratch_shapes=[pltpu.VMEM((S, MLA_V), BF16)],
        sem=("parallel", "parallel"), args=(q_raw, kv, kr, qgn, qgr, kgn, kgr, cos, sin, _diag_bias()), comm=comm)


def _mla_bwd(q_raw, kv, kr, o, lse, do, qf, kf, gains, tabs, B, S, comm=None):
    T = B * S
    TQ = ATT_TQ
    nQ = S // TQ
    qgn, qgr, kgn, kgr = gains
    cos, sin = tabs
    scale = MLA_QK ** -0.5

    def body(q_ref, kv_ref, kr_ref, o_ref, lse_ref, do_ref, qf_s, kf_s, qgn_ref, qgr_ref, kgn_ref, kgr_ref, c_ref, s_ref,
             bias_ref, dq_ref, dkv_ref, dkr_ref, dqgn_ref, dqgr_ref, dkgn_ref, dkgr_ref,
             v_s, dl_s, dq_s, dk_s, dv_s):
        b, h = pl.program_id(0), pl.program_id(1)

        def blk(t):
            return pl.ds(pl.multiple_of(t * TQ, TQ), TQ)

        def prep(t, _):
            rows = blk(t)
            v_s[rows, :] = kv_ref[rows, MLA_NOPE:].astype(BF16)
            dl_s[rows, :] = jnp.sum(do_ref[rows, :].astype(F32) * o_ref[rows, :].astype(F32), axis=-1, keepdims=True)
            dk_s[rows, :] = jnp.zeros((TQ, MLA_PAD), F32)
            dv_s[rows, :] = jnp.zeros((TQ, MLA_V), F32)
            return 0

        lax.fori_loop(0, nQ, prep, 0, unroll=2)

        gqn, gqr = jnp.zeros((1, MLA_NOPE), F32), jnp.zeros((1, MLA_R2), F32)
        for i in range(nQ):
            rows = slice(i * TQ, (i + 1) * TQ)
            q, doi, lse_i, dl_i = qf_s[rows, :], do_ref[rows, :], lse_ref[rows, :], dl_s[rows, :]

            def part(cols, bias):
                k, v = kf_s[cols, :], v_s[cols, :]
                s = _dot(q, k, NT)
                if bias is not None:
                    s = s + bias
                p = jnp.exp(s - lse_i)
                ds = (p * (_dot(doi, v, NT) - dl_i)).astype(BF16)
                dk_s[cols, :] += _dot(ds, q, TN)
                dv_s[cols, :] += _dot(p.astype(BF16), doi, TN)
                return _dot(ds, k, NN)

            dq = part(rows, bias_ref[...])
            for c0 in range(0, i * TQ, 2 * TQ):
                dq = dq + part(slice(c0, min(c0 + 2 * TQ, i * TQ)), None)
            dq_s[...] = dq
            dqn, dqr, a0, a1 = _head_norm_rope_bwd(dq_s[:, :MLA_NOPE], dq_s[:, MLA_NOPE:], q_ref[rows, :MLA_NOPE],
                                                   q_ref[rows, MLA_NOPE:], qgn_ref[...], qgr_ref[...],
                                                   c_ref[rows, :], s_ref[rows, :], scale)
            _store_pair(dq_ref, rows, dqn, dqr)
            gqn, gqr = gqn + a0, gqr + a1

        def post(t, carry):
            rows = blk(t)
            dkn, dkr, a2, a3 = _head_norm_rope_bwd(dk_s[rows, :MLA_NOPE], dk_s[rows, MLA_NOPE:],
                                                   kv_ref[rows, :MLA_NOPE], kr_ref[rows, :], kgn_ref[...], kgr_ref[...],
                                                   c_ref[rows, :], s_ref[rows, :], 1.0)
            dkv_ref[rows, :MLA_NOPE] = dkn.astype(BF16)
            dkv_ref[rows, MLA_NOPE:] = dv_s[rows, :].astype(BF16)

            @pl.when(h == 0)
            def _():
                dkr_ref[rows, :] = dkr

            @pl.when(h > 0)
            def _():
                dkr_ref[rows, :] += dkr

            return carry[0] + a2, carry[1] + a3

        gkn, gkr = lax.fori_loop(0, nQ, post, (jnp.zeros((1, MLA_NOPE), F32), jnp.zeros((1, MLA_R2), F32)), unroll=2)
        first = (b == 0) & (h == 0)

        @pl.when(first)
        def _():
            dqgn_ref[...] = gqn
            dqgr_ref[...] = gqr
            dkgn_ref[...] = gkn
            dkgr_ref[...] = gkr

        @pl.when(jnp.logical_not(first))
        def _():
            dqgn_ref[...] += gqn
            dqgr_ref[...] += gqr
            dkgn_ref[...] += gkn
            dkgr_ref[...] += gkr

    def vec(n):
        return _bs((1, n), lambda b, h: (0, 0))

    def cols(n):
        return _bs((S, n), lambda b, h: (b, h))

    tab = _bs((S, MLA_R2), lambda b, h: (0, 0))
    return _pcall(
        body, name="mla_bwd", grid=(B, MLA_H),
        in_specs=[cols(MLA_PAD), cols(MLA_NOPE + MLA_V), _bs((S, MLA_R2), lambda b, h: (b, 0)), cols(MLA_V),
                  _bs((None, S, 1), lambda b, h: (h, b, 0)), cols(MLA_V), cols(MLA_PAD), cols(MLA_PAD),
                  vec(MLA_NOPE), vec(MLA_R2), vec(MLA_NOPE), vec(MLA_R2), tab, tab,
                  _bs((TQ, TQ), lambda b, h: (0, 0))],
        out_specs=[cols(MLA_PAD), cols(MLA_NOPE + MLA_V), _bs((S, MLA_R2), lambda b, h: (b, 0)),
                   vec(MLA_NOPE), vec(MLA_R2), vec(MLA_NOPE), vec(MLA_R2)],
        out_shape=[jax.ShapeDtypeStruct((T, MLA_H * MLA_PAD), BF16),
                   jax.ShapeDtypeStruct((T, MLA_H * (MLA_NOPE + MLA_V)), BF16),
                   jax.ShapeDtypeStruct((T, MLA_R2), F32), jax.ShapeDtypeStruct((1, MLA_NOPE), F32),
                   jax.ShapeDtypeStruct((1, MLA_R2), F32), jax.ShapeDtypeStruct((1, MLA_NOPE), F32),
                   jax.ShapeDtypeStruct((1, MLA_R2), F32)],
        scratch_shapes=[pltpu.VMEM((S, MLA_V), BF16), pltpu.VMEM((S, 1), F32), pltpu.VMEM((TQ, MLA_PAD), F32),
                        pltpu.VMEM((S, MLA_PAD), F32), pltpu.VMEM((S, MLA_V), F32)],
        sem=("arbitrary", "arbitrary"),
        args=(q_raw, kv, kr, o, lse, do, qf, kf, qgn, qgr, kgn, kgr, cos, sin, _diag_bias()), comm=comm)


def _adamw(name, recvs, w, m, v, tr=None, comm=None):
    n, R, C = recvs[0].shape
    L = len(recvs)
    Lw, Rw, _ = w.shape
    assert Lw * Rw == L * R and w.shape[2] == C
    tr = R if tr is None else tr
    assert R % tr == 0 and Rw % tr == 0
    per = R // tr
    per_w = Rw // tr
    c1 = 1.0 - ADAM_B1 ** ADAM_STEP
    c2 = 1.0 - ADAM_B2 ** ADAM_STEP

    def body(*refs):
        r_refs = refs[:L]
        w_ref, m_ref, v_ref, g_ref, d_ref, nm_ref, nv_ref = refs[L:]
        layer = pl.program_id(0) // per

        def total(r_ref):
            t = r_ref[0].astype(F32)
            for k in range(1, n):
                t = t + r_ref[k].astype(F32)
            return t

        g = total(r_refs[0]) if L == 1 else lax.switch(layer, [functools.partial(total, r) for r in r_refs])
        mm = ADAM_B1 * m_ref[...] + (1.0 - ADAM_B1) * g
        vv = ADAM_B2 * v_ref[...] + (1.0 - ADAM_B2) * (g * g)
        g_ref[...] = g
        nm_ref[...] = mm
        nv_ref[...] = vv
        d_ref[...] = -ADAM_LR * ((mm / c1) / (jnp.sqrt(vv / c2) + ADAM_EPS) + ADAM_WD * w_ref[...])

    blk = _bs((None, tr, C), lambda i: (i // per_w, i % per_w, 0))
    r_specs = [_bs((n, tr, C), functools.partial(lambda l, i: (0, jnp.clip(i - l * per, 0, per - 1), 0), l))
               for l in range(L)]
    outs, got = _pcall(body, name=name, grid=(L * per,), in_specs=r_specs + [blk, blk, blk], out_specs=[blk] * 4,
                       out_shape=[jax.ShapeDtypeStruct(w.shape, F32)] * 4, scratch_shapes=[], sem=("arbitrary",),
                       args=(*recvs, w, m, v), comm=comm)
    return outs if comm is None else (outs, got)


def _sum8(name, a):
    n, R, C = a.shape

    def body(a_ref, o_ref):
        s = a_ref[0]
        for k in range(1, n):
            s = s + a_ref[k]
        o_ref[...] = s

    return pl.pallas_call(body, name=name, out_shape=jax.ShapeDtypeStruct((R, C), a.dtype))(a)


def _sds(shape, dt):
    return jax.ShapeDtypeStruct(shape, dt)


def _norm_proj(name, x, g, w, o_spec, out_shape, tm=1024, comm=None):
    T, K = x.shape
    J, _, n = w.shape

    def body(x_ref, g_ref, w_ref, o_ref, h_ref, hs):
        @pl.when(pl.program_id(1) == 0)
        def _():
            xf = x_ref[...]
            r = lax.rsqrt(jnp.mean(xf * xf, axis=-1, keepdims=True) + RMS_EPS)
            h = (xf * r * g_ref[...]).astype(BF16)
            hs[...] = h
            h_ref[...] = h

        o_ref[...] = _dot(hs[...], w_ref[...], NN).astype(o_ref.dtype)

    row = _bs((tm, K), lambda m, j: (m, 0))
    (out, h), got = _pcall(
        body, name=name, grid=(T // tm, J),
        in_specs=[row, _bs((1, K), lambda m, j: (0, 0)), _bs((None, K, n), lambda m, j: (j, 0, 0))],
        out_specs=[o_spec, row], out_shape=[out_shape, _sds((T, K), BF16)], scratch_shapes=[pltpu.VMEM((tm, K), BF16)],
        sem=("parallel", "arbitrary"), args=(x, g, w), comm=comm)
    return out, h, got


def _proj_shared_dx(name, d, w, tm=1024, comm=None):
    J, T, n = d.shape
    K = w.shape[1]
    return _mm(name, d, w, grid=(T // tm, J), a_spec=_bs((None, tm, n), lambda m, k: (k, m, 0)),
               b_spec=_bs((None, K, n), lambda m, k: (k, 0, 0)), o_spec=_bs((tm, K), lambda m, k: (m, 0)),
               out_shape=_sds((T, K), BF16), dims=NT, kax=1, acc_shape=(tm, K), comm=comm)


def _out_proj(name, a, w, res, tm=512):
    J, T, k = a.shape
    N = w.shape[2]
    return _mm(name, a, w, grid=(T // tm,), a_spec=_bs((J, tm, k), lambda m: (0, m, 0)),
               b_spec=_bs((J, k, N), lambda m: (0, 0, 0)), o_spec=_bs((tm, N), lambda m: (m, 0)),
               out_shape=_sds((T, N), F32), dims=NN, res=res, res_spec=_bs((tm, N), lambda m: (m, 0)), jb=J)


def _out_proj_dx(name, dx, w, tm=1024, comm=None):
    T, N = dx.shape
    J, k, _ = w.shape
    return _mm(name, dx, w, grid=(T // tm, J), a_spec=_bs((tm, N), lambda m, j: (m, 0)),
               b_spec=_bs((None, k, N), lambda m, j: (j, 0, 0)), o_spec=_bs((None, tm, k), lambda m, j: (j, m, 0)),
               out_shape=_sds((J, T, k), BF16), dims=NT, comm=comm)


def _out_proj_dw(name, a, dx, tt=1024, comm=None):
    J, T, k = a.shape
    N = dx.shape[1]
    tt = min(tt, T)
    return _mm(name, a, dx, grid=(J, T // tt), a_spec=_bs((None, tt, k), lambda j, t: (j, t, 0)),
               b_spec=_bs((tt, N), lambda j, t: (t, 0)), o_spec=_bs((None, k, N), lambda j, t: (j, 0, 0)),
               out_shape=_sds((J, k, N), BF16), dims=TN, kax=1, acc_shape=(k, N), comm=comm)


def _dense(name, a, b, dims, out_dtype, tm=512, res=None, comm=None):
    if dims == TN:
        T, K = a.shape
        N = b.shape[1]
        return _mm(name, a, b, grid=(T // tm,), a_spec=_bs((tm, K), lambda t: (t, 0)),
                   b_spec=_bs((tm, N), lambda t: (t, 0)), o_spec=_bs((K, N), lambda t: (0, 0)),
                   out_shape=_sds((K, N), out_dtype), dims=TN, kax=0, acc_shape=(K, N), comm=comm)
    M, K = a.shape
    N = b.shape[1] if dims == NN else b.shape[0]
    row = _bs((tm, N), lambda m: (m, 0))
    return _mm(name, a, b, grid=(M // tm,), a_spec=_bs((tm, K), lambda m: (m, 0)), b_spec=_bs(b.shape, lambda m: (0, 0)),
               o_spec=row, out_shape=_sds((M, N), out_dtype), dims=dims, res=res,
               res_spec=row if res is not None else None, comm=comm)


def _bf16(x):
    return x.astype(BF16)


def _ffn_fwd(i, x, norm_g, w_in, cw, cb, w_out, B, S, comm_in=None, tgt=None):
    T = x.shape[0]
    u, h, got = _norm_proj(f"ffn{i}_in", x, norm_g, w_in, _bs((None, 1024, FSH), lambda m, j: (j, m, 0)),
                           _sds((NDEV, T, FSH), BF16), comm=comm_in)
    u4 = u.reshape(2, 4, T, FSH)
    gt, gcb = _convffn_fwd(f"ffn{i}_gate", u4, cw, cb, B, S)
    if tgt is None:
        y = _out_proj(f"ffn{i}_out", gt, w_out, x)
    else:
        y = _out_proj_loss(f"ffn{i}_out_loss", gt, w_out, x, tgt)
    return y, (x, h, u4, gt, gcb), got


def _ffn_bwd(i, dy, dyb, saved, norm_g, w_in, cw, w_out, B, S, first_half_early, riders=(None, None)):
    x, h, u4, gt, gcb = saved
    dgt = _out_proj_dx(f"ffn{i}_out_dx", dyb, w_out, comm=riders[0])
    dw_out = _out_proj_dw(f"ffn{i}_out_dw", gt, dyb, comm=riders[1])
    dgt, got0 = dgt if riders[0] is not None else (dgt, None)
    dw_out, got1 = dw_out if riders[1] is not None else (dw_out, None)
    dw_out = dw_out.reshape(NDEV, FSH // 2, D_MODEL)
    (du4, dcw, dcb), (r_out,) = _convffn_bwd(f"ffn{i}_gate_bwd", u4, gcb, cw, dgt, B, S, comm=_Exchange([dw_out]))
    du = du4.reshape(NDEV, du4.shape[2], FSH)
    dw_in = _out_proj_dw(f"ffn{i}_in_dw", du, h, tt=2048)
    r_in = None
    if first_half_early:
        dh, (r_in,) = _proj_shared_dx(f"ffn{i}_in_dx", du, w_in, comm=_Exchange([dw_in], rows=[(0, FSH // 2)]))
    else:
        dh = _proj_shared_dx(f"ffn{i}_in_dx", du, w_in)
    dx, dgn, dxb = _rms_bwd(f"ffn{i}_norm_bwd", x, norm_g, dh, dres=dy, also_bf16=True)
    return dx, dxb, dict(w_in=dw_in, norm=dgn, cw=dcw, cb=dcb), r_out, r_in, (got0, got1)


def kernel(x, ret_norm, ret_w_in, ret_gn, ret_w_out, mla_norm, mla_w_in, mla_q_norm, mla_w_qb, mla_kv_norm, mla_w_kvb, mla_q_head_norm, mla_k_head_norm, mla_w_out, ffn_norm, ffn_w_in, ffn_conv_w, ffn_conv_b, ffn_w_out, loss_target, m_ret_norm, m_ret_w_in, m_ret_gn, m_ret_w_out, m_mla_norm, m_mla_w_in, m_mla_q_norm, m_mla_w_qb, m_mla_kv_norm, m_mla_w_kvb, m_mla_q_head_norm, m_mla_k_head_norm, m_mla_w_out, m_ffn_norm, m_ffn_w_in, m_ffn_conv_w, m_ffn_conv_b, m_ffn_w_out, v_ret_norm, v_ret_w_in, v_ret_gn, v_ret_w_out, v_mla_norm, v_mla_w_in, v_mla_q_norm, v_mla_w_qb, v_mla_kv_norm, v_mla_w_kvb, v_mla_q_head_norm, v_mla_k_head_norm, v_mla_w_out, v_ffn_norm, v_ffn_w_in, v_ffn_conv_w, v_ffn_conv_b, v_ffn_w_out):
    B, S, D = x.shape
    T = B * S
    w = dict(ret_norm=ret_norm, ret_w_in=ret_w_in, ret_gn=ret_gn, ret_w_out=ret_w_out, mla_norm=mla_norm,
             mla_w_in=mla_w_in, mla_q_norm=mla_q_norm, mla_w_qb=mla_w_qb, mla_kv_norm=mla_kv_norm, mla_w_kvb=mla_w_kvb,
             mla_q_head_norm=mla_q_head_norm, mla_k_head_norm=mla_k_head_norm, mla_w_out=mla_w_out, ffn_norm=ffn_norm,
             ffn_w_in=ffn_w_in, ffn_conv_w=ffn_conv_w, ffn_conv_b=ffn_conv_b, ffn_w_out=ffn_w_out)
    mom = dict(ret_norm=m_ret_norm, ret_w_in=m_ret_w_in, ret_gn=m_ret_gn, ret_w_out=m_ret_w_out, mla_norm=m_mla_norm,
               mla_w_in=m_mla_w_in, mla_q_norm=m_mla_q_norm, mla_w_qb=m_mla_w_qb, mla_kv_norm=m_mla_kv_norm,
               mla_w_kvb=m_mla_w_kvb, mla_q_head_norm=m_mla_q_head_norm, mla_k_head_norm=m_mla_k_head_norm,
               mla_w_out=m_mla_w_out, ffn_norm=m_ffn_norm, ffn_w_in=m_ffn_w_in, ffn_conv_w=m_ffn_conv_w,
               ffn_conv_b=m_ffn_conv_b, ffn_w_out=m_ffn_w_out)
    var = dict(ret_norm=v_ret_norm, ret_w_in=v_ret_w_in, ret_gn=v_ret_gn, ret_w_out=v_ret_w_out, mla_norm=v_mla_norm,
               mla_w_in=v_mla_w_in, mla_q_norm=v_mla_q_norm, mla_w_qb=v_mla_w_qb, mla_kv_norm=v_mla_kv_norm,
               mla_w_kvb=v_mla_w_kvb, mla_q_head_norm=v_mla_q_head_norm, mla_k_head_norm=v_mla_k_head_norm,
               mla_w_out=v_mla_w_out, ffn_norm=v_ffn_norm, ffn_w_in=v_ffn_w_in, ffn_conv_w=v_ffn_conv_w,
               ffn_conv_b=v_ffn_conv_b, ffn_w_out=v_ffn_w_out)
    BIG = ["ret_w_in", "ret_w_out", "mla_w_in", "mla_w_qb", "mla_w_kvb", "mla_w_out", "ffn_w_in", "ffn_w_out"]
    REPL = ["ret_norm", "ffn_norm", "mla_q_head_norm", "mla_k_head_norm", "ffn_conv_b"]
    SHARDED_SMALL = ["ffn_conv_w", "ret_gn", "mla_norm", "mla_q_norm", "mla_kv_norm"]
    dev = _idx(_place())

    def blk16(k, i=0):
        return _bf16(w[k][i])

    small_vec = jnp.concatenate([w[k].reshape(-1) for k in SHARDED_SMALL])
    n_small = small_vec.shape[0]
    small_vec = jnp.pad(small_vec, (0, 4096 - n_small)).reshape(32, 128)
    Wret_in, sg = _comm_call("gather_ret_w_in", _Gather([blk16("ret_w_in"), small_vec], parts=2))
    sg = sg.reshape(NDEV, 4096)
    o0 = 0
    conv_w_full = sg[:, o0:o0 + 2112].reshape(NDEV, 2, 3, 352).transpose(1, 2, 0, 3).reshape(2, 3, FFN)
    o0 += 2112
    ret_gn_full = sg[:, o0:o0 + 256].reshape(NDEV, RET_H, 64).transpose(1, 0, 2).reshape(RET_H, 1, RET_DV)
    o0 += 256
    mla_norm_full = sg[:, o0:o0 + 128].reshape(1, D)
    o0 += 128
    q_norm_full = sg[:, o0:o0 + 48].reshape(1, MLA_QR)
    o0 += 48
    kv_norm_full = sg[:, o0:o0 + 32].reshape(1, MLA_KVR)

    cw = [conv_w_full[i].reshape(3, 4, FSH).transpose(1, 0, 2) for i in range(2)]
    cb = [ffn_conv_b[i].reshape(4, 1, FSH) for i in range(2)]
    fnorm = [ffn_norm[i].reshape(1, D) for i in range(2)]
    rtabs = _ret_tables(S)
    mtabs = _mla_tables(S)
    qh, kh = mla_q_head_norm.reshape(1, MLA_QK), mla_k_head_norm.reshape(1, MLA_QK)
    gains = (qh[:, :MLA_NOPE], _dup(qh[:, MLA_NOPE:]), kh[:, :MLA_NOPE], _dup(kh[:, MLA_NOPE:]))

    x0 = x.reshape(T, D)
    tgt = loss_target.reshape(T, D)
    proj, h0, (Wret_out, Wffn_out0) = _norm_proj(
        "ret_in", x0, ret_norm.reshape(1, D), Wret_in, _bs((1024, 768), lambda m, j: (m, j)), _sds((T, 6144), BF16),
        comm=_Gather([blk16("ret_w_out"), blk16("ffn_w_out", 0)]))
    Wret_out = Wret_out.reshape(RET_H * RET_DV, D)
    Wffn_out0 = Wffn_out0.reshape(4, FSH, D)
    (o_raw, rgt, states), (Wffn_in0,) = _ret_fwd(proj, rtabs, ret_gn_full, B, S, comm=_Gather([blk16("ffn_w_in", 0)]))
    x1 = _dense("ret_out", rgt, Wret_out, NN, F32, res=x0)
    MLA_W = ["mla_w_in", "mla_w_qb", "mla_w_kvb", "mla_w_out"]
    x2, ffn0_saved, got = _ffn_fwd(0, x1, fnorm[0], Wffn_in0, cw[0], cb[0], Wffn_out0, B, S,
                                   comm_in=_Gather([blk16(k) for k in MLA_W]))
    Wmla_in = got[0].reshape(D, MLA_QR + MLA_KVR + MLA_ROPE)
    Wq, Wkv, Wkr = Wmla_in[:, :MLA_QR], Wmla_in[:, MLA_QR:MLA_QR + MLA_KVR], Wmla_in[:, MLA_QR + MLA_KVR:]
    Wqb, Wkvb, Wmla_out = got[1:]

    def normed(name, xin, gain, wmat):
        n = wmat.shape[1]
        out, hn, _ = _norm_proj(name, xin, gain, wmat[None], _bs((1024, n), lambda m, j: (m, 0)), _sds((T, n), F32))
        return out, hn

    c_all, h2 = normed("mla_in", x2, mla_norm_full, jnp.concatenate([Wq, Wkv, _dup(Wkr)], axis=1))
    c_q, c_kv, k_rope = c_all[:, :MLA_QR], c_all[:, MLA_QR:MLA_QR + MLA_KVR], c_all[:, MLA_QR + MLA_KVR:]
    Wqb2 = jnp.concatenate([Wqb, Wqb[:, :, MLA_NOPE:]], axis=2).transpose(1, 0, 2).reshape(MLA_QR, MLA_H * MLA_PAD)
    Wkvb2 = Wkvb.transpose(1, 0, 2).reshape(MLA_KVR, MLA_H * (MLA_NOPE + MLA_V))
    Wmla_out2 = Wmla_out.reshape(D, D)
    q_raw, cqn = normed("mla_qb", c_q, q_norm_full, Wqb2)
    kvh, ckvn = normed("mla_kvb", c_kv, kv_norm_full, Wkvb2)
    (att, lse, qf, kf), (Wffn_in1, Wffn_out1) = _mla_fwd(
        q_raw, kvh, k_rope, gains, mtabs, B, S, comm=_Gather([blk16("ffn_w_in", 1), blk16("ffn_w_out", 1)]))
    Wffn_out1 = Wffn_out1.reshape(4, FSH, D)
    x3 = _dense("mla_out", att, Wmla_out2, NN, F32, res=x2)
    (dy, colsq, dyb), ffn1_saved, _ = _ffn_fwd(1, x3, fnorm[1], Wffn_in1, cw[1], cb[1], Wffn_out1, B, S, tgt=tgt)
    loss_part = 0.5 * jnp.sum(colsq) / D

    dx3, dx3b, gf1, r_ffn1_out, _, _ = _ffn_bwd(1, dy, dyb, ffn1_saved, fnorm[1], Wffn_in1, cw[1], Wffn_out1, B, S,
                                                first_half_early=False)
    datt = _dense("mla_out_dx", dx3b, Wmla_out2, NT, BF16)
    fh = FSH // 2
    (dq_raw, dkvh, dkr, dqgn, dqgr, dkgn, dkgr), (r_ffn1_in_a, r_ffn1_in_b) = _mla_bwd(
        q_raw, kvh, k_rope, att, lse, datt, qf, kf, gains, mtabs, B, S,
        comm=_Exchange([gf1["w_in"], gf1["w_in"]], rows=[(0, fh), (fh, fh)]))
    dcqn = _dense("mla_qb_dx", dq_raw, Wqb2, NT, F32, tm=1024)
    dckvn = _dense("mla_kvb_dx", dkvh, Wkvb2, NT, F32, tm=1024)
    dcq, dg_qn = _rms_bwd("mla_q_norm_bwd", c_q, q_norm_full, dcqn)
    dckv, dg_kvn = _rms_bwd("mla_kv_norm_bwd", c_kv, kv_norm_full, dckvn)
    dqgr, dkgr = _fold(dqgr), _fold(dkgr)
    dproj2 = _bf16(jnp.concatenate([dcq, dckv, _fold(dkr)], axis=-1))
    dh2 = _dense("mla_in_dx", dproj2, Wmla_in, NT, BF16)
    dx2, dg_mla_norm, dx2b = _rms_bwd("mla_norm_bwd", x2, mla_norm_full, dh2, dres=dx3, also_bf16=True)
    dWmla_out = _dense("mla_out_dw", att, dx3b, TN, BF16, tm=1024).reshape(NDEV, MLA_V, D)
    dWqb = _dense("mla_qb_dw", dq_raw, cqn, TN, BF16, tm=1024).reshape(MLA_H, MLA_PAD, MLA_QR)
    dWqb = jnp.concatenate([dWqb[:, :MLA_NOPE], dWqb[:, MLA_NOPE:MLA_QK] + dWqb[:, MLA_QK:]], axis=1)
    dWkvb = _dense("mla_kvb_dw", ckvn, dkvh, TN, BF16, tm=1024)
    dWkvb = dWkvb.reshape(MLA_KVR, MLA_H, MLA_NOPE + MLA_V).transpose(1, 0, 2)
    dWmla_in = _dense("mla_in_dw", h2, dproj2, TN, BF16).reshape(NDEV, 128, 704)

    dx1, dx1b, gf0, r_ffn0_out, r_ffn0_in_a, (r_mla_a, r_mla_b) = _ffn_bwd(
        0, dx2, dx2b, ffn0_saved, fnorm[0], Wffn_in0, cw[0], Wffn_out0, B, S, first_half_early=True,
        riders=(_Exchange([dWmla_out, dWqb]), _Exchange([dWkvb, dWmla_in])))
    r_mla = [*r_mla_a, *r_mla_b]
    drgt = _dense("ret_out_dx", dx1b, Wret_out, NT, BF16)
    dWret_out = _dense("ret_out_dw", rgt, dx1b, TN, BF16, tm=1024).reshape(NDEV, 256, D)
    (dproj, dgn_ret), (r_ffn0_in_b, r_ret_out) = _ret_bwd(
        proj, o_raw, states, drgt, rtabs, ret_gn_full, B, S,
        comm=_Exchange([gf0["w_in"], dWret_out], rows=[(fh, fh), None]))

    tt, hk, er = min(2048, T), D // 2, D // 8

    def eighths(arr, *which):
        return _Exchange([arr] * len(which), rows=[(w * er, er) for w in which])

    def ret_in_dw(name, half, comm):
        return _mm(name, h0, dproj, grid=(NDEV, T // tt), a_spec=_bs((tt, hk), lambda j, t: (t, half)),
                   b_spec=_bs((tt, 768), lambda j, t: (t, j)), o_spec=_bs((None, hk, 768), lambda j, t: (j, 0, 0)),
                   out_shape=_sds((NDEV, hk, 768), BF16), dims=TN, kax=1, acc_shape=(hk, 768), comm=comm)

    dW_top = ret_in_dw("ret_in_dw_top", 0, None)
    dW_bot, r_e01 = ret_in_dw("ret_in_dw_bot", 1, eighths(dW_top, 0, 1))
    both = _Exchange([dW_top, dW_top, dW_bot, dW_bot, dW_bot], rows=[(w * er, er) for w in (2, 3, 0, 1, 2)])
    dh0, r_e23456 = _mm(
        "ret_in_dx", dproj, Wret_in, grid=(T // 1024, NDEV), a_spec=_bs((1024, 768), lambda m, k: (m, k)),
        b_spec=_bs((None, D, 768), lambda m, k: (k, 0, 0)), o_spec=_bs((1024, D), lambda m, k: (m, 0)),
        out_shape=_sds((T, D), BF16), dims=NT, kax=1, acc_shape=(1024, D), comm=both)
    (dx0, dg_ret_norm), r_e7 = _rms_bwd("ret_norm_bwd", x0, ret_norm.reshape(1, D), dh0, dres=dx1,
                                        comm=eighths(dW_bot, 3))
    grad_x = dx0.reshape(B, S, D)
    received = dict(ret_w_in=[*r_e01, *r_e23456, *r_e7], ret_w_out=[r_ret_out], mla_w_out=[r_mla[0]], mla_w_qb=[r_mla[1]],
                    mla_w_kvb=[r_mla[2]], mla_w_in=[r_mla[3]],
                    ffn_w_in=[r_ffn0_in_a, r_ffn0_in_b, r_ffn1_in_a, r_ffn1_in_b], ffn_w_out=[r_ffn0_out, r_ffn1_out])

    dconv_w = jnp.stack([g_["cw"].transpose(1, 0, 2).reshape(3, FFN) for g_ in (gf0, gf1)])
    dconv_b = jnp.stack([g_["cb"].reshape(FFN) for g_ in (gf0, gf1)])
    small_parts = [dg_ret_norm, gf0["norm"], gf1["norm"], dg_mla_norm, dg_qn, dg_kvn, dqgn, dqgr, dkgn, dkgr, dgn_ret,
                   dconv_w, dconv_b, loss_part]
    small_g = jnp.concatenate([p.reshape(-1) for p in small_parts])
    n_grads = small_g.shape[0] - 1
    small_g = jnp.pad(small_g, (0, 240 * 128 - small_g.shape[0])).reshape(240, 128)
    small_all = _comm_call("gather_small_grads", _Gather([small_g]))[0]
    sred = _sum8("sum_small_grads", small_all).reshape(-1)
    loss = sred[n_grads]

    def take(n):
        nonlocal off
        out = sred[off:off + n]
        off += n
        return out

    off = 0
    g_small = dict(ret_norm=take(D).reshape(1, D), ffn_norm=take(2 * D).reshape(2, D), mla_norm=take(D),
                   mla_q_norm=take(MLA_QR), mla_kv_norm=take(MLA_KVR))
    g_small["mla_q_head_norm"] = take(MLA_QK).reshape(1, MLA_QK)
    g_small["mla_k_head_norm"] = take(MLA_QK).reshape(1, MLA_QK)
    g_small["ret_gn"] = take(RET_H * RET_DV).reshape(1, RET_H, RET_DV)
    g_small["ffn_conv_w"] = take(2 * 3 * FFN).reshape(2, 3, FFN)
    g_small["ffn_conv_b"] = take(2 * FFN).reshape(2, FFN)
    g_small["mla_norm"] = lax.dynamic_slice(g_small["mla_norm"], (dev * 128,), (128,)).reshape(1, 128)
    g_small["mla_q_norm"] = lax.dynamic_slice(g_small["mla_q_norm"], (dev * 48,), (48,)).reshape(1, 48)
    g_small["mla_kv_norm"] = lax.dynamic_slice(g_small["mla_kv_norm"], (dev * 32,), (32,)).reshape(1, 32)
    g_small["ret_gn"] = lax.dynamic_slice(g_small["ret_gn"], (0, 0, dev * 64), (1, RET_H, 64))
    g_small["ffn_conv_w"] = lax.dynamic_slice(g_small["ffn_conv_w"], (0, 0, dev * 352), (2, 3, 352))

    grads, delta, new_m, new_v = {}, {}, {}, {}
    for k in BIG:
        rcs = received[k]
        tr = max(t for t in range(16, 257, 16) if rcs[0].shape[1] % t == 0)
        flip = (lambda t: t.transpose(0, 2, 1)) if k in ("ffn_w_in", "mla_w_qb") else (lambda t: t)
        res = _adamw(f"adamw_{k}", rcs, flip(w[k]), flip(mom[k]), flip(var[k]), tr=tr)
        grads[k], delta[k], new_m[k], new_v[k] = (flip(t) for t in res)
    SMALL = REPL + SHARDED_SMALL

    def pack(d):
        vflat = jnp.concatenate([d[k].reshape(-1) for k in SMALL])
        return jnp.pad(vflat, (0, 96 * 128 - vflat.shape[0])).reshape(1, 96, 128)

    ps = _adamw("adamw_small", [pack(g_small)], pack(w), pack(mom), pack(var))
    off = 0
    for k in SMALL:
        n = w[k].size
        grads[k], delta[k], new_m[k], new_v[k] = (t.reshape(-1)[off:off + n].reshape(w[k].shape) for t in ps)
        off += n
    names = list(w)
    return (loss, grad_x, *[grads[k] for k in names], *[delta[k] for k in names], *[new_m[k] for k in names],
            *[new_v[k] for k in names])
```
